```python
import math
import jax, jax.numpy as jnp
from jax import lax
import numpy as np

D_MODEL = 1024
BATCH = 8
SEQ = 8192
DEPTH = 1

PLE_DIM = 256
CHUNK = 128
EPS = 1e-6
E_A = D_MODEL
G_A = 4
D_INNER = 2 * D_MODEL
HEAD_DIM = 64
N_HEADS = D_INNER // HEAD_DIM
N_STATE = 128
N_GROUPS = 4
CONV_K = 4
CONV_DIM = D_INNER + 2 * N_GROUPS * N_STATE
COL_SIZES = (E_A, E_A, E_A, D_INNER, CONV_DIM, N_HEADS, D_MODEL, D_MODEL)
N_IN = sum(COL_SIZES)

kernel_name = 'hybrid_gmlp_ssd_gated_merge_ple'


def rms_norm(x, g):
    xf = x.astype(jnp.float32)
    y = xf * lax.rsqrt(jnp.mean(xf * xf, axis=-1, keepdims=True) + EPS)
    return (y * g.astype(jnp.float32)).astype(x.dtype)


def layer_norm(x, g, b):
    xf = x.astype(jnp.float32)
    mu = jnp.mean(xf, axis=-1, keepdims=True)
    xc = xf - mu
    y = xc * lax.rsqrt(jnp.mean(xc * xc, axis=-1, keepdims=True) + EPS)
    return (y * g.astype(jnp.float32) + b.astype(jnp.float32)).astype(x.dtype)


def gmlp_branch(u, v, z, ln_g, ln_b, w_s, b_s):
    bsz, s, e = u.shape
    nc = s // CHUNK
    u = jax.nn.gelu(u)
    v = layer_norm(jax.nn.gelu(v), ln_g, ln_b)
    mask = jnp.tril(jnp.ones((CHUNK, CHUNK), dtype=bool))
    ws = jnp.where(mask[None], w_s, jnp.zeros_like(w_s)).astype(v.dtype)
    vc = v.reshape(bsz, nc, CHUNK, G_A, e // G_A)
    sv = jnp.einsum('gts,bcsgd->bctgd', ws, vc) + b_s.T.astype(v.dtype)[None, None, :, :, None]
    return u * sv.reshape(bsz, s, e) * jax.nn.silu(z)


def causal_depthwise_conv(x, w, b):
    c = x.shape[-1]
    y = lax.conv_general_dilated(x, w.astype(x.dtype)[:, None, :], window_strides=(1,),
                                 padding=[(CONV_K - 1, 0)],
                                 dimension_numbers=('NWC', 'WIO', 'NWC'),
                                 feature_group_count=c)
    return y + b.astype(x.dtype)


def ssd_scan(xs, dt, a_log, bm, cm, d_skip):
    bsz, s, h, pdim = xs.shape
    nc = s // CHUNK
    g = N_GROUPS
    r = h // g
    dtype = xs.dtype
    a = -jnp.exp(a_log.astype(jnp.float32)).reshape(g, r)
    X = xs.reshape(bsz, nc, CHUNK, g, r, pdim)
    dtc = dt.reshape(bsz, nc, CHUNK, g, r)
    Xdt = X * dtc[..., None].astype(dtype)
    dA_cs = jnp.cumsum(dtc * a, axis=2)
    Bc = bm.reshape(bsz, nc, CHUNK, g, N_STATE)
    Cc = cm.reshape(bsz, nc, CHUNK, g, N_STATE)
    mask = jnp.tril(jnp.ones((CHUNK, CHUNK), dtype=bool))[None, None, :, :, None, None]
    seg = dA_cs[:, :, :, None] - dA_cs[:, :, None, :]
    Lmat = jnp.exp(jnp.where(mask, seg, -jnp.inf)).astype(dtype)
    CB = jnp.einsum('bclgn,bcsgn->bclsg', Cc, Bc)
    y_diag = jnp.einsum('bclsg,bclsgr,bcsgrp->bclgrp', CB, Lmat, Xdt)
    decay_s = jnp.exp(dA_cs[:, :, -1:] - dA_cs).astype(dtype)
    states = jnp.einsum('bcsgn,bcsgr,bcsgrp->bcgrpn', Bc, decay_s, Xdt)
    chunk_decay = jnp.exp(dA_cs[:, :, -1]).astype(dtype)

    def step(hstate, inp):
        st, dec = inp
        return dec[..., None, None] * hstate + st, hstate

    h0 = jnp.zeros((bsz, g, r, pdim, N_STATE), dtype=states.dtype)
    _, prev = lax.scan(step, h0, (jnp.moveaxis(states, 1, 0), jnp.moveaxis(chunk_decay, 1, 0)))
    prev = jnp.moveaxis(prev, 0, 1)
    y_off = jnp.einsum('bclgn,bcgrpn,bclgr->bclgrp', Cc, prev, jnp.exp(dA_cs).astype(dtype))
    y = y_diag + y_off + X * d_skip.astype(dtype).reshape(g, r)[..., None]
    return y.reshape(bsz, s, h * pdim)


def gated_group_rms_norm(y, z, g):
    yz = y * jax.nn.silu(z)
    bsz, s, e = yz.shape
    yg = yz.reshape(bsz, s, N_GROUPS, e // N_GROUPS).astype(jnp.float32)
    yg = yg * lax.rsqrt(jnp.mean(yg * yg, axis=-1, keepdims=True) + EPS)
    return (yg.reshape(bsz, s, e) * g.astype(jnp.float32)).astype(y.dtype)


def _fwd_setup_inputs(seed: int = 0) -> dict:
    key = jax.random.key(seed)
    ks = jax.random.split(key, 24)
    f32 = jnp.float32
    nrm = lambda k, shape, scale: jax.random.normal(k, shape, f32) * scale
    gain = lambda k, shape: 1.0 + 0.02 * jax.random.normal(k, shape, f32)
    dt0 = jnp.exp(jax.random.uniform(ks[10], (DEPTH, N_HEADS), f32) * (math.log(0.1) - math.log(0.001)) + math.log(0.001))
    return {
        'x': jax.random.normal(ks[0], (BATCH, SEQ, D_MODEL), f32),
        'p': jax.random.normal(ks[1], (DEPTH, BATCH, SEQ, PLE_DIM), f32),
        'norm_g': gain(ks[2], (DEPTH, D_MODEL)),
        'w_in': nrm(ks[3], (DEPTH, D_MODEL, N_IN), D_MODEL ** -0.5),
        'ln_a_g': gain(ks[4], (DEPTH, E_A)),
        'ln_a_b': nrm(ks[5], (DEPTH, E_A), 0.02),
        'w_s': nrm(ks[6], (DEPTH, G_A, CHUNK, CHUNK), CHUNK ** -0.5),
        'b_s': gain(ks[7], (DEPTH, G_A, CHUNK)),
        'conv_w': nrm(ks[8], (DEPTH, CONV_K, CONV_DIM), CONV_K ** -0.5),
        'conv_b': nrm(ks[9], (DEPTH, CONV_DIM), 0.02),
        'dt_bias': dt0 + jnp.log(-jnp.expm1(-dt0)),
        'a_log': jnp.log(jax.random.uniform(ks[11], (DEPTH, N_HEADS), f32, 1.0, 16.0)),
        'd_skip': gain(ks[12], (DEPTH, N_HEADS)),
        'ssm_norm_g': gain(ks[13], (DEPTH, D_INNER)),
        'w_oa': nrm(ks[14], (DEPTH, E_A, D_MODEL), E_A ** -0.5),
        'w_ob': nrm(ks[15], (DEPTH, D_INNER, D_MODEL), D_INNER ** -0.5),
        'w_out': nrm(ks[16], (DEPTH, D_MODEL, D_MODEL), D_MODEL ** -0.5),
        'ple_norm_g': gain(ks[17], (DEPTH, D_MODEL)),
        'w_pg': nrm(ks[18], (DEPTH, D_MODEL, D_MODEL), D_MODEL ** -0.5),
        'w_ple': nrm(ks[19], (DEPTH, PLE_DIM, D_MODEL), PLE_DIM ** -0.5),
        'final_g': gain(ks[20], (D_MODEL,)),
    }


def _fwd_reference(x, p, norm_g, w_in, ln_a_g, ln_a_b, w_s, b_s, conv_w, conv_b, dt_bias, a_log,
              d_skip, ssm_norm_g, w_oa, w_ob, w_out, ple_norm_g, w_pg, w_ple, final_g):
    bsz, s, _ = x.shape
    splits = list(np.cumsum(COL_SIZES)[:-1])
    for i in range(DEPTH):
        h = rms_norm(x, norm_g[i])
        proj = h @ w_in[i].astype(h.dtype)
        u, v, z_a, z_b, xbc, dt_raw, g_a, g_b = jnp.split(proj, splits, axis=-1)
        y_a = gmlp_branch(u, v, z_a, ln_a_g[i], ln_a_b[i], w_s[i], b_s[i])
        o_a = y_a @ w_oa[i].astype(y_a.dtype)
        xbc = jax.nn.silu(causal_depthwise_conv(xbc, conv_w[i], conv_b[i]))
        xs, bm, cm = jnp.split(xbc, [D_INNER, D_INNER + N_GROUPS * N_STATE], axis=-1)
        dt = jax.nn.softplus(dt_raw.astype(jnp.float32) + dt_bias[i].astype(jnp.float32))
        y = ssd_scan(xs.reshape(bsz, s, N_HEADS, HEAD_DIM), dt, a_log[i],
                     bm.reshape(bsz, s, N_GROUPS, N_STATE), cm.reshape(bsz, s, N_GROUPS, N_STATE), d_skip[i])
        y_b = gated_group_rms_norm(y, z_b, ssm_norm_g[i])
        o_b = y_b @ w_ob[i].astype(y_b.dtype)
        merged = jax.nn.sigmoid(g_a) * o_a + jax.nn.sigmoid(g_b) * o_b
        x = x + merged @ w_out[i].astype(merged.dtype)
        hp = rms_norm(x, ple_norm_g[i])
        x = x + jax.nn.sigmoid(hp @ w_pg[i].astype(hp.dtype)) * (p[i] @ w_ple[i].astype(p.dtype))
    return rms_norm(x, final_g)


import jax as _jax
import jax.numpy as _jnp

TWIN_FORMAT = 'train_step'
FWD_PARAMS = ['x', 'p', 'norm_g', 'w_in', 'ln_a_g', 'ln_a_b', 'w_s', 'b_s', 'conv_w', 'conv_b', 'dt_bias', 'a_log', 'd_skip', 'ssm_norm_g', 'w_oa', 'w_ob', 'w_out', 'ple_norm_g', 'w_pg', 'w_ple', 'final_g']
TWIN_WEIGHTS = ['norm_g', 'w_in', 'ln_a_g', 'ln_a_b', 'w_s', 'b_s', 'conv_w', 'conv_b', 'dt_bias', 'a_log', 'd_skip', 'ssm_norm_g', 'w_oa', 'w_ob', 'w_out', 'ple_norm_g', 'w_pg', 'w_ple', 'final_g']
TWIN_DIFF_INPUT = 'x'
TWIN_INPUTS = ['x', 'p', 'norm_g', 'w_in', 'ln_a_g', 'ln_a_b', 'w_s', 'b_s', 'conv_w', 'conv_b', 'dt_bias', 'a_log', 'd_skip', 'ssm_norm_g', 'w_oa', 'w_ob', 'w_out', 'ple_norm_g', 'w_pg', 'w_ple', 'final_g', 'loss_target', 'm_norm_g', 'm_w_in', 'm_ln_a_g', 'm_ln_a_b', 'm_w_s', 'm_b_s', 'm_conv_w', 'm_conv_b', 'm_dt_bias', 'm_a_log', 'm_d_skip', 'm_ssm_norm_g', 'm_w_oa', 'm_w_ob', 'm_w_out', 'm_ple_norm_g', 'm_w_pg', 'm_w_ple', 'm_final_g', 'v_norm_g', 'v_w_in', 'v_ln_a_g', 'v_ln_a_b', 'v_w_s', 'v_b_s', 'v_conv_w', 'v_conv_b', 'v_dt_bias', 'v_a_log', 'v_d_skip', 'v_ssm_norm_g', 'v_w_oa', 'v_w_ob', 'v_w_out', 'v_ple_norm_g', 'v_w_pg', 'v_w_ple', 'v_final_g']
TWIN_OUTPUTS = ['loss', 'grad_x', 'grad_norm_g', 'grad_w_in', 'grad_ln_a_g', 'grad_ln_a_b', 'grad_w_s', 'grad_b_s', 'grad_conv_w', 'grad_conv_b', 'grad_dt_bias', 'grad_a_log', 'grad_d_skip', 'grad_ssm_norm_g', 'grad_w_oa', 'grad_w_ob', 'grad_w_out', 'grad_ple_norm_g', 'grad_w_pg', 'grad_w_ple', 'grad_final_g', 'delta_norm_g', 'delta_w_in', 'delta_ln_a_g', 'delta_ln_a_b', 'delta_w_s', 'delta_b_s', 'delta_conv_w', 'delta_conv_b', 'delta_dt_bias', 'delta_a_log', 'delta_d_skip', 'delta_ssm_norm_g', 'delta_w_oa', 'delta_w_ob', 'delta_w_out', 'delta_ple_norm_g', 'delta_w_pg', 'delta_w_ple', 'delta_final_g', 'new_m_norm_g', 'new_m_w_in', 'new_m_ln_a_g', 'new_m_ln_a_b', 'new_m_w_s', 'new_m_b_s', 'new_m_conv_w', 'new_m_conv_b', 'new_m_dt_bias', 'new_m_a_log', 'new_m_d_skip', 'new_m_ssm_norm_g', 'new_m_w_oa', 'new_m_w_ob', 'new_m_w_out', 'new_m_ple_norm_g', 'new_m_w_pg', 'new_m_w_ple', 'new_m_final_g', 'new_v_norm_g', 'new_v_w_in', 'new_v_ln_a_g', 'new_v_ln_a_b', 'new_v_w_s', 'new_v_b_s', 'new_v_conv_w', 'new_v_conv_b', 'new_v_dt_bias', 'new_v_a_log', 'new_v_d_skip', 'new_v_ssm_norm_g', 'new_v_w_oa', 'new_v_w_ob', 'new_v_w_out', 'new_v_ple_norm_g', 'new_v_w_pg', 'new_v_w_ple', 'new_v_final_g']
TWIN_LEAF_KINDS = {'loss': 'loss', 'grad_x': 'grad_x', 'grad_norm_g': 'grad_w', 'grad_w_in': 'grad_w', 'grad_ln_a_g': 'grad_w', 'grad_ln_a_b': 'grad_w', 'grad_w_s': 'grad_w', 'grad_b_s': 'grad_w', 'grad_conv_w': 'grad_w', 'grad_conv_b': 'grad_w', 'grad_dt_bias': 'grad_w', 'grad_a_log': 'grad_w', 'grad_d_skip': 'grad_w', 'grad_ssm_norm_g': 'grad_w', 'grad_w_oa': 'grad_w', 'grad_w_ob': 'grad_w', 'grad_w_out': 'grad_w', 'grad_ple_norm_g': 'grad_w', 'grad_w_pg': 'grad_w', 'grad_w_ple': 'grad_w', 'grad_final_g': 'grad_w', 'delta_norm_g': 'delta_w', 'delta_w_in': 'delta_w', 'delta_ln_a_g': 'delta_w', 'delta_ln_a_b': 'delta_w', 'delta_w_s': 'delta_w', 'delta_b_s': 'delta_w', 'delta_conv_w': 'delta_w', 'delta_conv_b': 'delta_w', 'delta_dt_bias': 'delta_w', 'delta_a_log': 'delta_w', 'delta_d_skip': 'delta_w', 'delta_ssm_norm_g': 'delta_w', 'delta_w_oa': 'delta_w', 'delta_w_ob': 'delta_w', 'delta_w_out': 'delta_w', 'delta_ple_norm_g': 'delta_w', 'delta_w_pg': 'delta_w', 'delta_w_ple': 'delta_w', 'delta_final_g': 'delta_w', 'new_m_norm_g': 'new_m', 'new_m_w_in': 'new_m', 'new_m_ln_a_g': 'new_m', 'new_m_ln_a_b': 'new_m', 'new_m_w_s': 'new_m', 'new_m_b_s': 'new_m', 'new_m_conv_w': 'new_m', 'new_m_conv_b': 'new_m', 'new_m_dt_bias': 'new_m', 'new_m_a_log': 'new_m', 'new_m_d_skip': 'new_m', 'new_m_ssm_norm_g': 'new_m', 'new_m_w_oa': 'new_m', 'new_m_w_ob': 'new_m', 'new_m_w_out': 'new_m', 'new_m_ple_norm_g': 'new_m', 'new_m_w_pg': 'new_m', 'new_m_w_ple': 'new_m', 'new_m_final_g': 'new_m', 'new_v_norm_g': 'new_v', 'new_v_w_in': 'new_v', 'new_v_ln_a_g': 'new_v', 'new_v_ln_a_b': 'new_v', 'new_v_w_s': 'new_v', 'new_v_b_s': 'new_v', 'new_v_conv_w': 'new_v', 'new_v_conv_b': 'new_v', 'new_v_dt_bias': 'new_v', 'new_v_a_log': 'new_v', 'new_v_d_skip': 'new_v', 'new_v_ssm_norm_g': 'new_v', 'new_v_w_oa': 'new_v', 'new_v_w_ob': 'new_v', 'new_v_w_out': 'new_v', 'new_v_ple_norm_g': 'new_v', 'new_v_w_pg': 'new_v', 'new_v_w_ple': 'new_v', 'new_v_final_g': 'new_v'}


def _forward(args):
    return _fwd_reference(*[args[k] for k in FWD_PARAMS])


def _output_shape():
    out = _jax.eval_shape(lambda: _forward(_fwd_setup_inputs(0)))
    return out.shape, out.dtype

N_MICROBATCH = 1
ADAM_LR = 0.001
ADAM_B1 = 0.9
ADAM_B2 = 0.999
ADAM_EPS = 1e-08
ADAM_WD = 0.01
ADAM_STEP = 10
PER_EXAMPLE_BATCH_AXIS = {'x': 0, 'p': 1, 'loss_target': 0}
SHARED_INPUTS = []
_WEIGHT_DTYPES = {'norm_g': _jnp.float32, 'w_in': _jnp.float32, 'ln_a_g': _jnp.float32, 'ln_a_b': _jnp.float32, 'w_s': _jnp.float32, 'b_s': _jnp.float32, 'conv_w': _jnp.float32, 'conv_b': _jnp.float32, 'dt_bias': _jnp.float32, 'a_log': _jnp.float32, 'd_skip': _jnp.float32, 'ssm_norm_g': _jnp.float32, 'w_oa': _jnp.float32, 'w_ob': _jnp.float32, 'w_out': _jnp.float32, 'ple_norm_g': _jnp.float32, 'w_pg': _jnp.float32, 'w_ple': _jnp.float32, 'final_g': _jnp.float32}
MOMENT_SCALE = {'norm_g': 1.810705e-01, 'w_in': 5.914222e-02, 'ln_a_g': 3.157500e-02, 'ln_a_b': 3.208135e-02, 'w_s': 4.175991e-02, 'b_s': 6.320794e-02, 'conv_w': 6.757652e-02, 'conv_b': 9.764355e-02, 'dt_bias': 1.260180e-01, 'a_log': 3.857808e-01, 'd_skip': 1.210003e+00, 'ssm_norm_g': 8.226517e-02, 'w_oa': 5.113537e-02, 'w_ob': 1.074961e-01, 'w_out': 1.191621e-01, 'ple_norm_g': 4.414541e-02, 'w_pg': 4.163147e-02, 'w_ple': 1.061464e-01, 'final_g': 6.395058e+01}


def _to_microbatches(a, axis):
    t = _jnp.moveaxis(a, axis, 0)
    t = t.reshape((N_MICROBATCH, t.shape[0] // N_MICROBATCH) + t.shape[1:])
    return _jnp.moveaxis(t, 1, axis + 1)


def setup_inputs(seed: int = 0) -> dict:
    inp = _fwd_setup_inputs(seed)
    key = _jax.random.fold_in(_jax.random.key(seed), 7919)
    shape, _ = _output_shape()
    out = dict(inp)
    out["loss_target"] = _jax.random.normal(_jax.random.fold_in(key, 0), shape, _jnp.float32)
    for i, name in enumerate(TWIN_WEIGHTS):
        w = inp[name].astype(_jnp.float32)
        if MOMENT_SCALE is None:
            s = _jnp.sqrt(_jnp.mean(_jnp.square(w)) + 1e-30)
        else:
            s = MOMENT_SCALE[name]
        km, kv = _jax.random.split(_jax.random.fold_in(key, i + 1))
        out[name] = w
        out["m_" + name] = s * _jax.random.normal(km, w.shape, _jnp.float32)
        out["v_" + name] = (s * s) * _jax.random.uniform(kv, w.shape, _jnp.float32, 0.5, 1.5)
    if N_MICROBATCH > 1:
        for name, axis in PER_EXAMPLE_BATCH_AXIS.items():
            out[name] = _to_microbatches(out[name], axis)
    return {'x': out['x'], 'p': out['p'], 'norm_g': out['norm_g'], 'w_in': out['w_in'], 'ln_a_g': out['ln_a_g'], 'ln_a_b': out['ln_a_b'], 'w_s': out['w_s'], 'b_s': out['b_s'], 'conv_w': out['conv_w'], 'conv_b': out['conv_b'], 'dt_bias': out['dt_bias'], 'a_log': out['a_log'], 'd_skip': out['d_skip'], 'ssm_norm_g': out['ssm_norm_g'], 'w_oa': out['w_oa'], 'w_ob': out['w_ob'], 'w_out': out['w_out'], 'ple_norm_g': out['ple_norm_g'], 'w_pg': out['w_pg'], 'w_ple': out['w_ple'], 'final_g': out['final_g'], 'loss_target': out['loss_target'], 'm_norm_g': out['m_norm_g'], 'm_w_in': out['m_w_in'], 'm_ln_a_g': out['m_ln_a_g'], 'm_ln_a_b': out['m_ln_a_b'], 'm_w_s': out['m_w_s'], 'm_b_s': out['m_b_s'], 'm_conv_w': out['m_conv_w'], 'm_conv_b': out['m_conv_b'], 'm_dt_bias': out['m_dt_bias'], 'm_a_log': out['m_a_log'], 'm_d_skip': out['m_d_skip'], 'm_ssm_norm_g': out['m_ssm_norm_g'], 'm_w_oa': out['m_w_oa'], 'm_w_ob': out['m_w_ob'], 'm_w_out': out['m_w_out'], 'm_ple_norm_g': out['m_ple_norm_g'], 'm_w_pg': out['m_w_pg'], 'm_w_ple': out['m_w_ple'], 'm_final_g': out['m_final_g'], 'v_norm_g': out['v_norm_g'], 'v_w_in': out['v_w_in'], 'v_ln_a_g': out['v_ln_a_g'], 'v_ln_a_b': out['v_ln_a_b'], 'v_w_s': out['v_w_s'], 'v_b_s': out['v_b_s'], 'v_conv_w': out['v_conv_w'], 'v_conv_b': out['v_conv_b'], 'v_dt_bias': out['v_dt_bias'], 'v_a_log': out['v_a_log'], 'v_d_skip': out['v_d_skip'], 'v_ssm_norm_g': out['v_ssm_norm_g'], 'v_w_oa': out['v_w_oa'], 'v_w_ob': out['v_w_ob'], 'v_w_out': out['v_w_out'], 'v_ple_norm_g': out['v_ple_norm_g'], 'v_w_pg': out['v_w_pg'], 'v_w_ple': out['v_w_ple'], 'v_final_g': out['v_final_g']}


def _loss(weights, diff, rest, loss_target):
    with _jax.named_scope("forward"):
        args = {**rest, TWIN_DIFF_INPUT: diff, **{k: w.astype(_WEIGHT_DTYPES[k]) for k, w in weights.items()}}
        y = _forward(args)
    with _jax.named_scope("loss_head"):
        err = _jnp.square(y.astype(_jnp.float32) - loss_target)
        return 0.5 * _jnp.sum(_jnp.mean(err, axis=-1)) if err.ndim else 0.5 * err


def _adamw(w, g, m, v):
    m = ADAM_B1 * m + (1.0 - ADAM_B1) * g
    v = ADAM_B2 * v + (1.0 - ADAM_B2) * _jnp.square(g)
    m_hat = m / (1.0 - ADAM_B1 ** ADAM_STEP)
    v_hat = v / (1.0 - ADAM_B2 ** ADAM_STEP)
    delta = -ADAM_LR * (m_hat / (_jnp.sqrt(v_hat) + ADAM_EPS) + ADAM_WD * w)
    return delta, m, v


def reference(x, p, norm_g, w_in, ln_a_g, ln_a_b, w_s, b_s, conv_w, conv_b, dt_bias, a_log, d_skip, ssm_norm_g, w_oa, w_ob, w_out, ple_norm_g, w_pg, w_ple, final_g, loss_target, m_norm_g, m_w_in, m_ln_a_g, m_ln_a_b, m_w_s, m_b_s, m_conv_w, m_conv_b, m_dt_bias, m_a_log, m_d_skip, m_ssm_norm_g, m_w_oa, m_w_ob, m_w_out, m_ple_norm_g, m_w_pg, m_w_ple, m_final_g, v_norm_g, v_w_in, v_ln_a_g, v_ln_a_b, v_w_s, v_b_s, v_conv_w, v_conv_b, v_dt_bias, v_a_log, v_d_skip, v_ssm_norm_g, v_w_oa, v_w_ob, v_w_out, v_ple_norm_g, v_w_pg, v_w_ple, v_final_g):
    given = dict(x=x, p=p, norm_g=norm_g, w_in=w_in, ln_a_g=ln_a_g, ln_a_b=ln_a_b, w_s=w_s, b_s=b_s, conv_w=conv_w, conv_b=conv_b, dt_bias=dt_bias, a_log=a_log, d_skip=d_skip, ssm_norm_g=ssm_norm_g, w_oa=w_oa, w_ob=w_ob, w_out=w_out, ple_norm_g=ple_norm_g, w_pg=w_pg, w_ple=w_ple, final_g=final_g, loss_target=loss_target, m_norm_g=m_norm_g, m_w_in=m_w_in, m_ln_a_g=m_ln_a_g, m_ln_a_b=m_ln_a_b, m_w_s=m_w_s, m_b_s=m_b_s, m_conv_w=m_conv_w, m_conv_b=m_conv_b, m_dt_bias=m_dt_bias, m_a_log=m_a_log, m_d_skip=m_d_skip, m_ssm_norm_g=m_ssm_norm_g, m_w_oa=m_w_oa, m_w_ob=m_w_ob, m_w_out=m_w_out, m_ple_norm_g=m_ple_norm_g, m_w_pg=m_w_pg, m_w_ple=m_w_ple, m_final_g=m_final_g, v_norm_g=v_norm_g, v_w_in=v_w_in, v_ln_a_g=v_ln_a_g, v_ln_a_b=v_ln_a_b, v_w_s=v_w_s, v_b_s=v_b_s, v_conv_w=v_conv_w, v_conv_b=v_conv_b, v_dt_bias=v_dt_bias, v_a_log=v_a_log, v_d_skip=v_d_skip, v_ssm_norm_g=v_ssm_norm_g, v_w_oa=v_w_oa, v_w_ob=v_w_ob, v_w_out=v_w_out, v_ple_norm_g=v_ple_norm_g, v_w_pg=v_w_pg, v_w_ple=v_w_ple, v_final_g=v_final_g)
    weights = {n: given[n] for n in TWIN_WEIGHTS}
    shared = {n: given[n] for n in SHARED_INPUTS}
    per_example = {n: given[n] for n in ['x', 'p']}
    grad_fn = _jax.value_and_grad(_loss, argnums=(0, 1))

    def one_microbatch(ex, loss_target):
        ex = dict(ex)
        diff = ex.pop(TWIN_DIFF_INPUT)
        return grad_fn(weights, diff, {**shared, **ex}, loss_target)

    if N_MICROBATCH == 1:
        loss, (grad_w, grad_x) = one_microbatch(per_example, given["loss_target"])
    else:
        def body(carry, xs):
            loss_sum, grad_sum = carry
            l_k, (gw_k, gx_k) = one_microbatch(xs[0], xs[1])
            with _jax.named_scope("update"):
                return (loss_sum + l_k, _jax.tree.map(_jnp.add, grad_sum, gw_k)), gx_k

        init = (_jnp.zeros((), _jnp.float32), _jax.tree.map(_jnp.zeros_like, weights))
        (loss, grad_w), grad_x = _jax.lax.scan(body, init, (per_example, given["loss_target"]))
    with _jax.named_scope("update"):
        delta_w, new_m, new_v = {}, {}, {}
        for n in TWIN_WEIGHTS:
            delta_w[n], new_m[n], new_v[n] = _adamw(weights[n], grad_w[n], given["m_" + n], given["v_" + n])
    return (loss, grad_x, *[grad_w[n] for n in TWIN_WEIGHTS], *[delta_w[n] for n in TWIN_WEIGHTS],
            *[new_m[n] for n in TWIN_WEIGHTS], *[new_v[n] for n in TWIN_WEIGHTS])
```

```python
import functools

import jax
import jax.numpy as jnp
from jax import lax
from jax.experimental import pallas as pl
from jax.experimental.pallas import tpu as pltpu

F32 = jnp.float32
MXU_DTYPE = jnp.bfloat16
ACT_DTYPE = jnp.bfloat16
WIRE_DTYPE = jnp.bfloat16

D_MODEL = 1024
PLE_DIM = 256
CHUNK = 128
EPS = 1e-6
E_A = D_MODEL
G_A = 4
D_INNER = 2 * D_MODEL
HEAD_DIM = 64
N_HEADS = D_INNER // HEAD_DIM
N_STATE = 128
N_GROUPS = 4
HEADS_PER_GROUP = N_HEADS // N_GROUPS
PAIRS_PER_GROUP = HEADS_PER_GROUP // 2
CONV_K = 4
CONV_DIM = D_INNER + 2 * N_GROUPS * N_STATE
N_IN = 3 * E_A + D_INNER + CONV_DIM + N_HEADS + 2 * D_MODEL
N_CHIPS = 4
N_DEV = 8
W_IN_BLOCK = N_IN // N_CHIPS

UVZ_W, XBC_W, ZB_W, G_W = 3 * E_A, CONV_DIM, D_INNER, 2 * D_MODEL
MAIN_W = UVZ_W + XBC_W + ZB_W + G_W
UVZ_CB, XBC_CB, ZB_CB, G_CB = 0, 1, 3, 4
DT_W = N_GROUPS * 128

ADAM_LR, ADAM_B1, ADAM_B2, ADAM_EPS, ADAM_WD, ADAM_STEP = 0.001, 0.9, 0.999, 1e-08, 0.01, 10

MESH = pl.DeviceIdType.MESH
ANY = pl.BlockSpec(memory_space=pl.ANY)


def _mxu(v):
    return v.astype(MXU_DTYPE)


def _dot(a, b, dims=(((1,), (0,)), ((), ()))):
    return lax.dot_general(_mxu(a), _mxu(b), dims, preferred_element_type=F32)


def _matmul(a, b, *, mode, name, out_dtype, m, n, k, tm=1024, tn=1024, tk=1024, a_off=0, b_off=0,
            extras=(), epilogue=None):
    tm, tn, tk = min(tm, m), min(tn, n), min(tk, k)
    assert m % tm == 0 and n % tn == 0 and k % tk == 0, (name, m, n, k, tm, tn, tk)
    nk = k // tk
    if mode == "nn":
        assert a_off % tk == 0 and b_off % tn == 0
        a_spec = pl.BlockSpec((tm, tk), lambda i, j, kk: (i, kk + a_off // tk))
        b_spec = pl.BlockSpec((tk, tn), lambda i, j, kk: (kk, j + b_off // tn))
        dims = (((1,), (0,)), ((), ()))
    elif mode == "nt":
        a_spec = pl.BlockSpec((tm, tk), lambda i, j, kk: (i, kk))
        b_spec = pl.BlockSpec((tn, tk), lambda i, j, kk: (j, kk))
        dims = (((1,), (1,)), ((), ()))
    else:
        assert a_off % tm == 0 and b_off % tn == 0
        a_spec = pl.BlockSpec((tk, tm), lambda i, j, kk: (kk, i + a_off // tm))
        b_spec = pl.BlockSpec((tk, tn), lambda i, j, kk: (kk, j + b_off // tn))
        dims = (((0,), (0,)), ((), ()))
    ne = len(extras)

    def finish(acc, extra_refs, o_ref):
        res = acc if epilogue is None else epilogue(acc, *[e[...] for e in extra_refs])
        o_ref[...] = res.astype(o_ref.dtype)

    def body(a_ref, b_ref, *rest):
        extra_refs, o_ref = rest[:ne], rest[ne]
        part = _dot(a_ref[...], b_ref[...], dims)
        if nk == 1:
            finish(part, extra_refs, o_ref)
            return
        acc_ref = rest[ne + 1]
        kk = pl.program_id(2)

        @pl.when(kk == 0)
        def _():
            acc_ref[...] = part

        @pl.when(kk > 0)
        def _():
            acc_ref[...] += part

        @pl.when(kk == nk - 1)
        def _():
            finish(acc_ref[...], extra_refs, o_ref)

    o_spec = pl.BlockSpec((tm, tn), lambda i, j, kk: (i, j))
    return pl.pallas_call(
        body, name=name, grid=(m // tm, n // tn, nk),
        in_specs=[a_spec, b_spec] + [o_spec] * ne, out_specs=o_spec,
        out_shape=jax.ShapeDtypeStruct((m, n), out_dtype),
        scratch_shapes=[pltpu.VMEM((tm, tn), F32)] if nk > 1 else [],
        compiler_params=pltpu.CompilerParams(dimension_semantics=("parallel", "parallel", "arbitrary")),
    )(a, b, *extras)


def _row_spec(tm, width, cb):
    return pl.BlockSpec((tm, width), lambda i: (i, cb))


def _whole_spec(shape):
    nd = len(shape)
    return pl.BlockSpec(tuple(shape), lambda i: (0,) * nd)


def _rows_call(name, f, rows, pars, outs, *, tm, nrows):
    tm = min(tm, nrows)
    nr, npar = len(rows), len(pars)

    def body(*refs):
        rv = [r[...].astype(F32) for r in refs[:nr]]
        pv = [p[...] for p in refs[nr:nr + npar]]
        res = f(*rv, *pv)
        for o_ref, r in zip(refs[nr + npar:], res):
            o_ref[...] = r.astype(o_ref.dtype)

    return pl.pallas_call(
        body, name=name, grid=(nrows // tm,),
        in_specs=[_row_spec(tm, w, cb) for _, w, cb in rows] + [_whole_spec(p.shape) for p in pars],
        out_specs=[_row_spec(tm, w, 0) for w, _ in outs],
        out_shape=[jax.ShapeDtypeStruct((nrows, w), dt) for w, dt in outs],
        compiler_params=pltpu.CompilerParams(dimension_semantics=("parallel",)),
    )(*[r[0] for r in rows], *pars)


def _rows_vjp_call(name, f, rows, pars, cots, drows, *, tm, nrows):
    tm = min(tm, nrows)
    nr, npar, nc = len(rows), len(pars), len(cots)
    alias_bufs, aliases = [], {}
    out_shape, out_specs = [], []
    for (arr, w, cb), d in zip(rows, drows):
        if d is None:
            continue
        dt, into = d
        if into is None:
            out_shape.append(jax.ShapeDtypeStruct((nrows, w), dt))
            out_specs.append(_row_spec(tm, w, 0))
        else:
            buf, total, ocb = into
            if buf is not None:
                aliases[nr + npar + nc + len(alias_bufs)] = len(out_shape)
                alias_bufs.append(buf)
            out_shape.append(jax.ShapeDtypeStruct((nrows, total), dt))
            out_specs.append(_row_spec(tm, w, ocb))
    n_drow = len(out_shape)
    for p in pars:
        out_shape.append(jax.ShapeDtypeStruct(p.shape, F32))
        out_specs.append(_whole_spec(p.shape))
    na = len(alias_bufs)

    def body(*refs):
        rv = [r[...].astype(F32) for r in refs[:nr]]
        pv = [p[...] for p in refs[nr:nr + npar]]
        cv = tuple(c[...].astype(F32) for c in refs[nr + npar:nr + npar + nc])
        o_refs = refs[nr + npar + nc + na:]
        _, vjp = jax.vjp(f, *rv, *pv)
        g = vjp(cv)
        oi = 0
        for ri, d in enumerate(drows):
            if d is not None:
                o_refs[oi][...] = g[ri].astype(o_refs[oi].dtype)
                oi += 1
        first = pl.program_id(0) == 0
        for pi in range(npar):
            acc = o_refs[n_drow + pi]

            @pl.when(first)
            def _(acc=acc):
                acc[...] = jnp.zeros_like(acc)

            acc[...] += g[nr + pi]

    return pl.pallas_call(
        body, name=name, grid=(nrows // tm,),
        in_specs=[_row_spec(tm, w, cb) for _, w, cb in rows] + [_whole_spec(p.shape) for p in pars]
        + [_row_spec(tm, w, cb) for _, w, cb in cots] + [ANY] * na,
        out_specs=out_specs, out_shape=out_shape, input_output_aliases=aliases,
        compiler_params=pltpu.CompilerParams(dimension_semantics=("arbitrary",)),
    )(*[r[0] for r in rows], *pars, *[c[0] for c in cots], *alias_bufs)


def _rms(x, g):
    return x * lax.rsqrt(jnp.mean(x * x, axis=-1, keepdims=True) + EPS) * g


def _f_rms(x, g):
    return (_rms(x, g),)


def _tril_mask():
    return lax.broadcasted_iota(jnp.int32, (CHUNK, CHUNK), 0) >= lax.broadcasted_iota(jnp.int32, (CHUNK, CHUNK), 1)


def _f_branch_a(uvz, ln_g, ln_b, w_s, b_s):
    u = jax.nn.gelu(uvz[:, :E_A])
    v = jax.nn.gelu(uvz[:, E_A:2 * E_A])
    z = uvz[:, 2 * E_A:]
    xc = v - jnp.mean(v, axis=-1, keepdims=True)
    vn = xc * lax.rsqrt(jnp.mean(xc * xc, axis=-1, keepdims=True) + EPS) * ln_g + ln_b
    mask = _tril_mask()
    ws = [jnp.where(mask, w_s[g], 0.0) for g in range(G_A)]
    gw = E_A // G_A
    rows = []
    for c in range(uvz.shape[0] // CHUNK):
        vc = vn[c * CHUNK:(c + 1) * CHUNK]
        rows.append(jnp.concatenate([_dot(ws[g], vc[:, g * gw:(g + 1) * gw]) + b_s[g] for g in range(G_A)], axis=1))
    sv = rows[0] if len(rows) == 1 else jnp.concatenate(rows, axis=0)
    return (u * sv * jax.nn.silu(z),)


def _f_gnorm(y, zb, g):
    yz = y * jax.nn.silu(zb)
    gw = D_INNER // N_GROUPS
    parts = []
    for i in range(N_GROUPS):
        s = yz[:, i * gw:(i + 1) * gw]
        parts.append(s * lax.rsqrt(jnp.mean(s * s, axis=-1, keepdims=True) + EPS))
    return (jnp.concatenate(parts, axis=1) * g,)


def _f_merge(g2, oa, ob):
    return (jax.nn.sigmoid(g2[:, :D_MODEL]) * oa + jax.nn.sigmoid(g2[:, D_MODEL:]) * ob,)


def _f_loss(x1, gp, pe, tgt, fg):
    x2 = x1 + jax.nn.sigmoid(gp) * pe
    err = _rms(x2, fg) - tgt
    return 0.5 * jnp.sum(jnp.mean(err * err, axis=-1))


def _head(x1, gp, pe, tgt, fg, *, tm, nrows):
    tm = min(tm, nrows)

    def body(x1_ref, gp_ref, pe_ref, t_ref, fg_ref, dx_ref, dgp_ref, dpe_ref, dfg_ref, loss_ref):
        loss, vjp = jax.vjp(_f_loss, x1_ref[...], gp_ref[...], pe_ref[...], t_ref[...], fg_ref[...])
        dx, dgp, dpe, _, dfg = vjp(jnp.ones((), F32))
        dx_ref[...] = dx
        dgp_ref[...] = dgp.astype(dgp_ref.dtype)
        dpe_ref[...] = dpe.astype(dpe_ref.dtype)

        @pl.when(pl.program_id(0) == 0)
        def _():
            dfg_ref[...] = jnp.zeros_like(dfg_ref)
            loss_ref[...] = jnp.zeros_like(loss_ref)

        dfg_ref[...] += dfg
        loss_ref[...] += jnp.full(loss_ref.shape, loss, F32)

    row = _row_spec(tm, D_MODEL, 0)
    return pl.pallas_call(
        body, name="head", grid=(nrows // tm,),
        in_specs=[row, row, row, row, _whole_spec((1, D_MODEL))],
        out_specs=[row, row, row, _whole_spec((1, D_MODEL)), _whole_spec((1, 128))],
        out_shape=[jax.ShapeDtypeStruct((nrows, D_MODEL), F32), jax.ShapeDtypeStruct((nrows, D_MODEL), ACT_DTYPE),
                   jax.ShapeDtypeStruct((nrows, D_MODEL), ACT_DTYPE), jax.ShapeDtypeStruct((1, D_MODEL), F32),
                   jax.ShapeDtypeStruct((1, 128), F32)],
        compiler_params=pltpu.CompilerParams(dimension_semantics=("arbitrary",)),
    )(x1, gp, pe, tgt, fg)


def _shift_rows(cur, edge, j, up):
    tm = cur.shape[0]
    row = lax.broadcasted_iota(jnp.int32, cur.shape, 0)
    if up:
        sh = pltpu.roll(cur, tm - j, 0)
        e = jnp.tile(pltpu.roll(edge, 8 - j, 0), (tm // 8, 1))
        return jnp.where(row >= tm - j, e, sh)
    sh = pltpu.roll(cur, j, 0)
    e = jnp.tile(pltpu.roll(edge, j, 0), (tm // 8, 1))
    return jnp.where(row < j, e, sh)


def _conv_pre(cur, prev, w, b):
    acc = cur * w[CONV_K - 1:CONV_K] + b
    taps = [cur]
    for j in range(1, CONV_K):
        s = _shift_rows(cur, prev, j, up=False)
        taps.append(s)
        acc = acc + s * w[CONV_K - 1 - j:CONV_K - j]
    return acc, taps


def _halo_specs(tm, nrows, cb, before):
    nb = tm // 8
    last = nrows // 8 - 1
    if before:
        return pl.BlockSpec((8, XBC_W), lambda i: (jnp.maximum(i * nb - 1, 0), cb))
    return pl.BlockSpec((8, XBC_W), lambda i: (jnp.minimum((i + 1) * nb, last), cb))


def _conv_fwd(proj, conv_w, conv_b, *, tm, nrows):
    tm = min(tm, nrows)

    def body(cur_ref, prev_ref, w_ref, b_ref, o_ref):
        prev = jnp.where(pl.program_id(0) == 0, 0.0, prev_ref[...].astype(F32))
        pre, _ = _conv_pre(cur_ref[...].astype(F32), prev, w_ref[...], b_ref[...])
        o_ref[...] = jax.nn.silu(pre).astype(o_ref.dtype)

    return pl.pallas_call(
        body, name="conv_fwd", grid=(nrows // tm,),
        in_specs=[_row_spec(tm, XBC_W, XBC_CB), _halo_specs(tm, nrows, XBC_CB, True),
                  _whole_spec((CONV_K, XBC_W)), _whole_spec((1, XBC_W))],
        out_specs=_row_spec(tm, XBC_W, 0), out_shape=jax.ShapeDtypeStruct((nrows, XBC_W), ACT_DTYPE),
        compiler_params=pltpu.CompilerParams(dimension_semantics=("parallel",)),
    )(proj, proj, conv_w, conv_b)


def _conv_bwd_pre(proj, conv_w, conv_b, dact, *, tm, nrows):
    tm = min(tm, nrows)
    nb = N_GROUPS * N_STATE

    def body(cur_ref, prev_ref, w_ref, b_ref, dxs_ref, dbm_ref, dcm_ref, dpre_ref, dw_ref, db_ref):
        prev = jnp.where(pl.program_id(0) == 0, 0.0, prev_ref[...].astype(F32))
        pre, taps = _conv_pre(cur_ref[...].astype(F32), prev, w_ref[...], b_ref[...])
        sg = jax.nn.sigmoid(pre)
        dy = jnp.concatenate([dxs_ref[...], dbm_ref[...], dcm_ref[...]], axis=1).astype(F32)
        dpre = dy * sg * (1.0 + pre * (1.0 - sg))
        dpre_ref[...] = dpre.astype(dpre_ref.dtype)

        @pl.when(pl.program_id(0) == 0)
        def _():
            dw_ref[...] = jnp.zeros_like(dw_ref)
            db_ref[...] = jnp.zeros_like(db_ref)

        db_ref[...] += jnp.sum(dpre, axis=0, keepdims=True)
        for j in range(CONV_K):
            k = CONV_K - 1 - j
            dw_ref[k:k + 1, :] += jnp.sum(dpre * taps[j], axis=0, keepdims=True)

    return pl.pallas_call(
        body, name="conv_bwd_pre", grid=(nrows // tm,),
        in_specs=[_row_spec(tm, XBC_W, XBC_CB), _halo_specs(tm, nrows, XBC_CB, True),
                  _whole_spec((CONV_K, XBC_W)), _whole_spec((1, XBC_W)),
                  _row_spec(tm, D_INNER, 0), _row_spec(tm, nb, 0), _row_spec(tm, nb, 0)],
        out_specs=[_row_spec(tm, XBC_W, 0), _whole_spec((CONV_K, XBC_W)), _whole_spec((1, XBC_W))],
        out_shape=[jax.ShapeDtypeStruct((nrows, XBC_W), ACT_DTYPE), jax.ShapeDtypeStruct((CONV_K, XBC_W), F32),
                   jax.ShapeDtypeStruct((1, XBC_W), F32)],
        compiler_params=pltpu.CompilerParams(dimension_semantics=("arbitrary",)),
    )(proj, proj, conv_w, conv_b, *dact)


def _conv_bwd_x(dpre, conv_w, dproj, *, tm, nrows):
    tm = min(tm, nrows)
    ntiles = nrows // tm

    def body(cur_ref, nxt_ref, w_ref, _, o_ref):
        cur = cur_ref[...].astype(F32)
        nxt = jnp.where(pl.program_id(0) == ntiles - 1, 0.0, nxt_ref[...].astype(F32))
        w = w_ref[...]
        acc = cur * w[CONV_K - 1:CONV_K]
        for j in range(1, CONV_K):
            acc = acc + _shift_rows(cur, nxt, j, up=True) * w[CONV_K - 1 - j:CONV_K - j]
        o_ref[...] = acc.astype(o_ref.dtype)

    return pl.pallas_call(
        body, name="conv_bwd_x", grid=(ntiles,),
        in_specs=[_row_spec(tm, XBC_W, 0), _halo_specs(tm, nrows, 0, False), _whole_spec((CONV_K, XBC_W)), ANY],
        out_specs=_row_spec(tm, XBC_W, XBC_CB), out_shape=jax.ShapeDtypeStruct(dproj.shape, dproj.dtype),
        input_output_aliases={3: 0},
        compiler_params=pltpu.CompilerParams(dimension_semantics=("parallel",)),
    )(dpre, dpre, conv_w, dproj)


def _ssd_chunk(xs, bm, cm, dtr, hprev, dtb, alog, dsk):
    hp = lax.Precision.HIGHEST
    li = lax.broadcasted_iota(jnp.int32, (CHUNK, CHUNK), 0)
    si = lax.broadcasted_iota(jnp.int32, (CHUNK, CHUNK), 1)
    causal = li >= si
    tri = causal.astype(F32)
    lo = si < HEAD_DIM
    top = li < HEAD_DIM
    dt = jax.nn.softplus(dtr + dtb)
    da = dt * (-jnp.exp(alog))
    cs = jnp.dot(tri, da, precision=hp, preferred_element_type=F32)
    cst = lax.dot_general(da, tri, (((0,), (1,)), ((), ())), precision=hp, preferred_element_type=F32)
    ecs = jnp.exp(cs)
    last = cs[CHUNK - 1:CHUNK, :]
    dec = jnp.exp(last - cs)
    cdec = jnp.exp(last)
    cb = _dot(cm, bm, (((1,), (1,)), ((), ())))
    ys, hs = [], []
    for q in range(PAIRS_PER_GROUP):
        h0, h1 = 2 * q, 2 * q + 1
        x = xs[:, 128 * q:128 * (q + 1)]
        xdt = x * jnp.where(lo, dt[:, h0:h0 + 1], dt[:, h1:h1 + 1])
        yd = []
        for h in (h0, h1):
            lmat = jnp.exp(jnp.where(causal, cs[:, h:h + 1] - cst[h:h + 1, :], -jnp.inf))
            yd.append(_dot(cb * lmat, xdt))
        y_diag = jnp.where(lo, yd[0], yd[1])
        y_off = jnp.where(lo, ecs[:, h0:h0 + 1], ecs[:, h1:h1 + 1]) * _dot(cm, hprev[q], (((1,), (1,)), ((), ())))
        xdec = xdt * jnp.where(lo, dec[:, h0:h0 + 1], dec[:, h1:h1 + 1])
        st = _dot(xdec, bm, (((0,), (0,)), ((), ())))
        hs.append(jnp.where(top, cdec[:, h0:h0 + 1], cdec[:, h1:h1 + 1]) * hprev[q] + st)
        ys.append(y_diag + y_off + x * jnp.where(lo, dsk[:, h0:h0 + 1], dsk[:, h1:h1 + 1]))
    return jnp.concatenate(ys, axis=1), jnp.stack(hs)


_XS_GW = D_INNER // N_GROUPS


def _ssd_specs(rev, nchunks):
    def c_of(c):
        return nchunks - 1 - c if rev else c

    xs = pl.BlockSpec((CHUNK, _XS_GW), lambda g, c: (c_of(c), g))
    bm = pl.BlockSpec((CHUNK, N_STATE), lambda g, c: (c_of(c), D_INNER // N_STATE + g))
    cm = pl.BlockSpec((CHUNK, N_STATE), lambda g, c: (c_of(c), D_INNER // N_STATE + N_GROUPS + g))
    dt = pl.BlockSpec((CHUNK, 128), lambda g, c: (c_of(c), g))
    par = pl.BlockSpec((1, 128), lambda g, c: (0, g))
    st = pl.BlockSpec((None, None, PAIRS_PER_GROUP, 128, N_STATE), lambda g, c: (g, c_of(c), 0, 0, 0))
    return xs, bm, cm, dt, par, st


def _ssd_fwd(act, dtr, dtb, alog, dsk, *, nrows):
    nchunks = nrows // CHUNK
    xs, bm, cm, dt, par, st = _ssd_specs(False, nchunks)

    def body(xs_ref, b_ref, c_ref, dt_ref, dtb_ref, al_ref, dk_ref, y_ref, st_ref, h_ref):
        @pl.when(pl.program_id(1) == 0)
        def _():
            h_ref[...] = jnp.zeros_like(h_ref)

        hprev = h_ref[...]
        st_ref[...] = hprev
        y, hnew = _ssd_chunk(xs_ref[...].astype(F32), b_ref[...].astype(F32), c_ref[...].astype(F32), dt_ref[...],
                             hprev, dtb_ref[...], al_ref[...], dk_ref[...])
        y_ref[...] = y.astype(y_ref.dtype)
        h_ref[...] = hnew

    return pl.pallas_call(
        body, name="ssd_fwd", grid=(N_GROUPS, nchunks),
        in_specs=[xs, bm, cm, dt, par, par, par], out_specs=[xs, st],
        out_shape=[jax.ShapeDtypeStruct((nrows, D_INNER), ACT_DTYPE),
                   jax.ShapeDtypeStruct((N_GROUPS, nchunks, PAIRS_PER_GROUP, 128, N_STATE), F32)],
        scratch_shapes=[pltpu.VMEM((PAIRS_PER_GROUP, 128, N_STATE), F32)],
        compiler_params=pltpu.CompilerParams(dimension_semantics=("arbitrary", "arbitrary")),
    )(act, act, act, dtr, dtb, alog, dsk)


def _ssd_bwd(act, dtr, dtb, alog, dsk, states, dy, *, nrows):
    nchunks = nrows // CHUNK
    xs, bm, cm, dt, par, st = _ssd_specs(True, nchunks)

    def body(xs_ref, b_ref, c_ref, dt_ref, dtb_ref, al_ref, dk_ref, st_ref, dy_ref,
             dxs_ref, db_ref, dc_ref, ddt_ref, ddtb_ref, dal_ref, ddk_ref, dh_ref):
        first = pl.program_id(1) == 0

        @pl.when(first)
        def _():
            dh_ref[...] = jnp.zeros_like(dh_ref)
            ddtb_ref[...] = jnp.zeros_like(ddtb_ref)
            dal_ref[...] = jnp.zeros_like(dal_ref)
            ddk_ref[...] = jnp.zeros_like(ddk_ref)

        _, vjp = jax.vjp(_ssd_chunk, xs_ref[...].astype(F32), b_ref[...].astype(F32), c_ref[...].astype(F32),
                         dt_ref[...], st_ref[...], dtb_ref[...], al_ref[...], dk_ref[...])
        dxs, db, dc, ddt, dh, ddtb, dal, ddk = vjp((dy_ref[...].astype(F32), dh_ref[...]))
        dxs_ref[...] = dxs.astype(dxs_ref.dtype)
        db_ref[...] = db.astype(db_ref.dtype)
        dc_ref[...] = dc.astype(dc_ref.dtype)
        ddt_ref[...] = ddt
        dh_ref[...] = dh
        ddtb_ref[...] += ddtb
        dal_ref[...] += dal
        ddk_ref[...] += ddk

    nb = N_GROUPS * N_STATE
    bspec = pl.BlockSpec((CHUNK, N_STATE), lambda g, c: (nchunks - 1 - c, g))
    return pl.pallas_call(
        body, name="ssd_bwd", grid=(N_GROUPS, nchunks),
        in_specs=[xs, bm, cm, dt, par, par, par, st, xs],
        out_specs=[xs, bspec, bspec, dt, par, par, par],
        out_shape=[jax.ShapeDtypeStruct((nrows, D_INNER), ACT_DTYPE), jax.ShapeDtypeStruct((nrows, nb), ACT_DTYPE),
                   jax.ShapeDtypeStruct((nrows, nb), ACT_DTYPE), jax.ShapeDtypeStruct((nrows, DT_W), F32),
                   jax.ShapeDtypeStruct((1, DT_W), F32), jax.ShapeDtypeStruct((1, DT_W), F32),
                   jax.ShapeDtypeStruct((1, DT_W), F32)],
        scratch_shapes=[pltpu.VMEM((PAIRS_PER_GROUP, 128, N_STATE), F32)],
        compiler_params=pltpu.CompilerParams(dimension_semantics=("arbitrary", "arbitrary")),
    )(act, act, act, dtr, dtb, alog, dsk, states, dy)


def _add_epilogue(acc, r):
    return r + acc


def _rms_and_skip(x, g):
    return _rms(x, g), x


def _local_step(x, p, tgt, w):
    s = x.shape[0]
    act_t, f32 = ACT_DTYPE, F32
    mm = functools.partial(_matmul)
    h = _rows_call("pre_norm", _f_rms, [(x, D_MODEL, 0)], [w["norm_g"]], [(D_MODEL, act_t)], tm=512, nrows=s)[0]
    proj = mm(h, w["w_main"], mode="nn", name="proj", out_dtype=act_t, m=s, n=MAIN_W, k=D_MODEL)
    dtr = mm(h, w["w_dt"], mode="nn", name="proj_dt", out_dtype=f32, m=s, n=DT_W, k=D_MODEL)
    a_pars = [w["ln_a_g"], w["ln_a_b"], w["w_s"], w["b_s"]]
    y_a = _rows_call("branch_a", _f_branch_a, [(proj, UVZ_W, UVZ_CB)], a_pars, [(E_A, act_t)], tm=256, nrows=s)[0]
    act = _conv_fwd(proj, w["conv_w"], w["conv_b"], tm=512, nrows=s)
    y, states = _ssd_fwd(act, dtr, w["dt_bias"], w["a_log"], w["d_skip"], nrows=s)
    gn_rows = [(y, D_INNER, 0), (proj, ZB_W, ZB_CB)]
    y_b = _rows_call("gnorm", _f_gnorm, gn_rows, [w["ssm_norm_g"]], [(D_INNER, act_t)], tm=512, nrows=s)[0]
    o_a = mm(y_a, w["w_oa"], mode="nn", name="out_a", out_dtype=act_t, m=s, n=D_MODEL, k=E_A)
    o_b = mm(y_b, w["w_ob"], mode="nn", name="out_b", out_dtype=act_t, m=s, n=D_MODEL, k=D_INNER)
    mg_rows = [(proj, G_W, G_CB), (o_a, D_MODEL, 0), (o_b, D_MODEL, 0)]
    merged = _rows_call("merge", _f_merge, mg_rows, [], [(D_MODEL, act_t)], tm=512, nrows=s)[0]
    x1 = mm(merged, w["w_out"], mode="nn", name="out_proj", out_dtype=f32, m=s, n=D_MODEL, k=D_MODEL,
            extras=(x,), epilogue=_add_epilogue)
    hp = _rows_call("ple_norm", _f_rms, [(x1, D_MODEL, 0)], [w["ple_norm_g"]], [(D_MODEL, act_t)], tm=512, nrows=s)[0]
    gp = mm(hp, w["w_pg"], mode="nn", name="ple_gate", out_dtype=f32, m=s, n=D_MODEL, k=D_MODEL)
    pe = mm(p, w["w_ple"], mode="nn", name="ple_proj", out_dtype=f32, m=s, n=D_MODEL, k=PLE_DIM)
    g = {}
    dx2, dgp, dpe, g["final_g"], loss = _head(x1, gp, pe, tgt, w["final_g"], tm=256, nrows=s)
    g["w_pg"] = mm(hp, dgp, mode="tn", name="d_w_pg", out_dtype=f32, m=D_MODEL, n=D_MODEL, k=s)
    g["w_ple"] = mm(p, dpe, mode="tn", name="d_w_ple", out_dtype=f32, m=PLE_DIM, n=D_MODEL, k=s)
    dhp = mm(dgp, w["w_pg"], mode="nt", name="d_hp", out_dtype=act_t, m=s, n=D_MODEL, k=D_MODEL)
    dx1, g["ple_norm_g"] = _rows_vjp_call(
        "ple_norm_bwd", _rms_and_skip, [(x1, D_MODEL, 0)], [w["ple_norm_g"]], [(dhp, D_MODEL, 0), (dx2, D_MODEL, 0)],
        [(f32, None)], tm=512, nrows=s)
    dmerged = mm(dx1, w["w_out"], mode="nt", name="d_merged", out_dtype=act_t, m=s, n=D_MODEL, k=D_MODEL)
    g["w_out"] = mm(merged, dx1, mode="tn", name="d_w_out", out_dtype=f32, m=D_MODEL, n=D_MODEL, k=s)
    dproj, do_a, do_b = _rows_vjp_call(
        "merge_bwd", _f_merge, mg_rows, [], [(dmerged, D_MODEL, 0)],
        [(act_t, (None, MAIN_W, G_CB)), (act_t, None), (act_t, None)], tm=512, nrows=s)
    dy_a = mm(do_a, w["w_oa"], mode="nt", name="d_y_a", out_dtype=act_t, m=s, n=E_A, k=D_MODEL)
    g["w_oa"] = mm(y_a, do_a, mode="tn", name="d_w_oa", out_dtype=f32, m=E_A, n=D_MODEL, k=s)
    dy_b = mm(do_b, w["w_ob"], mode="nt", name="d_y_b", out_dtype=act_t, m=s, n=D_INNER, k=D_MODEL)
    g["w_ob"] = mm(y_b, do_b, mode="tn", name="d_w_ob", out_dtype=f32, m=D_INNER, n=D_MODEL, k=s)
    dy, dproj, g["ssm_norm_g"] = _rows_vjp_call(
        "gnorm_bwd", _f_gnorm, gn_rows, [w["ssm_norm_g"]], [(dy_b, D_INNER, 0)],
        [(act_t, None), (act_t, (dproj, MAIN_W, ZB_CB))], tm=256, nrows=s)
    dxs, dbm, dcm, ddtr, g["dt_bias"], g["a_log"], g["d_skip"] = _ssd_bwd(
        act, dtr, w["dt_bias"], w["a_log"], w["d_skip"], states, dy, nrows=s)
    dpre, g["conv_w"], g["conv_b"] = _conv_bwd_pre(proj, w["conv_w"], w["conv_b"], (dxs, dbm, dcm), tm=512, nrows=s)
    dproj = _conv_bwd_x(dpre, w["conv_w"], dproj, tm=512, nrows=s)
    dproj, g["ln_a_g"], g["ln_a_b"], g["w_s"], g["b_s"] = _rows_vjp_call(
        "branch_a_bwd", _f_branch_a, [(proj, UVZ_W, UVZ_CB)], a_pars, [(dy_a, E_A, 0)],
        [(act_t, (dproj, MAIN_W, UVZ_CB))], tm=128, nrows=s)
    g["w_main"] = mm(h, dproj, mode="tn", name="d_w_main", out_dtype=f32, m=D_MODEL, n=MAIN_W, k=s)
    g["w_dt"] = mm(h, ddtr, mode="tn", name="d_w_dt", out_dtype=f32, m=D_MODEL, n=DT_W, k=s)
    dh = mm(dproj, w["w_main"], mode="nt", name="d_h_main", out_dtype=f32, m=s, n=D_MODEL, k=MAIN_W)
    dh = mm(ddtr, w["w_dt"], mode="nt", name="d_h", out_dtype=f32, m=s, n=D_MODEL, k=DT_W,
            extras=(dh,), epilogue=_add_epilogue)
    grad_x, g["norm_g"] = _rows_vjp_call(
        "pre_norm_bwd", _rms_and_skip, [(x, D_MODEL, 0)], [w["norm_g"]], [(dh, D_MODEL, 0), (dx1, D_MODEL, 0)],
        [(f32, None)], tm=512, nrows=s)
    return loss, grad_x, g


_O_ZB = 3 * E_A
_O_XBC = _O_ZB + D_INNER
_O_DT = _O_XBC + CONV_DIM
_O_G = _O_DT + N_HEADS


def _heads_to_lanes(v):
    r = v.shape[0]
    v = v.reshape(r, N_GROUPS, HEADS_PER_GROUP)
    return jnp.pad(v, ((0, 0), (0, 0), (0, 128 - HEADS_PER_GROUP))).reshape(r, DT_W)


def _lanes_to_heads(v):
    r = v.shape[0]
    return v.reshape(r, N_GROUPS, 128)[:, :, :HEADS_PER_GROUP].reshape(r, N_HEADS)


def _layout_weights(f):
    w_in = f["w_in"]
    w = dict(f)
    del w["w_in"]
    w["w_main"] = jnp.concatenate([w_in[:, :_O_ZB], w_in[:, _O_XBC:_O_DT], w_in[:, _O_ZB:_O_XBC], w_in[:, _O_G:]], axis=1)
    w["w_dt"] = _heads_to_lanes(w_in[:, _O_DT:_O_G])
    w["b_s"] = f["b_s"].reshape(G_A, CHUNK, 1)
    for n in ("dt_bias", "a_log", "d_skip"):
        w[n] = _heads_to_lanes(f[n])
    return w


def _natural_grads(g):
    out = dict(g)
    gm = out.pop("w_main")
    gdt = _lanes_to_heads(out.pop("w_dt"))
    out["w_in"] = jnp.concatenate(
        [gm[:, :UVZ_W], gm[:, UVZ_W + XBC_W:UVZ_W + XBC_W + ZB_W], gm[:, UVZ_W:UVZ_W + XBC_W], gdt, gm[:, MAIN_W - G_W:]],
        axis=1)
    out["b_s"] = g["b_s"].reshape(G_A, CHUNK)
    for n in ("dt_bias", "a_log", "d_skip"):
        out[n] = _lanes_to_heads(g[n])
    return out


def _place():
    return lax.axis_index("x"), lax.axis_index("y"), lax.axis_index("c")


def _other_chips(x, y):
    return [(1 - x, y), (x, 1 - y), (1 - x, 1 - y)]


def _rcopy(src, dst, ssem, rsem, dev):
    return pltpu.make_async_remote_copy(src_ref=src, dst_ref=dst, send_sem=ssem, recv_sem=rsem,
                                        device_id=dev, device_id_type=MESH)


def _half(ref_rows, half):
    hs = ref_rows // 2
    return pl.ds(pl.multiple_of(half * hs, 16), hs)


def _gather_weights(shards, conv_shard):
    nw = len(shards)

    def body(*refs):
        sh, cv = refs[:nw], refs[nw]
        out, cvo = refs[nw + 1:2 * nw + 1], refs[2 * nw + 1]
        ici_s, ici_r, fw_s, fw_r, loc, cv_s, cv_r = refs[2 * nw + 2:]
        x, y, c = _place()
        me, sib, chips = 2 * x + y, (x, y, 1 - c), _other_chips(x, y)
        local = [pltpu.make_async_copy(sh[w], out[w].at[me], loc.at[w]) for w in range(nw)]
        local.append(pltpu.make_async_copy(cv, cvo.at[me], loc.at[nw]))
        for cp in local:
            cp.start()
        sends = []
        for w in range(nw):
            mine = _half(sh[w].shape[0], c)
            for j, chip in enumerate(chips):
                sends.append(_rcopy(sh[w].at[mine], out[w].at[me, mine], ici_s.at[3 * w + j], ici_r.at[3 * w + j], (*chip, c)))
        for j, chip in enumerate(chips):
            sends.append(_rcopy(cv, cvo.at[me], cv_s.at[j], cv_r.at[j], (*chip, c)))
        for cp in sends:
            cp.start()
        for w in range(nw):
            mine = _half(sh[w].shape[0], c)
            for j, chip in enumerate(chips):
                slab = out[w].at[2 * chip[0] + chip[1], mine]
                _rcopy(slab, slab, ici_s.at[3 * w + j], ici_r.at[3 * w + j], (*chip, c)).wait_recv()
                fwd = _rcopy(slab, slab, fw_s.at[3 * w + j], fw_r.at[3 * w + j], sib)
                fwd.start()
                sends.append(fwd)
        for j, chip in enumerate(chips):
            blk = cvo.at[2 * chip[0] + chip[1]]
            _rcopy(blk, blk, cv_s.at[j], cv_r.at[j], (*chip, c)).wait_recv()
        for w in range(nw):
            theirs = _half(sh[w].shape[0], 1 - c)
            for j, chip in enumerate(chips):
                slab = out[w].at[2 * chip[0] + chip[1], theirs]
                _rcopy(slab, slab, fw_s.at[3 * w + j], fw_r.at[3 * w + j], sib).wait_recv()
        for cp in sends:
            cp.wait_send()
        for cp in local:
            cp.wait()

    dma = pltpu.SemaphoreType.DMA
    return pl.pallas_call(
        body, name="gather_weights",
        in_specs=[ANY] * (nw + 1), out_specs=[ANY] * (nw + 1),
        out_shape=[jax.ShapeDtypeStruct((N_CHIPS,) + s.shape, s.dtype) for s in shards]
        + [jax.ShapeDtypeStruct((N_CHIPS,) + conv_shard.shape, conv_shard.dtype)],
        scratch_shapes=[dma((3 * nw,)), dma((3 * nw,)), dma((3 * nw,)), dma((3 * nw,)), dma((nw + 1,)), dma((3,)), dma((3,))],
    )(*shards, conv_shard)


def _swap_with_sibling(arrs):
    n = len(arrs)

    def body(*refs):
        src, dst, s_sem, r_sem = refs[:n], refs[n:2 * n], refs[2 * n], refs[2 * n + 1]
        x, y, c = _place()
        cps = [_rcopy(src[i], dst[i], s_sem.at[i], r_sem.at[i], (x, y, 1 - c)) for i in range(n)]
        for cp in cps:
            cp.start()
        for cp in cps:
            cp.wait()

    dma = pltpu.SemaphoreType.DMA
    return pl.pallas_call(
        body, name="swap_with_sibling", in_specs=[ANY] * n, out_specs=[ANY] * n,
        out_shape=[jax.ShapeDtypeStruct(a.shape, a.dtype) for a in arrs], scratch_shapes=[dma((n,)), dma((n,))],
    )(*arrs)


def _scatter_blocks(arrs):
    n = len(arrs)

    def body(*refs):
        src, dst, s_sem, r_sem = refs[:n], refs[n:2 * n], refs[2 * n], refs[2 * n + 1]
        x, y, c = _place()
        cps = []
        for i in range(n):
            for j, chip in enumerate(_other_chips(x, y)):
                cps.append(_rcopy(src[i].at[2 * chip[0] + chip[1]], dst[i].at[j], s_sem.at[3 * i + j], r_sem.at[3 * i + j],
                                  (*chip, c)))
        for cp in cps:
            cp.start()
        for cp in cps:
            cp.wait()

    dma = pltpu.SemaphoreType.DMA
    return pl.pallas_call(
        body, name="scatter_blocks", in_specs=[ANY] * n, out_specs=[ANY] * n,
        out_shape=[jax.ShapeDtypeStruct((3,) + a.shape[1:], a.dtype) for a in arrs],
        scratch_shapes=[dma((3 * n,)), dma((3 * n,))],
    )(*arrs)


def _share_halves(arrs):
    n = len(arrs)

    def body(*refs):
        src, dst, s_sem, r_sem, loc = refs[:n], refs[n:2 * n], refs[2 * n], refs[2 * n + 1], refs[2 * n + 2]
        x, y, c = _place()
        local, cps = [], []
        for i in range(n):
            mine = _half(dst[i].shape[0], c)
            local.append(pltpu.make_async_copy(src[i], dst[i].at[mine], loc.at[i]))
            cps.append(_rcopy(src[i], dst[i].at[mine], s_sem.at[i], r_sem.at[i], (x, y, 1 - c)))
        for cp in local + cps:
            cp.start()
        for i in range(n):
            theirs = dst[i].at[_half(dst[i].shape[0], 1 - c)]
            _rcopy(src[i], theirs, s_sem.at[i], r_sem.at[i], (x, y, 1 - c)).wait_recv()
        for cp in cps:
            cp.wait_send()
        for cp in local:
            cp.wait()

    dma = pltpu.SemaphoreType.DMA
    return pl.pallas_call(
        body, name="share_halves", in_specs=[ANY] * n, out_specs=[ANY] * n,
        out_shape=[jax.ShapeDtypeStruct((2 * a.shape[0],) + a.shape[1:], a.dtype) for a in arrs],
        scratch_shapes=[dma((n,)), dma((n,)), dma((n,))],
    )(*arrs)


def _allreduce_small(packed):
    rows = packed.shape[0]

    def body(p_ref, o_ref, buf, s_sem, r_sem):
        x, y, c = _place()
        me = 4 * x + 2 * y + c
        buf[me] = p_ref[...]
        cps = []
        for d in range(1, N_DEV):
            px, py, pc = ((1 - x) if d & 4 else x), ((1 - y) if d & 2 else y), ((1 - c) if d & 1 else c)
            cps.append(_rcopy(p_ref, buf.at[me], s_sem.at[d - 1], r_sem.at[d - 1], (px, py, pc)))
        for cp in cps:
            cp.start()
        for d in range(1, N_DEV):
            px, py, pc = ((1 - x) if d & 4 else x), ((1 - y) if d & 2 else y), ((1 - c) if d & 1 else c)
            landed = buf.at[4 * px + 2 * py + pc]
            _rcopy(p_ref, landed, s_sem.at[d - 1], r_sem.at[d - 1], (px, py, pc)).wait_recv()
        for cp in cps:
            cp.wait_send()
        acc = buf[0]
        for d in range(1, N_DEV):
            acc = acc + buf[d]
        o_ref[...] = acc

    dma = pltpu.SemaphoreType.DMA
    return pl.pallas_call(
        body, name="allreduce_small", out_shape=jax.ShapeDtypeStruct(packed.shape, F32),
        scratch_shapes=[pltpu.VMEM((N_DEV, rows, 128), F32), dma((N_DEV - 1,)), dma((N_DEV - 1,))],
    )(packed)


def _row_tile(rows, cols):
    tr = max(8, min(rows, (1 << 20) // (4 * cols) // 8 * 8))
    while rows % tr:
        tr -= 8
    return tr


def _chip_sum(name, g5, recv, c_arr):
    _, _, hs, cols = g5.shape
    tr = _row_tile(hs, cols)

    def body(_, a_ref, b_ref, o32_ref, ow_ref):
        s = a_ref[...] + b_ref[...].astype(F32)
        o32_ref[...] = s
        ow_ref[...] = s.astype(ow_ref.dtype)

    blk = pl.BlockSpec((None, tr, cols), lambda b, i, c: (b, i, 0))
    return pl.pallas_call(
        body, name=name,
        grid_spec=pltpu.PrefetchScalarGridSpec(
            num_scalar_prefetch=1, grid=(N_CHIPS, hs // tr),
            in_specs=[pl.BlockSpec((None, None, tr, cols), lambda b, i, c: (b, c[0], i, 0)), blk], out_specs=[blk, blk]),
        out_shape=[jax.ShapeDtypeStruct((N_CHIPS, hs, cols), F32), jax.ShapeDtypeStruct((N_CHIPS, hs, cols), WIRE_DTYPE)],
    )(c_arr, g5, recv)


def _final_sum(name, own, recv, me_arr):
    _, hs, cols = own.shape
    tr = _row_tile(hs, cols)

    def body(_, a_ref, r_ref, o_ref):
        o_ref[...] = ((a_ref[...] + r_ref[0].astype(F32)) + r_ref[1].astype(F32)) + r_ref[2].astype(F32)

    return pl.pallas_call(
        body, name=name,
        grid_spec=pltpu.PrefetchScalarGridSpec(
            num_scalar_prefetch=1, grid=(hs // tr,),
            in_specs=[pl.BlockSpec((None, tr, cols), lambda i, m: (m[0], i, 0)),
                      pl.BlockSpec((3, tr, cols), lambda i, m: (0, i, 0))],
            out_specs=pl.BlockSpec((tr, cols), lambda i, m: (i, 0))),
        out_shape=jax.ShapeDtypeStruct((hs, cols), F32),
    )(me_arr, own, recv)


def _adamw(w, g, m, v):
    m = ADAM_B1 * m + (1.0 - ADAM_B1) * g
    v = ADAM_B2 * v + (1.0 - ADAM_B2) * (g * g)
    m_hat = m / (1.0 - ADAM_B1 ** ADAM_STEP)
    v_hat = v / (1.0 - ADAM_B2 ** ADAM_STEP)
    return -ADAM_LR * (m_hat / (jnp.sqrt(v_hat) + ADAM_EPS) + ADAM_WD * w), m, v


def _adamw_call(name, w, g, m, v):
    rows, cols = w.shape
    tr = _row_tile(rows, cols)

    def body(w_ref, g_ref, m_ref, v_ref, d_ref, nm_ref, nv_ref):
        d_ref[...], nm_ref[...], nv_ref[...] = _adamw(w_ref[...], g_ref[...], m_ref[...], v_ref[...])

    blk = pl.BlockSpec((tr, cols), lambda i: (i, 0))
    return pl.pallas_call(
        body, name=name, grid=(rows // tr,), in_specs=[blk] * 4, out_specs=[blk] * 3,
        out_shape=[jax.ShapeDtypeStruct(w.shape, F32)] * 3,
        compiler_params=pltpu.CompilerParams(dimension_semantics=("parallel",)),
    )(w, g, m, v)


def _adamw_small(ws, gs, ms, vs):
    n = len(ws)

    def body(*refs):
        for i in range(n):
            w_ref, g_ref, m_ref, v_ref = (refs[k * n + i] for k in range(4))
            d, nm, nv = _adamw(w_ref[...], g_ref[...], m_ref[...], v_ref[...])
            refs[4 * n + i][...] = d
            refs[5 * n + i][...] = nm
            refs[6 * n + i][...] = nv

    out = pl.pallas_call(
        body, name="adamw_small", out_shape=[jax.ShapeDtypeStruct(a.shape, F32) for a in ws] * 3,
    )(*ws, *gs, *ms, *vs)
    return out[:n], out[n:2 * n], out[2 * n:]


_BIG = ("w_in", "w_oa", "w_ob", "w_out", "w_pg", "w_ple")
_SMALL = ("norm_g", "ln_a_g", "ln_a_b", "w_s", "b_s", "conv_w", "conv_b", "dt_bias", "a_log", "d_skip", "ssm_norm_g",
          "ple_norm_g", "final_g")
_WEIGHTS = ("norm_g", "w_in", "ln_a_g", "ln_a_b", "w_s", "b_s", "conv_w", "conv_b", "dt_bias", "a_log", "d_skip",
            "ssm_norm_g", "w_oa", "w_ob", "w_out", "ple_norm_g", "w_pg", "w_ple", "final_g")
_COL_SHARDED = ("w_in", "w_ple")
_PACK = 1024


def _blocks_to_full(n, blocks):
    if n in _COL_SHARDED:
        return blocks.transpose(1, 0, 2).reshape(blocks.shape[1], N_CHIPS * blocks.shape[2])
    return blocks.reshape(N_CHIPS * blocks.shape[1], blocks.shape[2])


def _full_to_blocks(n, full):
    if n in _COL_SHARDED:
        return full.reshape(full.shape[0], N_CHIPS, full.shape[1] // N_CHIPS).transpose(1, 0, 2)
    return full.reshape(N_CHIPS, full.shape[0] // N_CHIPS, full.shape[1])


def _two_d(n, a):
    if n == "w_s":
        return a.reshape(G_A * CHUNK, CHUNK)
    if n in ("b_s", "conv_w"):
        return a.reshape(a.shape[-2], a.shape[-1])
    return a.reshape(1, a.shape[-1])


def kernel(x, p, norm_g, w_in, ln_a_g, ln_a_b, w_s, b_s, conv_w, conv_b, dt_bias, a_log, d_skip, ssm_norm_g, w_oa, w_ob, w_out, ple_norm_g, w_pg, w_ple, final_g, loss_target, m_norm_g, m_w_in, m_ln_a_g, m_ln_a_b, m_w_s, m_b_s, m_conv_w, m_conv_b, m_dt_bias, m_a_log, m_d_skip, m_ssm_norm_g, m_w_oa, m_w_ob, m_w_out, m_ple_norm_g, m_w_pg, m_w_ple, m_final_g, v_norm_g, v_w_in, v_ln_a_g, v_ln_a_b, v_w_s, v_b_s, v_conv_w, v_conv_b, v_dt_bias, v_a_log, v_d_skip, v_ssm_norm_g, v_w_oa, v_w_ob, v_w_out, v_ple_norm_g, v_w_pg, v_w_ple, v_final_g):
    wt = dict(norm_g=norm_g, w_in=w_in, ln_a_g=ln_a_g, ln_a_b=ln_a_b, w_s=w_s, b_s=b_s, conv_w=conv_w, conv_b=conv_b,
              dt_bias=dt_bias, a_log=a_log, d_skip=d_skip, ssm_norm_g=ssm_norm_g, w_oa=w_oa, w_ob=w_ob, w_out=w_out,
              ple_norm_g=ple_norm_g, w_pg=w_pg, w_ple=w_ple, final_g=final_g)
    mom = dict(norm_g=m_norm_g, w_in=m_w_in, ln_a_g=m_ln_a_g, ln_a_b=m_ln_a_b, w_s=m_w_s, b_s=m_b_s, conv_w=m_conv_w,
               conv_b=m_conv_b, dt_bias=m_dt_bias, a_log=m_a_log, d_skip=m_d_skip, ssm_norm_g=m_ssm_norm_g, w_oa=m_w_oa,
               w_ob=m_w_ob, w_out=m_w_out, ple_norm_g=m_ple_norm_g, w_pg=m_w_pg, w_ple=m_w_ple, final_g=m_final_g)
    vel = dict(norm_g=v_norm_g, w_in=v_w_in, ln_a_g=v_ln_a_g, ln_a_b=v_ln_a_b, w_s=v_w_s, b_s=v_b_s, conv_w=v_conv_w,
               conv_b=v_conv_b, dt_bias=v_dt_bias, a_log=v_a_log, d_skip=v_d_skip, ssm_norm_g=v_ssm_norm_g, w_oa=v_w_oa,
               w_ob=v_w_ob, w_out=v_w_out, ple_norm_g=v_ple_norm_g, w_pg=v_w_pg, w_ple=v_w_ple, final_g=v_final_g)
    xi, yi, ci = _place()
    me = 2 * xi + yi
    c_arr = jnp.reshape(ci, (1,)).astype(jnp.int32)
    me_arr = jnp.reshape(me, (1,)).astype(jnp.int32)

    shard = {n: wt[n][0] for n in _BIG}
    blocks = _gather_weights([shard[n].astype(WIRE_DTYPE) for n in _BIG], conv_w[0])
    full = {n: _blocks_to_full(n, b) for n, b in zip(_BIG, blocks[:-1])}
    full["conv_w"] = _blocks_to_full("w_in", blocks[-1])
    for n in _SMALL:
        if n != "conv_w":
            full[n] = wt[n][0] if wt[n].ndim > 2 else wt[n].reshape(1, wt[n].shape[-1])

    loss_row, grad_x, g = _local_step(x[0], p[0, 0], loss_target[0], _layout_weights(full))
    g = _natural_grads(g)
    loss = lax.psum(loss_row[0, 0], ("x", "y", "c"))

    g5 = {}
    for n in _BIG:
        b = _full_to_blocks(n, g[n])
        g5[n] = b.reshape(N_CHIPS, 2, b.shape[1] // 2, b.shape[2])
    to_sibling = [lax.dynamic_index_in_dim(g5[n], 1 - ci, axis=1, keepdims=False).astype(WIRE_DTYPE) for n in _BIG]
    from_sibling = _swap_with_sibling(to_sibling)
    chip32, chip_wire = zip(*[_chip_sum("chip_sum_" + n, g5[n], r, c_arr) for n, r in zip(_BIG, from_sibling)])
    from_chips = _scatter_blocks(list(chip_wire))
    halves = [_final_sum("final_sum_" + n, a, r, me_arr) for n, a, r in zip(_BIG, chip32, from_chips)]
    grads = dict(zip(_BIG, _share_halves(halves)))

    pieces = [_two_d(n, g[n]).reshape(-1) for n in _SMALL]
    sizes = [v.shape[0] for v in pieces]
    padded = [-(-s // _PACK) * _PACK for s in sizes]
    packed = jnp.concatenate([jnp.pad(v, (0, ps - s)) for v, s, ps in zip(pieces, sizes, padded)]).reshape(-1, 128)
    summed = _allreduce_small(packed).reshape(-1)
    off = 0
    for n, s, ps in zip(_SMALL, sizes, padded):
        grads[n] = _two_d(n, g[n]).shape, summed[off:off + s]
        off += ps
    for n in _SMALL:
        shape, flat = grads[n]
        grads[n] = flat.reshape(shape)
    grads["conv_w"] = lax.dynamic_slice_in_dim(grads["conv_w"], me * (CONV_DIM // N_CHIPS), CONV_DIM // N_CHIPS, axis=1)

    delta, new_m, new_v = {}, {}, {}
    for n in _BIG:
        delta[n], new_m[n], new_v[n] = _adamw_call("adamw_" + n, shard[n], grads[n], mom[n][0], vel[n][0])
    small = _adamw_small([_two_d(n, wt[n]) for n in _SMALL], [grads[n] for n in _SMALL],
                         [_two_d(n, mom[n]) for n in _SMALL], [_two_d(n, vel[n]) for n in _SMALL])
    for i, n in enumerate(_SMALL):
        delta[n], new_m[n], new_v[n] = small[0][i], small[1][i], small[2][i]

    def shaped(d):
        return [d[n].reshape(wt[n].shape) for n in _WEIGHTS]

    return (loss, grad_x[None], *shaped(grads), *shaped(delta), *shaped(new_m), *shaped(new_v))
```

```python
import functools

import jax
import jax.numpy as jnp
from jax import lax
from jax.experimental import pallas as pl
from jax.experimental.pallas import tpu as pltpu

F32 = jnp.float32
MXU_DTYPE = jnp.bfloat16
ACT_DTYPE = jnp.bfloat16
WIRE_DTYPE = jnp.bfloat16

D_MODEL = 1024
PLE_DIM = 256
CHUNK = 128
EPS = 1e-6
E_A = D_MODEL
G_A = 4
D_INNER = 2 * D_MODEL
HEAD_DIM = 64
N_HEADS = D_INNER // HEAD_DIM
N_STATE = 128
N_GROUPS = 4
HEADS_PER_GROUP = N_HEADS // N_GROUPS
PAIRS_PER_GROUP = HEADS_PER_GROUP // 2
CONV_K = 4
CONV_DIM = D_INNER + 2 * N_GROUPS * N_STATE
N_IN = 3 * E_A + D_INNER + CONV_DIM + N_HEADS + 2 * D_MODEL
N_CHIPS = 4
N_DEV = 8
W_IN_BLOCK = N_IN // N_CHIPS

UVZ_W, XBC_W, ZB_W, G_W = 3 * E_A, CONV_DIM, D_INNER, 2 * D_MODEL
MAIN_W = UVZ_W + XBC_W + ZB_W + G_W
UVZ_CB, XBC_CB, ZB_CB, G_CB = 0, 1, 3, 4
DT_W = N_GROUPS * 128

ADAM_LR, ADAM_B1, ADAM_B2, ADAM_EPS, ADAM_WD, ADAM_STEP = 0.001, 0.9, 0.999, 1e-08, 0.01, 10

MESH = pl.DeviceIdType.MESH
ANY = pl.BlockSpec(memory_space=pl.ANY)


def _mxu(v):
    return v.astype(MXU_DTYPE)


def _dot(a, b, dims=(((1,), (0,)), ((), ()))):
    return lax.dot_general(_mxu(a), _mxu(b), dims, preferred_element_type=F32)


def _matmul(a, b, *, mode, name, out_dtype, m, n, k, tm=1024, tn=1024, tk=1024, a_off=0, b_off=0,
            extras=(), epilogue=None):
    tm, tn, tk = min(tm, m), min(tn, n), min(tk, k)
    assert m % tm == 0 and n % tn == 0 and k % tk == 0, (name, m, n, k, tm, tn, tk)
    nk = k // tk
    if mode == "nn":
        assert a_off % tk == 0 and b_off % tn == 0
        a_spec = pl.BlockSpec((tm, tk), lambda i, j, kk: (i, kk + a_off // tk))
        b_spec = pl.BlockSpec((tk, tn), lambda i, j, kk: (kk, j + b_off // tn))
        dims = (((1,), (0,)), ((), ()))
    elif mode == "nt":
        a_spec = pl.BlockSpec((tm, tk), lambda i, j, kk: (i, kk))
        b_spec = pl.BlockSpec((tn, tk), lambda i, j, kk: (j, kk))
        dims = (((1,), (1,)), ((), ()))
    else:
        assert a_off % tm == 0 and b_off % tn == 0
        a_spec = pl.BlockSpec((tk, tm), lambda i, j, kk: (kk, i + a_off // tm))
        b_spec = pl.BlockSpec((tk, tn), lambda i, j, kk: (kk, j + b_off // tn))
        dims = (((0,), (0,)), ((), ()))
    ne = len(extras)

    def finish(acc, extra_refs, o_ref):
        res = acc if epilogue is None else epilogue(acc, *[e[...] for e in extra_refs])
        o_ref[...] = res.astype(o_ref.dtype)

    def body(a_ref, b_ref, *rest):
        extra_refs, o_ref = rest[:ne], rest[ne]
        part = _dot(a_ref[...], b_ref[...], dims)
        if nk == 1:
            finish(part, extra_refs, o_ref)
            return
        acc_ref = rest[ne + 1]
        kk = pl.program_id(2)

        @pl.when(kk == 0)
        def _():
            acc_ref[...] = part

        @pl.when(kk > 0)
        def _():
            acc_ref[...] += part

        @pl.when(kk == nk - 1)
        def _():
            finish(acc_ref[...], extra_refs, o_ref)

    o_spec = pl.BlockSpec((tm, tn), lambda i, j, kk: (i, j))
    return pl.pallas_call(
        body, name=name, grid=(m // tm, n // tn, nk),
        in_specs=[a_spec, b_spec] + [o_spec] * ne, out_specs=o_spec,
        out_shape=jax.ShapeDtypeStruct((m, n), out_dtype),
        scratch_shapes=[pltpu.VMEM((tm, tn), F32)] if nk > 1 else [],
        compiler_params=pltpu.CompilerParams(dimension_semantics=("parallel", "parallel", "arbitrary")),
    )(a, b, *extras)


def _row_spec(tm, width, cb):
    return pl.BlockSpec((tm, width), lambda i: (i, cb))


def _whole_spec(shape):
    nd = len(shape)
    return pl.BlockSpec(tuple(shape), lambda i: (0,) * nd)


def _rows_call(name, f, rows, pars, outs, *, tm, nrows):
    tm = min(tm, nrows)
    nr, npar = len(rows), len(pars)

    def body(*refs):
        rv = [r[...].astype(F32) for r in refs[:nr]]
        pv = [p[...] for p in refs[nr:nr + npar]]
        res = f(*rv, *pv)
        for o_ref, r in zip(refs[nr + npar:], res):
            o_ref[...] = r.astype(o_ref.dtype)

    return pl.pallas_call(
        body, name=name, grid=(nrows // tm,),
        in_specs=[_row_spec(tm, w, cb) for _, w, cb in rows] + [_whole_spec(p.shape) for p in pars],
        out_specs=[_row_spec(tm, w, 0) for w, _ in outs],
        out_shape=[jax.ShapeDtypeStruct((nrows, w), dt) for w, dt in outs],
        compiler_params=pltpu.CompilerParams(dimension_semantics=("parallel",)),
    )(*[r[0] for r in rows], *pars)


def _rows_vjp_call(name, f, rows, pars, cots, drows, *, tm, nrows):
    tm = min(tm, nrows)
    nr, npar, nc = len(rows), len(pars), len(cots)
    alias_bufs, aliases = [], {}
    out_shape, out_specs = [], []
    for (arr, w, cb), d in zip(rows, drows):
        if d is None:
            continue
        dt, into = d
        if into is None:
            out_shape.append(jax.ShapeDtypeStruct((nrows, w), dt))
            out_specs.append(_row_spec(tm, w, 0))
        else:
            buf, total, ocb = into
            if buf is not None:
                aliases[nr + npar + nc + len(alias_bufs)] = len(out_shape)
                alias_bufs.append(buf)
            out_shape.append(jax.ShapeDtypeStruct((nrows, total), dt))
            out_specs.append(_row_spec(tm, w, ocb))
    n_drow = len(out_shape)
    for p in pars:
        out_shape.append(jax.ShapeDtypeStruct(p.shape, F32))
        out_specs.append(_whole_spec(p.shape))
    na = len(alias_bufs)

    def body(*refs):
        rv = [r[...].astype(F32) for r in refs[:nr]]
        pv = [p[...] for p in refs[nr:nr + npar]]
        cv = tuple(c[...].astype(F32) for c in refs[nr + npar:nr + npar + nc])
        o_refs = refs[nr + npar + nc + na:]
        _, vjp = jax.vjp(f, *rv, *pv)
        g = vjp(cv)
        oi = 0
        for ri, d in enumerate(drows):
            if d is not None:
                o_refs[oi][...] = g[ri].astype(o_refs[oi].dtype)
                oi += 1
        first = pl.program_id(0) == 0
        for pi in range(npar):
            acc = o_refs[n_drow + pi]

            @pl.when(first)
            def _(acc=acc):
                acc[...] = jnp.zeros_like(acc)

            acc[...] += g[nr + pi]

    return pl.pallas_call(
        body, name=name, grid=(nrows // tm,),
        in_specs=[_row_spec(tm, w, cb) for _, w, cb in rows] + [_whole_spec(p.shape) for p in pars]
        + [_row_spec(tm, w, cb) for _, w, cb in cots] + [ANY] * na,
        out_specs=out_specs, out_shape=out_shape, input_output_aliases=aliases,
        compiler_params=pltpu.CompilerParams(dimension_semantics=("arbitrary",)),
    )(*[r[0] for r in rows], *pars, *[c[0] for c in cots], *alias_bufs)


def _rms(x, g):
    return x * lax.rsqrt(jnp.mean(x * x, axis=-1, keepdims=True) + EPS) * g


def _f_rms(x, g):
    return (_rms(x, g),)


def _tril_mask():
    return lax.broadcasted_iota(jnp.int32, (CHUNK, CHUNK), 0) >= lax.broadcasted_iota(jnp.int32, (CHUNK, CHUNK), 1)


def _f_branch_a(uvz, ln_g, ln_b, w_s, b_s):
    u = jax.nn.gelu(uvz[:, :E_A])
    v = jax.nn.gelu(uvz[:, E_A:2 * E_A])
    z = uvz[:, 2 * E_A:]
    xc = v - jnp.mean(v, axis=-1, keepdims=True)
    vn = xc * lax.rsqrt(jnp.mean(xc * xc, axis=-1, keepdims=True) + EPS) * ln_g + ln_b
    mask = _tril_mask()
    ws = [jnp.where(mask, w_s[g], 0.0) for g in range(G_A)]
    gw = E_A // G_A
    rows = []
    for c in range(uvz.shape[0] // CHUNK):
        vc = vn[c * CHUNK:(c + 1) * CHUNK]
        rows.append(jnp.concatenate([_dot(ws[g], vc[:, g * gw:(g + 1) * gw]) + b_s[g] for g in range(G_A)], axis=1))
    sv = rows[0] if len(rows) == 1 else jnp.concatenate(rows, axis=0)
    return (u * sv * jax.nn.silu(z),)


def _f_gnorm(y, zb, g):
    yz = y * jax.nn.silu(zb)
    gw = D_INNER // N_GROUPS
    parts = []
    for i in range(N_GROUPS):
        s = yz[:, i * gw:(i + 1) * gw]
        parts.append(s * lax.rsqrt(jnp.mean(s * s, axis=-1, keepdims=True) + EPS))
    return (jnp.concatenate(parts, axis=1) * g,)


def _f_merge(g2, oa, ob):
    return (jax.nn.sigmoid(g2[:, :D_MODEL]) * oa + jax.nn.sigmoid(g2[:, D_MODEL:]) * ob,)


def _f_loss(x1, gp, pe, tgt, fg):
    x2 = x1 + jax.nn.sigmoid(gp) * pe
    err = _rms(x2, fg) - tgt
    return 0.5 * jnp.sum(jnp.mean(err * err, axis=-1))


def _head(x1, gp, pe, tgt, fg, *, tm, nrows):
    tm = min(tm, nrows)

    def body(x1_ref, gp_ref, pe_ref, t_ref, fg_ref, dx_ref, dgp_ref, dpe_ref, dfg_ref, loss_ref):
        loss, vjp = jax.vjp(_f_loss, x1_ref[...], gp_ref[...], pe_ref[...], t_ref[...], fg_ref[...])
        dx, dgp, dpe, _, dfg = vjp(jnp.ones((), F32))
        dx_ref[...] = dx
        dgp_ref[...] = dgp.astype(dgp_ref.dtype)
        dpe_ref[...] = dpe.astype(dpe_ref.dtype)

        @pl.when(pl.program_id(0) == 0)
        def _():
            dfg_ref[...] = jnp.zeros_like(dfg_ref)
            loss_ref[...] = jnp.zeros_like(loss_ref)

        dfg_ref[...] += dfg
        loss_ref[...] += jnp.full(loss_ref.shape, loss, F32)

    row = _row_spec(tm, D_MODEL, 0)
    return pl.pallas_call(
        body, name="head", grid=(nrows // tm,),
        in_specs=[row, row, row, row, _whole_spec((1, D_MODEL))],
        out_specs=[row, row, row, _whole_spec((1, D_MODEL)), _whole_spec((1, 128))],
        out_shape=[jax.ShapeDtypeStruct((nrows, D_MODEL), F32), jax.ShapeDtypeStruct((nrows, D_MODEL), ACT_DTYPE),
                   jax.ShapeDtypeStruct((nrows, D_MODEL), ACT_DTYPE), jax.ShapeDtypeStruct((1, D_MODEL), F32),
                   jax.ShapeDtypeStruct((1, 128), F32)],
        compiler_params=pltpu.CompilerParams(dimension_semantics=("arbitrary",)),
    )(x1, gp, pe, tgt, fg)


def _shift_rows(cur, edge, j, up):
    tm = cur.shape[0]
    row = lax.broadcasted_iota(jnp.int32, cur.shape, 0)
    if up:
        sh = pltpu.roll(cur, tm - j, 0)
        e = jnp.tile(pltpu.roll(edge, 8 - j, 0), (tm // 8, 1))
        return jnp.where(row >= tm - j, e, sh)
    sh = pltpu.roll(cur, j, 0)
    e = jnp.tile(pltpu.roll(edge, j, 0), (tm // 8, 1))
    return jnp.where(row < j, e, sh)


def _conv_pre(cur, prev, w, b):
    acc = cur * w[CONV_K - 1:CONV_K] + b
    taps = [cur]
    for j in range(1, CONV_K):
        s = _shift_rows(cur, prev, j, up=False)
        taps.append(s)
        acc = acc + s * w[CONV_K - 1 - j:CONV_K - j]
    return acc, taps


def _halo_specs(tm, nrows, cb, before):
    nb = tm // 8
    last = nrows // 8 - 1
    if before:
        return pl.BlockSpec((8, XBC_W), lambda i: (jnp.maximum(i * nb - 1, 0), cb))
    return pl.BlockSpec((8, XBC_W), lambda i: (jnp.minimum((i + 1) * nb, last), cb))


def _conv_fwd(proj, conv_w, conv_b, *, tm, nrows):
    tm = min(tm, nrows)

    def body(cur_ref, prev_ref, w_ref, b_ref, o_ref):
        prev = jnp.where(pl.program_id(0) == 0, 0.0, prev_ref[...].astype(F32))
        pre, _ = _conv_pre(cur_ref[...].astype(F32), prev, w_ref[...], b_ref[...])
        o_ref[...] = jax.nn.silu(pre).astype(o_ref.dtype)

    return pl.pallas_call(
        body, name="conv_fwd", grid=(nrows // tm,),
        in_specs=[_row_spec(tm, XBC_W, XBC_CB), _halo_specs(tm, nrows, XBC_CB, True),
                  _whole_spec((CONV_K, XBC_W)), _whole_spec((1, XBC_W))],
        out_specs=_row_spec(tm, XBC_W, 0), out_shape=jax.ShapeDtypeStruct((nrows, XBC_W), ACT_DTYPE),
        compiler_params=pltpu.CompilerParams(dimension_semantics=("parallel",)),
    )(proj, proj, conv_w, conv_b)


def _conv_bwd_pre(proj, conv_w, conv_b, dact, *, tm, nrows):
    tm = min(tm, nrows)
    nb = N_GROUPS * N_STATE

    def body(cur_ref, prev_ref, w_ref, b_ref, dxs_ref, dbm_ref, dcm_ref, dpre_ref, dw_ref, db_ref):
        prev = jnp.where(pl.program_id(0) == 0, 0.0, prev_ref[...].astype(F32))
        pre, taps = _conv_pre(cur_ref[...].astype(F32), prev, w_ref[...], b_ref[...])
        sg = jax.nn.sigmoid(pre)
        dy = jnp.concatenate([dxs_ref[...], dbm_ref[...], dcm_ref[...]], axis=1).astype(F32)
        dpre = dy * sg * (1.0 + pre * (1.0 - sg))
        dpre_ref[...] = dpre.astype(dpre_ref.dtype)

        @pl.when(pl.program_id(0) == 0)
        def _():
            dw_ref[...] = jnp.zeros_like(dw_ref)
            db_ref[...] = jnp.zeros_like(db_ref)

        db_ref[...] += jnp.sum(dpre, axis=0, keepdims=True)
        for j in range(CONV_K):
            k = CONV_K - 1 - j
            dw_ref[k:k + 1, :] += jnp.sum(dpre * taps[j], axis=0, keepdims=True)

    return pl.pallas_call(
        body, name="conv_bwd_pre", grid=(nrows // tm,),
        in_specs=[_row_spec(tm, XBC_W, XBC_CB), _halo_specs(tm, nrows, XBC_CB, True),
                  _whole_spec((CONV_K, XBC_W)), _whole_spec((1, XBC_W)),
                  _row_spec(tm, D_INNER, 0), _row_spec(tm, nb, 0), _row_spec(tm, nb, 0)],
        out_specs=[_row_spec(tm, XBC_W, 0), _whole_spec((CONV_K, XBC_W)), _whole_spec((1, XBC_W))],
        out_shape=[jax.ShapeDtypeStruct((nrows, XBC_W), ACT_DTYPE), jax.ShapeDtypeStruct((CONV_K, XBC_W), F32),
                   jax.ShapeDtypeStruct((1, XBC_W), F32)],
        compiler_params=pltpu.CompilerParams(dimension_semantics=("arbitrary",)),
    )(proj, proj, conv_w, conv_b, *dact)


def _conv_bwd_x(dpre, conv_w, dproj, *, tm, nrows):
    tm = min(tm, nrows)
    ntiles = nrows // tm

    def body(cur_ref, nxt_ref, w_ref, _, o_ref):
        cur = cur_ref[...].astype(F32)
        nxt = jnp.where(pl.program_id(0) == ntiles - 1, 0.0, nxt_ref[...].astype(F32))
        w = w_ref[...]
        acc = cur * w[CONV_K - 1:CONV_K]
        for j in range(1, CONV_K):
            acc = acc + _shift_rows(cur, nxt, j, up=True) * w[CONV_K - 1 - j:CONV_K - j]
        o_ref[...] = acc.astype(o_ref.dtype)

    return pl.pallas_call(
        body, name="conv_bwd_x", grid=(ntiles,),
        in_specs=[_row_spec(tm, XBC_W, 0), _halo_specs(tm, nrows, 0, False), _whole_spec((CONV_K, XBC_W)), ANY],
        out_specs=_row_spec(tm, XBC_W, XBC_CB), out_shape=jax.ShapeDtypeStruct(dproj.shape, dproj.dtype),
        input_output_aliases={3: 0},
        compiler_params=pltpu.CompilerParams(dimension_semantics=("parallel",)),
    )(dpre, dpre, conv_w, dproj)


SSD_SPAN = 4
_XS_GW = D_INNER // N_GROUPS
_NT = (((1,), (1,)), ((), ()))
_TN = (((0,), (0,)), ((), ()))


def _pair_lanes(cols, lo):
    return jnp.concatenate(
        [jnp.where(lo, cols[:, 2 * q:2 * q + 1], cols[:, 2 * q + 1:2 * q + 2]) for q in range(PAIRS_PER_GROUP)], axis=1)


def _ssd_chunk(k, xs, bm, cm, dtr, hprev, dtb, alog, dsk):
    causal, tri, lo = k
    dt = jax.nn.softplus(dtr + dtb)
    da = dt * (-jnp.exp(alog))
    cs = jnp.dot(tri, da, precision=lax.Precision.HIGHEST, preferred_element_type=F32)
    cst = cs.T
    last = cs[CHUNK - 1:CHUNK, :]
    xdt = xs * _pair_lanes(dt, lo)
    cb = _dot(cm, bm, _NT)
    yd = []
    for q in range(PAIRS_PER_GROUP):
        xq = xdt[:, 128 * q:128 * (q + 1)]
        y2 = [_dot(cb * jnp.exp(jnp.where(causal, cs[:, h:h + 1] - cst[h:h + 1, :], -jnp.inf)), xq)
              for h in (2 * q, 2 * q + 1)]
        yd.append(jnp.where(lo, y2[0], y2[1]))
    y_off = _pair_lanes(jnp.exp(cs), lo) * _dot(cm, hprev, _NT)
    st = _dot(xdt * _pair_lanes(jnp.exp(last - cs), lo), bm, _TN)
    cdec = jnp.exp(last)
    cd_rows = jnp.concatenate(
        [jnp.broadcast_to(cdec[:, h:h + 1], (HEAD_DIM, N_STATE)) for h in range(HEADS_PER_GROUP)], axis=0)
    y = jnp.concatenate(yd, axis=1) + y_off + xs * _pair_lanes(dsk, lo[:1])
    return y, cd_rows * hprev + st


def _ssd_span(xs, bm, cm, dtr, h0, dtb, alog, dsk):
    li = lax.broadcasted_iota(jnp.int32, (CHUNK, CHUNK), 0)
    si = lax.broadcasted_iota(jnp.int32, (CHUNK, CHUNK), 1)
    causal = li >= si
    k = (causal, causal.astype(F32), si < HEAD_DIM)
    h, ys = h0, []
    for t in range(xs.shape[0] // CHUNK):
        r = slice(t * CHUNK, (t + 1) * CHUNK)
        y, h = _ssd_chunk(k, xs[r], bm[r], cm[r], dtr[r], h, dtb, alog, dsk)
        ys.append(y)
    return (ys[0] if len(ys) == 1 else jnp.concatenate(ys, axis=0)), h


def _ssd_specs(rev, nsteps, rows):
    def s_of(s):
        return nsteps - 1 - s if rev else s

    xs = pl.BlockSpec((rows, _XS_GW), lambda g, s: (s_of(s), g))
    bm = pl.BlockSpec((rows, N_STATE), lambda g, s: (s_of(s), D_INNER // N_STATE + g))
    cm = pl.BlockSpec((rows, N_STATE), lambda g, s: (s_of(s), D_INNER // N_STATE + N_GROUPS + g))
    dt = pl.BlockSpec((rows, 128), lambda g, s: (s_of(s), g))
    par = pl.BlockSpec((1, 128), lambda g, s: (0, g))
    st = pl.BlockSpec((None, None, _XS_GW, N_STATE), lambda g, s: (g, s_of(s), 0, 0))
    return xs, bm, cm, dt, par, st


def _ssd_fwd(act, dtr, dtb, alog, dsk, *, nrows):
    rows = CHUNK * min(SSD_SPAN, nrows // CHUNK)
    nsteps = nrows // rows
    xs, bm, cm, dt, par, st = _ssd_specs(False, nsteps, rows)

    def body(xs_ref, b_ref, c_ref, dt_ref, dtb_ref, al_ref, dk_ref, y_ref, st_ref, h_ref):
        @pl.when(pl.program_id(1) == 0)
        def _():
            h_ref[...] = jnp.zeros_like(h_ref)

        h0 = h_ref[...]
        st_ref[...] = h0
        y, hnew = _ssd_span(xs_ref[...].astype(F32), b_ref[...].astype(F32), c_ref[...].astype(F32), dt_ref[...],
                            h0, dtb_ref[...], al_ref[...], dk_ref[...])
        y_ref[...] = y.astype(y_ref.dtype)
        h_ref[...] = hnew

    return pl.pallas_call(
        body, name="ssd_fwd", grid=(N_GROUPS, nsteps),
        in_specs=[xs, bm, cm, dt, par, par, par], out_specs=[xs, st],
        out_shape=[jax.ShapeDtypeStruct((nrows, D_INNER), ACT_DTYPE),
                   jax.ShapeDtypeStruct((N_GROUPS, nsteps, _XS_GW, N_STATE), F32)],
        scratch_shapes=[pltpu.VMEM((_XS_GW, N_STATE), F32)],
        compiler_params=pltpu.CompilerParams(dimension_semantics=("arbitrary", "arbitrary")),
    )(act, act, act, dtr, dtb, alog, dsk)


def _ssd_bwd(act, dtr, dtb, alog, dsk, states, dy, *, nrows):
    rows = CHUNK * min(SSD_SPAN, nrows // CHUNK)
    nsteps = nrows // rows
    xs, bm, cm, dt, par, st = _ssd_specs(True, nsteps, rows)

    def body(xs_ref, b_ref, c_ref, dt_ref, dtb_ref, al_ref, dk_ref, st_ref, dy_ref,
             dxs_ref, db_ref, dc_ref, ddt_ref, ddtb_ref, dal_ref, ddk_ref, dh_ref):
        @pl.when(pl.program_id(1) == 0)
        def _():
            dh_ref[...] = jnp.zeros_like(dh_ref)
            ddtb_ref[...] = jnp.zeros_like(ddtb_ref)
            dal_ref[...] = jnp.zeros_like(dal_ref)
            ddk_ref[...] = jnp.zeros_like(ddk_ref)

        _, vjp = jax.vjp(_ssd_span, xs_ref[...].astype(F32), b_ref[...].astype(F32), c_ref[...].astype(F32),
                         dt_ref[...], st_ref[...], dtb_ref[...], al_ref[...], dk_ref[...])
        dxs, db, dc, ddt, dh, ddtb, dal, ddk = vjp((dy_ref[...].astype(F32), dh_ref[...]))
        dxs_ref[...] = dxs.astype(dxs_ref.dtype)
        db_ref[...] = db.astype(db_ref.dtype)
        dc_ref[...] = dc.astype(dc_ref.dtype)
        ddt_ref[...] = ddt
        dh_ref[...] = dh
        ddtb_ref[...] += ddtb
        dal_ref[...] += dal
        ddk_ref[...] += ddk

    nb = N_GROUPS * N_STATE
    bspec = pl.BlockSpec((rows, N_STATE), lambda g, s: (nsteps - 1 - s, g))
    return pl.pallas_call(
        body, name="ssd_bwd", grid=(N_GROUPS, nsteps),
        in_specs=[xs, bm, cm, dt, par, par, par, st, xs],
        out_specs=[xs, bspec, bspec, dt, par, par, par],
        out_shape=[jax.ShapeDtypeStruct((nrows, D_INNER), ACT_DTYPE), jax.ShapeDtypeStruct((nrows, nb), ACT_DTYPE),
                   jax.ShapeDtypeStruct((nrows, nb), ACT_DTYPE), jax.ShapeDtypeStruct((nrows, DT_W), F32),
                   jax.ShapeDtypeStruct((1, DT_W), F32), jax.ShapeDtypeStruct((1, DT_W), F32),
                   jax.ShapeDtypeStruct((1, DT_W), F32)],
        scratch_shapes=[pltpu.VMEM((_XS_GW, N_STATE), F32)],
        compiler_params=pltpu.CompilerParams(dimension_semantics=("arbitrary", "arbitrary")),
    )(act, act, act, dtr, dtb, alog, dsk, states, dy)


def _add_epilogue(acc, r):
    return r + acc


def _rms_and_skip(x, g):
    return _rms(x, g), x


def _local_step(x, p, tgt, w):
    s = x.shape[0]
    act_t, f32 = ACT_DTYPE, F32
    mm = functools.partial(_matmul)
    h = _rows_call("pre_norm", _f_rms, [(x, D_MODEL, 0)], [w["norm_g"]], [(D_MODEL, act_t)], tm=512, nrows=s)[0]
    proj = mm(h, w["w_main"], mode="nn", name="proj", out_dtype=act_t, m=s, n=MAIN_W, k=D_MODEL)
    dtr = mm(h, w["w_dt"], mode="nn", name="proj_dt", out_dtype=f32, m=s, n=DT_W, k=D_MODEL)
    a_pars = [w["ln_a_g"], w["ln_a_b"], w["w_s"], w["b_s"]]
    y_a = _rows_call("branch_a", _f_branch_a, [(proj, UVZ_W, UVZ_CB)], a_pars, [(E_A, act_t)], tm=256, nrows=s)[0]
    act = _conv_fwd(proj, w["conv_w"], w["conv_b"], tm=512, nrows=s)
    y, states = _ssd_fwd(act, dtr, w["dt_bias"], w["a_log"], w["d_skip"], nrows=s)
    gn_rows = [(y, D_INNER, 0), (proj, ZB_W, ZB_CB)]
    y_b = _rows_call("gnorm", _f_gnorm, gn_rows, [w["ssm_norm_g"]], [(D_INNER, act_t)], tm=512, nrows=s)[0]
    o_a = mm(y_a, w["w_oa"], mode="nn", name="out_a", out_dtype=act_t, m=s, n=D_MODEL, k=E_A)
    o_b = mm(y_b, w["w_ob"], mode="nn", name="out_b", out_dtype=act_t, m=s, n=D_MODEL, k=D_INNER)
    mg_rows = [(proj, G_W, G_CB), (o_a, D_MODEL, 0), (o_b, D_MODEL, 0)]
    merged = _rows_call("merge", _f_merge, mg_rows, [], [(D_MODEL, act_t)], tm=512, nrows=s)[0]
    x1 = mm(merged, w["w_out"], mode="nn", name="out_proj", out_dtype=f32, m=s, n=D_MODEL, k=D_MODEL,
            extras=(x,), epilogue=_add_epilogue)
    hp = _rows_call("ple_norm", _f_rms, [(x1, D_MODEL, 0)], [w["ple_norm_g"]], [(D_MODEL, act_t)], tm=512, nrows=s)[0]
    gp = mm(hp, w["w_pg"], mode="nn", name="ple_gate", out_dtype=f32, m=s, n=D_MODEL, k=D_MODEL)
    pe = mm(p, w["w_ple"], mode="nn", name="ple_proj", out_dtype=f32, m=s, n=D_MODEL, k=PLE_DIM)
    g = {}
    dx2, dgp, dpe, g["final_g"], loss = _head(x1, gp, pe, tgt, w["final_g"], tm=256, nrows=s)
    g["w_pg"] = mm(hp, dgp, mode="tn", name="d_w_pg", out_dtype=f32, m=D_MODEL, n=D_MODEL, k=s)
    g["w_ple"] = mm(p, dpe, mode="tn", name="d_w_ple", out_dtype=f32, m=PLE_DIM, n=D_MODEL, k=s)
    dhp = mm(dgp, w["w_pg"], mode="nt", name="d_hp", out_dtype=act_t, m=s, n=D_MODEL, k=D_MODEL)
    dx1, g["ple_norm_g"] = _rows_vjp_call(
        "ple_norm_bwd", _rms_and_skip, [(x1, D_MODEL, 0)], [w["ple_norm_g"]], [(dhp, D_MODEL, 0), (dx2, D_MODEL, 0)],
        [(f32, None)], tm=512, nrows=s)
    dmerged = mm(dx1, w["w_out"], mode="nt", name="d_merged", out_dtype=act_t, m=s, n=D_MODEL, k=D_MODEL)
    g["w_out"] = mm(merged, dx1, mode="tn", name="d_w_out", out_dtype=f32, m=D_MODEL, n=D_MODEL, k=s)
    dproj, do_a, do_b = _rows_vjp_call(
        "merge_bwd", _f_merge, mg_rows, [], [(dmerged, D_MODEL, 0)],
        [(act_t, (None, MAIN_W, G_CB)), (act_t, None), (act_t, None)], tm=512, nrows=s)
    dy_a = mm(do_a, w["w_oa"], mode="nt", name="d_y_a", out_dtype=act_t, m=s, n=E_A, k=D_MODEL)
    g["w_oa"] = mm(y_a, do_a, mode="tn", name="d_w_oa", out_dtype=f32, m=E_A, n=D_MODEL, k=s)
    dy_b = mm(do_b, w["w_ob"], mode="nt", name="d_y_b", out_dtype=act_t, m=s, n=D_INNER, k=D_MODEL)
    g["w_ob"] = mm(y_b, do_b, mode="tn", name="d_w_ob", out_dtype=f32, m=D_INNER, n=D_MODEL, k=s)
    dy, dproj, g["ssm_norm_g"] = _rows_vjp_call(
        "gnorm_bwd", _f_gnorm, gn_rows, [w["ssm_norm_g"]], [(dy_b, D_INNER, 0)],
        [(act_t, None), (act_t, (dproj, MAIN_W, ZB_CB))], tm=256, nrows=s)
    dxs, dbm, dcm, ddtr, g["dt_bias"], g["a_log"], g["d_skip"] = _ssd_bwd(
        act, dtr, w["dt_bias"], w["a_log"], w["d_skip"], states, dy, nrows=s)
    dpre, g["conv_w"], g["conv_b"] = _conv_bwd_pre(proj, w["conv_w"], w["conv_b"], (dxs, dbm, dcm), tm=512, nrows=s)
    dproj = _conv_bwd_x(dpre, w["conv_w"], dproj, tm=512, nrows=s)
    dproj, g["ln_a_g"], g["ln_a_b"], g["w_s"], g["b_s"] = _rows_vjp_call(
        "branch_a_bwd", _f_branch_a, [(proj, UVZ_W, UVZ_CB)], a_pars, [(dy_a, E_A, 0)],
        [(act_t, (dproj, MAIN_W, UVZ_CB))], tm=128, nrows=s)
    g["w_main"] = mm(h, dproj, mode="tn", name="d_w_main", out_dtype=f32, m=D_MODEL, n=MAIN_W, k=s)
    g["w_dt"] = mm(h, ddtr, mode="tn", name="d_w_dt", out_dtype=f32, m=D_MODEL, n=DT_W, k=s)
    dh = mm(dproj, w["w_main"], mode="nt", name="d_h_main", out_dtype=f32, m=s, n=D_MODEL, k=MAIN_W)
    dh = mm(ddtr, w["w_dt"], mode="nt", name="d_h", out_dtype=f32, m=s, n=D_MODEL, k=DT_W,
            extras=(dh,), epilogue=_add_epilogue)
    grad_x, g["norm_g"] = _rows_vjp_call(
        "pre_norm_bwd", _rms_and_skip, [(x, D_MODEL, 0)], [w["norm_g"]], [(dh, D_MODEL, 0), (dx1, D_MODEL, 0)],
        [(f32, None)], tm=512, nrows=s)
    return loss, grad_x, g


_O_ZB = 3 * E_A
_O_XBC = _O_ZB + D_INNER
_O_DT = _O_XBC + CONV_DIM
_O_G = _O_DT + N_HEADS


def _heads_to_lanes(v):
    r = v.shape[0]
    v = v.reshape(r, N_GROUPS, HEADS_PER_GROUP)
    return jnp.pad(v, ((0, 0), (0, 0), (0, 128 - HEADS_PER_GROUP))).reshape(r, DT_W)


def _lanes_to_heads(v):
    r = v.shape[0]
    return v.reshape(r, N_GROUPS, 128)[:, :, :HEADS_PER_GROUP].reshape(r, N_HEADS)


def _block_cols(blocks, a, b):
    parts = []
    for k in range(N_CHIPS):
        lo, hi = max(a, k * W_IN_BLOCK), min(b, (k + 1) * W_IN_BLOCK)
        if lo < hi:
            parts.append(blocks[k][:, lo - k * W_IN_BLOCK:hi - k * W_IN_BLOCK])
    return parts


_W_IN_SEGMENTS = ((0, _O_ZB, "m", 0), (_O_ZB, _O_XBC, "m", UVZ_W + XBC_W), (_O_XBC, _O_DT, "m", UVZ_W),
                  (_O_DT, _O_G, "d", 0), (_O_G, N_IN, "m", MAIN_W - G_W))


def _w_in_grad_blocks(gm, gdt):
    blocks = []
    for k in range(N_CHIPS):
        a, b = k * W_IN_BLOCK, (k + 1) * W_IN_BLOCK
        parts = []
        for s, e, src, off in _W_IN_SEGMENTS:
            lo, hi = max(a, s), min(b, e)
            if lo < hi:
                parts.append((gm if src == "m" else gdt)[:, off + lo - s:off + hi - s])
        blocks.append(jnp.concatenate(parts, axis=1))
    return jnp.stack(blocks)


def _layout_weights(f, w_in_blocks=None):
    w = dict(f)
    if w_in_blocks is None:
        w_in = w.pop("w_in")
        w_in_blocks = jnp.stack([w_in[:, k * W_IN_BLOCK:(k + 1) * W_IN_BLOCK] for k in range(N_CHIPS)])
    cols = functools.partial(_block_cols, w_in_blocks)
    w["w_main"] = jnp.concatenate(cols(0, _O_ZB) + cols(_O_XBC, _O_DT) + cols(_O_ZB, _O_XBC) + cols(_O_G, N_IN), axis=1)
    w["w_dt"] = _heads_to_lanes(jnp.concatenate(cols(_O_DT, _O_G), axis=1))
    w["b_s"] = f["b_s"].reshape(G_A, CHUNK, 1)
    for n in ("dt_bias", "a_log", "d_skip"):
        w[n] = _heads_to_lanes(f[n])
    return w


def _natural_grads(g):
    out = dict(g)
    gm = out.pop("w_main")
    gdt = _lanes_to_heads(out.pop("w_dt"))
    out["w_in"] = jnp.concatenate(
        [gm[:, :UVZ_W], gm[:, UVZ_W + XBC_W:UVZ_W + XBC_W + ZB_W], gm[:, UVZ_W:UVZ_W + XBC_W], gdt, gm[:, MAIN_W - G_W:]],
        axis=1)
    out["b_s"] = g["b_s"].reshape(G_A, CHUNK)
    for n in ("dt_bias", "a_log", "d_skip"):
        out[n] = _lanes_to_heads(g[n])
    return out


def _place():
    return lax.axis_index("x"), lax.axis_index("y"), lax.axis_index("c")


def _other_chips(x, y):
    return [(1 - x, y), (x, 1 - y), (1 - x, 1 - y)]


def _rcopy(src, dst, ssem, rsem, dev):
    return pltpu.make_async_remote_copy(src_ref=src, dst_ref=dst, send_sem=ssem, recv_sem=rsem,
                                        device_id=dev, device_id_type=MESH)


def _half(ref_rows, half):
    hs = ref_rows // 2
    return pl.ds(pl.multiple_of(half * hs, 16), hs)


def _gather_weights(shards, conv_shard):
    nw = len(shards)

    def body(*refs):
        sh, cv = refs[:nw], refs[nw]
        out, cvo = refs[nw + 1:2 * nw + 1], refs[2 * nw + 1]
        ici_s, ici_r, fw_s, fw_r, own_s, own_r, cv_s, cv_r = refs[2 * nw + 2:]
        x, y, c = _place()
        me, sib, chips = 2 * x + y, (x, y, 1 - c), _other_chips(x, y)
        own = [_rcopy(sh[w], out[w].at[me], own_s.at[w], own_r.at[w], sib) for w in range(nw)]
        own.append(_rcopy(cv, cvo.at[me], own_s.at[nw], own_r.at[nw], sib))
        for cp in own:
            cp.start()
        sends = []
        for w in range(nw):
            mine = _half(sh[w].shape[0], c)
            for j, chip in enumerate(chips):
                sends.append(_rcopy(sh[w].at[mine], out[w].at[me, mine], ici_s.at[3 * w + j], ici_r.at[3 * w + j], (*chip, c)))
        for j, chip in enumerate(chips):
            sends.append(_rcopy(cv, cvo.at[me], cv_s.at[j], cv_r.at[j], (*chip, c)))
        for cp in sends:
            cp.start()
        for w in range(nw):
            mine = _half(sh[w].shape[0], c)
            for j, chip in enumerate(chips):
                slab = out[w].at[2 * chip[0] + chip[1], mine]
                _rcopy(slab, slab, ici_s.at[3 * w + j], ici_r.at[3 * w + j], (*chip, c)).wait_recv()
                fwd = _rcopy(slab, slab, fw_s.at[3 * w + j], fw_r.at[3 * w + j], sib)
                fwd.start()
                sends.append(fwd)
        for j, chip in enumerate(chips):
            blk = cvo.at[2 * chip[0] + chip[1]]
            _rcopy(blk, blk, cv_s.at[j], cv_r.at[j], (*chip, c)).wait_recv()
        for w in range(nw):
            theirs = _half(sh[w].shape[0], 1 - c)
            for j, chip in enumerate(chips):
                slab = out[w].at[2 * chip[0] + chip[1], theirs]
                _rcopy(slab, slab, fw_s.at[3 * w + j], fw_r.at[3 * w + j], sib).wait_recv()
        for cp in sends:
            cp.wait_send()
        for cp in own:
            cp.wait()

    dma = pltpu.SemaphoreType.DMA
    return pl.pallas_call(
        body, name="gather_weights",
        in_specs=[ANY] * (nw + 1), out_specs=[ANY] * (nw + 1),
        out_shape=[jax.ShapeDtypeStruct((N_CHIPS,) + s.shape, s.dtype) for s in shards]
        + [jax.ShapeDtypeStruct((N_CHIPS,) + conv_shard.shape, conv_shard.dtype)],
        scratch_shapes=[dma((3 * nw,)), dma((3 * nw,)), dma((3 * nw,)), dma((3 * nw,)), dma((nw + 1,)), dma((nw + 1,)),
                        dma((3,)), dma((3,))],
    )(*shards, conv_shard)


def _swap_with_sibling(arrs):
    n = len(arrs)

    def body(*refs):
        src, dst, s_sem, r_sem = refs[:n], refs[n:2 * n], refs[2 * n], refs[2 * n + 1]
        x, y, c = _place()
        cps = [_rcopy(src[i], dst[i], s_sem.at[i], r_sem.at[i], (x, y, 1 - c)) for i in range(n)]
        for cp in cps:
            cp.start()
        for cp in cps:
            cp.wait()

    dma = pltpu.SemaphoreType.DMA
    return pl.pallas_call(
        body, name="swap_with_sibling", in_specs=[ANY] * n, out_specs=[ANY] * n,
        out_shape=[jax.ShapeDtypeStruct(a.shape, a.dtype) for a in arrs], scratch_shapes=[dma((n,)), dma((n,))],
    )(*arrs)


def _scatter_blocks(arrs):
    n = len(arrs)

    def body(*refs):
        src, dst, s_sem, r_sem = refs[:n], refs[n:2 * n], refs[2 * n], refs[2 * n + 1]
        x, y, c = _place()
        cps = []
        for i in range(n):
            for j, chip in enumerate(_other_chips(x, y)):
                cps.append(_rcopy(src[i].at[2 * chip[0] + chip[1]], dst[i].at[j], s_sem.at[3 * i + j], r_sem.at[3 * i + j],
                                  (*chip, c)))
        for cp in cps:
            cp.start()
        for cp in cps:
            cp.wait()

    dma = pltpu.SemaphoreType.DMA
    return pl.pallas_call(
        body, name="scatter_blocks", in_specs=[ANY] * n, out_specs=[ANY] * n,
        out_shape=[jax.ShapeDtypeStruct((3,) + a.shape[1:], a.dtype) for a in arrs],
        scratch_shapes=[dma((3 * n,)), dma((3 * n,))],
    )(*arrs)


def _share_halves(arrs):
    n = len(arrs)

    def body(*refs):
        buf, s_sem, r_sem = refs[n:2 * n], refs[2 * n], refs[2 * n + 1]
        x, y, c = _place()
        cps = []
        for i in range(n):
            mine = buf[i].at[_half(buf[i].shape[0], c)]
            cps.append(_rcopy(mine, mine, s_sem.at[i], r_sem.at[i], (x, y, 1 - c)))
        for cp in cps:
            cp.start()
        for i in range(n):
            theirs = buf[i].at[_half(buf[i].shape[0], 1 - c)]
            _rcopy(theirs, theirs, s_sem.at[i], r_sem.at[i], (x, y, 1 - c)).wait_recv()
        for cp in cps:
            cp.wait_send()

    dma = pltpu.SemaphoreType.DMA
    return pl.pallas_call(
        body, name="share_halves", in_specs=[ANY] * n, out_specs=[ANY] * n,
        out_shape=[jax.ShapeDtypeStruct(a.shape, a.dtype) for a in arrs],
        input_output_aliases={i: i for i in range(n)}, scratch_shapes=[dma((n,)), dma((n,))],
    )(*arrs)


def _allreduce_small(packed):
    rows = packed.shape[0]

    def body(p_ref, o_ref, buf, s_sem, r_sem):
        x, y, c = _place()
        me = 4 * x + 2 * y + c
        buf[me] = p_ref[...]
        cps = []
        for d in range(1, N_DEV):
            px, py, pc = ((1 - x) if d & 4 else x), ((1 - y) if d & 2 else y), ((1 - c) if d & 1 else c)
            cps.append(_rcopy(p_ref, buf.at[me], s_sem.at[d - 1], r_sem.at[d - 1], (px, py, pc)))
        for cp in cps:
            cp.start()
        for d in range(1, N_DEV):
            px, py, pc = ((1 - x) if d & 4 else x), ((1 - y) if d & 2 else y), ((1 - c) if d & 1 else c)
            landed = buf.at[4 * px + 2 * py + pc]
            _rcopy(p_ref, landed, s_sem.at[d - 1], r_sem.at[d - 1], (px, py, pc)).wait_recv()
        for cp in cps:
            cp.wait_send()
        acc = buf[0]
        for d in range(1, N_DEV):
            acc = acc + buf[d]
        o_ref[...] = acc

    dma = pltpu.SemaphoreType.DMA
    return pl.pallas_call(
        body, name="allreduce_small", out_shape=jax.ShapeDtypeStruct(packed.shape, F32),
        scratch_shapes=[pltpu.VMEM((N_DEV, rows, 128), F32), dma((N_DEV - 1,)), dma((N_DEV - 1,))],
    )(packed)


def _row_tile(rows, cols):
    tr = max(8, min(rows, (1 << 20) // (4 * cols) // 8 * 8))
    while rows % tr:
        tr -= 8
    return tr


def _chip_sum(name, g5, recv, c_arr):
    _, _, hs, cols = g5.shape
    tr = _row_tile(hs, cols)

    def body(_, a_ref, b_ref, o32_ref, ow_ref):
        s = a_ref[...] + b_ref[...].astype(F32)
        o32_ref[...] = s
        ow_ref[...] = s.astype(ow_ref.dtype)

    blk = pl.BlockSpec((None, tr, cols), lambda b, i, c: (b, i, 0))
    return pl.pallas_call(
        body, name=name,
        grid_spec=pltpu.PrefetchScalarGridSpec(
            num_scalar_prefetch=1, grid=(N_CHIPS, hs // tr),
            in_specs=[pl.BlockSpec((None, None, tr, cols), lambda b, i, c: (b, c[0], i, 0)), blk], out_specs=[blk, blk]),
        out_shape=[jax.ShapeDtypeStruct((N_CHIPS, hs, cols), F32), jax.ShapeDtypeStruct((N_CHIPS, hs, cols), WIRE_DTYPE)],
    )(c_arr, g5, recv)


def _final_sum(name, own, recv, place_arr):
    _, hs, cols = own.shape
    tr = _row_tile(hs, cols)
    nt = hs // tr

    def body(_, a_ref, r_ref, o_ref):
        o_ref[...] = ((a_ref[...] + r_ref[0].astype(F32)) + r_ref[1].astype(F32)) + r_ref[2].astype(F32)

    return pl.pallas_call(
        body, name=name,
        grid_spec=pltpu.PrefetchScalarGridSpec(
            num_scalar_prefetch=1, grid=(nt,),
            in_specs=[pl.BlockSpec((None, tr, cols), lambda i, m: (m[0], i, 0)),
                      pl.BlockSpec((3, tr, cols), lambda i, m: (0, i, 0))],
            out_specs=pl.BlockSpec((tr, cols), lambda i, m: (m[1] * nt + i, 0))),
        out_shape=jax.ShapeDtypeStruct((2 * hs, cols), F32),
    )(place_arr, own, recv)


def _adamw(w, g, m, v):
    m = ADAM_B1 * m + (1.0 - ADAM_B1) * g
    v = ADAM_B2 * v + (1.0 - ADAM_B2) * (g * g)
    m_hat = m / (1.0 - ADAM_B1 ** ADAM_STEP)
    v_hat = v / (1.0 - ADAM_B2 ** ADAM_STEP)
    return -ADAM_LR * (m_hat / (jnp.sqrt(v_hat) + ADAM_EPS) + ADAM_WD * w), m, v


def _adamw_call(name, w, g, m, v):
    rows, cols = w.shape
    tr = _row_tile(rows, cols)

    def body(w_ref, g_ref, m_ref, v_ref, d_ref, nm_ref, nv_ref):
        d_ref[...], nm_ref[...], nv_ref[...] = _adamw(w_ref[...], g_ref[...], m_ref[...], v_ref[...])

    blk = pl.BlockSpec((tr, cols), lambda i: (i, 0))
    return pl.pallas_call(
        body, name=name, grid=(rows // tr,), in_specs=[blk] * 4, out_specs=[blk] * 3,
        out_shape=[jax.ShapeDtypeStruct(w.shape, F32)] * 3,
        compiler_params=pltpu.CompilerParams(dimension_semantics=("parallel",)),
    )(w, g, m, v)


def _adamw_small(ws, gs, ms, vs):
    n = len(ws)

    def body(*refs):
        for i in range(n):
            w_ref, g_ref, m_ref, v_ref = (refs[k * n + i] for k in range(4))
            d, nm, nv = _adamw(w_ref[...], g_ref[...], m_ref[...], v_ref[...])
            refs[4 * n + i][...] = d
            refs[5 * n + i][...] = nm
            refs[6 * n + i][...] = nv

    out = pl.pallas_call(
        body, name="adamw_small", out_shape=[jax.ShapeDtypeStruct(a.shape, F32) for a in ws] * 3,
    )(*ws, *gs, *ms, *vs)
    return out[:n], out[n:2 * n], out[2 * n:]


_BIG = ("w_in", "w_oa", "w_ob", "w_out", "w_pg", "w_ple")
_SMALL = ("norm_g", "ln_a_g", "ln_a_b", "w_s", "b_s", "conv_w", "conv_b", "dt_bias", "a_log", "d_skip", "ssm_norm_g",
          "ple_norm_g", "final_g")
_WEIGHTS = ("norm_g", "w_in", "ln_a_g", "ln_a_b", "w_s", "b_s", "conv_w", "conv_b", "dt_bias", "a_log", "d_skip",
            "ssm_norm_g", "w_oa", "w_ob", "w_out", "ple_norm_g", "w_pg", "w_ple", "final_g")
_COL_SHARDED = ("w_in", "w_ple")
_PACK = 1024


def _blocks_to_full(col_sharded, blocks):
    if col_sharded:
        return jnp.concatenate([blocks[k] for k in range(N_CHIPS)], axis=1)
    return blocks.reshape(N_CHIPS * blocks.shape[1], blocks.shape[2])


def _full_to_blocks(col_sharded, full):
    if col_sharded:
        w = full.shape[1] // N_CHIPS
        return jnp.stack([full[:, k * w:(k + 1) * w] for k in range(N_CHIPS)])
    return full.reshape(N_CHIPS, full.shape[0] // N_CHIPS, full.shape[1])


def _two_d(n, a):
    if n == "w_s":
        return a.reshape(G_A * CHUNK, CHUNK)
    if n in ("b_s", "conv_w"):
        return a.reshape(a.shape[-2], a.shape[-1])
    return a.reshape(1, a.shape[-1])


def kernel(x, p, norm_g, w_in, ln_a_g, ln_a_b, w_s, b_s, conv_w, conv_b, dt_bias, a_log, d_skip, ssm_norm_g, w_oa, w_ob, w_out, ple_norm_g, w_pg, w_ple, final_g, loss_target, m_norm_g, m_w_in, m_ln_a_g, m_ln_a_b, m_w_s, m_b_s, m_conv_w, m_conv_b, m_dt_bias, m_a_log, m_d_skip, m_ssm_norm_g, m_w_oa, m_w_ob, m_w_out, m_ple_norm_g, m_w_pg, m_w_ple, m_final_g, v_norm_g, v_w_in, v_ln_a_g, v_ln_a_b, v_w_s, v_b_s, v_conv_w, v_conv_b, v_dt_bias, v_a_log, v_d_skip, v_ssm_norm_g, v_w_oa, v_w_ob, v_w_out, v_ple_norm_g, v_w_pg, v_w_ple, v_final_g):
    wt = dict(norm_g=norm_g, w_in=w_in, ln_a_g=ln_a_g, ln_a_b=ln_a_b, w_s=w_s, b_s=b_s, conv_w=conv_w, conv_b=conv_b,
              dt_bias=dt_bias, a_log=a_log, d_skip=d_skip, ssm_norm_g=ssm_norm_g, w_oa=w_oa, w_ob=w_ob, w_out=w_out,
              ple_norm_g=ple_norm_g, w_pg=w_pg, w_ple=w_ple, final_g=final_g)
    mom = dict(norm_g=m_norm_g, w_in=m_w_in, ln_a_g=m_ln_a_g, ln_a_b=m_ln_a_b, w_s=m_w_s, b_s=m_b_s, conv_w=m_conv_w,
               conv_b=m_conv_b, dt_bias=m_dt_bias, a_log=m_a_log, d_skip=m_d_skip, ssm_norm_g=m_ssm_norm_g, w_oa=m_w_oa,
               w_ob=m_w_ob, w_out=m_w_out, ple_norm_g=m_ple_norm_g, w_pg=m_w_pg, w_ple=m_w_ple, final_g=m_final_g)
    vel = dict(norm_g=v_norm_g, w_in=v_w_in, ln_a_g=v_ln_a_g, ln_a_b=v_ln_a_b, w_s=v_w_s, b_s=v_b_s, conv_w=v_conv_w,
               conv_b=v_conv_b, dt_bias=v_dt_bias, a_log=v_a_log, d_skip=v_d_skip, ssm_norm_g=v_ssm_norm_g, w_oa=v_w_oa,
               w_ob=v_w_ob, w_out=v_w_out, ple_norm_g=v_ple_norm_g, w_pg=v_w_pg, w_ple=v_w_ple, final_g=v_final_g)
    xi, yi, ci = _place()
    me = 2 * xi + yi
    c_arr = jnp.reshape(ci, (1,)).astype(jnp.int32)
    place_arr = jnp.stack([me, ci]).astype(jnp.int32)

    shard = {n: wt[n][0] for n in _BIG}
    blocks = _gather_weights([shard[n].astype(WIRE_DTYPE) for n in _BIG], conv_w[0])
    full = {n: _blocks_to_full(n in _COL_SHARDED, b) for n, b in zip(_BIG[1:], blocks[1:-1])}
    full["conv_w"] = _blocks_to_full(True, blocks[-1])
    for n in _SMALL:
        if n != "conv_w":
            full[n] = wt[n][0] if wt[n].ndim > 2 else wt[n].reshape(1, wt[n].shape[-1])

    loss_row, grad_x, g = _local_step(x[0], p[0, 0], loss_target[0], _layout_weights(full, w_in_blocks=blocks[0]))
    w_in_grad = _w_in_grad_blocks(g["w_main"], _lanes_to_heads(g["w_dt"]))
    g = _natural_grads(g)
    loss = lax.psum(loss_row[0, 0], ("x", "y", "c"))

    g5 = {}
    for n in _BIG:
        b = w_in_grad if n == "w_in" else _full_to_blocks(n in _COL_SHARDED, g[n])
        g5[n] = b.reshape(N_CHIPS, 2, b.shape[1] // 2, b.shape[2])
    to_sibling = [lax.dynamic_index_in_dim(g5[n], 1 - ci, axis=1, keepdims=False).astype(WIRE_DTYPE) for n in _BIG]
    from_sibling = _swap_with_sibling(to_sibling)
    chip32, chip_wire = zip(*[_chip_sum("chip_sum_" + n, g5[n], r, c_arr) for n, r in zip(_BIG, from_sibling)])
    from_chips = _scatter_blocks(list(chip_wire))
    halves = [_final_sum("final_sum_" + n, a, r, place_arr) for n, a, r in zip(_BIG, chip32, from_chips)]
    grads = dict(zip(_BIG, _share_halves(halves)))

    pieces = [_two_d(n, g[n]).reshape(-1) for n in _SMALL]
    sizes = [v.shape[0] for v in pieces]
    padded = [-(-s // _PACK) * _PACK for s in sizes]
    packed = jnp.concatenate([jnp.pad(v, (0, ps - s)) for v, s, ps in zip(pieces, sizes, padded)]).reshape(-1, 128)
    summed = _allreduce_small(packed).reshape(-1)
    off = 0
    for n, s, ps in zip(_SMALL, sizes, padded):
        grads[n] = _two_d(n, g[n]).shape, summed[off:off + s]
        off += ps
    for n in _SMALL:
        shape, flat = grads[n]
        grads[n] = flat.reshape(shape)
    grads["conv_w"] = lax.dynamic_slice_in_dim(grads["conv_w"], me * (CONV_DIM // N_CHIPS), CONV_DIM // N_CHIPS, axis=1)

    delta, new_m, new_v = {}, {}, {}
    for n in _BIG:
        delta[n], new_m[n], new_v[n] = _adamw_call("adamw_" + n, shard[n], grads[n], mom[n][0], vel[n][0])
    small = _adamw_small([_two_d(n, wt[n]) for n in _SMALL], [grads[n] for n in _SMALL],
                         [_two_d(n, mom[n]) for n in _SMALL], [_two_d(n, vel[n]) for n in _SMALL])
    for i, n in enumerate(_SMALL):
        delta[n], new_m[n], new_v[n] = small[0][i], small[1][i], small[2][i]

    def shaped(d):
        return [d[n].reshape(wt[n].shape) for n in _WEIGHTS]

    return (loss, grad_x[None], *shaped(grads), *shaped(delta), *shaped(new_m), *shaped(new_v))
```

```python
import functools

import jax
import jax.numpy as jnp
from jax import lax
from jax.experimental import pallas as pl
from jax.experimental.pallas import tpu as pltpu

F32 = jnp.float32
MXU_DTYPE = jnp.bfloat16
ACT_DTYPE = jnp.bfloat16
WIRE_DTYPE = jnp.bfloat16

D_MODEL = 1024
PLE_DIM = 256
CHUNK = 128
EPS = 1e-6
E_A = D_MODEL
G_A = 4
D_INNER = 2 * D_MODEL
HEAD_DIM = 64
N_HEADS = D_INNER // HEAD_DIM
N_STATE = 128
N_GROUPS = 4
HEADS_PER_GROUP = N_HEADS // N_GROUPS
PAIRS_PER_GROUP = HEADS_PER_GROUP // 2
CONV_K = 4
CONV_DIM = D_INNER + 2 * N_GROUPS * N_STATE
N_IN = 3 * E_A + D_INNER + CONV_DIM + N_HEADS + 2 * D_MODEL
N_CHIPS = 4
N_DEV = 8
W_IN_BLOCK = N_IN // N_CHIPS

UVZ_W, XBC_W, ZB_W, G_W = 3 * E_A, CONV_DIM, D_INNER, 2 * D_MODEL
MAIN_W = UVZ_W + XBC_W + ZB_W + G_W
UVZ_CB, XBC_CB, ZB_CB, G_CB = 0, 1, 3, 4
DT_W = N_GROUPS * 128

ADAM_LR, ADAM_B1, ADAM_B2, ADAM_EPS, ADAM_WD, ADAM_STEP = 0.001, 0.9, 0.999, 1e-08, 0.01, 10

MESH = pl.DeviceIdType.MESH
ANY = pl.BlockSpec(memory_space=pl.ANY)


def _mxu(v):
    return v.astype(MXU_DTYPE)


def _dot(a, b, dims=(((1,), (0,)), ((), ()))):
    return lax.dot_general(_mxu(a), _mxu(b), dims, preferred_element_type=F32)


def _matmul(a, b, *, mode, name, out_dtype, m, n, k, tm=1024, tn=1024, tk=1024, a_off=0, b_off=0,
            extras=(), epilogue=None):
    tm, tn, tk = min(tm, m), min(tn, n), min(tk, k)
    assert m % tm == 0 and n % tn == 0 and k % tk == 0, (name, m, n, k, tm, tn, tk)
    nk = k // tk
    if mode == "nn":
        assert a_off % tk == 0 and b_off % tn == 0
        a_spec = pl.BlockSpec((tm, tk), lambda i, j, kk: (i, kk + a_off // tk))
        b_spec = pl.BlockSpec((tk, tn), lambda i, j, kk: (kk, j + b_off // tn))
        dims = (((1,), (0,)), ((), ()))
    elif mode == "nt":
        a_spec = pl.BlockSpec((tm, tk), lambda i, j, kk: (i, kk))
        b_spec = pl.BlockSpec((tn, tk), lambda i, j, kk: (j, kk))
        dims = (((1,), (1,)), ((), ()))
    else:
        assert a_off % tm == 0 and b_off % tn == 0
        a_spec = pl.BlockSpec((tk, tm), lambda i, j, kk: (kk, i + a_off // tm))
        b_spec = pl.BlockSpec((tk, tn), lambda i, j, kk: (kk, j + b_off // tn))
        dims = (((0,), (0,)), ((), ()))
    ne = len(extras)

    def finish(acc, extra_refs, o_ref):
        res = acc if epilogue is None else epilogue(acc, *[e[...] for e in extra_refs])
        o_ref[...] = res.astype(o_ref.dtype)

    def body(a_ref, b_ref, *rest):
        extra_refs, o_ref = rest[:ne], rest[ne]
        part = _dot(a_ref[...], b_ref[...], dims)
        if nk == 1:
            finish(part, extra_refs, o_ref)
            return
        acc_ref = rest[ne + 1]
        kk = pl.program_id(2)

        @pl.when(kk == 0)
        def _():
            acc_ref[...] = part

        @pl.when(kk > 0)
        def _():
            acc_ref[...] += part

        @pl.when(kk == nk - 1)
        def _():
            finish(acc_ref[...], extra_refs, o_ref)

    o_spec = pl.BlockSpec((tm, tn), lambda i, j, kk: (i, j))
    return pl.pallas_call(
        body, name=name, grid=(m // tm, n // tn, nk),
        in_specs=[a_spec, b_spec] + [o_spec] * ne, out_specs=o_spec,
        out_shape=jax.ShapeDtypeStruct((m, n), out_dtype),
        scratch_shapes=[pltpu.VMEM((tm, tn), F32)] if nk > 1 else [],
        compiler_params=pltpu.CompilerParams(dimension_semantics=("parallel", "parallel", "arbitrary")),
    )(a, b, *extras)


def _row_spec(tm, width, cb):
    return pl.BlockSpec((tm, width), lambda i: (i, cb))


def _whole_spec(shape):
    nd = len(shape)
    return pl.BlockSpec(tuple(shape), lambda i: (0,) * nd)


def _rows_call(name, f, rows, pars, outs, *, tm, nrows):
    tm = min(tm, nrows)
    nr, npar = len(rows), len(pars)

    def body(*refs):
        rv = [r[...].astype(F32) for r in refs[:nr]]
        pv = [p[...] for p in refs[nr:nr + npar]]
        res = f(*rv, *pv)
        for o_ref, r in zip(refs[nr + npar:], res):
            o_ref[...] = r.astype(o_ref.dtype)

    return pl.pallas_call(
        body, name=name, grid=(nrows // tm,),
        in_specs=[_row_spec(tm, w, cb) for _, w, cb in rows] + [_whole_spec(p.shape) for p in pars],
        out_specs=[_row_spec(tm, w, 0) for w, _ in outs],
        out_shape=[jax.ShapeDtypeStruct((nrows, w), dt) for w, dt in outs],
        compiler_params=pltpu.CompilerParams(dimension_semantics=("parallel",)),
    )(*[r[0] for r in rows], *pars)


def _rows_vjp_call(name, f, rows, pars, cots, drows, *, tm, nrows):
    tm = min(tm, nrows)
    nr, npar, nc = len(rows), len(pars), len(cots)
    alias_bufs, aliases = [], {}
    out_shape, out_specs = [], []
    for (arr, w, cb), d in zip(rows, drows):
        if d is None:
            continue
        dt, into = d
        if into is None:
            out_shape.append(jax.ShapeDtypeStruct((nrows, w), dt))
            out_specs.append(_row_spec(tm, w, 0))
        else:
            buf, total, ocb = into
            if buf is not None:
                aliases[nr + npar + nc + len(alias_bufs)] = len(out_shape)
                alias_bufs.append(buf)
            out_shape.append(jax.ShapeDtypeStruct((nrows, total), dt))
            out_specs.append(_row_spec(tm, w, ocb))
    n_drow = len(out_shape)
    for p in pars:
        out_shape.append(jax.ShapeDtypeStruct(p.shape, F32))
        out_specs.append(_whole_spec(p.shape))
    na = len(alias_bufs)

    def body(*refs):
        rv = [r[...].astype(F32) for r in refs[:nr]]
        pv = [p[...] for p in refs[nr:nr + npar]]
        cv = tuple(c[...].astype(F32) for c in refs[nr + npar:nr + npar + nc])
        o_refs = refs[nr + npar + nc + na:]
        _, vjp = jax.vjp(f, *rv, *pv)
        g = vjp(cv)
        oi = 0
        for ri, d in enumerate(drows):
            if d is not None:
                o_refs[oi][...] = g[ri].astype(o_refs[oi].dtype)
                oi += 1
        first = pl.program_id(0) == 0
        for pi in range(npar):
            acc = o_refs[n_drow + pi]

            @pl.when(first)
            def _(acc=acc):
                acc[...] = jnp.zeros_like(acc)

            acc[...] += g[nr + pi]

    return pl.pallas_call(
        body, name=name, grid=(nrows // tm,),
        in_specs=[_row_spec(tm, w, cb) for _, w, cb in rows] + [_whole_spec(p.shape) for p in pars]
        + [_row_spec(tm, w, cb) for _, w, cb in cots] + [ANY] * na,
        out_specs=out_specs, out_shape=out_shape, input_output_aliases=aliases,
        compiler_params=pltpu.CompilerParams(dimension_semantics=("arbitrary",)),
    )(*[r[0] for r in rows], *pars, *[c[0] for c in cots], *alias_bufs)


def _rms(x, g):
    return x * lax.rsqrt(jnp.mean(x * x, axis=-1, keepdims=True) + EPS) * g


def _f_rms(x, g):
    return (_rms(x, g),)


def _tril_mask():
    return lax.broadcasted_iota(jnp.int32, (CHUNK, CHUNK), 0) >= lax.broadcasted_iota(jnp.int32, (CHUNK, CHUNK), 1)


def _f_branch_a(uvz, ln_g, ln_b, w_s, b_s):
    u = jax.nn.gelu(uvz[:, :E_A])
    v = jax.nn.gelu(uvz[:, E_A:2 * E_A])
    z = uvz[:, 2 * E_A:]
    xc = v - jnp.mean(v, axis=-1, keepdims=True)
    vn = xc * lax.rsqrt(jnp.mean(xc * xc, axis=-1, keepdims=True) + EPS) * ln_g + ln_b
    mask = _tril_mask()
    ws = [jnp.where(mask, w_s[g], 0.0) for g in range(G_A)]
    gw = E_A // G_A
    rows = []
    for c in range(uvz.shape[0] // CHUNK):
        vc = vn[c * CHUNK:(c + 1) * CHUNK]
        rows.append(jnp.concatenate([_dot(ws[g], vc[:, g * gw:(g + 1) * gw]) + b_s[g] for g in range(G_A)], axis=1))
    sv = rows[0] if len(rows) == 1 else jnp.concatenate(rows, axis=0)
    return (u * sv * jax.nn.silu(z),)


def _f_gnorm(y, zb, g):
    yz = y * jax.nn.silu(zb)
    gw = D_INNER // N_GROUPS
    parts = []
    for i in range(N_GROUPS):
        s = yz[:, i * gw:(i + 1) * gw]
        parts.append(s * lax.rsqrt(jnp.mean(s * s, axis=-1, keepdims=True) + EPS))
    return (jnp.concatenate(parts, axis=1) * g,)


def _f_merge(g2, oa, ob):
    return (jax.nn.sigmoid(g2[:, :D_MODEL]) * oa + jax.nn.sigmoid(g2[:, D_MODEL:]) * ob,)


def _f_loss(x1, gp, pe, tgt, fg):
    x2 = x1 + jax.nn.sigmoid(gp) * pe
    err = _rms(x2, fg) - tgt
    return 0.5 * jnp.sum(jnp.mean(err * err, axis=-1))


def _head(x1, gp, pe, tgt, fg, *, tm, nrows):
    tm = min(tm, nrows)

    def body(x1_ref, gp_ref, pe_ref, t_ref, fg_ref, dx_ref, dgp_ref, dpe_ref, dfg_ref, loss_ref):
        loss, vjp = jax.vjp(_f_loss, x1_ref[...], gp_ref[...], pe_ref[...], t_ref[...], fg_ref[...])
        dx, dgp, dpe, _, dfg = vjp(jnp.ones((), F32))
        dx_ref[...] = dx
        dgp_ref[...] = dgp.astype(dgp_ref.dtype)
        dpe_ref[...] = dpe.astype(dpe_ref.dtype)

        @pl.when(pl.program_id(0) == 0)
        def _():
            dfg_ref[...] = jnp.zeros_like(dfg_ref)
            loss_ref[...] = jnp.zeros_like(loss_ref)

        dfg_ref[...] += dfg
        loss_ref[...] += jnp.full(loss_ref.shape, loss, F32)

    row = _row_spec(tm, D_MODEL, 0)
    return pl.pallas_call(
        body, name="head", grid=(nrows // tm,),
        in_specs=[row, row, row, row, _whole_spec((1, D_MODEL))],
        out_specs=[row, row, row, _whole_spec((1, D_MODEL)), _whole_spec((1, 128))],
        out_shape=[jax.ShapeDtypeStruct((nrows, D_MODEL), F32), jax.ShapeDtypeStruct((nrows, D_MODEL), ACT_DTYPE),
                   jax.ShapeDtypeStruct((nrows, D_MODEL), ACT_DTYPE), jax.ShapeDtypeStruct((1, D_MODEL), F32),
                   jax.ShapeDtypeStruct((1, 128), F32)],
        compiler_params=pltpu.CompilerParams(dimension_semantics=("arbitrary",)),
    )(x1, gp, pe, tgt, fg)


def _shift_rows(cur, edge, j, up):
    tm = cur.shape[0]
    row = lax.broadcasted_iota(jnp.int32, cur.shape, 0)
    if up:
        sh = pltpu.roll(cur, tm - j, 0)
        e = jnp.tile(pltpu.roll(edge, 8 - j, 0), (tm // 8, 1))
        return jnp.where(row >= tm - j, e, sh)
    sh = pltpu.roll(cur, j, 0)
    e = jnp.tile(pltpu.roll(edge, j, 0), (tm // 8, 1))
    return jnp.where(row < j, e, sh)


def _conv_pre(cur, prev, w, b):
    acc = cur * w[CONV_K - 1:CONV_K] + b
    taps = [cur]
    for j in range(1, CONV_K):
        s = _shift_rows(cur, prev, j, up=False)
        taps.append(s)
        acc = acc + s * w[CONV_K - 1 - j:CONV_K - j]
    return acc, taps


def _halo_specs(tm, nrows, cb, before):
    nb = tm // 8
    last = nrows // 8 - 1
    if before:
        return pl.BlockSpec((8, XBC_W), lambda i: (jnp.maximum(i * nb - 1, 0), cb))
    return pl.BlockSpec((8, XBC_W), lambda i: (jnp.minimum((i + 1) * nb, last), cb))


def _conv_fwd(proj, conv_w, conv_b, *, tm, nrows):
    tm = min(tm, nrows)

    def body(cur_ref, prev_ref, w_ref, b_ref, o_ref):
        prev = jnp.where(pl.program_id(0) == 0, 0.0, prev_ref[...].astype(F32))
        pre, _ = _conv_pre(cur_ref[...].astype(F32), prev, w_ref[...], b_ref[...])
        o_ref[...] = jax.nn.silu(pre).astype(o_ref.dtype)

    return pl.pallas_call(
        body, name="conv_fwd", grid=(nrows // tm,),
        in_specs=[_row_spec(tm, XBC_W, XBC_CB), _halo_specs(tm, nrows, XBC_CB, True),
                  _whole_spec((CONV_K, XBC_W)), _whole_spec((1, XBC_W))],
        out_specs=_row_spec(tm, XBC_W, 0), out_shape=jax.ShapeDtypeStruct((nrows, XBC_W), ACT_DTYPE),
        compiler_params=pltpu.CompilerParams(dimension_semantics=("parallel",)),
    )(proj, proj, conv_w, conv_b)


def _conv_bwd_pre(proj, conv_w, conv_b, dact, *, tm, nrows):
    tm = min(tm, nrows)
    nb = N_GROUPS * N_STATE

    def body(cur_ref, prev_ref, w_ref, b_ref, dxs_ref, dbm_ref, dcm_ref, dpre_ref, dw_ref, db_ref):
        prev = jnp.where(pl.program_id(0) == 0, 0.0, prev_ref[...].astype(F32))
        pre, taps = _conv_pre(cur_ref[...].astype(F32), prev, w_ref[...], b_ref[...])
        sg = jax.nn.sigmoid(pre)
        dy = jnp.concatenate([dxs_ref[...], dbm_ref[...], dcm_ref[...]], axis=1).astype(F32)
        dpre = dy * sg * (1.0 + pre * (1.0 - sg))
        dpre_ref[...] = dpre.astype(dpre_ref.dtype)

        @pl.when(pl.program_id(0) == 0)
        def _():
            dw_ref[...] = jnp.zeros_like(dw_ref)
            db_ref[...] = jnp.zeros_like(db_ref)

        db_ref[...] += jnp.sum(dpre, axis=0, keepdims=True)
        for j in range(CONV_K):
            k = CONV_K - 1 - j
            dw_ref[k:k + 1, :] += jnp.sum(dpre * taps[j], axis=0, keepdims=True)

    return pl.pallas_call(
        body, name="conv_bwd_pre", grid=(nrows // tm,),
        in_specs=[_row_spec(tm, XBC_W, XBC_CB), _halo_specs(tm, nrows, XBC_CB, True),
                  _whole_spec((CONV_K, XBC_W)), _whole_spec((1, XBC_W)),
                  _row_spec(tm, D_INNER, 0), _row_spec(tm, nb, 0), _row_spec(tm, nb, 0)],
        out_specs=[_row_spec(tm, XBC_W, 0), _whole_spec((CONV_K, XBC_W)), _whole_spec((1, XBC_W))],
        out_shape=[jax.ShapeDtypeStruct((nrows, XBC_W), ACT_DTYPE), jax.ShapeDtypeStruct((CONV_K, XBC_W), F32),
                   jax.ShapeDtypeStruct((1, XBC_W), F32)],
        compiler_params=pltpu.CompilerParams(dimension_semantics=("arbitrary",)),
    )(proj, proj, conv_w, conv_b, *dact)


def _conv_bwd_x(dpre, conv_w, dproj, *, tm, nrows):
    tm = min(tm, nrows)
    ntiles = nrows // tm

    def body(cur_ref, nxt_ref, w_ref, _, o_ref):
        cur = cur_ref[...].astype(F32)
        nxt = jnp.where(pl.program_id(0) == ntiles - 1, 0.0, nxt_ref[...].astype(F32))
        w = w_ref[...]
        acc = cur * w[CONV_K - 1:CONV_K]
        for j in range(1, CONV_K):
            acc = acc + _shift_rows(cur, nxt, j, up=True) * w[CONV_K - 1 - j:CONV_K - j]
        o_ref[...] = acc.astype(o_ref.dtype)

    return pl.pallas_call(
        body, name="conv_bwd_x", grid=(ntiles,),
        in_specs=[_row_spec(tm, XBC_W, 0), _halo_specs(tm, nrows, 0, False), _whole_spec((CONV_K, XBC_W)), ANY],
        out_specs=_row_spec(tm, XBC_W, XBC_CB), out_shape=jax.ShapeDtypeStruct(dproj.shape, dproj.dtype),
        input_output_aliases={3: 0},
        compiler_params=pltpu.CompilerParams(dimension_semantics=("parallel",)),
    )(dpre, dpre, conv_w, dproj)


SSD_SPAN = 4
_XS_GW = D_INNER // N_GROUPS
_NT = (((1,), (1,)), ((), ()))
_TN = (((0,), (0,)), ((), ()))


def _bf16_terms(x, terms):
    parts, rest = [], x
    for _ in range(terms):
        part = rest.astype(jnp.bfloat16)
        parts.append(part)
        rest = rest - part.astype(F32)
    return parts


def _head_lane_matrix():
    return (lax.broadcasted_iota(jnp.int32, (128, _XS_GW), 0)
            == lax.broadcasted_iota(jnp.int32, (128, _XS_GW), 1) // HEAD_DIM).astype(jnp.bfloat16)


@functools.partial(jax.custom_vjp, nondiff_argnums=(1,))
def _head_lanes(cols, terms):
    e = _head_lane_matrix()
    return sum(jnp.dot(t, e, preferred_element_type=F32) for t in _bf16_terms(cols, terms))


def _head_lanes_fwd(cols, terms):
    return _head_lanes(cols, terms), None


def _head_lanes_bwd(terms, _, g):
    e = _head_lane_matrix()
    return (sum(lax.dot_general(t, e, _NT, preferred_element_type=F32) for t in _bf16_terms(g, 2)),)


_head_lanes.defvjp(_head_lanes_fwd, _head_lanes_bwd)


def _ssd_chunk(k, xs, bm, cm, dtr, hprev, dtb, alog, dsk):
    causal, tri, lo = k
    dt = jax.nn.softplus(dtr + dtb)
    da = dt * (-jnp.exp(alog))
    cs = jnp.dot(tri, da, precision=lax.Precision.HIGHEST, preferred_element_type=F32)
    cst = cs.T
    cs_l = _head_lanes(cs, 3)
    xdt = xs * _head_lanes(dt, 2)
    cb = _dot(cm, bm, _NT)
    yd = []
    for q in range(PAIRS_PER_GROUP):
        xq = xdt[:, 128 * q:128 * (q + 1)]
        y2 = [_dot(cb * jnp.exp(jnp.where(causal, cs[:, h:h + 1] - cst[h:h + 1, :], -jnp.inf)), xq)
              for h in (2 * q, 2 * q + 1)]
        yd.append(jnp.where(lo, y2[0], y2[1]))
    y_off = jnp.exp(cs_l) * _dot(cm, hprev, _NT)
    st = _dot(xdt * jnp.exp(cs_l[CHUNK - 1:CHUNK, :] - cs_l), bm, _TN)
    cdec = jnp.exp(cs[CHUNK - 1:CHUNK, :])
    cd_rows = jnp.concatenate(
        [jnp.broadcast_to(cdec[:, h:h + 1], (HEAD_DIM, N_STATE)) for h in range(HEADS_PER_GROUP)], axis=0)
    dsk_l = _head_lanes(jnp.broadcast_to(dsk, (8, 128)), 2)[:1]
    y = jnp.concatenate(yd, axis=1) + y_off + xs * dsk_l
    return y, cd_rows * hprev + st


def _ssd_span(xs, bm, cm, dtr, h0, dtb, alog, dsk):
    li = lax.broadcasted_iota(jnp.int32, (CHUNK, CHUNK), 0)
    si = lax.broadcasted_iota(jnp.int32, (CHUNK, CHUNK), 1)
    causal = li >= si
    k = (causal, causal.astype(F32), si < HEAD_DIM)
    h, ys = h0, []
    for t in range(xs.shape[0] // CHUNK):
        r = slice(t * CHUNK, (t + 1) * CHUNK)
        y, h = _ssd_chunk(k, xs[r], bm[r], cm[r], dtr[r], h, dtb, alog, dsk)
        ys.append(y)
    return (ys[0] if len(ys) == 1 else jnp.concatenate(ys, axis=0)), h


def _ssd_specs(rev, nsteps, rows):
    def s_of(s):
        return nsteps - 1 - s if rev else s

    xs = pl.BlockSpec((rows, _XS_GW), lambda g, s: (s_of(s), g))
    bm = pl.BlockSpec((rows, N_STATE), lambda g, s: (s_of(s), D_INNER // N_STATE + g))
    cm = pl.BlockSpec((rows, N_STATE), lambda g, s: (s_of(s), D_INNER // N_STATE + N_GROUPS + g))
    dt = pl.BlockSpec((rows, 128), lambda g, s: (s_of(s), g))
    par = pl.BlockSpec((1, 128), lambda g, s: (0, g))
    st = pl.BlockSpec((None, None, _XS_GW, N_STATE), lambda g, s: (g, s_of(s), 0, 0))
    return xs, bm, cm, dt, par, st


def _ssd_fwd(act, dtr, dtb, alog, dsk, *, nrows):
    rows = CHUNK * min(SSD_SPAN, nrows // CHUNK)
    nsteps = nrows // rows
    xs, bm, cm, dt, par, st = _ssd_specs(False, nsteps, rows)

    def body(xs_ref, b_ref, c_ref, dt_ref, dtb_ref, al_ref, dk_ref, y_ref, st_ref, h_ref):
        @pl.when(pl.program_id(1) == 0)
        def _():
            h_ref[...] = jnp.zeros_like(h_ref)

        h0 = h_ref[...]
        st_ref[...] = h0
        y, hnew = _ssd_span(xs_ref[...].astype(F32), b_ref[...].astype(F32), c_ref[...].astype(F32), dt_ref[...],
                            h0, dtb_ref[...], al_ref[...], dk_ref[...])
        y_ref[...] = y.astype(y_ref.dtype)
        h_ref[...] = hnew

    return pl.pallas_call(
        body, name="ssd_fwd", grid=(N_GROUPS, nsteps),
        in_specs=[xs, bm, cm, dt, par, par, par], out_specs=[xs, st],
        out_shape=[jax.ShapeDtypeStruct((nrows, D_INNER), ACT_DTYPE),
                   jax.ShapeDtypeStruct((N_GROUPS, nsteps, _XS_GW, N_STATE), F32)],
        scratch_shapes=[pltpu.VMEM((_XS_GW, N_STATE), F32)],
        compiler_params=pltpu.CompilerParams(dimension_semantics=("arbitrary", "arbitrary")),
    )(act, act, act, dtr, dtb, alog, dsk)


def _ssd_bwd(act, dtr, dtb, alog, dsk, states, dy, *, nrows):
    rows = CHUNK * min(SSD_SPAN, nrows // CHUNK)
    nsteps = nrows // rows
    xs, bm, cm, dt, par, st = _ssd_specs(True, nsteps, rows)

    def body(xs_ref, b_ref, c_ref, dt_ref, dtb_ref, al_ref, dk_ref, st_ref, dy_ref,
             dxs_ref, db_ref, dc_ref, ddt_ref, ddtb_ref, dal_ref, ddk_ref, dh_ref):
        @pl.when(pl.program_id(1) == 0)
        def _():
            dh_ref[...] = jnp.zeros_like(dh_ref)
            ddtb_ref[...] = jnp.zeros_like(ddtb_ref)
            dal_ref[...] = jnp.zeros_like(dal_ref)
            ddk_ref[...] = jnp.zeros_like(ddk_ref)

        _, vjp = jax.vjp(_ssd_span, xs_ref[...].astype(F32), b_ref[...].astype(F32), c_ref[...].astype(F32),
                         dt_ref[...], st_ref[...], dtb_ref[...], al_ref[...], dk_ref[...])
        dxs, db, dc, ddt, dh, ddtb, dal, ddk = vjp((dy_ref[...].astype(F32), dh_ref[...]))
        dxs_ref[...] = dxs.astype(dxs_ref.dtype)
        db_ref[...] = db.astype(db_ref.dtype)
        dc_ref[...] = dc.astype(dc_ref.dtype)
        ddt_ref[...] = ddt
        dh_ref[...] = dh
        ddtb_ref[...] += ddtb
        dal_ref[...] += dal
        ddk_ref[...] += ddk

    nb = N_GROUPS * N_STATE
    bspec = pl.BlockSpec((rows, N_STATE), lambda g, s: (nsteps - 1 - s, g))
    return pl.pallas_call(
        body, name="ssd_bwd", grid=(N_GROUPS, nsteps),
        in_specs=[xs, bm, cm, dt, par, par, par, st, xs],
        out_specs=[xs, bspec, bspec, dt, par, par, par],
        out_shape=[jax.ShapeDtypeStruct((nrows, D_INNER), ACT_DTYPE), jax.ShapeDtypeStruct((nrows, nb), ACT_DTYPE),
                   jax.ShapeDtypeStruct((nrows, nb), ACT_DTYPE), jax.ShapeDtypeStruct((nrows, DT_W), F32),
                   jax.ShapeDtypeStruct((1, DT_W), F32), jax.ShapeDtypeStruct((1, DT_W), F32),
                   jax.ShapeDtypeStruct((1, DT_W), F32)],
        scratch_shapes=[pltpu.VMEM((_XS_GW, N_STATE), F32)],
        compiler_params=pltpu.CompilerParams(dimension_semantics=("arbitrary", "arbitrary")),
    )(act, act, act, dtr, dtb, alog, dsk, states, dy)


def _add_epilogue(acc, r):
    return r + acc


def _rms_and_skip(x, g):
    return _rms(x, g), x


def _local_step(x, p, tgt, w):
    s = x.shape[0]
    act_t, f32 = ACT_DTYPE, F32
    mm = functools.partial(_matmul)
    h = _rows_call("pre_norm", _f_rms, [(x, D_MODEL, 0)], [w["norm_g"]], [(D_MODEL, act_t)], tm=512, nrows=s)[0]
    proj = mm(h, w["w_main"], mode="nn", name="proj", out_dtype=act_t, m=s, n=MAIN_W, k=D_MODEL)
    dtr = mm(h, w["w_dt"], mode="nn", name="proj_dt", out_dtype=f32, m=s, n=DT_W, k=D_MODEL)
    a_pars = [w["ln_a_g"], w["ln_a_b"], w["w_s"], w["b_s"]]
    y_a = _rows_call("branch_a", _f_branch_a, [(proj, UVZ_W, UVZ_CB)], a_pars, [(E_A, act_t)], tm=256, nrows=s)[0]
    act = _conv_fwd(proj, w["conv_w"], w["conv_b"], tm=512, nrows=s)
    y, states = _ssd_fwd(act, dtr, w["dt_bias"], w["a_log"], w["d_skip"], nrows=s)
    gn_rows = [(y, D_INNER, 0), (proj, ZB_W, ZB_CB)]
    y_b = _rows_call("gnorm", _f_gnorm, gn_rows, [w["ssm_norm_g"]], [(D_INNER, act_t)], tm=512, nrows=s)[0]
    o_a = mm(y_a, w["w_oa"], mode="nn", name="out_a", out_dtype=act_t, m=s, n=D_MODEL, k=E_A)
    o_b = mm(y_b, w["w_ob"], mode="nn", name="out_b", out_dtype=act_t, m=s, n=D_MODEL, k=D_INNER)
    mg_rows = [(proj, G_W, G_CB), (o_a, D_MODEL, 0), (o_b, D_MODEL, 0)]
    merged = _rows_call("merge", _f_merge, mg_rows, [], [(D_MODEL, act_t)], tm=512, nrows=s)[0]
    x1 = mm(merged, w["w_out"], mode="nn", name="out_proj", out_dtype=f32, m=s, n=D_MODEL, k=D_MODEL,
            extras=(x,), epilogue=_add_epilogue)
    hp = _rows_call("ple_norm", _f_rms, [(x1, D_MODEL, 0)], [w["ple_norm_g"]], [(D_MODEL, act_t)], tm=512, nrows=s)[0]
    gp = mm(hp, w["w_pg"], mode="nn", name="ple_gate", out_dtype=f32, m=s, n=D_MODEL, k=D_MODEL)
    pe = mm(p, w["w_ple"], mode="nn", name="ple_proj", out_dtype=f32, m=s, n=D_MODEL, k=PLE_DIM)
    g = {}
    dx2, dgp, dpe, g["final_g"], loss = _head(x1, gp, pe, tgt, w["final_g"], tm=256, nrows=s)
    g["w_pg"] = mm(hp, dgp, mode="tn", name="d_w_pg", out_dtype=f32, m=D_MODEL, n=D_MODEL, k=s)
    g["w_ple"] = mm(p, dpe, mode="tn", name="d_w_ple", out_dtype=f32, m=PLE_DIM, n=D_MODEL, k=s)
    dhp = mm(dgp, w["w_pg"], mode="nt", name="d_hp", out_dtype=act_t, m=s, n=D_MODEL, k=D_MODEL)
    dx1, g["ple_norm_g"] = _rows_vjp_call(
        "ple_norm_bwd", _rms_and_skip, [(x1, D_MODEL, 0)], [w["ple_norm_g"]], [(dhp, D_MODEL, 0), (dx2, D_MODEL, 0)],
        [(f32, None)], tm=512, nrows=s)
    dmerged = mm(dx1, w["w_out"], mode="nt", name="d_merged", out_dtype=act_t, m=s, n=D_MODEL, k=D_MODEL)
    g["w_out"] = mm(merged, dx1, mode="tn", name="d_w_out", out_dtype=f32, m=D_MODEL, n=D_MODEL, k=s)
    dproj, do_a, do_b = _rows_vjp_call(
        "merge_bwd", _f_merge, mg_rows, [], [(dmerged, D_MODEL, 0)],
        [(act_t, (None, MAIN_W, G_CB)), (act_t, None), (act_t, None)], tm=512, nrows=s)
    dy_a = mm(do_a, w["w_oa"], mode="nt", name="d_y_a", out_dtype=act_t, m=s, n=E_A, k=D_MODEL)
    g["w_oa"] = mm(y_a, do_a, mode="tn", name="d_w_oa", out_dtype=f32, m=E_A, n=D_MODEL, k=s)
    dy_b = mm(do_b, w["w_ob"], mode="nt", name="d_y_b", out_dtype=act_t, m=s, n=D_INNER, k=D_MODEL)
    g["w_ob"] = mm(y_b, do_b, mode="tn", name="d_w_ob", out_dtype=f32, m=D_INNER, n=D_MODEL, k=s)
    dy, dproj, g["ssm_norm_g"] = _rows_vjp_call(
        "gnorm_bwd", _f_gnorm, gn_rows, [w["ssm_norm_g"]], [(dy_b, D_INNER, 0)],
        [(act_t, None), (act_t, (dproj, MAIN_W, ZB_CB))], tm=256, nrows=s)
    dxs, dbm, dcm, ddtr, g["dt_bias"], g["a_log"], g["d_skip"] = _ssd_bwd(
        act, dtr, w["dt_bias"], w["a_log"], w["d_skip"], states, dy, nrows=s)
    dpre, g["conv_w"], g["conv_b"] = _conv_bwd_pre(proj, w["conv_w"], w["conv_b"], (dxs, dbm, dcm), tm=512, nrows=s)
    dproj = _conv_bwd_x(dpre, w["conv_w"], dproj, tm=512, nrows=s)
    dproj, g["ln_a_g"], g["ln_a_b"], g["w_s"], g["b_s"] = _rows_vjp_call(
        "branch_a_bwd", _f_branch_a, [(proj, UVZ_W, UVZ_CB)], a_pars, [(dy_a, E_A, 0)],
        [(act_t, (dproj, MAIN_W, UVZ_CB))], tm=128, nrows=s)
    g["w_main"] = mm(h, dproj, mode="tn", name="d_w_main", out_dtype=f32, m=D_MODEL, n=MAIN_W, k=s)
    g["w_dt"] = mm(h, ddtr, mode="tn", name="d_w_dt", out_dtype=f32, m=D_MODEL, n=DT_W, k=s)
    dh = mm(dproj, w["w_main"], mode="nt", name="d_h_main", out_dtype=f32, m=s, n=D_MODEL, k=MAIN_W)
    dh = mm(ddtr, w["w_dt"], mode="nt", name="d_h", out_dtype=f32, m=s, n=D_MODEL, k=DT_W,
            extras=(dh,), epilogue=_add_epilogue)
    grad_x, g["norm_g"] = _rows_vjp_call(
        "pre_norm_bwd", _rms_and_skip, [(x, D_MODEL, 0)], [w["norm_g"]], [(dh, D_MODEL, 0), (dx1, D_MODEL, 0)],
        [(f32, None)], tm=512, nrows=s)
    return loss, grad_x, g


_O_ZB = 3 * E_A
_O_XBC = _O_ZB + D_INNER
_O_DT = _O_XBC + CONV_DIM
_O_G = _O_DT + N_HEADS


def _heads_to_lanes(v):
    r = v.shape[0]
    v = v.reshape(r, N_GROUPS, HEADS_PER_GROUP)
    return jnp.pad(v, ((0, 0), (0, 0), (0, 128 - HEADS_PER_GROUP))).reshape(r, DT_W)


def _lanes_to_heads(v):
    r = v.shape[0]
    return v.reshape(r, N_GROUPS, 128)[:, :, :HEADS_PER_GROUP].reshape(r, N_HEADS)


def _block_cols(blocks, a, b):
    parts = []
    for k in range(N_CHIPS):
        lo, hi = max(a, k * W_IN_BLOCK), min(b, (k + 1) * W_IN_BLOCK)
        if lo < hi:
            parts.append(blocks[k][:, lo - k * W_IN_BLOCK:hi - k * W_IN_BLOCK])
    return parts


_W_IN_SEGMENTS = ((0, _O_ZB, "m", 0), (_O_ZB, _O_XBC, "m", UVZ_W + XBC_W), (_O_XBC, _O_DT, "m", UVZ_W),
                  (_O_DT, _O_G, "d", 0), (_O_G, N_IN, "m", MAIN_W - G_W))


def _w_in_grad_blocks(gm, gdt):
    blocks = []
    for k in range(N_CHIPS):
        a, b = k * W_IN_BLOCK, (k + 1) * W_IN_BLOCK
        parts = []
        for s, e, src, off in _W_IN_SEGMENTS:
            lo, hi = max(a, s), min(b, e)
            if lo < hi:
                parts.append((gm if src == "m" else gdt)[:, off + lo - s:off + hi - s])
        blocks.append(jnp.concatenate(parts, axis=1))
    return jnp.stack(blocks)


def _layout_weights(f, w_in_blocks=None):
    w = dict(f)
    if w_in_blocks is None:
        w_in = w.pop("w_in")
        w_in_blocks = jnp.stack([w_in[:, k * W_IN_BLOCK:(k + 1) * W_IN_BLOCK] for k in range(N_CHIPS)])
    cols = functools.partial(_block_cols, w_in_blocks)
    w["w_main"] = jnp.concatenate(cols(0, _O_ZB) + cols(_O_XBC, _O_DT) + cols(_O_ZB, _O_XBC) + cols(_O_G, N_IN), axis=1)
    w["w_dt"] = _heads_to_lanes(jnp.concatenate(cols(_O_DT, _O_G), axis=1))
    w["b_s"] = f["b_s"].reshape(G_A, CHUNK, 1)
    for n in ("dt_bias", "a_log", "d_skip"):
        w[n] = _heads_to_lanes(f[n])
    return w


def _natural_grads(g):
    out = dict(g)
    gm = out.pop("w_main")
    gdt = _lanes_to_heads(out.pop("w_dt"))
    out["w_in"] = jnp.concatenate(
        [gm[:, :UVZ_W], gm[:, UVZ_W + XBC_W:UVZ_W + XBC_W + ZB_W], gm[:, UVZ_W:UVZ_W + XBC_W], gdt, gm[:, MAIN_W - G_W:]],
        axis=1)
    out["b_s"] = g["b_s"].reshape(G_A, CHUNK)
    for n in ("dt_bias", "a_log", "d_skip"):
        out[n] = _lanes_to_heads(g[n])
    return out


def _place():
    return lax.axis_index("x"), lax.axis_index("y"), lax.axis_index("c")


def _other_chips(x, y):
    return [(1 - x, y), (x, 1 - y), (1 - x, 1 - y)]


def _rcopy(src, dst, ssem, rsem, dev):
    return pltpu.make_async_remote_copy(src_ref=src, dst_ref=dst, send_sem=ssem, recv_sem=rsem,
                                        device_id=dev, device_id_type=MESH)


def _half(ref_rows, half):
    hs = ref_rows // 2
    return pl.ds(pl.multiple_of(half * hs, 16), hs)


def _gather_weights(shards, conv_shard):
    nw = len(shards)

    def body(*refs):
        sh, cv = refs[:nw], refs[nw]
        out, cvo = refs[nw + 1:2 * nw + 1], refs[2 * nw + 1]
        ici_s, ici_r, fw_s, fw_r, own_s, own_r, cv_s, cv_r = refs[2 * nw + 2:]
        x, y, c = _place()
        me, sib, chips = 2 * x + y, (x, y, 1 - c), _other_chips(x, y)
        own = [_rcopy(sh[w], out[w].at[me], own_s.at[w], own_r.at[w], sib) for w in range(nw)]
        own.append(_rcopy(cv, cvo.at[me], own_s.at[nw], own_r.at[nw], sib))
        for cp in own:
            cp.start()
        sends = []
        for w in range(nw):
            mine = _half(sh[w].shape[0], c)
            for j, chip in enumerate(chips):
                sends.append(_rcopy(sh[w].at[mine], out[w].at[me, mine], ici_s.at[3 * w + j], ici_r.at[3 * w + j], (*chip, c)))
        for j, chip in enumerate(chips):
            sends.append(_rcopy(cv, cvo.at[me], cv_s.at[j], cv_r.at[j], (*chip, c)))
        for cp in sends:
            cp.start()
        for w in range(nw):
            mine = _half(sh[w].shape[0], c)
            for j, chip in enumerate(chips):
                slab = out[w].at[2 * chip[0] + chip[1], mine]
                _rcopy(slab, slab, ici_s.at[3 * w + j], ici_r.at[3 * w + j], (*chip, c)).wait_recv()
                fwd = _rcopy(slab, slab, fw_s.at[3 * w + j], fw_r.at[3 * w + j], sib)
                fwd.start()
                sends.append(fwd)
        for j, chip in enumerate(chips):
            blk = cvo.at[2 * chip[0] + chip[1]]
            _rcopy(blk, blk, cv_s.at[j], cv_r.at[j], (*chip, c)).wait_recv()
        for w in range(nw):
            theirs = _half(sh[w].shape[0], 1 - c)
            for j, chip in enumerate(chips):
                slab = out[w].at[2 * chip[0] + chip[1], theirs]
                _rcopy(slab, slab, fw_s.at[3 * w + j], fw_r.at[3 * w + j], sib).wait_recv()
        for cp in sends:
            cp.wait_send()
        for cp in own:
            cp.wait()

    dma = pltpu.SemaphoreType.DMA
    return pl.pallas_call(
        body, name="gather_weights",
        in_specs=[ANY] * (nw + 1), out_specs=[ANY] * (nw + 1),
        out_shape=[jax.ShapeDtypeStruct((N_CHIPS,) + s.shape, s.dtype) for s in shards]
        + [jax.ShapeDtypeStruct((N_CHIPS,) + conv_shard.shape, conv_shard.dtype)],
        scratch_shapes=[dma((3 * nw,)), dma((3 * nw,)), dma((3 * nw,)), dma((3 * nw,)), dma((nw + 1,)), dma((nw + 1,)),
                        dma((3,)), dma((3,))],
    )(*shards, conv_shard)


def _swap_with_sibling(arrs):
    n = len(arrs)

    def body(*refs):
        src, dst, s_sem, r_sem = refs[:n], refs[n:2 * n], refs[2 * n], refs[2 * n + 1]
        x, y, c = _place()
        cps = [_rcopy(src[i], dst[i], s_sem.at[i], r_sem.at[i], (x, y, 1 - c)) for i in range(n)]
        for cp in cps:
            cp.start()
        for cp in cps:
            cp.wait()

    dma = pltpu.SemaphoreType.DMA
    return pl.pallas_call(
        body, name="swap_with_sibling", in_specs=[ANY] * n, out_specs=[ANY] * n,
        out_shape=[jax.ShapeDtypeStruct(a.shape, a.dtype) for a in arrs], scratch_shapes=[dma((n,)), dma((n,))],
    )(*arrs)


def _scatter_blocks(arrs):
    n = len(arrs)

    def body(*refs):
        src, dst, s_sem, r_sem = refs[:n], refs[n:2 * n], refs[2 * n], refs[2 * n + 1]
        x, y, c = _place()
        cps = []
        for i in range(n):
            for j, chip in enumerate(_other_chips(x, y)):
                cps.append(_rcopy(src[i].at[2 * chip[0] + chip[1]], dst[i].at[j], s_sem.at[3 * i + j], r_sem.at[3 * i + j],
                                  (*chip, c)))
        for cp in cps:
            cp.start()
        for cp in cps:
            cp.wait()

    dma = pltpu.SemaphoreType.DMA
    return pl.pallas_call(
        body, name="scatter_blocks", in_specs=[ANY] * n, out_specs=[ANY] * n,
        out_shape=[jax.ShapeDtypeStruct((3,) + a.shape[1:], a.dtype) for a in arrs],
        scratch_shapes=[dma((3 * n,)), dma((3 * n,))],
    )(*arrs)


def _share_halves(arrs):
    n = len(arrs)

    def body(*refs):
        buf, s_sem, r_sem = refs[n:2 * n], refs[2 * n], refs[2 * n + 1]
        x, y, c = _place()
        cps = []
        for i in range(n):
            mine = buf[i].at[_half(buf[i].shape[0], c)]
            cps.append(_rcopy(mine, mine, s_sem.at[i], r_sem.at[i], (x, y, 1 - c)))
        for cp in cps:
            cp.start()
        for i in range(n):
            theirs = buf[i].at[_half(buf[i].shape[0], 1 - c)]
            _rcopy(theirs, theirs, s_sem.at[i], r_sem.at[i], (x, y, 1 - c)).wait_recv()
        for cp in cps:
            cp.wait_send()

    dma = pltpu.SemaphoreType.DMA
    return pl.pallas_call(
        body, name="share_halves", in_specs=[ANY] * n, out_specs=[ANY] * n,
        out_shape=[jax.ShapeDtypeStruct(a.shape, a.dtype) for a in arrs],
        input_output_aliases={i: i for i in range(n)}, scratch_shapes=[dma((n,)), dma((n,))],
    )(*arrs)


def _allreduce_small(packed):
    rows = packed.shape[0]

    def body(p_ref, o_ref, buf, s_sem, r_sem):
        x, y, c = _place()
        me = 4 * x + 2 * y + c
        buf[me] = p_ref[...]
        cps = []
        for d in range(1, N_DEV):
            px, py, pc = ((1 - x) if d & 4 else x), ((1 - y) if d & 2 else y), ((1 - c) if d & 1 else c)
            cps.append(_rcopy(p_ref, buf.at[me], s_sem.at[d - 1], r_sem.at[d - 1], (px, py, pc)))
        for cp in cps:
            cp.start()
        for d in range(1, N_DEV):
            px, py, pc = ((1 - x) if d & 4 else x), ((1 - y) if d & 2 else y), ((1 - c) if d & 1 else c)
            landed = buf.at[4 * px + 2 * py + pc]
            _rcopy(p_ref, landed, s_sem.at[d - 1], r_sem.at[d - 1], (px, py, pc)).wait_recv()
        for cp in cps:
            cp.wait_send()
        acc = buf[0]
        for d in range(1, N_DEV):
            acc = acc + buf[d]
        o_ref[...] = acc

    dma = pltpu.SemaphoreType.DMA
    return pl.pallas_call(
        body, name="allreduce_small", out_shape=jax.ShapeDtypeStruct(packed.shape, F32),
        scratch_shapes=[pltpu.VMEM((N_DEV, rows, 128), F32), dma((N_DEV - 1,)), dma((N_DEV - 1,))],
    )(packed)


def _row_tile(rows, cols):
    tr = max(8, min(rows, (1 << 20) // (4 * cols) // 8 * 8))
    while rows % tr:
        tr -= 8
    return tr


def _chip_sum(name, g5, recv, c_arr):
    _, _, hs, cols = g5.shape
    tr = _row_tile(hs, cols)

    def body(_, a_ref, b_ref, o32_ref, ow_ref):
        s = a_ref[...] + b_ref[...].astype(F32)
        o32_ref[...] = s
        ow_ref[...] = s.astype(ow_ref.dtype)

    blk = pl.BlockSpec((None, tr, cols), lambda b, i, c: (b, i, 0))
    return pl.pallas_call(
        body, name=name,
        grid_spec=pltpu.PrefetchScalarGridSpec(
            num_scalar_prefetch=1, grid=(N_CHIPS, hs // tr),
            in_specs=[pl.BlockSpec((None, None, tr, cols), lambda b, i, c: (b, c[0], i, 0)), blk], out_specs=[blk, blk]),
        out_shape=[jax.ShapeDtypeStruct((N_CHIPS, hs, cols), F32), jax.ShapeDtypeStruct((N_CHIPS, hs, cols), WIRE_DTYPE)],
    )(c_arr, g5, recv)


def _final_sum(name, own, recv, place_arr):
    _, hs, cols = own.shape
    tr = _row_tile(hs, cols)
    nt = hs // tr

    def body(_, a_ref, r_ref, o_ref):
        o_ref[...] = ((a_ref[...] + r_ref[0].astype(F32)) + r_ref[1].astype(F32)) + r_ref[2].astype(F32)

    return pl.pallas_call(
        body, name=name,
        grid_spec=pltpu.PrefetchScalarGridSpec(
            num_scalar_prefetch=1, grid=(nt,),
            in_specs=[pl.BlockSpec((None, tr, cols), lambda i, m: (m[0], i, 0)),
                      pl.BlockSpec((3, tr, cols), lambda i, m: (0, i, 0))],
            out_specs=pl.BlockSpec((tr, cols), lambda i, m: (m[1] * nt + i, 0))),
        out_shape=jax.ShapeDtypeStruct((2 * hs, cols), F32),
    )(place_arr, own, recv)


def _adamw(w, g, m, v):
    m = ADAM_B1 * m + (1.0 - ADAM_B1) * g
    v = ADAM_B2 * v + (1.0 - ADAM_B2) * (g * g)
    m_hat = m / (1.0 - ADAM_B1 ** ADAM_STEP)
    v_hat = v / (1.0 - ADAM_B2 ** ADAM_STEP)
    return -ADAM_LR * (m_hat / (jnp.sqrt(v_hat) + ADAM_EPS) + ADAM_WD * w), m, v


def _adamw_call(name, w, g, m, v):
    rows, cols = w.shape
    tr = _row_tile(rows, cols)

    def body(w_ref, g_ref, m_ref, v_ref, d_ref, nm_ref, nv_ref):
        d_ref[...], nm_ref[...], nv_ref[...] = _adamw(w_ref[...], g_ref[...], m_ref[...], v_ref[...])

    blk = pl.BlockSpec((tr, cols), lambda i: (i, 0))
    return pl.pallas_call(
        body, name=name, grid=(rows // tr,), in_specs=[blk] * 4, out_specs=[blk] * 3,
        out_shape=[jax.ShapeDtypeStruct(w.shape, F32)] * 3,
        compiler_params=pltpu.CompilerParams(dimension_semantics=("parallel",)),
    )(w, g, m, v)


def _adamw_small(ws, gs, ms, vs):
    n = len(ws)

    def body(*refs):
        for i in range(n):
            w_ref, g_ref, m_ref, v_ref = (refs[k * n + i] for k in range(4))
            d, nm, nv = _adamw(w_ref[...], g_ref[...], m_ref[...], v_ref[...])
            refs[4 * n + i][...] = d
            refs[5 * n + i][...] = nm
            refs[6 * n + i][...] = nv

    out = pl.pallas_call(
        body, name="adamw_small", out_shape=[jax.ShapeDtypeStruct(a.shape, F32) for a in ws] * 3,
    )(*ws, *gs, *ms, *vs)
    return out[:n], out[n:2 * n], out[2 * n:]


_BIG = ("w_in", "w_oa", "w_ob", "w_out", "w_pg", "w_ple")
_SMALL = ("norm_g", "ln_a_g", "ln_a_b", "w_s", "b_s", "conv_w", "conv_b", "dt_bias", "a_log", "d_skip", "ssm_norm_g",
          "ple_norm_g", "final_g")
_WEIGHTS = ("norm_g", "w_in", "ln_a_g", "ln_a_b", "w_s", "b_s", "conv_w", "conv_b", "dt_bias", "a_log", "d_skip",
            "ssm_norm_g", "w_oa", "w_ob", "w_out", "ple_norm_g", "w_pg", "w_ple", "final_g")
_COL_SHARDED = ("w_in", "w_ple")
_PACK = 1024


def _blocks_to_full(col_sharded, blocks):
    if col_sharded:
        return jnp.concatenate([blocks[k] for k in range(N_CHIPS)], axis=1)
    return blocks.reshape(N_CHIPS * blocks.shape[1], blocks.shape[2])


def _full_to_blocks(col_sharded, full):
    if col_sharded:
        w = full.shape[1] // N_CHIPS
        return jnp.stack([full[:, k * w:(k + 1) * w] for k in range(N_CHIPS)])
    return full.reshape(N_CHIPS, full.shape[0] // N_CHIPS, full.shape[1])


def _two_d(n, a):
    if n == "w_s":
        return a.reshape(G_A * CHUNK, CHUNK)
    if n in ("b_s", "conv_w"):
        return a.reshape(a.shape[-2], a.shape[-1])
    return a.reshape(1, a.shape[-1])


def kernel(x, p, norm_g, w_in, ln_a_g, ln_a_b, w_s, b_s, conv_w, conv_b, dt_bias, a_log, d_skip, ssm_norm_g, w_oa, w_ob, w_out, ple_norm_g, w_pg, w_ple, final_g, loss_target, m_norm_g, m_w_in, m_ln_a_g, m_ln_a_b, m_w_s, m_b_s, m_conv_w, m_conv_b, m_dt_bias, m_a_log, m_d_skip, m_ssm_norm_g, m_w_oa, m_w_ob, m_w_out, m_ple_norm_g, m_w_pg, m_w_ple, m_final_g, v_norm_g, v_w_in, v_ln_a_g, v_ln_a_b, v_w_s, v_b_s, v_conv_w, v_conv_b, v_dt_bias, v_a_log, v_d_skip, v_ssm_norm_g, v_w_oa, v_w_ob, v_w_out, v_ple_norm_g, v_w_pg, v_w_ple, v_final_g):
    wt = dict(norm_g=norm_g, w_in=w_in, ln_a_g=ln_a_g, ln_a_b=ln_a_b, w_s=w_s, b_s=b_s, conv_w=conv_w, conv_b=conv_b,
              dt_bias=dt_bias, a_log=a_log, d_skip=d_skip, ssm_norm_g=ssm_norm_g, w_oa=w_oa, w_ob=w_ob, w_out=w_out,
              ple_norm_g=ple_norm_g, w_pg=w_pg, w_ple=w_ple, final_g=final_g)
    mom = dict(norm_g=m_norm_g, w_in=m_w_in, ln_a_g=m_ln_a_g, ln_a_b=m_ln_a_b, w_s=m_w_s, b_s=m_b_s, conv_w=m_conv_w,
               conv_b=m_conv_b, dt_bias=m_dt_bias, a_log=m_a_log, d_skip=m_d_skip, ssm_norm_g=m_ssm_norm_g, w_oa=m_w_oa,
               w_ob=m_w_ob, w_out=m_w_out, ple_norm_g=m_ple_norm_g, w_pg=m_w_pg, w_ple=m_w_ple, final_g=m_final_g)
    vel = dict(norm_g=v_norm_g, w_in=v_w_in, ln_a_g=v_ln_a_g, ln_a_b=v_ln_a_b, w_s=v_w_s, b_s=v_b_s, conv_w=v_conv_w,
               conv_b=v_conv_b, dt_bias=v_dt_bias, a_log=v_a_log, d_skip=v_d_skip, ssm_norm_g=v_ssm_norm_g, w_oa=v_w_oa,
               w_ob=v_w_ob, w_out=v_w_out, ple_norm_g=v_ple_norm_g, w_pg=v_w_pg, w_ple=v_w_ple, final_g=v_final_g)
    xi, yi, ci = _place()
    me = 2 * xi + yi
    c_arr = jnp.reshape(ci, (1,)).astype(jnp.int32)
    place_arr = jnp.stack([me, ci]).astype(jnp.int32)

    shard = {n: wt[n][0] for n in _BIG}
    blocks = _gather_weights([shard[n].astype(WIRE_DTYPE) for n in _BIG], conv_w[0])
    full = {n: _blocks_to_full(n in _COL_SHARDED, b) for n, b in zip(_BIG[1:], blocks[1:-1])}
    full["conv_w"] = _blocks_to_full(True, blocks[-1])
    for n in _SMALL:
        if n != "conv_w":
            full[n] = wt[n][0] if wt[n].ndim > 2 else wt[n].reshape(1, wt[n].shape[-1])

    loss_row, grad_x, g = _local_step(x[0], p[0, 0], loss_target[0], _layout_weights(full, w_in_blocks=blocks[0]))
    w_in_grad = _w_in_grad_blocks(g["w_main"], _lanes_to_heads(g["w_dt"]))
    g = _natural_grads(g)
    loss = lax.psum(loss_row[0, 0], ("x", "y", "c"))

    g5 = {}
    for n in _BIG:
        b = w_in_grad if n == "w_in" else _full_to_blocks(n in _COL_SHARDED, g[n])
        g5[n] = b.reshape(N_CHIPS, 2, b.shape[1] // 2, b.shape[2])
    to_sibling = [lax.dynamic_index_in_dim(g5[n], 1 - ci, axis=1, keepdims=False).astype(WIRE_DTYPE) for n in _BIG]
    from_sibling = _swap_with_sibling(to_sibling)
    chip32, chip_wire = zip(*[_chip_sum("chip_sum_" + n, g5[n], r, c_arr) for n, r in zip(_BIG, from_sibling)])
    from_chips = _scatter_blocks(list(chip_wire))
    halves = [_final_sum("final_sum_" + n, a, r, place_arr) for n, a, r in zip(_BIG, chip32, from_chips)]
    grads = dict(zip(_BIG, _share_halves(halves)))

    pieces = [_two_d(n, g[n]).reshape(-1) for n in _SMALL]
    sizes = [v.shape[0] for v in pieces]
    padded = [-(-s // _PACK) * _PACK for s in sizes]
    packed = jnp.concatenate([jnp.pad(v, (0, ps - s)) for v, s, ps in zip(pieces, sizes, padded)]).reshape(-1, 128)
    summed = _allreduce_small(packed).reshape(-1)
    off = 0
    for n, s, ps in zip(_SMALL, sizes, padded):
        grads[n] = _two_d(n, g[n]).shape, summed[off:off + s]
        off += ps
    for n in _SMALL:
        shape, flat = grads[n]
        grads[n] = flat.reshape(shape)
    grads["conv_w"] = lax.dynamic_slice_in_dim(grads["conv_w"], me * (CONV_DIM // N_CHIPS), CONV_DIM // N_CHIPS, axis=1)

    delta, new_m, new_v = {}, {}, {}
    for n in _BIG:
        delta[n], new_m[n], new_v[n] = _adamw_call("adamw_" + n, shard[n], grads[n], mom[n][0], vel[n][0])
    small = _adamw_small([_two_d(n, wt[n]) for n in _SMALL], [grads[n] for n in _SMALL],
                         [_two_d(n, mom[n]) for n in _SMALL], [_two_d(n, vel[n]) for n in _SMALL])
    for i, n in enumerate(_SMALL):
        delta[n], new_m[n], new_v[n] = small[0][i], small[1][i], small[2][i]

    def shaped(d):
        return [d[n].reshape(wt[n].shape) for n in _WEIGHTS]

    return (loss, grad_x[None], *shaped(grads), *shaped(delta), *shaped(new_m), *shaped(new_v))
```

```python
import functools

import jax
import jax.numpy as jnp
from jax import lax
from jax.experimental import pallas as pl
from jax.experimental.pallas import tpu as pltpu

F32 = jnp.float32
MXU_DTYPE = jnp.bfloat16
ACT_DTYPE = jnp.bfloat16
WIRE_DTYPE = jnp.bfloat16

D_MODEL = 1024
PLE_DIM = 256
CHUNK = 128
EPS = 1e-6
E_A = D_MODEL
G_A = 4
D_INNER = 2 * D_MODEL
HEAD_DIM = 64
N_HEADS = D_INNER // HEAD_DIM
N_STATE = 128
N_GROUPS = 4
HEADS_PER_GROUP = N_HEADS // N_GROUPS
PAIRS_PER_GROUP = HEADS_PER_GROUP // 2
CONV_K = 4
CONV_DIM = D_INNER + 2 * N_GROUPS * N_STATE
N_IN = 3 * E_A + D_INNER + CONV_DIM + N_HEADS + 2 * D_MODEL
N_CHIPS = 4
N_DEV = 8
W_IN_BLOCK = N_IN // N_CHIPS

UVZ_W, XBC_W, ZB_W, G_W = 3 * E_A, CONV_DIM, D_INNER, 2 * D_MODEL
MAIN_W = UVZ_W + XBC_W + ZB_W + G_W
UVZ_CB, XBC_CB, ZB_CB, G_CB = 0, 1, 3, 4
DT_W = N_GROUPS * 128

ADAM_LR, ADAM_B1, ADAM_B2, ADAM_EPS, ADAM_WD, ADAM_STEP = 0.001, 0.9, 0.999, 1e-08, 0.01, 10

MESH = pl.DeviceIdType.MESH
ANY = pl.BlockSpec(memory_space=pl.ANY)


def _mxu(v):
    return v.astype(MXU_DTYPE)


def _dot(a, b, dims=(((1,), (0,)), ((), ()))):
    return lax.dot_general(_mxu(a), _mxu(b), dims, preferred_element_type=F32)


MM_TILE = 1024
MM_VMEM_BUDGET = 46 << 20


def _mm_tk(m, n, k, tm, tn, a_bytes, b_bytes, out_bytes, extra_bytes):
    for parts in range(1, k // 128 + 1):
        if k % parts or (k // parts) % 128 and parts > 1:
            continue
        tk = k // parts
        need = 2 * tk * (tm * a_bytes + tn * b_bytes) + 2 * tm * tn * (out_bytes + extra_bytes) + (tm * tn * 4 if parts > 1 else 0)
        if need <= MM_VMEM_BUDGET:
            return tk
    return 128


def _matmul(a, b, *, mode, name, out_dtype, m, n, k, tm=MM_TILE, tn=MM_TILE, tk=None, a_off=0, b_off=0,
            extras=(), epilogue=None, after=()):
    tm, tn = min(tm, m), min(tn, n)
    if tk is None:
        tk = _mm_tk(m, n, k, tm, tn, a.dtype.itemsize, b.dtype.itemsize, jnp.dtype(out_dtype).itemsize,
                    sum(e.dtype.itemsize for e in extras))
    tk = min(tk, k)
    assert m % tm == 0 and n % tn == 0 and k % tk == 0, (name, m, n, k, tm, tn, tk)
    nk = k // tk
    if mode == "nn":
        assert a_off % tk == 0 and b_off % tn == 0
        a_spec = pl.BlockSpec((tm, tk), lambda i, j, kk: (i, kk + a_off // tk))
        b_spec = pl.BlockSpec((tk, tn), lambda i, j, kk: (kk, j + b_off // tn))
        dims = (((1,), (0,)), ((), ()))
    elif mode == "nt":
        a_spec = pl.BlockSpec((tm, tk), lambda i, j, kk: (i, kk))
        b_spec = pl.BlockSpec((tn, tk), lambda i, j, kk: (j, kk))
        dims = (((1,), (1,)), ((), ()))
    else:
        assert a_off % tm == 0 and b_off % tn == 0
        a_spec = pl.BlockSpec((tk, tm), lambda i, j, kk: (kk, i + a_off // tm))
        b_spec = pl.BlockSpec((tk, tn), lambda i, j, kk: (kk, j + b_off // tn))
        dims = (((0,), (0,)), ((), ()))
    ne = len(extras)

    def finish(acc, extra_refs, o_ref):
        res = acc if epilogue is None else epilogue(acc, *[e[...] for e in extra_refs])
        o_ref[...] = res.astype(o_ref.dtype)

    def body(a_ref, b_ref, *rest):
        extra_refs, o_ref = rest[:ne], rest[ne + len(after)]
        part = _dot(a_ref[...], b_ref[...], dims)
        if nk == 1:
            finish(part, extra_refs, o_ref)
            return
        acc_ref = rest[ne + len(after) + 1]
        kk = pl.program_id(2)

        @pl.when(kk == 0)
        def _():
            acc_ref[...] = part

        @pl.when(kk > 0)
        def _():
            acc_ref[...] += part

        @pl.when(kk == nk - 1)
        def _():
            finish(acc_ref[...], extra_refs, o_ref)

    o_spec = pl.BlockSpec((tm, tn), lambda i, j, kk: (i, j))
    return pl.pallas_call(
        body, name=name, grid=(m // tm, n // tn, nk),
        in_specs=[a_spec, b_spec] + [o_spec] * ne + [ANY] * len(after), out_specs=o_spec,
        out_shape=jax.ShapeDtypeStruct((m, n), out_dtype),
        scratch_shapes=[pltpu.VMEM((tm, tn), F32)] if nk > 1 else [],
        compiler_params=pltpu.CompilerParams(dimension_semantics=("parallel", "parallel", "arbitrary")),
    )(a, b, *extras, *after)


def _row_spec(tm, width, cb):
    return pl.BlockSpec((tm, width), lambda i: (i, cb))


def _whole_spec(shape):
    nd = len(shape)
    return pl.BlockSpec(tuple(shape), lambda i: (0,) * nd)


def _rows_call(name, f, rows, pars, outs, *, tm, nrows):
    tm = min(tm, nrows)
    nr, npar = len(rows), len(pars)

    def body(*refs):
        rv = [r[...].astype(F32) for r in refs[:nr]]
        pv = [p[...] for p in refs[nr:nr + npar]]
        res = f(*rv, *pv)
        for o_ref, r in zip(refs[nr + npar:], res):
            o_ref[...] = r.astype(o_ref.dtype)

    return pl.pallas_call(
        body, name=name, grid=(nrows // tm,),
        in_specs=[_row_spec(tm, w, cb) for _, w, cb in rows] + [_whole_spec(p.shape) for p in pars],
        out_specs=[_row_spec(tm, w, 0) for w, _ in outs],
        out_shape=[jax.ShapeDtypeStruct((nrows, w), dt) for w, dt in outs],
        compiler_params=pltpu.CompilerParams(dimension_semantics=("parallel",)),
    )(*[r[0] for r in rows], *pars)


def _rows_vjp_call(name, f, rows, pars, cots, drows, *, tm, nrows):
    tm = min(tm, nrows)
    nr, npar, nc = len(rows), len(pars), len(cots)
    alias_bufs, aliases = [], {}
    out_shape, out_specs = [], []
    for (arr, w, cb), d in zip(rows, drows):
        if d is None:
            continue
        dt, into = d
        if into is None:
            out_shape.append(jax.ShapeDtypeStruct((nrows, w), dt))
            out_specs.append(_row_spec(tm, w, 0))
        else:
            buf, total, ocb = into
            if buf is not None:
                aliases[nr + npar + nc + len(alias_bufs)] = len(out_shape)
                alias_bufs.append(buf)
            out_shape.append(jax.ShapeDtypeStruct((nrows, total), dt))
            out_specs.append(_row_spec(tm, w, ocb))
    n_drow = len(out_shape)
    for p in pars:
        out_shape.append(jax.ShapeDtypeStruct(p.shape, F32))
        out_specs.append(_whole_spec(p.shape))
    na = len(alias_bufs)

    def body(*refs):
        rv = [r[...].astype(F32) for r in refs[:nr]]
        pv = [p[...] for p in refs[nr:nr + npar]]
        cv = tuple(c[...].astype(F32) for c in refs[nr + npar:nr + npar + nc])
        o_refs = refs[nr + npar + nc + na:]
        _, vjp = jax.vjp(f, *rv, *pv)
        g = vjp(cv)
        oi = 0
        for ri, d in enumerate(drows):
            if d is not None:
                o_refs[oi][...] = g[ri].astype(o_refs[oi].dtype)
                oi += 1
        first = pl.program_id(0) == 0
        for pi in range(npar):
            acc = o_refs[n_drow + pi]

            @pl.when(first)
            def _(acc=acc):
                acc[...] = jnp.zeros_like(acc)

            acc[...] += g[nr + pi]

    return pl.pallas_call(
        body, name=name, grid=(nrows // tm,),
        in_specs=[_row_spec(tm, w, cb) for _, w, cb in rows] + [_whole_spec(p.shape) for p in pars]
        + [_row_spec(tm, w, cb) for _, w, cb in cots] + [ANY] * na,
        out_specs=out_specs, out_shape=out_shape, input_output_aliases=aliases,
        compiler_params=pltpu.CompilerParams(dimension_semantics=("arbitrary",)),
    )(*[r[0] for r in rows], *pars, *[c[0] for c in cots], *alias_bufs)


def _rms(x, g):
    return x * lax.rsqrt(jnp.mean(x * x, axis=-1, keepdims=True) + EPS) * g


def _f_rms(x, g):
    return (_rms(x, g),)


def _tril_mask():
    return lax.broadcasted_iota(jnp.int32, (CHUNK, CHUNK), 0) >= lax.broadcasted_iota(jnp.int32, (CHUNK, CHUNK), 1)


def _f_branch_a(uvz, ln_g, ln_b, w_s, b_s):
    u = jax.nn.gelu(uvz[:, :E_A])
    v = jax.nn.gelu(uvz[:, E_A:2 * E_A])
    z = uvz[:, 2 * E_A:]
    xc = v - jnp.mean(v, axis=-1, keepdims=True)
    vn = xc * lax.rsqrt(jnp.mean(xc * xc, axis=-1, keepdims=True) + EPS) * ln_g + ln_b
    mask = _tril_mask()
    ws = [jnp.where(mask, w_s[g], 0.0) for g in range(G_A)]
    gw = E_A // G_A
    rows = []
    for c in range(uvz.shape[0] // CHUNK):
        vc = vn[c * CHUNK:(c + 1) * CHUNK]
        rows.append(jnp.concatenate([_dot(ws[g], vc[:, g * gw:(g + 1) * gw]) + b_s[g] for g in range(G_A)], axis=1))
    sv = rows[0] if len(rows) == 1 else jnp.concatenate(rows, axis=0)
    return (u * sv * jax.nn.silu(z),)


def _f_gnorm(y, zb, g):
    yz = y * jax.nn.silu(zb)
    gw = D_INNER // N_GROUPS
    parts = []
    for i in range(N_GROUPS):
        s = yz[:, i * gw:(i + 1) * gw]
        parts.append(s * lax.rsqrt(jnp.mean(s * s, axis=-1, keepdims=True) + EPS))
    return (jnp.concatenate(parts, axis=1) * g,)


def _f_merge(g2, oa, ob):
    return (jax.nn.sigmoid(g2[:, :D_MODEL]) * oa + jax.nn.sigmoid(g2[:, D_MODEL:]) * ob,)


def _f_loss(x1, gp, pe, tgt, fg):
    x2 = x1 + jax.nn.sigmoid(gp) * pe
    err = _rms(x2, fg) - tgt
    return 0.5 * jnp.sum(jnp.mean(err * err, axis=-1))


def _head(x1, gp, pe, tgt, fg, *, tm, nrows):
    tm = min(tm, nrows)

    def body(x1_ref, gp_ref, pe_ref, t_ref, fg_ref, dx_ref, dgp_ref, dpe_ref, dfg_ref, loss_ref):
        loss, vjp = jax.vjp(_f_loss, x1_ref[...], gp_ref[...], pe_ref[...], t_ref[...], fg_ref[...])
        dx, dgp, dpe, _, dfg = vjp(jnp.ones((), F32))
        dx_ref[...] = dx
        dgp_ref[...] = dgp.astype(dgp_ref.dtype)
        dpe_ref[...] = dpe.astype(dpe_ref.dtype)

        @pl.when(pl.program_id(0) == 0)
        def _():
            dfg_ref[...] = jnp.zeros_like(dfg_ref)
            loss_ref[...] = jnp.zeros_like(loss_ref)

        dfg_ref[...] += dfg
        loss_ref[...] += jnp.full(loss_ref.shape, loss, F32)

    row = _row_spec(tm, D_MODEL, 0)
    return pl.pallas_call(
        body, name="head", grid=(nrows // tm,),
        in_specs=[row, row, row, row, _whole_spec((1, D_MODEL))],
        out_specs=[row, row, row, _whole_spec((1, D_MODEL)), _whole_spec((1, 128))],
        out_shape=[jax.ShapeDtypeStruct((nrows, D_MODEL), F32), jax.ShapeDtypeStruct((nrows, D_MODEL), ACT_DTYPE),
                   jax.ShapeDtypeStruct((nrows, D_MODEL), ACT_DTYPE), jax.ShapeDtypeStruct((1, D_MODEL), F32),
                   jax.ShapeDtypeStruct((1, 128), F32)],
        compiler_params=pltpu.CompilerParams(dimension_semantics=("arbitrary",)),
    )(x1, gp, pe, tgt, fg)


def _shift_rows(cur, edge, j, up):
    tm = cur.shape[0]
    row = lax.broadcasted_iota(jnp.int32, cur.shape, 0)
    if up:
        sh = pltpu.roll(cur, tm - j, 0)
        e = jnp.tile(pltpu.roll(edge, 8 - j, 0), (tm // 8, 1))
        return jnp.where(row >= tm - j, e, sh)
    sh = pltpu.roll(cur, j, 0)
    e = jnp.tile(pltpu.roll(edge, j, 0), (tm // 8, 1))
    return jnp.where(row < j, e, sh)


def _conv_pre(cur, prev, w, b):
    acc = cur * w[CONV_K - 1:CONV_K] + b
    taps = [cur]
    for j in range(1, CONV_K):
        s = _shift_rows(cur, prev, j, up=False)
        taps.append(s)
        acc = acc + s * w[CONV_K - 1 - j:CONV_K - j]
    return acc, taps


def _halo_specs(tm, nrows, cb, before):
    nb = tm // 8
    last = nrows // 8 - 1
    if before:
        return pl.BlockSpec((8, XBC_W), lambda i: (jnp.maximum(i * nb - 1, 0), cb))
    return pl.BlockSpec((8, XBC_W), lambda i: (jnp.minimum((i + 1) * nb, last), cb))


def _conv_fwd(proj, conv_w, conv_b, *, tm, nrows):
    tm = min(tm, nrows)

    def body(cur_ref, prev_ref, w_ref, b_ref, o_ref):
        prev = jnp.where(pl.program_id(0) == 0, 0.0, prev_ref[...].astype(F32))
        pre, _ = _conv_pre(cur_ref[...].astype(F32), prev, w_ref[...], b_ref[...])
        o_ref[...] = jax.nn.silu(pre).astype(o_ref.dtype)

    return pl.pallas_call(
        body, name="conv_fwd", grid=(nrows // tm,),
        in_specs=[_row_spec(tm, XBC_W, XBC_CB), _halo_specs(tm, nrows, XBC_CB, True),
                  _whole_spec((CONV_K, XBC_W)), _whole_spec((1, XBC_W))],
        out_specs=_row_spec(tm, XBC_W, 0), out_shape=jax.ShapeDtypeStruct((nrows, XBC_W), ACT_DTYPE),
        compiler_params=pltpu.CompilerParams(dimension_semantics=("parallel",)),
    )(proj, proj, conv_w, conv_b)


def _conv_bwd_pre(proj, conv_w, conv_b, dact, *, tm, nrows):
    tm = min(tm, nrows)
    nb = N_GROUPS * N_STATE

    def body(cur_ref, prev_ref, w_ref, b_ref, dxs_ref, dbm_ref, dcm_ref, dpre_ref, dw_ref, db_ref):
        prev = jnp.where(pl.program_id(0) == 0, 0.0, prev_ref[...].astype(F32))
        pre, taps = _conv_pre(cur_ref[...].astype(F32), prev, w_ref[...], b_ref[...])
        sg = jax.nn.sigmoid(pre)
        dy = jnp.concatenate([dxs_ref[...], dbm_ref[...], dcm_ref[...]], axis=1).astype(F32)
        dpre = dy * sg * (1.0 + pre * (1.0 - sg))
        dpre_ref[...] = dpre.astype(dpre_ref.dtype)

        @pl.when(pl.program_id(0) == 0)
        def _():
            dw_ref[...] = jnp.zeros_like(dw_ref)
            db_ref[...] = jnp.zeros_like(db_ref)

        db_ref[...] += jnp.sum(dpre, axis=0, keepdims=True)
        for j in range(CONV_K):
            k = CONV_K - 1 - j
            dw_ref[k:k + 1, :] += jnp.sum(dpre * taps[j], axis=0, keepdims=True)

    return pl.pallas_call(
        body, name="conv_bwd_pre", grid=(nrows // tm,),
        in_specs=[_row_spec(tm, XBC_W, XBC_CB), _halo_specs(tm, nrows, XBC_CB, True),
                  _whole_spec((CONV_K, XBC_W)), _whole_spec((1, XBC_W)),
                  _row_spec(tm, D_INNER, 0), _row_spec(tm, nb, 0), _row_spec(tm, nb, 0)],
        out_specs=[_row_spec(tm, XBC_W, 0), _whole_spec((CONV_K, XBC_W)), _whole_spec((1, XBC_W))],
        out_shape=[jax.ShapeDtypeStruct((nrows, XBC_W), ACT_DTYPE), jax.ShapeDtypeStruct((CONV_K, XBC_W), F32),
                   jax.ShapeDtypeStruct((1, XBC_W), F32)],
        compiler_params=pltpu.CompilerParams(dimension_semantics=("arbitrary",)),
    )(proj, proj, conv_w, conv_b, *dact)


def _conv_bwd_x(dpre, conv_w, dproj, *, tm, nrows):
    tm = min(tm, nrows)
    ntiles = nrows // tm

    def body(cur_ref, nxt_ref, w_ref, _, o_ref):
        cur = cur_ref[...].astype(F32)
        nxt = jnp.where(pl.program_id(0) == ntiles - 1, 0.0, nxt_ref[...].astype(F32))
        w = w_ref[...]
        acc = cur * w[CONV_K - 1:CONV_K]
        for j in range(1, CONV_K):
            acc = acc + _shift_rows(cur, nxt, j, up=True) * w[CONV_K - 1 - j:CONV_K - j]
        o_ref[...] = acc.astype(o_ref.dtype)

    return pl.pallas_call(
        body, name="conv_bwd_x", grid=(ntiles,),
        in_specs=[_row_spec(tm, XBC_W, 0), _halo_specs(tm, nrows, 0, False), _whole_spec((CONV_K, XBC_W)), ANY],
        out_specs=_row_spec(tm, XBC_W, XBC_CB), out_shape=jax.ShapeDtypeStruct(dproj.shape, dproj.dtype),
        input_output_aliases={3: 0},
        compiler_params=pltpu.CompilerParams(dimension_semantics=("parallel",)),
    )(dpre, dpre, conv_w, dproj)


SSD_SPAN = 4
_XS_GW = D_INNER // N_GROUPS
_NT = (((1,), (1,)), ((), ()))
_TN = (((0,), (0,)), ((), ()))


def _bf16_terms(x, terms):
    parts, rest = [], x
    for _ in range(terms):
        part = rest.astype(jnp.bfloat16)
        parts.append(part)
        rest = rest - part.astype(F32)
    return parts


def _head_lane_matrix():
    return (lax.broadcasted_iota(jnp.int32, (128, _XS_GW), 0)
            == lax.broadcasted_iota(jnp.int32, (128, _XS_GW), 1) // HEAD_DIM).astype(jnp.bfloat16)


@functools.partial(jax.custom_vjp, nondiff_argnums=(1,))
def _head_lanes(cols, terms):
    e = _head_lane_matrix()
    return sum(jnp.dot(t, e, preferred_element_type=F32) for t in _bf16_terms(cols, terms))


def _head_lanes_fwd(cols, terms):
    return _head_lanes(cols, terms), None


def _head_lanes_bwd(terms, _, g):
    e = _head_lane_matrix()
    return (sum(lax.dot_general(t, e, _NT, preferred_element_type=F32) for t in _bf16_terms(g, 2)),)


_head_lanes.defvjp(_head_lanes_fwd, _head_lanes_bwd)


def _ssd_chunk(k, xs, bm, cm, dtr, hprev, dtb, alog, dsk):
    causal, tri, lo = k
    dt = jax.nn.softplus(dtr + dtb)
    da = dt * (-jnp.exp(alog))
    cs = jnp.dot(tri, da, precision=lax.Precision.HIGHEST, preferred_element_type=F32)
    cst = cs.T
    cs_l = _head_lanes(cs, 3)
    xdt = xs * _head_lanes(dt, 2)
    cb = _dot(cm, bm, _NT)
    yd = []
    for q in range(PAIRS_PER_GROUP):
        xq = xdt[:, 128 * q:128 * (q + 1)]
        y2 = [_dot(cb * jnp.exp(jnp.where(causal, cs[:, h:h + 1] - cst[h:h + 1, :], -jnp.inf)), xq)
              for h in (2 * q, 2 * q + 1)]
        yd.append(jnp.where(lo, y2[0], y2[1]))
    y_off = jnp.exp(cs_l) * _dot(cm, hprev, _NT)
    st = _dot(xdt * jnp.exp(cs_l[CHUNK - 1:CHUNK, :] - cs_l), bm, _TN)
    cdec = jnp.exp(cs[CHUNK - 1:CHUNK, :])
    cd_rows = jnp.concatenate(
        [jnp.broadcast_to(cdec[:, h:h + 1], (HEAD_DIM, N_STATE)) for h in range(HEADS_PER_GROUP)], axis=0)
    dsk_l = _head_lanes(jnp.broadcast_to(dsk, (8, 128)), 2)[:1]
    y = jnp.concatenate(yd, axis=1) + y_off + xs * dsk_l
    return y, cd_rows * hprev + st


def _ssd_span(xs, bm, cm, dtr, h0, dtb, alog, dsk):
    li = lax.broadcasted_iota(jnp.int32, (CHUNK, CHUNK), 0)
    si = lax.broadcasted_iota(jnp.int32, (CHUNK, CHUNK), 1)
    causal = li >= si
    k = (causal, causal.astype(F32), si < HEAD_DIM)
    h, ys = h0, []
    for t in range(xs.shape[0] // CHUNK):
        r = slice(t * CHUNK, (t + 1) * CHUNK)
        y, h = _ssd_chunk(k, xs[r], bm[r], cm[r], dtr[r], h, dtb, alog, dsk)
        ys.append(y)
    return (ys[0] if len(ys) == 1 else jnp.concatenate(ys, axis=0)), h


def _ssd_specs(rev, nsteps, rows):
    def s_of(s):
        return nsteps - 1 - s if rev else s

    xs = pl.BlockSpec((rows, _XS_GW), lambda g, s: (s_of(s), g))
    bm = pl.BlockSpec((rows, N_STATE), lambda g, s: (s_of(s), D_INNER // N_STATE + g))
    cm = pl.BlockSpec((rows, N_STATE), lambda g, s: (s_of(s), D_INNER // N_STATE + N_GROUPS + g))
    dt = pl.BlockSpec((rows, 128), lambda g, s: (s_of(s), g))
    par = pl.BlockSpec((1, 128), lambda g, s: (0, g))
    st = pl.BlockSpec((None, None, _XS_GW, N_STATE), lambda g, s: (g, s_of(s), 0, 0))
    return xs, bm, cm, dt, par, st


def _ssd_fwd(act, dtr, dtb, alog, dsk, *, nrows):
    rows = CHUNK * min(SSD_SPAN, nrows // CHUNK)
    nsteps = nrows // rows
    xs, bm, cm, dt, par, st = _ssd_specs(False, nsteps, rows)

    def body(xs_ref, b_ref, c_ref, dt_ref, dtb_ref, al_ref, dk_ref, y_ref, st_ref, h_ref):
        @pl.when(pl.program_id(1) == 0)
        def _():
            h_ref[...] = jnp.zeros_like(h_ref)

        h0 = h_ref[...]
        st_ref[...] = h0
        y, hnew = _ssd_span(xs_ref[...].astype(F32), b_ref[...].astype(F32), c_ref[...].astype(F32), dt_ref[...],
                            h0, dtb_ref[...], al_ref[...], dk_ref[...])
        y_ref[...] = y.astype(y_ref.dtype)
        h_ref[...] = hnew

    return pl.pallas_call(
        body, name="ssd_fwd", grid=(N_GROUPS, nsteps),
        in_specs=[xs, bm, cm, dt, par, par, par], out_specs=[xs, st],
        out_shape=[jax.ShapeDtypeStruct((nrows, D_INNER), ACT_DTYPE),
                   jax.ShapeDtypeStruct((N_GROUPS, nsteps, _XS_GW, N_STATE), F32)],
        scratch_shapes=[pltpu.VMEM((_XS_GW, N_STATE), F32)],
        compiler_params=pltpu.CompilerParams(dimension_semantics=("arbitrary", "arbitrary")),
    )(act, act, act, dtr, dtb, alog, dsk)


def _ssd_bwd(act, dtr, dtb, alog, dsk, states, dy, *, nrows):
    rows = CHUNK * min(SSD_SPAN, nrows // CHUNK)
    nsteps = nrows // rows
    xs, bm, cm, dt, par, st = _ssd_specs(True, nsteps, rows)

    def body(xs_ref, b_ref, c_ref, dt_ref, dtb_ref, al_ref, dk_ref, st_ref, dy_ref,
             dxs_ref, db_ref, dc_ref, ddt_ref, ddtb_ref, dal_ref, ddk_ref, dh_ref):
        @pl.when(pl.program_id(1) == 0)
        def _():
            dh_ref[...] = jnp.zeros_like(dh_ref)
            ddtb_ref[...] = jnp.zeros_like(ddtb_ref)
            dal_ref[...] = jnp.zeros_like(dal_ref)
            ddk_ref[...] = jnp.zeros_like(ddk_ref)

        _, vjp = jax.vjp(_ssd_span, xs_ref[...].astype(F32), b_ref[...].astype(F32), c_ref[...].astype(F32),
                         dt_ref[...], st_ref[...], dtb_ref[...], al_ref[...], dk_ref[...])
        dxs, db, dc, ddt, dh, ddtb, dal, ddk = vjp((dy_ref[...].astype(F32), dh_ref[...]))
        dxs_ref[...] = dxs.astype(dxs_ref.dtype)
        db_ref[...] = db.astype(db_ref.dtype)
        dc_ref[...] = dc.astype(dc_ref.dtype)
        ddt_ref[...] = ddt
        dh_ref[...] = dh
        ddtb_ref[...] += ddtb
        dal_ref[...] += dal
        ddk_ref[...] += ddk

    nb = N_GROUPS * N_STATE
    bspec = pl.BlockSpec((rows, N_STATE), lambda g, s: (nsteps - 1 - s, g))
    return pl.pallas_call(
        body, name="ssd_bwd", grid=(N_GROUPS, nsteps),
        in_specs=[xs, bm, cm, dt, par, par, par, st, xs],
        out_specs=[xs, bspec, bspec, dt, par, par, par],
        out_shape=[jax.ShapeDtypeStruct((nrows, D_INNER), ACT_DTYPE), jax.ShapeDtypeStruct((nrows, nb), ACT_DTYPE),
                   jax.ShapeDtypeStruct((nrows, nb), ACT_DTYPE), jax.ShapeDtypeStruct((nrows, DT_W), F32),
                   jax.ShapeDtypeStruct((1, DT_W), F32), jax.ShapeDtypeStruct((1, DT_W), F32),
                   jax.ShapeDtypeStruct((1, DT_W), F32)],
        scratch_shapes=[pltpu.VMEM((_XS_GW, N_STATE), F32)],
        compiler_params=pltpu.CompilerParams(dimension_semantics=("arbitrary", "arbitrary")),
    )(act, act, act, dtr, dtb, alog, dsk, states, dy)


def _add_epilogue(acc, r):
    return r + acc


def _rms_and_skip(x, g):
    return _rms(x, g), x


def _forward_backward(x, p, tgt, w):
    s = x.shape[0]
    act_t, f32 = ACT_DTYPE, F32
    mm = functools.partial(_matmul)
    h = _rows_call("pre_norm", _f_rms, [(x, D_MODEL, 0)], [w["norm_g"]], [(D_MODEL, act_t)], tm=512, nrows=s)[0]
    proj = mm(h, w["w_main"], mode="nn", name="proj", out_dtype=act_t, m=s, n=MAIN_W, k=D_MODEL)
    dtr = mm(h, w["w_dt"], mode="nn", name="proj_dt", out_dtype=f32, m=s, n=DT_W, k=D_MODEL)
    a_pars = [w["ln_a_g"], w["ln_a_b"], w["w_s"], w["b_s"]]
    y_a = _rows_call("branch_a", _f_branch_a, [(proj, UVZ_W, UVZ_CB)], a_pars, [(E_A, act_t)], tm=256, nrows=s)[0]
    act = _conv_fwd(proj, w["conv_w"], w["conv_b"], tm=512, nrows=s)
    y, states = _ssd_fwd(act, dtr, w["dt_bias"], w["a_log"], w["d_skip"], nrows=s)
    gn_rows = [(y, D_INNER, 0), (proj, ZB_W, ZB_CB)]
    y_b = _rows_call("gnorm", _f_gnorm, gn_rows, [w["ssm_norm_g"]], [(D_INNER, act_t)], tm=512, nrows=s)[0]
    o_a = mm(y_a, w["w_oa"], mode="nn", name="out_a", out_dtype=act_t, m=s, n=D_MODEL, k=E_A)
    o_b = mm(y_b, w["w_ob"], mode="nn", name="out_b", out_dtype=act_t, m=s, n=D_MODEL, k=D_INNER)
    mg_rows = [(proj, G_W, G_CB), (o_a, D_MODEL, 0), (o_b, D_MODEL, 0)]
    merged = _rows_call("merge", _f_merge, mg_rows, [], [(D_MODEL, act_t)], tm=512, nrows=s)[0]
    x1 = mm(merged, w["w_out"], mode="nn", name="out_proj", out_dtype=f32, m=s, n=D_MODEL, k=D_MODEL,
            extras=(x,), epilogue=_add_epilogue)
    hp = _rows_call("ple_norm", _f_rms, [(x1, D_MODEL, 0)], [w["ple_norm_g"]], [(D_MODEL, act_t)], tm=512, nrows=s)[0]
    gp = mm(hp, w["w_pg"], mode="nn", name="ple_gate", out_dtype=f32, m=s, n=D_MODEL, k=D_MODEL)
    pe = mm(p, w["w_ple"], mode="nn", name="ple_proj", out_dtype=f32, m=s, n=D_MODEL, k=PLE_DIM)
    g = {}
    dx2, dgp, dpe, g["final_g"], loss = _head(x1, gp, pe, tgt, w["final_g"], tm=256, nrows=s)
    g["w_pg"] = mm(hp, dgp, mode="tn", name="d_w_pg", out_dtype=f32, m=D_MODEL, n=D_MODEL, k=s)
    g["w_ple"] = mm(p, dpe, mode="tn", name="d_w_ple", out_dtype=f32, m=PLE_DIM, n=D_MODEL, k=s)
    dhp = mm(dgp, w["w_pg"], mode="nt", name="d_hp", out_dtype=act_t, m=s, n=D_MODEL, k=D_MODEL)
    dx1, g["ple_norm_g"] = _rows_vjp_call(
        "ple_norm_bwd", _rms_and_skip, [(x1, D_MODEL, 0)], [w["ple_norm_g"]], [(dhp, D_MODEL, 0), (dx2, D_MODEL, 0)],
        [(f32, None)], tm=512, nrows=s)
    dmerged = mm(dx1, w["w_out"], mode="nt", name="d_merged", out_dtype=act_t, m=s, n=D_MODEL, k=D_MODEL)
    g["w_out"] = mm(merged, dx1, mode="tn", name="d_w_out", out_dtype=f32, m=D_MODEL, n=D_MODEL, k=s)
    dproj, do_a, do_b = _rows_vjp_call(
        "merge_bwd", _f_merge, mg_rows, [], [(dmerged, D_MODEL, 0)],
        [(act_t, (None, MAIN_W, G_CB)), (act_t, None), (act_t, None)], tm=512, nrows=s)
    dy_a = mm(do_a, w["w_oa"], mode="nt", name="d_y_a", out_dtype=act_t, m=s, n=E_A, k=D_MODEL)
    g["w_oa"] = mm(y_a, do_a, mode="tn", name="d_w_oa", out_dtype=f32, m=E_A, n=D_MODEL, k=s)
    dy_b = mm(do_b, w["w_ob"], mode="nt", name="d_y_b", out_dtype=act_t, m=s, n=D_INNER, k=D_MODEL)
    g["w_ob"] = mm(y_b, do_b, mode="tn", name="d_w_ob", out_dtype=f32, m=D_INNER, n=D_MODEL, k=s)
    dy, dproj, g["ssm_norm_g"] = _rows_vjp_call(
        "gnorm_bwd", _f_gnorm, gn_rows, [w["ssm_norm_g"]], [(dy_b, D_INNER, 0)],
        [(act_t, None), (act_t, (dproj, MAIN_W, ZB_CB))], tm=256, nrows=s)
    dxs, dbm, dcm, ddtr, g["dt_bias"], g["a_log"], g["d_skip"] = _ssd_bwd(
        act, dtr, w["dt_bias"], w["a_log"], w["d_skip"], states, dy, nrows=s)
    dpre, g["conv_w"], g["conv_b"] = _conv_bwd_pre(proj, w["conv_w"], w["conv_b"], (dxs, dbm, dcm), tm=512, nrows=s)
    dproj = _conv_bwd_x(dpre, w["conv_w"], dproj, tm=512, nrows=s)
    dproj, g["ln_a_g"], g["ln_a_b"], g["w_s"], g["b_s"] = _rows_vjp_call(
        "branch_a_bwd", _f_branch_a, [(proj, UVZ_W, UVZ_CB)], a_pars, [(dy_a, E_A, 0)],
        [(act_t, (dproj, MAIN_W, UVZ_CB))], tm=128, nrows=s)
    g["w_main"] = mm(h, dproj, mode="tn", name="d_w_main", out_dtype=f32, m=D_MODEL, n=MAIN_W, k=s)
    g["w_dt"] = mm(h, ddtr, mode="tn", name="d_w_dt", out_dtype=f32, m=D_MODEL, n=DT_W, k=s)
    return loss, g, (dproj, ddtr, dx1)


def _input_grad(x, w, ctx, after=()):
    dproj, ddtr, dx1 = ctx
    s = x.shape[0]
    dh = _matmul(dproj, w["w_main"], mode="nt", name="d_h_main", out_dtype=F32, m=s, n=D_MODEL, k=MAIN_W, after=after)
    dh = _matmul(ddtr, w["w_dt"], mode="nt", name="d_h", out_dtype=F32, m=s, n=D_MODEL, k=DT_W,
                 extras=(dh,), epilogue=_add_epilogue)
    return _rows_vjp_call(
        "pre_norm_bwd", _rms_and_skip, [(x, D_MODEL, 0)], [w["norm_g"]], [(dh, D_MODEL, 0), (dx1, D_MODEL, 0)],
        [(F32, None)], tm=512, nrows=s)


def _local_step(x, p, tgt, w):
    loss, g, ctx = _forward_backward(x, p, tgt, w)
    grad_x, g["norm_g"] = _input_grad(x, w, ctx)
    return loss, grad_x, g


_O_ZB = 3 * E_A
_O_XBC = _O_ZB + D_INNER
_O_DT = _O_XBC + CONV_DIM
_O_G = _O_DT + N_HEADS


def _heads_to_lanes(v):
    r = v.shape[0]
    v = v.reshape(r, N_GROUPS, HEADS_PER_GROUP)
    return jnp.pad(v, ((0, 0), (0, 0), (0, 128 - HEADS_PER_GROUP))).reshape(r, DT_W)


def _lanes_to_heads(v):
    r = v.shape[0]
    return v.reshape(r, N_GROUPS, 128)[:, :, :HEADS_PER_GROUP].reshape(r, N_HEADS)


def _block_cols(blocks, a, b):
    parts = []
    for k in range(N_CHIPS):
        lo, hi = max(a, k * W_IN_BLOCK), min(b, (k + 1) * W_IN_BLOCK)
        if lo < hi:
            parts.append(blocks[k][:, lo - k * W_IN_BLOCK:hi - k * W_IN_BLOCK])
    return parts


_W_IN_SEGMENTS = ((0, _O_ZB, "m", 0), (_O_ZB, _O_XBC, "m", UVZ_W + XBC_W), (_O_XBC, _O_DT, "m", UVZ_W),
                  (_O_DT, _O_G, "d", 0), (_O_G, N_IN, "m", MAIN_W - G_W))


def _w_in_grad_blocks(gm, gdt):
    blocks = []
    for k in range(N_CHIPS):
        a, b = k * W_IN_BLOCK, (k + 1) * W_IN_BLOCK
        parts = []
        for s, e, src, off in _W_IN_SEGMENTS:
            lo, hi = max(a, s), min(b, e)
            if lo < hi:
                parts.append((gm if src == "m" else gdt)[:, off + lo - s:off + hi - s])
        blocks.append(jnp.concatenate(parts, axis=1))
    return jnp.stack(blocks)


def _layout_weights(f, w_in_blocks=None):
    w = dict(f)
    if w_in_blocks is None:
        w_in = w.pop("w_in")
        w_in_blocks = jnp.stack([w_in[:, k * W_IN_BLOCK:(k + 1) * W_IN_BLOCK] for k in range(N_CHIPS)])
    cols = functools.partial(_block_cols, w_in_blocks)
    w["w_main"] = jnp.concatenate(cols(0, _O_ZB) + cols(_O_XBC, _O_DT) + cols(_O_ZB, _O_XBC) + cols(_O_G, N_IN), axis=1)
    w["w_dt"] = _heads_to_lanes(jnp.concatenate(cols(_O_DT, _O_G), axis=1))
    w["b_s"] = f["b_s"].reshape(G_A, CHUNK, 1)
    for n in ("dt_bias", "a_log", "d_skip"):
        w[n] = _heads_to_lanes(f[n])
    return w


def _natural_grads(g):
    out = dict(g)
    gm = out.pop("w_main")
    gdt = _lanes_to_heads(out.pop("w_dt"))
    out["w_in"] = jnp.concatenate(
        [gm[:, :UVZ_W], gm[:, UVZ_W + XBC_W:UVZ_W + XBC_W + ZB_W], gm[:, UVZ_W:UVZ_W + XBC_W], gdt, gm[:, MAIN_W - G_W:]],
        axis=1)
    out["b_s"] = g["b_s"].reshape(G_A, CHUNK)
    for n in ("dt_bias", "a_log", "d_skip"):
        out[n] = _lanes_to_heads(g[n])
    return out


def _place():
    return lax.axis_index("x"), lax.axis_index("y"), lax.axis_index("c")


def _other_chips(x, y):
    return [(1 - x, y), (x, 1 - y), (1 - x, 1 - y)]


def _rcopy(src, dst, ssem, rsem, dev):
    return pltpu.make_async_remote_copy(src_ref=src, dst_ref=dst, send_sem=ssem, recv_sem=rsem,
                                        device_id=dev, device_id_type=MESH)


def _half(ref_rows, half):
    hs = ref_rows // 2
    return pl.ds(pl.multiple_of(half * hs, 16), hs)


def _gather_weights(shards, conv_shard):
    nw = len(shards)

    def body(*refs):
        sh, cv = refs[:nw], refs[nw]
        out, cvo = refs[nw + 1:2 * nw + 1], refs[2 * nw + 1]
        ici_s, ici_r, fw_s, fw_r, own_s, own_r, cv_s, cv_r = refs[2 * nw + 2:]
        x, y, c = _place()
        me, sib, chips = 2 * x + y, (x, y, 1 - c), _other_chips(x, y)
        own = [_rcopy(sh[w], out[w].at[me], own_s.at[w], own_r.at[w], sib) for w in range(nw)]
        own.append(_rcopy(cv, cvo.at[me], own_s.at[nw], own_r.at[nw], sib))
        for cp in own:
            cp.start()
        sends = []
        for w in range(nw):
            mine = _half(sh[w].shape[0], c)
            for j, chip in enumerate(chips):
                sends.append(_rcopy(sh[w].at[mine], out[w].at[me, mine], ici_s.at[3 * w + j], ici_r.at[3 * w + j], (*chip, c)))
        for j, chip in enumerate(chips):
            sends.append(_rcopy(cv, cvo.at[me], cv_s.at[j], cv_r.at[j], (*chip, c)))
        for cp in sends:
            cp.start()
        for w in range(nw):
            mine = _half(sh[w].shape[0], c)
            for j, chip in enumerate(chips):
                slab = out[w].at[2 * chip[0] + chip[1], mine]
                _rcopy(slab, slab, ici_s.at[3 * w + j], ici_r.at[3 * w + j], (*chip, c)).wait_recv()
                fwd = _rcopy(slab, slab, fw_s.at[3 * w + j], fw_r.at[3 * w + j], sib)
                fwd.start()
                sends.append(fwd)
        for j, chip in enumerate(chips):
            blk = cvo.at[2 * chip[0] + chip[1]]
            _rcopy(blk, blk, cv_s.at[j], cv_r.at[j], (*chip, c)).wait_recv()
        for w in range(nw):
            theirs = _half(sh[w].shape[0], 1 - c)
            for j, chip in enumerate(chips):
                slab = out[w].at[2 * chip[0] + chip[1], theirs]
                _rcopy(slab, slab, fw_s.at[3 * w + j], fw_r.at[3 * w + j], sib).wait_recv()
        for cp in sends:
            cp.wait_send()
        for cp in own:
            cp.wait()

    dma = pltpu.SemaphoreType.DMA
    return pl.pallas_call(
        body, name="gather_weights",
        in_specs=[ANY] * (nw + 1), out_specs=[ANY] * (nw + 1),
        out_shape=[jax.ShapeDtypeStruct((N_CHIPS,) + s.shape, s.dtype) for s in shards]
        + [jax.ShapeDtypeStruct((N_CHIPS,) + conv_shard.shape, conv_shard.dtype)],
        scratch_shapes=[dma((3 * nw,)), dma((3 * nw,)), dma((3 * nw,)), dma((3 * nw,)), dma((nw + 1,)), dma((nw + 1,)),
                        dma((3,)), dma((3,))],
    )(*shards, conv_shard)


def _swap_with_sibling(arrs):
    n = len(arrs)

    def body(*refs):
        src, dst, s_sem, r_sem = refs[:n], refs[n:2 * n], refs[2 * n], refs[2 * n + 1]
        x, y, c = _place()
        cps = [_rcopy(src[i], dst[i], s_sem.at[i], r_sem.at[i], (x, y, 1 - c)) for i in range(n)]
        for cp in cps:
            cp.start()
        for cp in cps:
            cp.wait()

    dma = pltpu.SemaphoreType.DMA
    return pl.pallas_call(
        body, name="swap_with_sibling", in_specs=[ANY] * n, out_specs=[ANY] * n,
        out_shape=[jax.ShapeDtypeStruct(a.shape, a.dtype) for a in arrs], scratch_shapes=[dma((n,)), dma((n,))],
    )(*arrs)


_HBM = pl.BlockSpec(memory_space=pltpu.HBM)
_SEM = pl.BlockSpec(memory_space=pltpu.SEMAPHORE)
_EFFECT = pltpu.SideEffectType.DATAFLOW_SIDE_EFFECTING


def _scatter_copies(src, land, s_sem, r_sem):
    x, y, c = _place()
    return [_rcopy(src[i].at[2 * chip[0] + chip[1]], land[i].at[j], s_sem.at[3 * i + j], r_sem.at[3 * i + j], (*chip, c))
            for i in range(len(src)) for j, chip in enumerate(_other_chips(x, y))]


def _scatter_blocks_start(arrs):
    n = len(arrs)
    lands = [lax.empty((3,) + a.shape[1:], a.dtype) for a in arrs]

    def body(*refs):
        src, land, s_sem, r_sem, token = refs[:n], refs[n:2 * n], refs[2 * n], refs[2 * n + 1], refs[-1]
        for cp in _scatter_copies(src, land, s_sem, r_sem):
            cp.start()
        token[...] = jnp.zeros_like(token)

    dma = pltpu.SemaphoreType.DMA
    hbm = [pltpu.with_memory_space_constraint(a, pltpu.HBM) for a in list(arrs) + lands]
    out = pl.pallas_call(
        body, name="scatter_blocks_start",
        out_shape=[dma((3 * n,)), dma((3 * n,))] + [pltpu.HBM(a.shape, a.dtype) for a in hbm]
        + [jax.ShapeDtypeStruct((8, 128), F32)],
        in_specs=[_HBM] * (2 * n), out_specs=[_SEM, _SEM] + [_HBM] * (2 * n) + [pl.BlockSpec(memory_space=pltpu.VMEM)],
        input_output_aliases={i: 2 + i for i in range(2 * n)},
        compiler_params=pltpu.CompilerParams(has_side_effects=_EFFECT),
    )(*hbm)
    return out[0], out[1], out[2:2 + n], out[2 + n:2 + 2 * n], out[-1]


def _scatter_blocks_wait(s_sem, r_sem, srcs, lands, after):
    n = len(srcs)

    def body(*refs):
        src, land, s_sem, r_sem = refs[:n], refs[n:2 * n], refs[2 * n], refs[2 * n + 1]
        for cp in _scatter_copies(src, land, s_sem, r_sem):
            cp.wait_send()
            cp.wait_recv()

    out = pl.pallas_call(
        body, name="scatter_blocks_wait",
        out_shape=[pltpu.HBM(a.shape, a.dtype) for a in list(srcs) + list(lands)],
        in_specs=[_HBM] * (2 * n) + [_SEM, _SEM, ANY], out_specs=[_HBM] * (2 * n),
        input_output_aliases={i: i for i in range(2 * n)},
        compiler_params=pltpu.CompilerParams(has_side_effects=_EFFECT),
    )(*srcs, *lands, s_sem, r_sem, after)
    return out[n:]


def _share_halves(arrs):
    n = len(arrs)

    def body(*refs):
        buf, s_sem, r_sem = refs[n:2 * n], refs[2 * n], refs[2 * n + 1]
        x, y, c = _place()
        cps = []
        for i in range(n):
            mine = buf[i].at[_half(buf[i].shape[0], c)]
            cps.append(_rcopy(mine, mine, s_sem.at[i], r_sem.at[i], (x, y, 1 - c)))
        for cp in cps:
            cp.start()
        for i in range(n):
            theirs = buf[i].at[_half(buf[i].shape[0], 1 - c)]
            _rcopy(theirs, theirs, s_sem.at[i], r_sem.at[i], (x, y, 1 - c)).wait_recv()
        for cp in cps:
            cp.wait_send()

    dma = pltpu.SemaphoreType.DMA
    return pl.pallas_call(
        body, name="share_halves", in_specs=[ANY] * n, out_specs=[ANY] * n,
        out_shape=[jax.ShapeDtypeStruct(a.shape, a.dtype) for a in arrs],
        input_output_aliases={i: i for i in range(n)}, scratch_shapes=[dma((n,)), dma((n,))],
    )(*arrs)


def _allreduce_small(packed):
    rows = packed.shape[0]

    def body(p_ref, o_ref, buf, s_sem, r_sem):
        x, y, c = _place()
        me = 4 * x + 2 * y + c
        buf[me] = p_ref[...]
        cps = []
        for d in range(1, N_DEV):
            px, py, pc = ((1 - x) if d & 4 else x), ((1 - y) if d & 2 else y), ((1 - c) if d & 1 else c)
            cps.append(_rcopy(p_ref, buf.at[me], s_sem.at[d - 1], r_sem.at[d - 1], (px, py, pc)))
        for cp in cps:
            cp.start()
        for d in range(1, N_DEV):
            px, py, pc = ((1 - x) if d & 4 else x), ((1 - y) if d & 2 else y), ((1 - c) if d & 1 else c)
            landed = buf.at[4 * px + 2 * py + pc]
            _rcopy(p_ref, landed, s_sem.at[d - 1], r_sem.at[d - 1], (px, py, pc)).wait_recv()
        for cp in cps:
            cp.wait_send()
        acc = buf[0]
        for d in range(1, N_DEV):
            acc = acc + buf[d]
        o_ref[...] = acc

    dma = pltpu.SemaphoreType.DMA
    return pl.pallas_call(
        body, name="allreduce_small", out_shape=jax.ShapeDtypeStruct(packed.shape, F32),
        scratch_shapes=[pltpu.VMEM((N_DEV, rows, 128), F32), dma((N_DEV - 1,)), dma((N_DEV - 1,))],
    )(packed)


def _row_tile(rows, cols):
    tr = max(8, min(rows, (1 << 20) // (4 * cols) // 8 * 8))
    while rows % tr:
        tr -= 8
    return tr


def _chip_sum(name, g5, recv, c_arr):
    _, _, hs, cols = g5.shape
    tr = _row_tile(hs, cols)

    def body(_, a_ref, b_ref, o32_ref, ow_ref):
        s = a_ref[...] + b_ref[...].astype(F32)
        o32_ref[...] = s
        ow_ref[...] = s.astype(ow_ref.dtype)

    blk = pl.BlockSpec((None, tr, cols), lambda b, i, c: (b, i, 0))
    return pl.pallas_call(
        body, name=name,
        grid_spec=pltpu.PrefetchScalarGridSpec(
            num_scalar_prefetch=1, grid=(N_CHIPS, hs // tr),
            in_specs=[pl.BlockSpec((None, None, tr, cols), lambda b, i, c: (b, c[0], i, 0)), blk], out_specs=[blk, blk]),
        out_shape=[jax.ShapeDtypeStruct((N_CHIPS, hs, cols), F32), jax.ShapeDtypeStruct((N_CHIPS, hs, cols), WIRE_DTYPE)],
    )(c_arr, g5, recv)


def _final_sum(name, own, recv, place_arr):
    _, hs, cols = own.shape
    tr = _row_tile(hs, cols)
    nt = hs // tr

    def body(_, a_ref, r_ref, o_ref):
        o_ref[...] = ((a_ref[...] + r_ref[0].astype(F32)) + r_ref[1].astype(F32)) + r_ref[2].astype(F32)

    return pl.pallas_call(
        body, name=name,
        grid_spec=pltpu.PrefetchScalarGridSpec(
            num_scalar_prefetch=1, grid=(nt,),
            in_specs=[pl.BlockSpec((None, tr, cols), lambda i, m: (m[0], i, 0)),
                      pl.BlockSpec((3, tr, cols), lambda i, m: (0, i, 0))],
            out_specs=pl.BlockSpec((tr, cols), lambda i, m: (m[1] * nt + i, 0))),
        out_shape=jax.ShapeDtypeStruct((2 * hs, cols), F32),
    )(place_arr, own, recv)


def _adamw(w, g, m, v):
    m = ADAM_B1 * m + (1.0 - ADAM_B1) * g
    v = ADAM_B2 * v + (1.0 - ADAM_B2) * (g * g)
    m_hat = m / (1.0 - ADAM_B1 ** ADAM_STEP)
    v_hat = v / (1.0 - ADAM_B2 ** ADAM_STEP)
    return -ADAM_LR * (m_hat / (jnp.sqrt(v_hat) + ADAM_EPS) + ADAM_WD * w), m, v


def _adamw_call(name, w, g, m, v):
    rows, cols = w.shape
    tr = _row_tile(rows, cols)

    def body(w_ref, g_ref, m_ref, v_ref, d_ref, nm_ref, nv_ref):
        d_ref[...], nm_ref[...], nv_ref[...] = _adamw(w_ref[...], g_ref[...], m_ref[...], v_ref[...])

    blk = pl.BlockSpec((tr, cols), lambda i: (i, 0))
    return pl.pallas_call(
        body, name=name, grid=(rows // tr,), in_specs=[blk] * 4, out_specs=[blk] * 3,
        out_shape=[jax.ShapeDtypeStruct(w.shape, F32)] * 3,
        compiler_params=pltpu.CompilerParams(dimension_semantics=("parallel",)),
    )(w, g, m, v)


def _adamw_small(ws, gs, ms, vs):
    n = len(ws)

    def body(*refs):
        for i in range(n):
            w_ref, g_ref, m_ref, v_ref = (refs[k * n + i] for k in range(4))
            d, nm, nv = _adamw(w_ref[...], g_ref[...], m_ref[...], v_ref[...])
            refs[4 * n + i][...] = d
            refs[5 * n + i][...] = nm
            refs[6 * n + i][...] = nv

    out = pl.pallas_call(
        body, name="adamw_small", out_shape=[jax.ShapeDtypeStruct(a.shape, F32) for a in ws] * 3,
    )(*ws, *gs, *ms, *vs)
    return out[:n], out[n:2 * n], out[2 * n:]


_BIG = ("w_in", "w_oa", "w_ob", "w_out", "w_pg", "w_ple")
_SMALL = ("norm_g", "ln_a_g", "ln_a_b", "w_s", "b_s", "conv_w", "conv_b", "dt_bias", "a_log", "d_skip", "ssm_norm_g",
          "ple_norm_g", "final_g")
_WEIGHTS = ("norm_g", "w_in", "ln_a_g", "ln_a_b", "w_s", "b_s", "conv_w", "conv_b", "dt_bias", "a_log", "d_skip",
            "ssm_norm_g", "w_oa", "w_ob", "w_out", "ple_norm_g", "w_pg", "w_ple", "final_g")
_COL_SHARDED = ("w_in", "w_ple")
_PACK = 1024


def _blocks_to_full(col_sharded, blocks):
    if col_sharded:
        return jnp.concatenate([blocks[k] for k in range(N_CHIPS)], axis=1)
    return blocks.reshape(N_CHIPS * blocks.shape[1], blocks.shape[2])


def _full_to_blocks(col_sharded, full):
    if col_sharded:
        w = full.shape[1] // N_CHIPS
        return jnp.stack([full[:, k * w:(k + 1) * w] for k in range(N_CHIPS)])
    return full.reshape(N_CHIPS, full.shape[0] // N_CHIPS, full.shape[1])


def _two_d(n, a):
    if n == "w_s":
        return a.reshape(G_A * CHUNK, CHUNK)
    if n in ("b_s", "conv_w"):
        return a.reshape(a.shape[-2], a.shape[-1])
    return a.reshape(1, a.shape[-1])


def kernel(x, p, norm_g, w_in, ln_a_g, ln_a_b, w_s, b_s, conv_w, conv_b, dt_bias, a_log, d_skip, ssm_norm_g, w_oa, w_ob, w_out, ple_norm_g, w_pg, w_ple, final_g, loss_target, m_norm_g, m_w_in, m_ln_a_g, m_ln_a_b, m_w_s, m_b_s, m_conv_w, m_conv_b, m_dt_bias, m_a_log, m_d_skip, m_ssm_norm_g, m_w_oa, m_w_ob, m_w_out, m_ple_norm_g, m_w_pg, m_w_ple, m_final_g, v_norm_g, v_w_in, v_ln_a_g, v_ln_a_b, v_w_s, v_b_s, v_conv_w, v_conv_b, v_dt_bias, v_a_log, v_d_skip, v_ssm_norm_g, v_w_oa, v_w_ob, v_w_out, v_ple_norm_g, v_w_pg, v_w_ple, v_final_g):
    wt = dict(norm_g=norm_g, w_in=w_in, ln_a_g=ln_a_g, ln_a_b=ln_a_b, w_s=w_s, b_s=b_s, conv_w=conv_w, conv_b=conv_b,
              dt_bias=dt_bias, a_log=a_log, d_skip=d_skip, ssm_norm_g=ssm_norm_g, w_oa=w_oa, w_ob=w_ob, w_out=w_out,
              ple_norm_g=ple_norm_g, w_pg=w_pg, w_ple=w_ple, final_g=final_g)
    mom = dict(norm_g=m_norm_g, w_in=m_w_in, ln_a_g=m_ln_a_g, ln_a_b=m_ln_a_b, w_s=m_w_s, b_s=m_b_s, conv_w=m_conv_w,
               conv_b=m_conv_b, dt_bias=m_dt_bias, a_log=m_a_log, d_skip=m_d_skip, ssm_norm_g=m_ssm_norm_g, w_oa=m_w_oa,
               w_ob=m_w_ob, w_out=m_w_out, ple_norm_g=m_ple_norm_g, w_pg=m_w_pg, w_ple=m_w_ple, final_g=m_final_g)
    vel = dict(norm_g=v_norm_g, w_in=v_w_in, ln_a_g=v_ln_a_g, ln_a_b=v_ln_a_b, w_s=v_w_s, b_s=v_b_s, conv_w=v_conv_w,
               conv_b=v_conv_b, dt_bias=v_dt_bias, a_log=v_a_log, d_skip=v_d_skip, ssm_norm_g=v_ssm_norm_g, w_oa=v_w_oa,
               w_ob=v_w_ob, w_out=v_w_out, ple_norm_g=v_ple_norm_g, w_pg=v_w_pg, w_ple=v_w_ple, final_g=v_final_g)
    xi, yi, ci = _place()
    me = 2 * xi + yi
    c_arr = jnp.reshape(ci, (1,)).astype(jnp.int32)
    place_arr = jnp.stack([me, ci]).astype(jnp.int32)

    shard = {n: wt[n][0] for n in _BIG}
    blocks = _gather_weights([shard[n].astype(WIRE_DTYPE) for n in _BIG], conv_w[0])
    full = {n: _blocks_to_full(n in _COL_SHARDED, b) for n, b in zip(_BIG[1:], blocks[1:-1])}
    full["conv_w"] = _blocks_to_full(True, blocks[-1])
    for n in _SMALL:
        if n != "conv_w":
            full[n] = wt[n][0] if wt[n].ndim > 2 else wt[n].reshape(1, wt[n].shape[-1])

    w = _layout_weights(full, w_in_blocks=blocks[0])
    loss_row, g, ctx = _forward_backward(x[0], p[0, 0], loss_target[0], w)
    loss = lax.psum(loss_row[0, 0], ("x", "y", "c"))

    big = {n: _full_to_blocks(n in _COL_SHARDED, g[n]) for n in _BIG[1:]}
    big["w_in"] = _w_in_grad_blocks(g["w_main"], _lanes_to_heads(g["w_dt"]))
    g5 = {n: big[n].reshape(N_CHIPS, 2, big[n].shape[1] // 2, big[n].shape[2]) for n in _BIG}
    to_sibling = [lax.dynamic_index_in_dim(g5[n], 1 - ci, axis=1, keepdims=False).astype(WIRE_DTYPE) for n in _BIG]
    from_sibling = _swap_with_sibling(to_sibling)
    chip32, chip_wire = zip(*[_chip_sum("chip_sum_" + n, g5[n], r, c_arr) for n, r in zip(_BIG, from_sibling)])
    s_sem, r_sem, sent, lands, token = _scatter_blocks_start(list(chip_wire))
    grad_x, g["norm_g"] = _input_grad(x[0], w, ctx, after=(token,))
    from_chips = _scatter_blocks_wait(s_sem, r_sem, sent, lands, grad_x)
    halves = [_final_sum("final_sum_" + n, a, r, place_arr) for n, a, r in zip(_BIG, chip32, from_chips)]
    grads = dict(zip(_BIG, _share_halves(halves)))
    g = _natural_grads(g)

    pieces = [_two_d(n, g[n]).reshape(-1) for n in _SMALL]
    sizes = [v.shape[0] for v in pieces]
    padded = [-(-s // _PACK) * _PACK for s in sizes]
    packed = jnp.concatenate([jnp.pad(v, (0, ps - s)) for v, s, ps in zip(pieces, sizes, padded)]).reshape(-1, 128)
    summed = _allreduce_small(packed).reshape(-1)
    off = 0
    for n, s, ps in zip(_SMALL, sizes, padded):
        grads[n] = _two_d(n, g[n]).shape, summed[off:off + s]
        off += ps
    for n in _SMALL:
        shape, flat = grads[n]
        grads[n] = flat.reshape(shape)
    grads["conv_w"] = lax.dynamic_slice_in_dim(grads["conv_w"], me * (CONV_DIM // N_CHIPS), CONV_DIM // N_CHIPS, axis=1)

    delta, new_m, new_v = {}, {}, {}
    for n in _BIG:
        delta[n], new_m[n], new_v[n] = _adamw_call("adamw_" + n, shard[n], grads[n], mom[n][0], vel[n][0])
    small = _adamw_small([_two_d(n, wt[n]) for n in _SMALL], [grads[n] for n in _SMALL],
                         [_two_d(n, mom[n]) for n in _SMALL], [_two_d(n, vel[n]) for n in _SMALL])
    for i, n in enumerate(_SMALL):
        delta[n], new_m[n], new_v[n] = small[0][i], small[1][i], small[2][i]

    def shaped(d):
        return [d[n].reshape(wt[n].shape) for n in _WEIGHTS]

    return (loss, grad_x[None], *shaped(grads), *shaped(delta), *shaped(new_m), *shaped(new_v))
```

```python
import functools

import jax
import jax.numpy as jnp
from jax import lax
from jax.experimental import pallas as pl
from jax.experimental.pallas import tpu as pltpu

F32 = jnp.float32
MXU_DTYPE = jnp.bfloat16
ACT_DTYPE = jnp.bfloat16
WIRE_DTYPE = jnp.bfloat16

D_MODEL = 1024
PLE_DIM = 256
CHUNK = 128
EPS = 1e-6
E_A = D_MODEL
G_A = 4
D_INNER = 2 * D_MODEL
HEAD_DIM = 64
N_HEADS = D_INNER // HEAD_DIM
N_STATE = 128
N_GROUPS = 4
HEADS_PER_GROUP = N_HEADS // N_GROUPS
PAIRS_PER_GROUP = HEADS_PER_GROUP // 2
CONV_K = 4
CONV_DIM = D_INNER + 2 * N_GROUPS * N_STATE
N_IN = 3 * E_A + D_INNER + CONV_DIM + N_HEADS + 2 * D_MODEL
N_CHIPS = 4
N_DEV = 8
W_IN_BLOCK = N_IN // N_CHIPS

UVZ_W, XBC_W, ZB_W, G_W = 3 * E_A, CONV_DIM, D_INNER, 2 * D_MODEL
MAIN_W = UVZ_W + XBC_W + ZB_W + G_W
UVZ_CB, XBC_CB, ZB_CB, G_CB = 0, 1, 3, 4
DT_W = N_GROUPS * 128

ADAM_LR, ADAM_B1, ADAM_B2, ADAM_EPS, ADAM_WD, ADAM_STEP = 0.001, 0.9, 0.999, 1e-08, 0.01, 10

MESH = pl.DeviceIdType.MESH
ANY = pl.BlockSpec(memory_space=pl.ANY)


def _mxu(v):
    return v.astype(MXU_DTYPE)


def _dot(a, b, dims=(((1,), (0,)), ((), ()))):
    return lax.dot_general(_mxu(a), _mxu(b), dims, preferred_element_type=F32)


MM_TILE = 1024
MM_VMEM_BUDGET = 46 << 20


def _mm_tk(m, n, k, tm, tn, a_bytes, b_bytes, out_bytes, extra_bytes):
    for parts in range(1, k // 128 + 1):
        if k % parts or (k // parts) % 128 and parts > 1:
            continue
        tk = k // parts
        need = 2 * tk * (tm * a_bytes + tn * b_bytes) + 2 * tm * tn * (out_bytes + extra_bytes) + (tm * tn * 4 if parts > 1 else 0)
        if need <= MM_VMEM_BUDGET:
            return tk
    return 128


def _matmul(a, b, *, mode, name, out_dtype, m, n, k, tm=MM_TILE, tn=MM_TILE, tk=None, a_off=0, b_off=0,
            extras=(), epilogue=None, after=()):
    tm, tn = min(tm, m), min(tn, n)
    if tk is None:
        tk = _mm_tk(m, n, k, tm, tn, a.dtype.itemsize, b.dtype.itemsize, jnp.dtype(out_dtype).itemsize,
                    sum(e.dtype.itemsize for e in extras))
    tk = min(tk, k)
    assert m % tm == 0 and n % tn == 0 and k % tk == 0, (name, m, n, k, tm, tn, tk)
    nk = k // tk
    if mode == "nn":
        assert a_off % tk == 0 and b_off % tn == 0
        a_spec = pl.BlockSpec((tm, tk), lambda i, j, kk: (i, kk + a_off // tk))
        b_spec = pl.BlockSpec((tk, tn), lambda i, j, kk: (kk, j + b_off // tn))
        dims = (((1,), (0,)), ((), ()))
    elif mode == "nt":
        a_spec = pl.BlockSpec((tm, tk), lambda i, j, kk: (i, kk))
        b_spec = pl.BlockSpec((tn, tk), lambda i, j, kk: (j, kk))
        dims = (((1,), (1,)), ((), ()))
    else:
        assert a_off % tm == 0 and b_off % tn == 0
        a_spec = pl.BlockSpec((tk, tm), lambda i, j, kk: (kk, i + a_off // tm))
        b_spec = pl.BlockSpec((tk, tn), lambda i, j, kk: (kk, j + b_off // tn))
        dims = (((0,), (0,)), ((), ()))
    ne = len(extras)

    def finish(acc, extra_refs, o_ref):
        res = acc if epilogue is None else epilogue(acc, *[e[...] for e in extra_refs])
        o_ref[...] = res.astype(o_ref.dtype)

    def body(a_ref, b_ref, *rest):
        extra_refs, o_ref = rest[:ne], rest[ne + len(after)]
        part = _dot(a_ref[...], b_ref[...], dims)
        if nk == 1:
            finish(part, extra_refs, o_ref)
            return
        acc_ref = rest[ne + len(after) + 1]
        kk = pl.program_id(2)

        @pl.when(kk == 0)
        def _():
            acc_ref[...] = part

        @pl.when(kk > 0)
        def _():
            acc_ref[...] += part

        @pl.when(kk == nk - 1)
        def _():
            finish(acc_ref[...], extra_refs, o_ref)

    o_spec = pl.BlockSpec((tm, tn), lambda i, j, kk: (i, j))
    return pl.pallas_call(
        body, name=name, grid=(m // tm, n // tn, nk),
        in_specs=[a_spec, b_spec] + [o_spec] * ne + [ANY] * len(after), out_specs=o_spec,
        out_shape=jax.ShapeDtypeStruct((m, n), out_dtype),
        scratch_shapes=[pltpu.VMEM((tm, tn), F32)] if nk > 1 else [],
        compiler_params=pltpu.CompilerParams(dimension_semantics=("parallel", "parallel", "arbitrary")),
    )(a, b, *extras, *after)


def _rows_matmul(name, f, rows, pars, b, *, out_dtype, n, k, tm, nrows, tn=MM_TILE, extras=(), epilogue=None):
    tm, tn = min(tm, nrows), min(tn, n)
    assert nrows % tm == 0 and n % tn == 0, (name, nrows, n, tm, tn)
    nr, npar, ne, nj = len(rows), len(pars), len(extras), n // tn

    def body(*refs):
        row_refs, par_refs, b_ref = refs[:nr], refs[nr:nr + npar], refs[nr + npar]
        extra_refs = refs[nr + npar + 1:nr + npar + 1 + ne]
        a_ref, o_ref = refs[nr + npar + 1 + ne], refs[nr + npar + 2 + ne]

        def make_a():
            a = f(*[r[...].astype(F32) for r in row_refs], *[p[...] for p in par_refs])[0]
            a_ref[...] = a.astype(a_ref.dtype)

        if nj == 1:
            make_a()
        else:
            pl.when(pl.program_id(1) == 0)(make_a)
        res = _dot(a_ref[...], b_ref[...])
        if epilogue is not None:
            res = epilogue(res, *[e[...] for e in extra_refs])
        o_ref[...] = res.astype(o_ref.dtype)

    o_spec = pl.BlockSpec((tm, tn), lambda i, j: (i, j))
    return pl.pallas_call(
        body, name=name, grid=(nrows // tm, nj),
        in_specs=[pl.BlockSpec((tm, w), lambda i, j, cb=cb: (i, cb)) for _, w, cb in rows]
        + [pl.BlockSpec(tuple(p.shape), lambda i, j, nd=p.ndim: (0,) * nd) for p in pars]
        + [pl.BlockSpec((k, tn), lambda i, j: (0, j))] + [o_spec] * ne,
        out_specs=[pl.BlockSpec((tm, k), lambda i, j: (i, 0)), o_spec],
        out_shape=[jax.ShapeDtypeStruct((nrows, k), ACT_DTYPE), jax.ShapeDtypeStruct((nrows, n), out_dtype)],
        compiler_params=pltpu.CompilerParams(dimension_semantics=("parallel", "arbitrary")),
    )(*[r[0] for r in rows], *pars, b, *extras)


def _row_spec(tm, width, cb):
    return pl.BlockSpec((tm, width), lambda i: (i, cb))


def _whole_spec(shape):
    nd = len(shape)
    return pl.BlockSpec(tuple(shape), lambda i: (0,) * nd)


def _rows_call(name, f, rows, pars, outs, *, tm, nrows):
    tm = min(tm, nrows)
    nr, npar = len(rows), len(pars)

    def body(*refs):
        rv = [r[...].astype(F32) for r in refs[:nr]]
        pv = [p[...] for p in refs[nr:nr + npar]]
        res = f(*rv, *pv)
        for o_ref, r in zip(refs[nr + npar:], res):
            o_ref[...] = r.astype(o_ref.dtype)

    return pl.pallas_call(
        body, name=name, grid=(nrows // tm,),
        in_specs=[_row_spec(tm, w, cb) for _, w, cb in rows] + [_whole_spec(p.shape) for p in pars],
        out_specs=[_row_spec(tm, w, 0) for w, _ in outs],
        out_shape=[jax.ShapeDtypeStruct((nrows, w), dt) for w, dt in outs],
        compiler_params=pltpu.CompilerParams(dimension_semantics=("parallel",)),
    )(*[r[0] for r in rows], *pars)


def _rows_vjp_call(name, f, rows, pars, cots, drows, *, tm, nrows, cot_mm=None):
    tm = min(tm, nrows)
    nr, npar, nc = len(rows), len(pars), len(cots)
    mm_args, mm_specs = [], []
    if cot_mm is not None:
        mm_a, mm_b, mm_add = cot_mm
        mm_args = [mm_a, mm_b] + ([] if mm_add is None else [mm_add])
        mm_specs = [_row_spec(tm, mm_a.shape[1], 0), _whole_spec(mm_b.shape)]
        mm_specs += [] if mm_add is None else [_row_spec(tm, mm_b.shape[0], 0)]
    alias_bufs, aliases = [], {}
    out_shape, out_specs = [], []
    for (arr, w, cb), d in zip(rows, drows):
        if d is None:
            continue
        dt, into = d
        if into is None:
            out_shape.append(jax.ShapeDtypeStruct((nrows, w), dt))
            out_specs.append(_row_spec(tm, w, 0))
        else:
            buf, total, ocb = into
            if buf is not None:
                aliases[nr + npar + nc + len(alias_bufs)] = len(out_shape)
                alias_bufs.append(buf)
            out_shape.append(jax.ShapeDtypeStruct((nrows, total), dt))
            out_specs.append(_row_spec(tm, w, ocb))
    n_drow = len(out_shape)
    for p in pars:
        out_shape.append(jax.ShapeDtypeStruct(p.shape, F32))
        out_specs.append(_whole_spec(p.shape))
    na = len(alias_bufs)

    def body(*refs):
        rv = [r[...].astype(F32) for r in refs[:nr]]
        pv = [p[...] for p in refs[nr:nr + npar]]
        cv = tuple(c[...].astype(F32) for c in refs[nr + npar:nr + npar + nc])
        o_refs = refs[nr + npar + nc + na + len(mm_args):]
        if mm_args:
            mm_refs = refs[nr + npar + nc + na:nr + npar + nc + na + len(mm_args)]
            c0 = _dot(mm_refs[0][...], mm_refs[1][...], (((1,), (1,)), ((), ())))
            if len(mm_refs) == 3:
                c0 = c0 + mm_refs[2][...].astype(F32)
            cv = (c0,) + cv
        _, vjp = jax.vjp(f, *rv, *pv)
        g = vjp(cv)
        oi = 0
        for ri, d in enumerate(drows):
            if d is not None:
                o_refs[oi][...] = g[ri].astype(o_refs[oi].dtype)
                oi += 1
        first = pl.program_id(0) == 0
        for pi in range(npar):
            acc = o_refs[n_drow + pi]

            @pl.when(first)
            def _(acc=acc):
                acc[...] = jnp.zeros_like(acc)

            acc[...] += g[nr + pi]

    return pl.pallas_call(
        body, name=name, grid=(nrows // tm,),
        in_specs=[_row_spec(tm, w, cb) for _, w, cb in rows] + [_whole_spec(p.shape) for p in pars]
        + [_row_spec(tm, w, cb) for _, w, cb in cots] + [ANY] * na + mm_specs,
        out_specs=out_specs, out_shape=out_shape, input_output_aliases=aliases,
        compiler_params=pltpu.CompilerParams(dimension_semantics=("arbitrary",)),
    )(*[r[0] for r in rows], *pars, *[c[0] for c in cots], *alias_bufs, *mm_args)


def _rms(x, g):
    return x * lax.rsqrt(jnp.mean(x * x, axis=-1, keepdims=True) + EPS) * g


def _f_rms(x, g):
    return (_rms(x, g),)


def _tril_mask():
    return lax.broadcasted_iota(jnp.int32, (CHUNK, CHUNK), 0) >= lax.broadcasted_iota(jnp.int32, (CHUNK, CHUNK), 1)


def _f_branch_a(uvz, ln_g, ln_b, w_s, b_s):
    u = jax.nn.gelu(uvz[:, :E_A])
    v = jax.nn.gelu(uvz[:, E_A:2 * E_A])
    z = uvz[:, 2 * E_A:]
    xc = v - jnp.mean(v, axis=-1, keepdims=True)
    vn = xc * lax.rsqrt(jnp.mean(xc * xc, axis=-1, keepdims=True) + EPS) * ln_g + ln_b
    mask = _tril_mask()
    ws = [jnp.where(mask, w_s[g], 0.0) for g in range(G_A)]
    gw = E_A // G_A
    rows = []
    for c in range(uvz.shape[0] // CHUNK):
        vc = vn[c * CHUNK:(c + 1) * CHUNK]
        rows.append(jnp.concatenate([_dot(ws[g], vc[:, g * gw:(g + 1) * gw]) + b_s[g] for g in range(G_A)], axis=1))
    sv = rows[0] if len(rows) == 1 else jnp.concatenate(rows, axis=0)
    return (u * sv * jax.nn.silu(z),)


def _f_gnorm(y, zb, g):
    yz = y * jax.nn.silu(zb)
    gw = D_INNER // N_GROUPS
    parts = []
    for i in range(N_GROUPS):
        s = yz[:, i * gw:(i + 1) * gw]
        parts.append(s * lax.rsqrt(jnp.mean(s * s, axis=-1, keepdims=True) + EPS))
    return (jnp.concatenate(parts, axis=1) * g,)


def _f_merge(g2, oa, ob):
    return (jax.nn.sigmoid(g2[:, :D_MODEL]) * oa + jax.nn.sigmoid(g2[:, D_MODEL:]) * ob,)


def _f_loss(x1, gp, pe, tgt, fg):
    x2 = x1 + jax.nn.sigmoid(gp) * pe
    err = _rms(x2, fg) - tgt
    return 0.5 * jnp.sum(jnp.mean(err * err, axis=-1))


def _head(x1, p, tgt, ple_g, w_pg, w_ple, fg, *, tm, nrows):
    tm = min(tm, nrows)

    def body(x1_ref, p_ref, t_ref, pg_ref, wpg_ref, wple_ref, fg_ref, hp_ref, dx_ref, dgp_ref, dpe_ref, dfg_ref, loss_ref):
        x1 = x1_ref[...]
        hp_ref[...] = _rms(x1, pg_ref[...]).astype(hp_ref.dtype)
        gp = _dot(hp_ref[...], wpg_ref[...])
        pe = _dot(p_ref[...], wple_ref[...])
        loss, vjp = jax.vjp(_f_loss, x1, gp, pe, t_ref[...], fg_ref[...])
        dx, dgp, dpe, _, dfg = vjp(jnp.ones((), F32))
        dx_ref[...] = dx
        dgp_ref[...] = dgp.astype(dgp_ref.dtype)
        dpe_ref[...] = dpe.astype(dpe_ref.dtype)

        @pl.when(pl.program_id(0) == 0)
        def _():
            dfg_ref[...] = jnp.zeros_like(dfg_ref)
            loss_ref[...] = jnp.zeros_like(loss_ref)

        dfg_ref[...] += dfg
        loss_ref[...] += jnp.full(loss_ref.shape, loss, F32)

    row = _row_spec(tm, D_MODEL, 0)
    act = jax.ShapeDtypeStruct((nrows, D_MODEL), ACT_DTYPE)
    return pl.pallas_call(
        body, name="head", grid=(nrows // tm,),
        in_specs=[row, _row_spec(tm, PLE_DIM, 0), row, _whole_spec((1, D_MODEL)), _whole_spec(w_pg.shape),
                  _whole_spec(w_ple.shape), _whole_spec((1, D_MODEL))],
        out_specs=[row, row, row, row, _whole_spec((1, D_MODEL)), _whole_spec((1, 128))],
        out_shape=[act, jax.ShapeDtypeStruct((nrows, D_MODEL), F32), act, act, jax.ShapeDtypeStruct((1, D_MODEL), F32),
                   jax.ShapeDtypeStruct((1, 128), F32)],
        compiler_params=pltpu.CompilerParams(dimension_semantics=("arbitrary",)),
    )(x1, p, tgt, ple_g, w_pg, w_ple, fg)


def _shift_rows(cur, edge, j, up):
    tm = cur.shape[0]
    row = lax.broadcasted_iota(jnp.int32, cur.shape, 0)
    if up:
        sh = pltpu.roll(cur, tm - j, 0)
        e = jnp.tile(pltpu.roll(edge, 8 - j, 0), (tm // 8, 1))
        return jnp.where(row >= tm - j, e, sh)
    sh = pltpu.roll(cur, j, 0)
    e = jnp.tile(pltpu.roll(edge, j, 0), (tm // 8, 1))
    return jnp.where(row < j, e, sh)


def _conv_pre(cur, prev, w, b):
    acc = cur * w[CONV_K - 1:CONV_K] + b
    taps = [cur]
    for j in range(1, CONV_K):
        s = _shift_rows(cur, prev, j, up=False)
        taps.append(s)
        acc = acc + s * w[CONV_K - 1 - j:CONV_K - j]
    return acc, taps


def _halo_specs(tm, nrows, cb, before):
    nb = tm // 8
    last = nrows // 8 - 1
    if before:
        return pl.BlockSpec((8, XBC_W), lambda i: (jnp.maximum(i * nb - 1, 0), cb))
    return pl.BlockSpec((8, XBC_W), lambda i: (jnp.minimum((i + 1) * nb, last), cb))


def _conv_fwd(proj, conv_w, conv_b, *, tm, nrows):
    tm = min(tm, nrows)

    def body(cur_ref, prev_ref, w_ref, b_ref, o_ref):
        prev = jnp.where(pl.program_id(0) == 0, 0.0, prev_ref[...].astype(F32))
        pre, _ = _conv_pre(cur_ref[...].astype(F32), prev, w_ref[...], b_ref[...])
        o_ref[...] = jax.nn.silu(pre).astype(o_ref.dtype)

    return pl.pallas_call(
        body, name="conv_fwd", grid=(nrows // tm,),
        in_specs=[_row_spec(tm, XBC_W, XBC_CB), _halo_specs(tm, nrows, XBC_CB, True),
                  _whole_spec((CONV_K, XBC_W)), _whole_spec((1, XBC_W))],
        out_specs=_row_spec(tm, XBC_W, 0), out_shape=jax.ShapeDtypeStruct((nrows, XBC_W), ACT_DTYPE),
        compiler_params=pltpu.CompilerParams(dimension_semantics=("parallel",)),
    )(proj, proj, conv_w, conv_b)


def _conv_bwd_pre(proj, conv_w, conv_b, dact, *, tm, nrows):
    tm = min(tm, nrows)
    nb = N_GROUPS * N_STATE

    def body(cur_ref, prev_ref, w_ref, b_ref, dxs_ref, dbm_ref, dcm_ref, dpre_ref, dw_ref, db_ref):
        prev = jnp.where(pl.program_id(0) == 0, 0.0, prev_ref[...].astype(F32))
        pre, taps = _conv_pre(cur_ref[...].astype(F32), prev, w_ref[...], b_ref[...])
        sg = jax.nn.sigmoid(pre)
        dy = jnp.concatenate([dxs_ref[...], dbm_ref[...], dcm_ref[...]], axis=1).astype(F32)
        dpre = dy * sg * (1.0 + pre * (1.0 - sg))
        dpre_ref[...] = dpre.astype(dpre_ref.dtype)

        @pl.when(pl.program_id(0) == 0)
        def _():
            dw_ref[...] = jnp.zeros_like(dw_ref)
            db_ref[...] = jnp.zeros_like(db_ref)

        db_ref[...] += jnp.sum(dpre, axis=0, keepdims=True)
        for j in range(CONV_K):
            k = CONV_K - 1 - j
            dw_ref[k:k + 1, :] += jnp.sum(dpre * taps[j], axis=0, keepdims=True)

    return pl.pallas_call(
        body, name="conv_bwd_pre", grid=(nrows // tm,),
        in_specs=[_row_spec(tm, XBC_W, XBC_CB), _halo_specs(tm, nrows, XBC_CB, True),
                  _whole_spec((CONV_K, XBC_W)), _whole_spec((1, XBC_W)),
                  _row_spec(tm, D_INNER, 0), _row_spec(tm, nb, 0), _row_spec(tm, nb, 0)],
        out_specs=[_row_spec(tm, XBC_W, 0), _whole_spec((CONV_K, XBC_W)), _whole_spec((1, XBC_W))],
        out_shape=[jax.ShapeDtypeStruct((nrows, XBC_W), ACT_DTYPE), jax.ShapeDtypeStruct((CONV_K, XBC_W), F32),
                   jax.ShapeDtypeStruct((1, XBC_W), F32)],
        compiler_params=pltpu.CompilerParams(dimension_semantics=("arbitrary",)),
    )(proj, proj, conv_w, conv_b, *dact)


def _conv_bwd_x(dpre, conv_w, dproj, *, tm, nrows):
    tm = min(tm, nrows)
    ntiles = nrows // tm

    def body(cur_ref, nxt_ref, w_ref, _, o_ref):
        cur = cur_ref[...].astype(F32)
        nxt = jnp.where(pl.program_id(0) == ntiles - 1, 0.0, nxt_ref[...].astype(F32))
        w = w_ref[...]
        acc = cur * w[CONV_K - 1:CONV_K]
        for j in range(1, CONV_K):
            acc = acc + _shift_rows(cur, nxt, j, up=True) * w[CONV_K - 1 - j:CONV_K - j]
        o_ref[...] = acc.astype(o_ref.dtype)

    return pl.pallas_call(
        body, name="conv_bwd_x", grid=(ntiles,),
        in_specs=[_row_spec(tm, XBC_W, 0), _halo_specs(tm, nrows, 0, False), _whole_spec((CONV_K, XBC_W)), ANY],
        out_specs=_row_spec(tm, XBC_W, XBC_CB), out_shape=jax.ShapeDtypeStruct(dproj.shape, dproj.dtype),
        input_output_aliases={3: 0},
        compiler_params=pltpu.CompilerParams(dimension_semantics=("parallel",)),
    )(dpre, dpre, conv_w, dproj)


SSD_SPAN = 4
_XS_GW = D_INNER // N_GROUPS
_NT = (((1,), (1,)), ((), ()))
_TN = (((0,), (0,)), ((), ()))


def _bf16_terms(x, terms):
    parts, rest = [], x
    for _ in range(terms):
        part = rest.astype(jnp.bfloat16)
        parts.append(part)
        rest = rest - part.astype(F32)
    return parts


def _head_lane_matrix():
    return (lax.broadcasted_iota(jnp.int32, (128, _XS_GW), 0)
            == lax.broadcasted_iota(jnp.int32, (128, _XS_GW), 1) // HEAD_DIM).astype(jnp.bfloat16)


@functools.partial(jax.custom_vjp, nondiff_argnums=(1,))
def _head_lanes(cols, terms):
    e = _head_lane_matrix()
    return sum(jnp.dot(t, e, preferred_element_type=F32) for t in _bf16_terms(cols, terms))


def _head_lanes_fwd(cols, terms):
    return _head_lanes(cols, terms), None


def _head_lanes_bwd(terms, _, g):
    e = _head_lane_matrix()
    return (sum(lax.dot_general(t, e, _NT, preferred_element_type=F32) for t in _bf16_terms(g, 2)),)


_head_lanes.defvjp(_head_lanes_fwd, _head_lanes_bwd)


def _ssd_chunk(k, xs, bm, cm, dtr, hprev, dtb, alog, dsk):
    causal, tri, lo = k
    dt = jax.nn.softplus(dtr + dtb)
    da = dt * (-jnp.exp(alog))
    cs = jnp.dot(tri, da, precision=lax.Precision.HIGHEST, preferred_element_type=F32)
    cst = cs.T
    cs_l = _head_lanes(cs, 3)
    xdt = xs * _head_lanes(dt, 2)
    cb = _dot(cm, bm, _NT)
    yd = []
    for q in range(PAIRS_PER_GROUP):
        xq = xdt[:, 128 * q:128 * (q + 1)]
        y2 = [_dot(cb * jnp.exp(jnp.where(causal, cs[:, h:h + 1] - cst[h:h + 1, :], -jnp.inf)), xq)
              for h in (2 * q, 2 * q + 1)]
        yd.append(jnp.where(lo, y2[0], y2[1]))
    y_off = jnp.exp(cs_l) * _dot(cm, hprev, _NT)
    st = _dot(xdt * jnp.exp(cs_l[CHUNK - 1:CHUNK, :] - cs_l), bm, _TN)
    cdec = jnp.exp(cs[CHUNK - 1:CHUNK, :])
    cd_rows = jnp.concatenate(
        [jnp.broadcast_to(cdec[:, h:h + 1], (HEAD_DIM, N_STATE)) for h in range(HEADS_PER_GROUP)], axis=0)
    dsk_l = _head_lanes(jnp.broadcast_to(dsk, (8, 128)), 2)[:1]
    y = jnp.concatenate(yd, axis=1) + y_off + xs * dsk_l
    return y, cd_rows * hprev + st


def _ssd_span(xs, bm, cm, dtr, h0, dtb, alog, dsk):
    li = lax.broadcasted_iota(jnp.int32, (CHUNK, CHUNK), 0)
    si = lax.broadcasted_iota(jnp.int32, (CHUNK, CHUNK), 1)
    causal = li >= si
    k = (causal, causal.astype(F32), si < HEAD_DIM)
    h, ys = h0, []
    for t in range(xs.shape[0] // CHUNK):
        r = slice(t * CHUNK, (t + 1) * CHUNK)
        y, h = _ssd_chunk(k, xs[r], bm[r], cm[r], dtr[r], h, dtb, alog, dsk)
        ys.append(y)
    return (ys[0] if len(ys) == 1 else jnp.concatenate(ys, axis=0)), h


def _ssd_specs(rev, nsteps, rows):
    def s_of(s):
        return nsteps - 1 - s if rev else s

    xs = pl.BlockSpec((rows, _XS_GW), lambda g, s: (s_of(s), g))
    bm = pl.BlockSpec((rows, N_STATE), lambda g, s: (s_of(s), D_INNER // N_STATE + g))
    cm = pl.BlockSpec((rows, N_STATE), lambda g, s: (s_of(s), D_INNER // N_STATE + N_GROUPS + g))
    dt = pl.BlockSpec((rows, 128), lambda g, s: (s_of(s), g))
    par = pl.BlockSpec((1, 128), lambda g, s: (0, g))
    st = pl.BlockSpec((None, None, _XS_GW, N_STATE), lambda g, s: (g, s_of(s), 0, 0))
    return xs, bm, cm, dt, par, st


def _ssd_fwd(act, dtr, dtb, alog, dsk, *, nrows):
    rows = CHUNK * min(SSD_SPAN, nrows // CHUNK)
    nsteps = nrows // rows
    xs, bm, cm, dt, par, st = _ssd_specs(False, nsteps, rows)

    def body(xs_ref, b_ref, c_ref, dt_ref, dtb_ref, al_ref, dk_ref, y_ref, st_ref, h_ref):
        @pl.when(pl.program_id(1) == 0)
        def _():
            h_ref[...] = jnp.zeros_like(h_ref)

        h0 = h_ref[...]
        st_ref[...] = h0
        y, hnew = _ssd_span(xs_ref[...].astype(F32), b_ref[...].astype(F32), c_ref[...].astype(F32), dt_ref[...],
                            h0, dtb_ref[...], al_ref[...], dk_ref[...])
        y_ref[...] = y.astype(y_ref.dtype)
        h_ref[...] = hnew

    return pl.pallas_call(
        body, name="ssd_fwd", grid=(N_GROUPS, nsteps),
        in_specs=[xs, bm, cm, dt, par, par, par], out_specs=[xs, st],
        out_shape=[jax.ShapeDtypeStruct((nrows, D_INNER), ACT_DTYPE),
                   jax.ShapeDtypeStruct((N_GROUPS, nsteps, _XS_GW, N_STATE), F32)],
        scratch_shapes=[pltpu.VMEM((_XS_GW, N_STATE), F32)],
        compiler_params=pltpu.CompilerParams(dimension_semantics=("arbitrary", "arbitrary")),
    )(act, act, act, dtr, dtb, alog, dsk)


def _ssd_bwd(act, dtr, dtb, alog, dsk, states, dy, *, nrows):
    rows = CHUNK * min(SSD_SPAN, nrows // CHUNK)
    nsteps = nrows // rows
    xs, bm, cm, dt, par, st = _ssd_specs(True, nsteps, rows)

    def body(xs_ref, b_ref, c_ref, dt_ref, dtb_ref, al_ref, dk_ref, st_ref, dy_ref,
             dxs_ref, db_ref, dc_ref, ddt_ref, ddtb_ref, dal_ref, ddk_ref, dh_ref):
        @pl.when(pl.program_id(1) == 0)
        def _():
            dh_ref[...] = jnp.zeros_like(dh_ref)
            ddtb_ref[...] = jnp.zeros_like(ddtb_ref)
            dal_ref[...] = jnp.zeros_like(dal_ref)
            ddk_ref[...] = jnp.zeros_like(ddk_ref)

        _, vjp = jax.vjp(_ssd_span, xs_ref[...].astype(F32), b_ref[...].astype(F32), c_ref[...].astype(F32),
                         dt_ref[...], st_ref[...], dtb_ref[...], al_ref[...], dk_ref[...])
        dxs, db, dc, ddt, dh, ddtb, dal, ddk = vjp((dy_ref[...].astype(F32), dh_ref[...]))
        dxs_ref[...] = dxs.astype(dxs_ref.dtype)
        db_ref[...] = db.astype(db_ref.dtype)
        dc_ref[...] = dc.astype(dc_ref.dtype)
        ddt_ref[...] = ddt
        dh_ref[...] = dh
        ddtb_ref[...] += ddtb
        dal_ref[...] += dal
        ddk_ref[...] += ddk

    nb = N_GROUPS * N_STATE
    bspec = pl.BlockSpec((rows, N_STATE), lambda g, s: (nsteps - 1 - s, g))
    return pl.pallas_call(
        body, name="ssd_bwd", grid=(N_GROUPS, nsteps),
        in_specs=[xs, bm, cm, dt, par, par, par, st, xs],
        out_specs=[xs, bspec, bspec, dt, par, par, par],
        out_shape=[jax.ShapeDtypeStruct((nrows, D_INNER), ACT_DTYPE), jax.ShapeDtypeStruct((nrows, nb), ACT_DTYPE),
                   jax.ShapeDtypeStruct((nrows, nb), ACT_DTYPE), jax.ShapeDtypeStruct((nrows, DT_W), F32),
                   jax.ShapeDtypeStruct((1, DT_W), F32), jax.ShapeDtypeStruct((1, DT_W), F32),
                   jax.ShapeDtypeStruct((1, DT_W), F32)],
        scratch_shapes=[pltpu.VMEM((_XS_GW, N_STATE), F32)],
        compiler_params=pltpu.CompilerParams(dimension_semantics=("arbitrary", "arbitrary")),
    )(act, act, act, dtr, dtb, alog, dsk, states, dy)


def _add_epilogue(acc, r):
    return r + acc


def _rms_and_skip(x, g):
    return _rms(x, g), x


def _forward_backward(x, p, tgt, w):
    s = x.shape[0]
    act_t, f32 = ACT_DTYPE, F32
    mm = functools.partial(_matmul)
    h, proj = _rows_matmul("proj", _f_rms, [(x, D_MODEL, 0)], [w["norm_g"]], w["w_main"], out_dtype=act_t,
                           n=MAIN_W, k=D_MODEL, tm=1024, nrows=s)
    dtr = mm(h, w["w_dt"], mode="nn", name="proj_dt", out_dtype=f32, m=s, n=DT_W, k=D_MODEL)
    a_pars = [w["ln_a_g"], w["ln_a_b"], w["w_s"], w["b_s"]]
    y_a, o_a = _rows_matmul("out_a", _f_branch_a, [(proj, UVZ_W, UVZ_CB)], a_pars, w["w_oa"], out_dtype=act_t,
                            n=D_MODEL, k=E_A, tm=256, nrows=s)
    act = _conv_fwd(proj, w["conv_w"], w["conv_b"], tm=512, nrows=s)
    y, states = _ssd_fwd(act, dtr, w["dt_bias"], w["a_log"], w["d_skip"], nrows=s)
    gn_rows = [(y, D_INNER, 0), (proj, ZB_W, ZB_CB)]
    y_b, o_b = _rows_matmul("out_b", _f_gnorm, gn_rows, [w["ssm_norm_g"]], w["w_ob"], out_dtype=act_t,
                            n=D_MODEL, k=D_INNER, tm=512, nrows=s)
    mg_rows = [(proj, G_W, G_CB), (o_a, D_MODEL, 0), (o_b, D_MODEL, 0)]
    merged, x1 = _rows_matmul("out_proj", _f_merge, mg_rows, [], w["w_out"], out_dtype=f32, n=D_MODEL, k=D_MODEL,
                              tm=512, nrows=s, extras=(x,), epilogue=_add_epilogue)
    g = {}
    hp, dx2, dgp, dpe, g["final_g"], loss = _head(x1, p, tgt, w["ple_norm_g"], w["w_pg"], w["w_ple"], w["final_g"],
                                                   tm=256, nrows=s)
    g["w_pg"] = mm(hp, dgp, mode="tn", name="d_w_pg", out_dtype=f32, m=D_MODEL, n=D_MODEL, k=s)
    g["w_ple"] = mm(p, dpe, mode="tn", name="d_w_ple", out_dtype=f32, m=PLE_DIM, n=D_MODEL, k=s)
    dx1, g["ple_norm_g"] = _rows_vjp_call(
        "ple_norm_bwd", _rms_and_skip, [(x1, D_MODEL, 0)], [w["ple_norm_g"]], [(dx2, D_MODEL, 0)],
        [(f32, None)], tm=512, nrows=s, cot_mm=(dgp, w["w_pg"], None))
    g["w_out"] = mm(merged, dx1, mode="tn", name="d_w_out", out_dtype=f32, m=D_MODEL, n=D_MODEL, k=s)
    dproj, do_a, do_b = _rows_vjp_call(
        "merge_bwd", _f_merge, mg_rows, [], [],
        [(act_t, (None, MAIN_W, G_CB)), (act_t, None), (act_t, None)], tm=512, nrows=s, cot_mm=(dx1, w["w_out"], None))
    g["w_oa"] = mm(y_a, do_a, mode="tn", name="d_w_oa", out_dtype=f32, m=E_A, n=D_MODEL, k=s)
    g["w_ob"] = mm(y_b, do_b, mode="tn", name="d_w_ob", out_dtype=f32, m=D_INNER, n=D_MODEL, k=s)
    dy, dproj, g["ssm_norm_g"] = _rows_vjp_call(
        "gnorm_bwd", _f_gnorm, gn_rows, [w["ssm_norm_g"]], [],
        [(act_t, None), (act_t, (dproj, MAIN_W, ZB_CB))], tm=256, nrows=s, cot_mm=(do_b, w["w_ob"], None))
    dxs, dbm, dcm, ddtr, g["dt_bias"], g["a_log"], g["d_skip"] = _ssd_bwd(
        act, dtr, w["dt_bias"], w["a_log"], w["d_skip"], states, dy, nrows=s)
    dpre, g["conv_w"], g["conv_b"] = _conv_bwd_pre(proj, w["conv_w"], w["conv_b"], (dxs, dbm, dcm), tm=512, nrows=s)
    dproj = _conv_bwd_x(dpre, w["conv_w"], dproj, tm=512, nrows=s)
    dproj, g["ln_a_g"], g["ln_a_b"], g["w_s"], g["b_s"] = _rows_vjp_call(
        "branch_a_bwd", _f_branch_a, [(proj, UVZ_W, UVZ_CB)], a_pars, [],
        [(act_t, (dproj, MAIN_W, UVZ_CB))], tm=256, nrows=s, cot_mm=(do_a, w["w_oa"], None))
    g["w_main"] = mm(h, dproj, mode="tn", name="d_w_main", out_dtype=f32, m=D_MODEL, n=MAIN_W, k=s)
    g["w_dt"] = mm(h, ddtr, mode="tn", name="d_w_dt", out_dtype=f32, m=D_MODEL, n=DT_W, k=s)
    return loss, g, (dproj, ddtr, dx1)


def _input_grad(x, w, ctx, after=()):
    dproj, ddtr, dx1 = ctx
    s = x.shape[0]
    dh = _matmul(dproj, w["w_main"], mode="nt", name="d_h_main", out_dtype=F32, m=s, n=D_MODEL, k=MAIN_W, after=after)
    return _rows_vjp_call(
        "pre_norm_bwd", _rms_and_skip, [(x, D_MODEL, 0)], [w["norm_g"]], [(dx1, D_MODEL, 0)],
        [(F32, None)], tm=512, nrows=s, cot_mm=(ddtr, w["w_dt"], dh))


def _local_step(x, p, tgt, w):
    loss, g, ctx = _forward_backward(x, p, tgt, w)
    grad_x, g["norm_g"] = _input_grad(x, w, ctx)
    return loss, grad_x, g


_O_ZB = 3 * E_A
_O_XBC = _O_ZB + D_INNER
_O_DT = _O_XBC + CONV_DIM
_O_G = _O_DT + N_HEADS


def _heads_to_lanes(v):
    r = v.shape[0]
    v = v.reshape(r, N_GROUPS, HEADS_PER_GROUP)
    return jnp.pad(v, ((0, 0), (0, 0), (0, 128 - HEADS_PER_GROUP))).reshape(r, DT_W)


def _lanes_to_heads(v):
    r = v.shape[0]
    return v.reshape(r, N_GROUPS, 128)[:, :, :HEADS_PER_GROUP].reshape(r, N_HEADS)


def _block_cols(blocks, a, b):
    parts = []
    for k in range(N_CHIPS):
        lo, hi = max(a, k * W_IN_BLOCK), min(b, (k + 1) * W_IN_BLOCK)
        if lo < hi:
            parts.append(blocks[k][:, lo - k * W_IN_BLOCK:hi - k * W_IN_BLOCK])
    return parts


_W_IN_SEGMENTS = ((0, _O_ZB, "m", 0), (_O_ZB, _O_XBC, "m", UVZ_W + XBC_W), (_O_XBC, _O_DT, "m", UVZ_W),
                  (_O_DT, _O_G, "d", 0), (_O_G, N_IN, "m", MAIN_W - G_W))


def _w_in_grad_blocks(gm, gdt):
    blocks = []
    for k in range(N_CHIPS):
        a, b = k * W_IN_BLOCK, (k + 1) * W_IN_BLOCK
        parts = []
        for s, e, src, off in _W_IN_SEGMENTS:
            lo, hi = max(a, s), min(b, e)
            if lo < hi:
                parts.append((gm if src == "m" else gdt)[:, off + lo - s:off + hi - s])
        blocks.append(jnp.concatenate(parts, axis=1))
    return jnp.stack(blocks)


def _layout_weights(f, w_in_blocks=None):
    w = dict(f)
    if w_in_blocks is None:
        w_in = w.pop("w_in")
        w_in_blocks = jnp.stack([w_in[:, k * W_IN_BLOCK:(k + 1) * W_IN_BLOCK] for k in range(N_CHIPS)])
    cols = functools.partial(_block_cols, w_in_blocks)
    w["w_main"] = jnp.concatenate(cols(0, _O_ZB) + cols(_O_XBC, _O_DT) + cols(_O_ZB, _O_XBC) + cols(_O_G, N_IN), axis=1)
    w["w_dt"] = _heads_to_lanes(jnp.concatenate(cols(_O_DT, _O_G), axis=1))
    w["b_s"] = f["b_s"].reshape(G_A, CHUNK, 1)
    for n in ("dt_bias", "a_log", "d_skip"):
        w[n] = _heads_to_lanes(f[n])
    return w


def _natural_grads(g):
    out = dict(g)
    gm = out.pop("w_main")
    gdt = _lanes_to_heads(out.pop("w_dt"))
    out["w_in"] = jnp.concatenate(
        [gm[:, :UVZ_W], gm[:, UVZ_W + XBC_W:UVZ_W + XBC_W + ZB_W], gm[:, UVZ_W:UVZ_W + XBC_W], gdt, gm[:, MAIN_W - G_W:]],
        axis=1)
    out["b_s"] = g["b_s"].reshape(G_A, CHUNK)
    for n in ("dt_bias", "a_log", "d_skip"):
        out[n] = _lanes_to_heads(g[n])
    return out


def _place():
    return lax.axis_index("x"), lax.axis_index("y"), lax.axis_index("c")


def _other_chips(x, y):
    return [(1 - x, y), (x, 1 - y), (1 - x, 1 - y)]


def _rcopy(src, dst, ssem, rsem, dev):
    return pltpu.make_async_remote_copy(src_ref=src, dst_ref=dst, send_sem=ssem, recv_sem=rsem,
                                        device_id=dev, device_id_type=MESH)


def _half(ref_rows, half):
    hs = ref_rows // 2
    return pl.ds(pl.multiple_of(half * hs, 16), hs)


def _gather_weights(shards, conv_shard):
    nw = len(shards)

    def body(*refs):
        sh, cv = refs[:nw], refs[nw]
        out, cvo = refs[nw + 1:2 * nw + 1], refs[2 * nw + 1]
        ici_s, ici_r, fw_s, fw_r, own_s, own_r, cv_s, cv_r = refs[2 * nw + 2:]
        x, y, c = _place()
        me, sib, chips = 2 * x + y, (x, y, 1 - c), _other_chips(x, y)
        own = [_rcopy(sh[w], out[w].at[me], own_s.at[w], own_r.at[w], sib) for w in range(nw)]
        own.append(_rcopy(cv, cvo.at[me], own_s.at[nw], own_r.at[nw], sib))
        for cp in own:
            cp.start()
        sends = []
        for w in range(nw):
            mine = _half(sh[w].shape[0], c)
            for j, chip in enumerate(chips):
                sends.append(_rcopy(sh[w].at[mine], out[w].at[me, mine], ici_s.at[3 * w + j], ici_r.at[3 * w + j], (*chip, c)))
        for j, chip in enumerate(chips):
            sends.append(_rcopy(cv, cvo.at[me], cv_s.at[j], cv_r.at[j], (*chip, c)))
        for cp in sends:
            cp.start()
        for w in range(nw):
            mine = _half(sh[w].shape[0], c)
            for j, chip in enumerate(chips):
                slab = out[w].at[2 * chip[0] + chip[1], mine]
                _rcopy(slab, slab, ici_s.at[3 * w + j], ici_r.at[3 * w + j], (*chip, c)).wait_recv()
                fwd = _rcopy(slab, slab, fw_s.at[3 * w + j], fw_r.at[3 * w + j], sib)
                fwd.start()
                sends.append(fwd)
        for j, chip in enumerate(chips):
            blk = cvo.at[2 * chip[0] + chip[1]]
            _rcopy(blk, blk, cv_s.at[j], cv_r.at[j], (*chip, c)).wait_recv()
        for w in range(nw):
            theirs = _half(sh[w].shape[0], 1 - c)
            for j, chip in enumerate(chips):
                slab = out[w].at[2 * chip[0] + chip[1], theirs]
                _rcopy(slab, slab, fw_s.at[3 * w + j], fw_r.at[3 * w + j], sib).wait_recv()
        for cp in sends:
            cp.wait_send()
        for cp in own:
            cp.wait()

    dma = pltpu.SemaphoreType.DMA
    return pl.pallas_call(
        body, name="gather_weights",
        in_specs=[ANY] * (nw + 1), out_specs=[ANY] * (nw + 1),
        out_shape=[jax.ShapeDtypeStruct((N_CHIPS,) + s.shape, s.dtype) for s in shards]
        + [jax.ShapeDtypeStruct((N_CHIPS,) + conv_shard.shape, conv_shard.dtype)],
        scratch_shapes=[dma((3 * nw,)), dma((3 * nw,)), dma((3 * nw,)), dma((3 * nw,)), dma((nw + 1,)), dma((nw + 1,)),
                        dma((3,)), dma((3,))],
    )(*shards, conv_shard)


def _swap_with_sibling(arrs):
    n = len(arrs)

    def body(*refs):
        src, dst, s_sem, r_sem = refs[:n], refs[n:2 * n], refs[2 * n], refs[2 * n + 1]
        x, y, c = _place()
        cps = [_rcopy(src[i], dst[i], s_sem.at[i], r_sem.at[i], (x, y, 1 - c)) for i in range(n)]
        for cp in cps:
            cp.start()
        for cp in cps:
            cp.wait()

    dma = pltpu.SemaphoreType.DMA
    return pl.pallas_call(
        body, name="swap_with_sibling", in_specs=[ANY] * n, out_specs=[ANY] * n,
        out_shape=[jax.ShapeDtypeStruct(a.shape, a.dtype) for a in arrs], scratch_shapes=[dma((n,)), dma((n,))],
    )(*arrs)


_HBM = pl.BlockSpec(memory_space=pltpu.HBM)
_SEM = pl.BlockSpec(memory_space=pltpu.SEMAPHORE)
_EFFECT = pltpu.SideEffectType.DATAFLOW_SIDE_EFFECTING


def _scatter_copies(src, land, s_sem, r_sem):
    x, y, c = _place()
    return [_rcopy(src[i].at[2 * chip[0] + chip[1]], land[i].at[j], s_sem.at[3 * i + j], r_sem.at[3 * i + j], (*chip, c))
            for i in range(len(src)) for j, chip in enumerate(_other_chips(x, y))]


def _scatter_blocks_start(arrs):
    n = len(arrs)
    lands = [lax.empty((3,) + a.shape[1:], a.dtype) for a in arrs]

    def body(*refs):
        src, land, s_sem, r_sem, token = refs[:n], refs[n:2 * n], refs[2 * n], refs[2 * n + 1], refs[-1]
        for cp in _scatter_copies(src, land, s_sem, r_sem):
            cp.start()
        token[...] = jnp.zeros_like(token)

    dma = pltpu.SemaphoreType.DMA
    hbm = [pltpu.with_memory_space_constraint(a, pltpu.HBM) for a in list(arrs) + lands]
    out = pl.pallas_call(
        body, name="scatter_blocks_start",
        out_shape=[dma((3 * n,)), dma((3 * n,))] + [pltpu.HBM(a.shape, a.dtype) for a in hbm]
        + [jax.ShapeDtypeStruct((8, 128), F32)],
        in_specs=[_HBM] * (2 * n), out_specs=[_SEM, _SEM] + [_HBM] * (2 * n) + [pl.BlockSpec(memory_space=pltpu.VMEM)],
        input_output_aliases={i: 2 + i for i in range(2 * n)},
        compiler_params=pltpu.CompilerParams(has_side_effects=_EFFECT),
    )(*hbm)
    return out[0], out[1], out[2:2 + n], out[2 + n:2 + 2 * n], out[-1]


def _scatter_blocks_wait(s_sem, r_sem, srcs, lands, after):
    n = len(srcs)

    def body(*refs):
        src, land, s_sem, r_sem = refs[:n], refs[n:2 * n], refs[2 * n], refs[2 * n + 1]
        for cp in _scatter_copies(src, land, s_sem, r_sem):
            cp.wait_send()
            cp.wait_recv()

    out = pl.pallas_call(
        body, name="scatter_blocks_wait",
        out_shape=[pltpu.HBM(a.shape, a.dtype) for a in list(srcs) + list(lands)],
        in_specs=[_HBM] * (2 * n) + [_SEM, _SEM, ANY], out_specs=[_HBM] * (2 * n),
        input_output_aliases={i: i for i in range(2 * n)},
        compiler_params=pltpu.CompilerParams(has_side_effects=_EFFECT),
    )(*srcs, *lands, s_sem, r_sem, after)
    return out[n:]


def _share_halves(arrs):
    n = len(arrs)

    def body(*refs):
        buf, s_sem, r_sem = refs[n:2 * n], refs[2 * n], refs[2 * n + 1]
        x, y, c = _place()
        cps = []
        for i in range(n):
            mine = buf[i].at[_half(buf[i].shape[0], c)]
            cps.append(_rcopy(mine, mine, s_sem.at[i], r_sem.at[i], (x, y, 1 - c)))
        for cp in cps:
            cp.start()
        for i in range(n):
            theirs = buf[i].at[_half(buf[i].shape[0], 1 - c)]
            _rcopy(theirs, theirs, s_sem.at[i], r_sem.at[i], (x, y, 1 - c)).wait_recv()
        for cp in cps:
            cp.wait_send()

    dma = pltpu.SemaphoreType.DMA
    return pl.pallas_call(
        body, name="share_halves", in_specs=[ANY] * n, out_specs=[ANY] * n,
        out_shape=[jax.ShapeDtypeStruct(a.shape, a.dtype) for a in arrs],
        input_output_aliases={i: i for i in range(n)}, scratch_shapes=[dma((n,)), dma((n,))],
    )(*arrs)


def _allreduce_small(packed):
    rows = packed.shape[0]

    def body(p_ref, o_ref, buf, s_sem, r_sem):
        x, y, c = _place()
        me = 4 * x + 2 * y + c
        buf[me] = p_ref[...]
        cps = []
        for d in range(1, N_DEV):
            px, py, pc = ((1 - x) if d & 4 else x), ((1 - y) if d & 2 else y), ((1 - c) if d & 1 else c)
            cps.append(_rcopy(p_ref, buf.at[me], s_sem.at[d - 1], r_sem.at[d - 1], (px, py, pc)))
        for cp in cps:
            cp.start()
        for d in range(1, N_DEV):
            px, py, pc = ((1 - x) if d & 4 else x), ((1 - y) if d & 2 else y), ((1 - c) if d & 1 else c)
            landed = buf.at[4 * px + 2 * py + pc]
            _rcopy(p_ref, landed, s_sem.at[d - 1], r_sem.at[d - 1], (px, py, pc)).wait_recv()
        for cp in cps:
            cp.wait_send()
        acc = buf[0]
        for d in range(1, N_DEV):
            acc = acc + buf[d]
        o_ref[...] = acc

    dma = pltpu.SemaphoreType.DMA
    return pl.pallas_call(
        body, name="allreduce_small", out_shape=jax.ShapeDtypeStruct(packed.shape, F32),
        scratch_shapes=[pltpu.VMEM((N_DEV, rows, 128), F32), dma((N_DEV - 1,)), dma((N_DEV - 1,))],
    )(packed)


def _row_tile(rows, cols):
    tr = max(8, min(rows, (1 << 20) // (4 * cols) // 8 * 8))
    while rows % tr:
        tr -= 8
    return tr


def _chip_sum(name, g5, recv, c_arr):
    _, _, hs, cols = g5.shape
    tr = _row_tile(hs, cols)

    def body(_, a_ref, b_ref, o32_ref, ow_ref):
        s = a_ref[...] + b_ref[...].astype(F32)
        o32_ref[...] = s
        ow_ref[...] = s.astype(ow_ref.dtype)

    blk = pl.BlockSpec((None, tr, cols), lambda b, i, c: (b, i, 0))
    return pl.pallas_call(
        body, name=name,
        grid_spec=pltpu.PrefetchScalarGridSpec(
            num_scalar_prefetch=1, grid=(N_CHIPS, hs // tr),
            in_specs=[pl.BlockSpec((None, None, tr, cols), lambda b, i, c: (b, c[0], i, 0)), blk], out_specs=[blk, blk]),
        out_shape=[jax.ShapeDtypeStruct((N_CHIPS, hs, cols), F32), jax.ShapeDtypeStruct((N_CHIPS, hs, cols), WIRE_DTYPE)],
    )(c_arr, g5, recv)


def _final_sum(name, own, recv, place_arr):
    _, hs, cols = own.shape
    tr = _row_tile(hs, cols)
    nt = hs // tr

    def body(_, a_ref, r_ref, o_ref):
        o_ref[...] = ((a_ref[...] + r_ref[0].astype(F32)) + r_ref[1].astype(F32)) + r_ref[2].astype(F32)

    return pl.pallas_call(
        body, name=name,
        grid_spec=pltpu.PrefetchScalarGridSpec(
            num_scalar_prefetch=1, grid=(nt,),
            in_specs=[pl.BlockSpec((None, tr, cols), lambda i, m: (m[0], i, 0)),
                      pl.BlockSpec((3, tr, cols), lambda i, m: (0, i, 0))],
            out_specs=pl.BlockSpec((tr, cols), lambda i, m: (m[1] * nt + i, 0))),
        out_shape=jax.ShapeDtypeStruct((2 * hs, cols), F32),
    )(place_arr, own, recv)


def _adamw(w, g, m, v):
    m = ADAM_B1 * m + (1.0 - ADAM_B1) * g
    v = ADAM_B2 * v + (1.0 - ADAM_B2) * (g * g)
    m_hat = m / (1.0 - ADAM_B1 ** ADAM_STEP)
    v_hat = v / (1.0 - ADAM_B2 ** ADAM_STEP)
    return -ADAM_LR * (m_hat / (jnp.sqrt(v_hat) + ADAM_EPS) + ADAM_WD * w), m, v


def _adamw_call(name, w, g, m, v):
    rows, cols = w.shape
    tr = _row_tile(rows, cols)

    def body(w_ref, g_ref, m_ref, v_ref, d_ref, nm_ref, nv_ref):
        d_ref[...], nm_ref[...], nv_ref[...] = _adamw(w_ref[...], g_ref[...], m_ref[...], v_ref[...])

    blk = pl.BlockSpec((tr, cols), lambda i: (i, 0))
    return pl.pallas_call(
        body, name=name, grid=(rows // tr,), in_specs=[blk] * 4, out_specs=[blk] * 3,
        out_shape=[jax.ShapeDtypeStruct(w.shape, F32)] * 3,
        compiler_params=pltpu.CompilerParams(dimension_semantics=("parallel",)),
    )(w, g, m, v)


def _adamw_small(ws, gs, ms, vs):
    n = len(ws)

    def body(*refs):
        for i in range(n):
            w_ref, g_ref, m_ref, v_ref = (refs[k * n + i] for k in range(4))
            d, nm, nv = _adamw(w_ref[...], g_ref[...], m_ref[...], v_ref[...])
            refs[4 * n + i][...] = d
            refs[5 * n + i][...] = nm
            refs[6 * n + i][...] = nv

    out = pl.pallas_call(
        body, name="adamw_small", out_shape=[jax.ShapeDtypeStruct(a.shape, F32) for a in ws] * 3,
    )(*ws, *gs, *ms, *vs)
    return out[:n], out[n:2 * n], out[2 * n:]


_BIG = ("w_in", "w_oa", "w_ob", "w_out", "w_pg", "w_ple")
_SMALL = ("norm_g", "ln_a_g", "ln_a_b", "w_s", "b_s", "conv_w", "conv_b", "dt_bias", "a_log", "d_skip", "ssm_norm_g",
          "ple_norm_g", "final_g")
_WEIGHTS = ("norm_g", "w_in", "ln_a_g", "ln_a_b", "w_s", "b_s", "conv_w", "conv_b", "dt_bias", "a_log", "d_skip",
            "ssm_norm_g", "w_oa", "w_ob", "w_out", "ple_norm_g", "w_pg", "w_ple", "final_g")
_COL_SHARDED = ("w_in", "w_ple")
_PACK = 1024


def _blocks_to_full(col_sharded, blocks):
    if col_sharded:
        return jnp.concatenate([blocks[k] for k in range(N_CHIPS)], axis=1)
    return blocks.reshape(N_CHIPS * blocks.shape[1], blocks.shape[2])


def _full_to_blocks(col_sharded, full):
    if col_sharded:
        w = full.shape[1] // N_CHIPS
        return jnp.stack([full[:, k * w:(k + 1) * w] for k in range(N_CHIPS)])
    return full.reshape(N_CHIPS, full.shape[0] // N_CHIPS, full.shape[1])


def _two_d(n, a):
    if n == "w_s":
        return a.reshape(G_A * CHUNK, CHUNK)
    if n in ("b_s", "conv_w"):
        return a.reshape(a.shape[-2], a.shape[-1])
    return a.reshape(1, a.shape[-1])


def kernel(x, p, norm_g, w_in, ln_a_g, ln_a_b, w_s, b_s, conv_w, conv_b, dt_bias, a_log, d_skip, ssm_norm_g, w_oa, w_ob, w_out, ple_norm_g, w_pg, w_ple, final_g, loss_target, m_norm_g, m_w_in, m_ln_a_g, m_ln_a_b, m_w_s, m_b_s, m_conv_w, m_conv_b, m_dt_bias, m_a_log, m_d_skip, m_ssm_norm_g, m_w_oa, m_w_ob, m_w_out, m_ple_norm_g, m_w_pg, m_w_ple, m_final_g, v_norm_g, v_w_in, v_ln_a_g, v_ln_a_b, v_w_s, v_b_s, v_conv_w, v_conv_b, v_dt_bias, v_a_log, v_d_skip, v_ssm_norm_g, v_w_oa, v_w_ob, v_w_out, v_ple_norm_g, v_w_pg, v_w_ple, v_final_g):
    wt = dict(norm_g=norm_g, w_in=w_in, ln_a_g=ln_a_g, ln_a_b=ln_a_b, w_s=w_s, b_s=b_s, conv_w=conv_w, conv_b=conv_b,
              dt_bias=dt_bias, a_log=a_log, d_skip=d_skip, ssm_norm_g=ssm_norm_g, w_oa=w_oa, w_ob=w_ob, w_out=w_out,
              ple_norm_g=ple_norm_g, w_pg=w_pg, w_ple=w_ple, final_g=final_g)
    mom = dict(norm_g=m_norm_g, w_in=m_w_in, ln_a_g=m_ln_a_g, ln_a_b=m_ln_a_b, w_s=m_w_s, b_s=m_b_s, conv_w=m_conv_w,
               conv_b=m_conv_b, dt_bias=m_dt_bias, a_log=m_a_log, d_skip=m_d_skip, ssm_norm_g=m_ssm_norm_g, w_oa=m_w_oa,
               w_ob=m_w_ob, w_out=m_w_out, ple_norm_g=m_ple_norm_g, w_pg=m_w_pg, w_ple=m_w_ple, final_g=m_final_g)
    vel = dict(norm_g=v_norm_g, w_in=v_w_in, ln_a_g=v_ln_a_g, ln_a_b=v_ln_a_b, w_s=v_w_s, b_s=v_b_s, conv_w=v_conv_w,
               conv_b=v_conv_b, dt_bias=v_dt_bias, a_log=v_a_log, d_skip=v_d_skip, ssm_norm_g=v_ssm_norm_g, w_oa=v_w_oa,
               w_ob=v_w_ob, w_out=v_w_out, ple_norm_g=v_ple_norm_g, w_pg=v_w_pg, w_ple=v_w_ple, final_g=v_final_g)
    xi, yi, ci = _place()
    me = 2 * xi + yi
    c_arr = jnp.reshape(ci, (1,)).astype(jnp.int32)
    place_arr = jnp.stack([me, ci]).astype(jnp.int32)

    shard = {n: wt[n][0] for n in _BIG}
    blocks = _gather_weights([shard[n].astype(WIRE_DTYPE) for n in _BIG], conv_w[0])
    full = {n: _blocks_to_full(n in _COL_SHARDED, b) for n, b in zip(_BIG[1:], blocks[1:-1])}
    full["conv_w"] = _blocks_to_full(True, blocks[-1])
    for n in _SMALL:
        if n != "conv_w":
            full[n] = wt[n][0] if wt[n].ndim > 2 else wt[n].reshape(1, wt[n].shape[-1])

    w = _layout_weights(full, w_in_blocks=blocks[0])
    loss_row, g, ctx = _forward_backward(x[0], p[0, 0], loss_target[0], w)
    loss = lax.psum(loss_row[0, 0], ("x", "y", "c"))

    big = {n: _full_to_blocks(n in _COL_SHARDED, g[n]) for n in _BIG[1:]}
    big["w_in"] = _w_in_grad_blocks(g["w_main"], _lanes_to_heads(g["w_dt"]))
    g5 = {n: big[n].reshape(N_CHIPS, 2, big[n].shape[1] // 2, big[n].shape[2]) for n in _BIG}
    to_sibling = [lax.dynamic_index_in_dim(g5[n], 1 - ci, axis=1, keepdims=False).astype(WIRE_DTYPE) for n in _BIG]
    from_sibling = _swap_with_sibling(to_sibling)
    chip32, chip_wire = zip(*[_chip_sum("chip_sum_" + n, g5[n], r, c_arr) for n, r in zip(_BIG, from_sibling)])
    s_sem, r_sem, sent, lands, token = _scatter_blocks_start(list(chip_wire))
    grad_x, g["norm_g"] = _input_grad(x[0], w, ctx, after=(token,))
    from_chips = _scatter_blocks_wait(s_sem, r_sem, sent, lands, grad_x)
    halves = [_final_sum("final_sum_" + n, a, r, place_arr) for n, a, r in zip(_BIG, chip32, from_chips)]
    grads = dict(zip(_BIG, _share_halves(halves)))
    g = _natural_grads(g)

    pieces = [_two_d(n, g[n]).reshape(-1) for n in _SMALL]
    sizes = [v.shape[0] for v in pieces]
    padded = [-(-s // _PACK) * _PACK for s in sizes]
    packed = jnp.concatenate([jnp.pad(v, (0, ps - s)) for v, s, ps in zip(pieces, sizes, padded)]).reshape(-1, 128)
    summed = _allreduce_small(packed).reshape(-1)
    off = 0
    for n, s, ps in zip(_SMALL, sizes, padded):
        grads[n] = _two_d(n, g[n]).shape, summed[off:off + s]
        off += ps
    for n in _SMALL:
        shape, flat = grads[n]
        grads[n] = flat.reshape(shape)
    grads["conv_w"] = lax.dynamic_slice_in_dim(grads["conv_w"], me * (CONV_DIM // N_CHIPS), CONV_DIM // N_CHIPS, axis=1)

    delta, new_m, new_v = {}, {}, {}
    for n in _BIG:
        delta[n], new_m[n], new_v[n] = _adamw_call("adamw_" + n, shard[n], grads[n], mom[n][0], vel[n][0])
    small = _adamw_small([_two_d(n, wt[n]) for n in _SMALL], [grads[n] for n in _SMALL],
                         [_two_d(n, mom[n]) for n in _SMALL], [_two_d(n, vel[n]) for n in _SMALL])
    for i, n in enumerate(_SMALL):
        delta[n], new_m[n], new_v[n] = small[0][i], small[1][i], small[2][i]

    def shaped(d):
        return [d[n].reshape(wt[n].shape) for n in _WEIGHTS]

    return (loss, grad_x[None], *shaped(grads), *shaped(delta), *shaped(new_m), *shaped(new_v))
```

```python
import functools

import jax
import jax.numpy as jnp
from jax import lax
from jax.experimental import pallas as pl
from jax.experimental.pallas import tpu as pltpu

F32 = jnp.float32
MXU_DTYPE = jnp.bfloat16
ACT_DTYPE = jnp.bfloat16
WIRE_DTYPE = jnp.bfloat16

D_MODEL = 1024
PLE_DIM = 256
CHUNK = 128
EPS = 1e-6
E_A = D_MODEL
G_A = 4
D_INNER = 2 * D_MODEL
HEAD_DIM = 64
N_HEADS = D_INNER // HEAD_DIM
N_STATE = 128
N_GROUPS = 4
HEADS_PER_GROUP = N_HEADS // N_GROUPS
PAIRS_PER_GROUP = HEADS_PER_GROUP // 2
CONV_K = 4
CONV_DIM = D_INNER + 2 * N_GROUPS * N_STATE
N_IN = 3 * E_A + D_INNER + CONV_DIM + N_HEADS + 2 * D_MODEL
N_CHIPS = 4
N_DEV = 8
W_IN_BLOCK = N_IN // N_CHIPS

UVZ_W, XBC_W, ZB_W, G_W = 3 * E_A, CONV_DIM, D_INNER, 2 * D_MODEL
MAIN_W = UVZ_W + XBC_W + ZB_W + G_W
UVZ_CB, XBC_CB, ZB_CB, G_CB = 0, 1, 3, 4
DT_W = N_GROUPS * 128

ADAM_LR, ADAM_B1, ADAM_B2, ADAM_EPS, ADAM_WD, ADAM_STEP = 0.001, 0.9, 0.999, 1e-08, 0.01, 10

MESH = pl.DeviceIdType.MESH
ANY = pl.BlockSpec(memory_space=pl.ANY)


def _mxu(v):
    return v.astype(MXU_DTYPE)


def _dot(a, b, dims=(((1,), (0,)), ((), ()))):
    return lax.dot_general(_mxu(a), _mxu(b), dims, preferred_element_type=F32)


MM_TILE = 1024
MM_VMEM_BUDGET = 46 << 20


def _mm_tk(m, n, k, tm, tn, a_bytes, b_bytes, out_bytes, extra_bytes):
    for parts in range(1, k // 128 + 1):
        if k % parts or (k // parts) % 128 and parts > 1:
            continue
        tk = k // parts
        need = 2 * tk * (tm * a_bytes + tn * b_bytes) + 2 * tm * tn * (out_bytes + extra_bytes) + (tm * tn * 4 if parts > 1 else 0)
        if need <= MM_VMEM_BUDGET:
            return tk
    return 128


def _matmul(a, b, *, mode, name, out_dtype, m, n, k, tm=MM_TILE, tn=MM_TILE, tk=None, a_off=0, b_off=0,
            extras=(), epilogue=None, after=()):
    tm, tn = min(tm, m), min(tn, n)
    if tk is None:
        tk = _mm_tk(m, n, k, tm, tn, a.dtype.itemsize, b.dtype.itemsize, jnp.dtype(out_dtype).itemsize,
                    sum(e.dtype.itemsize for e in extras))
    tk = min(tk, k)
    assert m % tm == 0 and n % tn == 0 and k % tk == 0, (name, m, n, k, tm, tn, tk)
    nk = k // tk
    if mode == "nn":
        assert a_off % tk == 0 and b_off % tn == 0
        a_spec = pl.BlockSpec((tm, tk), lambda i, j, kk: (i, kk + a_off // tk))
        b_spec = pl.BlockSpec((tk, tn), lambda i, j, kk: (kk, j + b_off // tn))
        dims = (((1,), (0,)), ((), ()))
    elif mode == "nt":
        a_spec = pl.BlockSpec((tm, tk), lambda i, j, kk: (i, kk))
        b_spec = pl.BlockSpec((tn, tk), lambda i, j, kk: (j, kk))
        dims = (((1,), (1,)), ((), ()))
    else:
        assert a_off % tm == 0 and b_off % tn == 0
        a_spec = pl.BlockSpec((tk, tm), lambda i, j, kk: (kk, i + a_off // tm))
        b_spec = pl.BlockSpec((tk, tn), lambda i, j, kk: (kk, j + b_off // tn))
        dims = (((0,), (0,)), ((), ()))
    ne = len(extras)

    def finish(acc, extra_refs, o_ref):
        res = acc if epilogue is None else epilogue(acc, *[e[...] for e in extra_refs])
        o_ref[...] = res.astype(o_ref.dtype)

    def body(a_ref, b_ref, *rest):
        extra_refs, o_ref = rest[:ne], rest[ne + len(after)]
        part = _dot(a_ref[...], b_ref[...], dims)
        if nk == 1:
            finish(part, extra_refs, o_ref)
            return
        acc_ref = rest[ne + len(after) + 1]
        kk = pl.program_id(2)

        @pl.when(kk == 0)
        def _():
            acc_ref[...] = part

        @pl.when(kk > 0)
        def _():
            acc_ref[...] += part

        @pl.when(kk == nk - 1)
        def _():
            finish(acc_ref[...], extra_refs, o_ref)

    o_spec = pl.BlockSpec((tm, tn), lambda i, j, kk: (i, j))
    return pl.pallas_call(
        body, name=name, grid=(m // tm, n // tn, nk),
        in_specs=[a_spec, b_spec] + [o_spec] * ne + [ANY] * len(after), out_specs=o_spec,
        out_shape=jax.ShapeDtypeStruct((m, n), out_dtype),
        scratch_shapes=[pltpu.VMEM((tm, tn), F32)] if nk > 1 else [],
        compiler_params=pltpu.CompilerParams(dimension_semantics=("parallel", "parallel", "arbitrary")),
    )(a, b, *extras, *after)


def _rows_matmul(name, f, rows, pars, b, *, out_dtype, n, k, tm, nrows, tn=MM_TILE, extras=(), epilogue=None, after=()):
    tm, tn = min(tm, nrows), min(tn, n)
    assert nrows % tm == 0 and n % tn == 0, (name, nrows, n, tm, tn)
    nr, npar, ne, nj = len(rows), len(pars), len(extras), n // tn

    def body(*refs):
        row_refs, par_refs, b_ref = refs[:nr], refs[nr:nr + npar], refs[nr + npar]
        extra_refs = refs[nr + npar + 1:nr + npar + 1 + ne]
        a_ref, o_ref = refs[nr + npar + 1 + ne + len(after)], refs[nr + npar + 2 + ne + len(after)]

        def make_a():
            a = f(*[r[...].astype(F32) for r in row_refs], *[p[...] for p in par_refs])[0]
            a_ref[...] = a.astype(a_ref.dtype)

        if nj == 1:
            make_a()
        else:
            pl.when(pl.program_id(1) == 0)(make_a)
        res = _dot(a_ref[...], b_ref[...])
        if epilogue is not None:
            res = epilogue(res, *[e[...] for e in extra_refs])
        o_ref[...] = res.astype(o_ref.dtype)

    o_spec = pl.BlockSpec((tm, tn), lambda i, j: (i, j))
    return pl.pallas_call(
        body, name=name, grid=(nrows // tm, nj),
        in_specs=[pl.BlockSpec((tm, w), lambda i, j, cb=cb: (i, cb)) for _, w, cb in rows]
        + [pl.BlockSpec(tuple(p.shape), lambda i, j, nd=p.ndim: (0,) * nd) for p in pars]
        + [pl.BlockSpec((k, tn), lambda i, j: (0, j))] + [o_spec] * ne + [ANY] * len(after),
        out_specs=[pl.BlockSpec((tm, k), lambda i, j: (i, 0)), o_spec],
        out_shape=[jax.ShapeDtypeStruct((nrows, k), ACT_DTYPE), jax.ShapeDtypeStruct((nrows, n), out_dtype)],
        compiler_params=pltpu.CompilerParams(dimension_semantics=("parallel", "arbitrary")),
    )(*[r[0] for r in rows], *pars, b, *extras, *after)


def _row_spec(tm, width, cb):
    return pl.BlockSpec((tm, width), lambda i: (i, cb))


def _whole_spec(shape):
    nd = len(shape)
    return pl.BlockSpec(tuple(shape), lambda i: (0,) * nd)


def _rows_call(name, f, rows, pars, outs, *, tm, nrows):
    tm = min(tm, nrows)
    nr, npar = len(rows), len(pars)

    def body(*refs):
        rv = [r[...].astype(F32) for r in refs[:nr]]
        pv = [p[...] for p in refs[nr:nr + npar]]
        res = f(*rv, *pv)
        for o_ref, r in zip(refs[nr + npar:], res):
            o_ref[...] = r.astype(o_ref.dtype)

    return pl.pallas_call(
        body, name=name, grid=(nrows // tm,),
        in_specs=[_row_spec(tm, w, cb) for _, w, cb in rows] + [_whole_spec(p.shape) for p in pars],
        out_specs=[_row_spec(tm, w, 0) for w, _ in outs],
        out_shape=[jax.ShapeDtypeStruct((nrows, w), dt) for w, dt in outs],
        compiler_params=pltpu.CompilerParams(dimension_semantics=("parallel",)),
    )(*[r[0] for r in rows], *pars)


def _rows_vjp_call(name, f, rows, pars, cots, drows, *, tm, nrows, cot_mm=None):
    tm = min(tm, nrows)
    nr, npar, nc = len(rows), len(pars), len(cots)
    mm_args, mm_specs = [], []
    if cot_mm is not None:
        mm_a, mm_b, mm_add = cot_mm
        mm_args = [mm_a, mm_b] + ([] if mm_add is None else [mm_add])
        mm_specs = [_row_spec(tm, mm_a.shape[1], 0), _whole_spec(mm_b.shape)]
        mm_specs += [] if mm_add is None else [_row_spec(tm, mm_b.shape[0], 0)]
    alias_bufs, aliases = [], {}
    out_shape, out_specs = [], []
    for (arr, w, cb), d in zip(rows, drows):
        if d is None:
            continue
        dt, into = d
        if into is None:
            out_shape.append(jax.ShapeDtypeStruct((nrows, w), dt))
            out_specs.append(_row_spec(tm, w, 0))
        else:
            buf, total, ocb = into
            if buf is not None:
                aliases[nr + npar + nc + len(alias_bufs)] = len(out_shape)
                alias_bufs.append(buf)
            out_shape.append(jax.ShapeDtypeStruct((nrows, total), dt))
            out_specs.append(_row_spec(tm, w, ocb))
    n_drow = len(out_shape)
    for p in pars:
        out_shape.append(jax.ShapeDtypeStruct(p.shape, F32))
        out_specs.append(_whole_spec(p.shape))
    na = len(alias_bufs)

    def body(*refs):
        rv = [r[...].astype(F32) for r in refs[:nr]]
        pv = [p[...] for p in refs[nr:nr + npar]]
        cv = tuple(c[...].astype(F32) for c in refs[nr + npar:nr + npar + nc])
        o_refs = refs[nr + npar + nc + na + len(mm_args):]
        if mm_args:
            mm_refs = refs[nr + npar + nc + na:nr + npar + nc + na + len(mm_args)]
            c0 = _dot(mm_refs[0][...], mm_refs[1][...], (((1,), (1,)), ((), ())))
            if len(mm_refs) == 3:
                c0 = c0 + mm_refs[2][...].astype(F32)
            cv = (c0,) + cv
        _, vjp = jax.vjp(f, *rv, *pv)
        g = vjp(cv)
        oi = 0
        for ri, d in enumerate(drows):
            if d is not None:
                o_refs[oi][...] = g[ri].astype(o_refs[oi].dtype)
                oi += 1
        first = pl.program_id(0) == 0
        for pi in range(npar):
            acc = o_refs[n_drow + pi]

            @pl.when(first)
            def _(acc=acc):
                acc[...] = jnp.zeros_like(acc)

            acc[...] += g[nr + pi]

    return pl.pallas_call(
        body, name=name, grid=(nrows // tm,),
        in_specs=[_row_spec(tm, w, cb) for _, w, cb in rows] + [_whole_spec(p.shape) for p in pars]
        + [_row_spec(tm, w, cb) for _, w, cb in cots] + [ANY] * na + mm_specs,
        out_specs=out_specs, out_shape=out_shape, input_output_aliases=aliases,
        compiler_params=pltpu.CompilerParams(dimension_semantics=("arbitrary",)),
    )(*[r[0] for r in rows], *pars, *[c[0] for c in cots], *alias_bufs, *mm_args)


def _rms(x, g):
    return x * lax.rsqrt(jnp.mean(x * x, axis=-1, keepdims=True) + EPS) * g


def _f_rms(x, g):
    return (_rms(x, g),)


def _tril_mask():
    return lax.broadcasted_iota(jnp.int32, (CHUNK, CHUNK), 0) >= lax.broadcasted_iota(jnp.int32, (CHUNK, CHUNK), 1)


def _f_branch_a(uvz, ln_g, ln_b, w_s, b_s):
    u = jax.nn.gelu(uvz[:, :E_A])
    v = jax.nn.gelu(uvz[:, E_A:2 * E_A])
    z = uvz[:, 2 * E_A:]
    xc = v - jnp.mean(v, axis=-1, keepdims=True)
    vn = xc * lax.rsqrt(jnp.mean(xc * xc, axis=-1, keepdims=True) + EPS) * ln_g + ln_b
    mask = _tril_mask()
    ws = [jnp.where(mask, w_s[g], 0.0) for g in range(G_A)]
    gw = E_A // G_A
    rows = []
    for c in range(uvz.shape[0] // CHUNK):
        vc = vn[c * CHUNK:(c + 1) * CHUNK]
        rows.append(jnp.concatenate([_dot(ws[g], vc[:, g * gw:(g + 1) * gw]) + b_s[g] for g in range(G_A)], axis=1))
    sv = rows[0] if len(rows) == 1 else jnp.concatenate(rows, axis=0)
    return (u * sv * jax.nn.silu(z),)


def _f_gnorm(y, zb, g):
    yz = y * jax.nn.silu(zb)
    gw = D_INNER // N_GROUPS
    parts = []
    for i in range(N_GROUPS):
        s = yz[:, i * gw:(i + 1) * gw]
        parts.append(s * lax.rsqrt(jnp.mean(s * s, axis=-1, keepdims=True) + EPS))
    return (jnp.concatenate(parts, axis=1) * g,)


def _f_merge(g2, oa, ob):
    return (jax.nn.sigmoid(g2[:, :D_MODEL]) * oa + jax.nn.sigmoid(g2[:, D_MODEL:]) * ob,)


def _f_loss(x1, gp, pe, tgt, fg):
    x2 = x1 + jax.nn.sigmoid(gp) * pe
    err = _rms(x2, fg) - tgt
    return 0.5 * jnp.sum(jnp.mean(err * err, axis=-1))


def _head(x1, p, tgt, ple_g, w_pg, w_ple, fg, *, tm, nrows):
    tm = min(tm, nrows)

    def body(x1_ref, p_ref, t_ref, pg_ref, wpg_ref, wple_ref, fg_ref, hp_ref, dx_ref, dgp_ref, dpe_ref, dfg_ref, loss_ref):
        x1 = x1_ref[...]
        hp_ref[...] = _rms(x1, pg_ref[...]).astype(hp_ref.dtype)
        gp = _dot(hp_ref[...], wpg_ref[...])
        pe = _dot(p_ref[...], wple_ref[...])
        loss, vjp = jax.vjp(_f_loss, x1, gp, pe, t_ref[...], fg_ref[...])
        dx, dgp, dpe, _, dfg = vjp(jnp.ones((), F32))
        dx_ref[...] = dx
        dgp_ref[...] = dgp.astype(dgp_ref.dtype)
        dpe_ref[...] = dpe.astype(dpe_ref.dtype)

        @pl.when(pl.program_id(0) == 0)
        def _():
            dfg_ref[...] = jnp.zeros_like(dfg_ref)
            loss_ref[...] = jnp.zeros_like(loss_ref)

        dfg_ref[...] += dfg
        loss_ref[...] += jnp.full(loss_ref.shape, loss, F32)

    row = _row_spec(tm, D_MODEL, 0)
    act = jax.ShapeDtypeStruct((nrows, D_MODEL), ACT_DTYPE)
    return pl.pallas_call(
        body, name="head", grid=(nrows // tm,),
        in_specs=[row, _row_spec(tm, PLE_DIM, 0), row, _whole_spec((1, D_MODEL)), _whole_spec(w_pg.shape),
                  _whole_spec(w_ple.shape), _whole_spec((1, D_MODEL))],
        out_specs=[row, row, row, row, _whole_spec((1, D_MODEL)), _whole_spec((1, 128))],
        out_shape=[act, jax.ShapeDtypeStruct((nrows, D_MODEL), F32), act, act, jax.ShapeDtypeStruct((1, D_MODEL), F32),
                   jax.ShapeDtypeStruct((1, 128), F32)],
        compiler_params=pltpu.CompilerParams(dimension_semantics=("arbitrary",)),
    )(x1, p, tgt, ple_g, w_pg, w_ple, fg)


def _shift_rows(cur, edge, j, up):
    tm = cur.shape[0]
    row = lax.broadcasted_iota(jnp.int32, cur.shape, 0)
    if up:
        sh = pltpu.roll(cur, tm - j, 0)
        e = jnp.tile(pltpu.roll(edge, 8 - j, 0), (tm // 8, 1))
        return jnp.where(row >= tm - j, e, sh)
    sh = pltpu.roll(cur, j, 0)
    e = jnp.tile(pltpu.roll(edge, j, 0), (tm // 8, 1))
    return jnp.where(row < j, e, sh)


def _conv_pre(cur, prev, w, b):
    acc = cur * w[CONV_K - 1:CONV_K] + b
    taps = [cur]
    for j in range(1, CONV_K):
        s = _shift_rows(cur, prev, j, up=False)
        taps.append(s)
        acc = acc + s * w[CONV_K - 1 - j:CONV_K - j]
    return acc, taps


def _halo_specs(tm, nrows, cb, before):
    nb = tm // 8
    last = nrows // 8 - 1
    if before:
        return pl.BlockSpec((8, XBC_W), lambda i: (jnp.maximum(i * nb - 1, 0), cb))
    return pl.BlockSpec((8, XBC_W), lambda i: (jnp.minimum((i + 1) * nb, last), cb))


def _conv_fwd(proj, conv_w, conv_b, *, tm, nrows):
    tm = min(tm, nrows)

    def body(cur_ref, prev_ref, w_ref, b_ref, o_ref):
        prev = jnp.where(pl.program_id(0) == 0, 0.0, prev_ref[...].astype(F32))
        pre, _ = _conv_pre(cur_ref[...].astype(F32), prev, w_ref[...], b_ref[...])
        o_ref[...] = jax.nn.silu(pre).astype(o_ref.dtype)

    return pl.pallas_call(
        body, name="conv_fwd", grid=(nrows // tm,),
        in_specs=[_row_spec(tm, XBC_W, XBC_CB), _halo_specs(tm, nrows, XBC_CB, True),
                  _whole_spec((CONV_K, XBC_W)), _whole_spec((1, XBC_W))],
        out_specs=_row_spec(tm, XBC_W, 0), out_shape=jax.ShapeDtypeStruct((nrows, XBC_W), ACT_DTYPE),
        compiler_params=pltpu.CompilerParams(dimension_semantics=("parallel",)),
    )(proj, proj, conv_w, conv_b)


def _conv_bwd_pre(proj, conv_w, conv_b, dact, *, tm, nrows):
    tm = min(tm, nrows)
    nb = N_GROUPS * N_STATE

    def body(cur_ref, prev_ref, w_ref, b_ref, dxs_ref, dbm_ref, dcm_ref, dpre_ref, dw_ref, db_ref):
        prev = jnp.where(pl.program_id(0) == 0, 0.0, prev_ref[...].astype(F32))
        pre, taps = _conv_pre(cur_ref[...].astype(F32), prev, w_ref[...], b_ref[...])
        sg = jax.nn.sigmoid(pre)
        dy = jnp.concatenate([dxs_ref[...], dbm_ref[...], dcm_ref[...]], axis=1).astype(F32)
        dpre = dy * sg * (1.0 + pre * (1.0 - sg))
        dpre_ref[...] = dpre.astype(dpre_ref.dtype)

        @pl.when(pl.program_id(0) == 0)
        def _():
            dw_ref[...] = jnp.zeros_like(dw_ref)
            db_ref[...] = jnp.zeros_like(db_ref)

        db_ref[...] += jnp.sum(dpre, axis=0, keepdims=True)
        for j in range(CONV_K):
            k = CONV_K - 1 - j
            dw_ref[k:k + 1, :] += jnp.sum(dpre * taps[j], axis=0, keepdims=True)

    return pl.pallas_call(
        body, name="conv_bwd_pre", grid=(nrows // tm,),
        in_specs=[_row_spec(tm, XBC_W, XBC_CB), _halo_specs(tm, nrows, XBC_CB, True),
                  _whole_spec((CONV_K, XBC_W)), _whole_spec((1, XBC_W)),
                  _row_spec(tm, D_INNER, 0), _row_spec(tm, nb, 0), _row_spec(tm, nb, 0)],
        out_specs=[_row_spec(tm, XBC_W, 0), _whole_spec((CONV_K, XBC_W)), _whole_spec((1, XBC_W))],
        out_shape=[jax.ShapeDtypeStruct((nrows, XBC_W), ACT_DTYPE), jax.ShapeDtypeStruct((CONV_K, XBC_W), F32),
                   jax.ShapeDtypeStruct((1, XBC_W), F32)],
        compiler_params=pltpu.CompilerParams(dimension_semantics=("arbitrary",)),
    )(proj, proj, conv_w, conv_b, *dact)


def _conv_bwd_x(dpre, conv_w, dproj, *, tm, nrows):
    tm = min(tm, nrows)
    ntiles = nrows // tm

    def body(cur_ref, nxt_ref, w_ref, _, o_ref):
        cur = cur_ref[...].astype(F32)
        nxt = jnp.where(pl.program_id(0) == ntiles - 1, 0.0, nxt_ref[...].astype(F32))
        w = w_ref[...]
        acc = cur * w[CONV_K - 1:CONV_K]
        for j in range(1, CONV_K):
            acc = acc + _shift_rows(cur, nxt, j, up=True) * w[CONV_K - 1 - j:CONV_K - j]
        o_ref[...] = acc.astype(o_ref.dtype)

    return pl.pallas_call(
        body, name="conv_bwd_x", grid=(ntiles,),
        in_specs=[_row_spec(tm, XBC_W, 0), _halo_specs(tm, nrows, 0, False), _whole_spec((CONV_K, XBC_W)), ANY],
        out_specs=_row_spec(tm, XBC_W, XBC_CB), out_shape=jax.ShapeDtypeStruct(dproj.shape, dproj.dtype),
        input_output_aliases={3: 0},
        compiler_params=pltpu.CompilerParams(dimension_semantics=("parallel",)),
    )(dpre, dpre, conv_w, dproj)


SSD_SPAN = 4
_XS_GW = D_INNER // N_GROUPS
_NT = (((1,), (1,)), ((), ()))
_TN = (((0,), (0,)), ((), ()))


def _bf16_terms(x, terms):
    parts, rest = [], x
    for _ in range(terms):
        part = rest.astype(jnp.bfloat16)
        parts.append(part)
        rest = rest - part.astype(F32)
    return parts


def _head_lane_matrix():
    return (lax.broadcasted_iota(jnp.int32, (128, _XS_GW), 0)
            == lax.broadcasted_iota(jnp.int32, (128, _XS_GW), 1) // HEAD_DIM).astype(jnp.bfloat16)


@functools.partial(jax.custom_vjp, nondiff_argnums=(1,))
def _head_lanes(cols, terms):
    e = _head_lane_matrix()
    return sum(jnp.dot(t, e, preferred_element_type=F32) for t in _bf16_terms(cols, terms))


def _head_lanes_fwd(cols, terms):
    return _head_lanes(cols, terms), None


def _head_lanes_bwd(terms, _, g):
    e = _head_lane_matrix()
    return (sum(lax.dot_general(t, e, _NT, preferred_element_type=F32) for t in _bf16_terms(g, 2)),)


_head_lanes.defvjp(_head_lanes_fwd, _head_lanes_bwd)


def _ssd_chunk(k, xs, bm, cm, dtr, hprev, dtb, alog, dsk):
    causal, tri, lo = k
    dt = jax.nn.softplus(dtr + dtb)
    da = dt * (-jnp.exp(alog))
    cs = jnp.dot(tri, da, precision=lax.Precision.HIGHEST, preferred_element_type=F32)
    cst = cs.T
    cs_l = _head_lanes(cs, 3)
    xdt = xs * _head_lanes(dt, 2)
    cb = _dot(cm, bm, _NT)
    yd = []
    for q in range(PAIRS_PER_GROUP):
        xq = xdt[:, 128 * q:128 * (q + 1)]
        y2 = [_dot(cb * jnp.exp(jnp.where(causal, cs[:, h:h + 1] - cst[h:h + 1, :], -jnp.inf)), xq)
              for h in (2 * q, 2 * q + 1)]
        yd.append(jnp.where(lo, y2[0], y2[1]))
    y_off = jnp.exp(cs_l) * _dot(cm, hprev, _NT)
    st = _dot(xdt * jnp.exp(cs_l[CHUNK - 1:CHUNK, :] - cs_l), bm, _TN)
    cdec = jnp.exp(cs[CHUNK - 1:CHUNK, :])
    cd_rows = jnp.concatenate(
        [jnp.broadcast_to(cdec[:, h:h + 1], (HEAD_DIM, N_STATE)) for h in range(HEADS_PER_GROUP)], axis=0)
    dsk_l = _head_lanes(jnp.broadcast_to(dsk, (8, 128)), 2)[:1]
    y = jnp.concatenate(yd, axis=1) + y_off + xs * dsk_l
    return y, cd_rows * hprev + st


def _ssd_span(xs, bm, cm, dtr, h0, dtb, alog, dsk):
    li = lax.broadcasted_iota(jnp.int32, (CHUNK, CHUNK), 0)
    si = lax.broadcasted_iota(jnp.int32, (CHUNK, CHUNK), 1)
    causal = li >= si
    k = (causal, causal.astype(F32), si < HEAD_DIM)
    h, ys = h0, []
    for t in range(xs.shape[0] // CHUNK):
        r = slice(t * CHUNK, (t + 1) * CHUNK)
        y, h = _ssd_chunk(k, xs[r], bm[r], cm[r], dtr[r], h, dtb, alog, dsk)
        ys.append(y)
    return (ys[0] if len(ys) == 1 else jnp.concatenate(ys, axis=0)), h


def _ssd_specs(rev, nsteps, rows):
    def s_of(s):
        return nsteps - 1 - s if rev else s

    xs = pl.BlockSpec((rows, _XS_GW), lambda g, s: (s_of(s), g))
    bm = pl.BlockSpec((rows, N_STATE), lambda g, s: (s_of(s), D_INNER // N_STATE + g))
    cm = pl.BlockSpec((rows, N_STATE), lambda g, s: (s_of(s), D_INNER // N_STATE + N_GROUPS + g))
    dt = pl.BlockSpec((rows, 128), lambda g, s: (s_of(s), g))
    par = pl.BlockSpec((1, 128), lambda g, s: (0, g))
    st = pl.BlockSpec((None, None, _XS_GW, N_STATE), lambda g, s: (g, s_of(s), 0, 0))
    return xs, bm, cm, dt, par, st


def _ssd_fwd(act, dtr, dtb, alog, dsk, *, nrows):
    rows = CHUNK * min(SSD_SPAN, nrows // CHUNK)
    nsteps = nrows // rows
    xs, bm, cm, dt, par, st = _ssd_specs(False, nsteps, rows)

    def body(xs_ref, b_ref, c_ref, dt_ref, dtb_ref, al_ref, dk_ref, y_ref, st_ref, h_ref):
        @pl.when(pl.program_id(1) == 0)
        def _():
            h_ref[...] = jnp.zeros_like(h_ref)

        h0 = h_ref[...]
        st_ref[...] = h0
        y, hnew = _ssd_span(xs_ref[...].astype(F32), b_ref[...].astype(F32), c_ref[...].astype(F32), dt_ref[...],
                            h0, dtb_ref[...], al_ref[...], dk_ref[...])
        y_ref[...] = y.astype(y_ref.dtype)
        h_ref[...] = hnew

    return pl.pallas_call(
        body, name="ssd_fwd", grid=(N_GROUPS, nsteps),
        in_specs=[xs, bm, cm, dt, par, par, par], out_specs=[xs, st],
        out_shape=[jax.ShapeDtypeStruct((nrows, D_INNER), ACT_DTYPE),
                   jax.ShapeDtypeStruct((N_GROUPS, nsteps, _XS_GW, N_STATE), F32)],
        scratch_shapes=[pltpu.VMEM((_XS_GW, N_STATE), F32)],
        compiler_params=pltpu.CompilerParams(dimension_semantics=("arbitrary", "arbitrary")),
    )(act, act, act, dtr, dtb, alog, dsk)


def _ssd_bwd(act, dtr, dtb, alog, dsk, states, dy, *, nrows):
    rows = CHUNK * min(SSD_SPAN, nrows // CHUNK)
    nsteps = nrows // rows
    xs, bm, cm, dt, par, st = _ssd_specs(True, nsteps, rows)

    def body(xs_ref, b_ref, c_ref, dt_ref, dtb_ref, al_ref, dk_ref, st_ref, dy_ref,
             dxs_ref, db_ref, dc_ref, ddt_ref, ddtb_ref, dal_ref, ddk_ref, dh_ref):
        @pl.when(pl.program_id(1) == 0)
        def _():
            dh_ref[...] = jnp.zeros_like(dh_ref)
            ddtb_ref[...] = jnp.zeros_like(ddtb_ref)
            dal_ref[...] = jnp.zeros_like(dal_ref)
            ddk_ref[...] = jnp.zeros_like(ddk_ref)

        _, vjp = jax.vjp(_ssd_span, xs_ref[...].astype(F32), b_ref[...].astype(F32), c_ref[...].astype(F32),
                         dt_ref[...], st_ref[...], dtb_ref[...], al_ref[...], dk_ref[...])
        dxs, db, dc, ddt, dh, ddtb, dal, ddk = vjp((dy_ref[...].astype(F32), dh_ref[...]))
        dxs_ref[...] = dxs.astype(dxs_ref.dtype)
        db_ref[...] = db.astype(db_ref.dtype)
        dc_ref[...] = dc.astype(dc_ref.dtype)
        ddt_ref[...] = ddt
        dh_ref[...] = dh
        ddtb_ref[...] += ddtb
        dal_ref[...] += dal
        ddk_ref[...] += ddk

    nb = N_GROUPS * N_STATE
    bspec = pl.BlockSpec((rows, N_STATE), lambda g, s: (nsteps - 1 - s, g))
    return pl.pallas_call(
        body, name="ssd_bwd", grid=(N_GROUPS, nsteps),
        in_specs=[xs, bm, cm, dt, par, par, par, st, xs],
        out_specs=[xs, bspec, bspec, dt, par, par, par],
        out_shape=[jax.ShapeDtypeStruct((nrows, D_INNER), ACT_DTYPE), jax.ShapeDtypeStruct((nrows, nb), ACT_DTYPE),
                   jax.ShapeDtypeStruct((nrows, nb), ACT_DTYPE), jax.ShapeDtypeStruct((nrows, DT_W), F32),
                   jax.ShapeDtypeStruct((1, DT_W), F32), jax.ShapeDtypeStruct((1, DT_W), F32),
                   jax.ShapeDtypeStruct((1, DT_W), F32)],
        scratch_shapes=[pltpu.VMEM((_XS_GW, N_STATE), F32)],
        compiler_params=pltpu.CompilerParams(dimension_semantics=("arbitrary", "arbitrary")),
    )(act, act, act, dtr, dtb, alog, dsk, states, dy)


def _add_epilogue(acc, r):
    return r + acc


def _rms_and_skip(x, g):
    return _rms(x, g), x


def _forward_backward(x, p, tgt, w, late_weights=None, after=()):
    s = x.shape[0]
    act_t, f32 = ACT_DTYPE, F32
    mm = functools.partial(_matmul)
    h, proj = _rows_matmul("proj", _f_rms, [(x, D_MODEL, 0)], [w["norm_g"]], w["w_main"], out_dtype=act_t,
                           n=MAIN_W, k=D_MODEL, tm=1024, nrows=s, after=after)
    dtr = mm(h, w["w_dt"], mode="nn", name="proj_dt", out_dtype=f32, m=s, n=DT_W, k=D_MODEL)
    act = _conv_fwd(proj, w["conv_w"], w["conv_b"], tm=512, nrows=s)
    y, states = _ssd_fwd(act, dtr, w["dt_bias"], w["a_log"], w["d_skip"], nrows=s)
    if late_weights is not None:
        w = {**w, **late_weights(states)}
    a_pars = [w["ln_a_g"], w["ln_a_b"], w["w_s"], w["b_s"]]
    y_a, o_a = _rows_matmul("out_a", _f_branch_a, [(proj, UVZ_W, UVZ_CB)], a_pars, w["w_oa"], out_dtype=act_t,
                            n=D_MODEL, k=E_A, tm=256, nrows=s)
    gn_rows = [(y, D_INNER, 0), (proj, ZB_W, ZB_CB)]
    y_b, o_b = _rows_matmul("out_b", _f_gnorm, gn_rows, [w["ssm_norm_g"]], w["w_ob"], out_dtype=act_t,
                            n=D_MODEL, k=D_INNER, tm=512, nrows=s)
    mg_rows = [(proj, G_W, G_CB), (o_a, D_MODEL, 0), (o_b, D_MODEL, 0)]
    merged, x1 = _rows_matmul("out_proj", _f_merge, mg_rows, [], w["w_out"], out_dtype=f32, n=D_MODEL, k=D_MODEL,
                              tm=512, nrows=s, extras=(x,), epilogue=_add_epilogue)
    g = {}
    hp, dx2, dgp, dpe, g["final_g"], loss = _head(x1, p, tgt, w["ple_norm_g"], w["w_pg"], w["w_ple"], w["final_g"],
                                                   tm=256, nrows=s)
    g["w_pg"] = mm(hp, dgp, mode="tn", name="d_w_pg", out_dtype=f32, m=D_MODEL, n=D_MODEL, k=s)
    g["w_ple"] = mm(p, dpe, mode="tn", name="d_w_ple", out_dtype=f32, m=PLE_DIM, n=D_MODEL, k=s)
    dx1, g["ple_norm_g"] = _rows_vjp_call(
        "ple_norm_bwd", _rms_and_skip, [(x1, D_MODEL, 0)], [w["ple_norm_g"]], [(dx2, D_MODEL, 0)],
        [(f32, None)], tm=512, nrows=s, cot_mm=(dgp, w["w_pg"], None))
    g["w_out"] = mm(merged, dx1, mode="tn", name="d_w_out", out_dtype=f32, m=D_MODEL, n=D_MODEL, k=s)
    dproj, do_a, do_b = _rows_vjp_call(
        "merge_bwd", _f_merge, mg_rows, [], [],
        [(act_t, (None, MAIN_W, G_CB)), (act_t, None), (act_t, None)], tm=512, nrows=s, cot_mm=(dx1, w["w_out"], None))
    g["w_oa"] = mm(y_a, do_a, mode="tn", name="d_w_oa", out_dtype=f32, m=E_A, n=D_MODEL, k=s)
    g["w_ob"] = mm(y_b, do_b, mode="tn", name="d_w_ob", out_dtype=f32, m=D_INNER, n=D_MODEL, k=s)
    dy, dproj, g["ssm_norm_g"] = _rows_vjp_call(
        "gnorm_bwd", _f_gnorm, gn_rows, [w["ssm_norm_g"]], [],
        [(act_t, None), (act_t, (dproj, MAIN_W, ZB_CB))], tm=256, nrows=s, cot_mm=(do_b, w["w_ob"], None))
    dxs, dbm, dcm, ddtr, g["dt_bias"], g["a_log"], g["d_skip"] = _ssd_bwd(
        act, dtr, w["dt_bias"], w["a_log"], w["d_skip"], states, dy, nrows=s)
    dpre, g["conv_w"], g["conv_b"] = _conv_bwd_pre(proj, w["conv_w"], w["conv_b"], (dxs, dbm, dcm), tm=512, nrows=s)
    dproj = _conv_bwd_x(dpre, w["conv_w"], dproj, tm=512, nrows=s)
    dproj, g["ln_a_g"], g["ln_a_b"], g["w_s"], g["b_s"] = _rows_vjp_call(
        "branch_a_bwd", _f_branch_a, [(proj, UVZ_W, UVZ_CB)], a_pars, [],
        [(act_t, (dproj, MAIN_W, UVZ_CB))], tm=256, nrows=s, cot_mm=(do_a, w["w_oa"], None))
    g["w_main"] = mm(h, dproj, mode="tn", name="d_w_main", out_dtype=f32, m=D_MODEL, n=MAIN_W, k=s)
    g["w_dt"] = mm(h, ddtr, mode="tn", name="d_w_dt", out_dtype=f32, m=D_MODEL, n=DT_W, k=s)
    return loss, g, (dproj, ddtr, dx1)


def _input_grad(x, w, ctx, after=()):
    dproj, ddtr, dx1 = ctx
    s = x.shape[0]
    dh = _matmul(dproj, w["w_main"], mode="nt", name="d_h_main", out_dtype=F32, m=s, n=D_MODEL, k=MAIN_W, after=after)
    return _rows_vjp_call(
        "pre_norm_bwd", _rms_and_skip, [(x, D_MODEL, 0)], [w["norm_g"]], [(dx1, D_MODEL, 0)],
        [(F32, None)], tm=512, nrows=s, cot_mm=(ddtr, w["w_dt"], dh))


def _local_step(x, p, tgt, w):
    loss, g, ctx = _forward_backward(x, p, tgt, w)
    grad_x, g["norm_g"] = _input_grad(x, w, ctx)
    return loss, grad_x, g


_O_ZB = 3 * E_A
_O_XBC = _O_ZB + D_INNER
_O_DT = _O_XBC + CONV_DIM
_O_G = _O_DT + N_HEADS


def _heads_to_lanes(v):
    r = v.shape[0]
    v = v.reshape(r, N_GROUPS, HEADS_PER_GROUP)
    return jnp.pad(v, ((0, 0), (0, 0), (0, 128 - HEADS_PER_GROUP))).reshape(r, DT_W)


def _lanes_to_heads(v):
    r = v.shape[0]
    return v.reshape(r, N_GROUPS, 128)[:, :, :HEADS_PER_GROUP].reshape(r, N_HEADS)


def _block_cols(blocks, a, b):
    parts = []
    for k in range(N_CHIPS):
        lo, hi = max(a, k * W_IN_BLOCK), min(b, (k + 1) * W_IN_BLOCK)
        if lo < hi:
            parts.append(blocks[k][:, lo - k * W_IN_BLOCK:hi - k * W_IN_BLOCK])
    return parts


_W_IN_SEGMENTS = ((0, _O_ZB, "m", 0), (_O_ZB, _O_XBC, "m", UVZ_W + XBC_W), (_O_XBC, _O_DT, "m", UVZ_W),
                  (_O_DT, _O_G, "d", 0), (_O_G, N_IN, "m", MAIN_W - G_W))


def _w_in_grad_blocks(gm, gdt):
    blocks = []
    for k in range(N_CHIPS):
        a, b = k * W_IN_BLOCK, (k + 1) * W_IN_BLOCK
        parts = []
        for s, e, src, off in _W_IN_SEGMENTS:
            lo, hi = max(a, s), min(b, e)
            if lo < hi:
                parts.append((gm if src == "m" else gdt)[:, off + lo - s:off + hi - s])
        blocks.append(jnp.concatenate(parts, axis=1))
    return jnp.stack(blocks)


def _layout_weights(f, w_in_blocks=None):
    w = dict(f)
    if w_in_blocks is None:
        w_in = w.pop("w_in")
        w_in_blocks = jnp.stack([w_in[:, k * W_IN_BLOCK:(k + 1) * W_IN_BLOCK] for k in range(N_CHIPS)])
    cols = functools.partial(_block_cols, w_in_blocks)
    w["w_main"] = jnp.concatenate(cols(0, _O_ZB) + cols(_O_XBC, _O_DT) + cols(_O_ZB, _O_XBC) + cols(_O_G, N_IN), axis=1)
    w["w_dt"] = _heads_to_lanes(jnp.concatenate(cols(_O_DT, _O_G), axis=1))
    w["b_s"] = f["b_s"].reshape(G_A, CHUNK, 1)
    for n in ("dt_bias", "a_log", "d_skip"):
        w[n] = _heads_to_lanes(f[n])
    return w


def _natural_grads(g):
    out = dict(g)
    gm = out.pop("w_main")
    gdt = _lanes_to_heads(out.pop("w_dt"))
    out["w_in"] = jnp.concatenate(
        [gm[:, :UVZ_W], gm[:, UVZ_W + XBC_W:UVZ_W + XBC_W + ZB_W], gm[:, UVZ_W:UVZ_W + XBC_W], gdt, gm[:, MAIN_W - G_W:]],
        axis=1)
    out["b_s"] = g["b_s"].reshape(G_A, CHUNK)
    for n in ("dt_bias", "a_log", "d_skip"):
        out[n] = _lanes_to_heads(g[n])
    return out


def _place():
    return lax.axis_index("x"), lax.axis_index("y"), lax.axis_index("c")


def _other_chips(x, y):
    return [(1 - x, y), (x, 1 - y), (1 - x, 1 - y)]


def _rcopy(src, dst, ssem, rsem, dev):
    return pltpu.make_async_remote_copy(src_ref=src, dst_ref=dst, send_sem=ssem, recv_sem=rsem,
                                        device_id=dev, device_id_type=MESH)


def _half(ref_rows, half):
    hs = ref_rows // 2
    return pl.ds(pl.multiple_of(half * hs, 16), hs)


def _gather_weights(shards, conv_shard):
    nw = len(shards)

    def body(*refs):
        sh, cv = refs[:nw], refs[nw]
        out, cvo = refs[nw + 1:2 * nw + 1], refs[2 * nw + 1]
        ici_s, ici_r, fw_s, fw_r, own_s, own_r, cv_s, cv_r = refs[2 * nw + 2:]
        x, y, c = _place()
        me, sib, chips = 2 * x + y, (x, y, 1 - c), _other_chips(x, y)
        own = [_rcopy(sh[w], out[w].at[me], own_s.at[w], own_r.at[w], sib) for w in range(nw)]
        own.append(_rcopy(cv, cvo.at[me], own_s.at[nw], own_r.at[nw], sib))
        for cp in own:
            cp.start()
        sends = []
        for w in range(nw):
            mine = _half(sh[w].shape[0], c)
            for j, chip in enumerate(chips):
                sends.append(_rcopy(sh[w].at[mine], out[w].at[me, mine], ici_s.at[3 * w + j], ici_r.at[3 * w + j], (*chip, c)))
        for j, chip in enumerate(chips):
            sends.append(_rcopy(cv, cvo.at[me], cv_s.at[j], cv_r.at[j], (*chip, c)))
        for cp in sends:
            cp.start()
        for w in range(nw):
            mine = _half(sh[w].shape[0], c)
            for j, chip in enumerate(chips):
                slab = out[w].at[2 * chip[0] + chip[1], mine]
                _rcopy(slab, slab, ici_s.at[3 * w + j], ici_r.at[3 * w + j], (*chip, c)).wait_recv()
                fwd = _rcopy(slab, slab, fw_s.at[3 * w + j], fw_r.at[3 * w + j], sib)
                fwd.start()
                sends.append(fwd)
        for j, chip in enumerate(chips):
            blk = cvo.at[2 * chip[0] + chip[1]]
            _rcopy(blk, blk, cv_s.at[j], cv_r.at[j], (*chip, c)).wait_recv()
        for w in range(nw):
            theirs = _half(sh[w].shape[0], 1 - c)
            for j, chip in enumerate(chips):
                slab = out[w].at[2 * chip[0] + chip[1], theirs]
                _rcopy(slab, slab, fw_s.at[3 * w + j], fw_r.at[3 * w + j], sib).wait_recv()
        for cp in sends:
            cp.wait_send()
        for cp in own:
            cp.wait()

    dma = pltpu.SemaphoreType.DMA
    return pl.pallas_call(
        body, name="gather_weights",
        in_specs=[ANY] * (nw + 1), out_specs=[ANY] * (nw + 1),
        out_shape=[jax.ShapeDtypeStruct((N_CHIPS,) + s.shape, s.dtype) for s in shards]
        + [jax.ShapeDtypeStruct((N_CHIPS,) + conv_shard.shape, conv_shard.dtype)],
        scratch_shapes=[dma((3 * nw,)), dma((3 * nw,)), dma((3 * nw,)), dma((3 * nw,)), dma((nw + 1,)), dma((nw + 1,)),
                        dma((3,)), dma((3,))],
    )(*shards, conv_shard)


_HBM = pl.BlockSpec(memory_space=pltpu.HBM)
_SEM = pl.BlockSpec(memory_space=pltpu.SEMAPHORE)
_EFFECT = pltpu.SideEffectType.DATAFLOW_SIDE_EFFECTING


def _late_gather_copies(sh, out, s_sem, r_sem):
    x, y, c = _place()
    to = [(*chip, c) for chip in _other_chips(x, y)] + [(x, y, 1 - c)]
    return [_rcopy(sh[w], out[w].at[2 * x + y], s_sem.at[4 * w + j], r_sem.at[4 * w + j], dev)
            for w in range(len(sh)) for j, dev in enumerate(to)]


def _late_gather_start(shards):
    n = len(shards)
    lands = [lax.empty((N_CHIPS,) + a.shape, a.dtype) for a in shards]

    def body(*refs):
        for cp in _late_gather_copies(refs[:n], refs[n:2 * n], refs[2 * n], refs[2 * n + 1]):
            cp.start()
        refs[-1][...] = jnp.zeros_like(refs[-1])

    dma = pltpu.SemaphoreType.DMA
    hbm = [pltpu.with_memory_space_constraint(a, pltpu.HBM) for a in list(shards) + lands]
    out = pl.pallas_call(
        body, name="late_gather_start",
        out_shape=[dma((4 * n,)), dma((4 * n,))] + [pltpu.HBM(a.shape, a.dtype) for a in hbm]
        + [jax.ShapeDtypeStruct((8, 128), F32)],
        in_specs=[_HBM] * (2 * n), out_specs=[_SEM, _SEM] + [_HBM] * (2 * n) + [pl.BlockSpec(memory_space=pltpu.VMEM)],
        input_output_aliases={i: 2 + i for i in range(2 * n)},
        compiler_params=pltpu.CompilerParams(has_side_effects=_EFFECT),
    )(*hbm)
    return out[0], out[1], out[2:2 + n], out[2 + n:2 + 2 * n], out[-1]


def _late_gather_wait(s_sem, r_sem, srcs, lands, after):
    n = len(srcs)

    def body(*refs):
        for cp in _late_gather_copies(refs[:n], refs[n:2 * n], refs[2 * n], refs[2 * n + 1]):
            cp.wait_send()
            cp.wait_recv()

    out = pl.pallas_call(
        body, name="late_gather_wait",
        out_shape=[pltpu.HBM(a.shape, a.dtype) for a in list(srcs) + list(lands)],
        in_specs=[_HBM] * (2 * n) + [_SEM, _SEM, ANY], out_specs=[_HBM] * (2 * n),
        input_output_aliases={i: i for i in range(2 * n)},
        compiler_params=pltpu.CompilerParams(has_side_effects=_EFFECT),
    )(*srcs, *lands, s_sem, r_sem, after)
    return out[n:]


def _swap_with_sibling(arrs):
    n = len(arrs)

    def body(*refs):
        src, dst, s_sem, r_sem = refs[:n], refs[n:2 * n], refs[2 * n], refs[2 * n + 1]
        x, y, c = _place()
        cps = [_rcopy(src[i], dst[i], s_sem.at[i], r_sem.at[i], (x, y, 1 - c)) for i in range(n)]
        for cp in cps:
            cp.start()
        for cp in cps:
            cp.wait()

    dma = pltpu.SemaphoreType.DMA
    return pl.pallas_call(
        body, name="swap_with_sibling", in_specs=[ANY] * n, out_specs=[ANY] * n,
        out_shape=[jax.ShapeDtypeStruct(a.shape, a.dtype) for a in arrs], scratch_shapes=[dma((n,)), dma((n,))],
    )(*arrs)


def _scatter_copies(src, land, s_sem, r_sem):
    x, y, c = _place()
    return [_rcopy(src[i].at[2 * chip[0] + chip[1]], land[i].at[j], s_sem.at[3 * i + j], r_sem.at[3 * i + j], (*chip, c))
            for i in range(len(src)) for j, chip in enumerate(_other_chips(x, y))]


def _scatter_blocks_start(arrs):
    n = len(arrs)
    lands = [lax.empty((3,) + a.shape[1:], a.dtype) for a in arrs]

    def body(*refs):
        src, land, s_sem, r_sem, token = refs[:n], refs[n:2 * n], refs[2 * n], refs[2 * n + 1], refs[-1]
        for cp in _scatter_copies(src, land, s_sem, r_sem):
            cp.start()
        token[...] = jnp.zeros_like(token)

    dma = pltpu.SemaphoreType.DMA
    hbm = [pltpu.with_memory_space_constraint(a, pltpu.HBM) for a in list(arrs) + lands]
    out = pl.pallas_call(
        body, name="scatter_blocks_start",
        out_shape=[dma((3 * n,)), dma((3 * n,))] + [pltpu.HBM(a.shape, a.dtype) for a in hbm]
        + [jax.ShapeDtypeStruct((8, 128), F32)],
        in_specs=[_HBM] * (2 * n), out_specs=[_SEM, _SEM] + [_HBM] * (2 * n) + [pl.BlockSpec(memory_space=pltpu.VMEM)],
        input_output_aliases={i: 2 + i for i in range(2 * n)},
        compiler_params=pltpu.CompilerParams(has_side_effects=_EFFECT),
    )(*hbm)
    return out[0], out[1], out[2:2 + n], out[2 + n:2 + 2 * n], out[-1]


def _scatter_blocks_wait(s_sem, r_sem, srcs, lands, after):
    n = len(srcs)

    def body(*refs):
        src, land, s_sem, r_sem = refs[:n], refs[n:2 * n], refs[2 * n], refs[2 * n + 1]
        for cp in _scatter_copies(src, land, s_sem, r_sem):
            cp.wait_send()
            cp.wait_recv()

    out = pl.pallas_call(
        body, name="scatter_blocks_wait",
        out_shape=[pltpu.HBM(a.shape, a.dtype) for a in list(srcs) + list(lands)],
        in_specs=[_HBM] * (2 * n) + [_SEM, _SEM, ANY], out_specs=[_HBM] * (2 * n),
        input_output_aliases={i: i for i in range(2 * n)},
        compiler_params=pltpu.CompilerParams(has_side_effects=_EFFECT),
    )(*srcs, *lands, s_sem, r_sem, after)
    return out[n:]


def _share_halves(arrs):
    n = len(arrs)

    def body(*refs):
        buf, s_sem, r_sem = refs[n:2 * n], refs[2 * n], refs[2 * n + 1]
        x, y, c = _place()
        cps = []
        for i in range(n):
            mine = buf[i].at[_half(buf[i].shape[0], c)]
            cps.append(_rcopy(mine, mine, s_sem.at[i], r_sem.at[i], (x, y, 1 - c)))
        for cp in cps:
            cp.start()
        for i in range(n):
            theirs = buf[i].at[_half(buf[i].shape[0], 1 - c)]
            _rcopy(theirs, theirs, s_sem.at[i], r_sem.at[i], (x, y, 1 - c)).wait_recv()
        for cp in cps:
            cp.wait_send()

    dma = pltpu.SemaphoreType.DMA
    return pl.pallas_call(
        body, name="share_halves", in_specs=[ANY] * n, out_specs=[ANY] * n,
        out_shape=[jax.ShapeDtypeStruct(a.shape, a.dtype) for a in arrs],
        input_output_aliases={i: i for i in range(n)}, scratch_shapes=[dma((n,)), dma((n,))],
    )(*arrs)


def _allreduce_small(packed):
    rows = packed.shape[0]

    def body(p_ref, o_ref, buf, s_sem, r_sem):
        x, y, c = _place()
        me = 4 * x + 2 * y + c
        buf[me] = p_ref[...]
        cps = []
        for d in range(1, N_DEV):
            px, py, pc = ((1 - x) if d & 4 else x), ((1 - y) if d & 2 else y), ((1 - c) if d & 1 else c)
            cps.append(_rcopy(p_ref, buf.at[me], s_sem.at[d - 1], r_sem.at[d - 1], (px, py, pc)))
        for cp in cps:
            cp.start()
        for d in range(1, N_DEV):
            px, py, pc = ((1 - x) if d & 4 else x), ((1 - y) if d & 2 else y), ((1 - c) if d & 1 else c)
            landed = buf.at[4 * px + 2 * py + pc]
            _rcopy(p_ref, landed, s_sem.at[d - 1], r_sem.at[d - 1], (px, py, pc)).wait_recv()
        for cp in cps:
            cp.wait_send()
        acc = buf[0]
        for d in range(1, N_DEV):
            acc = acc + buf[d]
        o_ref[...] = acc

    dma = pltpu.SemaphoreType.DMA
    return pl.pallas_call(
        body, name="allreduce_small", out_shape=jax.ShapeDtypeStruct(packed.shape, F32),
        scratch_shapes=[pltpu.VMEM((N_DEV, rows, 128), F32), dma((N_DEV - 1,)), dma((N_DEV - 1,))],
    )(packed)


def _row_tile(rows, cols):
    tr = max(8, min(rows, (1 << 20) // (4 * cols) // 8 * 8))
    while rows % tr:
        tr -= 8
    return tr


def _chip_sum(name, g5, recv, c_arr):
    _, _, hs, cols = g5.shape
    tr = _row_tile(hs, cols)

    def body(_, a_ref, b_ref, o32_ref, ow_ref):
        s = a_ref[...] + b_ref[...].astype(F32)
        o32_ref[...] = s
        ow_ref[...] = s.astype(ow_ref.dtype)

    blk = pl.BlockSpec((None, tr, cols), lambda b, i, c: (b, i, 0))
    return pl.pallas_call(
        body, name=name,
        grid_spec=pltpu.PrefetchScalarGridSpec(
            num_scalar_prefetch=1, grid=(N_CHIPS, hs // tr),
            in_specs=[pl.BlockSpec((None, None, tr, cols), lambda b, i, c: (b, c[0], i, 0)), blk], out_specs=[blk, blk]),
        out_shape=[jax.ShapeDtypeStruct((N_CHIPS, hs, cols), F32), jax.ShapeDtypeStruct((N_CHIPS, hs, cols), WIRE_DTYPE)],
    )(c_arr, g5, recv)


def _final_sum(name, own, recv, place_arr):
    _, hs, cols = own.shape
    tr = _row_tile(hs, cols)
    nt = hs // tr

    def body(_, a_ref, r_ref, o_ref):
        o_ref[...] = ((a_ref[...] + r_ref[0].astype(F32)) + r_ref[1].astype(F32)) + r_ref[2].astype(F32)

    return pl.pallas_call(
        body, name=name,
        grid_spec=pltpu.PrefetchScalarGridSpec(
            num_scalar_prefetch=1, grid=(nt,),
            in_specs=[pl.BlockSpec((None, tr, cols), lambda i, m: (m[0], i, 0)),
                      pl.BlockSpec((3, tr, cols), lambda i, m: (0, i, 0))],
            out_specs=pl.BlockSpec((tr, cols), lambda i, m: (m[1] * nt + i, 0))),
        out_shape=jax.ShapeDtypeStruct((2 * hs, cols), F32),
    )(place_arr, own, recv)


def _adamw(w, g, m, v):
    m = ADAM_B1 * m + (1.0 - ADAM_B1) * g
    v = ADAM_B2 * v + (1.0 - ADAM_B2) * (g * g)
    m_hat = m / (1.0 - ADAM_B1 ** ADAM_STEP)
    v_hat = v / (1.0 - ADAM_B2 ** ADAM_STEP)
    return -ADAM_LR * (m_hat / (jnp.sqrt(v_hat) + ADAM_EPS) + ADAM_WD * w), m, v


def _adamw_call(name, w, g, m, v):
    rows, cols = w.shape
    tr = _row_tile(rows, cols)

    def body(w_ref, g_ref, m_ref, v_ref, d_ref, nm_ref, nv_ref):
        d_ref[...], nm_ref[...], nv_ref[...] = _adamw(w_ref[...], g_ref[...], m_ref[...], v_ref[...])

    blk = pl.BlockSpec((tr, cols), lambda i: (i, 0))
    return pl.pallas_call(
        body, name=name, grid=(rows // tr,), in_specs=[blk] * 4, out_specs=[blk] * 3,
        out_shape=[jax.ShapeDtypeStruct(w.shape, F32)] * 3,
        compiler_params=pltpu.CompilerParams(dimension_semantics=("parallel",)),
    )(w, g, m, v)


def _adamw_small(ws, gs, ms, vs):
    n = len(ws)

    def body(*refs):
        for i in range(n):
            w_ref, g_ref, m_ref, v_ref = (refs[k * n + i] for k in range(4))
            d, nm, nv = _adamw(w_ref[...], g_ref[...], m_ref[...], v_ref[...])
            refs[4 * n + i][...] = d
            refs[5 * n + i][...] = nm
            refs[6 * n + i][...] = nv

    out = pl.pallas_call(
        body, name="adamw_small", out_shape=[jax.ShapeDtypeStruct(a.shape, F32) for a in ws] * 3,
    )(*ws, *gs, *ms, *vs)
    return out[:n], out[n:2 * n], out[2 * n:]


_BIG = ("w_in", "w_oa", "w_ob", "w_out", "w_pg", "w_ple")
_SMALL = ("norm_g", "ln_a_g", "ln_a_b", "w_s", "b_s", "conv_w", "conv_b", "dt_bias", "a_log", "d_skip", "ssm_norm_g",
          "ple_norm_g", "final_g")
_WEIGHTS = ("norm_g", "w_in", "ln_a_g", "ln_a_b", "w_s", "b_s", "conv_w", "conv_b", "dt_bias", "a_log", "d_skip",
            "ssm_norm_g", "w_oa", "w_ob", "w_out", "ple_norm_g", "w_pg", "w_ple", "final_g")
_COL_SHARDED = ("w_in", "w_ple")
_PACK = 1024


def _blocks_to_full(col_sharded, blocks):
    if col_sharded:
        return jnp.concatenate([blocks[k] for k in range(N_CHIPS)], axis=1)
    return blocks.reshape(N_CHIPS * blocks.shape[1], blocks.shape[2])


def _full_to_blocks(col_sharded, full):
    if col_sharded:
        w = full.shape[1] // N_CHIPS
        return jnp.stack([full[:, k * w:(k + 1) * w] for k in range(N_CHIPS)])
    return full.reshape(N_CHIPS, full.shape[0] // N_CHIPS, full.shape[1])


def _two_d(n, a):
    if n == "w_s":
        return a.reshape(G_A * CHUNK, CHUNK)
    if n in ("b_s", "conv_w"):
        return a.reshape(a.shape[-2], a.shape[-1])
    return a.reshape(1, a.shape[-1])


def kernel(x, p, norm_g, w_in, ln_a_g, ln_a_b, w_s, b_s, conv_w, conv_b, dt_bias, a_log, d_skip, ssm_norm_g, w_oa, w_ob, w_out, ple_norm_g, w_pg, w_ple, final_g, loss_target, m_norm_g, m_w_in, m_ln_a_g, m_ln_a_b, m_w_s, m_b_s, m_conv_w, m_conv_b, m_dt_bias, m_a_log, m_d_skip, m_ssm_norm_g, m_w_oa, m_w_ob, m_w_out, m_ple_norm_g, m_w_pg, m_w_ple, m_final_g, v_norm_g, v_w_in, v_ln_a_g, v_ln_a_b, v_w_s, v_b_s, v_conv_w, v_conv_b, v_dt_bias, v_a_log, v_d_skip, v_ssm_norm_g, v_w_oa, v_w_ob, v_w_out, v_ple_norm_g, v_w_pg, v_w_ple, v_final_g):
    wt = dict(norm_g=norm_g, w_in=w_in, ln_a_g=ln_a_g, ln_a_b=ln_a_b, w_s=w_s, b_s=b_s, conv_w=conv_w, conv_b=conv_b,
              dt_bias=dt_bias, a_log=a_log, d_skip=d_skip, ssm_norm_g=ssm_norm_g, w_oa=w_oa, w_ob=w_ob, w_out=w_out,
              ple_norm_g=ple_norm_g, w_pg=w_pg, w_ple=w_ple, final_g=final_g)
    mom = dict(norm_g=m_norm_g, w_in=m_w_in, ln_a_g=m_ln_a_g, ln_a_b=m_ln_a_b, w_s=m_w_s, b_s=m_b_s, conv_w=m_conv_w,
               conv_b=m_conv_b, dt_bias=m_dt_bias, a_log=m_a_log, d_skip=m_d_skip, ssm_norm_g=m_ssm_norm_g, w_oa=m_w_oa,
               w_ob=m_w_ob, w_out=m_w_out, ple_norm_g=m_ple_norm_g, w_pg=m_w_pg, w_ple=m_w_ple, final_g=m_final_g)
    vel = dict(norm_g=v_norm_g, w_in=v_w_in, ln_a_g=v_ln_a_g, ln_a_b=v_ln_a_b, w_s=v_w_s, b_s=v_b_s, conv_w=v_conv_w,
               conv_b=v_conv_b, dt_bias=v_dt_bias, a_log=v_a_log, d_skip=v_d_skip, ssm_norm_g=v_ssm_norm_g, w_oa=v_w_oa,
               w_ob=v_w_ob, w_out=v_w_out, ple_norm_g=v_ple_norm_g, w_pg=v_w_pg, w_ple=v_w_ple, final_g=v_final_g)
    xi, yi, ci = _place()
    me = 2 * xi + yi
    c_arr = jnp.reshape(ci, (1,)).astype(jnp.int32)
    place_arr = jnp.stack([me, ci]).astype(jnp.int32)

    shard = {n: wt[n][0] for n in _BIG}
    wire = {n: shard[n].astype(WIRE_DTYPE) for n in _BIG}
    w_in_blocks, conv_blocks = _gather_weights([wire["w_in"]], conv_w[0])
    g_ssem, g_rsem, g_sent, g_lands, g_token = _late_gather_start([wire[n] for n in _BIG[1:]])
    full = {"conv_w": _blocks_to_full(True, conv_blocks)}
    for n in _SMALL:
        if n != "conv_w":
            full[n] = wt[n][0] if wt[n].ndim > 2 else wt[n].reshape(1, wt[n].shape[-1])

    def late_weights(after):
        blocks = _late_gather_wait(g_ssem, g_rsem, g_sent, g_lands, after)
        return {n: _blocks_to_full(n in _COL_SHARDED, b) for n, b in zip(_BIG[1:], blocks)}

    w = _layout_weights(full, w_in_blocks=w_in_blocks)
    loss_row, g, ctx = _forward_backward(x[0], p[0, 0], loss_target[0], w, late_weights, after=(g_token,))
    loss = lax.psum(loss_row[0, 0], ("x", "y", "c"))

    big = {n: _full_to_blocks(n in _COL_SHARDED, g[n]) for n in _BIG[1:]}
    big["w_in"] = _w_in_grad_blocks(g["w_main"], _lanes_to_heads(g["w_dt"]))
    g5 = {n: big[n].reshape(N_CHIPS, 2, big[n].shape[1] // 2, big[n].shape[2]) for n in _BIG}
    to_sibling = [lax.dynamic_index_in_dim(g5[n], 1 - ci, axis=1, keepdims=False).astype(WIRE_DTYPE) for n in _BIG]
    from_sibling = _swap_with_sibling(to_sibling)
    chip32, chip_wire = zip(*[_chip_sum("chip_sum_" + n, g5[n], r, c_arr) for n, r in zip(_BIG, from_sibling)])
    s_sem, r_sem, sent, lands, token = _scatter_blocks_start(list(chip_wire))
    grad_x, g["norm_g"] = _input_grad(x[0], w, ctx, after=(token,))
    from_chips = _scatter_blocks_wait(s_sem, r_sem, sent, lands, grad_x)
    halves = [_final_sum("final_sum_" + n, a, r, place_arr) for n, a, r in zip(_BIG, chip32, from_chips)]
    grads = dict(zip(_BIG, _share_halves(halves)))
    g = _natural_grads(g)

    pieces = [_two_d(n, g[n]).reshape(-1) for n in _SMALL]
    sizes = [v.shape[0] for v in pieces]
    padded = [-(-s // _PACK) * _PACK for s in sizes]
    packed = jnp.concatenate([jnp.pad(v, (0, ps - s)) for v, s, ps in zip(pieces, sizes, padded)]).reshape(-1, 128)
    summed = _allreduce_small(packed).reshape(-1)
    off = 0
    for n, s, ps in zip(_SMALL, sizes, padded):
        grads[n] = _two_d(n, g[n]).shape, summed[off:off + s]
        off += ps
    for n in _SMALL:
        shape, flat = grads[n]
        grads[n] = flat.reshape(shape)
    grads["conv_w"] = lax.dynamic_slice_in_dim(grads["conv_w"], me * (CONV_DIM // N_CHIPS), CONV_DIM // N_CHIPS, axis=1)

    delta, new_m, new_v = {}, {}, {}
    for n in _BIG:
        delta[n], new_m[n], new_v[n] = _adamw_call("adamw_" + n, shard[n], grads[n], mom[n][0], vel[n][0])
    small = _adamw_small([_two_d(n, wt[n]) for n in _SMALL], [grads[n] for n in _SMALL],
                         [_two_d(n, mom[n]) for n in _SMALL], [_two_d(n, vel[n]) for n in _SMALL])
    for i, n in enumerate(_SMALL):
        delta[n], new_m[n], new_v[n] = small[0][i], small[1][i], small[2][i]

    def shaped(d):
        return [d[n].reshape(wt[n].shape) for n in _WEIGHTS]

    return (loss, grad_x[None], *shaped(grads), *shaped(delta), *shaped(new_m), *shaped(new_v))
```

```python
import functools

import jax
import jax.numpy as jnp
from jax import lax
from jax.experimental import pallas as pl
from jax.experimental.pallas import tpu as pltpu

F32 = jnp.float32
MXU_DTYPE = jnp.bfloat16
ACT_DTYPE = jnp.bfloat16
WIRE_DTYPE = jnp.bfloat16

D_MODEL = 1024
PLE_DIM = 256
CHUNK = 128
EPS = 1e-6
E_A = D_MODEL
G_A = 4
D_INNER = 2 * D_MODEL
HEAD_DIM = 64
N_HEADS = D_INNER // HEAD_DIM
N_STATE = 128
N_GROUPS = 4
HEADS_PER_GROUP = N_HEADS // N_GROUPS
PAIRS_PER_GROUP = HEADS_PER_GROUP // 2
CONV_K = 4
CONV_DIM = D_INNER + 2 * N_GROUPS * N_STATE
N_IN = 3 * E_A + D_INNER + CONV_DIM + N_HEADS + 2 * D_MODEL
N_CHIPS = 4
N_DEV = 8
W_IN_BLOCK = N_IN // N_CHIPS

UVZ_W, XBC_W, ZB_W, G_W = 3 * E_A, CONV_DIM, D_INNER, 2 * D_MODEL
MAIN_W = UVZ_W + XBC_W + ZB_W + G_W
UVZ_CB, XBC_CB, ZB_CB, G_CB = 0, 1, 3, 4
DT_W = N_GROUPS * 128

ADAM_LR, ADAM_B1, ADAM_B2, ADAM_EPS, ADAM_WD, ADAM_STEP = 0.001, 0.9, 0.999, 1e-08, 0.01, 10

MESH = pl.DeviceIdType.MESH
ANY = pl.BlockSpec(memory_space=pl.ANY)


def _mxu(v):
    return v.astype(MXU_DTYPE)


def _dot(a, b, dims=(((1,), (0,)), ((), ()))):
    return lax.dot_general(_mxu(a), _mxu(b), dims, preferred_element_type=F32)


MM_TILE = 1024
MM_VMEM_BUDGET = 46 << 20


def _mm_tk(m, n, k, tm, tn, a_bytes, b_bytes, out_bytes, extra_bytes):
    for parts in range(1, k // 128 + 1):
        if k % parts or (k // parts) % 128 and parts > 1:
            continue
        tk = k // parts
        need = 2 * tk * (tm * a_bytes + tn * b_bytes) + 2 * tm * tn * (out_bytes + extra_bytes) + (tm * tn * 4 if parts > 1 else 0)
        if need <= MM_VMEM_BUDGET:
            return tk
    return 128


def _matmul(a, b, *, mode, name, out_dtype, m, n, k, tm=MM_TILE, tn=MM_TILE, tk=None, a_off=0, b_off=0,
            extras=(), epilogue=None, after=()):
    tm, tn = min(tm, m), min(tn, n)
    if tk is None:
        tk = _mm_tk(m, n, k, tm, tn, a.dtype.itemsize, b.dtype.itemsize, jnp.dtype(out_dtype).itemsize,
                    sum(e.dtype.itemsize for e in extras))
    tk = min(tk, k)
    assert m % tm == 0 and n % tn == 0 and k % tk == 0, (name, m, n, k, tm, tn, tk)
    nk = k // tk
    if mode == "nn":
        assert a_off % tk == 0 and b_off % tn == 0
        a_spec = pl.BlockSpec((tm, tk), lambda i, j, kk: (i, kk + a_off // tk))
        b_spec = pl.BlockSpec((tk, tn), lambda i, j, kk: (kk, j + b_off // tn))
        dims = (((1,), (0,)), ((), ()))
    elif mode == "nt":
        a_spec = pl.BlockSpec((tm, tk), lambda i, j, kk: (i, kk))
        b_spec = pl.BlockSpec((tn, tk), lambda i, j, kk: (j, kk))
        dims = (((1,), (1,)), ((), ()))
    else:
        assert a_off % tm == 0 and b_off % tn == 0
        a_spec = pl.BlockSpec((tk, tm), lambda i, j, kk: (kk, i + a_off // tm))
        b_spec = pl.BlockSpec((tk, tn), lambda i, j, kk: (kk, j + b_off // tn))
        dims = (((0,), (0,)), ((), ()))
    ne = len(extras)

    def finish(acc, extra_refs, o_ref):
        res = acc if epilogue is None else epilogue(acc, *[e[...] for e in extra_refs])
        o_ref[...] = res.astype(o_ref.dtype)

    def body(a_ref, b_ref, *rest):
        extra_refs, o_ref = rest[:ne], rest[ne + len(after)]
        part = _dot(a_ref[...], b_ref[...], dims)
        if nk == 1:
            finish(part, extra_refs, o_ref)
            return
        acc_ref = rest[ne + len(after) + 1]
        kk = pl.program_id(2)

        @pl.when(kk == 0)
        def _():
            acc_ref[...] = part

        @pl.when(kk > 0)
        def _():
            acc_ref[...] += part

        @pl.when(kk == nk - 1)
        def _():
            finish(acc_ref[...], extra_refs, o_ref)

    o_spec = pl.BlockSpec((tm, tn), lambda i, j, kk: (i, j))
    return pl.pallas_call(
        body, name=name, grid=(m // tm, n // tn, nk),
        in_specs=[a_spec, b_spec] + [o_spec] * ne + [ANY] * len(after), out_specs=o_spec,
        out_shape=jax.ShapeDtypeStruct((m, n), out_dtype),
        scratch_shapes=[pltpu.VMEM((tm, tn), F32)] if nk > 1 else [],
        compiler_params=pltpu.CompilerParams(dimension_semantics=("parallel", "parallel", "arbitrary")),
    )(a, b, *extras, *after)


def _rows_matmul(name, f, rows, pars, b, *, out_dtype, n, k, tm, nrows, tn=MM_TILE, extras=(), epilogue=None, after=()):
    tm, tn = min(tm, nrows), min(tn, n)
    assert nrows % tm == 0 and n % tn == 0, (name, nrows, n, tm, tn)
    nr, npar, ne, nj = len(rows), len(pars), len(extras), n // tn

    def body(*refs):
        row_refs, par_refs, b_ref = refs[:nr], refs[nr:nr + npar], refs[nr + npar]
        extra_refs = refs[nr + npar + 1:nr + npar + 1 + ne]
        a_ref, o_ref = refs[nr + npar + 1 + ne + len(after)], refs[nr + npar + 2 + ne + len(after)]

        def make_a():
            a = f(*[r[...].astype(F32) for r in row_refs], *[p[...] for p in par_refs])[0]
            a_ref[...] = a.astype(a_ref.dtype)

        if nj == 1:
            make_a()
        else:
            pl.when(pl.program_id(1) == 0)(make_a)
        res = _dot(a_ref[...], b_ref[...])
        if epilogue is not None:
            res = epilogue(res, *[e[...] for e in extra_refs])
        o_ref[...] = res.astype(o_ref.dtype)

    o_spec = pl.BlockSpec((tm, tn), lambda i, j: (i, j))
    return pl.pallas_call(
        body, name=name, grid=(nrows // tm, nj),
        in_specs=[pl.BlockSpec((tm, w), lambda i, j, cb=cb: (i, cb)) for _, w, cb in rows]
        + [pl.BlockSpec(tuple(p.shape), lambda i, j, nd=p.ndim: (0,) * nd) for p in pars]
        + [pl.BlockSpec((k, tn), lambda i, j: (0, j))] + [o_spec] * ne + [ANY] * len(after),
        out_specs=[pl.BlockSpec((tm, k), lambda i, j: (i, 0)), o_spec],
        out_shape=[jax.ShapeDtypeStruct((nrows, k), ACT_DTYPE), jax.ShapeDtypeStruct((nrows, n), out_dtype)],
        compiler_params=pltpu.CompilerParams(dimension_semantics=("parallel", "arbitrary")),
    )(*[r[0] for r in rows], *pars, b, *extras, *after)


def _row_spec(tm, width, cb):
    return pl.BlockSpec((tm, width), lambda i: (i, cb))


def _whole_spec(shape):
    nd = len(shape)
    return pl.BlockSpec(tuple(shape), lambda i: (0,) * nd)


def _rows_call(name, f, rows, pars, outs, *, tm, nrows):
    tm = min(tm, nrows)
    nr, npar = len(rows), len(pars)

    def body(*refs):
        rv = [r[...].astype(F32) for r in refs[:nr]]
        pv = [p[...] for p in refs[nr:nr + npar]]
        res = f(*rv, *pv)
        for o_ref, r in zip(refs[nr + npar:], res):
            o_ref[...] = r.astype(o_ref.dtype)

    return pl.pallas_call(
        body, name=name, grid=(nrows // tm,),
        in_specs=[_row_spec(tm, w, cb) for _, w, cb in rows] + [_whole_spec(p.shape) for p in pars],
        out_specs=[_row_spec(tm, w, 0) for w, _ in outs],
        out_shape=[jax.ShapeDtypeStruct((nrows, w), dt) for w, dt in outs],
        compiler_params=pltpu.CompilerParams(dimension_semantics=("parallel",)),
    )(*[r[0] for r in rows], *pars)


def _rows_vjp_call(name, f, rows, pars, cots, drows, *, tm, nrows, cot_mm=None):
    tm = min(tm, nrows)
    nr, npar, nc = len(rows), len(pars), len(cots)
    mm_args, mm_specs = [], []
    if cot_mm is not None:
        mm_a, mm_b, mm_add = cot_mm
        mm_args = [mm_a, mm_b] + ([] if mm_add is None else [mm_add])
        mm_specs = [_row_spec(tm, mm_a.shape[1], 0), _whole_spec(mm_b.shape)]
        mm_specs += [] if mm_add is None else [_row_spec(tm, mm_b.shape[0], 0)]
    alias_bufs, aliases = [], {}
    out_shape, out_specs = [], []
    for (arr, w, cb), d in zip(rows, drows):
        if d is None:
            continue
        dt, into = d
        if into is None:
            out_shape.append(jax.ShapeDtypeStruct((nrows, w), dt))
            out_specs.append(_row_spec(tm, w, 0))
        else:
            buf, total, ocb = into
            if buf is not None:
                aliases[nr + npar + nc + len(alias_bufs)] = len(out_shape)
                alias_bufs.append(buf)
            out_shape.append(jax.ShapeDtypeStruct((nrows, total), dt))
            out_specs.append(_row_spec(tm, w, ocb))
    n_drow = len(out_shape)
    for p in pars:
        out_shape.append(jax.ShapeDtypeStruct(p.shape, F32))
        out_specs.append(_whole_spec(p.shape))
    na = len(alias_bufs)

    def body(*refs):
        rv = [r[...].astype(F32) for r in refs[:nr]]
        pv = [p[...] for p in refs[nr:nr + npar]]
        cv = tuple(c[...].astype(F32) for c in refs[nr + npar:nr + npar + nc])
        o_refs = refs[nr + npar + nc + na + len(mm_args):]
        if mm_args:
            mm_refs = refs[nr + npar + nc + na:nr + npar + nc + na + len(mm_args)]
            c0 = _dot(mm_refs[0][...], mm_refs[1][...], (((1,), (1,)), ((), ())))
            if len(mm_refs) == 3:
                c0 = c0 + mm_refs[2][...].astype(F32)
            cv = (c0,) + cv
        _, vjp = jax.vjp(f, *rv, *pv)
        g = vjp(cv)
        oi = 0
        for ri, d in enumerate(drows):
            if d is not None:
                o_refs[oi][...] = g[ri].astype(o_refs[oi].dtype)
                oi += 1
        first = pl.program_id(0) == 0
        for pi in range(npar):
            acc = o_refs[n_drow + pi]

            @pl.when(first)
            def _(acc=acc):
                acc[...] = jnp.zeros_like(acc)

            acc[...] += g[nr + pi]

    return pl.pallas_call(
        body, name=name, grid=(nrows // tm,),
        in_specs=[_row_spec(tm, w, cb) for _, w, cb in rows] + [_whole_spec(p.shape) for p in pars]
        + [_row_spec(tm, w, cb) for _, w, cb in cots] + [ANY] * na + mm_specs,
        out_specs=out_specs, out_shape=out_shape, input_output_aliases=aliases,
        compiler_params=pltpu.CompilerParams(dimension_semantics=("arbitrary",)),
    )(*[r[0] for r in rows], *pars, *[c[0] for c in cots], *alias_bufs, *mm_args)


def _rms(x, g):
    return x * lax.rsqrt(jnp.mean(x * x, axis=-1, keepdims=True) + EPS) * g


def _f_rms(x, g):
    return (_rms(x, g),)


def _tril_mask():
    return lax.broadcasted_iota(jnp.int32, (CHUNK, CHUNK), 0) >= lax.broadcasted_iota(jnp.int32, (CHUNK, CHUNK), 1)


def _f_branch_a(uvz, ln_g, ln_b, w_s, b_s):
    u = jax.nn.gelu(uvz[:, :E_A])
    v = jax.nn.gelu(uvz[:, E_A:2 * E_A])
    z = uvz[:, 2 * E_A:]
    xc = v - jnp.mean(v, axis=-1, keepdims=True)
    vn = xc * lax.rsqrt(jnp.mean(xc * xc, axis=-1, keepdims=True) + EPS) * ln_g + ln_b
    mask = _tril_mask()
    ws = [jnp.where(mask, w_s[g], 0.0) for g in range(G_A)]
    gw = E_A // G_A
    rows = []
    for c in range(uvz.shape[0] // CHUNK):
        vc = vn[c * CHUNK:(c + 1) * CHUNK]
        rows.append(jnp.concatenate([_dot(ws[g], vc[:, g * gw:(g + 1) * gw]) + b_s[g] for g in range(G_A)], axis=1))
    sv = rows[0] if len(rows) == 1 else jnp.concatenate(rows, axis=0)
    return (u * sv * jax.nn.silu(z),)


def _f_gnorm(y, zb, g):
    yz = y * jax.nn.silu(zb)
    gw = D_INNER // N_GROUPS
    parts = []
    for i in range(N_GROUPS):
        s = yz[:, i * gw:(i + 1) * gw]
        parts.append(s * lax.rsqrt(jnp.mean(s * s, axis=-1, keepdims=True) + EPS))
    return (jnp.concatenate(parts, axis=1) * g,)


def _f_merge(g2, oa, ob):
    return (jax.nn.sigmoid(g2[:, :D_MODEL]) * oa + jax.nn.sigmoid(g2[:, D_MODEL:]) * ob,)


def _f_loss(x1, gp, pe, tgt, fg):
    x2 = x1 + jax.nn.sigmoid(gp) * pe
    err = _rms(x2, fg) - tgt
    return 0.5 * jnp.sum(jnp.mean(err * err, axis=-1))


def _head(x1, p, tgt, ple_g, w_pg, w_ple, fg, *, tm, nrows):
    tm = min(tm, nrows)

    def body(x1_ref, p_ref, t_ref, pg_ref, wpg_ref, wple_ref, fg_ref, hp_ref, dx_ref, dgp_ref, dpe_ref, dfg_ref, loss_ref):
        x1 = x1_ref[...]
        hp_ref[...] = _rms(x1, pg_ref[...]).astype(hp_ref.dtype)
        gp = _dot(hp_ref[...], wpg_ref[...])
        pe = _dot(p_ref[...], wple_ref[...])
        loss, vjp = jax.vjp(_f_loss, x1, gp, pe, t_ref[...], fg_ref[...])
        dx, dgp, dpe, _, dfg = vjp(jnp.ones((), F32))
        dx_ref[...] = dx
        dgp_ref[...] = dgp.astype(dgp_ref.dtype)
        dpe_ref[...] = dpe.astype(dpe_ref.dtype)

        @pl.when(pl.program_id(0) == 0)
        def _():
            dfg_ref[...] = jnp.zeros_like(dfg_ref)
            loss_ref[...] = jnp.zeros_like(loss_ref)

        dfg_ref[...] += dfg
        loss_ref[...] += jnp.full(loss_ref.shape, loss, F32)

    row = _row_spec(tm, D_MODEL, 0)
    act = jax.ShapeDtypeStruct((nrows, D_MODEL), ACT_DTYPE)
    return pl.pallas_call(
        body, name="head", grid=(nrows // tm,),
        in_specs=[row, _row_spec(tm, PLE_DIM, 0), row, _whole_spec((1, D_MODEL)), _whole_spec(w_pg.shape),
                  _whole_spec(w_ple.shape), _whole_spec((1, D_MODEL))],
        out_specs=[row, row, row, row, _whole_spec((1, D_MODEL)), _whole_spec((1, 128))],
        out_shape=[act, jax.ShapeDtypeStruct((nrows, D_MODEL), F32), act, act, jax.ShapeDtypeStruct((1, D_MODEL), F32),
                   jax.ShapeDtypeStruct((1, 128), F32)],
        compiler_params=pltpu.CompilerParams(dimension_semantics=("arbitrary",)),
    )(x1, p, tgt, ple_g, w_pg, w_ple, fg)


def _shift_rows(cur, edge, j, up):
    tm = cur.shape[0]
    row = lax.broadcasted_iota(jnp.int32, cur.shape, 0)
    if up:
        sh = pltpu.roll(cur, tm - j, 0)
        e = jnp.tile(pltpu.roll(edge, 8 - j, 0), (tm // 8, 1))
        return jnp.where(row >= tm - j, e, sh)
    sh = pltpu.roll(cur, j, 0)
    e = jnp.tile(pltpu.roll(edge, j, 0), (tm // 8, 1))
    return jnp.where(row < j, e, sh)


def _conv_pre(cur, prev, w, b):
    acc = cur * w[CONV_K - 1:CONV_K] + b
    taps = [cur]
    for j in range(1, CONV_K):
        s = _shift_rows(cur, prev, j, up=False)
        taps.append(s)
        acc = acc + s * w[CONV_K - 1 - j:CONV_K - j]
    return acc, taps


def _halo_specs(tm, nrows, cb, before):
    nb = tm // 8
    last = nrows // 8 - 1
    if before:
        return pl.BlockSpec((8, XBC_W), lambda i: (jnp.maximum(i * nb - 1, 0), cb))
    return pl.BlockSpec((8, XBC_W), lambda i: (jnp.minimum((i + 1) * nb, last), cb))


def _conv_fwd(proj, conv_w, conv_b, *, tm, nrows):
    tm = min(tm, nrows)

    def body(cur_ref, prev_ref, w_ref, b_ref, o_ref):
        prev = jnp.where(pl.program_id(0) == 0, 0.0, prev_ref[...].astype(F32))
        pre, _ = _conv_pre(cur_ref[...].astype(F32), prev, w_ref[...], b_ref[...])
        o_ref[...] = jax.nn.silu(pre).astype(o_ref.dtype)

    return pl.pallas_call(
        body, name="conv_fwd", grid=(nrows // tm,),
        in_specs=[_row_spec(tm, XBC_W, XBC_CB), _halo_specs(tm, nrows, XBC_CB, True),
                  _whole_spec((CONV_K, XBC_W)), _whole_spec((1, XBC_W))],
        out_specs=_row_spec(tm, XBC_W, 0), out_shape=jax.ShapeDtypeStruct((nrows, XBC_W), ACT_DTYPE),
        compiler_params=pltpu.CompilerParams(dimension_semantics=("parallel",)),
    )(proj, proj, conv_w, conv_b)


def _conv_bwd_pre(proj, conv_w, conv_b, dact, *, tm, nrows):
    tm = min(tm, nrows)
    nb = N_GROUPS * N_STATE

    def body(cur_ref, prev_ref, w_ref, b_ref, dxs_ref, dbm_ref, dcm_ref, dpre_ref, dw_ref, db_ref):
        prev = jnp.where(pl.program_id(0) == 0, 0.0, prev_ref[...].astype(F32))
        pre, taps = _conv_pre(cur_ref[...].astype(F32), prev, w_ref[...], b_ref[...])
        sg = jax.nn.sigmoid(pre)
        dy = jnp.concatenate([dxs_ref[...], dbm_ref[...], dcm_ref[...]], axis=1).astype(F32)
        dpre = dy * sg * (1.0 + pre * (1.0 - sg))
        dpre_ref[...] = dpre.astype(dpre_ref.dtype)

        @pl.when(pl.program_id(0) == 0)
        def _():
            dw_ref[...] = jnp.zeros_like(dw_ref)
            db_ref[...] = jnp.zeros_like(db_ref)

        db_ref[...] += jnp.sum(dpre, axis=0, keepdims=True)
        for j in range(CONV_K):
            k = CONV_K - 1 - j
            dw_ref[k:k + 1, :] += jnp.sum(dpre * taps[j], axis=0, keepdims=True)

    return pl.pallas_call(
        body, name="conv_bwd_pre", grid=(nrows // tm,),
        in_specs=[_row_spec(tm, XBC_W, XBC_CB), _halo_specs(tm, nrows, XBC_CB, True),
                  _whole_spec((CONV_K, XBC_W)), _whole_spec((1, XBC_W)),
                  _row_spec(tm, D_INNER, 0), _row_spec(tm, nb, 0), _row_spec(tm, nb, 0)],
        out_specs=[_row_spec(tm, XBC_W, 0), _whole_spec((CONV_K, XBC_W)), _whole_spec((1, XBC_W))],
        out_shape=[jax.ShapeDtypeStruct((nrows, XBC_W), ACT_DTYPE), jax.ShapeDtypeStruct((CONV_K, XBC_W), F32),
                   jax.ShapeDtypeStruct((1, XBC_W), F32)],
        compiler_params=pltpu.CompilerParams(dimension_semantics=("arbitrary",)),
    )(proj, proj, conv_w, conv_b, *dact)


def _conv_bwd_x(dpre, conv_w, dproj, *, tm, nrows):
    tm = min(tm, nrows)
    ntiles = nrows // tm

    def body(cur_ref, nxt_ref, w_ref, _, o_ref):
        cur = cur_ref[...].astype(F32)
        nxt = jnp.where(pl.program_id(0) == ntiles - 1, 0.0, nxt_ref[...].astype(F32))
        w = w_ref[...]
        acc = cur * w[CONV_K - 1:CONV_K]
        for j in range(1, CONV_K):
            acc = acc + _shift_rows(cur, nxt, j, up=True) * w[CONV_K - 1 - j:CONV_K - j]
        o_ref[...] = acc.astype(o_ref.dtype)

    return pl.pallas_call(
        body, name="conv_bwd_x", grid=(ntiles,),
        in_specs=[_row_spec(tm, XBC_W, 0), _halo_specs(tm, nrows, 0, False), _whole_spec((CONV_K, XBC_W)), ANY],
        out_specs=_row_spec(tm, XBC_W, XBC_CB), out_shape=jax.ShapeDtypeStruct(dproj.shape, dproj.dtype),
        input_output_aliases={3: 0},
        compiler_params=pltpu.CompilerParams(dimension_semantics=("parallel",)),
    )(dpre, dpre, conv_w, dproj)


SSD_SPAN = 4
_XS_GW = D_INNER // N_GROUPS
_NT = (((1,), (1,)), ((), ()))
_TN = (((0,), (0,)), ((), ()))


def _bf16_terms(x, terms):
    parts, rest = [], x
    for _ in range(terms):
        part = rest.astype(jnp.bfloat16)
        parts.append(part)
        rest = rest - part.astype(F32)
    return parts


def _head_lane_matrix():
    return (lax.broadcasted_iota(jnp.int32, (128, _XS_GW), 0)
            == lax.broadcasted_iota(jnp.int32, (128, _XS_GW), 1) // HEAD_DIM).astype(jnp.bfloat16)


@functools.partial(jax.custom_vjp, nondiff_argnums=(1,))
def _head_lanes(cols, terms):
    e = _head_lane_matrix()
    return sum(jnp.dot(t, e, preferred_element_type=F32) for t in _bf16_terms(cols, terms))


def _head_lanes_fwd(cols, terms):
    return _head_lanes(cols, terms), None


def _head_lanes_bwd(terms, _, g):
    e = _head_lane_matrix()
    return (sum(lax.dot_general(t, e, _NT, preferred_element_type=F32) for t in _bf16_terms(g, 2)),)


_head_lanes.defvjp(_head_lanes_fwd, _head_lanes_bwd)


def _ssd_chunk(k, xs, bm, cm, dtr, hprev, dtb, alog, dsk):
    causal, tri, lo = k
    dt = jax.nn.softplus(dtr + dtb)
    da = dt * (-jnp.exp(alog))
    cs = jnp.dot(tri, da, precision=lax.Precision.HIGHEST, preferred_element_type=F32)
    cst = cs.T
    cs_l = _head_lanes(cs, 3)
    xdt = xs * _head_lanes(dt, 2)
    cb = _dot(cm, bm, _NT)
    yd = []
    for q in range(PAIRS_PER_GROUP):
        xq = xdt[:, 128 * q:128 * (q + 1)]
        y2 = [_dot(cb * jnp.exp(jnp.where(causal, cs[:, h:h + 1] - cst[h:h + 1, :], -jnp.inf)), xq)
              for h in (2 * q, 2 * q + 1)]
        yd.append(jnp.where(lo, y2[0], y2[1]))
    y_off = jnp.exp(cs_l) * _dot(cm, hprev, _NT)
    st = _dot(xdt * jnp.exp(cs_l[CHUNK - 1:CHUNK, :] - cs_l), bm, _TN)
    cdec = jnp.exp(cs[CHUNK - 1:CHUNK, :])
    cd_rows = jnp.concatenate(
        [jnp.broadcast_to(cdec[:, h:h + 1], (HEAD_DIM, N_STATE)) for h in range(HEADS_PER_GROUP)], axis=0)
    dsk_l = _head_lanes(jnp.broadcast_to(dsk, (8, 128)), 2)[:1]
    y = jnp.concatenate(yd, axis=1) + y_off + xs * dsk_l
    return y, cd_rows * hprev + st


def _ssd_span(xs, bm, cm, dtr, h0, dtb, alog, dsk):
    li = lax.broadcasted_iota(jnp.int32, (CHUNK, CHUNK), 0)
    si = lax.broadcasted_iota(jnp.int32, (CHUNK, CHUNK), 1)
    causal = li >= si
    k = (causal, causal.astype(F32), si < HEAD_DIM)
    h, ys = h0, []
    for t in range(xs.shape[0] // CHUNK):
        r = slice(t * CHUNK, (t + 1) * CHUNK)
        y, h = _ssd_chunk(k, xs[r], bm[r], cm[r], dtr[r], h, dtb, alog, dsk)
        ys.append(y)
    return (ys[0] if len(ys) == 1 else jnp.concatenate(ys, axis=0)), h


def _ssd_specs(rev, nsteps, rows):
    def s_of(s):
        return nsteps - 1 - s if rev else s

    xs = pl.BlockSpec((rows, _XS_GW), lambda g, s: (s_of(s), g))
    bm = pl.BlockSpec((rows, N_STATE), lambda g, s: (s_of(s), D_INNER // N_STATE + g))
    cm = pl.BlockSpec((rows, N_STATE), lambda g, s: (s_of(s), D_INNER // N_STATE + N_GROUPS + g))
    dt = pl.BlockSpec((rows, 128), lambda g, s: (s_of(s), g))
    par = pl.BlockSpec((1, 128), lambda g, s: (0, g))
    st = pl.BlockSpec((None, None, _XS_GW, N_STATE), lambda g, s: (g, s_of(s), 0, 0))
    return xs, bm, cm, dt, par, st


def _ssd_fwd(act, dtr, dtb, alog, dsk, *, nrows):
    rows = CHUNK * min(SSD_SPAN, nrows // CHUNK)
    nsteps = nrows // rows
    xs, bm, cm, dt, par, st = _ssd_specs(False, nsteps, rows)

    def body(xs_ref, b_ref, c_ref, dt_ref, dtb_ref, al_ref, dk_ref, y_ref, st_ref, h_ref):
        @pl.when(pl.program_id(1) == 0)
        def _():
            h_ref[...] = jnp.zeros_like(h_ref)

        h0 = h_ref[...]
        st_ref[...] = h0
        y, hnew = _ssd_span(xs_ref[...].astype(F32), b_ref[...].astype(F32), c_ref[...].astype(F32), dt_ref[...],
                            h0, dtb_ref[...], al_ref[...], dk_ref[...])
        y_ref[...] = y.astype(y_ref.dtype)
        h_ref[...] = hnew

    return pl.pallas_call(
        body, name="ssd_fwd", grid=(N_GROUPS, nsteps),
        in_specs=[xs, bm, cm, dt, par, par, par], out_specs=[xs, st],
        out_shape=[jax.ShapeDtypeStruct((nrows, D_INNER), ACT_DTYPE),
                   jax.ShapeDtypeStruct((N_GROUPS, nsteps, _XS_GW, N_STATE), F32)],
        scratch_shapes=[pltpu.VMEM((_XS_GW, N_STATE), F32)],
        compiler_params=pltpu.CompilerParams(dimension_semantics=("arbitrary", "arbitrary")),
    )(act, act, act, dtr, dtb, alog, dsk)


def _ssd_bwd(act, dtr, dtb, alog, dsk, states, dy, *, nrows):
    rows = CHUNK * min(SSD_SPAN, nrows // CHUNK)
    nsteps = nrows // rows
    xs, bm, cm, dt, par, st = _ssd_specs(True, nsteps, rows)

    def body(xs_ref, b_ref, c_ref, dt_ref, dtb_ref, al_ref, dk_ref, st_ref, dy_ref,
             dxs_ref, db_ref, dc_ref, ddt_ref, ddtb_ref, dal_ref, ddk_ref, dh_ref):
        @pl.when(pl.program_id(1) == 0)
        def _():
            dh_ref[...] = jnp.zeros_like(dh_ref)
            ddtb_ref[...] = jnp.zeros_like(ddtb_ref)
            dal_ref[...] = jnp.zeros_like(dal_ref)
            ddk_ref[...] = jnp.zeros_like(ddk_ref)

        _, vjp = jax.vjp(_ssd_span, xs_ref[...].astype(F32), b_ref[...].astype(F32), c_ref[...].astype(F32),
                         dt_ref[...], st_ref[...], dtb_ref[...], al_ref[...], dk_ref[...])
        dxs, db, dc, ddt, dh, ddtb, dal, ddk = vjp((dy_ref[...].astype(F32), dh_ref[...]))
        dxs_ref[...] = dxs.astype(dxs_ref.dtype)
        db_ref[...] = db.astype(db_ref.dtype)
        dc_ref[...] = dc.astype(dc_ref.dtype)
        ddt_ref[...] = ddt
        dh_ref[...] = dh
        ddtb_ref[...] += ddtb
        dal_ref[...] += dal
        ddk_ref[...] += ddk

    nb = N_GROUPS * N_STATE
    bspec = pl.BlockSpec((rows, N_STATE), lambda g, s: (nsteps - 1 - s, g))
    return pl.pallas_call(
        body, name="ssd_bwd", grid=(N_GROUPS, nsteps),
        in_specs=[xs, bm, cm, dt, par, par, par, st, xs],
        out_specs=[xs, bspec, bspec, dt, par, par, par],
        out_shape=[jax.ShapeDtypeStruct((nrows, D_INNER), ACT_DTYPE), jax.ShapeDtypeStruct((nrows, nb), ACT_DTYPE),
                   jax.ShapeDtypeStruct((nrows, nb), ACT_DTYPE), jax.ShapeDtypeStruct((nrows, DT_W), F32),
                   jax.ShapeDtypeStruct((1, DT_W), F32), jax.ShapeDtypeStruct((1, DT_W), F32),
                   jax.ShapeDtypeStruct((1, DT_W), F32)],
        scratch_shapes=[pltpu.VMEM((_XS_GW, N_STATE), F32)],
        compiler_params=pltpu.CompilerParams(dimension_semantics=("arbitrary", "arbitrary")),
    )(act, act, act, dtr, dtb, alog, dsk, states, dy)


def _add_epilogue(acc, r):
    return r + acc


def _rms_and_skip(x, g):
    return _rms(x, g), x


def _forward_backward(x, p, tgt, w, late_weights=None, after=()):
    s = x.shape[0]
    act_t, f32 = ACT_DTYPE, F32
    mm = functools.partial(_matmul)
    h, proj = _rows_matmul("proj", _f_rms, [(x, D_MODEL, 0)], [w["norm_g"]], w["w_main"], out_dtype=act_t,
                           n=MAIN_W, k=D_MODEL, tm=1024, nrows=s, after=after)
    dtr = mm(h, w["w_dt"], mode="nn", name="proj_dt", out_dtype=f32, m=s, n=DT_W, k=D_MODEL)
    act = _conv_fwd(proj, w["conv_w"], w["conv_b"], tm=512, nrows=s)
    y, states = _ssd_fwd(act, dtr, w["dt_bias"], w["a_log"], w["d_skip"], nrows=s)
    if late_weights is not None:
        w = {**w, **late_weights(states)}
    a_pars = [w["ln_a_g"], w["ln_a_b"], w["w_s"], w["b_s"]]
    y_a, o_a = _rows_matmul("out_a", _f_branch_a, [(proj, UVZ_W, UVZ_CB)], a_pars, w["w_oa"], out_dtype=act_t,
                            n=D_MODEL, k=E_A, tm=256, nrows=s)
    gn_rows = [(y, D_INNER, 0), (proj, ZB_W, ZB_CB)]
    y_b, o_b = _rows_matmul("out_b", _f_gnorm, gn_rows, [w["ssm_norm_g"]], w["w_ob"], out_dtype=act_t,
                            n=D_MODEL, k=D_INNER, tm=512, nrows=s)
    mg_rows = [(proj, G_W, G_CB), (o_a, D_MODEL, 0), (o_b, D_MODEL, 0)]
    merged, x1 = _rows_matmul("out_proj", _f_merge, mg_rows, [], w["w_out"], out_dtype=f32, n=D_MODEL, k=D_MODEL,
                              tm=512, nrows=s, extras=(x,), epilogue=_add_epilogue)
    g = {}
    hp, dx2, dgp, dpe, g["final_g"], loss = _head(x1, p, tgt, w["ple_norm_g"], w["w_pg"], w["w_ple"], w["final_g"],
                                                   tm=256, nrows=s)
    g["w_pg"] = mm(hp, dgp, mode="tn", name="d_w_pg", out_dtype=f32, m=D_MODEL, n=D_MODEL, k=s)
    g["w_ple"] = mm(p, dpe, mode="tn", name="d_w_ple", out_dtype=f32, m=PLE_DIM, n=D_MODEL, k=s)
    dx1, g["ple_norm_g"] = _rows_vjp_call(
        "ple_norm_bwd", _rms_and_skip, [(x1, D_MODEL, 0)], [w["ple_norm_g"]], [(dx2, D_MODEL, 0)],
        [(f32, None)], tm=512, nrows=s, cot_mm=(dgp, w["w_pg"], None))
    g["w_out"] = mm(merged, dx1, mode="tn", name="d_w_out", out_dtype=f32, m=D_MODEL, n=D_MODEL, k=s)
    dproj, do_a, do_b = _rows_vjp_call(
        "merge_bwd", _f_merge, mg_rows, [], [],
        [(act_t, (None, MAIN_W, G_CB)), (act_t, None), (act_t, None)], tm=512, nrows=s, cot_mm=(dx1, w["w_out"], None))
    g["w_oa"] = mm(y_a, do_a, mode="tn", name="d_w_oa", out_dtype=f32, m=E_A, n=D_MODEL, k=s)
    g["w_ob"] = mm(y_b, do_b, mode="tn", name="d_w_ob", out_dtype=f32, m=D_INNER, n=D_MODEL, k=s)
    dy, dproj, g["ssm_norm_g"] = _rows_vjp_call(
        "gnorm_bwd", _f_gnorm, gn_rows, [w["ssm_norm_g"]], [],
        [(act_t, None), (act_t, (dproj, MAIN_W, ZB_CB))], tm=256, nrows=s, cot_mm=(do_b, w["w_ob"], None))
    dxs, dbm, dcm, ddtr, g["dt_bias"], g["a_log"], g["d_skip"] = _ssd_bwd(
        act, dtr, w["dt_bias"], w["a_log"], w["d_skip"], states, dy, nrows=s)
    dpre, g["conv_w"], g["conv_b"] = _conv_bwd_pre(proj, w["conv_w"], w["conv_b"], (dxs, dbm, dcm), tm=512, nrows=s)
    dproj = _conv_bwd_x(dpre, w["conv_w"], dproj, tm=512, nrows=s)
    dproj, g["ln_a_g"], g["ln_a_b"], g["w_s"], g["b_s"] = _rows_vjp_call(
        "branch_a_bwd", _f_branch_a, [(proj, UVZ_W, UVZ_CB)], a_pars, [],
        [(act_t, (dproj, MAIN_W, UVZ_CB))], tm=256, nrows=s, cot_mm=(do_a, w["w_oa"], None))
    g["w_main"] = mm(h, dproj, mode="tn", name="d_w_main", out_dtype=f32, m=D_MODEL, n=MAIN_W, k=s)
    g["w_dt"] = mm(h, ddtr, mode="tn", name="d_w_dt", out_dtype=f32, m=D_MODEL, n=DT_W, k=s)
    return loss, g, (dproj, ddtr, dx1)


def _input_grad(x, w, ctx, after=()):
    dproj, ddtr, dx1 = ctx
    s = x.shape[0]
    dh = _matmul(dproj, w["w_main"], mode="nt", name="d_h_main", out_dtype=F32, m=s, n=D_MODEL, k=MAIN_W, after=after)
    return _rows_vjp_call(
        "pre_norm_bwd", _rms_and_skip, [(x, D_MODEL, 0)], [w["norm_g"]], [(dx1, D_MODEL, 0)],
        [(F32, None)], tm=512, nrows=s, cot_mm=(ddtr, w["w_dt"], dh))


def _local_step(x, p, tgt, w):
    loss, g, ctx = _forward_backward(x, p, tgt, w)
    grad_x, g["norm_g"] = _input_grad(x, w, ctx)
    return loss, grad_x, g


_O_ZB = 3 * E_A
_O_XBC = _O_ZB + D_INNER
_O_DT = _O_XBC + CONV_DIM
_O_G = _O_DT + N_HEADS


def _heads_to_lanes(v):
    r = v.shape[0]
    v = v.reshape(r, N_GROUPS, HEADS_PER_GROUP)
    return jnp.pad(v, ((0, 0), (0, 0), (0, 128 - HEADS_PER_GROUP))).reshape(r, DT_W)


def _lanes_to_heads(v):
    r = v.shape[0]
    return v.reshape(r, N_GROUPS, 128)[:, :, :HEADS_PER_GROUP].reshape(r, N_HEADS)


def _block_cols(blocks, a, b):
    parts = []
    for k in range(N_CHIPS):
        lo, hi = max(a, k * W_IN_BLOCK), min(b, (k + 1) * W_IN_BLOCK)
        if lo < hi:
            parts.append(blocks[k][:, lo - k * W_IN_BLOCK:hi - k * W_IN_BLOCK])
    return parts


_W_IN_SEGMENTS = ((0, _O_ZB, "m", 0), (_O_ZB, _O_XBC, "m", UVZ_W + XBC_W), (_O_XBC, _O_DT, "m", UVZ_W),
                  (_O_DT, _O_G, "d", 0), (_O_G, N_IN, "m", MAIN_W - G_W))


def _w_in_grad_blocks(gm, gdt):
    blocks = []
    for k in range(N_CHIPS):
        a, b = k * W_IN_BLOCK, (k + 1) * W_IN_BLOCK
        parts = []
        for s, e, src, off in _W_IN_SEGMENTS:
            lo, hi = max(a, s), min(b, e)
            if lo < hi:
                parts.append((gm if src == "m" else gdt)[:, off + lo - s:off + hi - s])
        blocks.append(jnp.concatenate(parts, axis=1))
    return jnp.stack(blocks)


def _layout_weights(f, w_in_blocks=None):
    w = dict(f)
    if w_in_blocks is None:
        w_in = w.pop("w_in")
        w_in_blocks = jnp.stack([w_in[:, k * W_IN_BLOCK:(k + 1) * W_IN_BLOCK] for k in range(N_CHIPS)])
    cols = functools.partial(_block_cols, w_in_blocks)
    w["w_main"] = jnp.concatenate(cols(0, _O_ZB) + cols(_O_XBC, _O_DT) + cols(_O_ZB, _O_XBC) + cols(_O_G, N_IN), axis=1)
    w["w_dt"] = _heads_to_lanes(jnp.concatenate(cols(_O_DT, _O_G), axis=1))
    w["b_s"] = f["b_s"].reshape(G_A, CHUNK, 1)
    for n in ("dt_bias", "a_log", "d_skip"):
        w[n] = _heads_to_lanes(f[n])
    return w


def _natural_grads(g):
    out = dict(g)
    gm = out.pop("w_main")
    gdt = _lanes_to_heads(out.pop("w_dt"))
    out["w_in"] = jnp.concatenate(
        [gm[:, :UVZ_W], gm[:, UVZ_W + XBC_W:UVZ_W + XBC_W + ZB_W], gm[:, UVZ_W:UVZ_W + XBC_W], gdt, gm[:, MAIN_W - G_W:]],
        axis=1)
    out["b_s"] = g["b_s"].reshape(G_A, CHUNK)
    for n in ("dt_bias", "a_log", "d_skip"):
        out[n] = _lanes_to_heads(g[n])
    return out


def _place():
    return lax.axis_index("x"), lax.axis_index("y"), lax.axis_index("c")


def _other_chips(x, y):
    return [(1 - x, y), (x, 1 - y), (1 - x, 1 - y)]


def _rcopy(src, dst, ssem, rsem, dev):
    return pltpu.make_async_remote_copy(src_ref=src, dst_ref=dst, send_sem=ssem, recv_sem=rsem,
                                        device_id=dev, device_id_type=MESH)


def _half(ref_rows, half):
    hs = ref_rows // 2
    return pl.ds(pl.multiple_of(half * hs, 16), hs)


def _gather_weights(shards, conv_shard):
    nw = len(shards)

    def body(*refs):
        sh, cv = refs[:nw], refs[nw]
        out, cvo = refs[nw + 1:2 * nw + 1], refs[2 * nw + 1]
        ici_s, ici_r, fw_s, fw_r, own_s, own_r, cv_s, cv_r = refs[2 * nw + 2:]
        x, y, c = _place()
        me, sib, chips = 2 * x + y, (x, y, 1 - c), _other_chips(x, y)
        own = [_rcopy(sh[w], out[w].at[me], own_s.at[w], own_r.at[w], sib) for w in range(nw)]
        own.append(_rcopy(cv, cvo.at[me], own_s.at[nw], own_r.at[nw], sib))
        for cp in own:
            cp.start()
        sends = []
        for w in range(nw):
            mine = _half(sh[w].shape[0], c)
            for j, chip in enumerate(chips):
                sends.append(_rcopy(sh[w].at[mine], out[w].at[me, mine], ici_s.at[3 * w + j], ici_r.at[3 * w + j], (*chip, c)))
        for j, chip in enumerate(chips):
            sends.append(_rcopy(cv, cvo.at[me], cv_s.at[j], cv_r.at[j], (*chip, c)))
        for cp in sends:
            cp.start()
        for w in range(nw):
            mine = _half(sh[w].shape[0], c)
            for j, chip in enumerate(chips):
                slab = out[w].at[2 * chip[0] + chip[1], mine]
                _rcopy(slab, slab, ici_s.at[3 * w + j], ici_r.at[3 * w + j], (*chip, c)).wait_recv()
                fwd = _rcopy(slab, slab, fw_s.at[3 * w + j], fw_r.at[3 * w + j], sib)
                fwd.start()
                sends.append(fwd)
        for j, chip in enumerate(chips):
            blk = cvo.at[2 * chip[0] + chip[1]]
            _rcopy(blk, blk, cv_s.at[j], cv_r.at[j], (*chip, c)).wait_recv()
        for w in range(nw):
            theirs = _half(sh[w].shape[0], 1 - c)
            for j, chip in enumerate(chips):
                slab = out[w].at[2 * chip[0] + chip[1], theirs]
                _rcopy(slab, slab, fw_s.at[3 * w + j], fw_r.at[3 * w + j], sib).wait_recv()
        for cp in sends:
            cp.wait_send()
        for cp in own:
            cp.wait()

    dma = pltpu.SemaphoreType.DMA
    return pl.pallas_call(
        body, name="gather_weights",
        in_specs=[ANY] * (nw + 1), out_specs=[ANY] * (nw + 1),
        out_shape=[jax.ShapeDtypeStruct((N_CHIPS,) + s.shape, s.dtype) for s in shards]
        + [jax.ShapeDtypeStruct((N_CHIPS,) + conv_shard.shape, conv_shard.dtype)],
        scratch_shapes=[dma((3 * nw,)), dma((3 * nw,)), dma((3 * nw,)), dma((3 * nw,)), dma((nw + 1,)), dma((nw + 1,)),
                        dma((3,)), dma((3,))],
    )(*shards, conv_shard)


_HBM = pl.BlockSpec(memory_space=pltpu.HBM)
_SEM = pl.BlockSpec(memory_space=pltpu.SEMAPHORE)
_EFFECT = pltpu.SideEffectType.DATAFLOW_SIDE_EFFECTING


def _late_gather_copies(sh, out, s_sem, r_sem):
    x, y, c = _place()
    to = [(*chip, c) for chip in _other_chips(x, y)] + [(x, y, 1 - c)]
    return [_rcopy(sh[w], out[w].at[2 * x + y], s_sem.at[4 * w + j], r_sem.at[4 * w + j], dev)
            for w in range(len(sh)) for j, dev in enumerate(to)]


def _late_gather_start(shards):
    n = len(shards)
    lands = [lax.empty((N_CHIPS,) + a.shape, a.dtype) for a in shards]

    def body(*refs):
        for cp in _late_gather_copies(refs[:n], refs[n:2 * n], refs[2 * n], refs[2 * n + 1]):
            cp.start()
        refs[-1][...] = jnp.zeros_like(refs[-1])

    dma = pltpu.SemaphoreType.DMA
    hbm = [pltpu.with_memory_space_constraint(a, pltpu.HBM) for a in list(shards) + lands]
    out = pl.pallas_call(
        body, name="late_gather_start",
        out_shape=[dma((4 * n,)), dma((4 * n,))] + [pltpu.HBM(a.shape, a.dtype) for a in hbm]
        + [jax.ShapeDtypeStruct((8, 128), F32)],
        in_specs=[_HBM] * (2 * n), out_specs=[_SEM, _SEM] + [_HBM] * (2 * n) + [pl.BlockSpec(memory_space=pltpu.VMEM)],
        input_output_aliases={i: 2 + i for i in range(2 * n)},
        compiler_params=pltpu.CompilerParams(has_side_effects=_EFFECT),
    )(*hbm)
    return out[0], out[1], out[2:2 + n], out[2 + n:2 + 2 * n], out[-1]


def _late_gather_wait(s_sem, r_sem, srcs, lands, after):
    n = len(srcs)

    def body(*refs):
        for cp in _late_gather_copies(refs[:n], refs[n:2 * n], refs[2 * n], refs[2 * n + 1]):
            cp.wait_send()
            cp.wait_recv()

    out = pl.pallas_call(
        body, name="late_gather_wait",
        out_shape=[pltpu.HBM(a.shape, a.dtype) for a in list(srcs) + list(lands)],
        in_specs=[_HBM] * (2 * n) + [_SEM, _SEM, ANY], out_specs=[_HBM] * (2 * n),
        input_output_aliases={i: i for i in range(2 * n)},
        compiler_params=pltpu.CompilerParams(has_side_effects=_EFFECT),
    )(*srcs, *lands, s_sem, r_sem, after)
    return out[n:]


def _swap_with_sibling(arrs):
    n = len(arrs)

    def body(*refs):
        src, dst, s_sem, r_sem = refs[:n], refs[n:2 * n], refs[2 * n], refs[2 * n + 1]
        x, y, c = _place()
        cps = [_rcopy(src[i], dst[i], s_sem.at[i], r_sem.at[i], (x, y, 1 - c)) for i in range(n)]
        for cp in cps:
            cp.start()
        for cp in cps:
            cp.wait()

    dma = pltpu.SemaphoreType.DMA
    return pl.pallas_call(
        body, name="swap_with_sibling", in_specs=[ANY] * n, out_specs=[ANY] * n,
        out_shape=[jax.ShapeDtypeStruct(a.shape, a.dtype) for a in arrs], scratch_shapes=[dma((n,)), dma((n,))],
    )(*arrs)


def _scatter_copies(src, land, s_sem, r_sem):
    x, y, c = _place()
    return [_rcopy(src[i].at[2 * chip[0] + chip[1]], land[i].at[j], s_sem.at[3 * i + j], r_sem.at[3 * i + j], (*chip, c))
            for i in range(len(src)) for j, chip in enumerate(_other_chips(x, y))]


def _scatter_blocks_start(arrs):
    n = len(arrs)
    lands = [lax.empty((3,) + a.shape[1:], a.dtype) for a in arrs]

    def body(*refs):
        src, land, s_sem, r_sem, token = refs[:n], refs[n:2 * n], refs[2 * n], refs[2 * n + 1], refs[-1]
        for cp in _scatter_copies(src, land, s_sem, r_sem):
            cp.start()
        token[...] = jnp.zeros_like(token)

    dma = pltpu.SemaphoreType.DMA
    hbm = [pltpu.with_memory_space_constraint(a, pltpu.HBM) for a in list(arrs) + lands]
    out = pl.pallas_call(
        body, name="scatter_blocks_start",
        out_shape=[dma((3 * n,)), dma((3 * n,))] + [pltpu.HBM(a.shape, a.dtype) for a in hbm]
        + [jax.ShapeDtypeStruct((8, 128), F32)],
        in_specs=[_HBM] * (2 * n), out_specs=[_SEM, _SEM] + [_HBM] * (2 * n) + [pl.BlockSpec(memory_space=pltpu.VMEM)],
        input_output_aliases={i: 2 + i for i in range(2 * n)},
        compiler_params=pltpu.CompilerParams(has_side_effects=_EFFECT),
    )(*hbm)
    return out[0], out[1], out[2:2 + n], out[2 + n:2 + 2 * n], out[-1]


def _scatter_blocks_wait(s_sem, r_sem, srcs, lands, after):
    n = len(srcs)

    def body(*refs):
        src, land, s_sem, r_sem = refs[:n], refs[n:2 * n], refs[2 * n], refs[2 * n + 1]
        for cp in _scatter_copies(src, land, s_sem, r_sem):
            cp.wait_send()
            cp.wait_recv()

    out = pl.pallas_call(
        body, name="scatter_blocks_wait",
        out_shape=[pltpu.HBM(a.shape, a.dtype) for a in list(srcs) + list(lands)],
        in_specs=[_HBM] * (2 * n) + [_SEM, _SEM, ANY], out_specs=[_HBM] * (2 * n),
        input_output_aliases={i: i for i in range(2 * n)},
        compiler_params=pltpu.CompilerParams(has_side_effects=_EFFECT),
    )(*srcs, *lands, s_sem, r_sem, after)
    return out[:n], out[n:]


def _share_halves(arrs):
    n = len(arrs)

    def body(*refs):
        buf, s_sem, r_sem = refs[n:2 * n], refs[2 * n], refs[2 * n + 1]
        x, y, c = _place()
        cps = []
        for i in range(n):
            mine = buf[i].at[_half(buf[i].shape[0], c)]
            cps.append(_rcopy(mine, mine, s_sem.at[i], r_sem.at[i], (x, y, 1 - c)))
        for cp in cps:
            cp.start()
        for i in range(n):
            theirs = buf[i].at[_half(buf[i].shape[0], 1 - c)]
            _rcopy(theirs, theirs, s_sem.at[i], r_sem.at[i], (x, y, 1 - c)).wait_recv()
        for cp in cps:
            cp.wait_send()

    dma = pltpu.SemaphoreType.DMA
    return pl.pallas_call(
        body, name="share_halves", in_specs=[ANY] * n, out_specs=[ANY] * n,
        out_shape=[jax.ShapeDtypeStruct(a.shape, a.dtype) for a in arrs],
        input_output_aliases={i: i for i in range(n)}, scratch_shapes=[dma((n,)), dma((n,))],
    )(*arrs)


def _allreduce_small(packed):
    rows = packed.shape[0]

    def body(p_ref, o_ref, buf, s_sem, r_sem):
        x, y, c = _place()
        me = 4 * x + 2 * y + c
        buf[me] = p_ref[...]
        cps = []
        for d in range(1, N_DEV):
            px, py, pc = ((1 - x) if d & 4 else x), ((1 - y) if d & 2 else y), ((1 - c) if d & 1 else c)
            cps.append(_rcopy(p_ref, buf.at[me], s_sem.at[d - 1], r_sem.at[d - 1], (px, py, pc)))
        for cp in cps:
            cp.start()
        for d in range(1, N_DEV):
            px, py, pc = ((1 - x) if d & 4 else x), ((1 - y) if d & 2 else y), ((1 - c) if d & 1 else c)
            landed = buf.at[4 * px + 2 * py + pc]
            _rcopy(p_ref, landed, s_sem.at[d - 1], r_sem.at[d - 1], (px, py, pc)).wait_recv()
        for cp in cps:
            cp.wait_send()
        acc = buf[0]
        for d in range(1, N_DEV):
            acc = acc + buf[d]
        o_ref[...] = acc

    dma = pltpu.SemaphoreType.DMA
    return pl.pallas_call(
        body, name="allreduce_small", out_shape=jax.ShapeDtypeStruct(packed.shape, F32),
        scratch_shapes=[pltpu.VMEM((N_DEV, rows, 128), F32), dma((N_DEV - 1,)), dma((N_DEV - 1,))],
    )(packed)


def _row_tile(rows, cols):
    tr = max(8, min(rows, (1 << 20) // (4 * cols) // 8 * 8))
    while rows % tr:
        tr -= 8
    return tr


def _chip_sum(name, g5, recv, c_arr):
    nb, _, hs, cols = g5.shape
    tr = _row_tile(hs, cols)

    def body(_, a_ref, b_ref, o_ref):
        o_ref[...] = (a_ref[...] + b_ref[...].astype(F32)).astype(o_ref.dtype)

    blk = pl.BlockSpec((None, tr, cols), lambda b, i, c: (b, i, 0))
    return pl.pallas_call(
        body, name=name,
        grid_spec=pltpu.PrefetchScalarGridSpec(
            num_scalar_prefetch=1, grid=(nb, hs // tr),
            in_specs=[pl.BlockSpec((None, None, tr, cols), lambda b, i, c: (b, c[0], i, 0)), blk], out_specs=blk),
        out_shape=jax.ShapeDtypeStruct((nb, hs, cols), WIRE_DTYPE),
    )(c_arr, g5, recv)


def _final_sum(name, own, recv, place_arr):
    _, hs, cols = own.shape
    tr = _row_tile(hs, cols)
    nt = hs // tr

    def body(_, a_ref, r_ref, o_ref):
        o_ref[...] = ((a_ref[...].astype(F32) + r_ref[0].astype(F32)) + r_ref[1].astype(F32)) + r_ref[2].astype(F32)

    return pl.pallas_call(
        body, name=name,
        grid_spec=pltpu.PrefetchScalarGridSpec(
            num_scalar_prefetch=1, grid=(nt,),
            in_specs=[pl.BlockSpec((None, tr, cols), lambda i, m: (m[0], i, 0)),
                      pl.BlockSpec((3, tr, cols), lambda i, m: (0, i, 0))],
            out_specs=pl.BlockSpec((tr, cols), lambda i, m: (m[1] * nt + i, 0))),
        out_shape=jax.ShapeDtypeStruct((2 * hs, cols), F32),
    )(place_arr, own, recv)


def _adamw(w, g, m, v):
    m = ADAM_B1 * m + (1.0 - ADAM_B1) * g
    v = ADAM_B2 * v + (1.0 - ADAM_B2) * (g * g)
    m_hat = m / (1.0 - ADAM_B1 ** ADAM_STEP)
    v_hat = v / (1.0 - ADAM_B2 ** ADAM_STEP)
    return -ADAM_LR * (m_hat / (jnp.sqrt(v_hat) + ADAM_EPS) + ADAM_WD * w), m, v


def _adamw_call(name, w, g, m, v):
    rows, cols = w.shape
    tr = _row_tile(rows, cols)

    def body(w_ref, g_ref, m_ref, v_ref, d_ref, nm_ref, nv_ref):
        d_ref[...], nm_ref[...], nv_ref[...] = _adamw(w_ref[...], g_ref[...], m_ref[...], v_ref[...])

    blk = pl.BlockSpec((tr, cols), lambda i: (i, 0))
    return pl.pallas_call(
        body, name=name, grid=(rows // tr,), in_specs=[blk] * 4, out_specs=[blk] * 3,
        out_shape=[jax.ShapeDtypeStruct(w.shape, F32)] * 3,
        compiler_params=pltpu.CompilerParams(dimension_semantics=("parallel",)),
    )(w, g, m, v)


def _adamw_small(ws, gs, ms, vs):
    n = len(ws)

    def body(*refs):
        for i in range(n):
            w_ref, g_ref, m_ref, v_ref = (refs[k * n + i] for k in range(4))
            d, nm, nv = _adamw(w_ref[...], g_ref[...], m_ref[...], v_ref[...])
            refs[4 * n + i][...] = d
            refs[5 * n + i][...] = nm
            refs[6 * n + i][...] = nv

    out = pl.pallas_call(
        body, name="adamw_small", out_shape=[jax.ShapeDtypeStruct(a.shape, F32) for a in ws] * 3,
    )(*ws, *gs, *ms, *vs)
    return out[:n], out[n:2 * n], out[2 * n:]


_BIG = ("w_in", "w_oa", "w_ob", "w_out", "w_pg", "w_ple")
_SMALL = ("norm_g", "ln_a_g", "ln_a_b", "w_s", "b_s", "conv_w", "conv_b", "dt_bias", "a_log", "d_skip", "ssm_norm_g",
          "ple_norm_g", "final_g")
_WEIGHTS = ("norm_g", "w_in", "ln_a_g", "ln_a_b", "w_s", "b_s", "conv_w", "conv_b", "dt_bias", "a_log", "d_skip",
            "ssm_norm_g", "w_oa", "w_ob", "w_out", "ple_norm_g", "w_pg", "w_ple", "final_g")
_COL_SHARDED = ("w_in", "w_ple")
_PACK = 1024


def _blocks_to_full(col_sharded, blocks):
    if col_sharded:
        return jnp.concatenate([blocks[k] for k in range(N_CHIPS)], axis=1)
    return blocks.reshape(N_CHIPS * blocks.shape[1], blocks.shape[2])


def _full_to_blocks(col_sharded, full):
    if col_sharded:
        w = full.shape[1] // N_CHIPS
        return jnp.stack([full[:, k * w:(k + 1) * w] for k in range(N_CHIPS)])
    return full.reshape(N_CHIPS, full.shape[0] // N_CHIPS, full.shape[1])


def _two_d(n, a):
    if n == "w_s":
        return a.reshape(G_A * CHUNK, CHUNK)
    if n in ("b_s", "conv_w"):
        return a.reshape(a.shape[-2], a.shape[-1])
    return a.reshape(1, a.shape[-1])


def kernel(x, p, norm_g, w_in, ln_a_g, ln_a_b, w_s, b_s, conv_w, conv_b, dt_bias, a_log, d_skip, ssm_norm_g, w_oa, w_ob, w_out, ple_norm_g, w_pg, w_ple, final_g, loss_target, m_norm_g, m_w_in, m_ln_a_g, m_ln_a_b, m_w_s, m_b_s, m_conv_w, m_conv_b, m_dt_bias, m_a_log, m_d_skip, m_ssm_norm_g, m_w_oa, m_w_ob, m_w_out, m_ple_norm_g, m_w_pg, m_w_ple, m_final_g, v_norm_g, v_w_in, v_ln_a_g, v_ln_a_b, v_w_s, v_b_s, v_conv_w, v_conv_b, v_dt_bias, v_a_log, v_d_skip, v_ssm_norm_g, v_w_oa, v_w_ob, v_w_out, v_ple_norm_g, v_w_pg, v_w_ple, v_final_g):
    wt = dict(norm_g=norm_g, w_in=w_in, ln_a_g=ln_a_g, ln_a_b=ln_a_b, w_s=w_s, b_s=b_s, conv_w=conv_w, conv_b=conv_b,
              dt_bias=dt_bias, a_log=a_log, d_skip=d_skip, ssm_norm_g=ssm_norm_g, w_oa=w_oa, w_ob=w_ob, w_out=w_out,
              ple_norm_g=ple_norm_g, w_pg=w_pg, w_ple=w_ple, final_g=final_g)
    mom = dict(norm_g=m_norm_g, w_in=m_w_in, ln_a_g=m_ln_a_g, ln_a_b=m_ln_a_b, w_s=m_w_s, b_s=m_b_s, conv_w=m_conv_w,
               conv_b=m_conv_b, dt_bias=m_dt_bias, a_log=m_a_log, d_skip=m_d_skip, ssm_norm_g=m_ssm_norm_g, w_oa=m_w_oa,
               w_ob=m_w_ob, w_out=m_w_out, ple_norm_g=m_ple_norm_g, w_pg=m_w_pg, w_ple=m_w_ple, final_g=m_final_g)
    vel = dict(norm_g=v_norm_g, w_in=v_w_in, ln_a_g=v_ln_a_g, ln_a_b=v_ln_a_b, w_s=v_w_s, b_s=v_b_s, conv_w=v_conv_w,
               conv_b=v_conv_b, dt_bias=v_dt_bias, a_log=v_a_log, d_skip=v_d_skip, ssm_norm_g=v_ssm_norm_g, w_oa=v_w_oa,
               w_ob=v_w_ob, w_out=v_w_out, ple_norm_g=v_ple_norm_g, w_pg=v_w_pg, w_ple=v_w_ple, final_g=v_final_g)
    xi, yi, ci = _place()
    me = 2 * xi + yi
    c_arr = jnp.reshape(ci, (1,)).astype(jnp.int32)
    place_arr = jnp.stack([me, ci]).astype(jnp.int32)

    shard = {n: wt[n][0] for n in _BIG}
    wire = {n: shard[n].astype(WIRE_DTYPE) for n in _BIG}
    w_in_blocks, conv_blocks = _gather_weights([wire["w_in"]], conv_w[0])
    g_ssem, g_rsem, g_sent, g_lands, g_token = _late_gather_start([wire[n] for n in _BIG[1:]])
    full = {"conv_w": _blocks_to_full(True, conv_blocks)}
    for n in _SMALL:
        if n != "conv_w":
            full[n] = wt[n][0] if wt[n].ndim > 2 else wt[n].reshape(1, wt[n].shape[-1])

    def late_weights(after):
        blocks = _late_gather_wait(g_ssem, g_rsem, g_sent, g_lands, after)
        return {n: _blocks_to_full(n in _COL_SHARDED, b) for n, b in zip(_BIG[1:], blocks)}

    w = _layout_weights(full, w_in_blocks=w_in_blocks)
    loss_row, g, ctx = _forward_backward(x[0], p[0, 0], loss_target[0], w, late_weights, after=(g_token,))
    loss = lax.psum(loss_row[0, 0], ("x", "y", "c"))

    parts = {n: _full_to_blocks(n in _COL_SHARDED, g[n]) for n in _BIG[1:]}
    parts["w_main"] = g["w_main"][None]
    parts["w_dt"] = jnp.pad(_lanes_to_heads(g["w_dt"]), ((0, 0), (0, 128 - N_HEADS)))[None]
    names = ("w_main", "w_dt") + _BIG[1:]
    g5 = {n: parts[n].reshape(parts[n].shape[0], 2, parts[n].shape[1] // 2, parts[n].shape[2]) for n in names}
    to_sibling = [lax.dynamic_index_in_dim(g5[n], 1 - ci, axis=1, keepdims=False).astype(WIRE_DTYPE) for n in names]
    from_sibling = _swap_with_sibling(to_sibling)
    chip = {n: _chip_sum("chip_sum_" + n, g5[n], r, c_arr) for n, r in zip(names, from_sibling)}
    chip["w_in"] = _w_in_grad_blocks(chip["w_main"][0], chip["w_dt"][0])
    chip_wire = [chip[n] for n in _BIG]
    s_sem, r_sem, sent, lands, token = _scatter_blocks_start(chip_wire)
    grad_x, g["norm_g"] = _input_grad(x[0], w, ctx, after=(token,))
    sent, from_chips = _scatter_blocks_wait(s_sem, r_sem, sent, lands, grad_x)
    halves = [_final_sum("final_sum_" + n, a, r, place_arr) for n, a, r in zip(_BIG, sent, from_chips)]
    grads = dict(zip(_BIG, _share_halves(halves)))
    g = _natural_grads(g)

    pieces = [_two_d(n, g[n]).reshape(-1) for n in _SMALL]
    sizes = [v.shape[0] for v in pieces]
    padded = [-(-s // _PACK) * _PACK for s in sizes]
    packed = jnp.concatenate([jnp.pad(v, (0, ps - s)) for v, s, ps in zip(pieces, sizes, padded)]).reshape(-1, 128)
    summed = _allreduce_small(packed).reshape(-1)
    off = 0
    for n, s, ps in zip(_SMALL, sizes, padded):
        grads[n] = _two_d(n, g[n]).shape, summed[off:off + s]
        off += ps
    for n in _SMALL:
        shape, flat = grads[n]
        grads[n] = flat.reshape(shape)
    grads["conv_w"] = lax.dynamic_slice_in_dim(grads["conv_w"], me * (CONV_DIM // N_CHIPS), CONV_DIM // N_CHIPS, axis=1)

    delta, new_m, new_v = {}, {}, {}
    for n in _BIG:
        delta[n], new_m[n], new_v[n] = _adamw_call("adamw_" + n, shard[n], grads[n], mom[n][0], vel[n][0])
    small = _adamw_small([_two_d(n, wt[n]) for n in _SMALL], [grads[n] for n in _SMALL],
                         [_two_d(n, mom[n]) for n in _SMALL], [_two_d(n, vel[n]) for n in _SMALL])
    for i, n in enumerate(_SMALL):
        delta[n], new_m[n], new_v[n] = small[0][i], small[1][i], small[2][i]

    def shaped(d):
        return [d[n].reshape(wt[n].shape) for n in _WEIGHTS]

    return (loss, grad_x[None], *shaped(grads), *shaped(delta), *shaped(new_m), *shaped(new_v))
```

```python
import functools

import jax
import jax.numpy as jnp
from jax import lax
from jax.experimental import pallas as pl
from jax.experimental.pallas import tpu as pltpu

F32 = jnp.float32
MXU_DTYPE = jnp.bfloat16
ACT_DTYPE = jnp.bfloat16
WIRE_DTYPE = jnp.bfloat16

D_MODEL = 1024
PLE_DIM = 256
CHUNK = 128
EPS = 1e-6
E_A = D_MODEL
G_A = 4
D_INNER = 2 * D_MODEL
HEAD_DIM = 64
N_HEADS = D_INNER // HEAD_DIM
N_STATE = 128
N_GROUPS = 4
HEADS_PER_GROUP = N_HEADS // N_GROUPS
PAIRS_PER_GROUP = HEADS_PER_GROUP // 2
CONV_K = 4
CONV_DIM = D_INNER + 2 * N_GROUPS * N_STATE
N_IN = 3 * E_A + D_INNER + CONV_DIM + N_HEADS + 2 * D_MODEL
N_CHIPS = 4
N_DEV = 8
W_IN_BLOCK = N_IN // N_CHIPS

UVZ_W, XBC_W, ZB_W, G_W = 3 * E_A, CONV_DIM, D_INNER, 2 * D_MODEL
MAIN_W = UVZ_W + XBC_W + ZB_W + G_W
UVZ_CB, XBC_CB, ZB_CB, G_CB = 0, 1, 3, 4
DT_W = N_GROUPS * 128

ADAM_LR, ADAM_B1, ADAM_B2, ADAM_EPS, ADAM_WD, ADAM_STEP = 0.001, 0.9, 0.999, 1e-08, 0.01, 10

MESH = pl.DeviceIdType.MESH
ANY = pl.BlockSpec(memory_space=pl.ANY)


def _mxu(v):
    return v.astype(MXU_DTYPE)


def _dot(a, b, dims=(((1,), (0,)), ((), ()))):
    return lax.dot_general(_mxu(a), _mxu(b), dims, preferred_element_type=F32)


MM_TILE = 1024
MM_VMEM_BUDGET = 46 << 20


def _mm_tk(m, n, k, tm, tn, a_bytes, b_bytes, out_bytes, extra_bytes):
    for parts in range(1, k // 128 + 1):
        if k % parts or (k // parts) % 128 and parts > 1:
            continue
        tk = k // parts
        need = 2 * tk * (tm * a_bytes + tn * b_bytes) + 2 * tm * tn * (out_bytes + extra_bytes) + (tm * tn * 4 if parts > 1 else 0)
        if need <= MM_VMEM_BUDGET:
            return tk
    return 128


def _matmul(a, b, *, mode, name, out_dtype, m, n, k, tm=MM_TILE, tn=MM_TILE, tk=None, a_off=0, b_off=0,
            extras=(), epilogue=None, after=()):
    tm, tn = min(tm, m), min(tn, n)
    if tk is None:
        tk = _mm_tk(m, n, k, tm, tn, a.dtype.itemsize, b.dtype.itemsize, jnp.dtype(out_dtype).itemsize,
                    sum(e.dtype.itemsize for e in extras))
    tk = min(tk, k)
    assert m % tm == 0 and n % tn == 0 and k % tk == 0, (name, m, n, k, tm, tn, tk)
    nk = k // tk
    if mode == "nn":
        assert a_off % tk == 0 and b_off % tn == 0
        a_spec = pl.BlockSpec((tm, tk), lambda i, j, kk: (i, kk + a_off // tk))
        b_spec = pl.BlockSpec((tk, tn), lambda i, j, kk: (kk, j + b_off // tn))
        dims = (((1,), (0,)), ((), ()))
    elif mode == "nt":
        a_spec = pl.BlockSpec((tm, tk), lambda i, j, kk: (i, kk))
        b_spec = pl.BlockSpec((tn, tk), lambda i, j, kk: (j, kk))
        dims = (((1,), (1,)), ((), ()))
    else:
        assert a_off % tm == 0 and b_off % tn == 0
        a_spec = pl.BlockSpec((tk, tm), lambda i, j, kk: (kk, i + a_off // tm))
        b_spec = pl.BlockSpec((tk, tn), lambda i, j, kk: (kk, j + b_off // tn))
        dims = (((0,), (0,)), ((), ()))
    ne = len(extras)

    def finish(acc, extra_refs, o_ref):
        res = acc if epilogue is None else epilogue(acc, *[e[...] for e in extra_refs])
        o_ref[...] = res.astype(o_ref.dtype)

    def body(a_ref, b_ref, *rest):
        extra_refs, o_ref = rest[:ne], rest[ne + len(after)]
        part = _dot(a_ref[...], b_ref[...], dims)
        if nk == 1:
            finish(part, extra_refs, o_ref)
            return
        acc_ref = rest[ne + len(after) + 1]
        kk = pl.program_id(2)

        @pl.when(kk == 0)
        def _():
            acc_ref[...] = part

        @pl.when(kk > 0)
        def _():
            acc_ref[...] += part

        @pl.when(kk == nk - 1)
        def _():
            finish(acc_ref[...], extra_refs, o_ref)

    o_spec = pl.BlockSpec((tm, tn), lambda i, j, kk: (i, j))
    return pl.pallas_call(
        body, name=name, grid=(m // tm, n // tn, nk),
        in_specs=[a_spec, b_spec] + [o_spec] * ne + [ANY] * len(after), out_specs=o_spec,
        out_shape=jax.ShapeDtypeStruct((m, n), out_dtype),
        scratch_shapes=[pltpu.VMEM((tm, tn), F32)] if nk > 1 else [],
        compiler_params=pltpu.CompilerParams(dimension_semantics=("parallel", "parallel", "arbitrary")),
    )(a, b, *extras, *after)


def _rows_matmul(name, f, rows, pars, b, *, out_dtype, n, k, tm, nrows, tn=MM_TILE, extras=(), epilogue=None, after=(),
                 side=None):
    tm, tn = min(tm, nrows), min(tn, n)
    assert nrows % tm == 0 and n % tn == 0, (name, nrows, n, tm, tn)
    nr, npar, ne, nj = len(rows), len(pars), len(extras), n // tn
    ns = 0 if side is None else 1
    n_in = nr + npar + 1 + ne + ns + len(after)

    def body(*refs):
        row_refs, par_refs, b_ref = refs[:nr], refs[nr:nr + npar], refs[nr + npar]
        extra_refs = refs[nr + npar + 1:nr + npar + 1 + ne]
        a_ref, o_ref = refs[n_in], refs[n_in + 1]

        def make_a():
            a = f(*[r[...].astype(F32) for r in row_refs], *[p[...] for p in par_refs])[0]
            a_ref[...] = a.astype(a_ref.dtype)
            if ns:
                refs[n_in + 2][...] = _dot(a_ref[...], refs[nr + npar + 1 + ne][...]).astype(refs[n_in + 2].dtype)

        if nj == 1:
            make_a()
        else:
            pl.when(pl.program_id(1) == 0)(make_a)
        res = _dot(a_ref[...], b_ref[...])
        if epilogue is not None:
            res = epilogue(res, *[e[...] for e in extra_refs])
        o_ref[...] = res.astype(o_ref.dtype)

    o_spec = pl.BlockSpec((tm, tn), lambda i, j: (i, j))
    side_in = [] if side is None else [pl.BlockSpec(tuple(side[0].shape), lambda i, j: (0, 0))]
    side_out = [] if side is None else [pl.BlockSpec((tm, side[0].shape[1]), lambda i, j: (i, 0))]
    side_shape = [] if side is None else [jax.ShapeDtypeStruct((nrows, side[0].shape[1]), side[1])]
    return pl.pallas_call(
        body, name=name, grid=(nrows // tm, nj),
        in_specs=[pl.BlockSpec((tm, w), lambda i, j, cb=cb: (i, cb)) for _, w, cb in rows]
        + [pl.BlockSpec(tuple(p.shape), lambda i, j, nd=p.ndim: (0,) * nd) for p in pars]
        + [pl.BlockSpec((k, tn), lambda i, j: (0, j))] + [o_spec] * ne + side_in + [ANY] * len(after),
        out_specs=[pl.BlockSpec((tm, k), lambda i, j: (i, 0)), o_spec] + side_out,
        out_shape=[jax.ShapeDtypeStruct((nrows, k), ACT_DTYPE), jax.ShapeDtypeStruct((nrows, n), out_dtype)] + side_shape,
        compiler_params=pltpu.CompilerParams(dimension_semantics=("parallel", "arbitrary")),
    )(*[r[0] for r in rows], *pars, b, *extras, *([] if side is None else [side[0]]), *after)


def _row_spec(tm, width, cb):
    return pl.BlockSpec((tm, width), lambda i: (i, cb))


def _whole_spec(shape):
    nd = len(shape)
    return pl.BlockSpec(tuple(shape), lambda i: (0,) * nd)


def _rows_call(name, f, rows, pars, outs, *, tm, nrows):
    tm = min(tm, nrows)
    nr, npar = len(rows), len(pars)

    def body(*refs):
        rv = [r[...].astype(F32) for r in refs[:nr]]
        pv = [p[...] for p in refs[nr:nr + npar]]
        res = f(*rv, *pv)
        for o_ref, r in zip(refs[nr + npar:], res):
            o_ref[...] = r.astype(o_ref.dtype)

    return pl.pallas_call(
        body, name=name, grid=(nrows // tm,),
        in_specs=[_row_spec(tm, w, cb) for _, w, cb in rows] + [_whole_spec(p.shape) for p in pars],
        out_specs=[_row_spec(tm, w, 0) for w, _ in outs],
        out_shape=[jax.ShapeDtypeStruct((nrows, w), dt) for w, dt in outs],
        compiler_params=pltpu.CompilerParams(dimension_semantics=("parallel",)),
    )(*[r[0] for r in rows], *pars)


def _rows_vjp_call(name, f, rows, pars, cots, drows, *, tm, nrows, cot_mm=None):
    tm = min(tm, nrows)
    nr, npar, nc = len(rows), len(pars), len(cots)
    mm_args, mm_specs = [], []
    if cot_mm is not None:
        mm_a, mm_b, mm_add = cot_mm
        mm_args = [mm_a, mm_b] + ([] if mm_add is None else [mm_add])
        mm_specs = [_row_spec(tm, mm_a.shape[1], 0), _whole_spec(mm_b.shape)]
        mm_specs += [] if mm_add is None else [_row_spec(tm, mm_b.shape[0], 0)]
    alias_bufs, aliases = [], {}
    out_shape, out_specs = [], []
    for (arr, w, cb), d in zip(rows, drows):
        if d is None:
            continue
        dt, into = d
        if into is None:
            out_shape.append(jax.ShapeDtypeStruct((nrows, w), dt))
            out_specs.append(_row_spec(tm, w, 0))
        else:
            buf, total, ocb = into
            if buf is not None:
                aliases[nr + npar + nc + len(alias_bufs)] = len(out_shape)
                alias_bufs.append(buf)
            out_shape.append(jax.ShapeDtypeStruct((nrows, total), dt))
            out_specs.append(_row_spec(tm, w, ocb))
    n_drow = len(out_shape)
    for p in pars:
        out_shape.append(jax.ShapeDtypeStruct(p.shape, F32))
        out_specs.append(_whole_spec(p.shape))
    na = len(alias_bufs)

    def body(*refs):
        rv = [r[...].astype(F32) for r in refs[:nr]]
        pv = [p[...] for p in refs[nr:nr + npar]]
        cv = tuple(c[...].astype(F32) for c in refs[nr + npar:nr + npar + nc])
        o_refs = refs[nr + npar + nc + na + len(mm_args):]
        if mm_args:
            mm_refs = refs[nr + npar + nc + na:nr + npar + nc + na + len(mm_args)]
            c0 = _dot(mm_refs[0][...], mm_refs[1][...], (((1,), (1,)), ((), ())))
            if len(mm_refs) == 3:
                c0 = c0 + mm_refs[2][...].astype(F32)
            cv = (c0,) + cv
        _, vjp = jax.vjp(f, *rv, *pv)
        g = vjp(cv)
        oi = 0
        for ri, d in enumerate(drows):
            if d is not None:
                o_refs[oi][...] = g[ri].astype(o_refs[oi].dtype)
                oi += 1
        first = pl.program_id(0) == 0
        for pi in range(npar):
            acc = o_refs[n_drow + pi]

            @pl.when(first)
            def _(acc=acc):
                acc[...] = jnp.zeros_like(acc)

            acc[...] += g[nr + pi]

    return pl.pallas_call(
        body, name=name, grid=(nrows // tm,),
        in_specs=[_row_spec(tm, w, cb) for _, w, cb in rows] + [_whole_spec(p.shape) for p in pars]
        + [_row_spec(tm, w, cb) for _, w, cb in cots] + [ANY] * na + mm_specs,
        out_specs=out_specs, out_shape=out_shape, input_output_aliases=aliases,
        compiler_params=pltpu.CompilerParams(dimension_semantics=("arbitrary",)),
    )(*[r[0] for r in rows], *pars, *[c[0] for c in cots], *alias_bufs, *mm_args)


def _rms(x, g):
    return x * lax.rsqrt(jnp.mean(x * x, axis=-1, keepdims=True) + EPS) * g


def _f_rms(x, g):
    return (_rms(x, g),)


def _tril_mask():
    return lax.broadcasted_iota(jnp.int32, (CHUNK, CHUNK), 0) >= lax.broadcasted_iota(jnp.int32, (CHUNK, CHUNK), 1)


def _f_branch_a(uvz, ln_g, ln_b, w_s, b_s):
    u = jax.nn.gelu(uvz[:, :E_A])
    v = jax.nn.gelu(uvz[:, E_A:2 * E_A])
    z = uvz[:, 2 * E_A:]
    xc = v - jnp.mean(v, axis=-1, keepdims=True)
    vn = xc * lax.rsqrt(jnp.mean(xc * xc, axis=-1, keepdims=True) + EPS) * ln_g + ln_b
    mask = _tril_mask()
    ws = [jnp.where(mask, w_s[g], 0.0) for g in range(G_A)]
    gw = E_A // G_A
    rows = []
    for c in range(uvz.shape[0] // CHUNK):
        vc = vn[c * CHUNK:(c + 1) * CHUNK]
        rows.append(jnp.concatenate([_dot(ws[g], vc[:, g * gw:(g + 1) * gw]) + b_s[g] for g in range(G_A)], axis=1))
    sv = rows[0] if len(rows) == 1 else jnp.concatenate(rows, axis=0)
    return (u * sv * jax.nn.silu(z),)


def _f_gnorm(y, zb, g):
    yz = y * jax.nn.silu(zb)
    gw = D_INNER // N_GROUPS
    parts = []
    for i in range(N_GROUPS):
        s = yz[:, i * gw:(i + 1) * gw]
        parts.append(s * lax.rsqrt(jnp.mean(s * s, axis=-1, keepdims=True) + EPS))
    return (jnp.concatenate(parts, axis=1) * g,)


def _f_merge(g2, oa, ob):
    return (jax.nn.sigmoid(g2[:, :D_MODEL]) * oa + jax.nn.sigmoid(g2[:, D_MODEL:]) * ob,)


def _f_loss(x1, gp, pe, tgt, fg):
    x2 = x1 + jax.nn.sigmoid(gp) * pe
    err = _rms(x2, fg) - tgt
    return 0.5 * jnp.sum(jnp.mean(err * err, axis=-1))


def _head(x1, p, tgt, ple_g, w_pg, w_ple, fg, *, tm, nrows):
    tm = min(tm, nrows)

    def body(x1_ref, p_ref, t_ref, pg_ref, wpg_ref, wple_ref, fg_ref, hp_ref, dx_ref, dgp_ref, dpe_ref, dfg_ref, loss_ref):
        x1 = x1_ref[...]
        hp_ref[...] = _rms(x1, pg_ref[...]).astype(hp_ref.dtype)
        gp = _dot(hp_ref[...], wpg_ref[...])
        pe = _dot(p_ref[...], wple_ref[...])
        loss, vjp = jax.vjp(_f_loss, x1, gp, pe, t_ref[...], fg_ref[...])
        dx, dgp, dpe, _, dfg = vjp(jnp.ones((), F32))
        dx_ref[...] = dx
        dgp_ref[...] = dgp.astype(dgp_ref.dtype)
        dpe_ref[...] = dpe.astype(dpe_ref.dtype)

        @pl.when(pl.program_id(0) == 0)
        def _():
            dfg_ref[...] = jnp.zeros_like(dfg_ref)
            loss_ref[...] = jnp.zeros_like(loss_ref)

        dfg_ref[...] += dfg
        loss_ref[...] += jnp.full(loss_ref.shape, loss, F32)

    row = _row_spec(tm, D_MODEL, 0)
    act = jax.ShapeDtypeStruct((nrows, D_MODEL), ACT_DTYPE)
    return pl.pallas_call(
        body, name="head", grid=(nrows // tm,),
        in_specs=[row, _row_spec(tm, PLE_DIM, 0), row, _whole_spec((1, D_MODEL)), _whole_spec(w_pg.shape),
                  _whole_spec(w_ple.shape), _whole_spec((1, D_MODEL))],
        out_specs=[row, row, row, row, _whole_spec((1, D_MODEL)), _whole_spec((1, 128))],
        out_shape=[act, jax.ShapeDtypeStruct((nrows, D_MODEL), F32), act, act, jax.ShapeDtypeStruct((1, D_MODEL), F32),
                   jax.ShapeDtypeStruct((1, 128), F32)],
        compiler_params=pltpu.CompilerParams(dimension_semantics=("arbitrary",)),
    )(x1, p, tgt, ple_g, w_pg, w_ple, fg)


def _shift_rows(cur, edge, j, up):
    tm = cur.shape[0]
    row = lax.broadcasted_iota(jnp.int32, cur.shape, 0)
    if up:
        sh = pltpu.roll(cur, tm - j, 0)
        e = jnp.tile(pltpu.roll(edge, 8 - j, 0), (tm // 8, 1))
        return jnp.where(row >= tm - j, e, sh)
    sh = pltpu.roll(cur, j, 0)
    e = jnp.tile(pltpu.roll(edge, j, 0), (tm // 8, 1))
    return jnp.where(row < j, e, sh)


def _conv_pre(cur, prev, w, b):
    acc = cur * w[CONV_K - 1:CONV_K] + b
    taps = [cur]
    for j in range(1, CONV_K):
        s = _shift_rows(cur, prev, j, up=False)
        taps.append(s)
        acc = acc + s * w[CONV_K - 1 - j:CONV_K - j]
    return acc, taps


def _halo_specs(tm, nrows, cb, before):
    nb = tm // 8
    last = nrows // 8 - 1
    if before:
        return pl.BlockSpec((8, XBC_W), lambda i: (jnp.maximum(i * nb - 1, 0), cb))
    return pl.BlockSpec((8, XBC_W), lambda i: (jnp.minimum((i + 1) * nb, last), cb))


def _conv_fwd(proj, conv_w, conv_b, *, tm, nrows):
    tm = min(tm, nrows)

    def body(cur_ref, prev_ref, w_ref, b_ref, o_ref):
        prev = jnp.where(pl.program_id(0) == 0, 0.0, prev_ref[...].astype(F32))
        pre, _ = _conv_pre(cur_ref[...].astype(F32), prev, w_ref[...], b_ref[...])
        o_ref[...] = jax.nn.silu(pre).astype(o_ref.dtype)

    return pl.pallas_call(
        body, name="conv_fwd", grid=(nrows // tm,),
        in_specs=[_row_spec(tm, XBC_W, XBC_CB), _halo_specs(tm, nrows, XBC_CB, True),
                  _whole_spec((CONV_K, XBC_W)), _whole_spec((1, XBC_W))],
        out_specs=_row_spec(tm, XBC_W, 0), out_shape=jax.ShapeDtypeStruct((nrows, XBC_W), ACT_DTYPE),
        compiler_params=pltpu.CompilerParams(dimension_semantics=("parallel",)),
    )(proj, proj, conv_w, conv_b)


def _conv_bwd_pre(proj, conv_w, conv_b, dact, *, tm, nrows):
    tm = min(tm, nrows)
    nb = N_GROUPS * N_STATE

    def body(cur_ref, prev_ref, w_ref, b_ref, dxs_ref, dbm_ref, dcm_ref, dpre_ref, dw_ref, db_ref):
        prev = jnp.where(pl.program_id(0) == 0, 0.0, prev_ref[...].astype(F32))
        pre, taps = _conv_pre(cur_ref[...].astype(F32), prev, w_ref[...], b_ref[...])
        sg = jax.nn.sigmoid(pre)
        dy = jnp.concatenate([dxs_ref[...], dbm_ref[...], dcm_ref[...]], axis=1).astype(F32)
        dpre = dy * sg * (1.0 + pre * (1.0 - sg))
        dpre_ref[...] = dpre.astype(dpre_ref.dtype)

        @pl.when(pl.program_id(0) == 0)
        def _():
            dw_ref[...] = jnp.zeros_like(dw_ref)
            db_ref[...] = jnp.zeros_like(db_ref)

        db_ref[...] += jnp.sum(dpre, axis=0, keepdims=True)
        for j in range(CONV_K):
            k = CONV_K - 1 - j
            dw_ref[k:k + 1, :] += jnp.sum(dpre * taps[j], axis=0, keepdims=True)

    return pl.pallas_call(
        body, name="conv_bwd_pre", grid=(nrows // tm,),
        in_specs=[_row_spec(tm, XBC_W, XBC_CB), _halo_specs(tm, nrows, XBC_CB, True),
                  _whole_spec((CONV_K, XBC_W)), _whole_spec((1, XBC_W)),
                  _row_spec(tm, D_INNER, 0), _row_spec(tm, nb, 0), _row_spec(tm, nb, 0)],
        out_specs=[_row_spec(tm, XBC_W, 0), _whole_spec((CONV_K, XBC_W)), _whole_spec((1, XBC_W))],
        out_shape=[jax.ShapeDtypeStruct((nrows, XBC_W), ACT_DTYPE), jax.ShapeDtypeStruct((CONV_K, XBC_W), F32),
                   jax.ShapeDtypeStruct((1, XBC_W), F32)],
        compiler_params=pltpu.CompilerParams(dimension_semantics=("arbitrary",)),
    )(proj, proj, conv_w, conv_b, *dact)


def _conv_bwd_x(dpre, conv_w, dproj, *, tm, nrows):
    tm = min(tm, nrows)
    ntiles = nrows // tm

    def body(cur_ref, nxt_ref, w_ref, _, o_ref):
        cur = cur_ref[...].astype(F32)
        nxt = jnp.where(pl.program_id(0) == ntiles - 1, 0.0, nxt_ref[...].astype(F32))
        w = w_ref[...]
        acc = cur * w[CONV_K - 1:CONV_K]
        for j in range(1, CONV_K):
            acc = acc + _shift_rows(cur, nxt, j, up=True) * w[CONV_K - 1 - j:CONV_K - j]
        o_ref[...] = acc.astype(o_ref.dtype)

    return pl.pallas_call(
        body, name="conv_bwd_x", grid=(ntiles,),
        in_specs=[_row_spec(tm, XBC_W, 0), _halo_specs(tm, nrows, 0, False), _whole_spec((CONV_K, XBC_W)), ANY],
        out_specs=_row_spec(tm, XBC_W, XBC_CB), out_shape=jax.ShapeDtypeStruct(dproj.shape, dproj.dtype),
        input_output_aliases={3: 0},
        compiler_params=pltpu.CompilerParams(dimension_semantics=("parallel",)),
    )(dpre, dpre, conv_w, dproj)


SSD_SPAN = 4
_XS_GW = D_INNER // N_GROUPS
_NT = (((1,), (1,)), ((), ()))
_TN = (((0,), (0,)), ((), ()))


def _bf16_terms(x, terms):
    parts, rest = [], x
    for _ in range(terms):
        part = rest.astype(jnp.bfloat16)
        parts.append(part)
        rest = rest - part.astype(F32)
    return parts


def _head_lane_matrix():
    return (lax.broadcasted_iota(jnp.int32, (128, _XS_GW), 0)
            == lax.broadcasted_iota(jnp.int32, (128, _XS_GW), 1) // HEAD_DIM).astype(jnp.bfloat16)


@functools.partial(jax.custom_vjp, nondiff_argnums=(1,))
def _head_lanes(cols, terms):
    e = _head_lane_matrix()
    return sum(jnp.dot(t, e, preferred_element_type=F32) for t in _bf16_terms(cols, terms))


def _head_lanes_fwd(cols, terms):
    return _head_lanes(cols, terms), None


def _head_lanes_bwd(terms, _, g):
    e = _head_lane_matrix()
    return (sum(lax.dot_general(t, e, _NT, preferred_element_type=F32) for t in _bf16_terms(g, 2)),)


_head_lanes.defvjp(_head_lanes_fwd, _head_lanes_bwd)


def _ssd_chunk(k, xs, bm, cm, dtr, hprev, dtb, alog, dsk):
    causal, tri, lo = k
    dt = jax.nn.softplus(dtr + dtb)
    da = dt * (-jnp.exp(alog))
    cs = jnp.dot(tri, da, precision=lax.Precision.HIGHEST, preferred_element_type=F32)
    cst = cs.T
    cs_l = _head_lanes(cs, 3)
    xdt = xs * _head_lanes(dt, 2)
    cb = _dot(cm, bm, _NT)
    yd = []
    for q in range(PAIRS_PER_GROUP):
        xq = xdt[:, 128 * q:128 * (q + 1)]
        y2 = [_dot(cb * jnp.exp(jnp.where(causal, cs[:, h:h + 1] - cst[h:h + 1, :], -jnp.inf)), xq)
              for h in (2 * q, 2 * q + 1)]
        yd.append(jnp.where(lo, y2[0], y2[1]))
    y_off = jnp.exp(cs_l) * _dot(cm, hprev, _NT)
    st = _dot(xdt * jnp.exp(cs_l[CHUNK - 1:CHUNK, :] - cs_l), bm, _TN)
    cdec = jnp.exp(cs[CHUNK - 1:CHUNK, :])
    cd_rows = jnp.concatenate(
        [jnp.broadcast_to(cdec[:, h:h + 1], (HEAD_DIM, N_STATE)) for h in range(HEADS_PER_GROUP)], axis=0)
    dsk_l = _head_lanes(jnp.broadcast_to(dsk, (8, 128)), 2)[:1]
    y = jnp.concatenate(yd, axis=1) + y_off + xs * dsk_l
    return y, cd_rows * hprev + st


def _ssd_span(xs, bm, cm, dtr, h0, dtb, alog, dsk):
    li = lax.broadcasted_iota(jnp.int32, (CHUNK, CHUNK), 0)
    si = lax.broadcasted_iota(jnp.int32, (CHUNK, CHUNK), 1)
    causal = li >= si
    k = (causal, causal.astype(F32), si < HEAD_DIM)
    h, ys = h0, []
    for t in range(xs.shape[0] // CHUNK):
        r = slice(t * CHUNK, (t + 1) * CHUNK)
        y, h = _ssd_chunk(k, xs[r], bm[r], cm[r], dtr[r], h, dtb, alog, dsk)
        ys.append(y)
    return (ys[0] if len(ys) == 1 else jnp.concatenate(ys, axis=0)), h


def _ssd_specs(rev, nsteps, rows):
    def s_of(s):
        return nsteps - 1 - s if rev else s

    xs = pl.BlockSpec((rows, _XS_GW), lambda g, s: (s_of(s), g))
    bm = pl.BlockSpec((rows, N_STATE), lambda g, s: (s_of(s), D_INNER // N_STATE + g))
    cm = pl.BlockSpec((rows, N_STATE), lambda g, s: (s_of(s), D_INNER // N_STATE + N_GROUPS + g))
    dt = pl.BlockSpec((rows, 128), lambda g, s: (s_of(s), g))
    par = pl.BlockSpec((1, 128), lambda g, s: (0, g))
    st = pl.BlockSpec((None, None, _XS_GW, N_STATE), lambda g, s: (g, s_of(s), 0, 0))
    return xs, bm, cm, dt, par, st


def _ssd_fwd(act, dtr, dtb, alog, dsk, *, nrows):
    rows = CHUNK * min(SSD_SPAN, nrows // CHUNK)
    nsteps = nrows // rows
    xs, bm, cm, dt, par, st = _ssd_specs(False, nsteps, rows)

    def body(xs_ref, b_ref, c_ref, dt_ref, dtb_ref, al_ref, dk_ref, y_ref, st_ref, h_ref):
        @pl.when(pl.program_id(1) == 0)
        def _():
            h_ref[...] = jnp.zeros_like(h_ref)

        h0 = h_ref[...]
        st_ref[...] = h0
        y, hnew = _ssd_span(xs_ref[...].astype(F32), b_ref[...].astype(F32), c_ref[...].astype(F32), dt_ref[...],
                            h0, dtb_ref[...], al_ref[...], dk_ref[...])
        y_ref[...] = y.astype(y_ref.dtype)
        h_ref[...] = hnew

    return pl.pallas_call(
        body, name="ssd_fwd", grid=(N_GROUPS, nsteps),
        in_specs=[xs, bm, cm, dt, par, par, par], out_specs=[xs, st],
        out_shape=[jax.ShapeDtypeStruct((nrows, D_INNER), ACT_DTYPE),
                   jax.ShapeDtypeStruct((N_GROUPS, nsteps, _XS_GW, N_STATE), F32)],
        scratch_shapes=[pltpu.VMEM((_XS_GW, N_STATE), F32)],
        compiler_params=pltpu.CompilerParams(dimension_semantics=("arbitrary", "arbitrary")),
    )(act, act, act, dtr, dtb, alog, dsk)


def _ssd_bwd(act, dtr, dtb, alog, dsk, states, dy, *, nrows):
    rows = CHUNK * min(SSD_SPAN, nrows // CHUNK)
    nsteps = nrows // rows
    xs, bm, cm, dt, par, st = _ssd_specs(True, nsteps, rows)

    def body(xs_ref, b_ref, c_ref, dt_ref, dtb_ref, al_ref, dk_ref, st_ref, dy_ref,
             dxs_ref, db_ref, dc_ref, ddt_ref, ddtb_ref, dal_ref, ddk_ref, dh_ref):
        @pl.when(pl.program_id(1) == 0)
        def _():
            dh_ref[...] = jnp.zeros_like(dh_ref)
            ddtb_ref[...] = jnp.zeros_like(ddtb_ref)
            dal_ref[...] = jnp.zeros_like(dal_ref)
            ddk_ref[...] = jnp.zeros_like(ddk_ref)

        _, vjp = jax.vjp(_ssd_span, xs_ref[...].astype(F32), b_ref[...].astype(F32), c_ref[...].astype(F32),
                         dt_ref[...], st_ref[...], dtb_ref[...], al_ref[...], dk_ref[...])
        dxs, db, dc, ddt, dh, ddtb, dal, ddk = vjp((dy_ref[...].astype(F32), dh_ref[...]))
        dxs_ref[...] = dxs.astype(dxs_ref.dtype)
        db_ref[...] = db.astype(db_ref.dtype)
        dc_ref[...] = dc.astype(dc_ref.dtype)
        ddt_ref[...] = ddt
        dh_ref[...] = dh
        ddtb_ref[...] += ddtb
        dal_ref[...] += dal
        ddk_ref[...] += ddk

    nb = N_GROUPS * N_STATE
    bspec = pl.BlockSpec((rows, N_STATE), lambda g, s: (nsteps - 1 - s, g))
    return pl.pallas_call(
        body, name="ssd_bwd", grid=(N_GROUPS, nsteps),
        in_specs=[xs, bm, cm, dt, par, par, par, st, xs],
        out_specs=[xs, bspec, bspec, dt, par, par, par],
        out_shape=[jax.ShapeDtypeStruct((nrows, D_INNER), ACT_DTYPE), jax.ShapeDtypeStruct((nrows, nb), ACT_DTYPE),
                   jax.ShapeDtypeStruct((nrows, nb), ACT_DTYPE), jax.ShapeDtypeStruct((nrows, DT_W), F32),
                   jax.ShapeDtypeStruct((1, DT_W), F32), jax.ShapeDtypeStruct((1, DT_W), F32),
                   jax.ShapeDtypeStruct((1, DT_W), F32)],
        scratch_shapes=[pltpu.VMEM((_XS_GW, N_STATE), F32)],
        compiler_params=pltpu.CompilerParams(dimension_semantics=("arbitrary", "arbitrary")),
    )(act, act, act, dtr, dtb, alog, dsk, states, dy)


def _add_epilogue(acc, r):
    return r + acc


def _rms_and_skip(x, g):
    return _rms(x, g), x


def _forward_backward(x, p, tgt, w, late_weights=None, after=()):
    s = x.shape[0]
    act_t, f32 = ACT_DTYPE, F32
    mm = functools.partial(_matmul)
    h, proj, dtr = _rows_matmul("proj", _f_rms, [(x, D_MODEL, 0)], [w["norm_g"]], w["w_main"], out_dtype=act_t,
                                n=MAIN_W, k=D_MODEL, tm=1024, nrows=s, after=after, side=(w["w_dt"], f32))
    act = _conv_fwd(proj, w["conv_w"], w["conv_b"], tm=512, nrows=s)
    y, states = _ssd_fwd(act, dtr, w["dt_bias"], w["a_log"], w["d_skip"], nrows=s)
    if late_weights is not None:
        w = {**w, **late_weights(states)}
    a_pars = [w["ln_a_g"], w["ln_a_b"], w["w_s"], w["b_s"]]
    y_a, o_a = _rows_matmul("out_a", _f_branch_a, [(proj, UVZ_W, UVZ_CB)], a_pars, w["w_oa"], out_dtype=act_t,
                            n=D_MODEL, k=E_A, tm=256, nrows=s)
    gn_rows = [(y, D_INNER, 0), (proj, ZB_W, ZB_CB)]
    y_b, o_b = _rows_matmul("out_b", _f_gnorm, gn_rows, [w["ssm_norm_g"]], w["w_ob"], out_dtype=act_t,
                            n=D_MODEL, k=D_INNER, tm=512, nrows=s)
    mg_rows = [(proj, G_W, G_CB), (o_a, D_MODEL, 0), (o_b, D_MODEL, 0)]
    merged, x1 = _rows_matmul("out_proj", _f_merge, mg_rows, [], w["w_out"], out_dtype=f32, n=D_MODEL, k=D_MODEL,
                              tm=512, nrows=s, extras=(x,), epilogue=_add_epilogue)
    g = {}
    hp, dx2, dgp, dpe, g["final_g"], loss = _head(x1, p, tgt, w["ple_norm_g"], w["w_pg"], w["w_ple"], w["final_g"],
                                                   tm=256, nrows=s)
    g["w_pg"] = mm(hp, dgp, mode="tn", name="d_w_pg", out_dtype=f32, m=D_MODEL, n=D_MODEL, k=s)
    g["w_ple"] = mm(p, dpe, mode="tn", name="d_w_ple", out_dtype=f32, m=PLE_DIM, n=D_MODEL, k=s)
    dx1, g["ple_norm_g"] = _rows_vjp_call(
        "ple_norm_bwd", _rms_and_skip, [(x1, D_MODEL, 0)], [w["ple_norm_g"]], [(dx2, D_MODEL, 0)],
        [(f32, None)], tm=512, nrows=s, cot_mm=(dgp, w["w_pg"], None))
    g["w_out"] = mm(merged, dx1, mode="tn", name="d_w_out", out_dtype=f32, m=D_MODEL, n=D_MODEL, k=s)
    dproj, do_a, do_b = _rows_vjp_call(
        "merge_bwd", _f_merge, mg_rows, [], [],
        [(act_t, (None, MAIN_W, G_CB)), (act_t, None), (act_t, None)], tm=512, nrows=s, cot_mm=(dx1, w["w_out"], None))
    g["w_oa"] = mm(y_a, do_a, mode="tn", name="d_w_oa", out_dtype=f32, m=E_A, n=D_MODEL, k=s)
    g["w_ob"] = mm(y_b, do_b, mode="tn", name="d_w_ob", out_dtype=f32, m=D_INNER, n=D_MODEL, k=s)
    dy, dproj, g["ssm_norm_g"] = _rows_vjp_call(
        "gnorm_bwd", _f_gnorm, gn_rows, [w["ssm_norm_g"]], [],
        [(act_t, None), (act_t, (dproj, MAIN_W, ZB_CB))], tm=256, nrows=s, cot_mm=(do_b, w["w_ob"], None))
    dxs, dbm, dcm, ddtr, g["dt_bias"], g["a_log"], g["d_skip"] = _ssd_bwd(
        act, dtr, w["dt_bias"], w["a_log"], w["d_skip"], states, dy, nrows=s)
    dpre, g["conv_w"], g["conv_b"] = _conv_bwd_pre(proj, w["conv_w"], w["conv_b"], (dxs, dbm, dcm), tm=512, nrows=s)
    dproj = _conv_bwd_x(dpre, w["conv_w"], dproj, tm=512, nrows=s)
    dproj, g["ln_a_g"], g["ln_a_b"], g["w_s"], g["b_s"] = _rows_vjp_call(
        "branch_a_bwd", _f_branch_a, [(proj, UVZ_W, UVZ_CB)], a_pars, [],
        [(act_t, (dproj, MAIN_W, UVZ_CB))], tm=256, nrows=s, cot_mm=(do_a, w["w_oa"], None))
    g["w_main"] = mm(h, dproj, mode="tn", name="d_w_main", out_dtype=f32, m=D_MODEL, n=MAIN_W, k=s)
    g["w_dt"] = mm(h, ddtr, mode="tn", name="d_w_dt", out_dtype=f32, m=D_MODEL, n=DT_W, k=s)
    return loss, g, (dproj, ddtr, dx1)


def _input_grad(x, w, ctx, after=()):
    dproj, ddtr, dx1 = ctx
    s = x.shape[0]
    dh = _matmul(dproj, w["w_main"], mode="nt", name="d_h_main", out_dtype=F32, m=s, n=D_MODEL, k=MAIN_W, after=after)
    return _rows_vjp_call(
        "pre_norm_bwd", _rms_and_skip, [(x, D_MODEL, 0)], [w["norm_g"]], [(dx1, D_MODEL, 0)],
        [(F32, None)], tm=512, nrows=s, cot_mm=(ddtr, w["w_dt"], dh))


def _local_step(x, p, tgt, w):
    loss, g, ctx = _forward_backward(x, p, tgt, w)
    grad_x, g["norm_g"] = _input_grad(x, w, ctx)
    return loss, grad_x, g


_O_ZB = 3 * E_A
_O_XBC = _O_ZB + D_INNER
_O_DT = _O_XBC + CONV_DIM
_O_G = _O_DT + N_HEADS


def _heads_to_lanes(v):
    r = v.shape[0]
    v = v.reshape(r, N_GROUPS, HEADS_PER_GROUP)
    return jnp.pad(v, ((0, 0), (0, 0), (0, 128 - HEADS_PER_GROUP))).reshape(r, DT_W)


def _lanes_to_heads(v):
    r = v.shape[0]
    return v.reshape(r, N_GROUPS, 128)[:, :, :HEADS_PER_GROUP].reshape(r, N_HEADS)


def _block_cols(blocks, a, b):
    parts = []
    for k in range(N_CHIPS):
        lo, hi = max(a, k * W_IN_BLOCK), min(b, (k + 1) * W_IN_BLOCK)
        if lo < hi:
            parts.append(blocks[k][:, lo - k * W_IN_BLOCK:hi - k * W_IN_BLOCK])
    return parts


_W_IN_SEGMENTS = ((0, _O_ZB, "m", 0), (_O_ZB, _O_XBC, "m", UVZ_W + XBC_W), (_O_XBC, _O_DT, "m", UVZ_W),
                  (_O_DT, _O_G, "d", 0), (_O_G, N_IN, "m", MAIN_W - G_W))


def _w_in_grad_blocks(gm, gdt):
    blocks = []
    for k in range(N_CHIPS):
        a, b = k * W_IN_BLOCK, (k + 1) * W_IN_BLOCK
        parts = []
        for s, e, src, off in _W_IN_SEGMENTS:
            lo, hi = max(a, s), min(b, e)
            if lo < hi:
                parts.append((gm if src == "m" else gdt)[:, off + lo - s:off + hi - s])
        blocks.append(jnp.concatenate(parts, axis=1))
    return jnp.stack(blocks)


def _layout_weights(f, w_in_blocks=None):
    w = dict(f)
    if w_in_blocks is None:
        w_in = w.pop("w_in")
        w_in_blocks = jnp.stack([w_in[:, k * W_IN_BLOCK:(k + 1) * W_IN_BLOCK] for k in range(N_CHIPS)])
    cols = functools.partial(_block_cols, w_in_blocks)
    w["w_main"] = jnp.concatenate(cols(0, _O_ZB) + cols(_O_XBC, _O_DT) + cols(_O_ZB, _O_XBC) + cols(_O_G, N_IN), axis=1)
    w["w_dt"] = _heads_to_lanes(jnp.concatenate(cols(_O_DT, _O_G), axis=1))
    w["b_s"] = f["b_s"].reshape(G_A, CHUNK, 1)
    for n in ("dt_bias", "a_log", "d_skip"):
        w[n] = _heads_to_lanes(f[n])
    return w


def _natural_grads(g):
    out = dict(g)
    gm = out.pop("w_main")
    gdt = _lanes_to_heads(out.pop("w_dt"))
    out["w_in"] = jnp.concatenate(
        [gm[:, :UVZ_W], gm[:, UVZ_W + XBC_W:UVZ_W + XBC_W + ZB_W], gm[:, UVZ_W:UVZ_W + XBC_W], gdt, gm[:, MAIN_W - G_W:]],
        axis=1)
    out["b_s"] = g["b_s"].reshape(G_A, CHUNK)
    for n in ("dt_bias", "a_log", "d_skip"):
        out[n] = _lanes_to_heads(g[n])
    return out


def _place():
    return lax.axis_index("x"), lax.axis_index("y"), lax.axis_index("c")


def _other_chips(x, y):
    return [(1 - x, y), (x, 1 - y), (1 - x, 1 - y)]


def _rcopy(src, dst, ssem, rsem, dev):
    return pltpu.make_async_remote_copy(src_ref=src, dst_ref=dst, send_sem=ssem, recv_sem=rsem,
                                        device_id=dev, device_id_type=MESH)


def _half(ref_rows, half):
    hs = ref_rows // 2
    return pl.ds(pl.multiple_of(half * hs, 16), hs)


def _gather_weights(shards, conv_shard):
    nw = len(shards)

    def body(*refs):
        sh, cv = refs[:nw], refs[nw]
        out, cvo = refs[nw + 1:2 * nw + 1], refs[2 * nw + 1]
        ici_s, ici_r, fw_s, fw_r, own_s, own_r, cv_s, cv_r = refs[2 * nw + 2:]
        x, y, c = _place()
        me, sib, chips = 2 * x + y, (x, y, 1 - c), _other_chips(x, y)
        own = [_rcopy(sh[w], out[w].at[me], own_s.at[w], own_r.at[w], sib) for w in range(nw)]
        own.append(_rcopy(cv, cvo.at[me], own_s.at[nw], own_r.at[nw], sib))
        for cp in own:
            cp.start()
        sends = []
        for w in range(nw):
            mine = _half(sh[w].shape[0], c)
            for j, chip in enumerate(chips):
                sends.append(_rcopy(sh[w].at[mine], out[w].at[me, mine], ici_s.at[3 * w + j], ici_r.at[3 * w + j], (*chip, c)))
        for j, chip in enumerate(chips):
            sends.append(_rcopy(cv, cvo.at[me], cv_s.at[j], cv_r.at[j], (*chip, c)))
        for cp in sends:
            cp.start()
        for w in range(nw):
            mine = _half(sh[w].shape[0], c)
            for j, chip in enumerate(chips):
                slab = out[w].at[2 * chip[0] + chip[1], mine]
                _rcopy(slab, slab, ici_s.at[3 * w + j], ici_r.at[3 * w + j], (*chip, c)).wait_recv()
                fwd = _rcopy(slab, slab, fw_s.at[3 * w + j], fw_r.at[3 * w + j], sib)
                fwd.start()
                sends.append(fwd)
        for j, chip in enumerate(chips):
            blk = cvo.at[2 * chip[0] + chip[1]]
            _rcopy(blk, blk, cv_s.at[j], cv_r.at[j], (*chip, c)).wait_recv()
        for w in range(nw):
            theirs = _half(sh[w].shape[0], 1 - c)
            for j, chip in enumerate(chips):
                slab = out[w].at[2 * chip[0] + chip[1], theirs]
                _rcopy(slab, slab, fw_s.at[3 * w + j], fw_r.at[3 * w + j], sib).wait_recv()
        for cp in sends:
            cp.wait_send()
        for cp in own:
            cp.wait()

    dma = pltpu.SemaphoreType.DMA
    return pl.pallas_call(
        body, name="gather_weights",
        in_specs=[ANY] * (nw + 1), out_specs=[ANY] * (nw + 1),
        out_shape=[jax.ShapeDtypeStruct((N_CHIPS,) + s.shape, s.dtype) for s in shards]
        + [jax.ShapeDtypeStruct((N_CHIPS,) + conv_shard.shape, conv_shard.dtype)],
        scratch_shapes=[dma((3 * nw,)), dma((3 * nw,)), dma((3 * nw,)), dma((3 * nw,)), dma((nw + 1,)), dma((nw + 1,)),
                        dma((3,)), dma((3,))],
    )(*shards, conv_shard)


_HBM = pl.BlockSpec(memory_space=pltpu.HBM)
_SEM = pl.BlockSpec(memory_space=pltpu.SEMAPHORE)
_EFFECT = pltpu.SideEffectType.DATAFLOW_SIDE_EFFECTING


def _late_gather_copies(sh, out, s_sem, r_sem):
    x, y, c = _place()
    to = [(*chip, c) for chip in _other_chips(x, y)] + [(x, y, 1 - c)]
    return [_rcopy(sh[w], out[w].at[2 * x + y], s_sem.at[4 * w + j], r_sem.at[4 * w + j], dev)
            for w in range(len(sh)) for j, dev in enumerate(to)]


def _late_gather_start(shards):
    n = len(shards)
    lands = [lax.empty((N_CHIPS,) + a.shape, a.dtype) for a in shards]

    def body(*refs):
        for cp in _late_gather_copies(refs[:n], refs[n:2 * n], refs[2 * n], refs[2 * n + 1]):
            cp.start()
        refs[-1][...] = jnp.zeros_like(refs[-1])

    dma = pltpu.SemaphoreType.DMA
    hbm = [pltpu.with_memory_space_constraint(a, pltpu.HBM) for a in list(shards) + lands]
    out = pl.pallas_call(
        body, name="late_gather_start",
        out_shape=[dma((4 * n,)), dma((4 * n,))] + [pltpu.HBM(a.shape, a.dtype) for a in hbm]
        + [jax.ShapeDtypeStruct((8, 128), F32)],
        in_specs=[_HBM] * (2 * n), out_specs=[_SEM, _SEM] + [_HBM] * (2 * n) + [pl.BlockSpec(memory_space=pltpu.VMEM)],
        input_output_aliases={i: 2 + i for i in range(2 * n)},
        compiler_params=pltpu.CompilerParams(has_side_effects=_EFFECT),
    )(*hbm)
    return out[0], out[1], out[2:2 + n], out[2 + n:2 + 2 * n], out[-1]


def _late_gather_wait(s_sem, r_sem, srcs, lands, after):
    n = len(srcs)

    def body(*refs):
        for cp in _late_gather_copies(refs[:n], refs[n:2 * n], refs[2 * n], refs[2 * n + 1]):
            cp.wait_send()
            cp.wait_recv()

    out = pl.pallas_call(
        body, name="late_gather_wait",
        out_shape=[pltpu.HBM(a.shape, a.dtype) for a in list(srcs) + list(lands)],
        in_specs=[_HBM] * (2 * n) + [_SEM, _SEM, ANY], out_specs=[_HBM] * (2 * n),
        input_output_aliases={i: i for i in range(2 * n)},
        compiler_params=pltpu.CompilerParams(has_side_effects=_EFFECT),
    )(*srcs, *lands, s_sem, r_sem, after)
    return out[n:]


def _swap_with_sibling(arrs):
    n = len(arrs)

    def body(*refs):
        src, dst, s_sem, r_sem = refs[:n], refs[n:2 * n], refs[2 * n], refs[2 * n + 1]
        x, y, c = _place()
        cps = [_rcopy(src[i], dst[i], s_sem.at[i], r_sem.at[i], (x, y, 1 - c)) for i in range(n)]
        for cp in cps:
            cp.start()
        for cp in cps:
            cp.wait()

    dma = pltpu.SemaphoreType.DMA
    return pl.pallas_call(
        body, name="swap_with_sibling", in_specs=[ANY] * n, out_specs=[ANY] * n,
        out_shape=[jax.ShapeDtypeStruct(a.shape, a.dtype) for a in arrs], scratch_shapes=[dma((n,)), dma((n,))],
    )(*arrs)


def _scatter_copies(src, land, s_sem, r_sem):
    x, y, c = _place()
    return [_rcopy(src[i].at[2 * chip[0] + chip[1]], land[i].at[j], s_sem.at[3 * i + j], r_sem.at[3 * i + j], (*chip, c))
            for i in range(len(src)) for j, chip in enumerate(_other_chips(x, y))]


def _scatter_blocks_start(arrs):
    n = len(arrs)
    lands = [lax.empty((3,) + a.shape[1:], a.dtype) for a in arrs]

    def body(*refs):
        src, land, s_sem, r_sem, token = refs[:n], refs[n:2 * n], refs[2 * n], refs[2 * n + 1], refs[-1]
        for cp in _scatter_copies(src, land, s_sem, r_sem):
            cp.start()
        token[...] = jnp.zeros_like(token)

    dma = pltpu.SemaphoreType.DMA
    hbm = [pltpu.with_memory_space_constraint(a, pltpu.HBM) for a in list(arrs) + lands]
    out = pl.pallas_call(
        body, name="scatter_blocks_start",
        out_shape=[dma((3 * n,)), dma((3 * n,))] + [pltpu.HBM(a.shape, a.dtype) for a in hbm]
        + [jax.ShapeDtypeStruct((8, 128), F32)],
        in_specs=[_HBM] * (2 * n), out_specs=[_SEM, _SEM] + [_HBM] * (2 * n) + [pl.BlockSpec(memory_space=pltpu.VMEM)],
        input_output_aliases={i: 2 + i for i in range(2 * n)},
        compiler_params=pltpu.CompilerParams(has_side_effects=_EFFECT),
    )(*hbm)
    return out[0], out[1], out[2:2 + n], out[2 + n:2 + 2 * n], out[-1]


def _scatter_blocks_wait(s_sem, r_sem, srcs, lands, after):
    n = len(srcs)

    def body(*refs):
        src, land, s_sem, r_sem = refs[:n], refs[n:2 * n], refs[2 * n], refs[2 * n + 1]
        for cp in _scatter_copies(src, land, s_sem, r_sem):
            cp.wait_send()
            cp.wait_recv()

    out = pl.pallas_call(
        body, name="scatter_blocks_wait",
        out_shape=[pltpu.HBM(a.shape, a.dtype) for a in list(srcs) + list(lands)],
        in_specs=[_HBM] * (2 * n) + [_SEM, _SEM] + [ANY] * len(after), out_specs=[_HBM] * (2 * n),
        input_output_aliases={i: i for i in range(2 * n)},
        compiler_params=pltpu.CompilerParams(has_side_effects=_EFFECT),
    )(*srcs, *lands, s_sem, r_sem, *after)
    return out[:n], out[n:]


def _share_halves(arrs):
    n = len(arrs)

    def body(*refs):
        buf, s_sem, r_sem = refs[n:2 * n], refs[2 * n], refs[2 * n + 1]
        x, y, c = _place()
        cps = []
        for i in range(n):
            mine = buf[i].at[_half(buf[i].shape[0], c)]
            cps.append(_rcopy(mine, mine, s_sem.at[i], r_sem.at[i], (x, y, 1 - c)))
        for cp in cps:
            cp.start()
        for i in range(n):
            theirs = buf[i].at[_half(buf[i].shape[0], 1 - c)]
            _rcopy(theirs, theirs, s_sem.at[i], r_sem.at[i], (x, y, 1 - c)).wait_recv()
        for cp in cps:
            cp.wait_send()

    dma = pltpu.SemaphoreType.DMA
    return pl.pallas_call(
        body, name="share_halves", in_specs=[ANY] * n, out_specs=[ANY] * n,
        out_shape=[jax.ShapeDtypeStruct(a.shape, a.dtype) for a in arrs],
        input_output_aliases={i: i for i in range(n)}, scratch_shapes=[dma((n,)), dma((n,))],
    )(*arrs)


def _small_gather_copies(src, land, s_sem, r_sem):
    x, y, c = _place()
    cps = []
    for d in range(1, N_DEV):
        peer = ((1 - x) if d & 4 else x), ((1 - y) if d & 2 else y), ((1 - c) if d & 1 else c)
        cps.append(_rcopy(src, land.at[4 * x + 2 * y + c], s_sem.at[d - 1], r_sem.at[d - 1], peer))
    return cps


def _small_gather_start(packed):
    def body(src, land, s_sem, r_sem, _, __, token):
        for cp in _small_gather_copies(src, land, s_sem, r_sem):
            cp.start()
        token[...] = jnp.zeros_like(token)

    dma = pltpu.SemaphoreType.DMA
    hbm = [pltpu.with_memory_space_constraint(a, pltpu.HBM) for a in (packed, lax.empty((N_DEV,) + packed.shape, F32))]
    return pl.pallas_call(
        body, name="small_gather_start",
        out_shape=[dma((N_DEV - 1,)), dma((N_DEV - 1,))] + [pltpu.HBM(a.shape, a.dtype) for a in hbm]
        + [jax.ShapeDtypeStruct((8, 128), F32)],
        in_specs=[_HBM] * 2, out_specs=[_SEM, _SEM, _HBM, _HBM, pl.BlockSpec(memory_space=pltpu.VMEM)],
        input_output_aliases={0: 2, 1: 3}, compiler_params=pltpu.CompilerParams(has_side_effects=_EFFECT),
    )(*hbm)


def _small_gather_wait(s_sem, r_sem, src, land, after):
    def body(src, land, s_sem, r_sem, *_):
        for cp in _small_gather_copies(src, land, s_sem, r_sem):
            cp.wait_send()
            cp.wait_recv()

    return pl.pallas_call(
        body, name="small_gather_wait", out_shape=[pltpu.HBM(src.shape, src.dtype), pltpu.HBM(land.shape, land.dtype)],
        in_specs=[_HBM, _HBM, _SEM, _SEM, ANY], out_specs=[_HBM, _HBM], input_output_aliases={0: 0, 1: 1},
        compiler_params=pltpu.CompilerParams(has_side_effects=_EFFECT),
    )(src, land, s_sem, r_sem, after)


def _small_sum(own, land, dev_arr):
    def body(me_ref, own_ref, land_ref, o_ref):
        acc = jnp.zeros(o_ref.shape, F32)
        for d in range(N_DEV):
            acc = acc + jnp.where(me_ref[0] == d, own_ref[...], land_ref[d])
        o_ref[...] = acc

    return pl.pallas_call(
        body, name="small_sum", out_shape=jax.ShapeDtypeStruct(own.shape, F32),
        grid_spec=pltpu.PrefetchScalarGridSpec(
            num_scalar_prefetch=1, grid=(1,),
            in_specs=[pl.BlockSpec(own.shape, lambda i, m: (0, 0)), pl.BlockSpec(land.shape, lambda i, m: (0, 0, 0))],
            out_specs=pl.BlockSpec(own.shape, lambda i, m: (0, 0))),
    )(dev_arr, own, land)


def _row_tile(rows, cols):
    tr = max(8, min(rows, (1 << 20) // (4 * cols) // 8 * 8))
    while rows % tr:
        tr -= 8
    return tr


def _chip_sum(name, g5, recv, c_arr):
    nb, _, hs, cols = g5.shape
    tr = _row_tile(hs, cols)

    def body(_, a_ref, b_ref, o_ref):
        o_ref[...] = (a_ref[...] + b_ref[...].astype(F32)).astype(o_ref.dtype)

    blk = pl.BlockSpec((None, tr, cols), lambda b, i, c: (b, i, 0))
    return pl.pallas_call(
        body, name=name,
        grid_spec=pltpu.PrefetchScalarGridSpec(
            num_scalar_prefetch=1, grid=(nb, hs // tr),
            in_specs=[pl.BlockSpec((None, None, tr, cols), lambda b, i, c: (b, c[0], i, 0)), blk], out_specs=blk),
        out_shape=jax.ShapeDtypeStruct((nb, hs, cols), WIRE_DTYPE),
    )(c_arr, g5, recv)


def _final_sum(name, own, recv, place_arr):
    _, hs, cols = own.shape
    tr = _row_tile(hs, cols)
    nt = hs // tr

    def body(_, a_ref, r_ref, o_ref):
        o_ref[...] = ((a_ref[...].astype(F32) + r_ref[0].astype(F32)) + r_ref[1].astype(F32)) + r_ref[2].astype(F32)

    return pl.pallas_call(
        body, name=name,
        grid_spec=pltpu.PrefetchScalarGridSpec(
            num_scalar_prefetch=1, grid=(nt,),
            in_specs=[pl.BlockSpec((None, tr, cols), lambda i, m: (m[0], i, 0)),
                      pl.BlockSpec((3, tr, cols), lambda i, m: (0, i, 0))],
            out_specs=pl.BlockSpec((tr, cols), lambda i, m: (m[1] * nt + i, 0))),
        out_shape=jax.ShapeDtypeStruct((2 * hs, cols), F32),
    )(place_arr, own, recv)


def _adamw(w, g, m, v):
    m = ADAM_B1 * m + (1.0 - ADAM_B1) * g
    v = ADAM_B2 * v + (1.0 - ADAM_B2) * (g * g)
    m_hat = m / (1.0 - ADAM_B1 ** ADAM_STEP)
    v_hat = v / (1.0 - ADAM_B2 ** ADAM_STEP)
    return -ADAM_LR * (m_hat / (jnp.sqrt(v_hat) + ADAM_EPS) + ADAM_WD * w), m, v


def _adamw_call(name, w, g, m, v):
    rows, cols = w.shape
    tr = _row_tile(rows, cols)

    def body(w_ref, g_ref, m_ref, v_ref, d_ref, nm_ref, nv_ref, go_ref):
        g = g_ref[...]
        d_ref[...], nm_ref[...], nv_ref[...] = _adamw(w_ref[...], g, m_ref[...], v_ref[...])
        go_ref[...] = g

    blk = pl.BlockSpec((tr, cols), lambda i: (i, 0))
    return pl.pallas_call(
        body, name=name, grid=(rows // tr,), in_specs=[blk] * 4, out_specs=[blk] * 4,
        out_shape=[jax.ShapeDtypeStruct(w.shape, F32)] * 4,
        compiler_params=pltpu.CompilerParams(dimension_semantics=("parallel",)),
    )(w, g, m, v)


def _adamw_small(ws, gs, ms, vs):
    n = len(ws)

    def body(*refs):
        for i in range(n):
            w_ref, g_ref, m_ref, v_ref = (refs[k * n + i] for k in range(4))
            d, nm, nv = _adamw(w_ref[...], g_ref[...], m_ref[...], v_ref[...])
            refs[4 * n + i][...] = d
            refs[5 * n + i][...] = nm
            refs[6 * n + i][...] = nv

    out = pl.pallas_call(
        body, name="adamw_small", out_shape=[jax.ShapeDtypeStruct(a.shape, F32) for a in ws] * 3,
    )(*ws, *gs, *ms, *vs)
    return out[:n], out[n:2 * n], out[2 * n:]


_BIG = ("w_in", "w_oa", "w_ob", "w_out", "w_pg", "w_ple")
_SMALL = ("norm_g", "ln_a_g", "ln_a_b", "w_s", "b_s", "conv_w", "conv_b", "dt_bias", "a_log", "d_skip", "ssm_norm_g",
          "ple_norm_g", "final_g")
_WEIGHTS = ("norm_g", "w_in", "ln_a_g", "ln_a_b", "w_s", "b_s", "conv_w", "conv_b", "dt_bias", "a_log", "d_skip",
            "ssm_norm_g", "w_oa", "w_ob", "w_out", "ple_norm_g", "w_pg", "w_ple", "final_g")
_COL_SHARDED = ("w_in", "w_ple")
_PACK = 1024


def _blocks_to_full(col_sharded, blocks):
    if col_sharded:
        return jnp.concatenate([blocks[k] for k in range(N_CHIPS)], axis=1)
    return blocks.reshape(N_CHIPS * blocks.shape[1], blocks.shape[2])


def _full_to_blocks(col_sharded, full):
    if col_sharded:
        w = full.shape[1] // N_CHIPS
        return jnp.stack([full[:, k * w:(k + 1) * w] for k in range(N_CHIPS)])
    return full.reshape(N_CHIPS, full.shape[0] // N_CHIPS, full.shape[1])


def _two_d(n, a):
    if n == "w_s":
        return a.reshape(G_A * CHUNK, CHUNK)
    if n in ("b_s", "conv_w"):
        return a.reshape(a.shape[-2], a.shape[-1])
    return a.reshape(1, a.shape[-1])


def kernel(x, p, norm_g, w_in, ln_a_g, ln_a_b, w_s, b_s, conv_w, conv_b, dt_bias, a_log, d_skip, ssm_norm_g, w_oa, w_ob, w_out, ple_norm_g, w_pg, w_ple, final_g, loss_target, m_norm_g, m_w_in, m_ln_a_g, m_ln_a_b, m_w_s, m_b_s, m_conv_w, m_conv_b, m_dt_bias, m_a_log, m_d_skip, m_ssm_norm_g, m_w_oa, m_w_ob, m_w_out, m_ple_norm_g, m_w_pg, m_w_ple, m_final_g, v_norm_g, v_w_in, v_ln_a_g, v_ln_a_b, v_w_s, v_b_s, v_conv_w, v_conv_b, v_dt_bias, v_a_log, v_d_skip, v_ssm_norm_g, v_w_oa, v_w_ob, v_w_out, v_ple_norm_g, v_w_pg, v_w_ple, v_final_g):
    wt = dict(norm_g=norm_g, w_in=w_in, ln_a_g=ln_a_g, ln_a_b=ln_a_b, w_s=w_s, b_s=b_s, conv_w=conv_w, conv_b=conv_b,
              dt_bias=dt_bias, a_log=a_log, d_skip=d_skip, ssm_norm_g=ssm_norm_g, w_oa=w_oa, w_ob=w_ob, w_out=w_out,
              ple_norm_g=ple_norm_g, w_pg=w_pg, w_ple=w_ple, final_g=final_g)
    mom = dict(norm_g=m_norm_g, w_in=m_w_in, ln_a_g=m_ln_a_g, ln_a_b=m_ln_a_b, w_s=m_w_s, b_s=m_b_s, conv_w=m_conv_w,
               conv_b=m_conv_b, dt_bias=m_dt_bias, a_log=m_a_log, d_skip=m_d_skip, ssm_norm_g=m_ssm_norm_g, w_oa=m_w_oa,
               w_ob=m_w_ob, w_out=m_w_out, ple_norm_g=m_ple_norm_g, w_pg=m_w_pg, w_ple=m_w_ple, final_g=m_final_g)
    vel = dict(norm_g=v_norm_g, w_in=v_w_in, ln_a_g=v_ln_a_g, ln_a_b=v_ln_a_b, w_s=v_w_s, b_s=v_b_s, conv_w=v_conv_w,
               conv_b=v_conv_b, dt_bias=v_dt_bias, a_log=v_a_log, d_skip=v_d_skip, ssm_norm_g=v_ssm_norm_g, w_oa=v_w_oa,
               w_ob=v_w_ob, w_out=v_w_out, ple_norm_g=v_ple_norm_g, w_pg=v_w_pg, w_ple=v_w_ple, final_g=v_final_g)
    xi, yi, ci = _place()
    me = 2 * xi + yi
    c_arr = jnp.reshape(ci, (1,)).astype(jnp.int32)
    place_arr = jnp.stack([me, ci]).astype(jnp.int32)

    shard = {n: wt[n][0] for n in _BIG}
    wire = {n: shard[n].astype(WIRE_DTYPE) for n in _BIG}
    w_in_blocks, conv_blocks = _gather_weights([wire["w_in"]], conv_w[0])
    g_ssem, g_rsem, g_sent, g_lands, g_token = _late_gather_start([wire[n] for n in _BIG[1:]])
    full = {"conv_w": _blocks_to_full(True, conv_blocks)}
    for n in _SMALL:
        if n != "conv_w":
            full[n] = wt[n][0] if wt[n].ndim > 2 else wt[n].reshape(1, wt[n].shape[-1])

    def late_weights(after):
        blocks = _late_gather_wait(g_ssem, g_rsem, g_sent, g_lands, after)
        return {n: _blocks_to_full(n in _COL_SHARDED, b) for n, b in zip(_BIG[1:], blocks)}

    w = _layout_weights(full, w_in_blocks=w_in_blocks)
    loss_row, g, ctx = _forward_backward(x[0], p[0, 0], loss_target[0], w, late_weights, after=(g_token,))
    loss = lax.psum(loss_row[0, 0], ("x", "y", "c"))

    parts = {n: _full_to_blocks(n in _COL_SHARDED, g[n]) for n in _BIG[1:]}
    parts["w_main"] = g["w_main"][None]
    parts["w_dt"] = jnp.pad(_lanes_to_heads(g["w_dt"]), ((0, 0), (0, 128 - N_HEADS)))[None]
    names = ("w_main", "w_dt") + _BIG[1:]
    g5 = {n: parts[n].reshape(parts[n].shape[0], 2, parts[n].shape[1] // 2, parts[n].shape[2]) for n in names}
    to_sibling = [lax.dynamic_index_in_dim(g5[n], 1 - ci, axis=1, keepdims=False).astype(WIRE_DTYPE) for n in names]
    from_sibling = _swap_with_sibling(to_sibling)
    chip = {n: _chip_sum("chip_sum_" + n, g5[n], r, c_arr) for n, r in zip(names, from_sibling)}
    chip["w_in"] = _w_in_grad_blocks(chip["w_main"][0], chip["w_dt"][0])
    chip_wire = [chip[n] for n in _BIG]
    s_sem, r_sem, sent, lands, token = _scatter_blocks_start(chip_wire)
    grad_x, g["norm_g"] = _input_grad(x[0], w, ctx, after=(token,))
    g = _natural_grads(g)

    pieces = [_two_d(n, g[n]).reshape(-1) for n in _SMALL]
    sizes = [v.shape[0] for v in pieces]
    padded = [-(-s // _PACK) * _PACK for s in sizes]
    packed = jnp.concatenate([jnp.pad(v, (0, ps - s)) for v, s, ps in zip(pieces, sizes, padded)]).reshape(-1, 128)
    a_ssem, a_rsem, a_src, a_land, a_token = _small_gather_start(packed)

    sent, from_chips = _scatter_blocks_wait(s_sem, r_sem, sent, lands, (grad_x, a_token))
    halves = [_final_sum("final_sum_" + n, a, r, place_arr) for n, a, r in zip(_BIG, sent, from_chips)]
    grads = dict(zip(_BIG, _share_halves(halves)))
    delta, new_m, new_v = {}, {}, {}
    for n in _BIG:
        delta[n], new_m[n], new_v[n], grads[n] = _adamw_call("adamw_" + n, shard[n], grads[n], mom[n][0], vel[n][0])

    a_src, a_land = _small_gather_wait(a_ssem, a_rsem, a_src, a_land, delta["w_in"])
    summed = _small_sum(a_src, a_land, jnp.reshape(4 * xi + 2 * yi + ci, (1,)).astype(jnp.int32)).reshape(-1)
    off = 0
    for n, s, ps in zip(_SMALL, sizes, padded):
        grads[n] = summed[off:off + s].reshape(_two_d(n, g[n]).shape)
        off += ps
    grads["conv_w"] = lax.dynamic_slice_in_dim(grads["conv_w"], me * (CONV_DIM // N_CHIPS), CONV_DIM // N_CHIPS, axis=1)
    small = _adamw_small([_two_d(n, wt[n]) for n in _SMALL], [grads[n] for n in _SMALL],
                         [_two_d(n, mom[n]) for n in _SMALL], [_two_d(n, vel[n]) for n in _SMALL])
    for i, n in enumerate(_SMALL):
        delta[n], new_m[n], new_v[n] = small[0][i], small[1][i], small[2][i]

    def shaped(d):
        return [d[n].reshape(wt[n].shape) for n in _WEIGHTS]

    return (loss, grad_x[None], *shaped(grads), *shaped(delta), *shaped(new_m), *shaped(new_v))
```

```python
import functools

import jax
import jax.numpy as jnp
from jax import lax
from jax.experimental import pallas as pl
from jax.experimental.pallas import tpu as pltpu

F32 = jnp.float32
MXU_DTYPE = jnp.bfloat16
ACT_DTYPE = jnp.bfloat16
WIRE_DTYPE = jnp.bfloat16

D_MODEL = 1024
PLE_DIM = 256
CHUNK = 128
EPS = 1e-6
E_A = D_MODEL
G_A = 4
D_INNER = 2 * D_MODEL
HEAD_DIM = 64
N_HEADS = D_INNER // HEAD_DIM
N_STATE = 128
N_GROUPS = 4
HEADS_PER_GROUP = N_HEADS // N_GROUPS
PAIRS_PER_GROUP = HEADS_PER_GROUP // 2
CONV_K = 4
CONV_DIM = D_INNER + 2 * N_GROUPS * N_STATE
N_IN = 3 * E_A + D_INNER + CONV_DIM + N_HEADS + 2 * D_MODEL
N_CHIPS = 4
N_DEV = 8
W_IN_BLOCK = N_IN // N_CHIPS

UVZ_W, XBC_W, ZB_W, G_W = 3 * E_A, CONV_DIM, D_INNER, 2 * D_MODEL
MAIN_W = UVZ_W + XBC_W + ZB_W + G_W
UVZ_CB, XBC_CB, ZB_CB, G_CB = 0, 1, 3, 4
DT_W = N_GROUPS * 128

ADAM_LR, ADAM_B1, ADAM_B2, ADAM_EPS, ADAM_WD, ADAM_STEP = 0.001, 0.9, 0.999, 1e-08, 0.01, 10

MESH = pl.DeviceIdType.MESH
ANY = pl.BlockSpec(memory_space=pl.ANY)


def _mxu(v):
    return v.astype(MXU_DTYPE)


def _dot(a, b, dims=(((1,), (0,)), ((), ()))):
    return lax.dot_general(_mxu(a), _mxu(b), dims, preferred_element_type=F32)


MM_TILE = 1024
MM_VMEM_BUDGET = 46 << 20


def _mm_tk(m, n, k, tm, tn, a_bytes, b_bytes, out_bytes, extra_bytes):
    for parts in range(1, k // 128 + 1):
        if k % parts or (k // parts) % 128 and parts > 1:
            continue
        tk = k // parts
        need = 2 * tk * (tm * a_bytes + tn * b_bytes) + 2 * tm * tn * (out_bytes + extra_bytes) + (tm * tn * 4 if parts > 1 else 0)
        if need <= MM_VMEM_BUDGET:
            return tk
    return 128


def _matmul(a, b, *, mode, name, out_dtype, m, n, k, tm=MM_TILE, tn=MM_TILE, tk=None, a_off=0, b_off=0,
            extras=(), epilogue=None, after=()):
    tm, tn = min(tm, m), min(tn, n)
    if tk is None:
        tk = _mm_tk(m, n, k, tm, tn, a.dtype.itemsize, b.dtype.itemsize, jnp.dtype(out_dtype).itemsize,
                    sum(e.dtype.itemsize for e in extras))
    tk = min(tk, k)
    assert m % tm == 0 and n % tn == 0 and k % tk == 0, (name, m, n, k, tm, tn, tk)
    nk = k // tk
    if mode == "nn":
        assert a_off % tk == 0 and b_off % tn == 0
        a_spec = pl.BlockSpec((tm, tk), lambda i, j, kk: (i, kk + a_off // tk))
        b_spec = pl.BlockSpec((tk, tn), lambda i, j, kk: (kk, j + b_off // tn))
        dims = (((1,), (0,)), ((), ()))
    elif mode == "nt":
        a_spec = pl.BlockSpec((tm, tk), lambda i, j, kk: (i, kk))
        b_spec = pl.BlockSpec((tn, tk), lambda i, j, kk: (j, kk))
        dims = (((1,), (1,)), ((), ()))
    else:
        assert a_off % tm == 0 and b_off % tn == 0
        a_spec = pl.BlockSpec((tk, tm), lambda i, j, kk: (kk, i + a_off // tm))
        b_spec = pl.BlockSpec((tk, tn), lambda i, j, kk: (kk, j + b_off // tn))
        dims = (((0,), (0,)), ((), ()))
    ne = len(extras)

    def finish(acc, extra_refs, o_ref):
        res = acc if epilogue is None else epilogue(acc, *[e[...] for e in extra_refs])
        o_ref[...] = res.astype(o_ref.dtype)

    def body(a_ref, b_ref, *rest):
        extra_refs, o_ref = rest[:ne], rest[ne + len(after)]
        part = _dot(a_ref[...], b_ref[...], dims)
        if nk == 1:
            finish(part, extra_refs, o_ref)
            return
        acc_ref = rest[ne + len(after) + 1]
        kk = pl.program_id(2)

        @pl.when(kk == 0)
        def _():
            acc_ref[...] = part

        @pl.when(kk > 0)
        def _():
            acc_ref[...] += part

        @pl.when(kk == nk - 1)
        def _():
            finish(acc_ref[...], extra_refs, o_ref)

    o_spec = pl.BlockSpec((tm, tn), lambda i, j, kk: (i, j))
    return pl.pallas_call(
        body, name=name, grid=(m // tm, n // tn, nk),
        in_specs=[a_spec, b_spec] + [o_spec] * ne + [ANY] * len(after), out_specs=o_spec,
        out_shape=jax.ShapeDtypeStruct((m, n), out_dtype),
        scratch_shapes=[pltpu.VMEM((tm, tn), F32)] if nk > 1 else [],
        compiler_params=pltpu.CompilerParams(dimension_semantics=("parallel", "parallel", "arbitrary")),
    )(a, b, *extras, *after)


def _rows_matmul(name, f, rows, pars, b, *, out_dtype, n, k, tm, nrows, tn=MM_TILE, extras=(), epilogue=None, after=(),
                 side=None):
    tm, tn = min(tm, nrows), min(tn, n)
    assert nrows % tm == 0 and n % tn == 0, (name, nrows, n, tm, tn)
    nr, npar, ne, nj = len(rows), len(pars), len(extras), n // tn
    ns = 0 if side is None else 1
    n_in = nr + npar + 1 + ne + ns + len(after)

    def body(*refs):
        row_refs, par_refs, b_ref = refs[:nr], refs[nr:nr + npar], refs[nr + npar]
        extra_refs = refs[nr + npar + 1:nr + npar + 1 + ne]
        a_ref, o_ref = refs[n_in], refs[n_in + 1]

        def make_a():
            a = f(*[r[...].astype(F32) for r in row_refs], *[p[...] for p in par_refs])[0]
            a_ref[...] = a.astype(a_ref.dtype)
            if ns:
                refs[n_in + 2][...] = _dot(a_ref[...], refs[nr + npar + 1 + ne][...]).astype(refs[n_in + 2].dtype)

        if nj == 1:
            make_a()
        else:
            pl.when(pl.program_id(1) == 0)(make_a)
        res = _dot(a_ref[...], b_ref[...])
        if epilogue is not None:
            res = epilogue(res, *[e[...] for e in extra_refs])
        o_ref[...] = res.astype(o_ref.dtype)

    o_spec = pl.BlockSpec((tm, tn), lambda i, j: (i, j))
    side_in = [] if side is None else [pl.BlockSpec(tuple(side[0].shape), lambda i, j: (0, 0))]
    side_out = [] if side is None else [pl.BlockSpec((tm, side[0].shape[1]), lambda i, j: (i, 0))]
    side_shape = [] if side is None else [jax.ShapeDtypeStruct((nrows, side[0].shape[1]), side[1])]
    return pl.pallas_call(
        body, name=name, grid=(nrows // tm, nj),
        in_specs=[pl.BlockSpec((tm, w), lambda i, j, cb=cb: (i, cb)) for _, w, cb in rows]
        + [pl.BlockSpec(tuple(p.shape), lambda i, j, nd=p.ndim: (0,) * nd) for p in pars]
        + [pl.BlockSpec((k, tn), lambda i, j: (0, j))] + [o_spec] * ne + side_in + [ANY] * len(after),
        out_specs=[pl.BlockSpec((tm, k), lambda i, j: (i, 0)), o_spec] + side_out,
        out_shape=[jax.ShapeDtypeStruct((nrows, k), ACT_DTYPE), jax.ShapeDtypeStruct((nrows, n), out_dtype)] + side_shape,
        compiler_params=pltpu.CompilerParams(dimension_semantics=("parallel", "arbitrary")),
    )(*[r[0] for r in rows], *pars, b, *extras, *([] if side is None else [side[0]]), *after)


def _row_spec(tm, width, cb):
    return pl.BlockSpec((tm, width), lambda i: (i, cb))


def _whole_spec(shape):
    nd = len(shape)
    return pl.BlockSpec(tuple(shape), lambda i: (0,) * nd)


def _rows_call(name, f, rows, pars, outs, *, tm, nrows):
    tm = min(tm, nrows)
    nr, npar = len(rows), len(pars)

    def body(*refs):
        rv = [r[...].astype(F32) for r in refs[:nr]]
        pv = [p[...] for p in refs[nr:nr + npar]]
        res = f(*rv, *pv)
        for o_ref, r in zip(refs[nr + npar:], res):
            o_ref[...] = r.astype(o_ref.dtype)

    return pl.pallas_call(
        body, name=name, grid=(nrows // tm,),
        in_specs=[_row_spec(tm, w, cb) for _, w, cb in rows] + [_whole_spec(p.shape) for p in pars],
        out_specs=[_row_spec(tm, w, 0) for w, _ in outs],
        out_shape=[jax.ShapeDtypeStruct((nrows, w), dt) for w, dt in outs],
        compiler_params=pltpu.CompilerParams(dimension_semantics=("parallel",)),
    )(*[r[0] for r in rows], *pars)


def _rows_vjp_call(name, f, rows, pars, cots, drows, *, tm, nrows, cot_mm=None):
    tm = min(tm, nrows)
    nr, npar, nc = len(rows), len(pars), len(cots)
    mm_args, mm_specs = [], []
    if cot_mm is not None:
        mm_a, mm_b, mm_add = cot_mm
        mm_args = [mm_a, mm_b] + ([] if mm_add is None else [mm_add])
        mm_specs = [_row_spec(tm, mm_a.shape[1], 0), _whole_spec(mm_b.shape)]
        mm_specs += [] if mm_add is None else [_row_spec(tm, mm_b.shape[0], 0)]
    alias_bufs, aliases = [], {}
    out_shape, out_specs = [], []
    for (arr, w, cb), d in zip(rows, drows):
        if d is None:
            continue
        dt, into = d
        if into is None:
            out_shape.append(jax.ShapeDtypeStruct((nrows, w), dt))
            out_specs.append(_row_spec(tm, w, 0))
        else:
            buf, total, ocb = into
            if buf is not None:
                aliases[nr + npar + nc + len(alias_bufs)] = len(out_shape)
                alias_bufs.append(buf)
            out_shape.append(jax.ShapeDtypeStruct((nrows, total), dt))
            out_specs.append(_row_spec(tm, w, ocb))
    n_drow = len(out_shape)
    for p in pars:
        out_shape.append(jax.ShapeDtypeStruct(p.shape, F32))
        out_specs.append(_whole_spec(p.shape))
    na = len(alias_bufs)

    def body(*refs):
        rv = [r[...].astype(F32) for r in refs[:nr]]
        pv = [p[...] for p in refs[nr:nr + npar]]
        cv = tuple(c[...].astype(F32) for c in refs[nr + npar:nr + npar + nc])
        o_refs = refs[nr + npar + nc + na + len(mm_args):]
        if mm_args:
            mm_refs = refs[nr + npar + nc + na:nr + npar + nc + na + len(mm_args)]
            c0 = _dot(mm_refs[0][...], mm_refs[1][...], (((1,), (1,)), ((), ())))
            if len(mm_refs) == 3:
                c0 = c0 + mm_refs[2][...].astype(F32)
            cv = (c0,) + cv
        _, vjp = jax.vjp(f, *rv, *pv)
        g = vjp(cv)
        oi = 0
        for ri, d in enumerate(drows):
            if d is not None:
                o_refs[oi][...] = g[ri].astype(o_refs[oi].dtype)
                oi += 1
        first = pl.program_id(0) == 0
        for pi in range(npar):
            acc = o_refs[n_drow + pi]

            @pl.when(first)
            def _(acc=acc):
                acc[...] = jnp.zeros_like(acc)

            acc[...] += g[nr + pi]

    return pl.pallas_call(
        body, name=name, grid=(nrows // tm,),
        in_specs=[_row_spec(tm, w, cb) for _, w, cb in rows] + [_whole_spec(p.shape) for p in pars]
        + [_row_spec(tm, w, cb) for _, w, cb in cots] + [ANY] * na + mm_specs,
        out_specs=out_specs, out_shape=out_shape, input_output_aliases=aliases,
        compiler_params=pltpu.CompilerParams(dimension_semantics=("arbitrary",)),
    )(*[r[0] for r in rows], *pars, *[c[0] for c in cots], *alias_bufs, *mm_args)


def _rms(x, g):
    return x * lax.rsqrt(jnp.mean(x * x, axis=-1, keepdims=True) + EPS) * g


def _f_rms(x, g):
    return (_rms(x, g),)


def _tril_mask():
    return lax.broadcasted_iota(jnp.int32, (CHUNK, CHUNK), 0) >= lax.broadcasted_iota(jnp.int32, (CHUNK, CHUNK), 1)


def _f_branch_a(uvz, ln_g, ln_b, w_s, b_s):
    u = jax.nn.gelu(uvz[:, :E_A])
    v = jax.nn.gelu(uvz[:, E_A:2 * E_A])
    z = uvz[:, 2 * E_A:]
    xc = v - jnp.mean(v, axis=-1, keepdims=True)
    vn = xc * lax.rsqrt(jnp.mean(xc * xc, axis=-1, keepdims=True) + EPS) * ln_g + ln_b
    mask = _tril_mask()
    ws = [jnp.where(mask, w_s[g], 0.0) for g in range(G_A)]
    gw = E_A // G_A
    rows = []
    for c in range(uvz.shape[0] // CHUNK):
        vc = vn[c * CHUNK:(c + 1) * CHUNK]
        rows.append(jnp.concatenate([_dot(ws[g], vc[:, g * gw:(g + 1) * gw]) + b_s[g] for g in range(G_A)], axis=1))
    sv = rows[0] if len(rows) == 1 else jnp.concatenate(rows, axis=0)
    return (u * sv * jax.nn.silu(z),)


def _f_gnorm(y, zb, g):
    yz = y * jax.nn.silu(zb)
    gw = D_INNER // N_GROUPS
    parts = []
    for i in range(N_GROUPS):
        s = yz[:, i * gw:(i + 1) * gw]
        parts.append(s * lax.rsqrt(jnp.mean(s * s, axis=-1, keepdims=True) + EPS))
    return (jnp.concatenate(parts, axis=1) * g,)


def _f_merge(g2, oa, ob):
    return (jax.nn.sigmoid(g2[:, :D_MODEL]) * oa + jax.nn.sigmoid(g2[:, D_MODEL:]) * ob,)


def _f_loss(x1, gp, pe, tgt, fg):
    x2 = x1 + jax.nn.sigmoid(gp) * pe
    err = _rms(x2, fg) - tgt
    return 0.5 * jnp.sum(jnp.mean(err * err, axis=-1))


def _head(x1, p, tgt, ple_g, w_pg, w_ple, fg, *, tm, nrows):
    tm = min(tm, nrows)

    def body(x1_ref, p_ref, t_ref, pg_ref, wpg_ref, wple_ref, fg_ref, hp_ref, dx_ref, dgp_ref, dpe_ref, dfg_ref, loss_ref):
        x1 = x1_ref[...]
        hp_ref[...] = _rms(x1, pg_ref[...]).astype(hp_ref.dtype)
        gp = _dot(hp_ref[...], wpg_ref[...])
        pe = _dot(p_ref[...], wple_ref[...])
        loss, vjp = jax.vjp(_f_loss, x1, gp, pe, t_ref[...], fg_ref[...])
        dx, dgp, dpe, _, dfg = vjp(jnp.ones((), F32))
        dx_ref[...] = dx
        dgp_ref[...] = dgp.astype(dgp_ref.dtype)
        dpe_ref[...] = dpe.astype(dpe_ref.dtype)

        @pl.when(pl.program_id(0) == 0)
        def _():
            dfg_ref[...] = jnp.zeros_like(dfg_ref)
            loss_ref[...] = jnp.zeros_like(loss_ref)

        dfg_ref[...] += dfg
        loss_ref[...] += jnp.full(loss_ref.shape, loss, F32)

    row = _row_spec(tm, D_MODEL, 0)
    act = jax.ShapeDtypeStruct((nrows, D_MODEL), ACT_DTYPE)
    return pl.pallas_call(
        body, name="head", grid=(nrows // tm,),
        in_specs=[row, _row_spec(tm, PLE_DIM, 0), row, _whole_spec((1, D_MODEL)), _whole_spec(w_pg.shape),
                  _whole_spec(w_ple.shape), _whole_spec((1, D_MODEL))],
        out_specs=[row, row, row, row, _whole_spec((1, D_MODEL)), _whole_spec((1, 128))],
        out_shape=[act, jax.ShapeDtypeStruct((nrows, D_MODEL), F32), act, act, jax.ShapeDtypeStruct((1, D_MODEL), F32),
                   jax.ShapeDtypeStruct((1, 128), F32)],
        compiler_params=pltpu.CompilerParams(dimension_semantics=("arbitrary",)),
    )(x1, p, tgt, ple_g, w_pg, w_ple, fg)


def _shift_rows(cur, edge, j, up):
    tm = cur.shape[0]
    row = lax.broadcasted_iota(jnp.int32, cur.shape, 0)
    if up:
        sh = pltpu.roll(cur, tm - j, 0)
        e = jnp.tile(pltpu.roll(edge, 8 - j, 0), (tm // 8, 1))
        return jnp.where(row >= tm - j, e, sh)
    sh = pltpu.roll(cur, j, 0)
    e = jnp.tile(pltpu.roll(edge, j, 0), (tm // 8, 1))
    return jnp.where(row < j, e, sh)


def _conv_pre(cur, prev, w, b):
    acc = cur * w[CONV_K - 1:CONV_K] + b
    taps = [cur]
    for j in range(1, CONV_K):
        s = _shift_rows(cur, prev, j, up=False)
        taps.append(s)
        acc = acc + s * w[CONV_K - 1 - j:CONV_K - j]
    return acc, taps


def _halo_specs(tm, nrows, cb, before):
    nb = tm // 8
    last = nrows // 8 - 1
    if before:
        return pl.BlockSpec((8, XBC_W), lambda i: (jnp.maximum(i * nb - 1, 0), cb))
    return pl.BlockSpec((8, XBC_W), lambda i: (jnp.minimum((i + 1) * nb, last), cb))


def _conv_fwd(proj, conv_w, conv_b, *, tm, nrows):
    tm = min(tm, nrows)

    def body(cur_ref, prev_ref, w_ref, b_ref, o_ref):
        prev = jnp.where(pl.program_id(0) == 0, 0.0, prev_ref[...].astype(F32))
        pre, _ = _conv_pre(cur_ref[...].astype(F32), prev, w_ref[...], b_ref[...])
        o_ref[...] = jax.nn.silu(pre).astype(o_ref.dtype)

    return pl.pallas_call(
        body, name="conv_fwd", grid=(nrows // tm,),
        in_specs=[_row_spec(tm, XBC_W, XBC_CB), _halo_specs(tm, nrows, XBC_CB, True),
                  _whole_spec((CONV_K, XBC_W)), _whole_spec((1, XBC_W))],
        out_specs=_row_spec(tm, XBC_W, 0), out_shape=jax.ShapeDtypeStruct((nrows, XBC_W), ACT_DTYPE),
        compiler_params=pltpu.CompilerParams(dimension_semantics=("parallel",)),
    )(proj, proj, conv_w, conv_b)


def _conv_bwd_pre(proj, conv_w, conv_b, dact, *, tm, nrows):
    tm = min(tm, nrows)
    nb = N_GROUPS * N_STATE

    def body(cur_ref, prev_ref, w_ref, b_ref, dxs_ref, dbm_ref, dcm_ref, dpre_ref, dw_ref, db_ref):
        prev = jnp.where(pl.program_id(0) == 0, 0.0, prev_ref[...].astype(F32))
        pre, taps = _conv_pre(cur_ref[...].astype(F32), prev, w_ref[...], b_ref[...])
        sg = jax.nn.sigmoid(pre)
        dy = jnp.concatenate([dxs_ref[...], dbm_ref[...], dcm_ref[...]], axis=1).astype(F32)
        dpre = dy * sg * (1.0 + pre * (1.0 - sg))
        dpre_ref[...] = dpre.astype(dpre_ref.dtype)

        @pl.when(pl.program_id(0) == 0)
        def _():
            dw_ref[...] = jnp.zeros_like(dw_ref)
            db_ref[...] = jnp.zeros_like(db_ref)

        db_ref[...] += jnp.sum(dpre, axis=0, keepdims=True)
        for j in range(CONV_K):
            k = CONV_K - 1 - j
            dw_ref[k:k + 1, :] += jnp.sum(dpre * taps[j], axis=0, keepdims=True)

    return pl.pallas_call(
        body, name="conv_bwd_pre", grid=(nrows // tm,),
        in_specs=[_row_spec(tm, XBC_W, XBC_CB), _halo_specs(tm, nrows, XBC_CB, True),
                  _whole_spec((CONV_K, XBC_W)), _whole_spec((1, XBC_W)),
                  _row_spec(tm, D_INNER, 0), _row_spec(tm, nb, 0), _row_spec(tm, nb, 0)],
        out_specs=[_row_spec(tm, XBC_W, 0), _whole_spec((CONV_K, XBC_W)), _whole_spec((1, XBC_W))],
        out_shape=[jax.ShapeDtypeStruct((nrows, XBC_W), ACT_DTYPE), jax.ShapeDtypeStruct((CONV_K, XBC_W), F32),
                   jax.ShapeDtypeStruct((1, XBC_W), F32)],
        compiler_params=pltpu.CompilerParams(dimension_semantics=("arbitrary",)),
    )(proj, proj, conv_w, conv_b, *dact)


def _conv_bwd_x(dpre, conv_w, dproj, *, tm, nrows):
    tm = min(tm, nrows)
    ntiles = nrows // tm

    def body(cur_ref, nxt_ref, w_ref, _, o_ref):
        cur = cur_ref[...].astype(F32)
        nxt = jnp.where(pl.program_id(0) == ntiles - 1, 0.0, nxt_ref[...].astype(F32))
        w = w_ref[...]
        acc = cur * w[CONV_K - 1:CONV_K]
        for j in range(1, CONV_K):
            acc = acc + _shift_rows(cur, nxt, j, up=True) * w[CONV_K - 1 - j:CONV_K - j]
        o_ref[...] = acc.astype(o_ref.dtype)

    return pl.pallas_call(
        body, name="conv_bwd_x", grid=(ntiles,),
        in_specs=[_row_spec(tm, XBC_W, 0), _halo_specs(tm, nrows, 0, False), _whole_spec((CONV_K, XBC_W)), ANY],
        out_specs=_row_spec(tm, XBC_W, XBC_CB), out_shape=jax.ShapeDtypeStruct(dproj.shape, dproj.dtype),
        input_output_aliases={3: 0},
        compiler_params=pltpu.CompilerParams(dimension_semantics=("parallel",)),
    )(dpre, dpre, conv_w, dproj)


SSD_SPAN = 4
_XS_GW = D_INNER // N_GROUPS
_NT = (((1,), (1,)), ((), ()))
_TN = (((0,), (0,)), ((), ()))


def _bf16_terms(x, terms):
    parts, rest = [], x
    for _ in range(terms):
        part = rest.astype(jnp.bfloat16)
        parts.append(part)
        rest = rest - part.astype(F32)
    return parts


def _head_lane_matrix():
    return (lax.broadcasted_iota(jnp.int32, (128, _XS_GW), 0)
            == lax.broadcasted_iota(jnp.int32, (128, _XS_GW), 1) // HEAD_DIM).astype(jnp.bfloat16)


@functools.partial(jax.custom_vjp, nondiff_argnums=(1,))
def _head_lanes(cols, terms):
    e = _head_lane_matrix()
    return sum(jnp.dot(t, e, preferred_element_type=F32) for t in _bf16_terms(cols, terms))


def _head_lanes_fwd(cols, terms):
    return _head_lanes(cols, terms), None


def _head_lanes_bwd(terms, _, g):
    e = _head_lane_matrix()
    return (sum(lax.dot_general(t, e, _NT, preferred_element_type=F32) for t in _bf16_terms(g, 2)),)


_head_lanes.defvjp(_head_lanes_fwd, _head_lanes_bwd)


def _ssd_chunk(k, xs, bm, cm, dtr, hprev, dtb, alog, dsk):
    causal, tri, lo = k
    dt = jax.nn.softplus(dtr + dtb)
    da = dt * (-jnp.exp(alog))
    cs = jnp.dot(tri, da, precision=lax.Precision.HIGHEST, preferred_element_type=F32)
    cst = cs.T
    cs_l = _head_lanes(cs, 3)
    xdt = xs * _head_lanes(dt, 2)
    cb = _dot(cm, bm, _NT)
    yd = []
    for q in range(PAIRS_PER_GROUP):
        xq = xdt[:, 128 * q:128 * (q + 1)]
        y2 = [_dot(cb * jnp.exp(jnp.where(causal, cs[:, h:h + 1] - cst[h:h + 1, :], -jnp.inf)), xq)
              for h in (2 * q, 2 * q + 1)]
        yd.append(jnp.where(lo, y2[0], y2[1]))
    y_off = jnp.exp(cs_l) * _dot(cm, hprev, _NT)
    st = _dot(xdt * jnp.exp(cs_l[CHUNK - 1:CHUNK, :] - cs_l), bm, _TN)
    cdec = jnp.exp(cs[CHUNK - 1:CHUNK, :])
    cd_rows = jnp.concatenate(
        [jnp.broadcast_to(cdec[:, h:h + 1], (HEAD_DIM, N_STATE)) for h in range(HEADS_PER_GROUP)], axis=0)
    dsk_l = _head_lanes(jnp.broadcast_to(dsk, (8, 128)), 2)[:1]
    y = jnp.concatenate(yd, axis=1) + y_off + xs * dsk_l
    return y, cd_rows * hprev + st


def _ssd_span(xs, bm, cm, dtr, h0, dtb, alog, dsk):
    li = lax.broadcasted_iota(jnp.int32, (CHUNK, CHUNK), 0)
    si = lax.broadcasted_iota(jnp.int32, (CHUNK, CHUNK), 1)
    causal = li >= si
    k = (causal, causal.astype(F32), si < HEAD_DIM)
    h, ys = h0, []
    for t in range(xs.shape[0] // CHUNK):
        r = slice(t * CHUNK, (t + 1) * CHUNK)
        y, h = _ssd_chunk(k, xs[r], bm[r], cm[r], dtr[r], h, dtb, alog, dsk)
        ys.append(y)
    return (ys[0] if len(ys) == 1 else jnp.concatenate(ys, axis=0)), h


def _ssd_specs(rev, nsteps, rows):
    def s_of(s):
        return nsteps - 1 - s if rev else s

    xs = pl.BlockSpec((rows, _XS_GW), lambda g, s: (s_of(s), g))
    bm = pl.BlockSpec((rows, N_STATE), lambda g, s: (s_of(s), D_INNER // N_STATE + g))
    cm = pl.BlockSpec((rows, N_STATE), lambda g, s: (s_of(s), D_INNER // N_STATE + N_GROUPS + g))
    dt = pl.BlockSpec((rows, 128), lambda g, s: (s_of(s), g))
    par = pl.BlockSpec((1, 128), lambda g, s: (0, g))
    st = pl.BlockSpec((None, None, _XS_GW, N_STATE), lambda g, s: (g, s_of(s), 0, 0))
    return xs, bm, cm, dt, par, st


def _ssd_fwd(act, dtr, dtb, alog, dsk, *, nrows):
    rows = CHUNK * min(SSD_SPAN, nrows // CHUNK)
    nsteps = nrows // rows
    xs, bm, cm, dt, par, st = _ssd_specs(False, nsteps, rows)

    def body(xs_ref, b_ref, c_ref, dt_ref, dtb_ref, al_ref, dk_ref, y_ref, st_ref, h_ref):
        @pl.when(pl.program_id(1) == 0)
        def _():
            h_ref[...] = jnp.zeros_like(h_ref)

        h0 = h_ref[...]
        st_ref[...] = h0
        y, hnew = _ssd_span(xs_ref[...].astype(F32), b_ref[...].astype(F32), c_ref[...].astype(F32), dt_ref[...],
                            h0, dtb_ref[...], al_ref[...], dk_ref[...])
        y_ref[...] = y.astype(y_ref.dtype)
        h_ref[...] = hnew

    return pl.pallas_call(
        body, name="ssd_fwd", grid=(N_GROUPS, nsteps),
        in_specs=[xs, bm, cm, dt, par, par, par], out_specs=[xs, st],
        out_shape=[jax.ShapeDtypeStruct((nrows, D_INNER), ACT_DTYPE),
                   jax.ShapeDtypeStruct((N_GROUPS, nsteps, _XS_GW, N_STATE), F32)],
        scratch_shapes=[pltpu.VMEM((_XS_GW, N_STATE), F32)],
        compiler_params=pltpu.CompilerParams(dimension_semantics=("arbitrary", "arbitrary")),
    )(act, act, act, dtr, dtb, alog, dsk)


def _ssd_bwd(act, dtr, dtb, alog, dsk, states, dy, *, nrows):
    rows = CHUNK * min(SSD_SPAN, nrows // CHUNK)
    nsteps = nrows // rows
    xs, bm, cm, dt, par, st = _ssd_specs(True, nsteps, rows)

    def body(xs_ref, b_ref, c_ref, dt_ref, dtb_ref, al_ref, dk_ref, st_ref, dy_ref,
             dxs_ref, db_ref, dc_ref, ddt_ref, ddtb_ref, dal_ref, ddk_ref, dh_ref):
        @pl.when(pl.program_id(1) == 0)
        def _():
            dh_ref[...] = jnp.zeros_like(dh_ref)
            ddtb_ref[...] = jnp.zeros_like(ddtb_ref)
            dal_ref[...] = jnp.zeros_like(dal_ref)
            ddk_ref[...] = jnp.zeros_like(ddk_ref)

        _, vjp = jax.vjp(_ssd_span, xs_ref[...].astype(F32), b_ref[...].astype(F32), c_ref[...].astype(F32),
                         dt_ref[...], st_ref[...], dtb_ref[...], al_ref[...], dk_ref[...])
        dxs, db, dc, ddt, dh, ddtb, dal, ddk = vjp((dy_ref[...].astype(F32), dh_ref[...]))
        dxs_ref[...] = dxs.astype(dxs_ref.dtype)
        db_ref[...] = db.astype(db_ref.dtype)
        dc_ref[...] = dc.astype(dc_ref.dtype)
        ddt_ref[...] = ddt
        dh_ref[...] = dh
        ddtb_ref[...] += ddtb
        dal_ref[...] += dal
        ddk_ref[...] += ddk

    nb = N_GROUPS * N_STATE
    bspec = pl.BlockSpec((rows, N_STATE), lambda g, s: (nsteps - 1 - s, g))
    return pl.pallas_call(
        body, name="ssd_bwd", grid=(N_GROUPS, nsteps),
        in_specs=[xs, bm, cm, dt, par, par, par, st, xs],
        out_specs=[xs, bspec, bspec, dt, par, par, par],
        out_shape=[jax.ShapeDtypeStruct((nrows, D_INNER), ACT_DTYPE), jax.ShapeDtypeStruct((nrows, nb), ACT_DTYPE),
                   jax.ShapeDtypeStruct((nrows, nb), ACT_DTYPE), jax.ShapeDtypeStruct((nrows, DT_W), F32),
                   jax.ShapeDtypeStruct((1, DT_W), F32), jax.ShapeDtypeStruct((1, DT_W), F32),
                   jax.ShapeDtypeStruct((1, DT_W), F32)],
        scratch_shapes=[pltpu.VMEM((_XS_GW, N_STATE), F32)],
        compiler_params=pltpu.CompilerParams(dimension_semantics=("arbitrary", "arbitrary")),
    )(act, act, act, dtr, dtb, alog, dsk, states, dy)


def _add_epilogue(acc, r):
    return r + acc


def _rms_and_skip(x, g):
    return _rms(x, g), x


def _forward_backward(x, p, tgt, w, late_weights=None, after=()):
    s = x.shape[0]
    act_t, f32 = ACT_DTYPE, F32
    mm = functools.partial(_matmul)
    h, proj, dtr = _rows_matmul("proj", _f_rms, [(x, D_MODEL, 0)], [w["norm_g"]], w["w_main"], out_dtype=act_t,
                                n=MAIN_W, k=D_MODEL, tm=1024, nrows=s, after=after, side=(w["w_dt"], f32))
    act = _conv_fwd(proj, w["conv_w"], w["conv_b"], tm=512, nrows=s)
    y, states = _ssd_fwd(act, dtr, w["dt_bias"], w["a_log"], w["d_skip"], nrows=s)
    if late_weights is not None:
        w = {**w, **late_weights(states)}
    a_pars = [w["ln_a_g"], w["ln_a_b"], w["w_s"], w["b_s"]]
    y_a, o_a = _rows_matmul("out_a", _f_branch_a, [(proj, UVZ_W, UVZ_CB)], a_pars, w["w_oa"], out_dtype=act_t,
                            n=D_MODEL, k=E_A, tm=256, nrows=s)
    gn_rows = [(y, D_INNER, 0), (proj, ZB_W, ZB_CB)]
    y_b, o_b = _rows_matmul("out_b", _f_gnorm, gn_rows, [w["ssm_norm_g"]], w["w_ob"], out_dtype=act_t,
                            n=D_MODEL, k=D_INNER, tm=512, nrows=s)
    mg_rows = [(proj, G_W, G_CB), (o_a, D_MODEL, 0), (o_b, D_MODEL, 0)]
    merged, x1 = _rows_matmul("out_proj", _f_merge, mg_rows, [], w["w_out"], out_dtype=f32, n=D_MODEL, k=D_MODEL,
                              tm=512, nrows=s, extras=(x,), epilogue=_add_epilogue)
    g = {}
    hp, dx2, dgp, dpe, g["final_g"], loss = _head(x1, p, tgt, w["ple_norm_g"], w["w_pg"], w["w_ple"], w["final_g"],
                                                   tm=256, nrows=s)
    g["w_pg"] = mm(hp, dgp, mode="tn", name="d_w_pg", out_dtype=f32, m=D_MODEL, n=D_MODEL, k=s)
    g["w_ple"] = mm(p, dpe, mode="tn", name="d_w_ple", out_dtype=f32, m=PLE_DIM, n=D_MODEL, k=s)
    dx1, g["ple_norm_g"] = _rows_vjp_call(
        "ple_norm_bwd", _rms_and_skip, [(x1, D_MODEL, 0)], [w["ple_norm_g"]], [(dx2, D_MODEL, 0)],
        [(f32, None)], tm=512, nrows=s, cot_mm=(dgp, w["w_pg"], None))
    g["w_out"] = mm(merged, dx1, mode="tn", name="d_w_out", out_dtype=f32, m=D_MODEL, n=D_MODEL, k=s)
    dproj, do_a, do_b = _rows_vjp_call(
        "merge_bwd", _f_merge, mg_rows, [], [],
        [(act_t, (None, MAIN_W, G_CB)), (act_t, None), (act_t, None)], tm=512, nrows=s, cot_mm=(dx1, w["w_out"], None))
    g["w_oa"] = mm(y_a, do_a, mode="tn", name="d_w_oa", out_dtype=f32, m=E_A, n=D_MODEL, k=s)
    g["w_ob"] = mm(y_b, do_b, mode="tn", name="d_w_ob", out_dtype=f32, m=D_INNER, n=D_MODEL, k=s)
    dy, dproj, g["ssm_norm_g"] = _rows_vjp_call(
        "gnorm_bwd", _f_gnorm, gn_rows, [w["ssm_norm_g"]], [],
        [(act_t, None), (act_t, (dproj, MAIN_W, ZB_CB))], tm=256, nrows=s, cot_mm=(do_b, w["w_ob"], None))
    dxs, dbm, dcm, ddtr, g["dt_bias"], g["a_log"], g["d_skip"] = _ssd_bwd(
        act, dtr, w["dt_bias"], w["a_log"], w["d_skip"], states, dy, nrows=s)
    dpre, g["conv_w"], g["conv_b"] = _conv_bwd_pre(proj, w["conv_w"], w["conv_b"], (dxs, dbm, dcm), tm=512, nrows=s)
    dproj = _conv_bwd_x(dpre, w["conv_w"], dproj, tm=512, nrows=s)
    dproj, g["ln_a_g"], g["ln_a_b"], g["w_s"], g["b_s"] = _rows_vjp_call(
        "branch_a_bwd", _f_branch_a, [(proj, UVZ_W, UVZ_CB)], a_pars, [],
        [(act_t, (dproj, MAIN_W, UVZ_CB))], tm=256, nrows=s, cot_mm=(do_a, w["w_oa"], None))
    g["w_main"] = mm(h, dproj, mode="tn", name="d_w_main", out_dtype=f32, m=D_MODEL, n=MAIN_W, k=s)
    g["w_dt"] = mm(h, ddtr, mode="tn", name="d_w_dt", out_dtype=f32, m=D_MODEL, n=DT_W, k=s)
    return loss, g, (dproj, ddtr, dx1)


def _input_grad(x, w, ctx, after=()):
    dproj, ddtr, dx1 = ctx
    s = x.shape[0]
    dh = _matmul(dproj, w["w_main"], mode="nt", name="d_h_main", out_dtype=F32, m=s, n=D_MODEL, k=MAIN_W, after=after)
    return _rows_vjp_call(
        "pre_norm_bwd", _rms_and_skip, [(x, D_MODEL, 0)], [w["norm_g"]], [(dx1, D_MODEL, 0)],
        [(F32, None)], tm=512, nrows=s, cot_mm=(ddtr, w["w_dt"], dh))


def _local_step(x, p, tgt, w):
    loss, g, ctx = _forward_backward(x, p, tgt, w)
    grad_x, g["norm_g"] = _input_grad(x, w, ctx)
    return loss, grad_x, g


_O_ZB = 3 * E_A
_O_XBC = _O_ZB + D_INNER
_O_DT = _O_XBC + CONV_DIM
_O_G = _O_DT + N_HEADS


def _heads_to_lanes(v):
    r = v.shape[0]
    v = v.reshape(r, N_GROUPS, HEADS_PER_GROUP)
    return jnp.pad(v, ((0, 0), (0, 0), (0, 128 - HEADS_PER_GROUP))).reshape(r, DT_W)


def _lanes_to_heads(v):
    r = v.shape[0]
    return v.reshape(r, N_GROUPS, 128)[:, :, :HEADS_PER_GROUP].reshape(r, N_HEADS)


def _block_cols(blocks, a, b):
    parts = []
    for k in range(N_CHIPS):
        lo, hi = max(a, k * W_IN_BLOCK), min(b, (k + 1) * W_IN_BLOCK)
        if lo < hi:
            parts.append(blocks[k][:, lo - k * W_IN_BLOCK:hi - k * W_IN_BLOCK])
    return parts


_W_IN_SEGMENTS = ((0, _O_ZB, "m", 0), (_O_ZB, _O_XBC, "m", UVZ_W + XBC_W), (_O_XBC, _O_DT, "m", UVZ_W),
                  (_O_DT, _O_G, "d", 0), (_O_G, N_IN, "m", MAIN_W - G_W))


def _w_in_grad_blocks(gm, gdt):
    blocks = []
    for k in range(N_CHIPS):
        a, b = k * W_IN_BLOCK, (k + 1) * W_IN_BLOCK
        parts = []
        for s, e, src, off in _W_IN_SEGMENTS:
            lo, hi = max(a, s), min(b, e)
            if lo < hi:
                parts.append((gm if src == "m" else gdt)[:, off + lo - s:off + hi - s])
        blocks.append(jnp.concatenate(parts, axis=1))
    return jnp.stack(blocks)


def _layout_weights(f, w_in_blocks=None):
    w = dict(f)
    if w_in_blocks is None:
        w_in = w.pop("w_in")
        w_in_blocks = jnp.stack([w_in[:, k * W_IN_BLOCK:(k + 1) * W_IN_BLOCK] for k in range(N_CHIPS)])
    cols = functools.partial(_block_cols, w_in_blocks)
    w["w_main"] = jnp.concatenate(cols(0, _O_ZB) + cols(_O_XBC, _O_DT) + cols(_O_ZB, _O_XBC) + cols(_O_G, N_IN), axis=1)
    w["w_dt"] = _heads_to_lanes(jnp.concatenate(cols(_O_DT, _O_G), axis=1))
    w["b_s"] = f["b_s"].reshape(G_A, CHUNK, 1)
    for n in ("dt_bias", "a_log", "d_skip"):
        w[n] = _heads_to_lanes(f[n])
    return w


def _natural_grads(g):
    out = dict(g)
    gm = out.pop("w_main")
    gdt = _lanes_to_heads(out.pop("w_dt"))
    out["w_in"] = jnp.concatenate(
        [gm[:, :UVZ_W], gm[:, UVZ_W + XBC_W:UVZ_W + XBC_W + ZB_W], gm[:, UVZ_W:UVZ_W + XBC_W], gdt, gm[:, MAIN_W - G_W:]],
        axis=1)
    out["b_s"] = g["b_s"].reshape(G_A, CHUNK)
    for n in ("dt_bias", "a_log", "d_skip"):
        out[n] = _lanes_to_heads(g[n])
    return out


def _place():
    return lax.axis_index("x"), lax.axis_index("y"), lax.axis_index("c")


def _other_chips(x, y):
    return [(1 - x, y), (x, 1 - y), (1 - x, 1 - y)]


def _rcopy(src, dst, ssem, rsem, dev):
    return pltpu.make_async_remote_copy(src_ref=src, dst_ref=dst, send_sem=ssem, recv_sem=rsem,
                                        device_id=dev, device_id_type=MESH)


def _half(ref_rows, half):
    hs = ref_rows // 2
    return pl.ds(pl.multiple_of(half * hs, 16), hs)


def _gather_weights(shards, conv_shard):
    nw = len(shards)

    def body(*refs):
        sh, cv = refs[:nw], refs[nw]
        out, cvo = refs[nw + 1:2 * nw + 1], refs[2 * nw + 1]
        ici_s, ici_r, fw_s, fw_r, own_s, own_r, cv_s, cv_r = refs[2 * nw + 2:]
        x, y, c = _place()
        me, sib, chips = 2 * x + y, (x, y, 1 - c), _other_chips(x, y)
        own = [_rcopy(sh[w], out[w].at[me], own_s.at[w], own_r.at[w], sib) for w in range(nw)]
        own.append(_rcopy(cv, cvo.at[me], own_s.at[nw], own_r.at[nw], sib))
        for cp in own:
            cp.start()
        sends = []
        for w in range(nw):
            mine = _half(sh[w].shape[0], c)
            for j, chip in enumerate(chips):
                sends.append(_rcopy(sh[w].at[mine], out[w].at[me, mine], ici_s.at[3 * w + j], ici_r.at[3 * w + j], (*chip, c)))
        for j, chip in enumerate(chips):
            sends.append(_rcopy(cv, cvo.at[me], cv_s.at[j], cv_r.at[j], (*chip, c)))
        for cp in sends:
            cp.start()
        for w in range(nw):
            mine = _half(sh[w].shape[0], c)
            for j, chip in enumerate(chips):
                slab = out[w].at[2 * chip[0] + chip[1], mine]
                _rcopy(slab, slab, ici_s.at[3 * w + j], ici_r.at[3 * w + j], (*chip, c)).wait_recv()
                fwd = _rcopy(slab, slab, fw_s.at[3 * w + j], fw_r.at[3 * w + j], sib)
                fwd.start()
                sends.append(fwd)
        for j, chip in enumerate(chips):
            blk = cvo.at[2 * chip[0] + chip[1]]
            _rcopy(blk, blk, cv_s.at[j], cv_r.at[j], (*chip, c)).wait_recv()
        for w in range(nw):
            theirs = _half(sh[w].shape[0], 1 - c)
            for j, chip in enumerate(chips):
                slab = out[w].at[2 * chip[0] + chip[1], theirs]
                _rcopy(slab, slab, fw_s.at[3 * w + j], fw_r.at[3 * w + j], sib).wait_recv()
        for cp in sends:
            cp.wait_send()
        for cp in own:
            cp.wait()

    dma = pltpu.SemaphoreType.DMA
    return pl.pallas_call(
        body, name="gather_weights",
        in_specs=[ANY] * (nw + 1), out_specs=[ANY] * (nw + 1),
        out_shape=[jax.ShapeDtypeStruct((N_CHIPS,) + s.shape, s.dtype) for s in shards]
        + [jax.ShapeDtypeStruct((N_CHIPS,) + conv_shard.shape, conv_shard.dtype)],
        scratch_shapes=[dma((3 * nw,)), dma((3 * nw,)), dma((3 * nw,)), dma((3 * nw,)), dma((nw + 1,)), dma((nw + 1,)),
                        dma((3,)), dma((3,))],
    )(*shards, conv_shard)


_HBM = pl.BlockSpec(memory_space=pltpu.HBM)
_SEM = pl.BlockSpec(memory_space=pltpu.SEMAPHORE)
_EFFECT = pltpu.SideEffectType.DATAFLOW_SIDE_EFFECTING


def _late_gather_copies(sh, out, s_sem, r_sem):
    x, y, c = _place()
    to = [(*chip, c) for chip in _other_chips(x, y)] + [(x, y, 1 - c)]
    return [_rcopy(sh[w], out[w].at[2 * x + y], s_sem.at[4 * w + j], r_sem.at[4 * w + j], dev)
            for w in range(len(sh)) for j, dev in enumerate(to)]


def _late_gather_start(shards):
    n = len(shards)
    lands = [lax.empty((N_CHIPS,) + a.shape, a.dtype) for a in shards]

    def body(*refs):
        for cp in _late_gather_copies(refs[:n], refs[n:2 * n], refs[2 * n], refs[2 * n + 1]):
            cp.start()
        refs[-1][...] = jnp.zeros_like(refs[-1])

    dma = pltpu.SemaphoreType.DMA
    hbm = [pltpu.with_memory_space_constraint(a, pltpu.HBM) for a in list(shards) + lands]
    out = pl.pallas_call(
        body, name="late_gather_start",
        out_shape=[dma((4 * n,)), dma((4 * n,))] + [pltpu.HBM(a.shape, a.dtype) for a in hbm]
        + [jax.ShapeDtypeStruct((8, 128), F32)],
        in_specs=[_HBM] * (2 * n), out_specs=[_SEM, _SEM] + [_HBM] * (2 * n) + [pl.BlockSpec(memory_space=pltpu.VMEM)],
        input_output_aliases={i: 2 + i for i in range(2 * n)},
        compiler_params=pltpu.CompilerParams(has_side_effects=_EFFECT),
    )(*hbm)
    return out[0], out[1], out[2:2 + n], out[2 + n:2 + 2 * n], out[-1]


def _late_gather_wait(s_sem, r_sem, srcs, lands, after):
    n = len(srcs)

    def body(*refs):
        for cp in _late_gather_copies(refs[:n], refs[n:2 * n], refs[2 * n], refs[2 * n + 1]):
            cp.wait_send()
            cp.wait_recv()

    out = pl.pallas_call(
        body, name="late_gather_wait",
        out_shape=[pltpu.HBM(a.shape, a.dtype) for a in list(srcs) + list(lands)],
        in_specs=[_HBM] * (2 * n) + [_SEM, _SEM, ANY], out_specs=[_HBM] * (2 * n),
        input_output_aliases={i: i for i in range(2 * n)},
        compiler_params=pltpu.CompilerParams(has_side_effects=_EFFECT),
    )(*srcs, *lands, s_sem, r_sem, after)
    return out[n:]


def _swap_with_sibling(arrs):
    n = len(arrs)

    def body(*refs):
        src, dst, s_sem, r_sem = refs[:n], refs[n:2 * n], refs[2 * n], refs[2 * n + 1]
        x, y, c = _place()
        cps = [_rcopy(src[i], dst[i], s_sem.at[i], r_sem.at[i], (x, y, 1 - c)) for i in range(n)]
        for cp in cps:
            cp.start()
        for cp in cps:
            cp.wait()

    dma = pltpu.SemaphoreType.DMA
    return pl.pallas_call(
        body, name="swap_with_sibling", in_specs=[ANY] * n, out_specs=[ANY] * n,
        out_shape=[jax.ShapeDtypeStruct(a.shape, a.dtype) for a in arrs], scratch_shapes=[dma((n,)), dma((n,))],
    )(*arrs)


def _scatter_copies(src, land, s_sem, r_sem):
    x, y, c = _place()
    return [_rcopy(src[i].at[2 * chip[0] + chip[1]], land[i].at[j], s_sem.at[3 * i + j], r_sem.at[3 * i + j], (*chip, c))
            for i in range(len(src)) for j, chip in enumerate(_other_chips(x, y))]


def _scatter_blocks_start(arrs):
    n = len(arrs)
    lands = [lax.empty((3,) + a.shape[1:], a.dtype) for a in arrs]

    def body(*refs):
        src, land, s_sem, r_sem, token = refs[:n], refs[n:2 * n], refs[2 * n], refs[2 * n + 1], refs[-1]
        for cp in _scatter_copies(src, land, s_sem, r_sem):
            cp.start()
        token[...] = jnp.zeros_like(token)

    dma = pltpu.SemaphoreType.DMA
    hbm = [pltpu.with_memory_space_constraint(a, pltpu.HBM) for a in list(arrs) + lands]
    out = pl.pallas_call(
        body, name="scatter_blocks_start",
        out_shape=[dma((3 * n,)), dma((3 * n,))] + [pltpu.HBM(a.shape, a.dtype) for a in hbm]
        + [jax.ShapeDtypeStruct((8, 128), F32)],
        in_specs=[_HBM] * (2 * n), out_specs=[_SEM, _SEM] + [_HBM] * (2 * n) + [pl.BlockSpec(memory_space=pltpu.VMEM)],
        input_output_aliases={i: 2 + i for i in range(2 * n)},
        compiler_params=pltpu.CompilerParams(has_side_effects=_EFFECT),
    )(*hbm)
    return out[0], out[1], out[2:2 + n], out[2 + n:2 + 2 * n], out[-1]


def _scatter_blocks_wait(s_sem, r_sem, srcs, lands, after):
    n = len(srcs)

    def body(*refs):
        src, land, s_sem, r_sem = refs[:n], refs[n:2 * n], refs[2 * n], refs[2 * n + 1]
        for cp in _scatter_copies(src, land, s_sem, r_sem):
            cp.wait_send()
            cp.wait_recv()

    out = pl.pallas_call(
        body, name="scatter_blocks_wait",
        out_shape=[pltpu.HBM(a.shape, a.dtype) for a in list(srcs) + list(lands)],
        in_specs=[_HBM] * (2 * n) + [_SEM, _SEM] + [ANY] * len(after), out_specs=[_HBM] * (2 * n),
        input_output_aliases={i: i for i in range(2 * n)},
        compiler_params=pltpu.CompilerParams(has_side_effects=_EFFECT),
    )(*srcs, *lands, s_sem, r_sem, *after)
    return out[:n], out[n:]


def _share_halves(arrs):
    n = len(arrs)

    def body(*refs):
        buf, s_sem, r_sem = refs[n:2 * n], refs[2 * n], refs[2 * n + 1]
        x, y, c = _place()
        cps = []
        for i in range(n):
            mine = buf[i].at[_half(buf[i].shape[0], c)]
            cps.append(_rcopy(mine, mine, s_sem.at[i], r_sem.at[i], (x, y, 1 - c)))
        for cp in cps:
            cp.start()
        for i in range(n):
            theirs = buf[i].at[_half(buf[i].shape[0], 1 - c)]
            _rcopy(theirs, theirs, s_sem.at[i], r_sem.at[i], (x, y, 1 - c)).wait_recv()
        for cp in cps:
            cp.wait_send()

    dma = pltpu.SemaphoreType.DMA
    return pl.pallas_call(
        body, name="share_halves", in_specs=[ANY] * n, out_specs=[ANY] * n,
        out_shape=[jax.ShapeDtypeStruct(a.shape, a.dtype) for a in arrs],
        input_output_aliases={i: i for i in range(n)}, scratch_shapes=[dma((n,)), dma((n,))],
    )(*arrs)


def _small_gather_copies(src, land, s_sem, r_sem):
    x, y, c = _place()
    cps = []
    for d in range(1, N_DEV):
        peer = ((1 - x) if d & 4 else x), ((1 - y) if d & 2 else y), ((1 - c) if d & 1 else c)
        cps.append(_rcopy(src, land.at[4 * x + 2 * y + c], s_sem.at[d - 1], r_sem.at[d - 1], peer))
    return cps


def _small_gather_start(packed):
    def body(src, land, s_sem, r_sem, _, __, token):
        for cp in _small_gather_copies(src, land, s_sem, r_sem):
            cp.start()
        token[...] = jnp.zeros_like(token)

    dma = pltpu.SemaphoreType.DMA
    hbm = [pltpu.with_memory_space_constraint(a, pltpu.HBM) for a in (packed, lax.empty((N_DEV,) + packed.shape, F32))]
    return pl.pallas_call(
        body, name="small_gather_start",
        out_shape=[dma((N_DEV - 1,)), dma((N_DEV - 1,))] + [pltpu.HBM(a.shape, a.dtype) for a in hbm]
        + [jax.ShapeDtypeStruct((8, 128), F32)],
        in_specs=[_HBM] * 2, out_specs=[_SEM, _SEM, _HBM, _HBM, pl.BlockSpec(memory_space=pltpu.VMEM)],
        input_output_aliases={0: 2, 1: 3}, compiler_params=pltpu.CompilerParams(has_side_effects=_EFFECT),
    )(*hbm)


def _small_gather_wait(s_sem, r_sem, src, land, after):
    def body(src, land, s_sem, r_sem, *_):
        for cp in _small_gather_copies(src, land, s_sem, r_sem):
            cp.wait_send()
            cp.wait_recv()

    return pl.pallas_call(
        body, name="small_gather_wait", out_shape=[pltpu.HBM(src.shape, src.dtype), pltpu.HBM(land.shape, land.dtype)],
        in_specs=[_HBM, _HBM, _SEM, _SEM, ANY], out_specs=[_HBM, _HBM], input_output_aliases={0: 0, 1: 1},
        compiler_params=pltpu.CompilerParams(has_side_effects=_EFFECT),
    )(src, land, s_sem, r_sem, after)


def _small_sum(own, land, dev_arr):
    def body(me_ref, own_ref, land_ref, o_ref):
        acc = jnp.zeros(o_ref.shape, F32)
        for d in range(N_DEV):
            acc = acc + jnp.where(me_ref[0] == d, own_ref[...], land_ref[d])
        o_ref[...] = acc

    return pl.pallas_call(
        body, name="small_sum", out_shape=jax.ShapeDtypeStruct(own.shape, F32),
        grid_spec=pltpu.PrefetchScalarGridSpec(
            num_scalar_prefetch=1, grid=(1,),
            in_specs=[pl.BlockSpec(own.shape, lambda i, m: (0, 0)), pl.BlockSpec(land.shape, lambda i, m: (0, 0, 0))],
            out_specs=pl.BlockSpec(own.shape, lambda i, m: (0, 0))),
    )(dev_arr, own, land)


def _row_tile(rows, cols):
    tr = max(8, min(rows, (1 << 20) // (4 * cols) // 8 * 8))
    while rows % tr:
        tr -= 8
    return tr


def _chip_sum(name, g5, recv, c_arr):
    nb, _, hs, cols = g5.shape
    tr = _row_tile(hs, cols)

    def body(_, a_ref, b_ref, o_ref):
        o_ref[...] = (a_ref[...] + b_ref[...].astype(F32)).astype(o_ref.dtype)

    blk = pl.BlockSpec((None, tr, cols), lambda b, i, c: (b, i, 0))
    return pl.pallas_call(
        body, name=name,
        grid_spec=pltpu.PrefetchScalarGridSpec(
            num_scalar_prefetch=1, grid=(nb, hs // tr),
            in_specs=[pl.BlockSpec((None, None, tr, cols), lambda b, i, c: (b, c[0], i, 0)), blk], out_specs=blk),
        out_shape=jax.ShapeDtypeStruct((nb, hs, cols), WIRE_DTYPE),
    )(c_arr, g5, recv)


def _final_sum(name, own, recv, place_arr):
    _, hs, cols = own.shape
    tr = _row_tile(hs, cols)
    nt = hs // tr

    def body(_, a_ref, r_ref, o_ref):
        o_ref[...] = ((a_ref[...].astype(F32) + r_ref[0].astype(F32)) + r_ref[1].astype(F32)) + r_ref[2].astype(F32)

    return pl.pallas_call(
        body, name=name,
        grid_spec=pltpu.PrefetchScalarGridSpec(
            num_scalar_prefetch=1, grid=(nt,),
            in_specs=[pl.BlockSpec((None, tr, cols), lambda i, m: (m[0], i, 0)),
                      pl.BlockSpec((3, tr, cols), lambda i, m: (0, i, 0))],
            out_specs=pl.BlockSpec((tr, cols), lambda i, m: (m[1] * nt + i, 0))),
        out_shape=jax.ShapeDtypeStruct((2 * hs, cols), F32),
    )(place_arr, own, recv)


def _adamw(w, g, m, v):
    m = ADAM_B1 * m + (1.0 - ADAM_B1) * g
    v = ADAM_B2 * v + (1.0 - ADAM_B2) * (g * g)
    m_hat = m / (1.0 - ADAM_B1 ** ADAM_STEP)
    v_hat = v / (1.0 - ADAM_B2 ** ADAM_STEP)
    return -ADAM_LR * (m_hat / (jnp.sqrt(v_hat) + ADAM_EPS) + ADAM_WD * w), m, v


def _adamw_call(name, w, g, m, v):
    rows, cols = w.shape
    tr = _row_tile(rows, cols)
    if 4 * tr * cols >= (1 << 18):
        blk, steps = pl.BlockSpec((tr, cols), lambda i: (i, 0)), rows // tr
    else:
        blk, steps = pl.BlockSpec((rows, 128), lambda i: (0, i)), cols // 128

    def body(w_ref, g_ref, m_ref, v_ref, d_ref, nm_ref, nv_ref, go_ref):
        g = g_ref[...]
        d_ref[...], nm_ref[...], nv_ref[...] = _adamw(w_ref[...], g, m_ref[...], v_ref[...])
        go_ref[...] = g

    return pl.pallas_call(
        body, name=name, grid=(steps,), in_specs=[blk] * 4, out_specs=[blk] * 4,
        out_shape=[jax.ShapeDtypeStruct(w.shape, F32)] * 4,
        compiler_params=pltpu.CompilerParams(dimension_semantics=("parallel",)),
    )(w, g, m, v)


def _adamw_small(ws, gs, ms, vs):
    n = len(ws)

    def body(*refs):
        for i in range(n):
            w_ref, g_ref, m_ref, v_ref = (refs[k * n + i] for k in range(4))
            d, nm, nv = _adamw(w_ref[...], g_ref[...], m_ref[...], v_ref[...])
            refs[4 * n + i][...] = d
            refs[5 * n + i][...] = nm
            refs[6 * n + i][...] = nv

    out = pl.pallas_call(
        body, name="adamw_small", out_shape=[jax.ShapeDtypeStruct(a.shape, F32) for a in ws] * 3,
    )(*ws, *gs, *ms, *vs)
    return out[:n], out[n:2 * n], out[2 * n:]


_BIG = ("w_in", "w_oa", "w_ob", "w_out", "w_pg", "w_ple")
_SMALL = ("norm_g", "ln_a_g", "ln_a_b", "w_s", "b_s", "conv_w", "conv_b", "dt_bias", "a_log", "d_skip", "ssm_norm_g",
          "ple_norm_g", "final_g")
_WEIGHTS = ("norm_g", "w_in", "ln_a_g", "ln_a_b", "w_s", "b_s", "conv_w", "conv_b", "dt_bias", "a_log", "d_skip",
            "ssm_norm_g", "w_oa", "w_ob", "w_out", "ple_norm_g", "w_pg", "w_ple", "final_g")
_COL_SHARDED = ("w_in", "w_ple")
_PACK = 1024


def _blocks_to_full(col_sharded, blocks):
    if col_sharded:
        return jnp.concatenate([blocks[k] for k in range(N_CHIPS)], axis=1)
    return blocks.reshape(N_CHIPS * blocks.shape[1], blocks.shape[2])


def _full_to_blocks(col_sharded, full):
    if col_sharded:
        w = full.shape[1] // N_CHIPS
        return jnp.stack([full[:, k * w:(k + 1) * w] for k in range(N_CHIPS)])
    return full.reshape(N_CHIPS, full.shape[0] // N_CHIPS, full.shape[1])


def _two_d(n, a):
    if n == "w_s":
        return a.reshape(G_A * CHUNK, CHUNK)
    if n in ("b_s", "conv_w"):
        return a.reshape(a.shape[-2], a.shape[-1])
    return a.reshape(1, a.shape[-1])


def kernel(x, p, norm_g, w_in, ln_a_g, ln_a_b, w_s, b_s, conv_w, conv_b, dt_bias, a_log, d_skip, ssm_norm_g, w_oa, w_ob, w_out, ple_norm_g, w_pg, w_ple, final_g, loss_target, m_norm_g, m_w_in, m_ln_a_g, m_ln_a_b, m_w_s, m_b_s, m_conv_w, m_conv_b, m_dt_bias, m_a_log, m_d_skip, m_ssm_norm_g, m_w_oa, m_w_ob, m_w_out, m_ple_norm_g, m_w_pg, m_w_ple, m_final_g, v_norm_g, v_w_in, v_ln_a_g, v_ln_a_b, v_w_s, v_b_s, v_conv_w, v_conv_b, v_dt_bias, v_a_log, v_d_skip, v_ssm_norm_g, v_w_oa, v_w_ob, v_w_out, v_ple_norm_g, v_w_pg, v_w_ple, v_final_g):
    wt = dict(norm_g=norm_g, w_in=w_in, ln_a_g=ln_a_g, ln_a_b=ln_a_b, w_s=w_s, b_s=b_s, conv_w=conv_w, conv_b=conv_b,
              dt_bias=dt_bias, a_log=a_log, d_skip=d_skip, ssm_norm_g=ssm_norm_g, w_oa=w_oa, w_ob=w_ob, w_out=w_out,
              ple_norm_g=ple_norm_g, w_pg=w_pg, w_ple=w_ple, final_g=final_g)
    mom = dict(norm_g=m_norm_g, w_in=m_w_in, ln_a_g=m_ln_a_g, ln_a_b=m_ln_a_b, w_s=m_w_s, b_s=m_b_s, conv_w=m_conv_w,
               conv_b=m_conv_b, dt_bias=m_dt_bias, a_log=m_a_log, d_skip=m_d_skip, ssm_norm_g=m_ssm_norm_g, w_oa=m_w_oa,
               w_ob=m_w_ob, w_out=m_w_out, ple_norm_g=m_ple_norm_g, w_pg=m_w_pg, w_ple=m_w_ple, final_g=m_final_g)
    vel = dict(norm_g=v_norm_g, w_in=v_w_in, ln_a_g=v_ln_a_g, ln_a_b=v_ln_a_b, w_s=v_w_s, b_s=v_b_s, conv_w=v_conv_w,
               conv_b=v_conv_b, dt_bias=v_dt_bias, a_log=v_a_log, d_skip=v_d_skip, ssm_norm_g=v_ssm_norm_g, w_oa=v_w_oa,
               w_ob=v_w_ob, w_out=v_w_out, ple_norm_g=v_ple_norm_g, w_pg=v_w_pg, w_ple=v_w_ple, final_g=v_final_g)
    xi, yi, ci = _place()
    me = 2 * xi + yi
    c_arr = jnp.reshape(ci, (1,)).astype(jnp.int32)
    place_arr = jnp.stack([me, ci]).astype(jnp.int32)

    shard = {n: wt[n][0] for n in _BIG}
    wire = {n: shard[n].astype(WIRE_DTYPE) for n in _BIG}
    w_in_blocks, conv_blocks = _gather_weights([wire["w_in"]], conv_w[0])
    g_ssem, g_rsem, g_sent, g_lands, g_token = _late_gather_start([wire[n] for n in _BIG[1:]])
    full = {"conv_w": _blocks_to_full(True, conv_blocks)}
    for n in _SMALL:
        if n != "conv_w":
            full[n] = wt[n][0] if wt[n].ndim > 2 else wt[n].reshape(1, wt[n].shape[-1])

    def late_weights(after):
        blocks = _late_gather_wait(g_ssem, g_rsem, g_sent, g_lands, after)
        return {n: _blocks_to_full(n in _COL_SHARDED, b) for n, b in zip(_BIG[1:], blocks)}

    w = _layout_weights(full, w_in_blocks=w_in_blocks)
    loss_row, g, ctx = _forward_backward(x[0], p[0, 0], loss_target[0], w, late_weights, after=(g_token,))
    loss = lax.psum(loss_row[0, 0], ("x", "y", "c"))

    parts = {n: _full_to_blocks(n in _COL_SHARDED, g[n]) for n in _BIG[1:]}
    parts["w_main"] = g["w_main"][None]
    parts["w_dt"] = jnp.pad(_lanes_to_heads(g["w_dt"]), ((0, 0), (0, 128 - N_HEADS)))[None]
    names = ("w_main", "w_dt") + _BIG[1:]
    g5 = {n: parts[n].reshape(parts[n].shape[0], 2, parts[n].shape[1] // 2, parts[n].shape[2]) for n in names}
    to_sibling = [lax.dynamic_index_in_dim(g5[n], 1 - ci, axis=1, keepdims=False).astype(WIRE_DTYPE) for n in names]
    from_sibling = _swap_with_sibling(to_sibling)
    chip = {n: _chip_sum("chip_sum_" + n, g5[n], r, c_arr) for n, r in zip(names, from_sibling)}
    chip["w_in"] = _w_in_grad_blocks(chip["w_main"][0], chip["w_dt"][0])
    chip_wire = [chip[n] for n in _BIG]
    s_sem, r_sem, sent, lands, token = _scatter_blocks_start(chip_wire)
    grad_x, g["norm_g"] = _input_grad(x[0], w, ctx, after=(token,))
    g = _natural_grads(g)

    pieces = [_two_d(n, g[n]).reshape(-1) for n in _SMALL]
    sizes = [v.shape[0] for v in pieces]
    padded = [-(-s // _PACK) * _PACK for s in sizes]
    packed = jnp.concatenate([jnp.pad(v, (0, ps - s)) for v, s, ps in zip(pieces, sizes, padded)]).reshape(-1, 128)
    a_ssem, a_rsem, a_src, a_land, a_token = _small_gather_start(packed)

    sent, from_chips = _scatter_blocks_wait(s_sem, r_sem, sent, lands, (grad_x, a_token))
    halves = [_final_sum("final_sum_" + n, a, r, place_arr) for n, a, r in zip(_BIG, sent, from_chips)]
    grads = dict(zip(_BIG, _share_halves(halves)))
    delta, new_m, new_v = {}, {}, {}
    for n in _BIG:
        t = jnp.transpose if n == "w_in" else (lambda a: a)
        res = _adamw_call("adamw_" + n, t(shard[n]), t(grads[n]), t(mom[n][0]), t(vel[n][0]))
        delta[n], new_m[n], new_v[n], grads[n] = (t(r) for r in res)

    a_src, a_land = _small_gather_wait(a_ssem, a_rsem, a_src, a_land, delta["w_in"])
    summed = _small_sum(a_src, a_land, jnp.reshape(4 * xi + 2 * yi + ci, (1,)).astype(jnp.int32)).reshape(-1)
    off = 0
    for n, s, ps in zip(_SMALL, sizes, padded):
        grads[n] = summed[off:off + s].reshape(_two_d(n, g[n]).shape)
        off += ps
    grads["conv_w"] = lax.dynamic_slice_in_dim(grads["conv_w"], me * (CONV_DIM // N_CHIPS), CONV_DIM // N_CHIPS, axis=1)
    small = _adamw_small([_two_d(n, wt[n]) for n in _SMALL], [grads[n] for n in _SMALL],
                         [_two_d(n, mom[n]) for n in _SMALL], [_two_d(n, vel[n]) for n in _SMALL])
    for i, n in enumerate(_SMALL):
        delta[n], new_m[n], new_v[n] = small[0][i], small[1][i], small[2][i]

    def shaped(d):
        return [d[n].reshape(wt[n].shape) for n in _WEIGHTS]

    return (loss, grad_x[None], *shaped(grads), *shaped(delta), *shaped(new_m), *shaped(new_v))
```

```python
import functools

import jax
import jax.numpy as jnp
from jax import lax
from jax.experimental import pallas as pl
from jax.experimental.pallas import tpu as pltpu

F32 = jnp.float32
MXU_DTYPE = jnp.bfloat16
ACT_DTYPE = jnp.bfloat16
WIRE_DTYPE = jnp.bfloat16

D_MODEL = 1024
PLE_DIM = 256
CHUNK = 128
EPS = 1e-6
E_A = D_MODEL
G_A = 4
D_INNER = 2 * D_MODEL
HEAD_DIM = 64
N_HEADS = D_INNER // HEAD_DIM
N_STATE = 128
N_GROUPS = 4
HEADS_PER_GROUP = N_HEADS // N_GROUPS
PAIRS_PER_GROUP = HEADS_PER_GROUP // 2
CONV_K = 4
CONV_DIM = D_INNER + 2 * N_GROUPS * N_STATE
N_IN = 3 * E_A + D_INNER + CONV_DIM + N_HEADS + 2 * D_MODEL
N_CHIPS = 4
N_DEV = 8
W_IN_BLOCK = N_IN // N_CHIPS

UVZ_W, XBC_W, ZB_W, G_W = 3 * E_A, CONV_DIM, D_INNER, 2 * D_MODEL
MAIN_W = UVZ_W + XBC_W + ZB_W + G_W
UVZ_CB, XBC_CB, ZB_CB, G_CB = 0, 1, 3, 4
DT_W = N_GROUPS * 128

ADAM_LR, ADAM_B1, ADAM_B2, ADAM_EPS, ADAM_WD, ADAM_STEP = 0.001, 0.9, 0.999, 1e-08, 0.01, 10

MESH = pl.DeviceIdType.MESH
ANY = pl.BlockSpec(memory_space=pl.ANY)


def _mxu(v):
    return v.astype(MXU_DTYPE)


def _dot(a, b, dims=(((1,), (0,)), ((), ()))):
    return lax.dot_general(_mxu(a), _mxu(b), dims, preferred_element_type=F32)


MM_TILE = 1024
MM_VMEM_BUDGET = 46 << 20


def _mm_tk(m, n, k, tm, tn, a_bytes, b_bytes, out_bytes, extra_bytes):
    for parts in range(1, k // 128 + 1):
        if k % parts or (k // parts) % 128 and parts > 1:
            continue
        tk = k // parts
        need = 2 * tk * (tm * a_bytes + tn * b_bytes) + 2 * tm * tn * (out_bytes + extra_bytes) + (tm * tn * 4 if parts > 1 else 0)
        if need <= MM_VMEM_BUDGET:
            return tk
    return 128


def _matmul(a, b, *, mode, name, out_dtype, m, n, k, tm=MM_TILE, tn=MM_TILE, tk=None, a_off=0, b_off=0,
            extras=(), epilogue=None, after=()):
    tm, tn = min(tm, m), min(tn, n)
    if tk is None:
        tk = _mm_tk(m, n, k, tm, tn, a.dtype.itemsize, b.dtype.itemsize, jnp.dtype(out_dtype).itemsize,
                    sum(e.dtype.itemsize for e in extras))
    tk = min(tk, k)
    assert m % tm == 0 and n % tn == 0 and k % tk == 0, (name, m, n, k, tm, tn, tk)
    nk = k // tk
    if mode == "nn":
        assert a_off % tk == 0 and b_off % tn == 0
        a_spec = pl.BlockSpec((tm, tk), lambda i, j, kk: (i, kk + a_off // tk))
        b_spec = pl.BlockSpec((tk, tn), lambda i, j, kk: (kk, j + b_off // tn))
        dims = (((1,), (0,)), ((), ()))
    elif mode == "nt":
        a_spec = pl.BlockSpec((tm, tk), lambda i, j, kk: (i, kk))
        b_spec = pl.BlockSpec((tn, tk), lambda i, j, kk: (j, kk))
        dims = (((1,), (1,)), ((), ()))
    else:
        assert a_off % tm == 0 and b_off % tn == 0
        a_spec = pl.BlockSpec((tk, tm), lambda i, j, kk: (kk, i + a_off // tm))
        b_spec = pl.BlockSpec((tk, tn), lambda i, j, kk: (kk, j + b_off // tn))
        dims = (((0,), (0,)), ((), ()))
    ne = len(extras)

    def finish(acc, extra_refs, o_ref):
        res = acc if epilogue is None else epilogue(acc, *[e[...] for e in extra_refs])
        o_ref[...] = res.astype(o_ref.dtype)

    def body(a_ref, b_ref, *rest):
        extra_refs, o_ref = rest[:ne], rest[ne + len(after)]
        part = _dot(a_ref[...], b_ref[...], dims)
        if nk == 1:
            finish(part, extra_refs, o_ref)
            return
        acc_ref = rest[ne + len(after) + 1]
        kk = pl.program_id(2)

        @pl.when(kk == 0)
        def _():
            acc_ref[...] = part

        @pl.when(kk > 0)
        def _():
            acc_ref[...] += part

        @pl.when(kk == nk - 1)
        def _():
            finish(acc_ref[...], extra_refs, o_ref)

    o_spec = pl.BlockSpec((tm, tn), lambda i, j, kk: (i, j))
    return pl.pallas_call(
        body, name=name, grid=(m // tm, n // tn, nk),
        in_specs=[a_spec, b_spec] + [o_spec] * ne + [ANY] * len(after), out_specs=o_spec,
        out_shape=jax.ShapeDtypeStruct((m, n), out_dtype),
        scratch_shapes=[pltpu.VMEM((tm, tn), F32)] if nk > 1 else [],
        compiler_params=pltpu.CompilerParams(dimension_semantics=("parallel", "parallel", "arbitrary")),
    )(a, b, *extras, *after)


def _rows_matmul(name, f, rows, pars, b, *, out_dtype, n, k, tm, nrows, tn=MM_TILE, extras=(), epilogue=None, after=(),
                 side=None):
    tm, tn = min(tm, nrows), min(tn, n)
    assert nrows % tm == 0 and n % tn == 0, (name, nrows, n, tm, tn)
    nr, npar, ne, nj = len(rows), len(pars), len(extras), n // tn
    ns = 0 if side is None else 1
    n_in = nr + npar + 1 + ne + ns + len(after)

    def body(*refs):
        row_refs, par_refs, b_ref = refs[:nr], refs[nr:nr + npar], refs[nr + npar]
        extra_refs = refs[nr + npar + 1:nr + npar + 1 + ne]
        a_ref, o_ref = refs[n_in], refs[n_in + 1]

        def make_a():
            a = f(*[r[...].astype(F32) for r in row_refs], *[p[...] for p in par_refs])[0]
            a_ref[...] = a.astype(a_ref.dtype)
            if ns:
                refs[n_in + 2][...] = _dot(a_ref[...], refs[nr + npar + 1 + ne][...]).astype(refs[n_in + 2].dtype)

        if nj == 1:
            make_a()
        else:
            pl.when(pl.program_id(1) == 0)(make_a)
        res = _dot(a_ref[...], b_ref[...])
        if epilogue is not None:
            res = epilogue(res, *[e[...] for e in extra_refs])
        o_ref[...] = res.astype(o_ref.dtype)

    o_spec = pl.BlockSpec((tm, tn), lambda i, j: (i, j))
    side_in = [] if side is None else [pl.BlockSpec(tuple(side[0].shape), lambda i, j: (0, 0))]
    side_out = [] if side is None else [pl.BlockSpec((tm, side[0].shape[1]), lambda i, j: (i, 0))]
    side_shape = [] if side is None else [jax.ShapeDtypeStruct((nrows, side[0].shape[1]), side[1])]
    return pl.pallas_call(
        body, name=name, grid=(nrows // tm, nj),
        in_specs=[pl.BlockSpec((tm, w), lambda i, j, cb=cb: (i, cb)) for _, w, cb in rows]
        + [pl.BlockSpec(tuple(p.shape), lambda i, j, nd=p.ndim: (0,) * nd) for p in pars]
        + [pl.BlockSpec((k, tn), lambda i, j: (0, j))] + [o_spec] * ne + side_in + [ANY] * len(after),
        out_specs=[pl.BlockSpec((tm, k), lambda i, j: (i, 0)), o_spec] + side_out,
        out_shape=[jax.ShapeDtypeStruct((nrows, k), ACT_DTYPE), jax.ShapeDtypeStruct((nrows, n), out_dtype)] + side_shape,
        compiler_params=pltpu.CompilerParams(dimension_semantics=("parallel", "arbitrary")),
    )(*[r[0] for r in rows], *pars, b, *extras, *([] if side is None else [side[0]]), *after)


def _row_spec(tm, width, cb):
    return pl.BlockSpec((tm, width), lambda i: (i, cb))


def _whole_spec(shape):
    nd = len(shape)
    return pl.BlockSpec(tuple(shape), lambda i: (0,) * nd)


def _rows_call(name, f, rows, pars, outs, *, tm, nrows):
    tm = min(tm, nrows)
    nr, npar = len(rows), len(pars)

    def body(*refs):
        rv = [r[...].astype(F32) for r in refs[:nr]]
        pv = [p[...] for p in refs[nr:nr + npar]]
        res = f(*rv, *pv)
        for o_ref, r in zip(refs[nr + npar:], res):
            o_ref[...] = r.astype(o_ref.dtype)

    return pl.pallas_call(
        body, name=name, grid=(nrows // tm,),
        in_specs=[_row_spec(tm, w, cb) for _, w, cb in rows] + [_whole_spec(p.shape) for p in pars],
        out_specs=[_row_spec(tm, w, 0) for w, _ in outs],
        out_shape=[jax.ShapeDtypeStruct((nrows, w), dt) for w, dt in outs],
        compiler_params=pltpu.CompilerParams(dimension_semantics=("parallel",)),
    )(*[r[0] for r in rows], *pars)


def _rows_vjp_call(name, f, rows, pars, cots, drows, *, tm, nrows, cot_mm=None):
    tm = min(tm, nrows)
    nr, npar, nc = len(rows), len(pars), len(cots)
    mm_args, mm_specs = [], []
    if cot_mm is not None:
        mm_a, mm_b, mm_add = cot_mm
        mm_args = [mm_a, mm_b] + ([] if mm_add is None else [mm_add])
        mm_specs = [_row_spec(tm, mm_a.shape[1], 0), _whole_spec(mm_b.shape)]
        mm_specs += [] if mm_add is None else [_row_spec(tm, mm_b.shape[0], 0)]
    alias_bufs, aliases = [], {}
    out_shape, out_specs = [], []
    for (arr, w, cb), d in zip(rows, drows):
        if d is None:
            continue
        dt, into = d
        if into is None:
            out_shape.append(jax.ShapeDtypeStruct((nrows, w), dt))
            out_specs.append(_row_spec(tm, w, 0))
        else:
            buf, total, ocb = into
            if buf is not None:
                aliases[nr + npar + nc + len(alias_bufs)] = len(out_shape)
                alias_bufs.append(buf)
            out_shape.append(jax.ShapeDtypeStruct((nrows, total), dt))
            out_specs.append(_row_spec(tm, w, ocb))
    n_drow = len(out_shape)
    for p in pars:
        out_shape.append(jax.ShapeDtypeStruct(p.shape, F32))
        out_specs.append(_whole_spec(p.shape))
    na = len(alias_bufs)

    def body(*refs):
        rv = [r[...].astype(F32) for r in refs[:nr]]
        pv = [p[...] for p in refs[nr:nr + npar]]
        cv = tuple(c[...].astype(F32) for c in refs[nr + npar:nr + npar + nc])
        o_refs = refs[nr + npar + nc + na + len(mm_args):]
        if mm_args:
            mm_refs = refs[nr + npar + nc + na:nr + npar + nc + na + len(mm_args)]
            c0 = _dot(mm_refs[0][...], mm_refs[1][...], (((1,), (1,)), ((), ())))
            if len(mm_refs) == 3:
                c0 = c0 + mm_refs[2][...].astype(F32)
            cv = (c0,) + cv
        _, vjp = jax.vjp(f, *rv, *pv)
        g = vjp(cv)
        oi = 0
        for ri, d in enumerate(drows):
            if d is not None:
                o_refs[oi][...] = g[ri].astype(o_refs[oi].dtype)
                oi += 1
        first = pl.program_id(0) == 0
        for pi in range(npar):
            acc = o_refs[n_drow + pi]

            @pl.when(first)
            def _(acc=acc):
                acc[...] = jnp.zeros_like(acc)

            acc[...] += g[nr + pi]

    return pl.pallas_call(
        body, name=name, grid=(nrows // tm,),
        in_specs=[_row_spec(tm, w, cb) for _, w, cb in rows] + [_whole_spec(p.shape) for p in pars]
        + [_row_spec(tm, w, cb) for _, w, cb in cots] + [ANY] * na + mm_specs,
        out_specs=out_specs, out_shape=out_shape, input_output_aliases=aliases,
        compiler_params=pltpu.CompilerParams(dimension_semantics=("arbitrary",)),
    )(*[r[0] for r in rows], *pars, *[c[0] for c in cots], *alias_bufs, *mm_args)


def _rms(x, g):
    return x * lax.rsqrt(jnp.mean(x * x, axis=-1, keepdims=True) + EPS) * g


def _f_rms(x, g):
    return (_rms(x, g),)


def _tril_mask():
    return lax.broadcasted_iota(jnp.int32, (CHUNK, CHUNK), 0) >= lax.broadcasted_iota(jnp.int32, (CHUNK, CHUNK), 1)


def _f_branch_a(uvz, ln_g, ln_b, w_s, b_s):
    u = jax.nn.gelu(uvz[:, :E_A])
    v = jax.nn.gelu(uvz[:, E_A:2 * E_A])
    z = uvz[:, 2 * E_A:]
    xc = v - jnp.mean(v, axis=-1, keepdims=True)
    vn = xc * lax.rsqrt(jnp.mean(xc * xc, axis=-1, keepdims=True) + EPS) * ln_g + ln_b
    mask = _tril_mask()
    ws = [jnp.where(mask, w_s[g], 0.0) for g in range(G_A)]
    gw = E_A // G_A
    rows = []
    for c in range(uvz.shape[0] // CHUNK):
        vc = vn[c * CHUNK:(c + 1) * CHUNK]
        rows.append(jnp.concatenate([_dot(ws[g], vc[:, g * gw:(g + 1) * gw]) + b_s[g] for g in range(G_A)], axis=1))
    sv = rows[0] if len(rows) == 1 else jnp.concatenate(rows, axis=0)
    return (u * sv * jax.nn.silu(z),)


def _f_gnorm(y, zb, g):
    yz = y * jax.nn.silu(zb)
    gw = D_INNER // N_GROUPS
    parts = []
    for i in range(N_GROUPS):
        s = yz[:, i * gw:(i + 1) * gw]
        parts.append(s * lax.rsqrt(jnp.mean(s * s, axis=-1, keepdims=True) + EPS))
    return (jnp.concatenate(parts, axis=1) * g,)


def _f_merge(g2, oa, ob):
    return (jax.nn.sigmoid(g2[:, :D_MODEL]) * oa + jax.nn.sigmoid(g2[:, D_MODEL:]) * ob,)


def _f_loss(x1, gp, pe, tgt, fg):
    x2 = x1 + jax.nn.sigmoid(gp) * pe
    err = _rms(x2, fg) - tgt
    return 0.5 * jnp.sum(jnp.mean(err * err, axis=-1))


def _head(x1, p, tgt, ple_g, w_pg, w_ple, fg, *, tm, nrows):
    tm = min(tm, nrows)

    def body(x1_ref, p_ref, t_ref, pg_ref, wpg_ref, wple_ref, fg_ref, hp_ref, dx_ref, dgp_ref, dpe_ref, dfg_ref, loss_ref):
        x1 = x1_ref[...]
        hp_ref[...] = _rms(x1, pg_ref[...]).astype(hp_ref.dtype)
        gp = _dot(hp_ref[...], wpg_ref[...])
        pe = _dot(p_ref[...], wple_ref[...])
        loss, vjp = jax.vjp(_f_loss, x1, gp, pe, t_ref[...], fg_ref[...])
        dx, dgp, dpe, _, dfg = vjp(jnp.ones((), F32))
        dx_ref[...] = dx
        dgp_ref[...] = dgp.astype(dgp_ref.dtype)
        dpe_ref[...] = dpe.astype(dpe_ref.dtype)

        @pl.when(pl.program_id(0) == 0)
        def _():
            dfg_ref[...] = jnp.zeros_like(dfg_ref)
            loss_ref[...] = jnp.zeros_like(loss_ref)

        dfg_ref[...] += dfg
        loss_ref[...] += jnp.full(loss_ref.shape, loss, F32)

    row = _row_spec(tm, D_MODEL, 0)
    act = jax.ShapeDtypeStruct((nrows, D_MODEL), ACT_DTYPE)
    return pl.pallas_call(
        body, name="head", grid=(nrows // tm,),
        in_specs=[row, _row_spec(tm, PLE_DIM, 0), row, _whole_spec((1, D_MODEL)), _whole_spec(w_pg.shape),
                  _whole_spec(w_ple.shape), _whole_spec((1, D_MODEL))],
        out_specs=[row, row, row, row, _whole_spec((1, D_MODEL)), _whole_spec((1, 128))],
        out_shape=[act, jax.ShapeDtypeStruct((nrows, D_MODEL), F32), act, act, jax.ShapeDtypeStruct((1, D_MODEL), F32),
                   jax.ShapeDtypeStruct((1, 128), F32)],
        compiler_params=pltpu.CompilerParams(dimension_semantics=("arbitrary",)),
    )(x1, p, tgt, ple_g, w_pg, w_ple, fg)


def _shift_rows(cur, edge, j, up):
    tm = cur.shape[0]
    row = lax.broadcasted_iota(jnp.int32, cur.shape, 0)
    if up:
        sh = pltpu.roll(cur, tm - j, 0)
        e = jnp.tile(pltpu.roll(edge, 8 - j, 0), (tm // 8, 1))
        return jnp.where(row >= tm - j, e, sh)
    sh = pltpu.roll(cur, j, 0)
    e = jnp.tile(pltpu.roll(edge, j, 0), (tm // 8, 1))
    return jnp.where(row < j, e, sh)


def _conv_pre(cur, prev, w, b):
    acc = cur * w[CONV_K - 1:CONV_K] + b
    for j in range(1, CONV_K):
        acc = acc + _shift_rows(cur, prev, j, up=False) * w[CONV_K - 1 - j:CONV_K - j]
    return acc


def _halo_specs(tm, nrows, cb, before):
    nb = tm // 8
    last = nrows // 8 - 1
    if before:
        return pl.BlockSpec((8, XBC_W), lambda i: (jnp.maximum(i * nb - 1, 0), cb))
    return pl.BlockSpec((8, XBC_W), lambda i: (jnp.minimum((i + 1) * nb, last), cb))


def _conv_fwd(proj, conv_w, conv_b, *, tm, nrows):
    tm = min(tm, nrows)

    def body(cur_ref, prev_ref, w_ref, b_ref, o_ref, pre_ref):
        prev = jnp.where(pl.program_id(0) == 0, 0.0, prev_ref[...].astype(F32))
        pre = _conv_pre(cur_ref[...].astype(F32), prev, w_ref[...], b_ref[...])
        o_ref[...] = jax.nn.silu(pre).astype(o_ref.dtype)
        pre_ref[...] = pre.astype(pre_ref.dtype)

    out = jax.ShapeDtypeStruct((nrows, XBC_W), ACT_DTYPE)
    return pl.pallas_call(
        body, name="conv_fwd", grid=(nrows // tm,),
        in_specs=[_row_spec(tm, XBC_W, XBC_CB), _halo_specs(tm, nrows, XBC_CB, True),
                  _whole_spec((CONV_K, XBC_W)), _whole_spec((1, XBC_W))],
        out_specs=[_row_spec(tm, XBC_W, 0)] * 2, out_shape=[out, out],
        compiler_params=pltpu.CompilerParams(dimension_semantics=("parallel",)),
    )(proj, proj, conv_w, conv_b)


def _conv_bwd_act(pre, dact, *, tm, nrows):
    tm = min(tm, nrows)
    nb = N_GROUPS * N_STATE

    def body(pre_ref, dxs_ref, dbm_ref, dcm_ref, dpre_ref, db_ref):
        pre = pre_ref[...].astype(F32)
        sg = jax.nn.sigmoid(pre)
        dy = jnp.concatenate([dxs_ref[...], dbm_ref[...], dcm_ref[...]], axis=1).astype(F32)
        dpre = dy * sg * (1.0 + pre * (1.0 - sg))
        dpre_ref[...] = dpre.astype(dpre_ref.dtype)

        @pl.when(pl.program_id(0) == 0)
        def _():
            db_ref[...] = jnp.zeros_like(db_ref)

        db_ref[...] += jnp.sum(dpre, axis=0, keepdims=True)

    return pl.pallas_call(
        body, name="conv_bwd_act", grid=(nrows // tm,),
        in_specs=[_row_spec(tm, XBC_W, 0), _row_spec(tm, D_INNER, 0), _row_spec(tm, nb, 0), _row_spec(tm, nb, 0)],
        out_specs=[_row_spec(tm, XBC_W, 0), _whole_spec((1, XBC_W))],
        out_shape=[jax.ShapeDtypeStruct((nrows, XBC_W), ACT_DTYPE), jax.ShapeDtypeStruct((1, XBC_W), F32)],
        compiler_params=pltpu.CompilerParams(dimension_semantics=("arbitrary",)),
    )(pre, *dact)


def _conv_bwd_x(dpre, proj, conv_w, dproj, *, tm, nrows):
    tm = min(tm, nrows)
    ntiles = nrows // tm

    def body(cur_ref, nxt_ref, x_ref, w_ref, _, o_ref, dw_ref):
        cur = cur_ref[...].astype(F32)
        nxt = jnp.where(pl.program_id(0) == ntiles - 1, 0.0, nxt_ref[...].astype(F32))
        x = x_ref[...].astype(F32)
        w = w_ref[...]

        @pl.when(pl.program_id(0) == 0)
        def _():
            dw_ref[...] = jnp.zeros_like(dw_ref)

        acc = cur * w[CONV_K - 1:CONV_K]
        dw_ref[CONV_K - 1:CONV_K, :] += jnp.sum(cur * x, axis=0, keepdims=True)
        for j in range(1, CONV_K):
            u = _shift_rows(cur, nxt, j, up=True)
            acc = acc + u * w[CONV_K - 1 - j:CONV_K - j]
            dw_ref[CONV_K - 1 - j:CONV_K - j, :] += jnp.sum(u * x, axis=0, keepdims=True)
        o_ref[...] = acc.astype(o_ref.dtype)

    return pl.pallas_call(
        body, name="conv_bwd_x", grid=(ntiles,),
        in_specs=[_row_spec(tm, XBC_W, 0), _halo_specs(tm, nrows, 0, False), _row_spec(tm, XBC_W, XBC_CB),
                  _whole_spec((CONV_K, XBC_W)), ANY],
        out_specs=[_row_spec(tm, XBC_W, XBC_CB), _whole_spec((CONV_K, XBC_W))],
        out_shape=[jax.ShapeDtypeStruct(dproj.shape, dproj.dtype), jax.ShapeDtypeStruct((CONV_K, XBC_W), F32)],
        input_output_aliases={4: 0},
        compiler_params=pltpu.CompilerParams(dimension_semantics=("arbitrary",)),
    )(dpre, dpre, proj, conv_w, dproj)


SSD_SPAN = 4
SSD_FWD_SPANS = 2
_XS_GW = D_INNER // N_GROUPS
_NT = (((1,), (1,)), ((), ()))
_TN = (((0,), (0,)), ((), ()))


def _bf16_terms(x, terms):
    parts, rest = [], x
    for _ in range(terms):
        part = rest.astype(jnp.bfloat16)
        parts.append(part)
        rest = rest - part.astype(F32)
    return parts


def _head_lane_matrix():
    return (lax.broadcasted_iota(jnp.int32, (128, _XS_GW), 0)
            == lax.broadcasted_iota(jnp.int32, (128, _XS_GW), 1) // HEAD_DIM).astype(jnp.bfloat16)


@functools.partial(jax.custom_vjp, nondiff_argnums=(1,))
def _head_lanes(cols, terms):
    e = _head_lane_matrix()
    return sum(jnp.dot(t, e, preferred_element_type=F32) for t in _bf16_terms(cols, terms))


def _head_lanes_fwd(cols, terms):
    return _head_lanes(cols, terms), None


def _head_lanes_bwd(terms, _, g):
    e = _head_lane_matrix()
    return (sum(lax.dot_general(t, e, _NT, preferred_element_type=F32) for t in _bf16_terms(g, 2)),)


_head_lanes.defvjp(_head_lanes_fwd, _head_lanes_bwd)


def _ssd_chunk(k, xs, bm, cm, dtr, hprev, dtb, alog, dsk):
    causal, tri, lo = k
    dt = jax.nn.softplus(dtr + dtb)
    da = dt * (-jnp.exp(alog))
    cs = jnp.dot(tri, da, precision=lax.Precision.HIGHEST, preferred_element_type=F32)
    cst = cs.T
    cs_l = _head_lanes(cs, 3)
    xdt = xs * _head_lanes(dt, 2)
    cb = _dot(cm, bm, _NT)
    yd = []
    for q in range(PAIRS_PER_GROUP):
        xq = xdt[:, 128 * q:128 * (q + 1)]
        y2 = [_dot(cb * jnp.exp(jnp.where(causal, cs[:, h:h + 1] - cst[h:h + 1, :], -jnp.inf)), xq)
              for h in (2 * q, 2 * q + 1)]
        yd.append(jnp.where(lo, y2[0], y2[1]))
    y_off = jnp.exp(cs_l) * _dot(cm, hprev, _NT)
    st = _dot(xdt * jnp.exp(cs_l[CHUNK - 1:CHUNK, :] - cs_l), bm, _TN)
    cdec = jnp.exp(cs[CHUNK - 1:CHUNK, :])
    cd_rows = jnp.concatenate(
        [jnp.broadcast_to(cdec[:, h:h + 1], (HEAD_DIM, N_STATE)) for h in range(HEADS_PER_GROUP)], axis=0)
    dsk_l = _head_lanes(jnp.broadcast_to(dsk, (8, 128)), 2)[:1]
    y = jnp.concatenate(yd, axis=1) + y_off + xs * dsk_l
    return y, cd_rows * hprev + st


def _ssd_span(xs, bm, cm, dtr, h0, dtb, alog, dsk):
    li = lax.broadcasted_iota(jnp.int32, (CHUNK, CHUNK), 0)
    si = lax.broadcasted_iota(jnp.int32, (CHUNK, CHUNK), 1)
    causal = li >= si
    k = (causal, causal.astype(F32), si < HEAD_DIM)
    h, ys = h0, []
    for t in range(xs.shape[0] // CHUNK):
        r = slice(t * CHUNK, (t + 1) * CHUNK)
        y, h = _ssd_chunk(k, xs[r], bm[r], cm[r], dtr[r], h, dtb, alog, dsk)
        ys.append(y)
    return (ys[0] if len(ys) == 1 else jnp.concatenate(ys, axis=0)), h


def _ssd_specs(rev, nsteps, rows):
    def s_of(s):
        return nsteps - 1 - s if rev else s

    xs = pl.BlockSpec((rows, _XS_GW), lambda g, s: (s_of(s), g))
    bm = pl.BlockSpec((rows, N_STATE), lambda g, s: (s_of(s), D_INNER // N_STATE + g))
    cm = pl.BlockSpec((rows, N_STATE), lambda g, s: (s_of(s), D_INNER // N_STATE + N_GROUPS + g))
    dt = pl.BlockSpec((rows, 128), lambda g, s: (s_of(s), g))
    par = pl.BlockSpec((1, 128), lambda g, s: (0, g))
    st = pl.BlockSpec((None, None, _XS_GW, N_STATE), lambda g, s: (g, s_of(s), 0, 0))
    return xs, bm, cm, dt, par, st


def _ssd_fwd(act, dtr, dtb, alog, dsk, *, nrows):
    span = CHUNK * min(SSD_SPAN, nrows // CHUNK)
    per_step = SSD_FWD_SPANS if nrows % (SSD_FWD_SPANS * span) == 0 else 1
    rows = per_step * span
    nsteps = nrows // rows
    xs, bm, cm, dt, par, _ = _ssd_specs(False, nsteps, rows)
    st = pl.BlockSpec((None, per_step, _XS_GW, N_STATE), lambda g, s: (g, s, 0, 0))

    def body(xs_ref, b_ref, c_ref, dt_ref, dtb_ref, al_ref, dk_ref, y_ref, st_ref, h_ref):
        @pl.when(pl.program_id(1) == 0)
        def _():
            h_ref[...] = jnp.zeros_like(h_ref)

        h = h_ref[...]
        for i in range(per_step):
            r = slice(i * span, (i + 1) * span)
            st_ref[i] = h
            y, h = _ssd_span(xs_ref[r, :].astype(F32), b_ref[r, :].astype(F32), c_ref[r, :].astype(F32), dt_ref[r, :],
                             h, dtb_ref[...], al_ref[...], dk_ref[...])
            y_ref[r, :] = y.astype(y_ref.dtype)
        h_ref[...] = h

    return pl.pallas_call(
        body, name="ssd_fwd", grid=(N_GROUPS, nsteps),
        in_specs=[xs, bm, cm, dt, par, par, par], out_specs=[xs, st],
        out_shape=[jax.ShapeDtypeStruct((nrows, D_INNER), ACT_DTYPE),
                   jax.ShapeDtypeStruct((N_GROUPS, nrows // span, _XS_GW, N_STATE), F32)],
        scratch_shapes=[pltpu.VMEM((_XS_GW, N_STATE), F32)],
        compiler_params=pltpu.CompilerParams(dimension_semantics=("arbitrary", "arbitrary")),
    )(act, act, act, dtr, dtb, alog, dsk)


def _ssd_bwd(act, dtr, dtb, alog, dsk, states, dy, *, nrows):
    rows = CHUNK * min(SSD_SPAN, nrows // CHUNK)
    nsteps = nrows // rows
    xs, bm, cm, dt, par, st = _ssd_specs(True, nsteps, rows)

    def body(xs_ref, b_ref, c_ref, dt_ref, dtb_ref, al_ref, dk_ref, st_ref, dy_ref,
             dxs_ref, db_ref, dc_ref, ddt_ref, ddtb_ref, dal_ref, ddk_ref, dh_ref):
        @pl.when(pl.program_id(1) == 0)
        def _():
            dh_ref[...] = jnp.zeros_like(dh_ref)
            ddtb_ref[...] = jnp.zeros_like(ddtb_ref)
            dal_ref[...] = jnp.zeros_like(dal_ref)
            ddk_ref[...] = jnp.zeros_like(ddk_ref)

        _, vjp = jax.vjp(_ssd_span, xs_ref[...].astype(F32), b_ref[...].astype(F32), c_ref[...].astype(F32),
                         dt_ref[...], st_ref[...], dtb_ref[...], al_ref[...], dk_ref[...])
        dxs, db, dc, ddt, dh, ddtb, dal, ddk = vjp((dy_ref[...].astype(F32), dh_ref[...]))
        dxs_ref[...] = dxs.astype(dxs_ref.dtype)
        db_ref[...] = db.astype(db_ref.dtype)
        dc_ref[...] = dc.astype(dc_ref.dtype)
        ddt_ref[...] = ddt
        dh_ref[...] = dh
        ddtb_ref[...] += ddtb
        dal_ref[...] += dal
        ddk_ref[...] += ddk

    nb = N_GROUPS * N_STATE
    bspec = pl.BlockSpec((rows, N_STATE), lambda g, s: (nsteps - 1 - s, g))
    return pl.pallas_call(
        body, name="ssd_bwd", grid=(N_GROUPS, nsteps),
        in_specs=[xs, bm, cm, dt, par, par, par, st, xs],
        out_specs=[xs, bspec, bspec, dt, par, par, par],
        out_shape=[jax.ShapeDtypeStruct((nrows, D_INNER), ACT_DTYPE), jax.ShapeDtypeStruct((nrows, nb), ACT_DTYPE),
                   jax.ShapeDtypeStruct((nrows, nb), ACT_DTYPE), jax.ShapeDtypeStruct((nrows, DT_W), F32),
                   jax.ShapeDtypeStruct((1, DT_W), F32), jax.ShapeDtypeStruct((1, DT_W), F32),
                   jax.ShapeDtypeStruct((1, DT_W), F32)],
        scratch_shapes=[pltpu.VMEM((_XS_GW, N_STATE), F32)],
        compiler_params=pltpu.CompilerParams(dimension_semantics=("arbitrary", "arbitrary")),
    )(act, act, act, dtr, dtb, alog, dsk, states, dy)


def _add_epilogue(acc, r):
    return r + acc


def _rms_and_skip(x, g):
    return _rms(x, g), x


def _forward_backward(x, p, tgt, w, late_weights=None, after=()):
    s = x.shape[0]
    act_t, f32 = ACT_DTYPE, F32
    mm = functools.partial(_matmul)
    h, proj, dtr = _rows_matmul("proj", _f_rms, [(x, D_MODEL, 0)], [w["norm_g"]], w["w_main"], out_dtype=act_t,
                                n=MAIN_W, k=D_MODEL, tm=1024, nrows=s, after=after, side=(w["w_dt"], f32))
    act, conv_pre = _conv_fwd(proj, w["conv_w"], w["conv_b"], tm=512, nrows=s)
    y, states = _ssd_fwd(act, dtr, w["dt_bias"], w["a_log"], w["d_skip"], nrows=s)
    if late_weights is not None:
        w = {**w, **late_weights(states)}
    a_pars = [w["ln_a_g"], w["ln_a_b"], w["w_s"], w["b_s"]]
    y_a, o_a = _rows_matmul("out_a", _f_branch_a, [(proj, UVZ_W, UVZ_CB)], a_pars, w["w_oa"], out_dtype=act_t,
                            n=D_MODEL, k=E_A, tm=256, nrows=s)
    gn_rows = [(y, D_INNER, 0), (proj, ZB_W, ZB_CB)]
    y_b, o_b = _rows_matmul("out_b", _f_gnorm, gn_rows, [w["ssm_norm_g"]], w["w_ob"], out_dtype=act_t,
                            n=D_MODEL, k=D_INNER, tm=512, nrows=s)
    mg_rows = [(proj, G_W, G_CB), (o_a, D_MODEL, 0), (o_b, D_MODEL, 0)]
    merged, x1 = _rows_matmul("out_proj", _f_merge, mg_rows, [], w["w_out"], out_dtype=f32, n=D_MODEL, k=D_MODEL,
                              tm=512, nrows=s, extras=(x,), epilogue=_add_epilogue)
    g = {}
    hp, dx2, dgp, dpe, g["final_g"], loss = _head(x1, p, tgt, w["ple_norm_g"], w["w_pg"], w["w_ple"], w["final_g"],
                                                   tm=256, nrows=s)
    g["w_pg"] = mm(hp, dgp, mode="tn", name="d_w_pg", out_dtype=f32, m=D_MODEL, n=D_MODEL, k=s)
    g["w_ple"] = mm(p, dpe, mode="tn", name="d_w_ple", out_dtype=f32, m=PLE_DIM, n=D_MODEL, k=s)
    dx1, g["ple_norm_g"] = _rows_vjp_call(
        "ple_norm_bwd", _rms_and_skip, [(x1, D_MODEL, 0)], [w["ple_norm_g"]], [(dx2, D_MODEL, 0)],
        [(f32, None)], tm=512, nrows=s, cot_mm=(dgp, w["w_pg"], None))
    g["w_out"] = mm(merged, dx1, mode="tn", name="d_w_out", out_dtype=f32, m=D_MODEL, n=D_MODEL, k=s)
    dproj, do_a, do_b = _rows_vjp_call(
        "merge_bwd", _f_merge, mg_rows, [], [],
        [(act_t, (None, MAIN_W, G_CB)), (act_t, None), (act_t, None)], tm=512, nrows=s, cot_mm=(dx1, w["w_out"], None))
    g["w_oa"] = mm(y_a, do_a, mode="tn", name="d_w_oa", out_dtype=f32, m=E_A, n=D_MODEL, k=s)
    g["w_ob"] = mm(y_b, do_b, mode="tn", name="d_w_ob", out_dtype=f32, m=D_INNER, n=D_MODEL, k=s)
    dy, dproj, g["ssm_norm_g"] = _rows_vjp_call(
        "gnorm_bwd", _f_gnorm, gn_rows, [w["ssm_norm_g"]], [],
        [(act_t, None), (act_t, (dproj, MAIN_W, ZB_CB))], tm=512, nrows=s, cot_mm=(do_b, w["w_ob"], None))
    dxs, dbm, dcm, ddtr, g["dt_bias"], g["a_log"], g["d_skip"] = _ssd_bwd(
        act, dtr, w["dt_bias"], w["a_log"], w["d_skip"], states, dy, nrows=s)
    dpre, g["conv_b"] = _conv_bwd_act(conv_pre, (dxs, dbm, dcm), tm=512, nrows=s)
    dproj, g["conv_w"] = _conv_bwd_x(dpre, proj, w["conv_w"], dproj, tm=512, nrows=s)
    dproj, g["ln_a_g"], g["ln_a_b"], g["w_s"], g["b_s"] = _rows_vjp_call(
        "branch_a_bwd", _f_branch_a, [(proj, UVZ_W, UVZ_CB)], a_pars, [],
        [(act_t, (dproj, MAIN_W, UVZ_CB))], tm=256, nrows=s, cot_mm=(do_a, w["w_oa"], None))
    g["w_main"] = mm(h, dproj, mode="tn", name="d_w_main", out_dtype=f32, m=D_MODEL, n=MAIN_W, k=s)
    g["w_dt"] = mm(h, ddtr, mode="tn", name="d_w_dt", out_dtype=f32, m=D_MODEL, n=DT_W, k=s)
    return loss, g, (dproj, ddtr, dx1)


def _input_grad(x, w, ctx, after=()):
    dproj, ddtr, dx1 = ctx
    s = x.shape[0]
    dh = _matmul(dproj, w["w_main"], mode="nt", name="d_h_main", out_dtype=F32, m=s, n=D_MODEL, k=MAIN_W, after=after)
    return _rows_vjp_call(
        "pre_norm_bwd", _rms_and_skip, [(x, D_MODEL, 0)], [w["norm_g"]], [(dx1, D_MODEL, 0)],
        [(F32, None)], tm=512, nrows=s, cot_mm=(ddtr, w["w_dt"], dh))


def _local_step(x, p, tgt, w):
    loss, g, ctx = _forward_backward(x, p, tgt, w)
    grad_x, g["norm_g"] = _input_grad(x, w, ctx)
    return loss, grad_x, g


_O_ZB = 3 * E_A
_O_XBC = _O_ZB + D_INNER
_O_DT = _O_XBC + CONV_DIM
_O_G = _O_DT + N_HEADS


def _heads_to_lanes(v):
    r = v.shape[0]
    v = v.reshape(r, N_GROUPS, HEADS_PER_GROUP)
    return jnp.pad(v, ((0, 0), (0, 0), (0, 128 - HEADS_PER_GROUP))).reshape(r, DT_W)


def _lanes_to_heads(v):
    r = v.shape[0]
    return v.reshape(r, N_GROUPS, 128)[:, :, :HEADS_PER_GROUP].reshape(r, N_HEADS)


def _block_cols(blocks, a, b):
    parts = []
    for k in range(N_CHIPS):
        lo, hi = max(a, k * W_IN_BLOCK), min(b, (k + 1) * W_IN_BLOCK)
        if lo < hi:
            parts.append(blocks[k][:, lo - k * W_IN_BLOCK:hi - k * W_IN_BLOCK])
    return parts


_W_IN_SEGMENTS = ((0, _O_ZB, "m", 0), (_O_ZB, _O_XBC, "m", UVZ_W + XBC_W), (_O_XBC, _O_DT, "m", UVZ_W),
                  (_O_DT, _O_G, "d", 0), (_O_G, N_IN, "m", MAIN_W - G_W))


def _w_in_grad_blocks(gm, gdt):
    blocks = []
    for k in range(N_CHIPS):
        a, b = k * W_IN_BLOCK, (k + 1) * W_IN_BLOCK
        parts = []
        for s, e, src, off in _W_IN_SEGMENTS:
            lo, hi = max(a, s), min(b, e)
            if lo < hi:
                parts.append((gm if src == "m" else gdt)[:, off + lo - s:off + hi - s])
        blocks.append(jnp.concatenate(parts, axis=1))
    return jnp.stack(blocks)


def _layout_weights(f, w_in_blocks=None):
    w = dict(f)
    if w_in_blocks is None:
        w_in = w.pop("w_in")
        w_in_blocks = jnp.stack([w_in[:, k * W_IN_BLOCK:(k + 1) * W_IN_BLOCK] for k in range(N_CHIPS)])
    cols = functools.partial(_block_cols, w_in_blocks)
    w["w_main"] = jnp.concatenate(cols(0, _O_ZB) + cols(_O_XBC, _O_DT) + cols(_O_ZB, _O_XBC) + cols(_O_G, N_IN), axis=1)
    w["w_dt"] = _heads_to_lanes(jnp.concatenate(cols(_O_DT, _O_G), axis=1))
    w["b_s"] = f["b_s"].reshape(G_A, CHUNK, 1)
    for n in ("dt_bias", "a_log", "d_skip"):
        w[n] = _heads_to_lanes(f[n])
    return w


def _natural_grads(g):
    out = dict(g)
    gm = out.pop("w_main")
    gdt = _lanes_to_heads(out.pop("w_dt"))
    out["w_in"] = jnp.concatenate(
        [gm[:, :UVZ_W], gm[:, UVZ_W + XBC_W:UVZ_W + XBC_W + ZB_W], gm[:, UVZ_W:UVZ_W + XBC_W], gdt, gm[:, MAIN_W - G_W:]],
        axis=1)
    out["b_s"] = g["b_s"].reshape(G_A, CHUNK)
    for n in ("dt_bias", "a_log", "d_skip"):
        out[n] = _lanes_to_heads(g[n])
    return out


def _place():
    return lax.axis_index("x"), lax.axis_index("y"), lax.axis_index("c")


def _other_chips(x, y):
    return [(1 - x, y), (x, 1 - y), (1 - x, 1 - y)]


def _rcopy(src, dst, ssem, rsem, dev):
    return pltpu.make_async_remote_copy(src_ref=src, dst_ref=dst, send_sem=ssem, recv_sem=rsem,
                                        device_id=dev, device_id_type=MESH)


def _half(ref_rows, half):
    hs = ref_rows // 2
    return pl.ds(pl.multiple_of(half * hs, 16), hs)


def _gather_weights(shards, conv_shard):
    nw = len(shards)

    def body(*refs):
        sh, cv = refs[:nw], refs[nw]
        out, cvo = refs[nw + 1:2 * nw + 1], refs[2 * nw + 1]
        ici_s, ici_r, fw_s, fw_r, own_s, own_r, cv_s, cv_r = refs[2 * nw + 2:]
        x, y, c = _place()
        me, sib, chips = 2 * x + y, (x, y, 1 - c), _other_chips(x, y)
        own = [_rcopy(sh[w], out[w].at[me], own_s.at[w], own_r.at[w], sib) for w in range(nw)]
        own.append(_rcopy(cv, cvo.at[me], own_s.at[nw], own_r.at[nw], sib))
        for cp in own:
            cp.start()
        sends = []
        for w in range(nw):
            mine = _half(sh[w].shape[0], c)
            for j, chip in enumerate(chips):
                sends.append(_rcopy(sh[w].at[mine], out[w].at[me, mine], ici_s.at[3 * w + j], ici_r.at[3 * w + j], (*chip, c)))
        for j, chip in enumerate(chips):
            sends.append(_rcopy(cv, cvo.at[me], cv_s.at[j], cv_r.at[j], (*chip, c)))
        for cp in sends:
            cp.start()
        for w in range(nw):
            mine = _half(sh[w].shape[0], c)
            for j, chip in enumerate(chips):
                slab = out[w].at[2 * chip[0] + chip[1], mine]
                _rcopy(slab, slab, ici_s.at[3 * w + j], ici_r.at[3 * w + j], (*chip, c)).wait_recv()
                fwd = _rcopy(slab, slab, fw_s.at[3 * w + j], fw_r.at[3 * w + j], sib)
                fwd.start()
                sends.append(fwd)
        for j, chip in enumerate(chips):
            blk = cvo.at[2 * chip[0] + chip[1]]
            _rcopy(blk, blk, cv_s.at[j], cv_r.at[j], (*chip, c)).wait_recv()
        for w in range(nw):
            theirs = _half(sh[w].shape[0], 1 - c)
            for j, chip in enumerate(chips):
                slab = out[w].at[2 * chip[0] + chip[1], theirs]
                _rcopy(slab, slab, fw_s.at[3 * w + j], fw_r.at[3 * w + j], sib).wait_recv()
        for cp in sends:
            cp.wait_send()
        for cp in own:
            cp.wait()

    dma = pltpu.SemaphoreType.DMA
    return pl.pallas_call(
        body, name="gather_weights",
        in_specs=[ANY] * (nw + 1), out_specs=[ANY] * (nw + 1),
        out_shape=[jax.ShapeDtypeStruct((N_CHIPS,) + s.shape, s.dtype) for s in shards]
        + [jax.ShapeDtypeStruct((N_CHIPS,) + conv_shard.shape, conv_shard.dtype)],
        scratch_shapes=[dma((3 * nw,)), dma((3 * nw,)), dma((3 * nw,)), dma((3 * nw,)), dma((nw + 1,)), dma((nw + 1,)),
                        dma((3,)), dma((3,))],
    )(*shards, conv_shard)


_HBM = pl.BlockSpec(memory_space=pltpu.HBM)
_SEM = pl.BlockSpec(memory_space=pltpu.SEMAPHORE)
_EFFECT = pltpu.SideEffectType.DATAFLOW_SIDE_EFFECTING


def _late_gather_copies(sh, out, s_sem, r_sem):
    x, y, c = _place()
    to = [(*chip, c) for chip in _other_chips(x, y)] + [(x, y, 1 - c)]
    return [_rcopy(sh[w], out[w].at[2 * x + y], s_sem.at[4 * w + j], r_sem.at[4 * w + j], dev)
            for w in range(len(sh)) for j, dev in enumerate(to)]


def _late_gather_start(shards):
    n = len(shards)
    lands = [lax.empty((N_CHIPS,) + a.shape, a.dtype) for a in shards]

    def body(*refs):
        for cp in _late_gather_copies(refs[:n], refs[n:2 * n], refs[2 * n], refs[2 * n + 1]):
            cp.start()
        refs[-1][...] = jnp.zeros_like(refs[-1])

    dma = pltpu.SemaphoreType.DMA
    hbm = [pltpu.with_memory_space_constraint(a, pltpu.HBM) for a in list(shards) + lands]
    out = pl.pallas_call(
        body, name="late_gather_start",
        out_shape=[dma((4 * n,)), dma((4 * n,))] + [pltpu.HBM(a.shape, a.dtype) for a in hbm]
        + [jax.ShapeDtypeStruct((8, 128), F32)],
        in_specs=[_HBM] * (2 * n), out_specs=[_SEM, _SEM] + [_HBM] * (2 * n) + [pl.BlockSpec(memory_space=pltpu.VMEM)],
        input_output_aliases={i: 2 + i for i in range(2 * n)},
        compiler_params=pltpu.CompilerParams(has_side_effects=_EFFECT),
    )(*hbm)
    return out[0], out[1], out[2:2 + n], out[2 + n:2 + 2 * n], out[-1]


def _late_gather_wait(s_sem, r_sem, srcs, lands, after):
    n = len(srcs)

    def body(*refs):
        for cp in _late_gather_copies(refs[:n], refs[n:2 * n], refs[2 * n], refs[2 * n + 1]):
            cp.wait_send()
            cp.wait_recv()

    out = pl.pallas_call(
        body, name="late_gather_wait",
        out_shape=[pltpu.HBM(a.shape, a.dtype) for a in list(srcs) + list(lands)],
        in_specs=[_HBM] * (2 * n) + [_SEM, _SEM, ANY], out_specs=[_HBM] * (2 * n),
        input_output_aliases={i: i for i in range(2 * n)},
        compiler_params=pltpu.CompilerParams(has_side_effects=_EFFECT),
    )(*srcs, *lands, s_sem, r_sem, after)
    return out[n:]


def _swap_with_sibling(arrs):
    n = len(arrs)

    def body(*refs):
        src, dst, s_sem, r_sem = refs[:n], refs[n:2 * n], refs[2 * n], refs[2 * n + 1]
        x, y, c = _place()
        cps = [_rcopy(src[i], dst[i], s_sem.at[i], r_sem.at[i], (x, y, 1 - c)) for i in range(n)]
        for cp in cps:
            cp.start()
        for cp in cps:
            cp.wait()

    dma = pltpu.SemaphoreType.DMA
    return pl.pallas_call(
        body, name="swap_with_sibling", in_specs=[ANY] * n, out_specs=[ANY] * n,
        out_shape=[jax.ShapeDtypeStruct(a.shape, a.dtype) for a in arrs], scratch_shapes=[dma((n,)), dma((n,))],
    )(*arrs)


def _scatter_copies(src, land, s_sem, r_sem):
    x, y, c = _place()
    return [_rcopy(src[i].at[2 * chip[0] + chip[1]], land[i].at[j], s_sem.at[3 * i + j], r_sem.at[3 * i + j], (*chip, c))
            for i in range(len(src)) for j, chip in enumerate(_other_chips(x, y))]


def _scatter_blocks_start(arrs):
    n = len(arrs)
    lands = [lax.empty((3,) + a.shape[1:], a.dtype) for a in arrs]

    def body(*refs):
        src, land, s_sem, r_sem, token = refs[:n], refs[n:2 * n], refs[2 * n], refs[2 * n + 1], refs[-1]
        for cp in _scatter_copies(src, land, s_sem, r_sem):
            cp.start()
        token[...] = jnp.zeros_like(token)

    dma = pltpu.SemaphoreType.DMA
    hbm = [pltpu.with_memory_space_constraint(a, pltpu.HBM) for a in list(arrs) + lands]
    out = pl.pallas_call(
        body, name="scatter_blocks_start",
        out_shape=[dma((3 * n,)), dma((3 * n,))] + [pltpu.HBM(a.shape, a.dtype) for a in hbm]
        + [jax.ShapeDtypeStruct((8, 128), F32)],
        in_specs=[_HBM] * (2 * n), out_specs=[_SEM, _SEM] + [_HBM] * (2 * n) + [pl.BlockSpec(memory_space=pltpu.VMEM)],
        input_output_aliases={i: 2 + i for i in range(2 * n)},
        compiler_params=pltpu.CompilerParams(has_side_effects=_EFFECT),
    )(*hbm)
    return out[0], out[1], out[2:2 + n], out[2 + n:2 + 2 * n], out[-1]


def _scatter_blocks_wait(s_sem, r_sem, srcs, lands, after):
    n = len(srcs)

    def body(*refs):
        src, land, s_sem, r_sem = refs[:n], refs[n:2 * n], refs[2 * n], refs[2 * n + 1]
        for cp in _scatter_copies(src, land, s_sem, r_sem):
            cp.wait_send()
            cp.wait_recv()

    out = pl.pallas_call(
        body, name="scatter_blocks_wait",
        out_shape=[pltpu.HBM(a.shape, a.dtype) for a in list(srcs) + list(lands)],
        in_specs=[_HBM] * (2 * n) + [_SEM, _SEM] + [ANY] * len(after), out_specs=[_HBM] * (2 * n),
        input_output_aliases={i: i for i in range(2 * n)},
        compiler_params=pltpu.CompilerParams(has_side_effects=_EFFECT),
    )(*srcs, *lands, s_sem, r_sem, *after)
    return out[:n], out[n:]


def _share_halves(arrs):
    n = len(arrs)

    def body(*refs):
        buf, s_sem, r_sem = refs[n:2 * n], refs[2 * n], refs[2 * n + 1]
        x, y, c = _place()
        cps = []
        for i in range(n):
            mine = buf[i].at[_half(buf[i].shape[0], c)]
            cps.append(_rcopy(mine, mine, s_sem.at[i], r_sem.at[i], (x, y, 1 - c)))
        for cp in cps:
            cp.start()
        for i in range(n):
            theirs = buf[i].at[_half(buf[i].shape[0], 1 - c)]
            _rcopy(theirs, theirs, s_sem.at[i], r_sem.at[i], (x, y, 1 - c)).wait_recv()
        for cp in cps:
            cp.wait_send()

    dma = pltpu.SemaphoreType.DMA
    return pl.pallas_call(
        body, name="share_halves", in_specs=[ANY] * n, out_specs=[ANY] * n,
        out_shape=[jax.ShapeDtypeStruct(a.shape, a.dtype) for a in arrs],
        input_output_aliases={i: i for i in range(n)}, scratch_shapes=[dma((n,)), dma((n,))],
    )(*arrs)


def _small_gather_copies(src, land, s_sem, r_sem):
    x, y, c = _place()
    cps = []
    for d in range(1, N_DEV):
        peer = ((1 - x) if d & 4 else x), ((1 - y) if d & 2 else y), ((1 - c) if d & 1 else c)
        cps.append(_rcopy(src, land.at[4 * x + 2 * y + c], s_sem.at[d - 1], r_sem.at[d - 1], peer))
    return cps


def _small_gather_start(packed):
    def body(src, land, s_sem, r_sem, _, __, token):
        for cp in _small_gather_copies(src, land, s_sem, r_sem):
            cp.start()
        token[...] = jnp.zeros_like(token)

    dma = pltpu.SemaphoreType.DMA
    hbm = [pltpu.with_memory_space_constraint(a, pltpu.HBM) for a in (packed, lax.empty((N_DEV,) + packed.shape, F32))]
    return pl.pallas_call(
        body, name="small_gather_start",
        out_shape=[dma((N_DEV - 1,)), dma((N_DEV - 1,))] + [pltpu.HBM(a.shape, a.dtype) for a in hbm]
        + [jax.ShapeDtypeStruct((8, 128), F32)],
        in_specs=[_HBM] * 2, out_specs=[_SEM, _SEM, _HBM, _HBM, pl.BlockSpec(memory_space=pltpu.VMEM)],
        input_output_aliases={0: 2, 1: 3}, compiler_params=pltpu.CompilerParams(has_side_effects=_EFFECT),
    )(*hbm)


def _small_gather_wait(s_sem, r_sem, src, land, after):
    def body(src, land, s_sem, r_sem, *_):
        for cp in _small_gather_copies(src, land, s_sem, r_sem):
            cp.wait_send()
            cp.wait_recv()

    return pl.pallas_call(
        body, name="small_gather_wait", out_shape=[pltpu.HBM(src.shape, src.dtype), pltpu.HBM(land.shape, land.dtype)],
        in_specs=[_HBM, _HBM, _SEM, _SEM, ANY], out_specs=[_HBM, _HBM], input_output_aliases={0: 0, 1: 1},
        compiler_params=pltpu.CompilerParams(has_side_effects=_EFFECT),
    )(src, land, s_sem, r_sem, after)


def _small_sum(own, land, dev_arr):
    def body(me_ref, own_ref, land_ref, o_ref):
        acc = jnp.zeros(o_ref.shape, F32)
        for d in range(N_DEV):
            acc = acc + jnp.where(me_ref[0] == d, own_ref[...], land_ref[d])
        o_ref[...] = acc

    return pl.pallas_call(
        body, name="small_sum", out_shape=jax.ShapeDtypeStruct(own.shape, F32),
        grid_spec=pltpu.PrefetchScalarGridSpec(
            num_scalar_prefetch=1, grid=(1,),
            in_specs=[pl.BlockSpec(own.shape, lambda i, m: (0, 0)), pl.BlockSpec(land.shape, lambda i, m: (0, 0, 0))],
            out_specs=pl.BlockSpec(own.shape, lambda i, m: (0, 0))),
    )(dev_arr, own, land)


def _row_tile(rows, cols):
    tr = max(8, min(rows, (1 << 20) // (4 * cols) // 8 * 8))
    while rows % tr:
        tr -= 8
    return tr


def _chip_sum(name, g5, recv, c_arr):
    nb, _, hs, cols = g5.shape
    tr = _row_tile(hs, cols)

    def body(_, a_ref, b_ref, o_ref):
        o_ref[...] = (a_ref[...] + b_ref[...].astype(F32)).astype(o_ref.dtype)

    blk = pl.BlockSpec((None, tr, cols), lambda b, i, c: (b, i, 0))
    return pl.pallas_call(
        body, name=name,
        grid_spec=pltpu.PrefetchScalarGridSpec(
            num_scalar_prefetch=1, grid=(nb, hs // tr),
            in_specs=[pl.BlockSpec((None, None, tr, cols), lambda b, i, c: (b, c[0], i, 0)), blk], out_specs=blk),
        out_shape=jax.ShapeDtypeStruct((nb, hs, cols), WIRE_DTYPE),
    )(c_arr, g5, recv)


def _final_sum(name, own, recv, place_arr):
    _, hs, cols = own.shape
    tr = _row_tile(hs, cols)
    nt = hs // tr

    def body(_, a_ref, r_ref, o_ref):
        o_ref[...] = ((a_ref[...].astype(F32) + r_ref[0].astype(F32)) + r_ref[1].astype(F32)) + r_ref[2].astype(F32)

    return pl.pallas_call(
        body, name=name,
        grid_spec=pltpu.PrefetchScalarGridSpec(
            num_scalar_prefetch=1, grid=(nt,),
            in_specs=[pl.BlockSpec((None, tr, cols), lambda i, m: (m[0], i, 0)),
                      pl.BlockSpec((3, tr, cols), lambda i, m: (0, i, 0))],
            out_specs=pl.BlockSpec((tr, cols), lambda i, m: (m[1] * nt + i, 0))),
        out_shape=jax.ShapeDtypeStruct((2 * hs, cols), F32),
    )(place_arr, own, recv)


def _adamw(w, g, m, v):
    m = ADAM_B1 * m + (1.0 - ADAM_B1) * g
    v = ADAM_B2 * v + (1.0 - ADAM_B2) * (g * g)
    m_hat = m / (1.0 - ADAM_B1 ** ADAM_STEP)
    v_hat = v / (1.0 - ADAM_B2 ** ADAM_STEP)
    return -ADAM_LR * (m_hat / (jnp.sqrt(v_hat) + ADAM_EPS) + ADAM_WD * w), m, v


def _adamw_call(name, w, g, m, v):
    rows, cols = w.shape
    tr = _row_tile(rows, cols)
    if 4 * tr * cols >= (1 << 18):
        blk, steps = pl.BlockSpec((tr, cols), lambda i: (i, 0)), rows // tr
    else:
        blk, steps = pl.BlockSpec((rows, 128), lambda i: (0, i)), cols // 128

    def body(w_ref, g_ref, m_ref, v_ref, d_ref, nm_ref, nv_ref, go_ref):
        g = g_ref[...]
        d_ref[...], nm_ref[...], nv_ref[...] = _adamw(w_ref[...], g, m_ref[...], v_ref[...])
        go_ref[...] = g

    return pl.pallas_call(
        body, name=name, grid=(steps,), in_specs=[blk] * 4, out_specs=[blk] * 4,
        out_shape=[jax.ShapeDtypeStruct(w.shape, F32)] * 4,
        compiler_params=pltpu.CompilerParams(dimension_semantics=("parallel",)),
    )(w, g, m, v)


def _adamw_small(ws, gs, ms, vs):
    n = len(ws)

    def body(*refs):
        for i in range(n):
            w_ref, g_ref, m_ref, v_ref = (refs[k * n + i] for k in range(4))
            d, nm, nv = _adamw(w_ref[...], g_ref[...], m_ref[...], v_ref[...])
            refs[4 * n + i][...] = d
            refs[5 * n + i][...] = nm
            refs[6 * n + i][...] = nv

    out = pl.pallas_call(
        body, name="adamw_small", out_shape=[jax.ShapeDtypeStruct(a.shape, F32) for a in ws] * 3,
    )(*ws, *gs, *ms, *vs)
    return out[:n], out[n:2 * n], out[2 * n:]


_BIG = ("w_in", "w_oa", "w_ob", "w_out", "w_pg", "w_ple")
_SMALL = ("norm_g", "ln_a_g", "ln_a_b", "w_s", "b_s", "conv_w", "conv_b", "dt_bias", "a_log", "d_skip", "ssm_norm_g",
          "ple_norm_g", "final_g")
_WEIGHTS = ("norm_g", "w_in", "ln_a_g", "ln_a_b", "w_s", "b_s", "conv_w", "conv_b", "dt_bias", "a_log", "d_skip",
            "ssm_norm_g", "w_oa", "w_ob", "w_out", "ple_norm_g", "w_pg", "w_ple", "final_g")
_COL_SHARDED = ("w_in", "w_ple")
_PACK = 1024


def _blocks_to_full(col_sharded, blocks):
    if col_sharded:
        return jnp.concatenate([blocks[k] for k in range(N_CHIPS)], axis=1)
    return blocks.reshape(N_CHIPS * blocks.shape[1], blocks.shape[2])


def _full_to_blocks(col_sharded, full):
    if col_sharded:
        w = full.shape[1] // N_CHIPS
        return jnp.stack([full[:, k * w:(k + 1) * w] for k in range(N_CHIPS)])
    return full.reshape(N_CHIPS, full.shape[0] // N_CHIPS, full.shape[1])


def _two_d(n, a):
    if n == "w_s":
        return a.reshape(G_A * CHUNK, CHUNK)
    if n in ("b_s", "conv_w"):
        return a.reshape(a.shape[-2], a.shape[-1])
    return a.reshape(1, a.shape[-1])


def kernel(x, p, norm_g, w_in, ln_a_g, ln_a_b, w_s, b_s, conv_w, conv_b, dt_bias, a_log, d_skip, ssm_norm_g, w_oa, w_ob, w_out, ple_norm_g, w_pg, w_ple, final_g, loss_target, m_norm_g, m_w_in, m_ln_a_g, m_ln_a_b, m_w_s, m_b_s, m_conv_w, m_conv_b, m_dt_bias, m_a_log, m_d_skip, m_ssm_norm_g, m_w_oa, m_w_ob, m_w_out, m_ple_norm_g, m_w_pg, m_w_ple, m_final_g, v_norm_g, v_w_in, v_ln_a_g, v_ln_a_b, v_w_s, v_b_s, v_conv_w, v_conv_b, v_dt_bias, v_a_log, v_d_skip, v_ssm_norm_g, v_w_oa, v_w_ob, v_w_out, v_ple_norm_g, v_w_pg, v_w_ple, v_final_g):
    wt = dict(norm_g=norm_g, w_in=w_in, ln_a_g=ln_a_g, ln_a_b=ln_a_b, w_s=w_s, b_s=b_s, conv_w=conv_w, conv_b=conv_b,
              dt_bias=dt_bias, a_log=a_log, d_skip=d_skip, ssm_norm_g=ssm_norm_g, w_oa=w_oa, w_ob=w_ob, w_out=w_out,
              ple_norm_g=ple_norm_g, w_pg=w_pg, w_ple=w_ple, final_g=final_g)
    mom = dict(norm_g=m_norm_g, w_in=m_w_in, ln_a_g=m_ln_a_g, ln_a_b=m_ln_a_b, w_s=m_w_s, b_s=m_b_s, conv_w=m_conv_w,
               conv_b=m_conv_b, dt_bias=m_dt_bias, a_log=m_a_log, d_skip=m_d_skip, ssm_norm_g=m_ssm_norm_g, w_oa=m_w_oa,
               w_ob=m_w_ob, w_out=m_w_out, ple_norm_g=m_ple_norm_g, w_pg=m_w_pg, w_ple=m_w_ple, final_g=m_final_g)
    vel = dict(norm_g=v_norm_g, w_in=v_w_in, ln_a_g=v_ln_a_g, ln_a_b=v_ln_a_b, w_s=v_w_s, b_s=v_b_s, conv_w=v_conv_w,
               conv_b=v_conv_b, dt_bias=v_dt_bias, a_log=v_a_log, d_skip=v_d_skip, ssm_norm_g=v_ssm_norm_g, w_oa=v_w_oa,
               w_ob=v_w_ob, w_out=v_w_out, ple_norm_g=v_ple_norm_g, w_pg=v_w_pg, w_ple=v_w_ple, final_g=v_final_g)
    xi, yi, ci = _place()
    me = 2 * xi + yi
    c_arr = jnp.reshape(ci, (1,)).astype(jnp.int32)
    place_arr = jnp.stack([me, ci]).astype(jnp.int32)

    shard = {n: wt[n][0] for n in _BIG}
    wire = {n: shard[n].astype(WIRE_DTYPE) for n in _BIG}
    w_in_blocks, conv_blocks = _gather_weights([wire["w_in"]], conv_w[0])
    g_ssem, g_rsem, g_sent, g_lands, g_token = _late_gather_start([wire[n] for n in _BIG[1:]])
    full = {"conv_w": _blocks_to_full(True, conv_blocks)}
    for n in _SMALL:
        if n != "conv_w":
            full[n] = wt[n][0] if wt[n].ndim > 2 else wt[n].reshape(1, wt[n].shape[-1])

    def late_weights(after):
        blocks = _late_gather_wait(g_ssem, g_rsem, g_sent, g_lands, after)
        return {n: _blocks_to_full(n in _COL_SHARDED, b) for n, b in zip(_BIG[1:], blocks)}

    w = _layout_weights(full, w_in_blocks=w_in_blocks)
    loss_row, g, ctx = _forward_backward(x[0], p[0, 0], loss_target[0], w, late_weights, after=(g_token,))
    loss = lax.psum(loss_row[0, 0], ("x", "y", "c"))

    parts = {n: _full_to_blocks(n in _COL_SHARDED, g[n]) for n in _BIG[1:]}
    parts["w_main"] = g["w_main"][None]
    parts["w_dt"] = jnp.pad(_lanes_to_heads(g["w_dt"]), ((0, 0), (0, 128 - N_HEADS)))[None]
    names = ("w_main", "w_dt") + _BIG[1:]
    g5 = {n: parts[n].reshape(parts[n].shape[0], 2, parts[n].shape[1] // 2, parts[n].shape[2]) for n in names}
    to_sibling = [lax.dynamic_index_in_dim(g5[n], 1 - ci, axis=1, keepdims=False).astype(WIRE_DTYPE) for n in names]
    from_sibling = _swap_with_sibling(to_sibling)
    chip = {n: _chip_sum("chip_sum_" + n, g5[n], r, c_arr) for n, r in zip(names, from_sibling)}
    chip["w_in"] = _w_in_grad_blocks(chip["w_main"][0], chip["w_dt"][0])
    chip_wire = [chip[n] for n in _BIG]
    s_sem, r_sem, sent, lands, token = _scatter_blocks_start(chip_wire)
    grad_x, g["norm_g"] = _input_grad(x[0], w, ctx, after=(token,))
    g = _natural_grads(g)

    pieces = [_two_d(n, g[n]).reshape(-1) for n in _SMALL]
    sizes = [v.shape[0] for v in pieces]
    padded = [-(-s // _PACK) * _PACK for s in sizes]
    packed = jnp.concatenate([jnp.pad(v, (0, ps - s)) for v, s, ps in zip(pieces, sizes, padded)]).reshape(-1, 128)
    a_ssem, a_rsem, a_src, a_land, a_token = _small_gather_start(packed)

    sent, from_chips = _scatter_blocks_wait(s_sem, r_sem, sent, lands, (grad_x, a_token))
    halves = [_final_sum("final_sum_" + n, a, r, place_arr) for n, a, r in zip(_BIG, sent, from_chips)]
    grads = dict(zip(_BIG, _share_halves(halves)))
    delta, new_m, new_v = {}, {}, {}
    for n in _BIG:
        t = jnp.transpose if n == "w_in" else (lambda a: a)
        res = _adamw_call("adamw_" + n, t(shard[n]), t(grads[n]), t(mom[n][0]), t(vel[n][0]))
        delta[n], new_m[n], new_v[n], grads[n] = (t(r) for r in res)

    a_src, a_land = _small_gather_wait(a_ssem, a_rsem, a_src, a_land, delta["w_in"])
    summed = _small_sum(a_src, a_land, jnp.reshape(4 * xi + 2 * yi + ci, (1,)).astype(jnp.int32)).reshape(-1)
    off = 0
    for n, s, ps in zip(_SMALL, sizes, padded):
        grads[n] = summed[off:off + s].reshape(_two_d(n, g[n]).shape)
        off += ps
    grads["conv_w"] = lax.dynamic_slice_in_dim(grads["conv_w"], me * (CONV_DIM // N_CHIPS), CONV_DIM // N_CHIPS, axis=1)
    small = _adamw_small([_two_d(n, wt[n]) for n in _SMALL], [grads[n] for n in _SMALL],
                         [_two_d(n, mom[n]) for n in _SMALL], [_two_d(n, vel[n]) for n in _SMALL])
    for i, n in enumerate(_SMALL):
        delta[n], new_m[n], new_v[n] = small[0][i], small[1][i], small[2][i]

    def shaped(d):
        return [d[n].reshape(wt[n].shape) for n in _WEIGHTS]

    return (loss, grad_x[None], *shaped(grads), *shaped(delta), *shaped(new_m), *shaped(new_v))
```

```python
import functools

import jax
import jax.numpy as jnp
from jax import lax
from jax.experimental import pallas as pl
from jax.experimental.pallas import tpu as pltpu

F32 = jnp.float32
MXU_DTYPE = jnp.bfloat16
ACT_DTYPE = jnp.bfloat16
WIRE_DTYPE = jnp.bfloat16

D_MODEL = 1024
PLE_DIM = 256
CHUNK = 128
EPS = 1e-6
E_A = D_MODEL
G_A = 4
D_INNER = 2 * D_MODEL
HEAD_DIM = 64
N_HEADS = D_INNER // HEAD_DIM
N_STATE = 128
N_GROUPS = 4
HEADS_PER_GROUP = N_HEADS // N_GROUPS
PAIRS_PER_GROUP = HEADS_PER_GROUP // 2
CONV_K = 4
CONV_DIM = D_INNER + 2 * N_GROUPS * N_STATE
N_IN = 3 * E_A + D_INNER + CONV_DIM + N_HEADS + 2 * D_MODEL
N_CHIPS = 4
N_DEV = 8
W_IN_BLOCK = N_IN // N_CHIPS

UVZ_W, XBC_W, ZB_W, G_W = 3 * E_A, CONV_DIM, D_INNER, 2 * D_MODEL
MAIN_W = UVZ_W + XBC_W + ZB_W + G_W
UVZ_CB, XBC_CB, ZB_CB, G_CB = 0, 1, 3, 4
DT_W = N_GROUPS * 128

ADAM_LR, ADAM_B1, ADAM_B2, ADAM_EPS, ADAM_WD, ADAM_STEP = 0.001, 0.9, 0.999, 1e-08, 0.01, 10

MESH = pl.DeviceIdType.MESH
ANY = pl.BlockSpec(memory_space=pl.ANY)


def _mxu(v):
    return v.astype(MXU_DTYPE)


def _dot(a, b, dims=(((1,), (0,)), ((), ()))):
    return lax.dot_general(_mxu(a), _mxu(b), dims, preferred_element_type=F32)


MM_TILE = 1024
MM_VMEM_BUDGET = 46 << 20


def _mm_tk(m, n, k, tm, tn, a_bytes, b_bytes, out_bytes, extra_bytes):
    for parts in range(1, k // 128 + 1):
        if k % parts or (k // parts) % 128 and parts > 1:
            continue
        tk = k // parts
        need = 2 * tk * (tm * a_bytes + tn * b_bytes) + 2 * tm * tn * (out_bytes + extra_bytes) + (tm * tn * 4 if parts > 1 else 0)
        if need <= MM_VMEM_BUDGET:
            return tk
    return 128


def _matmul(a, b, *, mode, name, out_dtype, m, n, k, tm=MM_TILE, tn=MM_TILE, tk=None, a_off=0, b_off=0,
            extras=(), epilogue=None, after=()):
    tm, tn = min(tm, m), min(tn, n)
    if tk is None:
        tk = _mm_tk(m, n, k, tm, tn, a.dtype.itemsize, b.dtype.itemsize, jnp.dtype(out_dtype).itemsize,
                    sum(e.dtype.itemsize for e in extras))
    tk = min(tk, k)
    assert m % tm == 0 and n % tn == 0 and k % tk == 0, (name, m, n, k, tm, tn, tk)
    nk = k // tk
    if mode == "nn":
        assert a_off % tk == 0 and b_off % tn == 0
        a_spec = pl.BlockSpec((tm, tk), lambda i, j, kk: (i, kk + a_off // tk))
        b_spec = pl.BlockSpec((tk, tn), lambda i, j, kk: (kk, j + b_off // tn))
        dims = (((1,), (0,)), ((), ()))
    elif mode == "nt":
        a_spec = pl.BlockSpec((tm, tk), lambda i, j, kk: (i, kk))
        b_spec = pl.BlockSpec((tn, tk), lambda i, j, kk: (j, kk))
        dims = (((1,), (1,)), ((), ()))
    else:
        assert a_off % tm == 0 and b_off % tn == 0
        a_spec = pl.BlockSpec((tk, tm), lambda i, j, kk: (kk, i + a_off // tm))
        b_spec = pl.BlockSpec((tk, tn), lambda i, j, kk: (kk, j + b_off // tn))
        dims = (((0,), (0,)), ((), ()))
    ne = len(extras)

    def finish(acc, extra_refs, o_ref):
        res = acc if epilogue is None else epilogue(acc, *[e[...] for e in extra_refs])
        o_ref[...] = res.astype(o_ref.dtype)

    def body(a_ref, b_ref, *rest):
        extra_refs, o_ref = rest[:ne], rest[ne + len(after)]
        part = _dot(a_ref[...], b_ref[...], dims)
        if nk == 1:
            finish(part, extra_refs, o_ref)
            return
        acc_ref = rest[ne + len(after) + 1]
        kk = pl.program_id(2)

        @pl.when(kk == 0)
        def _():
            acc_ref[...] = part

        @pl.when(kk > 0)
        def _():
            acc_ref[...] += part

        @pl.when(kk == nk - 1)
        def _():
            finish(acc_ref[...], extra_refs, o_ref)

    o_spec = pl.BlockSpec((tm, tn), lambda i, j, kk: (i, j))
    return pl.pallas_call(
        body, name=name, grid=(m // tm, n // tn, nk),
        in_specs=[a_spec, b_spec] + [o_spec] * ne + [ANY] * len(after), out_specs=o_spec,
        out_shape=jax.ShapeDtypeStruct((m, n), out_dtype),
        scratch_shapes=[pltpu.VMEM((tm, tn), F32)] if nk > 1 else [],
        compiler_params=pltpu.CompilerParams(dimension_semantics=("parallel", "parallel", "arbitrary")),
    )(a, b, *extras, *after)


def _rows_matmul(name, f, rows, pars, b, *, out_dtype, n, k, tm, nrows, tn=MM_TILE, extras=(), epilogue=None, after=(),
                 side=None):
    tm, tn = min(tm, nrows), min(tn, n)
    assert nrows % tm == 0 and n % tn == 0, (name, nrows, n, tm, tn)
    nr, npar, ne, nj = len(rows), len(pars), len(extras), n // tn
    ns = 0 if side is None else 1
    n_in = nr + npar + 1 + ne + ns + len(after)

    def body(*refs):
        row_refs, par_refs, b_ref = refs[:nr], refs[nr:nr + npar], refs[nr + npar]
        extra_refs = refs[nr + npar + 1:nr + npar + 1 + ne]
        a_ref, o_ref = refs[n_in], refs[n_in + 1]

        def make_a():
            a = f(*[r[...].astype(F32) for r in row_refs], *[p[...] for p in par_refs])[0]
            a_ref[...] = a.astype(a_ref.dtype)
            if ns:
                refs[n_in + 2][...] = _dot(a_ref[...], refs[nr + npar + 1 + ne][...]).astype(refs[n_in + 2].dtype)

        if nj == 1:
            make_a()
        else:
            pl.when(pl.program_id(1) == 0)(make_a)
        res = _dot(a_ref[...], b_ref[...])
        if epilogue is not None:
            res = epilogue(res, *[e[...] for e in extra_refs])
        o_ref[...] = res.astype(o_ref.dtype)

    o_spec = pl.BlockSpec((tm, tn), lambda i, j: (i, j))
    side_in = [] if side is None else [pl.BlockSpec(tuple(side[0].shape), lambda i, j: (0, 0))]
    side_out = [] if side is None else [pl.BlockSpec((tm, side[0].shape[1]), lambda i, j: (i, 0))]
    side_shape = [] if side is None else [jax.ShapeDtypeStruct((nrows, side[0].shape[1]), side[1])]
    return pl.pallas_call(
        body, name=name, grid=(nrows // tm, nj),
        in_specs=[pl.BlockSpec((tm, w), lambda i, j, cb=cb: (i, cb)) for _, w, cb in rows]
        + [pl.BlockSpec(tuple(p.shape), lambda i, j, nd=p.ndim: (0,) * nd) for p in pars]
        + [pl.BlockSpec((k, tn), lambda i, j: (0, j))] + [o_spec] * ne + side_in + [ANY] * len(after),
        out_specs=[pl.BlockSpec((tm, k), lambda i, j: (i, 0)), o_spec] + side_out,
        out_shape=[jax.ShapeDtypeStruct((nrows, k), ACT_DTYPE), jax.ShapeDtypeStruct((nrows, n), out_dtype)] + side_shape,
        compiler_params=pltpu.CompilerParams(dimension_semantics=("parallel", "arbitrary")),
    )(*[r[0] for r in rows], *pars, b, *extras, *([] if side is None else [side[0]]), *after)


def _row_spec(tm, width, cb):
    return pl.BlockSpec((tm, width), lambda i: (i, cb))


def _whole_spec(shape):
    nd = len(shape)
    return pl.BlockSpec(tuple(shape), lambda i: (0,) * nd)


def _rows_call(name, f, rows, pars, outs, *, tm, nrows):
    tm = min(tm, nrows)
    nr, npar = len(rows), len(pars)

    def body(*refs):
        rv = [r[...].astype(F32) for r in refs[:nr]]
        pv = [p[...] for p in refs[nr:nr + npar]]
        res = f(*rv, *pv)
        for o_ref, r in zip(refs[nr + npar:], res):
            o_ref[...] = r.astype(o_ref.dtype)

    return pl.pallas_call(
        body, name=name, grid=(nrows // tm,),
        in_specs=[_row_spec(tm, w, cb) for _, w, cb in rows] + [_whole_spec(p.shape) for p in pars],
        out_specs=[_row_spec(tm, w, 0) for w, _ in outs],
        out_shape=[jax.ShapeDtypeStruct((nrows, w), dt) for w, dt in outs],
        compiler_params=pltpu.CompilerParams(dimension_semantics=("parallel",)),
    )(*[r[0] for r in rows], *pars)


def _rows_vjp_call(name, f, rows, pars, cots, drows, *, tm, nrows, cot_mm=None):
    tm = min(tm, nrows)
    nr, npar, nc = len(rows), len(pars), len(cots)
    mm_args, mm_specs = [], []
    if cot_mm is not None:
        mm_a, mm_b, mm_add = cot_mm
        mm_args = [mm_a, mm_b] + ([] if mm_add is None else [mm_add])
        mm_specs = [_row_spec(tm, mm_a.shape[1], 0), _whole_spec(mm_b.shape)]
        mm_specs += [] if mm_add is None else [_row_spec(tm, mm_b.shape[0], 0)]
    alias_bufs, aliases = [], {}
    out_shape, out_specs = [], []
    for (arr, w, cb), d in zip(rows, drows):
        if d is None:
            continue
        dt, into = d
        if into is None:
            out_shape.append(jax.ShapeDtypeStruct((nrows, w), dt))
            out_specs.append(_row_spec(tm, w, 0))
        else:
            buf, total, ocb = into
            if buf is not None:
                aliases[nr + npar + nc + len(alias_bufs)] = len(out_shape)
                alias_bufs.append(buf)
            out_shape.append(jax.ShapeDtypeStruct((nrows, total), dt))
            out_specs.append(_row_spec(tm, w, ocb))
    n_drow = len(out_shape)
    for p in pars:
        out_shape.append(jax.ShapeDtypeStruct(p.shape, F32))
        out_specs.append(_whole_spec(p.shape))
    na = len(alias_bufs)

    def body(*refs):
        rv = [r[...].astype(F32) for r in refs[:nr]]
        pv = [p[...] for p in refs[nr:nr + npar]]
        cv = tuple(c[...].astype(F32) for c in refs[nr + npar:nr + npar + nc])
        o_refs = refs[nr + npar + nc + na + len(mm_args):]
        if mm_args:
            mm_refs = refs[nr + npar + nc + na:nr + npar + nc + na + len(mm_args)]
            c0 = _dot(mm_refs[0][...], mm_refs[1][...], (((1,), (1,)), ((), ())))
            if len(mm_refs) == 3:
                c0 = c0 + mm_refs[2][...].astype(F32)
            cv = (c0,) + cv
        _, vjp = jax.vjp(f, *rv, *pv)
        g = vjp(cv)
        oi = 0
        for ri, d in enumerate(drows):
            if d is not None:
                o_refs[oi][...] = g[ri].astype(o_refs[oi].dtype)
                oi += 1
        first = pl.program_id(0) == 0
        for pi in range(npar):
            acc = o_refs[n_drow + pi]

            @pl.when(first)
            def _(acc=acc):
                acc[...] = jnp.zeros_like(acc)

            acc[...] += g[nr + pi]

    return pl.pallas_call(
        body, name=name, grid=(nrows // tm,),
        in_specs=[_row_spec(tm, w, cb) for _, w, cb in rows] + [_whole_spec(p.shape) for p in pars]
        + [_row_spec(tm, w, cb) for _, w, cb in cots] + [ANY] * na + mm_specs,
        out_specs=out_specs, out_shape=out_shape, input_output_aliases=aliases,
        compiler_params=pltpu.CompilerParams(dimension_semantics=("arbitrary",)),
    )(*[r[0] for r in rows], *pars, *[c[0] for c in cots], *alias_bufs, *mm_args)


def _rms(x, g):
    return x * lax.rsqrt(jnp.mean(x * x, axis=-1, keepdims=True) + EPS) * g


def _f_rms(x, g):
    return (_rms(x, g),)


def _tril_mask():
    return lax.broadcasted_iota(jnp.int32, (CHUNK, CHUNK), 0) >= lax.broadcasted_iota(jnp.int32, (CHUNK, CHUNK), 1)


def _f_branch_a(uvz, ln_g, ln_b, w_s, b_s):
    u = jax.nn.gelu(uvz[:, :E_A])
    v = jax.nn.gelu(uvz[:, E_A:2 * E_A])
    z = uvz[:, 2 * E_A:]
    xc = v - jnp.mean(v, axis=-1, keepdims=True)
    vn = xc * lax.rsqrt(jnp.mean(xc * xc, axis=-1, keepdims=True) + EPS) * ln_g + ln_b
    mask = _tril_mask()
    ws = [jnp.where(mask, w_s[g], 0.0) for g in range(G_A)]
    gw = E_A // G_A
    rows = []
    for c in range(uvz.shape[0] // CHUNK):
        vc = vn[c * CHUNK:(c + 1) * CHUNK]
        rows.append(jnp.concatenate([_dot(ws[g], vc[:, g * gw:(g + 1) * gw]) + b_s[g] for g in range(G_A)], axis=1))
    sv = rows[0] if len(rows) == 1 else jnp.concatenate(rows, axis=0)
    return (u * sv * jax.nn.silu(z),)


def _f_gnorm(y, zb, g):
    yz = y * jax.nn.silu(zb)
    gw = D_INNER // N_GROUPS
    parts = []
    for i in range(N_GROUPS):
        s = yz[:, i * gw:(i + 1) * gw]
        parts.append(s * lax.rsqrt(jnp.mean(s * s, axis=-1, keepdims=True) + EPS))
    return (jnp.concatenate(parts, axis=1) * g,)


def _f_merge(g2, oa, ob):
    return (jax.nn.sigmoid(g2[:, :D_MODEL]) * oa + jax.nn.sigmoid(g2[:, D_MODEL:]) * ob,)


def _f_loss(x1, gp, pe, tgt, fg):
    x2 = x1 + jax.nn.sigmoid(gp) * pe
    err = _rms(x2, fg) - tgt
    return 0.5 * jnp.sum(jnp.mean(err * err, axis=-1))


def _head(x1, p, tgt, ple_g, w_pg, w_ple, fg, *, tm, nrows):
    tm = min(tm, nrows)

    def body(x1_ref, p_ref, t_ref, pg_ref, wpg_ref, wple_ref, fg_ref, hp_ref, dx_ref, dgp_ref, dpe_ref, dfg_ref, loss_ref):
        x1 = x1_ref[...]
        hp_ref[...] = _rms(x1, pg_ref[...]).astype(hp_ref.dtype)
        gp = _dot(hp_ref[...], wpg_ref[...])
        pe = _dot(p_ref[...], wple_ref[...])
        loss, vjp = jax.vjp(_f_loss, x1, gp, pe, t_ref[...], fg_ref[...])
        dx, dgp, dpe, _, dfg = vjp(jnp.ones((), F32))
        dx_ref[...] = dx
        dgp_ref[...] = dgp.astype(dgp_ref.dtype)
        dpe_ref[...] = dpe.astype(dpe_ref.dtype)

        @pl.when(pl.program_id(0) == 0)
        def _():
            dfg_ref[...] = jnp.zeros_like(dfg_ref)
            loss_ref[...] = jnp.zeros_like(loss_ref)

        dfg_ref[...] += dfg
        loss_ref[...] += jnp.full(loss_ref.shape, loss, F32)

    row = _row_spec(tm, D_MODEL, 0)
    act = jax.ShapeDtypeStruct((nrows, D_MODEL), ACT_DTYPE)
    return pl.pallas_call(
        body, name="head", grid=(nrows // tm,),
        in_specs=[row, _row_spec(tm, PLE_DIM, 0), row, _whole_spec((1, D_MODEL)), _whole_spec(w_pg.shape),
                  _whole_spec(w_ple.shape), _whole_spec((1, D_MODEL))],
        out_specs=[row, row, row, row, _whole_spec((1, D_MODEL)), _whole_spec((1, 128))],
        out_shape=[act, jax.ShapeDtypeStruct((nrows, D_MODEL), F32), act, act, jax.ShapeDtypeStruct((1, D_MODEL), F32),
                   jax.ShapeDtypeStruct((1, 128), F32)],
        compiler_params=pltpu.CompilerParams(dimension_semantics=("arbitrary",)),
    )(x1, p, tgt, ple_g, w_pg, w_ple, fg)


def _shift_rows(cur, edge, j, up):
    tm = cur.shape[0]
    row = lax.broadcasted_iota(jnp.int32, cur.shape, 0)
    if up:
        sh = pltpu.roll(cur, tm - j, 0)
        e = jnp.tile(pltpu.roll(edge, 8 - j, 0), (tm // 8, 1))
        return jnp.where(row >= tm - j, e, sh)
    sh = pltpu.roll(cur, j, 0)
    e = jnp.tile(pltpu.roll(edge, j, 0), (tm // 8, 1))
    return jnp.where(row < j, e, sh)


def _conv_pre(cur, prev, w, b):
    acc = cur * w[CONV_K - 1:CONV_K] + b
    for j in range(1, CONV_K):
        acc = acc + _shift_rows(cur, prev, j, up=False) * w[CONV_K - 1 - j:CONV_K - j]
    return acc


def _halo_specs(tm, nrows, cb, before):
    nb = tm // 8
    last = nrows // 8 - 1
    if before:
        return pl.BlockSpec((8, XBC_W), lambda i: (jnp.maximum(i * nb - 1, 0), cb))
    return pl.BlockSpec((8, XBC_W), lambda i: (jnp.minimum((i + 1) * nb, last), cb))


def _conv_fwd(proj, conv_w, conv_b, *, tm, nrows):
    tm = min(tm, nrows)

    def body(cur_ref, prev_ref, w_ref, b_ref, o_ref, pre_ref):
        prev = jnp.where(pl.program_id(0) == 0, 0.0, prev_ref[...].astype(F32))
        pre = _conv_pre(cur_ref[...].astype(F32), prev, w_ref[...], b_ref[...])
        o_ref[...] = jax.nn.silu(pre).astype(o_ref.dtype)
        pre_ref[...] = pre.astype(pre_ref.dtype)

    out = jax.ShapeDtypeStruct((nrows, XBC_W), ACT_DTYPE)
    return pl.pallas_call(
        body, name="conv_fwd", grid=(nrows // tm,),
        in_specs=[_row_spec(tm, XBC_W, XBC_CB), _halo_specs(tm, nrows, XBC_CB, True),
                  _whole_spec((CONV_K, XBC_W)), _whole_spec((1, XBC_W))],
        out_specs=[_row_spec(tm, XBC_W, 0)] * 2, out_shape=[out, out],
        compiler_params=pltpu.CompilerParams(dimension_semantics=("parallel",)),
    )(proj, proj, conv_w, conv_b)


def _conv_bwd_act(pre, dact, *, tm, nrows):
    tm = min(tm, nrows)
    nb = N_GROUPS * N_STATE

    def body(pre_ref, dxs_ref, dbm_ref, dcm_ref, dpre_ref, db_ref):
        pre = pre_ref[...].astype(F32)
        sg = jax.nn.sigmoid(pre)
        dy = jnp.concatenate([dxs_ref[...], dbm_ref[...], dcm_ref[...]], axis=1).astype(F32)
        dpre = dy * sg * (1.0 + pre * (1.0 - sg))
        dpre_ref[...] = dpre.astype(dpre_ref.dtype)

        @pl.when(pl.program_id(0) == 0)
        def _():
            db_ref[...] = jnp.zeros_like(db_ref)

        db_ref[...] += jnp.sum(dpre, axis=0, keepdims=True)

    return pl.pallas_call(
        body, name="conv_bwd_act", grid=(nrows // tm,),
        in_specs=[_row_spec(tm, XBC_W, 0), _row_spec(tm, D_INNER, 0), _row_spec(tm, nb, 0), _row_spec(tm, nb, 0)],
        out_specs=[_row_spec(tm, XBC_W, 0), _whole_spec((1, XBC_W))],
        out_shape=[jax.ShapeDtypeStruct((nrows, XBC_W), ACT_DTYPE), jax.ShapeDtypeStruct((1, XBC_W), F32)],
        compiler_params=pltpu.CompilerParams(dimension_semantics=("arbitrary",)),
    )(pre, *dact)


def _conv_bwd_x(dpre, proj, conv_w, dproj, *, tm, nrows):
    tm = min(tm, nrows)
    ntiles = nrows // tm

    def body(cur_ref, nxt_ref, x_ref, w_ref, _, o_ref, dw_ref):
        cur = cur_ref[...].astype(F32)
        nxt = jnp.where(pl.program_id(0) == ntiles - 1, 0.0, nxt_ref[...].astype(F32))
        x = x_ref[...].astype(F32)
        w = w_ref[...]

        @pl.when(pl.program_id(0) == 0)
        def _():
            dw_ref[...] = jnp.zeros_like(dw_ref)

        acc = cur * w[CONV_K - 1:CONV_K]
        dw_ref[CONV_K - 1:CONV_K, :] += jnp.sum(cur * x, axis=0, keepdims=True)
        for j in range(1, CONV_K):
            u = _shift_rows(cur, nxt, j, up=True)
            acc = acc + u * w[CONV_K - 1 - j:CONV_K - j]
            dw_ref[CONV_K - 1 - j:CONV_K - j, :] += jnp.sum(u * x, axis=0, keepdims=True)
        o_ref[...] = acc.astype(o_ref.dtype)

    return pl.pallas_call(
        body, name="conv_bwd_x", grid=(ntiles,),
        in_specs=[_row_spec(tm, XBC_W, 0), _halo_specs(tm, nrows, 0, False), _row_spec(tm, XBC_W, XBC_CB),
                  _whole_spec((CONV_K, XBC_W)), ANY],
        out_specs=[_row_spec(tm, XBC_W, XBC_CB), _whole_spec((CONV_K, XBC_W))],
        out_shape=[jax.ShapeDtypeStruct(dproj.shape, dproj.dtype), jax.ShapeDtypeStruct((CONV_K, XBC_W), F32)],
        input_output_aliases={4: 0},
        compiler_params=pltpu.CompilerParams(dimension_semantics=("arbitrary",)),
    )(dpre, dpre, proj, conv_w, dproj)


SSD_SPAN = 4
SSD_FWD_SPANS = 2
_XS_GW = D_INNER // N_GROUPS
_NT = (((1,), (1,)), ((), ()))
_TN = (((0,), (0,)), ((), ()))


def _bf16_terms(x, terms):
    parts, rest = [], x
    for _ in range(terms):
        part = rest.astype(jnp.bfloat16)
        parts.append(part)
        rest = rest - part.astype(F32)
    return parts


def _head_lane_matrix():
    return (lax.broadcasted_iota(jnp.int32, (128, _XS_GW), 0)
            == lax.broadcasted_iota(jnp.int32, (128, _XS_GW), 1) // HEAD_DIM).astype(jnp.bfloat16)


@functools.partial(jax.custom_vjp, nondiff_argnums=(1,))
def _head_lanes(cols, terms):
    e = _head_lane_matrix()
    return sum(jnp.dot(t, e, preferred_element_type=F32) for t in _bf16_terms(cols, terms))


def _head_lanes_fwd(cols, terms):
    return _head_lanes(cols, terms), None


def _head_lanes_bwd(terms, _, g):
    e = _head_lane_matrix()
    return (sum(lax.dot_general(t, e, _NT, preferred_element_type=F32) for t in _bf16_terms(g, 2)),)


_head_lanes.defvjp(_head_lanes_fwd, _head_lanes_bwd)


def _ssd_chunk(k, xs, bm, cm, dtr, hprev, dtb, alog, dsk):
    causal, tri, lo = k
    dt = jax.nn.softplus(dtr + dtb)
    da = dt * (-jnp.exp(alog))
    cs = jnp.dot(tri, da, precision=lax.Precision.HIGHEST, preferred_element_type=F32)
    cst = cs.T
    cs_l = _head_lanes(cs, 3)
    xdt = xs * _head_lanes(dt, 2)
    cb = _dot(cm, bm, _NT)
    yd = []
    for q in range(PAIRS_PER_GROUP):
        xq = xdt[:, 128 * q:128 * (q + 1)]
        y2 = [_dot(cb * jnp.exp(jnp.where(causal, cs[:, h:h + 1] - cst[h:h + 1, :], -jnp.inf)), xq)
              for h in (2 * q, 2 * q + 1)]
        yd.append(jnp.where(lo, y2[0], y2[1]))
    y_off = jnp.exp(cs_l) * _dot(cm, hprev, _NT)
    st = _dot(xdt * jnp.exp(cs_l[CHUNK - 1:CHUNK, :] - cs_l), bm, _TN)
    cdec = jnp.exp(cs[CHUNK - 1:CHUNK, :])
    cd_rows = jnp.concatenate(
        [jnp.broadcast_to(cdec[:, h:h + 1], (HEAD_DIM, N_STATE)) for h in range(HEADS_PER_GROUP)], axis=0)
    dsk_l = _head_lanes(jnp.broadcast_to(dsk, (8, 128)), 2)[:1]
    y = jnp.concatenate(yd, axis=1) + y_off + xs * dsk_l
    return y, cd_rows * hprev + st


def _ssd_span(xs, bm, cm, dtr, h0, dtb, alog, dsk):
    li = lax.broadcasted_iota(jnp.int32, (CHUNK, CHUNK), 0)
    si = lax.broadcasted_iota(jnp.int32, (CHUNK, CHUNK), 1)
    causal = li >= si
    k = (causal, causal.astype(F32), si < HEAD_DIM)
    h, ys = h0, []
    for t in range(xs.shape[0] // CHUNK):
        r = slice(t * CHUNK, (t + 1) * CHUNK)
        y, h = _ssd_chunk(k, xs[r], bm[r], cm[r], dtr[r], h, dtb, alog, dsk)
        ys.append(y)
    return (ys[0] if len(ys) == 1 else jnp.concatenate(ys, axis=0)), h


def _ssd_specs(rev, nsteps, rows):
    def s_of(s):
        return nsteps - 1 - s if rev else s

    xs = pl.BlockSpec((rows, _XS_GW), lambda g, s: (s_of(s), g))
    bm = pl.BlockSpec((rows, N_STATE), lambda g, s: (s_of(s), D_INNER // N_STATE + g))
    cm = pl.BlockSpec((rows, N_STATE), lambda g, s: (s_of(s), D_INNER // N_STATE + N_GROUPS + g))
    dt = pl.BlockSpec((rows, 128), lambda g, s: (s_of(s), g))
    par = pl.BlockSpec((1, 128), lambda g, s: (0, g))
    st = pl.BlockSpec((None, None, _XS_GW, N_STATE), lambda g, s: (g, s_of(s), 0, 0))
    return xs, bm, cm, dt, par, st


def _ssd_fwd(act, dtr, dtb, alog, dsk, *, nrows):
    span = CHUNK * min(SSD_SPAN, nrows // CHUNK)
    per_step = SSD_FWD_SPANS if nrows % (SSD_FWD_SPANS * span) == 0 else 1
    rows = per_step * span
    nsteps = nrows // rows
    xs, bm, cm, dt, par, _ = _ssd_specs(False, nsteps, rows)
    st = pl.BlockSpec((None, per_step, _XS_GW, N_STATE), lambda g, s: (g, s, 0, 0))

    def body(xs_ref, b_ref, c_ref, dt_ref, dtb_ref, al_ref, dk_ref, y_ref, st_ref, h_ref):
        @pl.when(pl.program_id(1) == 0)
        def _():
            h_ref[...] = jnp.zeros_like(h_ref)

        h = h_ref[...]
        for i in range(per_step):
            r = slice(i * span, (i + 1) * span)
            st_ref[i] = h
            y, h = _ssd_span(xs_ref[r, :].astype(F32), b_ref[r, :].astype(F32), c_ref[r, :].astype(F32), dt_ref[r, :],
                             h, dtb_ref[...], al_ref[...], dk_ref[...])
            y_ref[r, :] = y.astype(y_ref.dtype)
        h_ref[...] = h

    return pl.pallas_call(
        body, name="ssd_fwd", grid=(N_GROUPS, nsteps),
        in_specs=[xs, bm, cm, dt, par, par, par], out_specs=[xs, st],
        out_shape=[jax.ShapeDtypeStruct((nrows, D_INNER), ACT_DTYPE),
                   jax.ShapeDtypeStruct((N_GROUPS, nrows // span, _XS_GW, N_STATE), F32)],
        scratch_shapes=[pltpu.VMEM((_XS_GW, N_STATE), F32)],
        compiler_params=pltpu.CompilerParams(dimension_semantics=("arbitrary", "arbitrary")),
    )(act, act, act, dtr, dtb, alog, dsk)


def _ssd_bwd(act, dtr, dtb, alog, dsk, states, dy, *, nrows):
    rows = CHUNK * min(SSD_SPAN, nrows // CHUNK)
    nsteps = nrows // rows
    xs, bm, cm, dt, par, st = _ssd_specs(True, nsteps, rows)

    def body(xs_ref, b_ref, c_ref, dt_ref, dtb_ref, al_ref, dk_ref, st_ref, dy_ref,
             dxs_ref, db_ref, dc_ref, ddt_ref, ddtb_ref, dal_ref, ddk_ref, dh_ref):
        @pl.when(pl.program_id(1) == 0)
        def _():
            dh_ref[...] = jnp.zeros_like(dh_ref)
            ddtb_ref[...] = jnp.zeros_like(ddtb_ref)
            dal_ref[...] = jnp.zeros_like(dal_ref)
            ddk_ref[...] = jnp.zeros_like(ddk_ref)

        _, vjp = jax.vjp(_ssd_span, xs_ref[...].astype(F32), b_ref[...].astype(F32), c_ref[...].astype(F32),
                         dt_ref[...], st_ref[...], dtb_ref[...], al_ref[...], dk_ref[...])
        dxs, db, dc, ddt, dh, ddtb, dal, ddk = vjp((dy_ref[...].astype(F32), dh_ref[...]))
        dxs_ref[...] = dxs.astype(dxs_ref.dtype)
        db_ref[...] = db.astype(db_ref.dtype)
        dc_ref[...] = dc.astype(dc_ref.dtype)
        ddt_ref[...] = ddt
        dh_ref[...] = dh
        ddtb_ref[...] += ddtb
        dal_ref[...] += dal
        ddk_ref[...] += ddk

    nb = N_GROUPS * N_STATE
    bspec = pl.BlockSpec((rows, N_STATE), lambda g, s: (nsteps - 1 - s, g))
    return pl.pallas_call(
        body, name="ssd_bwd", grid=(N_GROUPS, nsteps),
        in_specs=[xs, bm, cm, dt, par, par, par, st, xs],
        out_specs=[xs, bspec, bspec, dt, par, par, par],
        out_shape=[jax.ShapeDtypeStruct((nrows, D_INNER), ACT_DTYPE), jax.ShapeDtypeStruct((nrows, nb), ACT_DTYPE),
                   jax.ShapeDtypeStruct((nrows, nb), ACT_DTYPE), jax.ShapeDtypeStruct((nrows, DT_W), F32),
                   jax.ShapeDtypeStruct((1, DT_W), F32), jax.ShapeDtypeStruct((1, DT_W), F32),
                   jax.ShapeDtypeStruct((1, DT_W), F32)],
        scratch_shapes=[pltpu.VMEM((_XS_GW, N_STATE), F32)],
        compiler_params=pltpu.CompilerParams(dimension_semantics=("arbitrary", "arbitrary")),
    )(act, act, act, dtr, dtb, alog, dsk, states, dy)


def _add_epilogue(acc, r):
    return r + acc


def _rms_and_skip(x, g):
    return _rms(x, g), x


def _forward_backward(x, p, tgt, w, late_weights=None, after=()):
    s = x.shape[0]
    act_t, f32 = ACT_DTYPE, F32
    mm = functools.partial(_matmul)
    h, proj, dtr = _rows_matmul("proj", _f_rms, [(x, D_MODEL, 0)], [w["norm_g"]], w["w_main"], out_dtype=act_t,
                                n=MAIN_W, k=D_MODEL, tm=1024, nrows=s, after=after, side=(w["w_dt"], f32))
    act, conv_pre = _conv_fwd(proj, w["conv_w"], w["conv_b"], tm=512, nrows=s)
    y, states = _ssd_fwd(act, dtr, w["dt_bias"], w["a_log"], w["d_skip"], nrows=s)
    if late_weights is not None:
        w = {**w, **late_weights(states)}
    a_pars = [w["ln_a_g"], w["ln_a_b"], w["w_s"], w["b_s"]]
    y_a, o_a = _rows_matmul("out_a", _f_branch_a, [(proj, UVZ_W, UVZ_CB)], a_pars, w["w_oa"], out_dtype=act_t,
                            n=D_MODEL, k=E_A, tm=512, nrows=s)
    gn_rows = [(y, D_INNER, 0), (proj, ZB_W, ZB_CB)]
    y_b, o_b = _rows_matmul("out_b", _f_gnorm, gn_rows, [w["ssm_norm_g"]], w["w_ob"], out_dtype=act_t,
                            n=D_MODEL, k=D_INNER, tm=1024, nrows=s)
    mg_rows = [(proj, G_W, G_CB), (o_a, D_MODEL, 0), (o_b, D_MODEL, 0)]
    merged, x1 = _rows_matmul("out_proj", _f_merge, mg_rows, [], w["w_out"], out_dtype=f32, n=D_MODEL, k=D_MODEL,
                              tm=1024, nrows=s, extras=(x,), epilogue=_add_epilogue)
    g = {}
    hp, dx2, dgp, dpe, g["final_g"], loss = _head(x1, p, tgt, w["ple_norm_g"], w["w_pg"], w["w_ple"], w["final_g"],
                                                   tm=512, nrows=s)
    g["w_pg"] = mm(hp, dgp, mode="tn", name="d_w_pg", out_dtype=f32, m=D_MODEL, n=D_MODEL, k=s)
    g["w_ple"] = mm(p, dpe, mode="tn", name="d_w_ple", out_dtype=f32, m=PLE_DIM, n=D_MODEL, k=s)
    dx1, g["ple_norm_g"] = _rows_vjp_call(
        "ple_norm_bwd", _rms_and_skip, [(x1, D_MODEL, 0)], [w["ple_norm_g"]], [(dx2, D_MODEL, 0)],
        [(f32, None)], tm=1024, nrows=s, cot_mm=(dgp, w["w_pg"], None))
    g["w_out"] = mm(merged, dx1, mode="tn", name="d_w_out", out_dtype=f32, m=D_MODEL, n=D_MODEL, k=s)
    dproj, do_a, do_b = _rows_vjp_call(
        "merge_bwd", _f_merge, mg_rows, [], [],
        [(act_t, (None, MAIN_W, G_CB)), (act_t, None), (act_t, None)], tm=1024, nrows=s, cot_mm=(dx1, w["w_out"], None))
    g["w_oa"] = mm(y_a, do_a, mode="tn", name="d_w_oa", out_dtype=f32, m=E_A, n=D_MODEL, k=s)
    g["w_ob"] = mm(y_b, do_b, mode="tn", name="d_w_ob", out_dtype=f32, m=D_INNER, n=D_MODEL, k=s)
    dy, dproj, g["ssm_norm_g"] = _rows_vjp_call(
        "gnorm_bwd", _f_gnorm, gn_rows, [w["ssm_norm_g"]], [],
        [(act_t, None), (act_t, (dproj, MAIN_W, ZB_CB))], tm=512, nrows=s, cot_mm=(do_b, w["w_ob"], None))
    dxs, dbm, dcm, ddtr, g["dt_bias"], g["a_log"], g["d_skip"] = _ssd_bwd(
        act, dtr, w["dt_bias"], w["a_log"], w["d_skip"], states, dy, nrows=s)
    dpre, g["conv_b"] = _conv_bwd_act(conv_pre, (dxs, dbm, dcm), tm=1024, nrows=s)
    dproj, g["conv_w"] = _conv_bwd_x(dpre, proj, w["conv_w"], dproj, tm=512, nrows=s)
    dproj, g["ln_a_g"], g["ln_a_b"], g["w_s"], g["b_s"] = _rows_vjp_call(
        "branch_a_bwd", _f_branch_a, [(proj, UVZ_W, UVZ_CB)], a_pars, [],
        [(act_t, (dproj, MAIN_W, UVZ_CB))], tm=512, nrows=s, cot_mm=(do_a, w["w_oa"], None))
    g["w_main"] = mm(h, dproj, mode="tn", name="d_w_main", out_dtype=f32, m=D_MODEL, n=MAIN_W, k=s)
    g["w_dt"] = mm(h, ddtr, mode="tn", name="d_w_dt", out_dtype=f32, m=D_MODEL, n=DT_W, k=s)
    return loss, g, (dproj, ddtr, dx1)


def _input_grad(x, w, ctx, after=()):
    dproj, ddtr, dx1 = ctx
    s = x.shape[0]
    dh = _matmul(dproj, w["w_main"], mode="nt", name="d_h_main", out_dtype=F32, m=s, n=D_MODEL, k=MAIN_W, after=after)
    return _rows_vjp_call(
        "pre_norm_bwd", _rms_and_skip, [(x, D_MODEL, 0)], [w["norm_g"]], [(dx1, D_MODEL, 0)],
        [(F32, None)], tm=1024, nrows=s, cot_mm=(ddtr, w["w_dt"], dh))


def _local_step(x, p, tgt, w):
    loss, g, ctx = _forward_backward(x, p, tgt, w)
    grad_x, g["norm_g"] = _input_grad(x, w, ctx)
    return loss, grad_x, g


_O_ZB = 3 * E_A
_O_XBC = _O_ZB + D_INNER
_O_DT = _O_XBC + CONV_DIM
_O_G = _O_DT + N_HEADS


def _heads_to_lanes(v):
    r = v.shape[0]
    v = v.reshape(r, N_GROUPS, HEADS_PER_GROUP)
    return jnp.pad(v, ((0, 0), (0, 0), (0, 128 - HEADS_PER_GROUP))).reshape(r, DT_W)


def _lanes_to_heads(v):
    r = v.shape[0]
    return v.reshape(r, N_GROUPS, 128)[:, :, :HEADS_PER_GROUP].reshape(r, N_HEADS)


def _block_cols(blocks, a, b):
    parts = []
    for k in range(N_CHIPS):
        lo, hi = max(a, k * W_IN_BLOCK), min(b, (k + 1) * W_IN_BLOCK)
        if lo < hi:
            parts.append(blocks[k][:, lo - k * W_IN_BLOCK:hi - k * W_IN_BLOCK])
    return parts


_W_IN_SEGMENTS = ((0, _O_ZB, "m", 0), (_O_ZB, _O_XBC, "m", UVZ_W + XBC_W), (_O_XBC, _O_DT, "m", UVZ_W),
                  (_O_DT, _O_G, "d", 0), (_O_G, N_IN, "m", MAIN_W - G_W))


def _w_in_grad_blocks(gm, gdt):
    blocks = []
    for k in range(N_CHIPS):
        a, b = k * W_IN_BLOCK, (k + 1) * W_IN_BLOCK
        parts = []
        for s, e, src, off in _W_IN_SEGMENTS:
            lo, hi = max(a, s), min(b, e)
            if lo < hi:
                parts.append((gm if src == "m" else gdt)[:, off + lo - s:off + hi - s])
        blocks.append(jnp.concatenate(parts, axis=1))
    return jnp.stack(blocks)


def _layout_weights(f, w_in_blocks=None):
    w = dict(f)
    if w_in_blocks is None:
        w_in = w.pop("w_in")
        w_in_blocks = jnp.stack([w_in[:, k * W_IN_BLOCK:(k + 1) * W_IN_BLOCK] for k in range(N_CHIPS)])
    cols = functools.partial(_block_cols, w_in_blocks)
    w["w_main"] = jnp.concatenate(cols(0, _O_ZB) + cols(_O_XBC, _O_DT) + cols(_O_ZB, _O_XBC) + cols(_O_G, N_IN), axis=1)
    w["w_dt"] = _heads_to_lanes(jnp.concatenate(cols(_O_DT, _O_G), axis=1))
    w["b_s"] = f["b_s"].reshape(G_A, CHUNK, 1)
    for n in ("dt_bias", "a_log", "d_skip"):
        w[n] = _heads_to_lanes(f[n])
    return w


def _natural_grads(g):
    out = dict(g)
    gm = out.pop("w_main")
    gdt = _lanes_to_heads(out.pop("w_dt"))
    out["w_in"] = jnp.concatenate(
        [gm[:, :UVZ_W], gm[:, UVZ_W + XBC_W:UVZ_W + XBC_W + ZB_W], gm[:, UVZ_W:UVZ_W + XBC_W], gdt, gm[:, MAIN_W - G_W:]],
        axis=1)
    out["b_s"] = g["b_s"].reshape(G_A, CHUNK)
    for n in ("dt_bias", "a_log", "d_skip"):
        out[n] = _lanes_to_heads(g[n])
    return out


def _place():
    return lax.axis_index("x"), lax.axis_index("y"), lax.axis_index("c")


def _other_chips(x, y):
    return [(1 - x, y), (x, 1 - y), (1 - x, 1 - y)]


def _rcopy(src, dst, ssem, rsem, dev):
    return pltpu.make_async_remote_copy(src_ref=src, dst_ref=dst, send_sem=ssem, recv_sem=rsem,
                                        device_id=dev, device_id_type=MESH)


def _half(ref_rows, half):
    hs = ref_rows // 2
    return pl.ds(pl.multiple_of(half * hs, 16), hs)


def _gather_weights(shards, conv_shard):
    nw = len(shards)

    def body(*refs):
        sh, cv = refs[:nw], refs[nw]
        out, cvo = refs[nw + 1:2 * nw + 1], refs[2 * nw + 1]
        ici_s, ici_r, fw_s, fw_r, own_s, own_r, cv_s, cv_r = refs[2 * nw + 2:]
        x, y, c = _place()
        me, sib, chips = 2 * x + y, (x, y, 1 - c), _other_chips(x, y)
        own = [_rcopy(sh[w], out[w].at[me], own_s.at[w], own_r.at[w], sib) for w in range(nw)]
        own.append(_rcopy(cv, cvo.at[me], own_s.at[nw], own_r.at[nw], sib))
        for cp in own:
            cp.start()
        sends = []
        for w in range(nw):
            mine = _half(sh[w].shape[0], c)
            for j, chip in enumerate(chips):
                sends.append(_rcopy(sh[w].at[mine], out[w].at[me, mine], ici_s.at[3 * w + j], ici_r.at[3 * w + j], (*chip, c)))
        for j, chip in enumerate(chips):
            sends.append(_rcopy(cv, cvo.at[me], cv_s.at[j], cv_r.at[j], (*chip, c)))
        for cp in sends:
            cp.start()
        for w in range(nw):
            mine = _half(sh[w].shape[0], c)
            for j, chip in enumerate(chips):
                slab = out[w].at[2 * chip[0] + chip[1], mine]
                _rcopy(slab, slab, ici_s.at[3 * w + j], ici_r.at[3 * w + j], (*chip, c)).wait_recv()
                fwd = _rcopy(slab, slab, fw_s.at[3 * w + j], fw_r.at[3 * w + j], sib)
                fwd.start()
                sends.append(fwd)
        for j, chip in enumerate(chips):
            blk = cvo.at[2 * chip[0] + chip[1]]
            _rcopy(blk, blk, cv_s.at[j], cv_r.at[j], (*chip, c)).wait_recv()
        for w in range(nw):
            theirs = _half(sh[w].shape[0], 1 - c)
            for j, chip in enumerate(chips):
                slab = out[w].at[2 * chip[0] + chip[1], theirs]
                _rcopy(slab, slab, fw_s.at[3 * w + j], fw_r.at[3 * w + j], sib).wait_recv()
        for cp in sends:
            cp.wait_send()
        for cp in own:
            cp.wait()

    dma = pltpu.SemaphoreType.DMA
    return pl.pallas_call(
        body, name="gather_weights",
        in_specs=[ANY] * (nw + 1), out_specs=[ANY] * (nw + 1),
        out_shape=[jax.ShapeDtypeStruct((N_CHIPS,) + s.shape, s.dtype) for s in shards]
        + [jax.ShapeDtypeStruct((N_CHIPS,) + conv_shard.shape, conv_shard.dtype)],
        scratch_shapes=[dma((3 * nw,)), dma((3 * nw,)), dma((3 * nw,)), dma((3 * nw,)), dma((nw + 1,)), dma((nw + 1,)),
                        dma((3,)), dma((3,))],
    )(*shards, conv_shard)


_HBM = pl.BlockSpec(memory_space=pltpu.HBM)
_SEM = pl.BlockSpec(memory_space=pltpu.SEMAPHORE)
_EFFECT = pltpu.SideEffectType.DATAFLOW_SIDE_EFFECTING


def _late_gather_copies(sh, out, s_sem, r_sem):
    x, y, c = _place()
    to = [(*chip, c) for chip in _other_chips(x, y)] + [(x, y, 1 - c)]
    return [_rcopy(sh[w], out[w].at[2 * x + y], s_sem.at[4 * w + j], r_sem.at[4 * w + j], dev)
            for w in range(len(sh)) for j, dev in enumerate(to)]


def _late_gather_start(shards):
    n = len(shards)
    lands = [lax.empty((N_CHIPS,) + a.shape, a.dtype) for a in shards]

    def body(*refs):
        for cp in _late_gather_copies(refs[:n], refs[n:2 * n], refs[2 * n], refs[2 * n + 1]):
            cp.start()
        refs[-1][...] = jnp.zeros_like(refs[-1])

    dma = pltpu.SemaphoreType.DMA
    hbm = [pltpu.with_memory_space_constraint(a, pltpu.HBM) for a in list(shards) + lands]
    out = pl.pallas_call(
        body, name="late_gather_start",
        out_shape=[dma((4 * n,)), dma((4 * n,))] + [pltpu.HBM(a.shape, a.dtype) for a in hbm]
        + [jax.ShapeDtypeStruct((8, 128), F32)],
        in_specs=[_HBM] * (2 * n), out_specs=[_SEM, _SEM] + [_HBM] * (2 * n) + [pl.BlockSpec(memory_space=pltpu.VMEM)],
        input_output_aliases={i: 2 + i for i in range(2 * n)},
        compiler_params=pltpu.CompilerParams(has_side_effects=_EFFECT),
    )(*hbm)
    return out[0], out[1], out[2:2 + n], out[2 + n:2 + 2 * n], out[-1]


def _late_gather_wait(s_sem, r_sem, srcs, lands, after):
    n = len(srcs)

    def body(*refs):
        for cp in _late_gather_copies(refs[:n], refs[n:2 * n], refs[2 * n], refs[2 * n + 1]):
            cp.wait_send()
            cp.wait_recv()

    out = pl.pallas_call(
        body, name="late_gather_wait",
        out_shape=[pltpu.HBM(a.shape, a.dtype) for a in list(srcs) + list(lands)],
        in_specs=[_HBM] * (2 * n) + [_SEM, _SEM, ANY], out_specs=[_HBM] * (2 * n),
        input_output_aliases={i: i for i in range(2 * n)},
        compiler_params=pltpu.CompilerParams(has_side_effects=_EFFECT),
    )(*srcs, *lands, s_sem, r_sem, after)
    return out[n:]


def _swap_with_sibling(arrs):
    n = len(arrs)

    def body(*refs):
        src, dst, s_sem, r_sem = refs[:n], refs[n:2 * n], refs[2 * n], refs[2 * n + 1]
        x, y, c = _place()
        cps = [_rcopy(src[i], dst[i], s_sem.at[i], r_sem.at[i], (x, y, 1 - c)) for i in range(n)]
        for cp in cps:
            cp.start()
        for cp in cps:
            cp.wait()

    dma = pltpu.SemaphoreType.DMA
    return pl.pallas_call(
        body, name="swap_with_sibling", in_specs=[ANY] * n, out_specs=[ANY] * n,
        out_shape=[jax.ShapeDtypeStruct(a.shape, a.dtype) for a in arrs], scratch_shapes=[dma((n,)), dma((n,))],
    )(*arrs)


def _scatter_copies(src, land, s_sem, r_sem):
    x, y, c = _place()
    return [_rcopy(src[i].at[2 * chip[0] + chip[1]], land[i].at[j], s_sem.at[3 * i + j], r_sem.at[3 * i + j], (*chip, c))
            for i in range(len(src)) for j, chip in enumerate(_other_chips(x, y))]


def _scatter_blocks_start(arrs):
    n = len(arrs)
    lands = [lax.empty((3,) + a.shape[1:], a.dtype) for a in arrs]

    def body(*refs):
        src, land, s_sem, r_sem, token = refs[:n], refs[n:2 * n], refs[2 * n], refs[2 * n + 1], refs[-1]
        for cp in _scatter_copies(src, land, s_sem, r_sem):
            cp.start()
        token[...] = jnp.zeros_like(token)

    dma = pltpu.SemaphoreType.DMA
    hbm = [pltpu.with_memory_space_constraint(a, pltpu.HBM) for a in list(arrs) + lands]
    out = pl.pallas_call(
        body, name="scatter_blocks_start",
        out_shape=[dma((3 * n,)), dma((3 * n,))] + [pltpu.HBM(a.shape, a.dtype) for a in hbm]
        + [jax.ShapeDtypeStruct((8, 128), F32)],
        in_specs=[_HBM] * (2 * n), out_specs=[_SEM, _SEM] + [_HBM] * (2 * n) + [pl.BlockSpec(memory_space=pltpu.VMEM)],
        input_output_aliases={i: 2 + i for i in range(2 * n)},
        compiler_params=pltpu.CompilerParams(has_side_effects=_EFFECT),
    )(*hbm)
    return out[0], out[1], out[2:2 + n], out[2 + n:2 + 2 * n], out[-1]


def _scatter_blocks_wait(s_sem, r_sem, srcs, lands, after):
    n = len(srcs)

    def body(*refs):
        src, land, s_sem, r_sem = refs[:n], refs[n:2 * n], refs[2 * n], refs[2 * n + 1]
        for cp in _scatter_copies(src, land, s_sem, r_sem):
            cp.wait_send()
            cp.wait_recv()

    out = pl.pallas_call(
        body, name="scatter_blocks_wait",
        out_shape=[pltpu.HBM(a.shape, a.dtype) for a in list(srcs) + list(lands)],
        in_specs=[_HBM] * (2 * n) + [_SEM, _SEM] + [ANY] * len(after), out_specs=[_HBM] * (2 * n),
        input_output_aliases={i: i for i in range(2 * n)},
        compiler_params=pltpu.CompilerParams(has_side_effects=_EFFECT),
    )(*srcs, *lands, s_sem, r_sem, *after)
    return out[:n], out[n:]


def _share_halves(arrs):
    n = len(arrs)

    def body(*refs):
        buf, s_sem, r_sem = refs[n:2 * n], refs[2 * n], refs[2 * n + 1]
        x, y, c = _place()
        cps = []
        for i in range(n):
            mine = buf[i].at[_half(buf[i].shape[0], c)]
            cps.append(_rcopy(mine, mine, s_sem.at[i], r_sem.at[i], (x, y, 1 - c)))
        for cp in cps:
            cp.start()
        for i in range(n):
            theirs = buf[i].at[_half(buf[i].shape[0], 1 - c)]
            _rcopy(theirs, theirs, s_sem.at[i], r_sem.at[i], (x, y, 1 - c)).wait_recv()
        for cp in cps:
            cp.wait_send()

    dma = pltpu.SemaphoreType.DMA
    return pl.pallas_call(
        body, name="share_halves", in_specs=[ANY] * n, out_specs=[ANY] * n,
        out_shape=[jax.ShapeDtypeStruct(a.shape, a.dtype) for a in arrs],
        input_output_aliases={i: i for i in range(n)}, scratch_shapes=[dma((n,)), dma((n,))],
    )(*arrs)


def _small_gather_copies(src, land, s_sem, r_sem):
    x, y, c = _place()
    cps = []
    for d in range(1, N_DEV):
        peer = ((1 - x) if d & 4 else x), ((1 - y) if d & 2 else y), ((1 - c) if d & 1 else c)
        cps.append(_rcopy(src, land.at[4 * x + 2 * y + c], s_sem.at[d - 1], r_sem.at[d - 1], peer))
    return cps


def _small_gather_start(packed):
    def body(src, land, s_sem, r_sem, _, __, token):
        for cp in _small_gather_copies(src, land, s_sem, r_sem):
            cp.start()
        token[...] = jnp.zeros_like(token)

    dma = pltpu.SemaphoreType.DMA
    hbm = [pltpu.with_memory_space_constraint(a, pltpu.HBM) for a in (packed, lax.empty((N_DEV,) + packed.shape, F32))]
    return pl.pallas_call(
        body, name="small_gather_start",
        out_shape=[dma((N_DEV - 1,)), dma((N_DEV - 1,))] + [pltpu.HBM(a.shape, a.dtype) for a in hbm]
        + [jax.ShapeDtypeStruct((8, 128), F32)],
        in_specs=[_HBM] * 2, out_specs=[_SEM, _SEM, _HBM, _HBM, pl.BlockSpec(memory_space=pltpu.VMEM)],
        input_output_aliases={0: 2, 1: 3}, compiler_params=pltpu.CompilerParams(has_side_effects=_EFFECT),
    )(*hbm)


def _small_gather_wait(s_sem, r_sem, src, land, after):
    def body(src, land, s_sem, r_sem, *_):
        for cp in _small_gather_copies(src, land, s_sem, r_sem):
            cp.wait_send()
            cp.wait_recv()

    return pl.pallas_call(
        body, name="small_gather_wait", out_shape=[pltpu.HBM(src.shape, src.dtype), pltpu.HBM(land.shape, land.dtype)],
        in_specs=[_HBM, _HBM, _SEM, _SEM, ANY], out_specs=[_HBM, _HBM], input_output_aliases={0: 0, 1: 1},
        compiler_params=pltpu.CompilerParams(has_side_effects=_EFFECT),
    )(src, land, s_sem, r_sem, after)


def _small_sum(own, land, dev_arr):
    def body(me_ref, own_ref, land_ref, o_ref):
        acc = jnp.zeros(o_ref.shape, F32)
        for d in range(N_DEV):
            acc = acc + jnp.where(me_ref[0] == d, own_ref[...], land_ref[d])
        o_ref[...] = acc

    return pl.pallas_call(
        body, name="small_sum", out_shape=jax.ShapeDtypeStruct(own.shape, F32),
        grid_spec=pltpu.PrefetchScalarGridSpec(
            num_scalar_prefetch=1, grid=(1,),
            in_specs=[pl.BlockSpec(own.shape, lambda i, m: (0, 0)), pl.BlockSpec(land.shape, lambda i, m: (0, 0, 0))],
            out_specs=pl.BlockSpec(own.shape, lambda i, m: (0, 0))),
    )(dev_arr, own, land)


def _row_tile(rows, cols):
    tr = max(8, min(rows, (1 << 20) // (4 * cols) // 8 * 8))
    while rows % tr:
        tr -= 8
    return tr


def _chip_sum(name, g5, recv, c_arr):
    nb, _, hs, cols = g5.shape
    tr = _row_tile(hs, cols)

    def body(_, a_ref, b_ref, o_ref):
        o_ref[...] = (a_ref[...] + b_ref[...].astype(F32)).astype(o_ref.dtype)

    blk = pl.BlockSpec((None, tr, cols), lambda b, i, c: (b, i, 0))
    return pl.pallas_call(
        body, name=name,
        grid_spec=pltpu.PrefetchScalarGridSpec(
            num_scalar_prefetch=1, grid=(nb, hs // tr),
            in_specs=[pl.BlockSpec((None, None, tr, cols), lambda b, i, c: (b, c[0], i, 0)), blk], out_specs=blk),
        out_shape=jax.ShapeDtypeStruct((nb, hs, cols), WIRE_DTYPE),
    )(c_arr, g5, recv)


def _final_sum(name, own, recv, place_arr):
    _, hs, cols = own.shape
    tr = _row_tile(hs, cols)
    nt = hs // tr

    def body(_, a_ref, r_ref, o_ref):
        o_ref[...] = ((a_ref[...].astype(F32) + r_ref[0].astype(F32)) + r_ref[1].astype(F32)) + r_ref[2].astype(F32)

    return pl.pallas_call(
        body, name=name,
        grid_spec=pltpu.PrefetchScalarGridSpec(
            num_scalar_prefetch=1, grid=(nt,),
            in_specs=[pl.BlockSpec((None, tr, cols), lambda i, m: (m[0], i, 0)),
                      pl.BlockSpec((3, tr, cols), lambda i, m: (0, i, 0))],
            out_specs=pl.BlockSpec((tr, cols), lambda i, m: (m[1] * nt + i, 0))),
        out_shape=jax.ShapeDtypeStruct((2 * hs, cols), F32),
    )(place_arr, own, recv)


def _adamw(w, g, m, v):
    m = ADAM_B1 * m + (1.0 - ADAM_B1) * g
    v = ADAM_B2 * v + (1.0 - ADAM_B2) * (g * g)
    m_hat = m / (1.0 - ADAM_B1 ** ADAM_STEP)
    v_hat = v / (1.0 - ADAM_B2 ** ADAM_STEP)
    return -ADAM_LR * (m_hat / (jnp.sqrt(v_hat) + ADAM_EPS) + ADAM_WD * w), m, v


def _adamw_call(name, w, g, m, v):
    rows, cols = w.shape
    tr = _row_tile(rows, cols)
    if 4 * tr * cols >= (1 << 18):
        blk, steps = pl.BlockSpec((tr, cols), lambda i: (i, 0)), rows // tr
    else:
        blk, steps = pl.BlockSpec((rows, 128), lambda i: (0, i)), cols // 128

    def body(w_ref, g_ref, m_ref, v_ref, d_ref, nm_ref, nv_ref, go_ref):
        g = g_ref[...]
        d_ref[...], nm_ref[...], nv_ref[...] = _adamw(w_ref[...], g, m_ref[...], v_ref[...])
        go_ref[...] = g

    return pl.pallas_call(
        body, name=name, grid=(steps,), in_specs=[blk] * 4, out_specs=[blk] * 4,
        out_shape=[jax.ShapeDtypeStruct(w.shape, F32)] * 4,
        compiler_params=pltpu.CompilerParams(dimension_semantics=("parallel",)),
    )(w, g, m, v)


def _adamw_small(ws, gs, ms, vs):
    n = len(ws)

    def body(*refs):
        for i in range(n):
            w_ref, g_ref, m_ref, v_ref = (refs[k * n + i] for k in range(4))
            d, nm, nv = _adamw(w_ref[...], g_ref[...], m_ref[...], v_ref[...])
            refs[4 * n + i][...] = d
            refs[5 * n + i][...] = nm
            refs[6 * n + i][...] = nv

    out = pl.pallas_call(
        body, name="adamw_small", out_shape=[jax.ShapeDtypeStruct(a.shape, F32) for a in ws] * 3,
    )(*ws, *gs, *ms, *vs)
    return out[:n], out[n:2 * n], out[2 * n:]


_BIG = ("w_in", "w_oa", "w_ob", "w_out", "w_pg", "w_ple")
_SMALL = ("norm_g", "ln_a_g", "ln_a_b", "w_s", "b_s", "conv_w", "conv_b", "dt_bias", "a_log", "d_skip", "ssm_norm_g",
          "ple_norm_g", "final_g")
_WEIGHTS = ("norm_g", "w_in", "ln_a_g", "ln_a_b", "w_s", "b_s", "conv_w", "conv_b", "dt_bias", "a_log", "d_skip",
            "ssm_norm_g", "w_oa", "w_ob", "w_out", "ple_norm_g", "w_pg", "w_ple", "final_g")
_COL_SHARDED = ("w_in", "w_ple")
_PACK = 1024


def _blocks_to_full(col_sharded, blocks):
    if col_sharded:
        return jnp.concatenate([blocks[k] for k in range(N_CHIPS)], axis=1)
    return blocks.reshape(N_CHIPS * blocks.shape[1], blocks.shape[2])


def _full_to_blocks(col_sharded, full):
    if col_sharded:
        w = full.shape[1] // N_CHIPS
        return jnp.stack([full[:, k * w:(k + 1) * w] for k in range(N_CHIPS)])
    return full.reshape(N_CHIPS, full.shape[0] // N_CHIPS, full.shape[1])


def _two_d(n, a):
    if n == "w_s":
        return a.reshape(G_A * CHUNK, CHUNK)
    if n in ("b_s", "conv_w"):
        return a.reshape(a.shape[-2], a.shape[-1])
    return a.reshape(1, a.shape[-1])


def kernel(x, p, norm_g, w_in, ln_a_g, ln_a_b, w_s, b_s, conv_w, conv_b, dt_bias, a_log, d_skip, ssm_norm_g, w_oa, w_ob, w_out, ple_norm_g, w_pg, w_ple, final_g, loss_target, m_norm_g, m_w_in, m_ln_a_g, m_ln_a_b, m_w_s, m_b_s, m_conv_w, m_conv_b, m_dt_bias, m_a_log, m_d_skip, m_ssm_norm_g, m_w_oa, m_w_ob, m_w_out, m_ple_norm_g, m_w_pg, m_w_ple, m_final_g, v_norm_g, v_w_in, v_ln_a_g, v_ln_a_b, v_w_s, v_b_s, v_conv_w, v_conv_b, v_dt_bias, v_a_log, v_d_skip, v_ssm_norm_g, v_w_oa, v_w_ob, v_w_out, v_ple_norm_g, v_w_pg, v_w_ple, v_final_g):
    wt = dict(norm_g=norm_g, w_in=w_in, ln_a_g=ln_a_g, ln_a_b=ln_a_b, w_s=w_s, b_s=b_s, conv_w=conv_w, conv_b=conv_b,
              dt_bias=dt_bias, a_log=a_log, d_skip=d_skip, ssm_norm_g=ssm_norm_g, w_oa=w_oa, w_ob=w_ob, w_out=w_out,
              ple_norm_g=ple_norm_g, w_pg=w_pg, w_ple=w_ple, final_g=final_g)
    mom = dict(norm_g=m_norm_g, w_in=m_w_in, ln_a_g=m_ln_a_g, ln_a_b=m_ln_a_b, w_s=m_w_s, b_s=m_b_s, conv_w=m_conv_w,
               conv_b=m_conv_b, dt_bias=m_dt_bias, a_log=m_a_log, d_skip=m_d_skip, ssm_norm_g=m_ssm_norm_g, w_oa=m_w_oa,
               w_ob=m_w_ob, w_out=m_w_out, ple_norm_g=m_ple_norm_g, w_pg=m_w_pg, w_ple=m_w_ple, final_g=m_final_g)
    vel = dict(norm_g=v_norm_g, w_in=v_w_in, ln_a_g=v_ln_a_g, ln_a_b=v_ln_a_b, w_s=v_w_s, b_s=v_b_s, conv_w=v_conv_w,
               conv_b=v_conv_b, dt_bias=v_dt_bias, a_log=v_a_log, d_skip=v_d_skip, ssm_norm_g=v_ssm_norm_g, w_oa=v_w_oa,
               w_ob=v_w_ob, w_out=v_w_out, ple_norm_g=v_ple_norm_g, w_pg=v_w_pg, w_ple=v_w_ple, final_g=v_final_g)
    xi, yi, ci = _place()
    me = 2 * xi + yi
    c_arr = jnp.reshape(ci, (1,)).astype(jnp.int32)
    place_arr = jnp.stack([me, ci]).astype(jnp.int32)

    shard = {n: wt[n][0] for n in _BIG}
    wire = {n: shard[n].astype(WIRE_DTYPE) for n in _BIG}
    w_in_blocks, conv_blocks = _gather_weights([wire["w_in"]], conv_w[0])
    g_ssem, g_rsem, g_sent, g_lands, g_token = _late_gather_start([wire[n] for n in _BIG[1:]])
    full = {"conv_w": _blocks_to_full(True, conv_blocks)}
    for n in _SMALL:
        if n != "conv_w":
            full[n] = wt[n][0] if wt[n].ndim > 2 else wt[n].reshape(1, wt[n].shape[-1])

    def late_weights(after):
        blocks = _late_gather_wait(g_ssem, g_rsem, g_sent, g_lands, after)
        return {n: _blocks_to_full(n in _COL_SHARDED, b) for n, b in zip(_BIG[1:], blocks)}

    w = _layout_weights(full, w_in_blocks=w_in_blocks)
    loss_row, g, ctx = _forward_backward(x[0], p[0, 0], loss_target[0], w, late_weights, after=(g_token,))
    loss = lax.psum(loss_row[0, 0], ("x", "y", "c"))

    parts = {n: _full_to_blocks(n in _COL_SHARDED, g[n]) for n in _BIG[1:]}
    parts["w_main"] = g["w_main"][None]
    parts["w_dt"] = jnp.pad(_lanes_to_heads(g["w_dt"]), ((0, 0), (0, 128 - N_HEADS)))[None]
    names = ("w_main", "w_dt") + _BIG[1:]
    g5 = {n: parts[n].reshape(parts[n].shape[0], 2, parts[n].shape[1] // 2, parts[n].shape[2]) for n in names}
    to_sibling = [lax.dynamic_index_in_dim(g5[n], 1 - ci, axis=1, keepdims=False).astype(WIRE_DTYPE) for n in names]
    from_sibling = _swap_with_sibling(to_sibling)
    chip = {n: _chip_sum("chip_sum_" + n, g5[n], r, c_arr) for n, r in zip(names, from_sibling)}
    chip["w_in"] = _w_in_grad_blocks(chip["w_main"][0], chip["w_dt"][0])
    chip_wire = [chip[n] for n in _BIG]
    s_sem, r_sem, sent, lands, token = _scatter_blocks_start(chip_wire)
    grad_x, g["norm_g"] = _input_grad(x[0], w, ctx, after=(token,))
    g = _natural_grads(g)

    pieces = [_two_d(n, g[n]).reshape(-1) for n in _SMALL]
    sizes = [v.shape[0] for v in pieces]
    padded = [-(-s // _PACK) * _PACK for s in sizes]
    packed = jnp.concatenate([jnp.pad(v, (0, ps - s)) for v, s, ps in zip(pieces, sizes, padded)]).reshape(-1, 128)
    a_ssem, a_rsem, a_src, a_land, a_token = _small_gather_start(packed)

    sent, from_chips = _scatter_blocks_wait(s_sem, r_sem, sent, lands, (grad_x, a_token))
    halves = [_final_sum("final_sum_" + n, a, r, place_arr) for n, a, r in zip(_BIG, sent, from_chips)]
    grads = dict(zip(_BIG, _share_halves(halves)))
    delta, new_m, new_v = {}, {}, {}
    for n in _BIG:
        t = jnp.transpose if n == "w_in" else (lambda a: a)
        res = _adamw_call("adamw_" + n, t(shard[n]), t(grads[n]), t(mom[n][0]), t(vel[n][0]))
        delta[n], new_m[n], new_v[n], grads[n] = (t(r) for r in res)

    a_src, a_land = _small_gather_wait(a_ssem, a_rsem, a_src, a_land, delta["w_in"])
    summed = _small_sum(a_src, a_land, jnp.reshape(4 * xi + 2 * yi + ci, (1,)).astype(jnp.int32)).reshape(-1)
    off = 0
    for n, s, ps in zip(_SMALL, sizes, padded):
        grads[n] = summed[off:off + s].reshape(_two_d(n, g[n]).shape)
        off += ps
    grads["conv_w"] = lax.dynamic_slice_in_dim(grads["conv_w"], me * (CONV_DIM // N_CHIPS), CONV_DIM // N_CHIPS, axis=1)
    small = _adamw_small([_two_d(n, wt[n]) for n in _SMALL], [grads[n] for n in _SMALL],
                         [_two_d(n, mom[n]) for n in _SMALL], [_two_d(n, vel[n]) for n in _SMALL])
    for i, n in enumerate(_SMALL):
        delta[n], new_m[n], new_v[n] = small[0][i], small[1][i], small[2][i]

    def shaped(d):
        return [d[n].reshape(wt[n].shape) for n in _WEIGHTS]

    return (loss, grad_x[None], *shaped(grads), *shaped(delta), *shaped(new_m), *shaped(new_v))
```

```python
import functools

import jax
import jax.numpy as jnp
from jax import lax
from jax.experimental import pallas as pl
from jax.experimental.pallas import tpu as pltpu

F32 = jnp.float32
MXU_DTYPE = jnp.bfloat16
ACT_DTYPE = jnp.bfloat16
WIRE_DTYPE = jnp.bfloat16

D_MODEL = 1024
PLE_DIM = 256
CHUNK = 128
EPS = 1e-6
E_A = D_MODEL
G_A = 4
D_INNER = 2 * D_MODEL
HEAD_DIM = 64
N_HEADS = D_INNER // HEAD_DIM
N_STATE = 128
N_GROUPS = 4
HEADS_PER_GROUP = N_HEADS // N_GROUPS
PAIRS_PER_GROUP = HEADS_PER_GROUP // 2
CONV_K = 4
CONV_DIM = D_INNER + 2 * N_GROUPS * N_STATE
N_IN = 3 * E_A + D_INNER + CONV_DIM + N_HEADS + 2 * D_MODEL
N_CHIPS = 4
N_DEV = 8
W_IN_BLOCK = N_IN // N_CHIPS

UVZ_W, XBC_W, ZB_W, G_W = 3 * E_A, CONV_DIM, D_INNER, 2 * D_MODEL
MAIN_W = UVZ_W + XBC_W + ZB_W + G_W
UVZ_CB, XBC_CB, ZB_CB, G_CB = 0, 1, 3, 4
DT_W = N_GROUPS * 128

ADAM_LR, ADAM_B1, ADAM_B2, ADAM_EPS, ADAM_WD, ADAM_STEP = 0.001, 0.9, 0.999, 1e-08, 0.01, 10

MESH = pl.DeviceIdType.MESH
ANY = pl.BlockSpec(memory_space=pl.ANY)


def _mxu(v):
    return v.astype(MXU_DTYPE)


def _dot(a, b, dims=(((1,), (0,)), ((), ()))):
    return lax.dot_general(_mxu(a), _mxu(b), dims, preferred_element_type=F32)


V7X_MXU_WIDTH = 256
V7X_SCOPED_VMEM_BYTES = 60000 * 1024
MM_TILE = 4 * V7X_MXU_WIDTH
MM_VMEM_BUDGET = 4 * V7X_SCOPED_VMEM_BYTES // 5


def _mm_tk(m, n, k, tm, tn, a_bytes, b_bytes, out_bytes, extra_bytes):
    for parts in range(1, k // 128 + 1):
        if k % parts or (k // parts) % 128 and parts > 1:
            continue
        tk = k // parts
        need = 2 * tk * (tm * a_bytes + tn * b_bytes) + 2 * tm * tn * (out_bytes + extra_bytes) + (tm * tn * 4 if parts > 1 else 0)
        if need <= MM_VMEM_BUDGET:
            return tk
    return 128


def _matmul(a, b, *, mode, name, out_dtype, m, n, k, tm=MM_TILE, tn=MM_TILE, tk=None, a_off=0, b_off=0,
            extras=(), epilogue=None, after=()):
    tm, tn = min(tm, m), min(tn, n)
    if tk is None:
        tk = _mm_tk(m, n, k, tm, tn, a.dtype.itemsize, b.dtype.itemsize, jnp.dtype(out_dtype).itemsize,
                    sum(e.dtype.itemsize for e in extras))
    tk = min(tk, k)
    assert m % tm == 0 and n % tn == 0 and k % tk == 0, (name, m, n, k, tm, tn, tk)
    nk = k // tk
    if mode == "nn":
        assert a_off % tk == 0 and b_off % tn == 0
        a_spec = pl.BlockSpec((tm, tk), lambda i, j, kk: (i, kk + a_off // tk))
        b_spec = pl.BlockSpec((tk, tn), lambda i, j, kk: (kk, j + b_off // tn))
        dims = (((1,), (0,)), ((), ()))
    elif mode == "nt":
        a_spec = pl.BlockSpec((tm, tk), lambda i, j, kk: (i, kk))
        b_spec = pl.BlockSpec((tn, tk), lambda i, j, kk: (j, kk))
        dims = (((1,), (1,)), ((), ()))
    else:
        assert a_off % tm == 0 and b_off % tn == 0
        a_spec = pl.BlockSpec((tk, tm), lambda i, j, kk: (kk, i + a_off // tm))
        b_spec = pl.BlockSpec((tk, tn), lambda i, j, kk: (kk, j + b_off // tn))
        dims = (((0,), (0,)), ((), ()))
    ne = len(extras)

    def finish(acc, extra_refs, o_ref):
        res = acc if epilogue is None else epilogue(acc, *[e[...] for e in extra_refs])
        o_ref[...] = res.astype(o_ref.dtype)

    def body(a_ref, b_ref, *rest):
        extra_refs, o_ref = rest[:ne], rest[ne + len(after)]
        part = _dot(a_ref[...], b_ref[...], dims)
        if nk == 1:
            finish(part, extra_refs, o_ref)
            return
        acc_ref = rest[ne + len(after) + 1]
        kk = pl.program_id(2)

        @pl.when(kk == 0)
        def _():
            acc_ref[...] = part

        @pl.when(kk > 0)
        def _():
            acc_ref[...] += part

        @pl.when(kk == nk - 1)
        def _():
            finish(acc_ref[...], extra_refs, o_ref)

    o_spec = pl.BlockSpec((tm, tn), lambda i, j, kk: (i, j))
    return pl.pallas_call(
        body, name=name, grid=(m // tm, n // tn, nk),
        in_specs=[a_spec, b_spec] + [o_spec] * ne + [ANY] * len(after), out_specs=o_spec,
        out_shape=jax.ShapeDtypeStruct((m, n), out_dtype),
        scratch_shapes=[pltpu.VMEM((tm, tn), F32)] if nk > 1 else [],
        compiler_params=pltpu.CompilerParams(dimension_semantics=("parallel", "parallel", "arbitrary")),
    )(a, b, *extras, *after)


def _rows_matmul(name, f, rows, pars, b, *, out_dtype, n, k, tm, nrows, tn=MM_TILE, extras=(), epilogue=None, after=(),
                 side=None):
    tm, tn = min(tm, nrows), min(tn, n)
    assert nrows % tm == 0 and n % tn == 0, (name, nrows, n, tm, tn)
    nr, npar, ne, nj = len(rows), len(pars), len(extras), n // tn
    ns = 0 if side is None else 1
    n_in = nr + npar + 1 + ne + ns + len(after)

    def body(*refs):
        row_refs, par_refs, b_ref = refs[:nr], refs[nr:nr + npar], refs[nr + npar]
        extra_refs = refs[nr + npar + 1:nr + npar + 1 + ne]
        a_ref, o_ref = refs[n_in], refs[n_in + 1]

        def make_a():
            a = f(*[r[...].astype(F32) for r in row_refs], *[p[...] for p in par_refs])[0]
            a_ref[...] = a.astype(a_ref.dtype)
            if ns:
                refs[n_in + 2][...] = _dot(a_ref[...], refs[nr + npar + 1 + ne][...]).astype(refs[n_in + 2].dtype)

        if nj == 1:
            make_a()
        else:
            pl.when(pl.program_id(1) == 0)(make_a)
        res = _dot(a_ref[...], b_ref[...])
        if epilogue is not None:
            res = epilogue(res, *[e[...] for e in extra_refs])
        o_ref[...] = res.astype(o_ref.dtype)

    o_spec = pl.BlockSpec((tm, tn), lambda i, j: (i, j))
    side_in = [] if side is None else [pl.BlockSpec(tuple(side[0].shape), lambda i, j: (0, 0))]
    side_out = [] if side is None else [pl.BlockSpec((tm, side[0].shape[1]), lambda i, j: (i, 0))]
    side_shape = [] if side is None else [jax.ShapeDtypeStruct((nrows, side[0].shape[1]), side[1])]
    return pl.pallas_call(
        body, name=name, grid=(nrows // tm, nj),
        in_specs=[pl.BlockSpec((tm, w), lambda i, j, cb=cb: (i, cb)) for _, w, cb in rows]
        + [pl.BlockSpec(tuple(p.shape), lambda i, j, nd=p.ndim: (0,) * nd) for p in pars]
        + [pl.BlockSpec((k, tn), lambda i, j: (0, j))] + [o_spec] * ne + side_in + [ANY] * len(after),
        out_specs=[pl.BlockSpec((tm, k), lambda i, j: (i, 0)), o_spec] + side_out,
        out_shape=[jax.ShapeDtypeStruct((nrows, k), ACT_DTYPE), jax.ShapeDtypeStruct((nrows, n), out_dtype)] + side_shape,
        compiler_params=pltpu.CompilerParams(dimension_semantics=("parallel", "arbitrary")),
    )(*[r[0] for r in rows], *pars, b, *extras, *([] if side is None else [side[0]]), *after)


def _row_spec(tm, width, cb):
    return pl.BlockSpec((tm, width), lambda i: (i, cb))


def _whole_spec(shape):
    nd = len(shape)
    return pl.BlockSpec(tuple(shape), lambda i: (0,) * nd)


def _rows_vjp_call(name, f, rows, pars, cots, drows, *, tm, nrows, cot_mm=None):
    tm = min(tm, nrows)
    nr, npar, nc = len(rows), len(pars), len(cots)
    mm_args, mm_specs = [], []
    if cot_mm is not None:
        mm_a, mm_b, mm_add = cot_mm
        mm_args = [mm_a, mm_b] + ([] if mm_add is None else [mm_add])
        mm_specs = [_row_spec(tm, mm_a.shape[1], 0), _whole_spec(mm_b.shape)]
        mm_specs += [] if mm_add is None else [_row_spec(tm, mm_b.shape[0], 0)]
    alias_bufs, aliases = [], {}
    out_shape, out_specs = [], []
    for (arr, w, cb), d in zip(rows, drows):
        if d is None:
            continue
        dt, into = d
        if into is None:
            out_shape.append(jax.ShapeDtypeStruct((nrows, w), dt))
            out_specs.append(_row_spec(tm, w, 0))
        else:
            buf, total, ocb = into
            if buf is not None:
                aliases[nr + npar + nc + len(alias_bufs)] = len(out_shape)
                alias_bufs.append(buf)
            out_shape.append(jax.ShapeDtypeStruct((nrows, total), dt))
            out_specs.append(_row_spec(tm, w, ocb))
    n_drow = len(out_shape)
    for p in pars:
        out_shape.append(jax.ShapeDtypeStruct(p.shape, F32))
        out_specs.append(_whole_spec(p.shape))
    na = len(alias_bufs)

    def body(*refs):
        rv = [r[...].astype(F32) for r in refs[:nr]]
        pv = [p[...] for p in refs[nr:nr + npar]]
        cv = tuple(c[...].astype(F32) for c in refs[nr + npar:nr + npar + nc])
        o_refs = refs[nr + npar + nc + na + len(mm_args):]
        if mm_args:
            mm_refs = refs[nr + npar + nc + na:nr + npar + nc + na + len(mm_args)]
            c0 = _dot(mm_refs[0][...], mm_refs[1][...], (((1,), (1,)), ((), ())))
            if len(mm_refs) == 3:
                c0 = c0 + mm_refs[2][...].astype(F32)
            cv = (c0,) + cv
        _, vjp = jax.vjp(f, *rv, *pv)
        g = vjp(cv)
        oi = 0
        for ri, d in enumerate(drows):
            if d is not None:
                o_refs[oi][...] = g[ri].astype(o_refs[oi].dtype)
                oi += 1
        first = pl.program_id(0) == 0
        for pi in range(npar):
            acc = o_refs[n_drow + pi]

            @pl.when(first)
            def _(acc=acc):
                acc[...] = jnp.zeros_like(acc)

            acc[...] += g[nr + pi]

    return pl.pallas_call(
        body, name=name, grid=(nrows // tm,),
        in_specs=[_row_spec(tm, w, cb) for _, w, cb in rows] + [_whole_spec(p.shape) for p in pars]
        + [_row_spec(tm, w, cb) for _, w, cb in cots] + [ANY] * na + mm_specs,
        out_specs=out_specs, out_shape=out_shape, input_output_aliases=aliases,
        compiler_params=pltpu.CompilerParams(dimension_semantics=("arbitrary",)),
    )(*[r[0] for r in rows], *pars, *[c[0] for c in cots], *alias_bufs, *mm_args)


def _rms(x, g):
    return x * lax.rsqrt(jnp.mean(x * x, axis=-1, keepdims=True) + EPS) * g


def _f_rms(x, g):
    return (_rms(x, g),)


def _tril_mask():
    return lax.broadcasted_iota(jnp.int32, (CHUNK, CHUNK), 0) >= lax.broadcasted_iota(jnp.int32, (CHUNK, CHUNK), 1)


def _f_branch_a(uvz, ln_g, ln_b, w_s, b_s):
    u = jax.nn.gelu(uvz[:, :E_A])
    v = jax.nn.gelu(uvz[:, E_A:2 * E_A])
    z = uvz[:, 2 * E_A:]
    xc = v - jnp.mean(v, axis=-1, keepdims=True)
    vn = xc * lax.rsqrt(jnp.mean(xc * xc, axis=-1, keepdims=True) + EPS) * ln_g + ln_b
    mask = _tril_mask()
    ws = [jnp.where(mask, w_s[g], 0.0) for g in range(G_A)]
    gw = E_A // G_A
    rows = []
    for c in range(uvz.shape[0] // CHUNK):
        vc = vn[c * CHUNK:(c + 1) * CHUNK]
        rows.append(jnp.concatenate([_dot(ws[g], vc[:, g * gw:(g + 1) * gw]) + b_s[g] for g in range(G_A)], axis=1))
    sv = rows[0] if len(rows) == 1 else jnp.concatenate(rows, axis=0)
    return (u * sv * jax.nn.silu(z),)


def _f_gnorm(y, zb, g):
    yz = y * jax.nn.silu(zb)
    gw = D_INNER // N_GROUPS
    parts = []
    for i in range(N_GROUPS):
        s = yz[:, i * gw:(i + 1) * gw]
        parts.append(s * lax.rsqrt(jnp.mean(s * s, axis=-1, keepdims=True) + EPS))
    return (jnp.concatenate(parts, axis=1) * g,)


def _f_merge(g2, oa, ob):
    return (jax.nn.sigmoid(g2[:, :D_MODEL]) * oa + jax.nn.sigmoid(g2[:, D_MODEL:]) * ob,)


def _f_loss(x1, gp, pe, tgt, fg):
    x2 = x1 + jax.nn.sigmoid(gp) * pe
    err = _rms(x2, fg) - tgt
    return 0.5 * jnp.sum(jnp.mean(err * err, axis=-1))


def _head(x1, p, tgt, ple_g, w_pg, w_ple, fg, *, tm, nrows):
    tm = min(tm, nrows)

    def body(x1_ref, p_ref, t_ref, pg_ref, wpg_ref, wple_ref, fg_ref, hp_ref, dx_ref, dgp_ref, dpe_ref, dfg_ref, loss_ref):
        x1 = x1_ref[...]
        hp_ref[...] = _rms(x1, pg_ref[...]).astype(hp_ref.dtype)
        gp = _dot(hp_ref[...], wpg_ref[...])
        pe = _dot(p_ref[...], wple_ref[...])
        loss, vjp = jax.vjp(_f_loss, x1, gp, pe, t_ref[...], fg_ref[...])
        dx, dgp, dpe, _, dfg = vjp(jnp.ones((), F32))
        dx_ref[...] = dx
        dgp_ref[...] = dgp.astype(dgp_ref.dtype)
        dpe_ref[...] = dpe.astype(dpe_ref.dtype)

        @pl.when(pl.program_id(0) == 0)
        def _():
            dfg_ref[...] = jnp.zeros_like(dfg_ref)
            loss_ref[...] = jnp.zeros_like(loss_ref)

        dfg_ref[...] += dfg
        loss_ref[...] += jnp.full(loss_ref.shape, loss, F32)

    row = _row_spec(tm, D_MODEL, 0)
    act = jax.ShapeDtypeStruct((nrows, D_MODEL), ACT_DTYPE)
    return pl.pallas_call(
        body, name="head", grid=(nrows // tm,),
        in_specs=[row, _row_spec(tm, PLE_DIM, 0), row, _whole_spec((1, D_MODEL)), _whole_spec(w_pg.shape),
                  _whole_spec(w_ple.shape), _whole_spec((1, D_MODEL))],
        out_specs=[row, row, row, row, _whole_spec((1, D_MODEL)), _whole_spec((1, 128))],
        out_shape=[act, jax.ShapeDtypeStruct((nrows, D_MODEL), F32), act, act, jax.ShapeDtypeStruct((1, D_MODEL), F32),
                   jax.ShapeDtypeStruct((1, 128), F32)],
        compiler_params=pltpu.CompilerParams(dimension_semantics=("arbitrary",)),
    )(x1, p, tgt, ple_g, w_pg, w_ple, fg)


def _shift_rows(cur, edge, j, up):
    tm = cur.shape[0]
    row = lax.broadcasted_iota(jnp.int32, cur.shape, 0)
    if up:
        sh = pltpu.roll(cur, tm - j, 0)
        e = jnp.tile(pltpu.roll(edge, 8 - j, 0), (tm // 8, 1))
        return jnp.where(row >= tm - j, e, sh)
    sh = pltpu.roll(cur, j, 0)
    e = jnp.tile(pltpu.roll(edge, j, 0), (tm // 8, 1))
    return jnp.where(row < j, e, sh)


def _conv_pre(cur, prev, w, b):
    acc = cur * w[CONV_K - 1:CONV_K] + b
    for j in range(1, CONV_K):
        acc = acc + _shift_rows(cur, prev, j, up=False) * w[CONV_K - 1 - j:CONV_K - j]
    return acc


def _halo_specs(tm, nrows, cb, before):
    nb = tm // 8
    last = nrows // 8 - 1
    if before:
        return pl.BlockSpec((8, XBC_W), lambda i: (jnp.maximum(i * nb - 1, 0), cb))
    return pl.BlockSpec((8, XBC_W), lambda i: (jnp.minimum((i + 1) * nb, last), cb))


def _conv_fwd(proj, conv_w, conv_b, *, tm, nrows):
    tm = min(tm, nrows)

    def body(cur_ref, prev_ref, w_ref, b_ref, o_ref, pre_ref):
        prev = jnp.where(pl.program_id(0) == 0, 0.0, prev_ref[...].astype(F32))
        pre = _conv_pre(cur_ref[...].astype(F32), prev, w_ref[...], b_ref[...])
        o_ref[...] = jax.nn.silu(pre).astype(o_ref.dtype)
        pre_ref[...] = pre.astype(pre_ref.dtype)

    out = jax.ShapeDtypeStruct((nrows, XBC_W), ACT_DTYPE)
    return pl.pallas_call(
        body, name="conv_fwd", grid=(nrows // tm,),
        in_specs=[_row_spec(tm, XBC_W, XBC_CB), _halo_specs(tm, nrows, XBC_CB, True),
                  _whole_spec((CONV_K, XBC_W)), _whole_spec((1, XBC_W))],
        out_specs=[_row_spec(tm, XBC_W, 0)] * 2, out_shape=[out, out],
        compiler_params=pltpu.CompilerParams(dimension_semantics=("parallel",)),
    )(proj, proj, conv_w, conv_b)


def _conv_bwd_act(pre, dact, *, tm, nrows):
    tm = min(tm, nrows)
    nb = N_GROUPS * N_STATE

    def body(pre_ref, dxs_ref, dbm_ref, dcm_ref, dpre_ref, db_ref):
        pre = pre_ref[...].astype(F32)
        sg = jax.nn.sigmoid(pre)
        dy = jnp.concatenate([dxs_ref[...], dbm_ref[...], dcm_ref[...]], axis=1).astype(F32)
        dpre = dy * sg * (1.0 + pre * (1.0 - sg))
        dpre_ref[...] = dpre.astype(dpre_ref.dtype)

        @pl.when(pl.program_id(0) == 0)
        def _():
            db_ref[...] = jnp.zeros_like(db_ref)

        db_ref[...] += jnp.sum(dpre, axis=0, keepdims=True)

    return pl.pallas_call(
        body, name="conv_bwd_act", grid=(nrows // tm,),
        in_specs=[_row_spec(tm, XBC_W, 0), _row_spec(tm, D_INNER, 0), _row_spec(tm, nb, 0), _row_spec(tm, nb, 0)],
        out_specs=[_row_spec(tm, XBC_W, 0), _whole_spec((1, XBC_W))],
        out_shape=[jax.ShapeDtypeStruct((nrows, XBC_W), ACT_DTYPE), jax.ShapeDtypeStruct((1, XBC_W), F32)],
        compiler_params=pltpu.CompilerParams(dimension_semantics=("arbitrary",)),
    )(pre, *dact)


def _conv_bwd_x(dpre, proj, conv_w, dproj, *, tm, nrows):
    tm = min(tm, nrows)
    ntiles = nrows // tm

    def body(cur_ref, nxt_ref, x_ref, w_ref, _, o_ref, dw_ref):
        cur = cur_ref[...].astype(F32)
        nxt = jnp.where(pl.program_id(0) == ntiles - 1, 0.0, nxt_ref[...].astype(F32))
        x = x_ref[...].astype(F32)
        w = w_ref[...]

        @pl.when(pl.program_id(0) == 0)
        def _():
            dw_ref[...] = jnp.zeros_like(dw_ref)

        acc = cur * w[CONV_K - 1:CONV_K]
        dw_ref[CONV_K - 1:CONV_K, :] += jnp.sum(cur * x, axis=0, keepdims=True)
        for j in range(1, CONV_K):
            u = _shift_rows(cur, nxt, j, up=True)
            acc = acc + u * w[CONV_K - 1 - j:CONV_K - j]
            dw_ref[CONV_K - 1 - j:CONV_K - j, :] += jnp.sum(u * x, axis=0, keepdims=True)
        o_ref[...] = acc.astype(o_ref.dtype)

    return pl.pallas_call(
        body, name="conv_bwd_x", grid=(ntiles,),
        in_specs=[_row_spec(tm, XBC_W, 0), _halo_specs(tm, nrows, 0, False), _row_spec(tm, XBC_W, XBC_CB),
                  _whole_spec((CONV_K, XBC_W)), ANY],
        out_specs=[_row_spec(tm, XBC_W, XBC_CB), _whole_spec((CONV_K, XBC_W))],
        out_shape=[jax.ShapeDtypeStruct(dproj.shape, dproj.dtype), jax.ShapeDtypeStruct((CONV_K, XBC_W), F32)],
        input_output_aliases={4: 0},
        compiler_params=pltpu.CompilerParams(dimension_semantics=("arbitrary",)),
    )(dpre, dpre, proj, conv_w, dproj)


SSD_SPAN = 4
SSD_FWD_SPANS = 2
_XS_GW = D_INNER // N_GROUPS
_NT = (((1,), (1,)), ((), ()))
_TN = (((0,), (0,)), ((), ()))


def _bf16_terms(x, terms):
    parts, rest = [], x
    for _ in range(terms):
        part = rest.astype(jnp.bfloat16)
        parts.append(part)
        rest = rest - part.astype(F32)
    return parts


def _head_lane_matrix():
    return (lax.broadcasted_iota(jnp.int32, (128, _XS_GW), 0)
            == lax.broadcasted_iota(jnp.int32, (128, _XS_GW), 1) // HEAD_DIM).astype(jnp.bfloat16)


@functools.partial(jax.custom_vjp, nondiff_argnums=(1,))
def _head_lanes(cols, terms):
    e = _head_lane_matrix()
    return sum(jnp.dot(t, e, preferred_element_type=F32) for t in _bf16_terms(cols, terms))


def _head_lanes_fwd(cols, terms):
    return _head_lanes(cols, terms), None


def _head_lanes_bwd(terms, _, g):
    e = _head_lane_matrix()
    return (sum(lax.dot_general(t, e, _NT, preferred_element_type=F32) for t in _bf16_terms(g, 2)),)


_head_lanes.defvjp(_head_lanes_fwd, _head_lanes_bwd)


def _ssd_chunk(k, xs, bm, cm, dtr, hprev, dtb, alog, dsk):
    causal, tri, lo = k
    dt = jax.nn.softplus(dtr + dtb)
    da = dt * (-jnp.exp(alog))
    cs = jnp.dot(tri, da, precision=lax.Precision.HIGHEST, preferred_element_type=F32)
    cst = cs.T
    cs_l = _head_lanes(cs, 3)
    xdt = xs * _head_lanes(dt, 2)
    cb = _dot(cm, bm, _NT)
    yd = []
    for q in range(PAIRS_PER_GROUP):
        xq = xdt[:, 128 * q:128 * (q + 1)]
        y2 = [_dot(cb * jnp.exp(jnp.where(causal, cs[:, h:h + 1] - cst[h:h + 1, :], -jnp.inf)), xq)
              for h in (2 * q, 2 * q + 1)]
        yd.append(jnp.where(lo, y2[0], y2[1]))
    y_off = jnp.exp(cs_l) * _dot(cm, hprev, _NT)
    st = _dot(xdt * jnp.exp(cs_l[CHUNK - 1:CHUNK, :] - cs_l), bm, _TN)
    cdec = jnp.exp(cs[CHUNK - 1:CHUNK, :])
    cd_rows = jnp.concatenate(
        [jnp.broadcast_to(cdec[:, h:h + 1], (HEAD_DIM, N_STATE)) for h in range(HEADS_PER_GROUP)], axis=0)
    dsk_l = _head_lanes(jnp.broadcast_to(dsk, (8, 128)), 2)[:1]
    y = jnp.concatenate(yd, axis=1) + y_off + xs * dsk_l
    return y, cd_rows * hprev + st


def _ssd_span(xs, bm, cm, dtr, h0, dtb, alog, dsk):
    li = lax.broadcasted_iota(jnp.int32, (CHUNK, CHUNK), 0)
    si = lax.broadcasted_iota(jnp.int32, (CHUNK, CHUNK), 1)
    causal = li >= si
    k = (causal, causal.astype(F32), si < HEAD_DIM)
    h, ys = h0, []
    for t in range(xs.shape[0] // CHUNK):
        r = slice(t * CHUNK, (t + 1) * CHUNK)
        y, h = _ssd_chunk(k, xs[r], bm[r], cm[r], dtr[r], h, dtb, alog, dsk)
        ys.append(y)
    return (ys[0] if len(ys) == 1 else jnp.concatenate(ys, axis=0)), h


def _ssd_specs(rev, nsteps, rows):
    def s_of(s):
        return nsteps - 1 - s if rev else s

    xs = pl.BlockSpec((rows, _XS_GW), lambda g, s: (s_of(s), g))
    bm = pl.BlockSpec((rows, N_STATE), lambda g, s: (s_of(s), D_INNER // N_STATE + g))
    cm = pl.BlockSpec((rows, N_STATE), lambda g, s: (s_of(s), D_INNER // N_STATE + N_GROUPS + g))
    dt = pl.BlockSpec((rows, 128), lambda g, s: (s_of(s), g))
    par = pl.BlockSpec((1, 128), lambda g, s: (0, g))
    st = pl.BlockSpec((None, None, _XS_GW, N_STATE), lambda g, s: (g, s_of(s), 0, 0))
    return xs, bm, cm, dt, par, st


def _ssd_fwd(act, dtr, dtb, alog, dsk, *, nrows):
    span = CHUNK * min(SSD_SPAN, nrows // CHUNK)
    per_step = SSD_FWD_SPANS if nrows % (SSD_FWD_SPANS * span) == 0 else 1
    rows = per_step * span
    nsteps = nrows // rows
    xs, bm, cm, dt, par, _ = _ssd_specs(False, nsteps, rows)
    st = pl.BlockSpec((None, per_step, _XS_GW, N_STATE), lambda g, s: (g, s, 0, 0))

    def body(xs_ref, b_ref, c_ref, dt_ref, dtb_ref, al_ref, dk_ref, y_ref, st_ref, h_ref):
        @pl.when(pl.program_id(1) == 0)
        def _():
            h_ref[...] = jnp.zeros_like(h_ref)

        h = h_ref[...]
        for i in range(per_step):
            r = slice(i * span, (i + 1) * span)
            st_ref[i] = h
            y, h = _ssd_span(xs_ref[r, :].astype(F32), b_ref[r, :].astype(F32), c_ref[r, :].astype(F32), dt_ref[r, :],
                             h, dtb_ref[...], al_ref[...], dk_ref[...])
            y_ref[r, :] = y.astype(y_ref.dtype)
        h_ref[...] = h

    return pl.pallas_call(
        body, name="ssd_fwd", grid=(N_GROUPS, nsteps),
        in_specs=[xs, bm, cm, dt, par, par, par], out_specs=[xs, st],
        out_shape=[jax.ShapeDtypeStruct((nrows, D_INNER), ACT_DTYPE),
                   jax.ShapeDtypeStruct((N_GROUPS, nrows // span, _XS_GW, N_STATE), F32)],
        scratch_shapes=[pltpu.VMEM((_XS_GW, N_STATE), F32)],
        compiler_params=pltpu.CompilerParams(dimension_semantics=("arbitrary", "arbitrary")),
    )(act, act, act, dtr, dtb, alog, dsk)


def _ssd_bwd(act, dtr, dtb, alog, dsk, states, dy, *, nrows):
    rows = CHUNK * min(SSD_SPAN, nrows // CHUNK)
    nsteps = nrows // rows
    xs, bm, cm, dt, par, st = _ssd_specs(True, nsteps, rows)

    def body(xs_ref, b_ref, c_ref, dt_ref, dtb_ref, al_ref, dk_ref, st_ref, dy_ref,
             dxs_ref, db_ref, dc_ref, ddt_ref, ddtb_ref, dal_ref, ddk_ref, dh_ref):
        @pl.when(pl.program_id(1) == 0)
        def _():
            dh_ref[...] = jnp.zeros_like(dh_ref)
            ddtb_ref[...] = jnp.zeros_like(ddtb_ref)
            dal_ref[...] = jnp.zeros_like(dal_ref)
            ddk_ref[...] = jnp.zeros_like(ddk_ref)

        _, vjp = jax.vjp(_ssd_span, xs_ref[...].astype(F32), b_ref[...].astype(F32), c_ref[...].astype(F32),
                         dt_ref[...], st_ref[...], dtb_ref[...], al_ref[...], dk_ref[...])
        dxs, db, dc, ddt, dh, ddtb, dal, ddk = vjp((dy_ref[...].astype(F32), dh_ref[...]))
        dxs_ref[...] = dxs.astype(dxs_ref.dtype)
        db_ref[...] = db.astype(db_ref.dtype)
        dc_ref[...] = dc.astype(dc_ref.dtype)
        ddt_ref[...] = ddt
        dh_ref[...] = dh
        ddtb_ref[...] += ddtb
        dal_ref[...] += dal
        ddk_ref[...] += ddk

    nb = N_GROUPS * N_STATE
    bspec = pl.BlockSpec((rows, N_STATE), lambda g, s: (nsteps - 1 - s, g))
    return pl.pallas_call(
        body, name="ssd_bwd", grid=(N_GROUPS, nsteps),
        in_specs=[xs, bm, cm, dt, par, par, par, st, xs],
        out_specs=[xs, bspec, bspec, dt, par, par, par],
        out_shape=[jax.ShapeDtypeStruct((nrows, D_INNER), ACT_DTYPE), jax.ShapeDtypeStruct((nrows, nb), ACT_DTYPE),
                   jax.ShapeDtypeStruct((nrows, nb), ACT_DTYPE), jax.ShapeDtypeStruct((nrows, DT_W), F32),
                   jax.ShapeDtypeStruct((1, DT_W), F32), jax.ShapeDtypeStruct((1, DT_W), F32),
                   jax.ShapeDtypeStruct((1, DT_W), F32)],
        scratch_shapes=[pltpu.VMEM((_XS_GW, N_STATE), F32)],
        compiler_params=pltpu.CompilerParams(dimension_semantics=("arbitrary", "arbitrary")),
    )(act, act, act, dtr, dtb, alog, dsk, states, dy)


def _add_epilogue(acc, r):
    return r + acc


def _rms_and_skip(x, g):
    return _rms(x, g), x


def _forward_backward(x, p, tgt, w, late_weights=None, after=()):
    s = x.shape[0]
    act_t, f32 = ACT_DTYPE, F32
    mm = functools.partial(_matmul)
    h, proj, dtr = _rows_matmul("proj", _f_rms, [(x, D_MODEL, 0)], [w["norm_g"]], w["w_main"], out_dtype=act_t,
                                n=MAIN_W, k=D_MODEL, tm=1024, tn=2 * MM_TILE, nrows=s, after=after, side=(w["w_dt"], f32))
    act, conv_pre = _conv_fwd(proj, w["conv_w"], w["conv_b"], tm=512, nrows=s)
    y, states = _ssd_fwd(act, dtr, w["dt_bias"], w["a_log"], w["d_skip"], nrows=s)
    if late_weights is not None:
        w = {**w, **late_weights(states)}
    a_pars = [w["ln_a_g"], w["ln_a_b"], w["w_s"], w["b_s"]]
    y_a, o_a = _rows_matmul("out_a", _f_branch_a, [(proj, UVZ_W, UVZ_CB)], a_pars, w["w_oa"], out_dtype=act_t,
                            n=D_MODEL, k=E_A, tm=512, nrows=s)
    gn_rows = [(y, D_INNER, 0), (proj, ZB_W, ZB_CB)]
    y_b, o_b = _rows_matmul("out_b", _f_gnorm, gn_rows, [w["ssm_norm_g"]], w["w_ob"], out_dtype=act_t,
                            n=D_MODEL, k=D_INNER, tm=1024, nrows=s)
    mg_rows = [(proj, G_W, G_CB), (o_a, D_MODEL, 0), (o_b, D_MODEL, 0)]
    merged, x1 = _rows_matmul("out_proj", _f_merge, mg_rows, [], w["w_out"], out_dtype=f32, n=D_MODEL, k=D_MODEL,
                              tm=1024, nrows=s, extras=(x,), epilogue=_add_epilogue)
    g = {}
    hp, dx2, dgp, dpe, g["final_g"], loss = _head(x1, p, tgt, w["ple_norm_g"], w["w_pg"], w["w_ple"], w["final_g"],
                                                   tm=512, nrows=s)
    g["w_pg"] = mm(hp, dgp, mode="tn", name="d_w_pg", out_dtype=f32, m=D_MODEL, n=D_MODEL, k=s)
    g["w_ple"] = mm(p, dpe, mode="tn", name="d_w_ple", out_dtype=f32, m=PLE_DIM, n=D_MODEL, k=s)
    dx1, g["ple_norm_g"] = _rows_vjp_call(
        "ple_norm_bwd", _rms_and_skip, [(x1, D_MODEL, 0)], [w["ple_norm_g"]], [(dx2, D_MODEL, 0)],
        [(f32, None)], tm=1024, nrows=s, cot_mm=(dgp, w["w_pg"], None))
    g["w_out"] = mm(merged, dx1, mode="tn", name="d_w_out", out_dtype=f32, m=D_MODEL, n=D_MODEL, k=s)
    dproj, do_a, do_b = _rows_vjp_call(
        "merge_bwd", _f_merge, mg_rows, [], [],
        [(act_t, (None, MAIN_W, G_CB)), (act_t, None), (act_t, None)], tm=1024, nrows=s, cot_mm=(dx1, w["w_out"], None))
    g["w_oa"] = mm(y_a, do_a, mode="tn", name="d_w_oa", out_dtype=f32, m=E_A, n=D_MODEL, k=s)
    g["w_ob"] = mm(y_b, do_b, mode="tn", name="d_w_ob", out_dtype=f32, m=D_INNER, n=D_MODEL, k=s)
    dy, dproj, g["ssm_norm_g"] = _rows_vjp_call(
        "gnorm_bwd", _f_gnorm, gn_rows, [w["ssm_norm_g"]], [],
        [(act_t, None), (act_t, (dproj, MAIN_W, ZB_CB))], tm=512, nrows=s, cot_mm=(do_b, w["w_ob"], None))
    dxs, dbm, dcm, ddtr, g["dt_bias"], g["a_log"], g["d_skip"] = _ssd_bwd(
        act, dtr, w["dt_bias"], w["a_log"], w["d_skip"], states, dy, nrows=s)
    dpre, g["conv_b"] = _conv_bwd_act(conv_pre, (dxs, dbm, dcm), tm=1024, nrows=s)
    dproj, g["conv_w"] = _conv_bwd_x(dpre, proj, w["conv_w"], dproj, tm=512, nrows=s)
    dproj, g["ln_a_g"], g["ln_a_b"], g["w_s"], g["b_s"] = _rows_vjp_call(
        "branch_a_bwd", _f_branch_a, [(proj, UVZ_W, UVZ_CB)], a_pars, [],
        [(act_t, (dproj, MAIN_W, UVZ_CB))], tm=512, nrows=s, cot_mm=(do_a, w["w_oa"], None))
    g["w_main"] = mm(h, dproj, mode="tn", name="d_w_main", out_dtype=f32, m=D_MODEL, n=MAIN_W, k=s)
    g["w_dt"] = mm(h, ddtr, mode="tn", name="d_w_dt", out_dtype=f32, m=D_MODEL, n=DT_W, k=s)
    return loss, g, (dproj, ddtr, dx1)


def _input_grad(x, w, ctx, after=()):
    dproj, ddtr, dx1 = ctx
    s = x.shape[0]
    dh = _matmul(dproj, w["w_main"], mode="nt", name="d_h_main", out_dtype=F32, m=s, n=D_MODEL, k=MAIN_W, after=after)
    return _rows_vjp_call(
        "pre_norm_bwd", _rms_and_skip, [(x, D_MODEL, 0)], [w["norm_g"]], [(dx1, D_MODEL, 0)],
        [(F32, None)], tm=1024, nrows=s, cot_mm=(ddtr, w["w_dt"], dh))


def _local_step(x, p, tgt, w):
    loss, g, ctx = _forward_backward(x, p, tgt, w)
    grad_x, g["norm_g"] = _input_grad(x, w, ctx)
    return loss, grad_x, g


_O_ZB = 3 * E_A
_O_XBC = _O_ZB + D_INNER
_O_DT = _O_XBC + CONV_DIM
_O_G = _O_DT + N_HEADS


def _heads_to_lanes(v):
    r = v.shape[0]
    v = v.reshape(r, N_GROUPS, HEADS_PER_GROUP)
    return jnp.pad(v, ((0, 0), (0, 0), (0, 128 - HEADS_PER_GROUP))).reshape(r, DT_W)


def _lanes_to_heads(v):
    r = v.shape[0]
    return v.reshape(r, N_GROUPS, 128)[:, :, :HEADS_PER_GROUP].reshape(r, N_HEADS)


def _block_cols(blocks, a, b):
    parts = []
    for k in range(N_CHIPS):
        lo, hi = max(a, k * W_IN_BLOCK), min(b, (k + 1) * W_IN_BLOCK)
        if lo < hi:
            parts.append(blocks[k][:, lo - k * W_IN_BLOCK:hi - k * W_IN_BLOCK])
    return parts


_W_IN_SEGMENTS = ((0, _O_ZB, "m", 0), (_O_ZB, _O_XBC, "m", UVZ_W + XBC_W), (_O_XBC, _O_DT, "m", UVZ_W),
                  (_O_DT, _O_G, "d", 0), (_O_G, N_IN, "m", MAIN_W - G_W))


def _w_in_grad_blocks(gm, gdt):
    blocks = []
    for k in range(N_CHIPS):
        a, b = k * W_IN_BLOCK, (k + 1) * W_IN_BLOCK
        parts = []
        for s, e, src, off in _W_IN_SEGMENTS:
            lo, hi = max(a, s), min(b, e)
            if lo < hi:
                parts.append((gm if src == "m" else gdt)[:, off + lo - s:off + hi - s])
        blocks.append(jnp.concatenate(parts, axis=1))
    return jnp.stack(blocks)


def _layout_weights(f, w_in_blocks=None):
    w = dict(f)
    if w_in_blocks is None:
        w_in = w.pop("w_in")
        w_in_blocks = jnp.stack([w_in[:, k * W_IN_BLOCK:(k + 1) * W_IN_BLOCK] for k in range(N_CHIPS)])
    cols = functools.partial(_block_cols, w_in_blocks)
    w["w_main"] = jnp.concatenate(cols(0, _O_ZB) + cols(_O_XBC, _O_DT) + cols(_O_ZB, _O_XBC) + cols(_O_G, N_IN), axis=1)
    w["w_dt"] = _heads_to_lanes(jnp.concatenate(cols(_O_DT, _O_G), axis=1))
    w["b_s"] = f["b_s"].reshape(G_A, CHUNK, 1)
    for n in ("dt_bias", "a_log", "d_skip"):
        w[n] = _heads_to_lanes(f[n])
    return w


def _natural_grads(g):
    out = dict(g)
    gm = out.pop("w_main")
    gdt = _lanes_to_heads(out.pop("w_dt"))
    out["w_in"] = jnp.concatenate(
        [gm[:, :UVZ_W], gm[:, UVZ_W + XBC_W:UVZ_W + XBC_W + ZB_W], gm[:, UVZ_W:UVZ_W + XBC_W], gdt, gm[:, MAIN_W - G_W:]],
        axis=1)
    out["b_s"] = g["b_s"].reshape(G_A, CHUNK)
    for n in ("dt_bias", "a_log", "d_skip"):
        out[n] = _lanes_to_heads(g[n])
    return out


def _place():
    return lax.axis_index("x"), lax.axis_index("y"), lax.axis_index("c")


def _other_chips(x, y):
    return [(1 - x, y), (x, 1 - y), (1 - x, 1 - y)]


def _rcopy(src, dst, ssem, rsem, dev):
    return pltpu.make_async_remote_copy(src_ref=src, dst_ref=dst, send_sem=ssem, recv_sem=rsem,
                                        device_id=dev, device_id_type=MESH)


def _half(ref_rows, half):
    hs = ref_rows // 2
    return pl.ds(pl.multiple_of(half * hs, 16), hs)


def _gather_weights(shards, conv_shard):
    nw = len(shards)

    def body(*refs):
        sh, cv = refs[:nw], refs[nw]
        out, cvo = refs[nw + 1:2 * nw + 1], refs[2 * nw + 1]
        ici_s, ici_r, fw_s, fw_r, own_s, own_r, cv_s, cv_r = refs[2 * nw + 2:]
        x, y, c = _place()
        me, sib, chips = 2 * x + y, (x, y, 1 - c), _other_chips(x, y)
        own = [_rcopy(sh[w], out[w].at[me], own_s.at[w], own_r.at[w], sib) for w in range(nw)]
        own.append(_rcopy(cv, cvo.at[me], own_s.at[nw], own_r.at[nw], sib))
        for cp in own:
            cp.start()
        sends = []
        for w in range(nw):
            mine = _half(sh[w].shape[0], c)
            for j, chip in enumerate(chips):
                sends.append(_rcopy(sh[w].at[mine], out[w].at[me, mine], ici_s.at[3 * w + j], ici_r.at[3 * w + j], (*chip, c)))
        for j, chip in enumerate(chips):
            sends.append(_rcopy(cv, cvo.at[me], cv_s.at[j], cv_r.at[j], (*chip, c)))
        for cp in sends:
            cp.start()
        for w in range(nw):
            mine = _half(sh[w].shape[0], c)
            for j, chip in enumerate(chips):
                slab = out[w].at[2 * chip[0] + chip[1], mine]
                _rcopy(slab, slab, ici_s.at[3 * w + j], ici_r.at[3 * w + j], (*chip, c)).wait_recv()
                fwd = _rcopy(slab, slab, fw_s.at[3 * w + j], fw_r.at[3 * w + j], sib)
                fwd.start()
                sends.append(fwd)
        for j, chip in enumerate(chips):
            blk = cvo.at[2 * chip[0] + chip[1]]
            _rcopy(blk, blk, cv_s.at[j], cv_r.at[j], (*chip, c)).wait_recv()
        for w in range(nw):
            theirs = _half(sh[w].shape[0], 1 - c)
            for j, chip in enumerate(chips):
                slab = out[w].at[2 * chip[0] + chip[1], theirs]
                _rcopy(slab, slab, fw_s.at[3 * w + j], fw_r.at[3 * w + j], sib).wait_recv()
        for cp in sends:
            cp.wait_send()
        for cp in own:
            cp.wait()

    dma = pltpu.SemaphoreType.DMA
    return pl.pallas_call(
        body, name="gather_weights",
        in_specs=[ANY] * (nw + 1), out_specs=[ANY] * (nw + 1),
        out_shape=[jax.ShapeDtypeStruct((N_CHIPS,) + s.shape, s.dtype) for s in shards]
        + [jax.ShapeDtypeStruct((N_CHIPS,) + conv_shard.shape, conv_shard.dtype)],
        scratch_shapes=[dma((3 * nw,)), dma((3 * nw,)), dma((3 * nw,)), dma((3 * nw,)), dma((nw + 1,)), dma((nw + 1,)),
                        dma((3,)), dma((3,))],
    )(*shards, conv_shard)


_HBM = pl.BlockSpec(memory_space=pltpu.HBM)
_SEM = pl.BlockSpec(memory_space=pltpu.SEMAPHORE)
_EFFECT = pltpu.SideEffectType.DATAFLOW_SIDE_EFFECTING


def _late_gather_copies(sh, out, s_sem, r_sem):
    x, y, c = _place()
    to = [(*chip, c) for chip in _other_chips(x, y)] + [(x, y, 1 - c)]
    return [_rcopy(sh[w], out[w].at[2 * x + y], s_sem.at[4 * w + j], r_sem.at[4 * w + j], dev)
            for w in range(len(sh)) for j, dev in enumerate(to)]


def _late_gather_start(shards):
    n = len(shards)
    lands = [lax.empty((N_CHIPS,) + a.shape, a.dtype) for a in shards]

    def body(*refs):
        for cp in _late_gather_copies(refs[:n], refs[n:2 * n], refs[2 * n], refs[2 * n + 1]):
            cp.start()
        refs[-1][...] = jnp.zeros_like(refs[-1])

    dma = pltpu.SemaphoreType.DMA
    hbm = [pltpu.with_memory_space_constraint(a, pltpu.HBM) for a in list(shards) + lands]
    out = pl.pallas_call(
        body, name="late_gather_start",
        out_shape=[dma((4 * n,)), dma((4 * n,))] + [pltpu.HBM(a.shape, a.dtype) for a in hbm]
        + [jax.ShapeDtypeStruct((8, 128), F32)],
        in_specs=[_HBM] * (2 * n), out_specs=[_SEM, _SEM] + [_HBM] * (2 * n) + [pl.BlockSpec(memory_space=pltpu.VMEM)],
        input_output_aliases={i: 2 + i for i in range(2 * n)},
        compiler_params=pltpu.CompilerParams(has_side_effects=_EFFECT),
    )(*hbm)
    return out[0], out[1], out[2:2 + n], out[2 + n:2 + 2 * n], out[-1]


def _late_gather_wait(s_sem, r_sem, srcs, lands, after):
    n = len(srcs)

    def body(*refs):
        for cp in _late_gather_copies(refs[:n], refs[n:2 * n], refs[2 * n], refs[2 * n + 1]):
            cp.wait_send()
            cp.wait_recv()

    out = pl.pallas_call(
        body, name="late_gather_wait",
        out_shape=[pltpu.HBM(a.shape, a.dtype) for a in list(srcs) + list(lands)],
        in_specs=[_HBM] * (2 * n) + [_SEM, _SEM, ANY], out_specs=[_HBM] * (2 * n),
        input_output_aliases={i: i for i in range(2 * n)},
        compiler_params=pltpu.CompilerParams(has_side_effects=_EFFECT),
    )(*srcs, *lands, s_sem, r_sem, after)
    return out[n:]


def _swap_with_sibling(arrs):
    n = len(arrs)

    def body(*refs):
        src, dst, s_sem, r_sem = refs[:n], refs[n:2 * n], refs[2 * n], refs[2 * n + 1]
        x, y, c = _place()
        cps = [_rcopy(src[i], dst[i], s_sem.at[i], r_sem.at[i], (x, y, 1 - c)) for i in range(n)]
        for cp in cps:
            cp.start()
        for cp in cps:
            cp.wait()

    dma = pltpu.SemaphoreType.DMA
    return pl.pallas_call(
        body, name="swap_with_sibling", in_specs=[ANY] * n, out_specs=[ANY] * n,
        out_shape=[jax.ShapeDtypeStruct(a.shape, a.dtype) for a in arrs], scratch_shapes=[dma((n,)), dma((n,))],
    )(*arrs)


def _scatter_copies(src, land, s_sem, r_sem):
    x, y, c = _place()
    return [_rcopy(src[i].at[2 * chip[0] + chip[1]], land[i].at[j], s_sem.at[3 * i + j], r_sem.at[3 * i + j], (*chip, c))
            for i in range(len(src)) for j, chip in enumerate(_other_chips(x, y))]


def _scatter_blocks_start(arrs):
    n = len(arrs)
    lands = [lax.empty((3,) + a.shape[1:], a.dtype) for a in arrs]

    def body(*refs):
        src, land, s_sem, r_sem, token = refs[:n], refs[n:2 * n], refs[2 * n], refs[2 * n + 1], refs[-1]
        for cp in _scatter_copies(src, land, s_sem, r_sem):
            cp.start()
        token[...] = jnp.zeros_like(token)

    dma = pltpu.SemaphoreType.DMA
    hbm = [pltpu.with_memory_space_constraint(a, pltpu.HBM) for a in list(arrs) + lands]
    out = pl.pallas_call(
        body, name="scatter_blocks_start",
        out_shape=[dma((3 * n,)), dma((3 * n,))] + [pltpu.HBM(a.shape, a.dtype) for a in hbm]
        + [jax.ShapeDtypeStruct((8, 128), F32)],
        in_specs=[_HBM] * (2 * n), out_specs=[_SEM, _SEM] + [_HBM] * (2 * n) + [pl.BlockSpec(memory_space=pltpu.VMEM)],
        input_output_aliases={i: 2 + i for i in range(2 * n)},
        compiler_params=pltpu.CompilerParams(has_side_effects=_EFFECT),
    )(*hbm)
    return out[0], out[1], out[2:2 + n], out[2 + n:2 + 2 * n], out[-1]


def _scatter_blocks_wait(s_sem, r_sem, srcs, lands, after):
    n = len(srcs)

    def body(*refs):
        src, land, s_sem, r_sem = refs[:n], refs[n:2 * n], refs[2 * n], refs[2 * n + 1]
        for cp in _scatter_copies(src, land, s_sem, r_sem):
            cp.wait_send()
            cp.wait_recv()

    out = pl.pallas_call(
        body, name="scatter_blocks_wait",
        out_shape=[pltpu.HBM(a.shape, a.dtype) for a in list(srcs) + list(lands)],
        in_specs=[_HBM] * (2 * n) + [_SEM, _SEM] + [ANY] * len(after), out_specs=[_HBM] * (2 * n),
        input_output_aliases={i: i for i in range(2 * n)},
        compiler_params=pltpu.CompilerParams(has_side_effects=_EFFECT),
    )(*srcs, *lands, s_sem, r_sem, *after)
    return out[:n], out[n:]


def _share_halves(arrs):
    n = len(arrs)

    def body(*refs):
        buf, s_sem, r_sem = refs[n:2 * n], refs[2 * n], refs[2 * n + 1]
        x, y, c = _place()
        cps = []
        for i in range(n):
            mine = buf[i].at[_half(buf[i].shape[0], c)]
            cps.append(_rcopy(mine, mine, s_sem.at[i], r_sem.at[i], (x, y, 1 - c)))
        for cp in cps:
            cp.start()
        for i in range(n):
            theirs = buf[i].at[_half(buf[i].shape[0], 1 - c)]
            _rcopy(theirs, theirs, s_sem.at[i], r_sem.at[i], (x, y, 1 - c)).wait_recv()
        for cp in cps:
            cp.wait_send()

    dma = pltpu.SemaphoreType.DMA
    return pl.pallas_call(
        body, name="share_halves", in_specs=[ANY] * n, out_specs=[ANY] * n,
        out_shape=[jax.ShapeDtypeStruct(a.shape, a.dtype) for a in arrs],
        input_output_aliases={i: i for i in range(n)}, scratch_shapes=[dma((n,)), dma((n,))],
    )(*arrs)


def _small_gather_copies(src, land, s_sem, r_sem):
    x, y, c = _place()
    cps = []
    for d in range(1, N_DEV):
        peer = ((1 - x) if d & 4 else x), ((1 - y) if d & 2 else y), ((1 - c) if d & 1 else c)
        cps.append(_rcopy(src, land.at[4 * x + 2 * y + c], s_sem.at[d - 1], r_sem.at[d - 1], peer))
    return cps


def _small_gather_start(packed):
    def body(src, land, s_sem, r_sem, _, __, token):
        for cp in _small_gather_copies(src, land, s_sem, r_sem):
            cp.start()
        token[...] = jnp.zeros_like(token)

    dma = pltpu.SemaphoreType.DMA
    hbm = [pltpu.with_memory_space_constraint(a, pltpu.HBM) for a in (packed, lax.empty((N_DEV,) + packed.shape, F32))]
    return pl.pallas_call(
        body, name="small_gather_start",
        out_shape=[dma((N_DEV - 1,)), dma((N_DEV - 1,))] + [pltpu.HBM(a.shape, a.dtype) for a in hbm]
        + [jax.ShapeDtypeStruct((8, 128), F32)],
        in_specs=[_HBM] * 2, out_specs=[_SEM, _SEM, _HBM, _HBM, pl.BlockSpec(memory_space=pltpu.VMEM)],
        input_output_aliases={0: 2, 1: 3}, compiler_params=pltpu.CompilerParams(has_side_effects=_EFFECT),
    )(*hbm)


def _small_gather_wait(s_sem, r_sem, src, land, after):
    def body(src, land, s_sem, r_sem, *_):
        for cp in _small_gather_copies(src, land, s_sem, r_sem):
            cp.wait_send()
            cp.wait_recv()

    return pl.pallas_call(
        body, name="small_gather_wait", out_shape=[pltpu.HBM(src.shape, src.dtype), pltpu.HBM(land.shape, land.dtype)],
        in_specs=[_HBM, _HBM, _SEM, _SEM, ANY], out_specs=[_HBM, _HBM], input_output_aliases={0: 0, 1: 1},
        compiler_params=pltpu.CompilerParams(has_side_effects=_EFFECT),
    )(src, land, s_sem, r_sem, after)


def _small_sum(own, land, dev_arr):
    def body(me_ref, own_ref, land_ref, o_ref):
        acc = jnp.zeros(o_ref.shape, F32)
        for d in range(N_DEV):
            acc = acc + jnp.where(me_ref[0] == d, own_ref[...], land_ref[d])
        o_ref[...] = acc

    return pl.pallas_call(
        body, name="small_sum", out_shape=jax.ShapeDtypeStruct(own.shape, F32),
        grid_spec=pltpu.PrefetchScalarGridSpec(
            num_scalar_prefetch=1, grid=(1,),
            in_specs=[pl.BlockSpec(own.shape, lambda i, m: (0, 0)), pl.BlockSpec(land.shape, lambda i, m: (0, 0, 0))],
            out_specs=pl.BlockSpec(own.shape, lambda i, m: (0, 0))),
    )(dev_arr, own, land)


def _row_tile(rows, cols):
    tr = max(8, min(rows, (1 << 20) // (4 * cols) // 8 * 8))
    while rows % tr:
        tr -= 8
    return tr


def _chip_sum(name, g5, recv, c_arr):
    nb, _, hs, cols = g5.shape
    tr = _row_tile(hs, cols)

    def body(_, a_ref, b_ref, o_ref):
        o_ref[...] = (a_ref[...] + b_ref[...].astype(F32)).astype(o_ref.dtype)

    blk = pl.BlockSpec((None, tr, cols), lambda b, i, c: (b, i, 0))
    return pl.pallas_call(
        body, name=name,
        grid_spec=pltpu.PrefetchScalarGridSpec(
            num_scalar_prefetch=1, grid=(nb, hs // tr),
            in_specs=[pl.BlockSpec((None, None, tr, cols), lambda b, i, c: (b, c[0], i, 0)), blk], out_specs=blk),
        out_shape=jax.ShapeDtypeStruct((nb, hs, cols), WIRE_DTYPE),
    )(c_arr, g5, recv)


def _final_sum(name, own, recv, place_arr):
    _, hs, cols = own.shape
    tr = _row_tile(hs, cols)
    nt = hs // tr

    def body(_, a_ref, r_ref, o_ref):
        o_ref[...] = ((a_ref[...].astype(F32) + r_ref[0].astype(F32)) + r_ref[1].astype(F32)) + r_ref[2].astype(F32)

    return pl.pallas_call(
        body, name=name,
        grid_spec=pltpu.PrefetchScalarGridSpec(
            num_scalar_prefetch=1, grid=(nt,),
            in_specs=[pl.BlockSpec((None, tr, cols), lambda i, m: (m[0], i, 0)),
                      pl.BlockSpec((3, tr, cols), lambda i, m: (0, i, 0))],
            out_specs=pl.BlockSpec((tr, cols), lambda i, m: (m[1] * nt + i, 0))),
        out_shape=jax.ShapeDtypeStruct((2 * hs, cols), F32),
    )(place_arr, own, recv)


def _adamw(w, g, m, v):
    m = ADAM_B1 * m + (1.0 - ADAM_B1) * g
    v = ADAM_B2 * v + (1.0 - ADAM_B2) * (g * g)
    m_hat = m / (1.0 - ADAM_B1 ** ADAM_STEP)
    v_hat = v / (1.0 - ADAM_B2 ** ADAM_STEP)
    return -ADAM_LR * (m_hat / (jnp.sqrt(v_hat) + ADAM_EPS) + ADAM_WD * w), m, v


def _adamw_call(name, w, g, m, v):
    rows, cols = w.shape
    tr = _row_tile(rows, cols)
    if 4 * tr * cols >= (1 << 18):
        blk, steps = pl.BlockSpec((tr, cols), lambda i: (i, 0)), rows // tr
    else:
        blk, steps = pl.BlockSpec((rows, 128), lambda i: (0, i)), cols // 128

    def body(w_ref, g_ref, m_ref, v_ref, d_ref, nm_ref, nv_ref, go_ref):
        g = g_ref[...]
        d_ref[...], nm_ref[...], nv_ref[...] = _adamw(w_ref[...], g, m_ref[...], v_ref[...])
        go_ref[...] = g

    return pl.pallas_call(
        body, name=name, grid=(steps,), in_specs=[blk] * 4, out_specs=[blk] * 4,
        out_shape=[jax.ShapeDtypeStruct(w.shape, F32)] * 4,
        compiler_params=pltpu.CompilerParams(dimension_semantics=("parallel",)),
    )(w, g, m, v)


def _adamw_small(ws, gs, ms, vs):
    n = len(ws)

    def body(*refs):
        for i in range(n):
            w_ref, g_ref, m_ref, v_ref = (refs[k * n + i] for k in range(4))
            d, nm, nv = _adamw(w_ref[...], g_ref[...], m_ref[...], v_ref[...])
            refs[4 * n + i][...] = d
            refs[5 * n + i][...] = nm
            refs[6 * n + i][...] = nv

    out = pl.pallas_call(
        body, name="adamw_small", out_shape=[jax.ShapeDtypeStruct(a.shape, F32) for a in ws] * 3,
    )(*ws, *gs, *ms, *vs)
    return out[:n], out[n:2 * n], out[2 * n:]


_BIG = ("w_in", "w_oa", "w_ob", "w_out", "w_pg", "w_ple")
_SMALL = ("norm_g", "ln_a_g", "ln_a_b", "w_s", "b_s", "conv_w", "conv_b", "dt_bias", "a_log", "d_skip", "ssm_norm_g",
          "ple_norm_g", "final_g")
_WEIGHTS = ("norm_g", "w_in", "ln_a_g", "ln_a_b", "w_s", "b_s", "conv_w", "conv_b", "dt_bias", "a_log", "d_skip",
            "ssm_norm_g", "w_oa", "w_ob", "w_out", "ple_norm_g", "w_pg", "w_ple", "final_g")
_COL_SHARDED = ("w_in", "w_ple")
_PACK = 1024


def _blocks_to_full(col_sharded, blocks):
    if col_sharded:
        return jnp.concatenate([blocks[k] for k in range(N_CHIPS)], axis=1)
    return blocks.reshape(N_CHIPS * blocks.shape[1], blocks.shape[2])


def _full_to_blocks(col_sharded, full):
    if col_sharded:
        w = full.shape[1] // N_CHIPS
        return jnp.stack([full[:, k * w:(k + 1) * w] for k in range(N_CHIPS)])
    return full.reshape(N_CHIPS, full.shape[0] // N_CHIPS, full.shape[1])


def _two_d(n, a):
    if n == "w_s":
        return a.reshape(G_A * CHUNK, CHUNK)
    if n in ("b_s", "conv_w"):
        return a.reshape(a.shape[-2], a.shape[-1])
    return a.reshape(1, a.shape[-1])


def kernel(x, p, norm_g, w_in, ln_a_g, ln_a_b, w_s, b_s, conv_w, conv_b, dt_bias, a_log, d_skip, ssm_norm_g, w_oa, w_ob, w_out, ple_norm_g, w_pg, w_ple, final_g, loss_target, m_norm_g, m_w_in, m_ln_a_g, m_ln_a_b, m_w_s, m_b_s, m_conv_w, m_conv_b, m_dt_bias, m_a_log, m_d_skip, m_ssm_norm_g, m_w_oa, m_w_ob, m_w_out, m_ple_norm_g, m_w_pg, m_w_ple, m_final_g, v_norm_g, v_w_in, v_ln_a_g, v_ln_a_b, v_w_s, v_b_s, v_conv_w, v_conv_b, v_dt_bias, v_a_log, v_d_skip, v_ssm_norm_g, v_w_oa, v_w_ob, v_w_out, v_ple_norm_g, v_w_pg, v_w_ple, v_final_g):
    wt = dict(norm_g=norm_g, w_in=w_in, ln_a_g=ln_a_g, ln_a_b=ln_a_b, w_s=w_s, b_s=b_s, conv_w=conv_w, conv_b=conv_b,
              dt_bias=dt_bias, a_log=a_log, d_skip=d_skip, ssm_norm_g=ssm_norm_g, w_oa=w_oa, w_ob=w_ob, w_out=w_out,
              ple_norm_g=ple_norm_g, w_pg=w_pg, w_ple=w_ple, final_g=final_g)
    mom = dict(norm_g=m_norm_g, w_in=m_w_in, ln_a_g=m_ln_a_g, ln_a_b=m_ln_a_b, w_s=m_w_s, b_s=m_b_s, conv_w=m_conv_w,
               conv_b=m_conv_b, dt_bias=m_dt_bias, a_log=m_a_log, d_skip=m_d_skip, ssm_norm_g=m_ssm_norm_g, w_oa=m_w_oa,
               w_ob=m_w_ob, w_out=m_w_out, ple_norm_g=m_ple_norm_g, w_pg=m_w_pg, w_ple=m_w_ple, final_g=m_final_g)
    vel = dict(norm_g=v_norm_g, w_in=v_w_in, ln_a_g=v_ln_a_g, ln_a_b=v_ln_a_b, w_s=v_w_s, b_s=v_b_s, conv_w=v_conv_w,
               conv_b=v_conv_b, dt_bias=v_dt_bias, a_log=v_a_log, d_skip=v_d_skip, ssm_norm_g=v_ssm_norm_g, w_oa=v_w_oa,
               w_ob=v_w_ob, w_out=v_w_out, ple_norm_g=v_ple_norm_g, w_pg=v_w_pg, w_ple=v_w_ple, final_g=v_final_g)
    xi, yi, ci = _place()
    me = 2 * xi + yi
    c_arr = jnp.reshape(ci, (1,)).astype(jnp.int32)
    place_arr = jnp.stack([me, ci]).astype(jnp.int32)

    shard = {n: wt[n][0] for n in _BIG}
    wire = {n: shard[n].astype(WIRE_DTYPE) for n in _BIG}
    w_in_blocks, conv_blocks = _gather_weights([wire["w_in"]], conv_w[0])
    g_ssem, g_rsem, g_sent, g_lands, g_token = _late_gather_start([wire[n] for n in _BIG[1:]])
    full = {"conv_w": _blocks_to_full(True, conv_blocks)}
    for n in _SMALL:
        if n != "conv_w":
            full[n] = wt[n][0] if wt[n].ndim > 2 else wt[n].reshape(1, wt[n].shape[-1])

    def late_weights(after):
        blocks = _late_gather_wait(g_ssem, g_rsem, g_sent, g_lands, after)
        return {n: _blocks_to_full(n in _COL_SHARDED, b) for n, b in zip(_BIG[1:], blocks)}

    w = _layout_weights(full, w_in_blocks=w_in_blocks)
    loss_row, g, ctx = _forward_backward(x[0], p[0, 0], loss_target[0], w, late_weights, after=(g_token,))
    loss = lax.psum(loss_row[0, 0], ("x", "y", "c"))

    parts = {n: _full_to_blocks(n in _COL_SHARDED, g[n]) for n in _BIG[1:]}
    parts["w_main"] = g["w_main"][None]
    parts["w_dt"] = jnp.pad(_lanes_to_heads(g["w_dt"]), ((0, 0), (0, 128 - N_HEADS)))[None]
    names = ("w_main", "w_dt") + _BIG[1:]
    g5 = {n: parts[n].reshape(parts[n].shape[0], 2, parts[n].shape[1] // 2, parts[n].shape[2]) for n in names}
    to_sibling = [lax.dynamic_index_in_dim(g5[n], 1 - ci, axis=1, keepdims=False).astype(WIRE_DTYPE) for n in names]
    from_sibling = _swap_with_sibling(to_sibling)
    chip = {n: _chip_sum("chip_sum_" + n, g5[n], r, c_arr) for n, r in zip(names, from_sibling)}
    chip["w_in"] = _w_in_grad_blocks(chip["w_main"][0], chip["w_dt"][0])
    chip_wire = [chip[n] for n in _BIG]
    s_sem, r_sem, sent, lands, token = _scatter_blocks_start(chip_wire)
    grad_x, g["norm_g"] = _input_grad(x[0], w, ctx, after=(token,))
    g = _natural_grads(g)

    pieces = [_two_d(n, g[n]).reshape(-1) for n in _SMALL]
    sizes = [v.shape[0] for v in pieces]
    padded = [-(-s // _PACK) * _PACK for s in sizes]
    packed = jnp.concatenate([jnp.pad(v, (0, ps - s)) for v, s, ps in zip(pieces, sizes, padded)]).reshape(-1, 128)
    a_ssem, a_rsem, a_src, a_land, a_token = _small_gather_start(packed)

    sent, from_chips = _scatter_blocks_wait(s_sem, r_sem, sent, lands, (grad_x, a_token))
    halves = [_final_sum("final_sum_" + n, a, r, place_arr) for n, a, r in zip(_BIG, sent, from_chips)]
    grads = dict(zip(_BIG, _share_halves(halves)))
    delta, new_m, new_v = {}, {}, {}
    for n in _BIG:
        t = jnp.transpose if n == "w_in" else (lambda a: a)
        res = _adamw_call("adamw_" + n, t(shard[n]), t(grads[n]), t(mom[n][0]), t(vel[n][0]))
        delta[n], new_m[n], new_v[n], grads[n] = (t(r) for r in res)

    a_src, a_land = _small_gather_wait(a_ssem, a_rsem, a_src, a_land, delta["w_in"])
    summed = _small_sum(a_src, a_land, jnp.reshape(4 * xi + 2 * yi + ci, (1,)).astype(jnp.int32)).reshape(-1)
    off = 0
    for n, s, ps in zip(_SMALL, sizes, padded):
        grads[n] = summed[off:off + s].reshape(_two_d(n, g[n]).shape)
        off += ps
    grads["conv_w"] = lax.dynamic_slice_in_dim(grads["conv_w"], me * (CONV_DIM // N_CHIPS), CONV_DIM // N_CHIPS, axis=1)
    small = _adamw_small([_two_d(n, wt[n]) for n in _SMALL], [grads[n] for n in _SMALL],
                         [_two_d(n, mom[n]) for n in _SMALL], [_two_d(n, vel[n]) for n in _SMALL])
    for i, n in enumerate(_SMALL):
        delta[n], new_m[n], new_v[n] = small[0][i], small[1][i], small[2][i]

    def shaped(d):
        return [d[n].reshape(wt[n].shape) for n in _WEIGHTS]

    return (loss, grad_x[None], *shaped(grads), *shaped(delta), *shaped(new_m), *shaped(new_v))
```

```python
import functools

import jax
import jax.numpy as jnp
from jax import lax
from jax.experimental import pallas as pl
from jax.experimental.pallas import tpu as pltpu

F32 = jnp.float32
MXU_DTYPE = jnp.bfloat16
ACT_DTYPE = jnp.bfloat16
WIRE_DTYPE = jnp.bfloat16

D_MODEL = 1024
PLE_DIM = 256
CHUNK = 128
EPS = 1e-6
E_A = D_MODEL
G_A = 4
D_INNER = 2 * D_MODEL
HEAD_DIM = 64
N_HEADS = D_INNER // HEAD_DIM
N_STATE = 128
N_GROUPS = 4
HEADS_PER_GROUP = N_HEADS // N_GROUPS
PAIRS_PER_GROUP = HEADS_PER_GROUP // 2
CONV_K = 4
CONV_DIM = D_INNER + 2 * N_GROUPS * N_STATE
N_IN = 3 * E_A + D_INNER + CONV_DIM + N_HEADS + 2 * D_MODEL
N_CHIPS = 4
N_DEV = 8
W_IN_BLOCK = N_IN // N_CHIPS

UVZ_W, XBC_W, ZB_W, G_W = 3 * E_A, CONV_DIM, D_INNER, 2 * D_MODEL
MAIN_W = UVZ_W + XBC_W + ZB_W + G_W
UVZ_CB, XBC_CB, ZB_CB, G_CB = 0, 1, 3, 4
DT_W = N_GROUPS * 128

ADAM_LR, ADAM_B1, ADAM_B2, ADAM_EPS, ADAM_WD, ADAM_STEP = 0.001, 0.9, 0.999, 1e-08, 0.01, 10

MESH = pl.DeviceIdType.MESH
ANY = pl.BlockSpec(memory_space=pl.ANY)


def _mxu(v):
    return v.astype(MXU_DTYPE)


def _dot(a, b, dims=(((1,), (0,)), ((), ()))):
    return lax.dot_general(_mxu(a), _mxu(b), dims, preferred_element_type=F32)


V7X_MXU_WIDTH = 256
V7X_SCOPED_VMEM_BYTES = 60000 * 1024
MM_TILE = 4 * V7X_MXU_WIDTH
MM_VMEM_BUDGET = 4 * V7X_SCOPED_VMEM_BYTES // 5


def _mm_tk(m, n, k, tm, tn, a_bytes, b_bytes, out_bytes, extra_bytes):
    one_tile = m == tm and n == tn
    for parts in range(2 if one_tile else 1, k // 128 + 1):
        if k % parts or (k // parts) % 128 and parts > 1:
            continue
        tk = k // parts
        a_bufs = 1 if (parts == 1 and m == tm) else 2
        b_bufs = 1 if (parts == 1 and n == tn) else 2
        need = (tk * (a_bufs * tm * a_bytes + b_bufs * tn * b_bytes) + 2 * tm * tn * (out_bytes + extra_bytes)
                + (tm * tn * 4 if parts > 1 else 0))
        if need <= MM_VMEM_BUDGET:
            return tk
    return 128


def _matmul(a, b, *, mode, name, out_dtype, m, n, k, tm=MM_TILE, tn=MM_TILE, tk=None, a_off=0, b_off=0,
            extras=(), epilogue=None, after=()):
    tm, tn = min(tm, m), min(tn, n)
    if tk is None:
        tk = _mm_tk(m, n, k, tm, tn, a.dtype.itemsize, b.dtype.itemsize, jnp.dtype(out_dtype).itemsize,
                    sum(e.dtype.itemsize for e in extras))
    tk = min(tk, k)
    assert m % tm == 0 and n % tn == 0 and k % tk == 0, (name, m, n, k, tm, tn, tk)
    nk = k // tk
    a_mode = pl.Buffered(1) if (nk == 1 and m == tm) else None
    b_mode = pl.Buffered(1) if (nk == 1 and n == tn) else None
    if mode == "nn":
        assert a_off % tk == 0 and b_off % tn == 0
        a_spec = pl.BlockSpec((tm, tk), lambda i, j, kk: (i, kk + a_off // tk), pipeline_mode=a_mode)
        b_spec = pl.BlockSpec((tk, tn), lambda i, j, kk: (kk, j + b_off // tn), pipeline_mode=b_mode)
        dims = (((1,), (0,)), ((), ()))
    elif mode == "nt":
        a_spec = pl.BlockSpec((tm, tk), lambda i, j, kk: (i, kk), pipeline_mode=a_mode)
        b_spec = pl.BlockSpec((tn, tk), lambda i, j, kk: (j, kk), pipeline_mode=b_mode)
        dims = (((1,), (1,)), ((), ()))
    else:
        assert a_off % tm == 0 and b_off % tn == 0
        a_spec = pl.BlockSpec((tk, tm), lambda i, j, kk: (kk, i + a_off // tm), pipeline_mode=a_mode)
        b_spec = pl.BlockSpec((tk, tn), lambda i, j, kk: (kk, j + b_off // tn), pipeline_mode=b_mode)
        dims = (((0,), (0,)), ((), ()))
    ne = len(extras)

    def finish(acc, extra_refs, o_ref):
        res = acc if epilogue is None else epilogue(acc, *[e[...] for e in extra_refs])
        o_ref[...] = res.astype(o_ref.dtype)

    def body(a_ref, b_ref, *rest):
        extra_refs, o_ref = rest[:ne], rest[ne + len(after)]
        part = _dot(a_ref[...], b_ref[...], dims)
        if nk == 1:
            finish(part, extra_refs, o_ref)
            return
        acc_ref = rest[ne + len(after) + 1]
        kk = pl.program_id(2)

        @pl.when(kk == 0)
        def _():
            acc_ref[...] = part

        @pl.when(kk > 0)
        def _():
            acc_ref[...] += part

        @pl.when(kk == nk - 1)
        def _():
            finish(acc_ref[...], extra_refs, o_ref)

    o_spec = pl.BlockSpec((tm, tn), lambda i, j, kk: (i, j))
    return pl.pallas_call(
        body, name=name, grid=(m // tm, n // tn, nk),
        in_specs=[a_spec, b_spec] + [o_spec] * ne + [ANY] * len(after), out_specs=o_spec,
        out_shape=jax.ShapeDtypeStruct((m, n), out_dtype),
        scratch_shapes=[pltpu.VMEM((tm, tn), F32)] if nk > 1 else [],
        compiler_params=pltpu.CompilerParams(dimension_semantics=("parallel", "parallel", "arbitrary")),
    )(a, b, *extras, *after)


def _rows_matmul(name, f, rows, pars, b, *, out_dtype, n, k, tm, nrows, tn=MM_TILE, extras=(), epilogue=None, after=(),
                 side=None):
    tm, tn = min(tm, nrows), min(tn, n)
    assert nrows % tm == 0 and n % tn == 0, (name, nrows, n, tm, tn)
    nr, npar, ne, nj = len(rows), len(pars), len(extras), n // tn
    ns = 0 if side is None else 1
    n_in = nr + npar + 1 + ne + ns + len(after)

    def body(*refs):
        row_refs, par_refs, b_ref = refs[:nr], refs[nr:nr + npar], refs[nr + npar]
        extra_refs = refs[nr + npar + 1:nr + npar + 1 + ne]
        a_ref, o_ref = refs[n_in], refs[n_in + 1]

        def make_a():
            a = f(*[r[...].astype(F32) for r in row_refs], *[p[...] for p in par_refs])[0]
            a_ref[...] = a.astype(a_ref.dtype)
            if ns:
                refs[n_in + 2][...] = _dot(a_ref[...], refs[nr + npar + 1 + ne][...]).astype(refs[n_in + 2].dtype)

        if nj == 1:
            make_a()
        else:
            pl.when(pl.program_id(1) == 0)(make_a)
        res = _dot(a_ref[...], b_ref[...])
        if epilogue is not None:
            res = epilogue(res, *[e[...] for e in extra_refs])
        o_ref[...] = res.astype(o_ref.dtype)

    o_spec = pl.BlockSpec((tm, tn), lambda i, j: (i, j))
    side_in = [] if side is None else [pl.BlockSpec(tuple(side[0].shape), lambda i, j: (0, 0))]
    side_out = [] if side is None else [pl.BlockSpec((tm, side[0].shape[1]), lambda i, j: (i, 0))]
    side_shape = [] if side is None else [jax.ShapeDtypeStruct((nrows, side[0].shape[1]), side[1])]
    return pl.pallas_call(
        body, name=name, grid=(nrows // tm, nj),
        in_specs=[pl.BlockSpec((tm, w), lambda i, j, cb=cb: (i, cb)) for _, w, cb in rows]
        + [pl.BlockSpec(tuple(p.shape), lambda i, j, nd=p.ndim: (0,) * nd) for p in pars]
        + [pl.BlockSpec((k, tn), lambda i, j: (0, j))] + [o_spec] * ne + side_in + [ANY] * len(after),
        out_specs=[pl.BlockSpec((tm, k), lambda i, j: (i, 0)), o_spec] + side_out,
        out_shape=[jax.ShapeDtypeStruct((nrows, k), ACT_DTYPE), jax.ShapeDtypeStruct((nrows, n), out_dtype)] + side_shape,
        compiler_params=pltpu.CompilerParams(dimension_semantics=("parallel", "arbitrary")),
    )(*[r[0] for r in rows], *pars, b, *extras, *([] if side is None else [side[0]]), *after)


def _row_spec(tm, width, cb):
    return pl.BlockSpec((tm, width), lambda i: (i, cb))


def _whole_spec(shape):
    nd = len(shape)
    return pl.BlockSpec(tuple(shape), lambda i: (0,) * nd)


def _rows_vjp_call(name, f, rows, pars, cots, drows, *, tm, nrows, cot_mm=None):
    tm = min(tm, nrows)
    nr, npar, nc = len(rows), len(pars), len(cots)
    mm_args, mm_specs = [], []
    if cot_mm is not None:
        mm_a, mm_b, mm_add = cot_mm
        mm_args = [mm_a, mm_b] + ([] if mm_add is None else [mm_add])
        mm_specs = [_row_spec(tm, mm_a.shape[1], 0), _whole_spec(mm_b.shape)]
        mm_specs += [] if mm_add is None else [_row_spec(tm, mm_b.shape[0], 0)]
    alias_bufs, aliases = [], {}
    out_shape, out_specs = [], []
    for (arr, w, cb), d in zip(rows, drows):
        if d is None:
            continue
        dt, into = d
        if into is None:
            out_shape.append(jax.ShapeDtypeStruct((nrows, w), dt))
            out_specs.append(_row_spec(tm, w, 0))
        else:
            buf, total, ocb = into
            if buf is not None:
                aliases[nr + npar + nc + len(alias_bufs)] = len(out_shape)
                alias_bufs.append(buf)
            out_shape.append(jax.ShapeDtypeStruct((nrows, total), dt))
            out_specs.append(_row_spec(tm, w, ocb))
    n_drow = len(out_shape)
    for p in pars:
        out_shape.append(jax.ShapeDtypeStruct(p.shape, F32))
        out_specs.append(_whole_spec(p.shape))
    na = len(alias_bufs)

    def body(*refs):
        rv = [r[...].astype(F32) for r in refs[:nr]]
        pv = [p[...] for p in refs[nr:nr + npar]]
        cv = tuple(c[...].astype(F32) for c in refs[nr + npar:nr + npar + nc])
        o_refs = refs[nr + npar + nc + na + len(mm_args):]
        if mm_args:
            mm_refs = refs[nr + npar + nc + na:nr + npar + nc + na + len(mm_args)]
            c0 = _dot(mm_refs[0][...], mm_refs[1][...], (((1,), (1,)), ((), ())))
            if len(mm_refs) == 3:
                c0 = c0 + mm_refs[2][...].astype(F32)
            cv = (c0,) + cv
        _, vjp = jax.vjp(f, *rv, *pv)
        g = vjp(cv)
        oi = 0
        for ri, d in enumerate(drows):
            if d is not None:
                o_refs[oi][...] = g[ri].astype(o_refs[oi].dtype)
                oi += 1
        first = pl.program_id(0) == 0
        for pi in range(npar):
            acc = o_refs[n_drow + pi]

            @pl.when(first)
            def _(acc=acc):
                acc[...] = jnp.zeros_like(acc)

            acc[...] += g[nr + pi]

    return pl.pallas_call(
        body, name=name, grid=(nrows // tm,),
        in_specs=[_row_spec(tm, w, cb) for _, w, cb in rows] + [_whole_spec(p.shape) for p in pars]
        + [_row_spec(tm, w, cb) for _, w, cb in cots] + [ANY] * na + mm_specs,
        out_specs=out_specs, out_shape=out_shape, input_output_aliases=aliases,
        compiler_params=pltpu.CompilerParams(dimension_semantics=("arbitrary",)),
    )(*[r[0] for r in rows], *pars, *[c[0] for c in cots], *alias_bufs, *mm_args)


def _rms(x, g):
    return x * lax.rsqrt(jnp.mean(x * x, axis=-1, keepdims=True) + EPS) * g


def _f_rms(x, g):
    return (_rms(x, g),)


def _tril_mask():
    return lax.broadcasted_iota(jnp.int32, (CHUNK, CHUNK), 0) >= lax.broadcasted_iota(jnp.int32, (CHUNK, CHUNK), 1)


def _f_branch_a(uvz, ln_g, ln_b, w_s, b_s):
    u = jax.nn.gelu(uvz[:, :E_A])
    v = jax.nn.gelu(uvz[:, E_A:2 * E_A])
    z = uvz[:, 2 * E_A:]
    xc = v - jnp.mean(v, axis=-1, keepdims=True)
    vn = xc * lax.rsqrt(jnp.mean(xc * xc, axis=-1, keepdims=True) + EPS) * ln_g + ln_b
    mask = _tril_mask()
    ws = [jnp.where(mask, w_s[g], 0.0) for g in range(G_A)]
    gw = E_A // G_A
    rows = []
    for c in range(uvz.shape[0] // CHUNK):
        vc = vn[c * CHUNK:(c + 1) * CHUNK]
        rows.append(jnp.concatenate([_dot(ws[g], vc[:, g * gw:(g + 1) * gw]) + b_s[g] for g in range(G_A)], axis=1))
    sv = rows[0] if len(rows) == 1 else jnp.concatenate(rows, axis=0)
    return (u * sv * jax.nn.silu(z),)


def _f_gnorm(y, zb, g):
    yz = y * jax.nn.silu(zb)
    gw = D_INNER // N_GROUPS
    parts = []
    for i in range(N_GROUPS):
        s = yz[:, i * gw:(i + 1) * gw]
        parts.append(s * lax.rsqrt(jnp.mean(s * s, axis=-1, keepdims=True) + EPS))
    return (jnp.concatenate(parts, axis=1) * g,)


def _f_merge(g2, oa, ob):
    return (jax.nn.sigmoid(g2[:, :D_MODEL]) * oa + jax.nn.sigmoid(g2[:, D_MODEL:]) * ob,)


def _f_loss(x1, gp, pe, tgt, fg):
    x2 = x1 + jax.nn.sigmoid(gp) * pe
    err = _rms(x2, fg) - tgt
    return 0.5 * jnp.sum(jnp.mean(err * err, axis=-1))


def _head(x1, p, tgt, ple_g, w_pg, w_ple, fg, *, tm, nrows):
    tm = min(tm, nrows)

    def body(x1_ref, p_ref, t_ref, pg_ref, wpg_ref, wple_ref, fg_ref, hp_ref, dx_ref, dgp_ref, dpe_ref, dfg_ref, loss_ref):
        x1 = x1_ref[...]
        hp_ref[...] = _rms(x1, pg_ref[...]).astype(hp_ref.dtype)
        gp = _dot(hp_ref[...], wpg_ref[...])
        pe = _dot(p_ref[...], wple_ref[...])
        loss, vjp = jax.vjp(_f_loss, x1, gp, pe, t_ref[...], fg_ref[...])
        dx, dgp, dpe, _, dfg = vjp(jnp.ones((), F32))
        dx_ref[...] = dx
        dgp_ref[...] = dgp.astype(dgp_ref.dtype)
        dpe_ref[...] = dpe.astype(dpe_ref.dtype)

        @pl.when(pl.program_id(0) == 0)
        def _():
            dfg_ref[...] = jnp.zeros_like(dfg_ref)
            loss_ref[...] = jnp.zeros_like(loss_ref)

        dfg_ref[...] += dfg
        loss_ref[...] += jnp.full(loss_ref.shape, loss, F32)

    row = _row_spec(tm, D_MODEL, 0)
    act = jax.ShapeDtypeStruct((nrows, D_MODEL), ACT_DTYPE)
    return pl.pallas_call(
        body, name="head", grid=(nrows // tm,),
        in_specs=[row, _row_spec(tm, PLE_DIM, 0), row, _whole_spec((1, D_MODEL)), _whole_spec(w_pg.shape),
                  _whole_spec(w_ple.shape), _whole_spec((1, D_MODEL))],
        out_specs=[row, row, row, row, _whole_spec((1, D_MODEL)), _whole_spec((1, 128))],
        out_shape=[act, jax.ShapeDtypeStruct((nrows, D_MODEL), F32), act, act, jax.ShapeDtypeStruct((1, D_MODEL), F32),
                   jax.ShapeDtypeStruct((1, 128), F32)],
        compiler_params=pltpu.CompilerParams(dimension_semantics=("arbitrary",)),
    )(x1, p, tgt, ple_g, w_pg, w_ple, fg)


def _shift_rows(cur, edge, j, up):
    tm = cur.shape[0]
    row = lax.broadcasted_iota(jnp.int32, cur.shape, 0)
    if up:
        sh = pltpu.roll(cur, tm - j, 0)
        e = jnp.tile(pltpu.roll(edge, 8 - j, 0), (tm // 8, 1))
        return jnp.where(row >= tm - j, e, sh)
    sh = pltpu.roll(cur, j, 0)
    e = jnp.tile(pltpu.roll(edge, j, 0), (tm // 8, 1))
    return jnp.where(row < j, e, sh)


def _conv_pre(cur, prev, w, b):
    acc = cur * w[CONV_K - 1:CONV_K] + b
    for j in range(1, CONV_K):
        acc = acc + _shift_rows(cur, prev, j, up=False) * w[CONV_K - 1 - j:CONV_K - j]
    return acc


def _halo_specs(tm, nrows, cb, before):
    nb = tm // 8
    last = nrows // 8 - 1
    if before:
        return pl.BlockSpec((8, XBC_W), lambda i: (jnp.maximum(i * nb - 1, 0), cb))
    return pl.BlockSpec((8, XBC_W), lambda i: (jnp.minimum((i + 1) * nb, last), cb))


def _conv_fwd(proj, conv_w, conv_b, *, tm, nrows):
    tm = min(tm, nrows)

    def body(cur_ref, prev_ref, w_ref, b_ref, o_ref, pre_ref):
        prev = jnp.where(pl.program_id(0) == 0, 0.0, prev_ref[...].astype(F32))
        pre = _conv_pre(cur_ref[...].astype(F32), prev, w_ref[...], b_ref[...])
        o_ref[...] = jax.nn.silu(pre).astype(o_ref.dtype)
        pre_ref[...] = pre.astype(pre_ref.dtype)

    out = jax.ShapeDtypeStruct((nrows, XBC_W), ACT_DTYPE)
    return pl.pallas_call(
        body, name="conv_fwd", grid=(nrows // tm,),
        in_specs=[_row_spec(tm, XBC_W, XBC_CB), _halo_specs(tm, nrows, XBC_CB, True),
                  _whole_spec((CONV_K, XBC_W)), _whole_spec((1, XBC_W))],
        out_specs=[_row_spec(tm, XBC_W, 0)] * 2, out_shape=[out, out],
        compiler_params=pltpu.CompilerParams(dimension_semantics=("parallel",)),
    )(proj, proj, conv_w, conv_b)


def _conv_bwd_act(pre, dact, *, tm, nrows):
    tm = min(tm, nrows)
    nb = N_GROUPS * N_STATE

    def body(pre_ref, dxs_ref, dbm_ref, dcm_ref, dpre_ref, db_ref):
        pre = pre_ref[...].astype(F32)
        sg = jax.nn.sigmoid(pre)
        dy = jnp.concatenate([dxs_ref[...], dbm_ref[...], dcm_ref[...]], axis=1).astype(F32)
        dpre = dy * sg * (1.0 + pre * (1.0 - sg))
        dpre_ref[...] = dpre.astype(dpre_ref.dtype)

        @pl.when(pl.program_id(0) == 0)
        def _():
            db_ref[...] = jnp.zeros_like(db_ref)

        db_ref[...] += jnp.sum(dpre, axis=0, keepdims=True)

    return pl.pallas_call(
        body, name="conv_bwd_act", grid=(nrows // tm,),
        in_specs=[_row_spec(tm, XBC_W, 0), _row_spec(tm, D_INNER, 0), _row_spec(tm, nb, 0), _row_spec(tm, nb, 0)],
        out_specs=[_row_spec(tm, XBC_W, 0), _whole_spec((1, XBC_W))],
        out_shape=[jax.ShapeDtypeStruct((nrows, XBC_W), ACT_DTYPE), jax.ShapeDtypeStruct((1, XBC_W), F32)],
        compiler_params=pltpu.CompilerParams(dimension_semantics=("arbitrary",)),
    )(pre, *dact)


def _conv_bwd_x(dpre, proj, conv_w, dproj, *, tm, nrows):
    tm = min(tm, nrows)
    ntiles = nrows // tm

    def body(cur_ref, nxt_ref, x_ref, w_ref, _, o_ref, dw_ref):
        cur = cur_ref[...].astype(F32)
        nxt = jnp.where(pl.program_id(0) == ntiles - 1, 0.0, nxt_ref[...].astype(F32))
        x = x_ref[...].astype(F32)
        w = w_ref[...]

        @pl.when(pl.program_id(0) == 0)
        def _():
            dw_ref[...] = jnp.zeros_like(dw_ref)

        acc = cur * w[CONV_K - 1:CONV_K]
        dw_ref[CONV_K - 1:CONV_K, :] += jnp.sum(cur * x, axis=0, keepdims=True)
        for j in range(1, CONV_K):
            u = _shift_rows(cur, nxt, j, up=True)
            acc = acc + u * w[CONV_K - 1 - j:CONV_K - j]
            dw_ref[CONV_K - 1 - j:CONV_K - j, :] += jnp.sum(u * x, axis=0, keepdims=True)
        o_ref[...] = acc.astype(o_ref.dtype)

    return pl.pallas_call(
        body, name="conv_bwd_x", grid=(ntiles,),
        in_specs=[_row_spec(tm, XBC_W, 0), _halo_specs(tm, nrows, 0, False), _row_spec(tm, XBC_W, XBC_CB),
                  _whole_spec((CONV_K, XBC_W)), ANY],
        out_specs=[_row_spec(tm, XBC_W, XBC_CB), _whole_spec((CONV_K, XBC_W))],
        out_shape=[jax.ShapeDtypeStruct(dproj.shape, dproj.dtype), jax.ShapeDtypeStruct((CONV_K, XBC_W), F32)],
        input_output_aliases={4: 0},
        compiler_params=pltpu.CompilerParams(dimension_semantics=("arbitrary",)),
    )(dpre, dpre, proj, conv_w, dproj)


SSD_SPAN = 4
SSD_FWD_SPANS = 2
_XS_GW = D_INNER // N_GROUPS
_NT = (((1,), (1,)), ((), ()))
_TN = (((0,), (0,)), ((), ()))


def _bf16_terms(x, terms):
    parts, rest = [], x
    for _ in range(terms):
        part = rest.astype(jnp.bfloat16)
        parts.append(part)
        rest = rest - part.astype(F32)
    return parts


def _head_lane_matrix():
    return (lax.broadcasted_iota(jnp.int32, (128, _XS_GW), 0)
            == lax.broadcasted_iota(jnp.int32, (128, _XS_GW), 1) // HEAD_DIM).astype(jnp.bfloat16)


@functools.partial(jax.custom_vjp, nondiff_argnums=(1,))
def _head_lanes(cols, terms):
    e = _head_lane_matrix()
    return sum(jnp.dot(t, e, preferred_element_type=F32) for t in _bf16_terms(cols, terms))


def _head_lanes_fwd(cols, terms):
    return _head_lanes(cols, terms), None


def _head_lanes_bwd(terms, _, g):
    e = _head_lane_matrix()
    return (sum(lax.dot_general(t, e, _NT, preferred_element_type=F32) for t in _bf16_terms(g, 2)),)


_head_lanes.defvjp(_head_lanes_fwd, _head_lanes_bwd)


def _ssd_chunk(k, xs, bm, cm, dtr, hprev, dtb, alog, dsk):
    causal, tri, lo = k
    dt = jax.nn.softplus(dtr + dtb)
    da = dt * (-jnp.exp(alog))
    cs = jnp.dot(tri, da, precision=lax.Precision.HIGHEST, preferred_element_type=F32)
    cst = cs.T
    cs_l = _head_lanes(cs, 3)
    xdt = xs * _head_lanes(dt, 2)
    cb = _dot(cm, bm, _NT)
    yd = []
    for q in range(PAIRS_PER_GROUP):
        xq = xdt[:, 128 * q:128 * (q + 1)]
        y2 = [_dot(cb * jnp.exp(jnp.where(causal, cs[:, h:h + 1] - cst[h:h + 1, :], -jnp.inf)), xq)
              for h in (2 * q, 2 * q + 1)]
        yd.append(jnp.where(lo, y2[0], y2[1]))
    y_off = jnp.exp(cs_l) * _dot(cm, hprev, _NT)
    st = _dot(xdt * jnp.exp(cs_l[CHUNK - 1:CHUNK, :] - cs_l), bm, _TN)
    cdec = jnp.exp(cs[CHUNK - 1:CHUNK, :])
    cd_rows = jnp.concatenate(
        [jnp.broadcast_to(cdec[:, h:h + 1], (HEAD_DIM, N_STATE)) for h in range(HEADS_PER_GROUP)], axis=0)
    dsk_l = _head_lanes(jnp.broadcast_to(dsk, (8, 128)), 2)[:1]
    y = jnp.concatenate(yd, axis=1) + y_off + xs * dsk_l
    return y, cd_rows * hprev + st


def _ssd_span(xs, bm, cm, dtr, h0, dtb, alog, dsk):
    li = lax.broadcasted_iota(jnp.int32, (CHUNK, CHUNK), 0)
    si = lax.broadcasted_iota(jnp.int32, (CHUNK, CHUNK), 1)
    causal = li >= si
    k = (causal, causal.astype(F32), si < HEAD_DIM)
    h, ys = h0, []
    for t in range(xs.shape[0] // CHUNK):
        r = slice(t * CHUNK, (t + 1) * CHUNK)
        y, h = _ssd_chunk(k, xs[r], bm[r], cm[r], dtr[r], h, dtb, alog, dsk)
        ys.append(y)
    return (ys[0] if len(ys) == 1 else jnp.concatenate(ys, axis=0)), h


def _ssd_specs(rev, nsteps, rows):
    def s_of(s):
        return nsteps - 1 - s if rev else s

    xs = pl.BlockSpec((rows, _XS_GW), lambda g, s: (s_of(s), g))
    bm = pl.BlockSpec((rows, N_STATE), lambda g, s: (s_of(s), D_INNER // N_STATE + g))
    cm = pl.BlockSpec((rows, N_STATE), lambda g, s: (s_of(s), D_INNER // N_STATE + N_GROUPS + g))
    dt = pl.BlockSpec((rows, 128), lambda g, s: (s_of(s), g))
    par = pl.BlockSpec((1, 128), lambda g, s: (0, g))
    st = pl.BlockSpec((None, None, _XS_GW, N_STATE), lambda g, s: (g, s_of(s), 0, 0))
    return xs, bm, cm, dt, par, st


def _ssd_fwd(act, dtr, dtb, alog, dsk, *, nrows):
    span = CHUNK * min(SSD_SPAN, nrows // CHUNK)
    per_step = SSD_FWD_SPANS if nrows % (SSD_FWD_SPANS * span) == 0 else 1
    rows = per_step * span
    nsteps = nrows // rows
    xs, bm, cm, dt, par, _ = _ssd_specs(False, nsteps, rows)
    st = pl.BlockSpec((None, per_step, _XS_GW, N_STATE), lambda g, s: (g, s, 0, 0))

    def body(xs_ref, b_ref, c_ref, dt_ref, dtb_ref, al_ref, dk_ref, y_ref, st_ref, h_ref):
        @pl.when(pl.program_id(1) == 0)
        def _():
            h_ref[...] = jnp.zeros_like(h_ref)

        h = h_ref[...]
        for i in range(per_step):
            r = slice(i * span, (i + 1) * span)
            st_ref[i] = h
            y, h = _ssd_span(xs_ref[r, :].astype(F32), b_ref[r, :].astype(F32), c_ref[r, :].astype(F32), dt_ref[r, :],
                             h, dtb_ref[...], al_ref[...], dk_ref[...])
            y_ref[r, :] = y.astype(y_ref.dtype)
        h_ref[...] = h

    return pl.pallas_call(
        body, name="ssd_fwd", grid=(N_GROUPS, nsteps),
        in_specs=[xs, bm, cm, dt, par, par, par], out_specs=[xs, st],
        out_shape=[jax.ShapeDtypeStruct((nrows, D_INNER), ACT_DTYPE),
                   jax.ShapeDtypeStruct((N_GROUPS, nrows // span, _XS_GW, N_STATE), F32)],
        scratch_shapes=[pltpu.VMEM((_XS_GW, N_STATE), F32)],
        compiler_params=pltpu.CompilerParams(dimension_semantics=("arbitrary", "arbitrary")),
    )(act, act, act, dtr, dtb, alog, dsk)


def _ssd_bwd(act, dtr, dtb, alog, dsk, states, dy, *, nrows):
    rows = CHUNK * min(SSD_SPAN, nrows // CHUNK)
    nsteps = nrows // rows
    xs, bm, cm, dt, par, st = _ssd_specs(True, nsteps, rows)

    def body(xs_ref, b_ref, c_ref, dt_ref, dtb_ref, al_ref, dk_ref, st_ref, dy_ref,
             dxs_ref, db_ref, dc_ref, ddt_ref, ddtb_ref, dal_ref, ddk_ref, dh_ref):
        @pl.when(pl.program_id(1) == 0)
        def _():
            dh_ref[...] = jnp.zeros_like(dh_ref)
            ddtb_ref[...] = jnp.zeros_like(ddtb_ref)
            dal_ref[...] = jnp.zeros_like(dal_ref)
            ddk_ref[...] = jnp.zeros_like(ddk_ref)

        _, vjp = jax.vjp(_ssd_span, xs_ref[...].astype(F32), b_ref[...].astype(F32), c_ref[...].astype(F32),
                         dt_ref[...], st_ref[...], dtb_ref[...], al_ref[...], dk_ref[...])
        dxs, db, dc, ddt, dh, ddtb, dal, ddk = vjp((dy_ref[...].astype(F32), dh_ref[...]))
        dxs_ref[...] = dxs.astype(dxs_ref.dtype)
        db_ref[...] = db.astype(db_ref.dtype)
        dc_ref[...] = dc.astype(dc_ref.dtype)
        ddt_ref[...] = ddt
        dh_ref[...] = dh
        ddtb_ref[...] += ddtb
        dal_ref[...] += dal
        ddk_ref[...] += ddk

    nb = N_GROUPS * N_STATE
    bspec = pl.BlockSpec((rows, N_STATE), lambda g, s: (nsteps - 1 - s, g))
    return pl.pallas_call(
        body, name="ssd_bwd", grid=(N_GROUPS, nsteps),
        in_specs=[xs, bm, cm, dt, par, par, par, st, xs],
        out_specs=[xs, bspec, bspec, dt, par, par, par],
        out_shape=[jax.ShapeDtypeStruct((nrows, D_INNER), ACT_DTYPE), jax.ShapeDtypeStruct((nrows, nb), ACT_DTYPE),
                   jax.ShapeDtypeStruct((nrows, nb), ACT_DTYPE), jax.ShapeDtypeStruct((nrows, DT_W), F32),
                   jax.ShapeDtypeStruct((1, DT_W), F32), jax.ShapeDtypeStruct((1, DT_W), F32),
                   jax.ShapeDtypeStruct((1, DT_W), F32)],
        scratch_shapes=[pltpu.VMEM((_XS_GW, N_STATE), F32)],
        compiler_params=pltpu.CompilerParams(dimension_semantics=("arbitrary", "arbitrary")),
    )(act, act, act, dtr, dtb, alog, dsk, states, dy)


def _add_epilogue(acc, r):
    return r + acc


def _rms_and_skip(x, g):
    return _rms(x, g), x


def _forward_backward(x, p, tgt, w, late_weights=None, after=()):
    s = x.shape[0]
    act_t, f32 = ACT_DTYPE, F32
    mm = functools.partial(_matmul)
    h, proj, dtr = _rows_matmul("proj", _f_rms, [(x, D_MODEL, 0)], [w["norm_g"]], w["w_main"], out_dtype=act_t,
                                n=MAIN_W, k=D_MODEL, tm=1024, tn=2 * MM_TILE, nrows=s, after=after, side=(w["w_dt"], f32))
    act, conv_pre = _conv_fwd(proj, w["conv_w"], w["conv_b"], tm=512, nrows=s)
    y, states = _ssd_fwd(act, dtr, w["dt_bias"], w["a_log"], w["d_skip"], nrows=s)
    if late_weights is not None:
        w = {**w, **late_weights(states)}
    a_pars = [w["ln_a_g"], w["ln_a_b"], w["w_s"], w["b_s"]]
    y_a, o_a = _rows_matmul("out_a", _f_branch_a, [(proj, UVZ_W, UVZ_CB)], a_pars, w["w_oa"], out_dtype=act_t,
                            n=D_MODEL, k=E_A, tm=512, nrows=s)
    gn_rows = [(y, D_INNER, 0), (proj, ZB_W, ZB_CB)]
    y_b, o_b = _rows_matmul("out_b", _f_gnorm, gn_rows, [w["ssm_norm_g"]], w["w_ob"], out_dtype=act_t,
                            n=D_MODEL, k=D_INNER, tm=1024, nrows=s)
    mg_rows = [(proj, G_W, G_CB), (o_a, D_MODEL, 0), (o_b, D_MODEL, 0)]
    merged, x1 = _rows_matmul("out_proj", _f_merge, mg_rows, [], w["w_out"], out_dtype=f32, n=D_MODEL, k=D_MODEL,
                              tm=1024, nrows=s, extras=(x,), epilogue=_add_epilogue)
    g = {}
    hp, dx2, dgp, dpe, g["final_g"], loss = _head(x1, p, tgt, w["ple_norm_g"], w["w_pg"], w["w_ple"], w["final_g"],
                                                   tm=512, nrows=s)
    g["w_pg"] = mm(hp, dgp, mode="tn", name="d_w_pg", out_dtype=f32, m=D_MODEL, n=D_MODEL, k=s)
    g["w_ple"] = mm(p, dpe, mode="tn", name="d_w_ple", out_dtype=f32, m=PLE_DIM, n=D_MODEL, k=s)
    dx1, g["ple_norm_g"] = _rows_vjp_call(
        "ple_norm_bwd", _rms_and_skip, [(x1, D_MODEL, 0)], [w["ple_norm_g"]], [(dx2, D_MODEL, 0)],
        [(f32, None)], tm=1024, nrows=s, cot_mm=(dgp, w["w_pg"], None))
    g["w_out"] = mm(merged, dx1, mode="tn", name="d_w_out", out_dtype=f32, m=D_MODEL, n=D_MODEL, k=s)
    dproj, do_a, do_b = _rows_vjp_call(
        "merge_bwd", _f_merge, mg_rows, [], [],
        [(act_t, (None, MAIN_W, G_CB)), (act_t, None), (act_t, None)], tm=1024, nrows=s, cot_mm=(dx1, w["w_out"], None))
    g["w_oa"] = mm(y_a, do_a, mode="tn", name="d_w_oa", out_dtype=f32, m=E_A, n=D_MODEL, k=s)
    g["w_ob"] = mm(y_b, do_b, mode="tn", name="d_w_ob", out_dtype=f32, m=D_INNER, n=D_MODEL, k=s)
    dy, dproj, g["ssm_norm_g"] = _rows_vjp_call(
        "gnorm_bwd", _f_gnorm, gn_rows, [w["ssm_norm_g"]], [],
        [(act_t, None), (act_t, (dproj, MAIN_W, ZB_CB))], tm=512, nrows=s, cot_mm=(do_b, w["w_ob"], None))
    dxs, dbm, dcm, ddtr, g["dt_bias"], g["a_log"], g["d_skip"] = _ssd_bwd(
        act, dtr, w["dt_bias"], w["a_log"], w["d_skip"], states, dy, nrows=s)
    dpre, g["conv_b"] = _conv_bwd_act(conv_pre, (dxs, dbm, dcm), tm=1024, nrows=s)
    dproj, g["conv_w"] = _conv_bwd_x(dpre, proj, w["conv_w"], dproj, tm=512, nrows=s)
    dproj, g["ln_a_g"], g["ln_a_b"], g["w_s"], g["b_s"] = _rows_vjp_call(
        "branch_a_bwd", _f_branch_a, [(proj, UVZ_W, UVZ_CB)], a_pars, [],
        [(act_t, (dproj, MAIN_W, UVZ_CB))], tm=512, nrows=s, cot_mm=(do_a, w["w_oa"], None))
    g["w_main"] = mm(h, dproj, mode="tn", name="d_w_main", out_dtype=f32, m=D_MODEL, n=MAIN_W, k=s, tn=MM_TILE // 2)
    g["w_dt"] = mm(h, ddtr, mode="tn", name="d_w_dt", out_dtype=f32, m=D_MODEL, n=DT_W, k=s)
    return loss, g, (dproj, ddtr, dx1)


def _input_grad(x, w, ctx, after=()):
    dproj, ddtr, dx1 = ctx
    s = x.shape[0]
    dh = _matmul(dproj, w["w_main"], mode="nt", name="d_h_main", out_dtype=F32, m=s, n=D_MODEL, k=MAIN_W,
                 tm=MM_TILE // 2, after=after)
    return _rows_vjp_call(
        "pre_norm_bwd", _rms_and_skip, [(x, D_MODEL, 0)], [w["norm_g"]], [(dx1, D_MODEL, 0)],
        [(F32, None)], tm=1024, nrows=s, cot_mm=(ddtr, w["w_dt"], dh))


def _local_step(x, p, tgt, w):
    loss, g, ctx = _forward_backward(x, p, tgt, w)
    grad_x, g["norm_g"] = _input_grad(x, w, ctx)
    return loss, grad_x, g


_O_ZB = 3 * E_A
_O_XBC = _O_ZB + D_INNER
_O_DT = _O_XBC + CONV_DIM
_O_G = _O_DT + N_HEADS


def _heads_to_lanes(v):
    r = v.shape[0]
    v = v.reshape(r, N_GROUPS, HEADS_PER_GROUP)
    return jnp.pad(v, ((0, 0), (0, 0), (0, 128 - HEADS_PER_GROUP))).reshape(r, DT_W)


def _lanes_to_heads(v):
    r = v.shape[0]
    return v.reshape(r, N_GROUPS, 128)[:, :, :HEADS_PER_GROUP].reshape(r, N_HEADS)


def _block_cols(blocks, a, b):
    parts = []
    for k in range(N_CHIPS):
        lo, hi = max(a, k * W_IN_BLOCK), min(b, (k + 1) * W_IN_BLOCK)
        if lo < hi:
            parts.append(blocks[k][:, lo - k * W_IN_BLOCK:hi - k * W_IN_BLOCK])
    return parts


_W_IN_SEGMENTS = ((0, _O_ZB, "m", 0), (_O_ZB, _O_XBC, "m", UVZ_W + XBC_W), (_O_XBC, _O_DT, "m", UVZ_W),
                  (_O_DT, _O_G, "d", 0), (_O_G, N_IN, "m", MAIN_W - G_W))


def _w_in_grad_blocks(gm, gdt):
    blocks = []
    for k in range(N_CHIPS):
        a, b = k * W_IN_BLOCK, (k + 1) * W_IN_BLOCK
        parts = []
        for s, e, src, off in _W_IN_SEGMENTS:
            lo, hi = max(a, s), min(b, e)
            if lo < hi:
                parts.append((gm if src == "m" else gdt)[:, off + lo - s:off + hi - s])
        blocks.append(jnp.concatenate(parts, axis=1))
    return jnp.stack(blocks)


def _layout_weights(f, w_in_blocks=None):
    w = dict(f)
    if w_in_blocks is None:
        w_in = w.pop("w_in")
        w_in_blocks = jnp.stack([w_in[:, k * W_IN_BLOCK:(k + 1) * W_IN_BLOCK] for k in range(N_CHIPS)])
    cols = functools.partial(_block_cols, w_in_blocks)
    w["w_main"] = jnp.concatenate(cols(0, _O_ZB) + cols(_O_XBC, _O_DT) + cols(_O_ZB, _O_XBC) + cols(_O_G, N_IN), axis=1)
    w["w_dt"] = _heads_to_lanes(jnp.concatenate(cols(_O_DT, _O_G), axis=1))
    w["b_s"] = f["b_s"].reshape(G_A, CHUNK, 1)
    for n in ("dt_bias", "a_log", "d_skip"):
        w[n] = _heads_to_lanes(f[n])
    return w


def _natural_grads(g):
    out = dict(g)
    gm = out.pop("w_main")
    gdt = _lanes_to_heads(out.pop("w_dt"))
    out["w_in"] = jnp.concatenate(
        [gm[:, :UVZ_W], gm[:, UVZ_W + XBC_W:UVZ_W + XBC_W + ZB_W], gm[:, UVZ_W:UVZ_W + XBC_W], gdt, gm[:, MAIN_W - G_W:]],
        axis=1)
    out["b_s"] = g["b_s"].reshape(G_A, CHUNK)
    for n in ("dt_bias", "a_log", "d_skip"):
        out[n] = _lanes_to_heads(g[n])
    return out


def _place():
    return lax.axis_index("x"), lax.axis_index("y"), lax.axis_index("c")


def _other_chips(x, y):
    return [(1 - x, y), (x, 1 - y), (1 - x, 1 - y)]


def _rcopy(src, dst, ssem, rsem, dev):
    return pltpu.make_async_remote_copy(src_ref=src, dst_ref=dst, send_sem=ssem, recv_sem=rsem,
                                        device_id=dev, device_id_type=MESH)


def _half(ref_rows, half):
    hs = ref_rows // 2
    return pl.ds(pl.multiple_of(half * hs, 16), hs)


def _gather_weights(shards, conv_shard):
    nw = len(shards)

    def body(*refs):
        sh, cv = refs[:nw], refs[nw]
        out, cvo = refs[nw + 1:2 * nw + 1], refs[2 * nw + 1]
        ici_s, ici_r, fw_s, fw_r, own_s, own_r, cv_s, cv_r = refs[2 * nw + 2:]
        x, y, c = _place()
        me, sib, chips = 2 * x + y, (x, y, 1 - c), _other_chips(x, y)
        own = [_rcopy(sh[w], out[w].at[me], own_s.at[w], own_r.at[w], sib) for w in range(nw)]
        own.append(_rcopy(cv, cvo.at[me], own_s.at[nw], own_r.at[nw], sib))
        for cp in own:
            cp.start()
        sends = []
        for w in range(nw):
            mine = _half(sh[w].shape[0], c)
            for j, chip in enumerate(chips):
                sends.append(_rcopy(sh[w].at[mine], out[w].at[me, mine], ici_s.at[3 * w + j], ici_r.at[3 * w + j], (*chip, c)))
        for j, chip in enumerate(chips):
            sends.append(_rcopy(cv, cvo.at[me], cv_s.at[j], cv_r.at[j], (*chip, c)))
        for cp in sends:
            cp.start()
        for w in range(nw):
            mine = _half(sh[w].shape[0], c)
            for j, chip in enumerate(chips):
                slab = out[w].at[2 * chip[0] + chip[1], mine]
                _rcopy(slab, slab, ici_s.at[3 * w + j], ici_r.at[3 * w + j], (*chip, c)).wait_recv()
                fwd = _rcopy(slab, slab, fw_s.at[3 * w + j], fw_r.at[3 * w + j], sib)
                fwd.start()
                sends.append(fwd)
        for j, chip in enumerate(chips):
            blk = cvo.at[2 * chip[0] + chip[1]]
            _rcopy(blk, blk, cv_s.at[j], cv_r.at[j], (*chip, c)).wait_recv()
        for w in range(nw):
            theirs = _half(sh[w].shape[0], 1 - c)
            for j, chip in enumerate(chips):
                slab = out[w].at[2 * chip[0] + chip[1], theirs]
                _rcopy(slab, slab, fw_s.at[3 * w + j], fw_r.at[3 * w + j], sib).wait_recv()
        for cp in sends:
            cp.wait_send()
        for cp in own:
            cp.wait()

    dma = pltpu.SemaphoreType.DMA
    return pl.pallas_call(
        body, name="gather_weights",
        in_specs=[ANY] * (nw + 1), out_specs=[ANY] * (nw + 1),
        out_shape=[jax.ShapeDtypeStruct((N_CHIPS,) + s.shape, s.dtype) for s in shards]
        + [jax.ShapeDtypeStruct((N_CHIPS,) + conv_shard.shape, conv_shard.dtype)],
        scratch_shapes=[dma((3 * nw,)), dma((3 * nw,)), dma((3 * nw,)), dma((3 * nw,)), dma((nw + 1,)), dma((nw + 1,)),
                        dma((3,)), dma((3,))],
    )(*shards, conv_shard)


_HBM = pl.BlockSpec(memory_space=pltpu.HBM)
_SEM = pl.BlockSpec(memory_space=pltpu.SEMAPHORE)
_EFFECT = pltpu.SideEffectType.DATAFLOW_SIDE_EFFECTING


def _late_gather_copies(sh, out, s_sem, r_sem):
    x, y, c = _place()
    to = [(*chip, c) for chip in _other_chips(x, y)] + [(x, y, 1 - c)]
    return [_rcopy(sh[w], out[w].at[2 * x + y], s_sem.at[4 * w + j], r_sem.at[4 * w + j], dev)
            for w in range(len(sh)) for j, dev in enumerate(to)]


def _late_gather_start(shards):
    n = len(shards)
    lands = [lax.empty((N_CHIPS,) + a.shape, a.dtype) for a in shards]

    def body(*refs):
        for cp in _late_gather_copies(refs[:n], refs[n:2 * n], refs[2 * n], refs[2 * n + 1]):
            cp.start()
        refs[-1][...] = jnp.zeros_like(refs[-1])

    dma = pltpu.SemaphoreType.DMA
    hbm = [pltpu.with_memory_space_constraint(a, pltpu.HBM) for a in list(shards) + lands]
    out = pl.pallas_call(
        body, name="late_gather_start",
        out_shape=[dma((4 * n,)), dma((4 * n,))] + [pltpu.HBM(a.shape, a.dtype) for a in hbm]
        + [jax.ShapeDtypeStruct((8, 128), F32)],
        in_specs=[_HBM] * (2 * n), out_specs=[_SEM, _SEM] + [_HBM] * (2 * n) + [pl.BlockSpec(memory_space=pltpu.VMEM)],
        input_output_aliases={i: 2 + i for i in range(2 * n)},
        compiler_params=pltpu.CompilerParams(has_side_effects=_EFFECT),
    )(*hbm)
    return out[0], out[1], out[2:2 + n], out[2 + n:2 + 2 * n], out[-1]


def _late_gather_wait(s_sem, r_sem, srcs, lands, after):
    n = len(srcs)

    def body(*refs):
        for cp in _late_gather_copies(refs[:n], refs[n:2 * n], refs[2 * n], refs[2 * n + 1]):
            cp.wait_send()
            cp.wait_recv()

    out = pl.pallas_call(
        body, name="late_gather_wait",
        out_shape=[pltpu.HBM(a.shape, a.dtype) for a in list(srcs) + list(lands)],
        in_specs=[_HBM] * (2 * n) + [_SEM, _SEM, ANY], out_specs=[_HBM] * (2 * n),
        input_output_aliases={i: i for i in range(2 * n)},
        compiler_params=pltpu.CompilerParams(has_side_effects=_EFFECT),
    )(*srcs, *lands, s_sem, r_sem, after)
    return out[n:]


def _swap_with_sibling(arrs):
    n = len(arrs)

    def body(*refs):
        src, dst, s_sem, r_sem = refs[:n], refs[n:2 * n], refs[2 * n], refs[2 * n + 1]
        x, y, c = _place()
        cps = [_rcopy(src[i], dst[i], s_sem.at[i], r_sem.at[i], (x, y, 1 - c)) for i in range(n)]
        for cp in cps:
            cp.start()
        for cp in cps:
            cp.wait()

    dma = pltpu.SemaphoreType.DMA
    return pl.pallas_call(
        body, name="swap_with_sibling", in_specs=[ANY] * n, out_specs=[ANY] * n,
        out_shape=[jax.ShapeDtypeStruct(a.shape, a.dtype) for a in arrs], scratch_shapes=[dma((n,)), dma((n,))],
    )(*arrs)


def _scatter_copies(src, land, s_sem, r_sem):
    x, y, c = _place()
    return [_rcopy(src[i].at[2 * chip[0] + chip[1]], land[i].at[j], s_sem.at[3 * i + j], r_sem.at[3 * i + j], (*chip, c))
            for i in range(len(src)) for j, chip in enumerate(_other_chips(x, y))]


def _scatter_blocks_start(arrs):
    n = len(arrs)
    lands = [lax.empty((3,) + a.shape[1:], a.dtype) for a in arrs]

    def body(*refs):
        src, land, s_sem, r_sem, token = refs[:n], refs[n:2 * n], refs[2 * n], refs[2 * n + 1], refs[-1]
        for cp in _scatter_copies(src, land, s_sem, r_sem):
            cp.start()
        token[...] = jnp.zeros_like(token)

    dma = pltpu.SemaphoreType.DMA
    hbm = [pltpu.with_memory_space_constraint(a, pltpu.HBM) for a in list(arrs) + lands]
    out = pl.pallas_call(
        body, name="scatter_blocks_start",
        out_shape=[dma((3 * n,)), dma((3 * n,))] + [pltpu.HBM(a.shape, a.dtype) for a in hbm]
        + [jax.ShapeDtypeStruct((8, 128), F32)],
        in_specs=[_HBM] * (2 * n), out_specs=[_SEM, _SEM] + [_HBM] * (2 * n) + [pl.BlockSpec(memory_space=pltpu.VMEM)],
        input_output_aliases={i: 2 + i for i in range(2 * n)},
        compiler_params=pltpu.CompilerParams(has_side_effects=_EFFECT),
    )(*hbm)
    return out[0], out[1], out[2:2 + n], out[2 + n:2 + 2 * n], out[-1]


def _scatter_blocks_wait(s_sem, r_sem, srcs, lands, after):
    n = len(srcs)

    def body(*refs):
        src, land, s_sem, r_sem = refs[:n], refs[n:2 * n], refs[2 * n], refs[2 * n + 1]
        for cp in _scatter_copies(src, land, s_sem, r_sem):
            cp.wait_send()
            cp.wait_recv()

    out = pl.pallas_call(
        body, name="scatter_blocks_wait",
        out_shape=[pltpu.HBM(a.shape, a.dtype) for a in list(srcs) + list(lands)],
        in_specs=[_HBM] * (2 * n) + [_SEM, _SEM] + [ANY] * len(after), out_specs=[_HBM] * (2 * n),
        input_output_aliases={i: i for i in range(2 * n)},
        compiler_params=pltpu.CompilerParams(has_side_effects=_EFFECT),
    )(*srcs, *lands, s_sem, r_sem, *after)
    return out[:n], out[n:]


def _share_halves(arrs):
    n = len(arrs)

    def body(*refs):
        buf, s_sem, r_sem = refs[n:2 * n], refs[2 * n], refs[2 * n + 1]
        x, y, c = _place()
        cps = []
        for i in range(n):
            mine = buf[i].at[_half(buf[i].shape[0], c)]
            cps.append(_rcopy(mine, mine, s_sem.at[i], r_sem.at[i], (x, y, 1 - c)))
        for cp in cps:
            cp.start()
        for i in range(n):
            theirs = buf[i].at[_half(buf[i].shape[0], 1 - c)]
            _rcopy(theirs, theirs, s_sem.at[i], r_sem.at[i], (x, y, 1 - c)).wait_recv()
        for cp in cps:
            cp.wait_send()

    dma = pltpu.SemaphoreType.DMA
    return pl.pallas_call(
        body, name="share_halves", in_specs=[ANY] * n, out_specs=[ANY] * n,
        out_shape=[jax.ShapeDtypeStruct(a.shape, a.dtype) for a in arrs],
        input_output_aliases={i: i for i in range(n)}, scratch_shapes=[dma((n,)), dma((n,))],
    )(*arrs)


def _small_gather_copies(src, land, s_sem, r_sem):
    x, y, c = _place()
    cps = []
    for d in range(1, N_DEV):
        peer = ((1 - x) if d & 4 else x), ((1 - y) if d & 2 else y), ((1 - c) if d & 1 else c)
        cps.append(_rcopy(src, land.at[4 * x + 2 * y + c], s_sem.at[d - 1], r_sem.at[d - 1], peer))
    return cps


def _small_gather_start(packed):
    def body(src, land, s_sem, r_sem, _, __, token):
        for cp in _small_gather_copies(src, land, s_sem, r_sem):
            cp.start()
        token[...] = jnp.zeros_like(token)

    dma = pltpu.SemaphoreType.DMA
    hbm = [pltpu.with_memory_space_constraint(a, pltpu.HBM) for a in (packed, lax.empty((N_DEV,) + packed.shape, F32))]
    return pl.pallas_call(
        body, name="small_gather_start",
        out_shape=[dma((N_DEV - 1,)), dma((N_DEV - 1,))] + [pltpu.HBM(a.shape, a.dtype) for a in hbm]
        + [jax.ShapeDtypeStruct((8, 128), F32)],
        in_specs=[_HBM] * 2, out_specs=[_SEM, _SEM, _HBM, _HBM, pl.BlockSpec(memory_space=pltpu.VMEM)],
        input_output_aliases={0: 2, 1: 3}, compiler_params=pltpu.CompilerParams(has_side_effects=_EFFECT),
    )(*hbm)


def _small_gather_wait(s_sem, r_sem, src, land, after):
    def body(src, land, s_sem, r_sem, *_):
        for cp in _small_gather_copies(src, land, s_sem, r_sem):
            cp.wait_send()
            cp.wait_recv()

    return pl.pallas_call(
        body, name="small_gather_wait", out_shape=[pltpu.HBM(src.shape, src.dtype), pltpu.HBM(land.shape, land.dtype)],
        in_specs=[_HBM, _HBM, _SEM, _SEM, ANY], out_specs=[_HBM, _HBM], input_output_aliases={0: 0, 1: 1},
        compiler_params=pltpu.CompilerParams(has_side_effects=_EFFECT),
    )(src, land, s_sem, r_sem, after)


def _small_sum(own, land, dev_arr):
    def body(me_ref, own_ref, land_ref, o_ref):
        acc = jnp.zeros(o_ref.shape, F32)
        for d in range(N_DEV):
            acc = acc + jnp.where(me_ref[0] == d, own_ref[...], land_ref[d])
        o_ref[...] = acc

    return pl.pallas_call(
        body, name="small_sum", out_shape=jax.ShapeDtypeStruct(own.shape, F32),
        grid_spec=pltpu.PrefetchScalarGridSpec(
            num_scalar_prefetch=1, grid=(1,),
            in_specs=[pl.BlockSpec(own.shape, lambda i, m: (0, 0)), pl.BlockSpec(land.shape, lambda i, m: (0, 0, 0))],
            out_specs=pl.BlockSpec(own.shape, lambda i, m: (0, 0))),
    )(dev_arr, own, land)


def _row_tile(rows, cols):
    tr = max(8, min(rows, (1 << 20) // (4 * cols) // 8 * 8))
    while rows % tr:
        tr -= 8
    return tr


def _chip_sum(name, g5, recv, c_arr):
    nb, _, hs, cols = g5.shape
    tr = _row_tile(hs, cols)

    def body(_, a_ref, b_ref, o_ref):
        o_ref[...] = (a_ref[...] + b_ref[...].astype(F32)).astype(o_ref.dtype)

    blk = pl.BlockSpec((None, tr, cols), lambda b, i, c: (b, i, 0))
    return pl.pallas_call(
        body, name=name,
        grid_spec=pltpu.PrefetchScalarGridSpec(
            num_scalar_prefetch=1, grid=(nb, hs // tr),
            in_specs=[pl.BlockSpec((None, None, tr, cols), lambda b, i, c: (b, c[0], i, 0)), blk], out_specs=blk),
        out_shape=jax.ShapeDtypeStruct((nb, hs, cols), WIRE_DTYPE),
    )(c_arr, g5, recv)


def _final_sum(name, own, recv, place_arr):
    _, hs, cols = own.shape
    tr = _row_tile(hs, cols)
    nt = hs // tr

    def body(_, a_ref, r_ref, o_ref):
        o_ref[...] = ((a_ref[...].astype(F32) + r_ref[0].astype(F32)) + r_ref[1].astype(F32)) + r_ref[2].astype(F32)

    return pl.pallas_call(
        body, name=name,
        grid_spec=pltpu.PrefetchScalarGridSpec(
            num_scalar_prefetch=1, grid=(nt,),
            in_specs=[pl.BlockSpec((None, tr, cols), lambda i, m: (m[0], i, 0)),
                      pl.BlockSpec((3, tr, cols), lambda i, m: (0, i, 0))],
            out_specs=pl.BlockSpec((tr, cols), lambda i, m: (m[1] * nt + i, 0))),
        out_shape=jax.ShapeDtypeStruct((2 * hs, cols), F32),
    )(place_arr, own, recv)


def _adamw(w, g, m, v):
    m = ADAM_B1 * m + (1.0 - ADAM_B1) * g
    v = ADAM_B2 * v + (1.0 - ADAM_B2) * (g * g)
    m_hat = m / (1.0 - ADAM_B1 ** ADAM_STEP)
    v_hat = v / (1.0 - ADAM_B2 ** ADAM_STEP)
    return -ADAM_LR * (m_hat / (jnp.sqrt(v_hat) + ADAM_EPS) + ADAM_WD * w), m, v


def _adamw_call(name, w, g, m, v):
    rows, cols = w.shape
    tr = _row_tile(rows, cols)
    if 4 * tr * cols >= (1 << 18):
        blk, steps = pl.BlockSpec((tr, cols), lambda i: (i, 0)), rows // tr
    else:
        blk, steps = pl.BlockSpec((rows, 128), lambda i: (0, i)), cols // 128

    def body(w_ref, g_ref, m_ref, v_ref, d_ref, nm_ref, nv_ref, go_ref):
        g = g_ref[...]
        d_ref[...], nm_ref[...], nv_ref[...] = _adamw(w_ref[...], g, m_ref[...], v_ref[...])
        go_ref[...] = g

    return pl.pallas_call(
        body, name=name, grid=(steps,), in_specs=[blk] * 4, out_specs=[blk] * 4,
        out_shape=[jax.ShapeDtypeStruct(w.shape, F32)] * 4,
        compiler_params=pltpu.CompilerParams(dimension_semantics=("parallel",)),
    )(w, g, m, v)


def _adamw_small(ws, gs, ms, vs):
    n = len(ws)

    def body(*refs):
        for i in range(n):
            w_ref, g_ref, m_ref, v_ref = (refs[k * n + i] for k in range(4))
            d, nm, nv = _adamw(w_ref[...], g_ref[...], m_ref[...], v_ref[...])
            refs[4 * n + i][...] = d
            refs[5 * n + i][...] = nm
            refs[6 * n + i][...] = nv

    out = pl.pallas_call(
        body, name="adamw_small", out_shape=[jax.ShapeDtypeStruct(a.shape, F32) for a in ws] * 3,
    )(*ws, *gs, *ms, *vs)
    return out[:n], out[n:2 * n], out[2 * n:]


_BIG = ("w_in", "w_oa", "w_ob", "w_out", "w_pg", "w_ple")
_SMALL = ("norm_g", "ln_a_g", "ln_a_b", "w_s", "b_s", "conv_w", "conv_b", "dt_bias", "a_log", "d_skip", "ssm_norm_g",
          "ple_norm_g", "final_g")
_WEIGHTS = ("norm_g", "w_in", "ln_a_g", "ln_a_b", "w_s", "b_s", "conv_w", "conv_b", "dt_bias", "a_log", "d_skip",
            "ssm_norm_g", "w_oa", "w_ob", "w_out", "ple_norm_g", "w_pg", "w_ple", "final_g")
_COL_SHARDED = ("w_in", "w_ple")
_PACK = 1024


def _blocks_to_full(col_sharded, blocks):
    if col_sharded:
        return jnp.concatenate([blocks[k] for k in range(N_CHIPS)], axis=1)
    return blocks.reshape(N_CHIPS * blocks.shape[1], blocks.shape[2])


def _full_to_blocks(col_sharded, full):
    if col_sharded:
        w = full.shape[1] // N_CHIPS
        return jnp.stack([full[:, k * w:(k + 1) * w] for k in range(N_CHIPS)])
    return full.reshape(N_CHIPS, full.shape[0] // N_CHIPS, full.shape[1])


def _two_d(n, a):
    if n == "w_s":
        return a.reshape(G_A * CHUNK, CHUNK)
    if n in ("b_s", "conv_w"):
        return a.reshape(a.shape[-2], a.shape[-1])
    return a.reshape(1, a.shape[-1])


def kernel(x, p, norm_g, w_in, ln_a_g, ln_a_b, w_s, b_s, conv_w, conv_b, dt_bias, a_log, d_skip, ssm_norm_g, w_oa, w_ob, w_out, ple_norm_g, w_pg, w_ple, final_g, loss_target, m_norm_g, m_w_in, m_ln_a_g, m_ln_a_b, m_w_s, m_b_s, m_conv_w, m_conv_b, m_dt_bias, m_a_log, m_d_skip, m_ssm_norm_g, m_w_oa, m_w_ob, m_w_out, m_ple_norm_g, m_w_pg, m_w_ple, m_final_g, v_norm_g, v_w_in, v_ln_a_g, v_ln_a_b, v_w_s, v_b_s, v_conv_w, v_conv_b, v_dt_bias, v_a_log, v_d_skip, v_ssm_norm_g, v_w_oa, v_w_ob, v_w_out, v_ple_norm_g, v_w_pg, v_w_ple, v_final_g):
    wt = dict(norm_g=norm_g, w_in=w_in, ln_a_g=ln_a_g, ln_a_b=ln_a_b, w_s=w_s, b_s=b_s, conv_w=conv_w, conv_b=conv_b,
              dt_bias=dt_bias, a_log=a_log, d_skip=d_skip, ssm_norm_g=ssm_norm_g, w_oa=w_oa, w_ob=w_ob, w_out=w_out,
              ple_norm_g=ple_norm_g, w_pg=w_pg, w_ple=w_ple, final_g=final_g)
    mom = dict(norm_g=m_norm_g, w_in=m_w_in, ln_a_g=m_ln_a_g, ln_a_b=m_ln_a_b, w_s=m_w_s, b_s=m_b_s, conv_w=m_conv_w,
               conv_b=m_conv_b, dt_bias=m_dt_bias, a_log=m_a_log, d_skip=m_d_skip, ssm_norm_g=m_ssm_norm_g, w_oa=m_w_oa,
               w_ob=m_w_ob, w_out=m_w_out, ple_norm_g=m_ple_norm_g, w_pg=m_w_pg, w_ple=m_w_ple, final_g=m_final_g)
    vel = dict(norm_g=v_norm_g, w_in=v_w_in, ln_a_g=v_ln_a_g, ln_a_b=v_ln_a_b, w_s=v_w_s, b_s=v_b_s, conv_w=v_conv_w,
               conv_b=v_conv_b, dt_bias=v_dt_bias, a_log=v_a_log, d_skip=v_d_skip, ssm_norm_g=v_ssm_norm_g, w_oa=v_w_oa,
               w_ob=v_w_ob, w_out=v_w_out, ple_norm_g=v_ple_norm_g, w_pg=v_w_pg, w_ple=v_w_ple, final_g=v_final_g)
    xi, yi, ci = _place()
    me = 2 * xi + yi
    c_arr = jnp.reshape(ci, (1,)).astype(jnp.int32)
    place_arr = jnp.stack([me, ci]).astype(jnp.int32)

    shard = {n: wt[n][0] for n in _BIG}
    wire = {n: shard[n].astype(WIRE_DTYPE) for n in _BIG}
    w_in_blocks, conv_blocks = _gather_weights([wire["w_in"]], conv_w[0])
    g_ssem, g_rsem, g_sent, g_lands, g_token = _late_gather_start([wire[n] for n in _BIG[1:]])
    full = {"conv_w": _blocks_to_full(True, conv_blocks)}
    for n in _SMALL:
        if n != "conv_w":
            full[n] = wt[n][0] if wt[n].ndim > 2 else wt[n].reshape(1, wt[n].shape[-1])

    def late_weights(after):
        blocks = _late_gather_wait(g_ssem, g_rsem, g_sent, g_lands, after)
        return {n: _blocks_to_full(n in _COL_SHARDED, b) for n, b in zip(_BIG[1:], blocks)}

    w = _layout_weights(full, w_in_blocks=w_in_blocks)
    loss_row, g, ctx = _forward_backward(x[0], p[0, 0], loss_target[0], w, late_weights, after=(g_token,))
    loss = lax.psum(loss_row[0, 0], ("x", "y", "c"))

    parts = {n: _full_to_blocks(n in _COL_SHARDED, g[n]) for n in _BIG[1:]}
    parts["w_main"] = g["w_main"][None]
    parts["w_dt"] = jnp.pad(_lanes_to_heads(g["w_dt"]), ((0, 0), (0, 128 - N_HEADS)))[None]
    names = ("w_main", "w_dt") + _BIG[1:]
    g5 = {n: parts[n].reshape(parts[n].shape[0], 2, parts[n].shape[1] // 2, parts[n].shape[2]) for n in names}
    to_sibling = [lax.dynamic_index_in_dim(g5[n], 1 - ci, axis=1, keepdims=False).astype(WIRE_DTYPE) for n in names]
    from_sibling = _swap_with_sibling(to_sibling)
    chip = {n: _chip_sum("chip_sum_" + n, g5[n], r, c_arr) for n, r in zip(names, from_sibling)}
    chip["w_in"] = _w_in_grad_blocks(chip["w_main"][0], chip["w_dt"][0])
    chip_wire = [chip[n] for n in _BIG]
    s_sem, r_sem, sent, lands, token = _scatter_blocks_start(chip_wire)
    grad_x, g["norm_g"] = _input_grad(x[0], w, ctx, after=(token,))
    g = _natural_grads(g)

    pieces = [_two_d(n, g[n]).reshape(-1) for n in _SMALL]
    sizes = [v.shape[0] for v in pieces]
    padded = [-(-s // _PACK) * _PACK for s in sizes]
    packed = jnp.concatenate([jnp.pad(v, (0, ps - s)) for v, s, ps in zip(pieces, sizes, padded)]).reshape(-1, 128)
    a_ssem, a_rsem, a_src, a_land, a_token = _small_gather_start(packed)

    sent, from_chips = _scatter_blocks_wait(s_sem, r_sem, sent, lands, (grad_x, a_token))
    halves = [_final_sum("final_sum_" + n, a, r, place_arr) for n, a, r in zip(_BIG, sent, from_chips)]
    grads = dict(zip(_BIG, _share_halves(halves)))
    delta, new_m, new_v = {}, {}, {}
    for n in _BIG:
        t = jnp.transpose if n == "w_in" else (lambda a: a)
        res = _adamw_call("adamw_" + n, t(shard[n]), t(grads[n]), t(mom[n][0]), t(vel[n][0]))
        delta[n], new_m[n], new_v[n], grads[n] = (t(r) for r in res)

    a_src, a_land = _small_gather_wait(a_ssem, a_rsem, a_src, a_land, delta["w_in"])
    summed = _small_sum(a_src, a_land, jnp.reshape(4 * xi + 2 * yi + ci, (1,)).astype(jnp.int32)).reshape(-1)
    off = 0
    for n, s, ps in zip(_SMALL, sizes, padded):
        grads[n] = summed[off:off + s].reshape(_two_d(n, g[n]).shape)
        off += ps
    grads["conv_w"] = lax.dynamic_slice_in_dim(grads["conv_w"], me * (CONV_DIM // N_CHIPS), CONV_DIM // N_CHIPS, axis=1)
    small = _adamw_small([_two_d(n, wt[n]) for n in _SMALL], [grads[n] for n in _SMALL],
                         [_two_d(n, mom[n]) for n in _SMALL], [_two_d(n, vel[n]) for n in _SMALL])
    for i, n in enumerate(_SMALL):
        delta[n], new_m[n], new_v[n] = small[0][i], small[1][i], small[2][i]

    def shaped(d):
        return [d[n].reshape(wt[n].shape) for n in _WEIGHTS]

    return (loss, grad_x[None], *shaped(grads), *shaped(delta), *shaped(new_m), *shaped(new_v))
```

```python
import functools

import jax
import jax.numpy as jnp
from jax import lax
from jax.experimental import pallas as pl
from jax.experimental.pallas import tpu as pltpu

F32 = jnp.float32
MXU_DTYPE = jnp.bfloat16
ACT_DTYPE = jnp.bfloat16
WIRE_DTYPE = jnp.bfloat16

D_MODEL = 1024
PLE_DIM = 256
CHUNK = 128
EPS = 1e-6
E_A = D_MODEL
G_A = 4
D_INNER = 2 * D_MODEL
HEAD_DIM = 64
N_HEADS = D_INNER // HEAD_DIM
N_STATE = 128
N_GROUPS = 4
HEADS_PER_GROUP = N_HEADS // N_GROUPS
PAIRS_PER_GROUP = HEADS_PER_GROUP // 2
CONV_K = 4
CONV_DIM = D_INNER + 2 * N_GROUPS * N_STATE
N_IN = 3 * E_A + D_INNER + CONV_DIM + N_HEADS + 2 * D_MODEL
N_CHIPS = 4
N_DEV = 8
W_IN_BLOCK = N_IN // N_CHIPS

UVZ_W, XBC_W, ZB_W, G_W = 3 * E_A, CONV_DIM, D_INNER, 2 * D_MODEL
MAIN_W = UVZ_W + XBC_W + ZB_W + G_W
UVZ_CB, XBC_CB, ZB_CB, G_CB = 0, 1, 3, 4
DT_W = N_GROUPS * 128

ADAM_LR, ADAM_B1, ADAM_B2, ADAM_EPS, ADAM_WD, ADAM_STEP = 0.001, 0.9, 0.999, 1e-08, 0.01, 10

MESH = pl.DeviceIdType.MESH
ANY = pl.BlockSpec(memory_space=pl.ANY)


def _mxu(v):
    return v.astype(MXU_DTYPE)


def _dot(a, b, dims=(((1,), (0,)), ((), ()))):
    return lax.dot_general(_mxu(a), _mxu(b), dims, preferred_element_type=F32)


V7X_MXU_WIDTH = 256
V7X_SCOPED_VMEM_BYTES = 60000 * 1024
MM_TILE = 4 * V7X_MXU_WIDTH
MM_VMEM_BUDGET = 4 * V7X_SCOPED_VMEM_BYTES // 5


def _mm_tk(m, n, k, tm, tn, a_bytes, b_bytes, out_bytes, extra_bytes):
    one_tile = m == tm and n == tn
    for parts in range(2 if one_tile else 1, k // 128 + 1):
        if k % parts or (k // parts) % 128 and parts > 1:
            continue
        tk = k // parts
        a_bufs = 1 if (parts == 1 and m == tm) else 2
        b_bufs = 1 if (parts == 1 and n == tn) else 2
        need = (tk * (a_bufs * tm * a_bytes + b_bufs * tn * b_bytes) + 2 * tm * tn * (out_bytes + extra_bytes)
                + (tm * tn * 4 if parts > 1 else 0))
        if need <= MM_VMEM_BUDGET:
            return tk
    return 128


def _matmul(a, b, *, mode, name, out_dtype, m, n, k, tm=MM_TILE, tn=MM_TILE, tk=None, a_off=0, b_off=0,
            extras=(), epilogue=None, after=()):
    tm, tn = min(tm, m), min(tn, n)
    if tk is None:
        tk = _mm_tk(m, n, k, tm, tn, a.dtype.itemsize, b.dtype.itemsize, jnp.dtype(out_dtype).itemsize,
                    sum(e.dtype.itemsize for e in extras))
    tk = min(tk, k)
    assert m % tm == 0 and n % tn == 0 and k % tk == 0, (name, m, n, k, tm, tn, tk)
    nk = k // tk
    a_mode = pl.Buffered(1) if (nk == 1 and m == tm) else None
    b_mode = pl.Buffered(1) if (nk == 1 and n == tn) else None
    if mode == "nn":
        assert a_off % tk == 0 and b_off % tn == 0
        a_spec = pl.BlockSpec((tm, tk), lambda i, j, kk: (i, kk + a_off // tk), pipeline_mode=a_mode)
        b_spec = pl.BlockSpec((tk, tn), lambda i, j, kk: (kk, j + b_off // tn), pipeline_mode=b_mode)
        dims = (((1,), (0,)), ((), ()))
    elif mode == "nt":
        a_spec = pl.BlockSpec((tm, tk), lambda i, j, kk: (i, kk), pipeline_mode=a_mode)
        b_spec = pl.BlockSpec((tn, tk), lambda i, j, kk: (j, kk), pipeline_mode=b_mode)
        dims = (((1,), (1,)), ((), ()))
    else:
        assert a_off % tm == 0 and b_off % tn == 0
        a_spec = pl.BlockSpec((tk, tm), lambda i, j, kk: (kk, i + a_off // tm), pipeline_mode=a_mode)
        b_spec = pl.BlockSpec((tk, tn), lambda i, j, kk: (kk, j + b_off // tn), pipeline_mode=b_mode)
        dims = (((0,), (0,)), ((), ()))
    ne = len(extras)

    def finish(acc, extra_refs, o_ref):
        res = acc if epilogue is None else epilogue(acc, *[e[...] for e in extra_refs])
        o_ref[...] = res.astype(o_ref.dtype)

    def body(a_ref, b_ref, *rest):
        extra_refs, o_ref = rest[:ne], rest[ne + len(after)]
        part = _dot(a_ref[...], b_ref[...], dims)
        if nk == 1:
            finish(part, extra_refs, o_ref)
            return
        acc_ref = rest[ne + len(after) + 1]
        kk = pl.program_id(2)

        @pl.when(kk == 0)
        def _():
            acc_ref[...] = part

        @pl.when(kk > 0)
        def _():
            acc_ref[...] += part

        @pl.when(kk == nk - 1)
        def _():
            finish(acc_ref[...], extra_refs, o_ref)

    o_spec = pl.BlockSpec((tm, tn), lambda i, j, kk: (i, j))
    return pl.pallas_call(
        body, name=name, grid=(m // tm, n // tn, nk),
        in_specs=[a_spec, b_spec] + [o_spec] * ne + [ANY] * len(after), out_specs=o_spec,
        out_shape=jax.ShapeDtypeStruct((m, n), out_dtype),
        scratch_shapes=[pltpu.VMEM((tm, tn), F32)] if nk > 1 else [],
        compiler_params=pltpu.CompilerParams(dimension_semantics=("parallel", "parallel", "arbitrary")),
    )(a, b, *extras, *after)


def _rows_matmul(name, f, rows, pars, b, *, out_dtype, n, k, tm, nrows, tn=MM_TILE, extras=(), epilogue=None, after=(),
                 side=None):
    tm, tn = min(tm, nrows), min(tn, n)
    assert nrows % tm == 0 and n % tn == 0, (name, nrows, n, tm, tn)
    nr, npar, ne, nj = len(rows), len(pars), len(extras), n // tn
    ns = 0 if side is None else 1
    n_in = nr + npar + 1 + ne + ns + len(after)

    def body(*refs):
        row_refs, par_refs, b_ref = refs[:nr], refs[nr:nr + npar], refs[nr + npar]
        extra_refs = refs[nr + npar + 1:nr + npar + 1 + ne]
        a_ref, o_ref = refs[n_in], refs[n_in + 1]

        def make_a():
            a = f(*[r[...].astype(F32) for r in row_refs], *[p[...] for p in par_refs])[0]
            a_ref[...] = a.astype(a_ref.dtype)
            if ns:
                refs[n_in + 2][...] = _dot(a_ref[...], refs[nr + npar + 1 + ne][...]).astype(refs[n_in + 2].dtype)

        if nj == 1:
            make_a()
        else:
            pl.when(pl.program_id(1) == 0)(make_a)
        res = _dot(a_ref[...], b_ref[...])
        if epilogue is not None:
            res = epilogue(res, *[e[...] for e in extra_refs])
        o_ref[...] = res.astype(o_ref.dtype)

    o_spec = pl.BlockSpec((tm, tn), lambda i, j: (i, j))
    side_in = [] if side is None else [pl.BlockSpec(tuple(side[0].shape), lambda i, j: (0, 0))]
    side_out = [] if side is None else [pl.BlockSpec((tm, side[0].shape[1]), lambda i, j: (i, 0))]
    side_shape = [] if side is None else [jax.ShapeDtypeStruct((nrows, side[0].shape[1]), side[1])]
    return pl.pallas_call(
        body, name=name, grid=(nrows // tm, nj),
        in_specs=[pl.BlockSpec((tm, w), lambda i, j, cb=cb: (i, cb)) for _, w, cb in rows]
        + [pl.BlockSpec(tuple(p.shape), lambda i, j, nd=p.ndim: (0,) * nd) for p in pars]
        + [pl.BlockSpec((k, tn), lambda i, j: (0, j))] + [o_spec] * ne + side_in + [ANY] * len(after),
        out_specs=[pl.BlockSpec((tm, k), lambda i, j: (i, 0)), o_spec] + side_out,
        out_shape=[jax.ShapeDtypeStruct((nrows, k), ACT_DTYPE), jax.ShapeDtypeStruct((nrows, n), out_dtype)] + side_shape,
        compiler_params=pltpu.CompilerParams(dimension_semantics=("parallel", "arbitrary")),
    )(*[r[0] for r in rows], *pars, b, *extras, *([] if side is None else [side[0]]), *after)


def _row_spec(tm, width, cb):
    return pl.BlockSpec((tm, width), lambda i: (i, cb))


def _whole_spec(shape):
    nd = len(shape)
    return pl.BlockSpec(tuple(shape), lambda i: (0,) * nd)


def _rows_vjp_call(name, f, rows, pars, cots, drows, *, tm, nrows, cot_mm=None):
    tm = min(tm, nrows)
    nr, npar, nc = len(rows), len(pars), len(cots)
    mm_args, mm_specs = [], []
    if cot_mm is not None:
        mm_a, mm_b, mm_add = cot_mm
        mm_args = [mm_a, mm_b] + ([] if mm_add is None else [mm_add])
        mm_specs = [_row_spec(tm, mm_a.shape[1], 0), _whole_spec(mm_b.shape)]
        mm_specs += [] if mm_add is None else [_row_spec(tm, mm_b.shape[0], 0)]
    alias_bufs, aliases = [], {}
    out_shape, out_specs = [], []
    for (arr, w, cb), d in zip(rows, drows):
        if d is None:
            continue
        dt, into = d
        if into is None:
            out_shape.append(jax.ShapeDtypeStruct((nrows, w), dt))
            out_specs.append(_row_spec(tm, w, 0))
        else:
            buf, total, ocb = into
            if buf is not None:
                aliases[nr + npar + nc + len(alias_bufs)] = len(out_shape)
                alias_bufs.append(buf)
            out_shape.append(jax.ShapeDtypeStruct((nrows, total), dt))
            out_specs.append(_row_spec(tm, w, ocb))
    n_drow = len(out_shape)
    for p in pars:
        out_shape.append(jax.ShapeDtypeStruct(p.shape, F32))
        out_specs.append(_whole_spec(p.shape))
    na = len(alias_bufs)

    def body(*refs):
        rv = [r[...].astype(F32) for r in refs[:nr]]
        pv = [p[...] for p in refs[nr:nr + npar]]
        cv = tuple(c[...].astype(F32) for c in refs[nr + npar:nr + npar + nc])
        o_refs = refs[nr + npar + nc + na + len(mm_args):]
        if mm_args:
            mm_refs = refs[nr + npar + nc + na:nr + npar + nc + na + len(mm_args)]
            c0 = _dot(mm_refs[0][...], mm_refs[1][...], (((1,), (1,)), ((), ())))
            if len(mm_refs) == 3:
                c0 = c0 + mm_refs[2][...].astype(F32)
            cv = (c0,) + cv
        _, vjp = jax.vjp(f, *rv, *pv)
        g = vjp(cv)
        oi = 0
        for ri, d in enumerate(drows):
            if d is not None:
                o_refs[oi][...] = g[ri].astype(o_refs[oi].dtype)
                oi += 1
        first = pl.program_id(0) == 0
        for pi in range(npar):
            acc = o_refs[n_drow + pi]

            @pl.when(first)
            def _(acc=acc):
                acc[...] = jnp.zeros_like(acc)

            acc[...] += g[nr + pi]

    return pl.pallas_call(
        body, name=name, grid=(nrows // tm,),
        in_specs=[_row_spec(tm, w, cb) for _, w, cb in rows] + [_whole_spec(p.shape) for p in pars]
        + [_row_spec(tm, w, cb) for _, w, cb in cots] + [ANY] * na + mm_specs,
        out_specs=out_specs, out_shape=out_shape, input_output_aliases=aliases,
        compiler_params=pltpu.CompilerParams(dimension_semantics=("arbitrary",)),
    )(*[r[0] for r in rows], *pars, *[c[0] for c in cots], *alias_bufs, *mm_args)


def _rms(x, g):
    return x * lax.rsqrt(jnp.mean(x * x, axis=-1, keepdims=True) + EPS) * g


def _f_rms(x, g):
    return (_rms(x, g),)


def _tril_mask():
    return lax.broadcasted_iota(jnp.int32, (CHUNK, CHUNK), 0) >= lax.broadcasted_iota(jnp.int32, (CHUNK, CHUNK), 1)


def _f_branch_a(uvz, ln_g, ln_b, w_s, b_s):
    u = jax.nn.gelu(uvz[:, :E_A])
    v = jax.nn.gelu(uvz[:, E_A:2 * E_A])
    z = uvz[:, 2 * E_A:]
    xc = v - jnp.mean(v, axis=-1, keepdims=True)
    vn = xc * lax.rsqrt(jnp.mean(xc * xc, axis=-1, keepdims=True) + EPS) * ln_g + ln_b
    mask = _tril_mask()
    ws = [jnp.where(mask, w_s[g], 0.0) for g in range(G_A)]
    gw = E_A // G_A
    rows = []
    for c in range(uvz.shape[0] // CHUNK):
        vc = vn[c * CHUNK:(c + 1) * CHUNK]
        rows.append(jnp.concatenate([_dot(ws[g], vc[:, g * gw:(g + 1) * gw]) + b_s[g] for g in range(G_A)], axis=1))
    sv = rows[0] if len(rows) == 1 else jnp.concatenate(rows, axis=0)
    return (u * sv * jax.nn.silu(z),)


def _f_gnorm(y, zb, g):
    yz = y * jax.nn.silu(zb)
    gw = D_INNER // N_GROUPS
    parts = []
    for i in range(N_GROUPS):
        s = yz[:, i * gw:(i + 1) * gw]
        parts.append(s * lax.rsqrt(jnp.mean(s * s, axis=-1, keepdims=True) + EPS))
    return (jnp.concatenate(parts, axis=1) * g,)


def _f_merge(g2, oa, ob):
    return (jax.nn.sigmoid(g2[:, :D_MODEL]) * oa + jax.nn.sigmoid(g2[:, D_MODEL:]) * ob,)


def _f_loss(x1, gp, pe, tgt, fg):
    x2 = x1 + jax.nn.sigmoid(gp) * pe
    err = _rms(x2, fg) - tgt
    return 0.5 * jnp.sum(jnp.mean(err * err, axis=-1))


def _head(x1, p, tgt, ple_g, w_pg, w_ple, fg, *, tm, nrows):
    tm = min(tm, nrows)

    def body(x1_ref, p_ref, t_ref, pg_ref, wpg_ref, wple_ref, fg_ref, hp_ref, dx_ref, dgp_ref, dpe_ref, dfg_ref, loss_ref):
        x1 = x1_ref[...]
        hp_ref[...] = _rms(x1, pg_ref[...]).astype(hp_ref.dtype)
        gp = _dot(hp_ref[...], wpg_ref[...])
        pe = _dot(p_ref[...], wple_ref[...])
        loss, vjp = jax.vjp(_f_loss, x1, gp, pe, t_ref[...], fg_ref[...])
        dx, dgp, dpe, _, dfg = vjp(jnp.ones((), F32))
        dx_ref[...] = dx
        dgp_ref[...] = dgp.astype(dgp_ref.dtype)
        dpe_ref[...] = dpe.astype(dpe_ref.dtype)

        @pl.when(pl.program_id(0) == 0)
        def _():
            dfg_ref[...] = jnp.zeros_like(dfg_ref)
            loss_ref[...] = jnp.zeros_like(loss_ref)

        dfg_ref[...] += dfg
        loss_ref[...] += jnp.full(loss_ref.shape, loss, F32)

    row = _row_spec(tm, D_MODEL, 0)
    act = jax.ShapeDtypeStruct((nrows, D_MODEL), ACT_DTYPE)
    return pl.pallas_call(
        body, name="head", grid=(nrows // tm,),
        in_specs=[row, _row_spec(tm, PLE_DIM, 0), row, _whole_spec((1, D_MODEL)), _whole_spec(w_pg.shape),
                  _whole_spec(w_ple.shape), _whole_spec((1, D_MODEL))],
        out_specs=[row, row, row, row, _whole_spec((1, D_MODEL)), _whole_spec((1, 128))],
        out_shape=[act, jax.ShapeDtypeStruct((nrows, D_MODEL), F32), act, act, jax.ShapeDtypeStruct((1, D_MODEL), F32),
                   jax.ShapeDtypeStruct((1, 128), F32)],
        compiler_params=pltpu.CompilerParams(dimension_semantics=("arbitrary",)),
    )(x1, p, tgt, ple_g, w_pg, w_ple, fg)


def _shift_rows(cur, edge, j, up):
    tm = cur.shape[0]
    row = lax.broadcasted_iota(jnp.int32, cur.shape, 0)
    if up:
        sh = pltpu.roll(cur, tm - j, 0)
        e = jnp.tile(pltpu.roll(edge, 8 - j, 0), (tm // 8, 1))
        return jnp.where(row >= tm - j, e, sh)
    sh = pltpu.roll(cur, j, 0)
    e = jnp.tile(pltpu.roll(edge, j, 0), (tm // 8, 1))
    return jnp.where(row < j, e, sh)


def _conv_pre(cur, prev, w, b):
    acc = cur * w[CONV_K - 1:CONV_K] + b
    for j in range(1, CONV_K):
        acc = acc + _shift_rows(cur, prev, j, up=False) * w[CONV_K - 1 - j:CONV_K - j]
    return acc


def _halo_specs(tm, nrows, cb, before):
    nb = tm // 8
    last = nrows // 8 - 1
    if before:
        return pl.BlockSpec((8, XBC_W), lambda i: (jnp.maximum(i * nb - 1, 0), cb))
    return pl.BlockSpec((8, XBC_W), lambda i: (jnp.minimum((i + 1) * nb, last), cb))


def _conv_fwd(proj, conv_w, conv_b, *, tm, nrows):
    tm = min(tm, nrows)

    def body(cur_ref, prev_ref, w_ref, b_ref, o_ref, pre_ref):
        prev = jnp.where(pl.program_id(0) == 0, 0.0, prev_ref[...].astype(F32))
        pre = _conv_pre(cur_ref[...].astype(F32), prev, w_ref[...], b_ref[...])
        o_ref[...] = jax.nn.silu(pre).astype(o_ref.dtype)
        pre_ref[...] = pre.astype(pre_ref.dtype)

    out = jax.ShapeDtypeStruct((nrows, XBC_W), ACT_DTYPE)
    return pl.pallas_call(
        body, name="conv_fwd", grid=(nrows // tm,),
        in_specs=[_row_spec(tm, XBC_W, XBC_CB), _halo_specs(tm, nrows, XBC_CB, True),
                  _whole_spec((CONV_K, XBC_W)), _whole_spec((1, XBC_W))],
        out_specs=[_row_spec(tm, XBC_W, 0)] * 2, out_shape=[out, out],
        compiler_params=pltpu.CompilerParams(dimension_semantics=("parallel",)),
    )(proj, proj, conv_w, conv_b)


def _conv_bwd_act(pre, dact, *, tm, nrows):
    tm = min(tm, nrows)
    nb = N_GROUPS * N_STATE

    def body(pre_ref, dxs_ref, dbm_ref, dcm_ref, dpre_ref, db_ref):
        pre = pre_ref[...].astype(F32)
        sg = jax.nn.sigmoid(pre)
        dy = jnp.concatenate([dxs_ref[...], dbm_ref[...], dcm_ref[...]], axis=1).astype(F32)
        dpre = dy * sg * (1.0 + pre * (1.0 - sg))
        dpre_ref[...] = dpre.astype(dpre_ref.dtype)

        @pl.when(pl.program_id(0) == 0)
        def _():
            db_ref[...] = jnp.zeros_like(db_ref)

        db_ref[...] += jnp.sum(dpre, axis=0, keepdims=True)

    return pl.pallas_call(
        body, name="conv_bwd_act", grid=(nrows // tm,),
        in_specs=[_row_spec(tm, XBC_W, 0), _row_spec(tm, D_INNER, 0), _row_spec(tm, nb, 0), _row_spec(tm, nb, 0)],
        out_specs=[_row_spec(tm, XBC_W, 0), _whole_spec((1, XBC_W))],
        out_shape=[jax.ShapeDtypeStruct((nrows, XBC_W), ACT_DTYPE), jax.ShapeDtypeStruct((1, XBC_W), F32)],
        compiler_params=pltpu.CompilerParams(dimension_semantics=("arbitrary",)),
    )(pre, *dact)


def _conv_bwd_x(dpre, proj, conv_w, dproj, *, tm, nrows):
    tm = min(tm, nrows)
    ntiles = nrows // tm

    def body(cur_ref, nxt_ref, x_ref, w_ref, _, o_ref, dw_ref):
        cur = cur_ref[...].astype(F32)
        nxt = jnp.where(pl.program_id(0) == ntiles - 1, 0.0, nxt_ref[...].astype(F32))
        x = x_ref[...].astype(F32)
        w = w_ref[...]

        @pl.when(pl.program_id(0) == 0)
        def _():
            dw_ref[...] = jnp.zeros_like(dw_ref)

        acc = cur * w[CONV_K - 1:CONV_K]
        dw_ref[CONV_K - 1:CONV_K, :] += jnp.sum(cur * x, axis=0, keepdims=True)
        for j in range(1, CONV_K):
            u = _shift_rows(cur, nxt, j, up=True)
            acc = acc + u * w[CONV_K - 1 - j:CONV_K - j]
            dw_ref[CONV_K - 1 - j:CONV_K - j, :] += jnp.sum(u * x, axis=0, keepdims=True)
        o_ref[...] = acc.astype(o_ref.dtype)

    return pl.pallas_call(
        body, name="conv_bwd_x", grid=(ntiles,),
        in_specs=[_row_spec(tm, XBC_W, 0), _halo_specs(tm, nrows, 0, False), _row_spec(tm, XBC_W, XBC_CB),
                  _whole_spec((CONV_K, XBC_W)), ANY],
        out_specs=[_row_spec(tm, XBC_W, XBC_CB), _whole_spec((CONV_K, XBC_W))],
        out_shape=[jax.ShapeDtypeStruct(dproj.shape, dproj.dtype), jax.ShapeDtypeStruct((CONV_K, XBC_W), F32)],
        input_output_aliases={4: 0},
        compiler_params=pltpu.CompilerParams(dimension_semantics=("arbitrary",)),
    )(dpre, dpre, proj, conv_w, dproj)


SSD_SPAN = 4
SSD_FWD_SPANS = 2
_XS_GW = D_INNER // N_GROUPS
_NT = (((1,), (1,)), ((), ()))
_TN = (((0,), (0,)), ((), ()))


def _bf16_terms(x, terms):
    parts, rest = [], x
    for _ in range(terms):
        part = rest.astype(jnp.bfloat16)
        parts.append(part)
        rest = rest - part.astype(F32)
    return parts


def _head_lane_matrix():
    return (lax.broadcasted_iota(jnp.int32, (128, _XS_GW), 0)
            == lax.broadcasted_iota(jnp.int32, (128, _XS_GW), 1) // HEAD_DIM).astype(jnp.bfloat16)


@functools.partial(jax.custom_vjp, nondiff_argnums=(1,))
def _head_lanes(cols, terms):
    e = _head_lane_matrix()
    return sum(jnp.dot(t, e, preferred_element_type=F32) for t in _bf16_terms(cols, terms))


def _head_lanes_fwd(cols, terms):
    return _head_lanes(cols, terms), None


def _head_lanes_bwd(terms, _, g):
    e = _head_lane_matrix()
    return (sum(lax.dot_general(t, e, _NT, preferred_element_type=F32) for t in _bf16_terms(g, 2)),)


_head_lanes.defvjp(_head_lanes_fwd, _head_lanes_bwd)


def _ssd_chunk(k, xs, bm, cm, dtr, hprev, dtb, alog, dsk):
    causal, tri, lo = k
    dt = jax.nn.softplus(dtr + dtb)
    da = dt * (-jnp.exp(alog))
    cs = jnp.dot(tri, da, precision=lax.Precision.HIGHEST, preferred_element_type=F32)
    cst = cs.T
    cs_l = _head_lanes(cs, 3)
    xdt = xs * _head_lanes(dt, 2)
    cb = _dot(cm, bm, _NT)
    yd = []
    for q in range(PAIRS_PER_GROUP):
        xq = xdt[:, 128 * q:128 * (q + 1)]
        y2 = [_dot(cb * jnp.exp(jnp.where(causal, cs[:, h:h + 1] - cst[h:h + 1, :], -jnp.inf)), xq)
              for h in (2 * q, 2 * q + 1)]
        yd.append(jnp.where(lo, y2[0], y2[1]))
    y_off = jnp.exp(cs_l) * _dot(cm, hprev, _NT)
    st = _dot(xdt * jnp.exp(cs_l[CHUNK - 1:CHUNK, :] - cs_l), bm, _TN)
    cdec = jnp.exp(cs[CHUNK - 1:CHUNK, :])
    cd_rows = jnp.concatenate(
        [jnp.broadcast_to(cdec[:, h:h + 1], (HEAD_DIM, N_STATE)) for h in range(HEADS_PER_GROUP)], axis=0)
    dsk_l = _head_lanes(jnp.broadcast_to(dsk, (8, 128)), 2)[:1]
    y = jnp.concatenate(yd, axis=1) + y_off + xs * dsk_l
    return y, cd_rows * hprev + st


def _ssd_span(xs, bm, cm, dtr, h0, dtb, alog, dsk):
    li = lax.broadcasted_iota(jnp.int32, (CHUNK, CHUNK), 0)
    si = lax.broadcasted_iota(jnp.int32, (CHUNK, CHUNK), 1)
    causal = li >= si
    k = (causal, causal.astype(F32), si < HEAD_DIM)
    h, ys = h0, []
    for t in range(xs.shape[0] // CHUNK):
        r = slice(t * CHUNK, (t + 1) * CHUNK)
        y, h = _ssd_chunk(k, xs[r], bm[r], cm[r], dtr[r], h, dtb, alog, dsk)
        ys.append(y)
    return (ys[0] if len(ys) == 1 else jnp.concatenate(ys, axis=0)), h


def _ssd_specs(rev, nsteps, rows):
    def s_of(s):
        return nsteps - 1 - s if rev else s

    xs = pl.BlockSpec((rows, _XS_GW), lambda g, s: (s_of(s), g))
    bm = pl.BlockSpec((rows, N_STATE), lambda g, s: (s_of(s), D_INNER // N_STATE + g))
    cm = pl.BlockSpec((rows, N_STATE), lambda g, s: (s_of(s), D_INNER // N_STATE + N_GROUPS + g))
    dt = pl.BlockSpec((rows, 128), lambda g, s: (s_of(s), g))
    par = pl.BlockSpec((1, 128), lambda g, s: (0, g))
    st = pl.BlockSpec((None, None, _XS_GW, N_STATE), lambda g, s: (g, s_of(s), 0, 0))
    return xs, bm, cm, dt, par, st


def _ssd_fwd(act, dtr, dtb, alog, dsk, *, nrows):
    span = CHUNK * min(SSD_SPAN, nrows // CHUNK)
    per_step = SSD_FWD_SPANS if nrows % (SSD_FWD_SPANS * span) == 0 else 1
    rows = per_step * span
    nsteps = nrows // rows
    xs, bm, cm, dt, par, _ = _ssd_specs(False, nsteps, rows)
    st = pl.BlockSpec((None, per_step, _XS_GW, N_STATE), lambda g, s: (g, s, 0, 0))

    def body(xs_ref, b_ref, c_ref, dt_ref, dtb_ref, al_ref, dk_ref, y_ref, st_ref, h_ref):
        @pl.when(pl.program_id(1) == 0)
        def _():
            h_ref[...] = jnp.zeros_like(h_ref)

        h = h_ref[...]
        for i in range(per_step):
            r = slice(i * span, (i + 1) * span)
            st_ref[i] = h
            y, h = _ssd_span(xs_ref[r, :].astype(F32), b_ref[r, :].astype(F32), c_ref[r, :].astype(F32), dt_ref[r, :],
                             h, dtb_ref[...], al_ref[...], dk_ref[...])
            y_ref[r, :] = y.astype(y_ref.dtype)
        h_ref[...] = h

    return pl.pallas_call(
        body, name="ssd_fwd", grid=(N_GROUPS, nsteps),
        in_specs=[xs, bm, cm, dt, par, par, par], out_specs=[xs, st],
        out_shape=[jax.ShapeDtypeStruct((nrows, D_INNER), ACT_DTYPE),
                   jax.ShapeDtypeStruct((N_GROUPS, nrows // span, _XS_GW, N_STATE), F32)],
        scratch_shapes=[pltpu.VMEM((_XS_GW, N_STATE), F32)],
        compiler_params=pltpu.CompilerParams(dimension_semantics=("arbitrary", "arbitrary")),
    )(act, act, act, dtr, dtb, alog, dsk)


def _ssd_bwd(act, dtr, dtb, alog, dsk, states, dy, *, nrows):
    rows = CHUNK * min(SSD_SPAN, nrows // CHUNK)
    nsteps = nrows // rows
    xs, bm, cm, dt, par, st = _ssd_specs(True, nsteps, rows)

    def body(xs_ref, b_ref, c_ref, dt_ref, dtb_ref, al_ref, dk_ref, st_ref, dy_ref,
             dxs_ref, db_ref, dc_ref, ddt_ref, ddtb_ref, dal_ref, ddk_ref, dh_ref):
        @pl.when(pl.program_id(1) == 0)
        def _():
            dh_ref[...] = jnp.zeros_like(dh_ref)
            ddtb_ref[...] = jnp.zeros_like(ddtb_ref)
            dal_ref[...] = jnp.zeros_like(dal_ref)
            ddk_ref[...] = jnp.zeros_like(ddk_ref)

        _, vjp = jax.vjp(_ssd_span, xs_ref[...].astype(F32), b_ref[...].astype(F32), c_ref[...].astype(F32),
                         dt_ref[...], st_ref[...], dtb_ref[...], al_ref[...], dk_ref[...])
        dxs, db, dc, ddt, dh, ddtb, dal, ddk = vjp((dy_ref[...].astype(F32), dh_ref[...]))
        dxs_ref[...] = dxs.astype(dxs_ref.dtype)
        db_ref[...] = db.astype(db_ref.dtype)
        dc_ref[...] = dc.astype(dc_ref.dtype)
        ddt_ref[...] = ddt
        dh_ref[...] = dh
        ddtb_ref[...] += ddtb
        dal_ref[...] += dal
        ddk_ref[...] += ddk

    nb = N_GROUPS * N_STATE
    bspec = pl.BlockSpec((rows, N_STATE), lambda g, s: (nsteps - 1 - s, g))
    return pl.pallas_call(
        body, name="ssd_bwd", grid=(N_GROUPS, nsteps),
        in_specs=[xs, bm, cm, dt, par, par, par, st, xs],
        out_specs=[xs, bspec, bspec, dt, par, par, par],
        out_shape=[jax.ShapeDtypeStruct((nrows, D_INNER), ACT_DTYPE), jax.ShapeDtypeStruct((nrows, nb), ACT_DTYPE),
                   jax.ShapeDtypeStruct((nrows, nb), ACT_DTYPE), jax.ShapeDtypeStruct((nrows, DT_W), F32),
                   jax.ShapeDtypeStruct((1, DT_W), F32), jax.ShapeDtypeStruct((1, DT_W), F32),
                   jax.ShapeDtypeStruct((1, DT_W), F32)],
        scratch_shapes=[pltpu.VMEM((_XS_GW, N_STATE), F32)],
        compiler_params=pltpu.CompilerParams(dimension_semantics=("arbitrary", "arbitrary")),
    )(act, act, act, dtr, dtb, alog, dsk, states, dy)


def _add_epilogue(acc, r):
    return r + acc


def _rms_and_skip(x, g):
    return _rms(x, g), x


def _forward_backward(x, p, tgt, w, late_weights=None, after=()):
    s = x.shape[0]
    act_t, f32 = ACT_DTYPE, F32
    mm = functools.partial(_matmul)
    h, proj, dtr = _rows_matmul("proj", _f_rms, [(x, D_MODEL, 0)], [w["norm_g"]], w["w_main"], out_dtype=act_t,
                                n=MAIN_W, k=D_MODEL, tm=1024, tn=2 * MM_TILE, nrows=s, after=after, side=(w["w_dt"], f32))
    act, conv_pre = _conv_fwd(proj, w["conv_w"], w["conv_b"], tm=512, nrows=s)
    y, states = _ssd_fwd(act, dtr, w["dt_bias"], w["a_log"], w["d_skip"], nrows=s)
    if late_weights is not None:
        w = {**w, **late_weights(states)}
    a_pars = [w["ln_a_g"], w["ln_a_b"], w["w_s"], w["b_s"]]
    y_a, o_a = _rows_matmul("out_a", _f_branch_a, [(proj, UVZ_W, UVZ_CB)], a_pars, w["w_oa"], out_dtype=act_t,
                            n=D_MODEL, k=E_A, tm=512, nrows=s)
    gn_rows = [(y, D_INNER, 0), (proj, ZB_W, ZB_CB)]
    y_b, o_b = _rows_matmul("out_b", _f_gnorm, gn_rows, [w["ssm_norm_g"]], w["w_ob"], out_dtype=act_t,
                            n=D_MODEL, k=D_INNER, tm=1024, nrows=s)
    mg_rows = [(proj, G_W, G_CB), (o_a, D_MODEL, 0), (o_b, D_MODEL, 0)]
    merged, x1 = _rows_matmul("out_proj", _f_merge, mg_rows, [], w["w_out"], out_dtype=f32, n=D_MODEL, k=D_MODEL,
                              tm=1024, nrows=s, extras=(x,), epilogue=_add_epilogue)
    g = {}
    hp, dx2, dgp, dpe, g["final_g"], loss = _head(x1, p, tgt, w["ple_norm_g"], w["w_pg"], w["w_ple"], w["final_g"],
                                                   tm=512, nrows=s)
    g["w_pg"] = mm(hp, dgp, mode="tn", name="d_w_pg", out_dtype=f32, m=D_MODEL, n=D_MODEL, k=s)
    g["w_ple"] = mm(p, dpe, mode="tn", name="d_w_ple", out_dtype=f32, m=PLE_DIM, n=D_MODEL, k=s)
    dx1, g["ple_norm_g"] = _rows_vjp_call(
        "ple_norm_bwd", _rms_and_skip, [(x1, D_MODEL, 0)], [w["ple_norm_g"]], [(dx2, D_MODEL, 0)],
        [(f32, None)], tm=1024, nrows=s, cot_mm=(dgp, w["w_pg"], None))
    g["w_out"] = mm(merged, dx1, mode="tn", name="d_w_out", out_dtype=f32, m=D_MODEL, n=D_MODEL, k=s)
    dproj, do_a, do_b = _rows_vjp_call(
        "merge_bwd", _f_merge, mg_rows, [], [],
        [(act_t, (None, MAIN_W, G_CB)), (act_t, None), (act_t, None)], tm=1024, nrows=s, cot_mm=(dx1, w["w_out"], None))
    g["w_oa"] = mm(y_a, do_a, mode="tn", name="d_w_oa", out_dtype=f32, m=E_A, n=D_MODEL, k=s)
    g["w_ob"] = mm(y_b, do_b, mode="tn", name="d_w_ob", out_dtype=f32, m=D_INNER, n=D_MODEL, k=s)
    dy, dproj, g["ssm_norm_g"] = _rows_vjp_call(
        "gnorm_bwd", _f_gnorm, gn_rows, [w["ssm_norm_g"]], [],
        [(act_t, None), (act_t, (dproj, MAIN_W, ZB_CB))], tm=512, nrows=s, cot_mm=(do_b, w["w_ob"], None))
    dxs, dbm, dcm, ddtr, g["dt_bias"], g["a_log"], g["d_skip"] = _ssd_bwd(
        act, dtr, w["dt_bias"], w["a_log"], w["d_skip"], states, dy, nrows=s)
    dpre, g["conv_b"] = _conv_bwd_act(conv_pre, (dxs, dbm, dcm), tm=1024, nrows=s)
    dproj, g["conv_w"] = _conv_bwd_x(dpre, proj, w["conv_w"], dproj, tm=512, nrows=s)
    dproj, g["ln_a_g"], g["ln_a_b"], g["w_s"], g["b_s"] = _rows_vjp_call(
        "branch_a_bwd", _f_branch_a, [(proj, UVZ_W, UVZ_CB)], a_pars, [],
        [(act_t, (dproj, MAIN_W, UVZ_CB))], tm=512, nrows=s, cot_mm=(do_a, w["w_oa"], None))
    g["w_main"] = mm(h, dproj, mode="tn", name="d_w_main", out_dtype=f32, m=D_MODEL, n=MAIN_W, k=s, tn=MM_TILE // 2)
    g["w_dt"] = mm(h, ddtr, mode="tn", name="d_w_dt", out_dtype=f32, m=D_MODEL, n=DT_W, k=s)
    return loss, g, (dproj, ddtr, dx1)


def _input_grad(x, w, ctx, after=()):
    dproj, ddtr, dx1 = ctx
    s = x.shape[0]
    dh = _matmul(dproj, w["w_main"], mode="nt", name="d_h_main", out_dtype=F32, m=s, n=D_MODEL, k=MAIN_W,
                 tm=MM_TILE // 2, after=after)
    return _rows_vjp_call(
        "pre_norm_bwd", _rms_and_skip, [(x, D_MODEL, 0)], [w["norm_g"]], [(dx1, D_MODEL, 0)],
        [(F32, None)], tm=1024, nrows=s, cot_mm=(ddtr, w["w_dt"], dh))


def _local_step(x, p, tgt, w):
    loss, g, ctx = _forward_backward(x, p, tgt, w)
    grad_x, g["norm_g"] = _input_grad(x, w, ctx)
    return loss, grad_x, g


_O_ZB = 3 * E_A
_O_XBC = _O_ZB + D_INNER
_O_DT = _O_XBC + CONV_DIM
_O_G = _O_DT + N_HEADS


def _heads_to_lanes(v):
    r = v.shape[0]
    v = v.reshape(r, N_GROUPS, HEADS_PER_GROUP)
    return jnp.pad(v, ((0, 0), (0, 0), (0, 128 - HEADS_PER_GROUP))).reshape(r, DT_W)


def _lanes_to_heads(v):
    r = v.shape[0]
    return v.reshape(r, N_GROUPS, 128)[:, :, :HEADS_PER_GROUP].reshape(r, N_HEADS)


def _block_cols(blocks, a, b):
    parts = []
    for k in range(N_CHIPS):
        lo, hi = max(a, k * W_IN_BLOCK), min(b, (k + 1) * W_IN_BLOCK)
        if lo < hi:
            parts.append(blocks[k][:, lo - k * W_IN_BLOCK:hi - k * W_IN_BLOCK])
    return parts


_W_IN_SEGMENTS = ((0, _O_ZB, "m", 0), (_O_ZB, _O_XBC, "m", UVZ_W + XBC_W), (_O_XBC, _O_DT, "m", UVZ_W),
                  (_O_DT, _O_G, "d", 0), (_O_G, N_IN, "m", MAIN_W - G_W))


def _w_in_grad_blocks(gm, gdt):
    blocks = []
    for k in range(N_CHIPS):
        a, b = k * W_IN_BLOCK, (k + 1) * W_IN_BLOCK
        parts = []
        for s, e, src, off in _W_IN_SEGMENTS:
            lo, hi = max(a, s), min(b, e)
            if lo < hi:
                parts.append((gm if src == "m" else gdt)[:, off + lo - s:off + hi - s])
        blocks.append(jnp.concatenate(parts, axis=1))
    return jnp.stack(blocks)


def _layout_weights(f, w_in_blocks=None):
    w = dict(f)
    if w_in_blocks is None:
        w_in = w.pop("w_in")
        w_in_blocks = jnp.stack([w_in[:, k * W_IN_BLOCK:(k + 1) * W_IN_BLOCK] for k in range(N_CHIPS)])
    cols = functools.partial(_block_cols, w_in_blocks)
    w["w_main"] = jnp.concatenate(cols(0, _O_ZB) + cols(_O_XBC, _O_DT) + cols(_O_ZB, _O_XBC) + cols(_O_G, N_IN), axis=1)
    w["w_dt"] = _heads_to_lanes(jnp.concatenate(cols(_O_DT, _O_G), axis=1))
    w["b_s"] = f["b_s"].reshape(G_A, CHUNK, 1)
    for n in ("dt_bias", "a_log", "d_skip"):
        w[n] = _heads_to_lanes(f[n])
    return w


def _natural_grads(g):
    out = dict(g)
    gm = out.pop("w_main")
    gdt = _lanes_to_heads(out.pop("w_dt"))
    out["w_in"] = jnp.concatenate(
        [gm[:, :UVZ_W], gm[:, UVZ_W + XBC_W:UVZ_W + XBC_W + ZB_W], gm[:, UVZ_W:UVZ_W + XBC_W], gdt, gm[:, MAIN_W - G_W:]],
        axis=1)
    out["b_s"] = g["b_s"].reshape(G_A, CHUNK)
    for n in ("dt_bias", "a_log", "d_skip"):
        out[n] = _lanes_to_heads(g[n])
    return out


def _place():
    return lax.axis_index("x"), lax.axis_index("y"), lax.axis_index("c")


def _other_chips(x, y):
    return [(1 - x, y), (x, 1 - y), (1 - x, 1 - y)]


def _rcopy(src, dst, ssem, rsem, dev):
    return pltpu.make_async_remote_copy(src_ref=src, dst_ref=dst, send_sem=ssem, recv_sem=rsem,
                                        device_id=dev, device_id_type=MESH)


def _half(ref_rows, half):
    hs = ref_rows // 2
    return pl.ds(pl.multiple_of(half * hs, 16), hs)


GATHER_CHUNKS = 4


def _gather_weights(shards, conv_shard):
    nw, nq = len(shards), GATHER_CHUNKS

    def body(*refs):
        sh, cv = refs[:nw], refs[nw]
        out, cvo = refs[nw + 1:2 * nw + 1], refs[2 * nw + 1]
        ici_s, ici_r, fw_s, fw_r, own_s, own_r, cv_s, cv_r = refs[2 * nw + 2:]
        x, y, c = _place()
        me, sib, chips = 2 * x + y, (x, y, 1 - c), _other_chips(x, y)
        own = [_rcopy(sh[w], out[w].at[me], own_s.at[w], own_r.at[w], sib) for w in range(nw)]
        own.append(_rcopy(cv, cvo.at[me], own_s.at[nw], own_r.at[nw], sib))
        for cp in own:
            cp.start()
        pieces = [(w, q, j, chip) for w in range(nw) for q in range(nq) for j, chip in enumerate(chips)]

        def rows(w, half, q):
            hs = sh[w].shape[0] // 2
            return pl.ds(pl.multiple_of(half * hs + q * (hs // nq), 16), hs // nq)

        def sem(w, q, j):
            return (3 * w + j) * nq + q

        sends = [_rcopy(sh[w].at[rows(w, c, q)], out[w].at[me, rows(w, c, q)], ici_s.at[sem(w, q, j)],
                        ici_r.at[sem(w, q, j)], (*chip, c)) for w, q, j, chip in pieces]
        for j, chip in enumerate(chips):
            sends.append(_rcopy(cv, cvo.at[me], cv_s.at[j], cv_r.at[j], (*chip, c)))
        for cp in sends:
            cp.start()
        for w, q, j, chip in pieces:
            slab = out[w].at[2 * chip[0] + chip[1], rows(w, c, q)]
            _rcopy(slab, slab, ici_s.at[sem(w, q, j)], ici_r.at[sem(w, q, j)], (*chip, c)).wait_recv()
            fwd = _rcopy(slab, slab, fw_s.at[sem(w, q, j)], fw_r.at[sem(w, q, j)], sib)
            fwd.start()
            sends.append(fwd)
        for j, chip in enumerate(chips):
            blk = cvo.at[2 * chip[0] + chip[1]]
            _rcopy(blk, blk, cv_s.at[j], cv_r.at[j], (*chip, c)).wait_recv()
        for w, q, j, chip in pieces:
            slab = out[w].at[2 * chip[0] + chip[1], rows(w, 1 - c, q)]
            _rcopy(slab, slab, fw_s.at[sem(w, q, j)], fw_r.at[sem(w, q, j)], sib).wait_recv()
        for cp in sends:
            cp.wait_send()
        for cp in own:
            cp.wait()

    dma = pltpu.SemaphoreType.DMA
    n_ici = 3 * nw * nq
    return pl.pallas_call(
        body, name="gather_weights",
        in_specs=[ANY] * (nw + 1), out_specs=[ANY] * (nw + 1),
        out_shape=[jax.ShapeDtypeStruct((N_CHIPS,) + s.shape, s.dtype) for s in shards]
        + [jax.ShapeDtypeStruct((N_CHIPS,) + conv_shard.shape, conv_shard.dtype)],
        scratch_shapes=[dma((n_ici,)), dma((n_ici,)), dma((n_ici,)), dma((n_ici,)), dma((nw + 1,)), dma((nw + 1,)),
                        dma((3,)), dma((3,))],
    )(*shards, conv_shard)


_HBM = pl.BlockSpec(memory_space=pltpu.HBM)
_SEM = pl.BlockSpec(memory_space=pltpu.SEMAPHORE)
_EFFECT = pltpu.SideEffectType.DATAFLOW_SIDE_EFFECTING


def _late_gather_copies(sh, out, s_sem, r_sem):
    x, y, c = _place()
    to = [(*chip, c) for chip in _other_chips(x, y)] + [(x, y, 1 - c)]
    return [_rcopy(sh[w], out[w].at[2 * x + y], s_sem.at[4 * w + j], r_sem.at[4 * w + j], dev)
            for w in range(len(sh)) for j, dev in enumerate(to)]


def _late_gather_start(shards):
    n = len(shards)
    lands = [lax.empty((N_CHIPS,) + a.shape, a.dtype) for a in shards]

    def body(*refs):
        for cp in _late_gather_copies(refs[:n], refs[n:2 * n], refs[2 * n], refs[2 * n + 1]):
            cp.start()
        refs[-1][...] = jnp.zeros_like(refs[-1])

    dma = pltpu.SemaphoreType.DMA
    hbm = [pltpu.with_memory_space_constraint(a, pltpu.HBM) for a in list(shards) + lands]
    out = pl.pallas_call(
        body, name="late_gather_start",
        out_shape=[dma((4 * n,)), dma((4 * n,))] + [pltpu.HBM(a.shape, a.dtype) for a in hbm]
        + [jax.ShapeDtypeStruct((8, 128), F32)],
        in_specs=[_HBM] * (2 * n), out_specs=[_SEM, _SEM] + [_HBM] * (2 * n) + [pl.BlockSpec(memory_space=pltpu.VMEM)],
        input_output_aliases={i: 2 + i for i in range(2 * n)},
        compiler_params=pltpu.CompilerParams(has_side_effects=_EFFECT),
    )(*hbm)
    return out[0], out[1], out[2:2 + n], out[2 + n:2 + 2 * n], out[-1]


def _late_gather_wait(s_sem, r_sem, srcs, lands, after):
    n = len(srcs)

    def body(*refs):
        for cp in _late_gather_copies(refs[:n], refs[n:2 * n], refs[2 * n], refs[2 * n + 1]):
            cp.wait_send()
            cp.wait_recv()

    out = pl.pallas_call(
        body, name="late_gather_wait",
        out_shape=[pltpu.HBM(a.shape, a.dtype) for a in list(srcs) + list(lands)],
        in_specs=[_HBM] * (2 * n) + [_SEM, _SEM, ANY], out_specs=[_HBM] * (2 * n),
        input_output_aliases={i: i for i in range(2 * n)},
        compiler_params=pltpu.CompilerParams(has_side_effects=_EFFECT),
    )(*srcs, *lands, s_sem, r_sem, after)
    return out[n:]


def _swap_with_sibling(arrs):
    n = len(arrs)

    def body(*refs):
        src, dst, s_sem, r_sem = refs[:n], refs[n:2 * n], refs[2 * n], refs[2 * n + 1]
        x, y, c = _place()
        cps = [_rcopy(src[i], dst[i], s_sem.at[i], r_sem.at[i], (x, y, 1 - c)) for i in range(n)]
        for cp in cps:
            cp.start()
        for cp in cps:
            cp.wait()

    dma = pltpu.SemaphoreType.DMA
    return pl.pallas_call(
        body, name="swap_with_sibling", in_specs=[ANY] * n, out_specs=[ANY] * n,
        out_shape=[jax.ShapeDtypeStruct(a.shape, a.dtype) for a in arrs], scratch_shapes=[dma((n,)), dma((n,))],
    )(*arrs)


def _scatter_copies(src, land, s_sem, r_sem):
    x, y, c = _place()
    return [_rcopy(src[i].at[2 * chip[0] + chip[1]], land[i].at[j], s_sem.at[3 * i + j], r_sem.at[3 * i + j], (*chip, c))
            for i in range(len(src)) for j, chip in enumerate(_other_chips(x, y))]


def _scatter_blocks_start(arrs):
    n = len(arrs)
    lands = [lax.empty((3,) + a.shape[1:], a.dtype) for a in arrs]

    def body(*refs):
        src, land, s_sem, r_sem, token = refs[:n], refs[n:2 * n], refs[2 * n], refs[2 * n + 1], refs[-1]
        for cp in _scatter_copies(src, land, s_sem, r_sem):
            cp.start()
        token[...] = jnp.zeros_like(token)

    dma = pltpu.SemaphoreType.DMA
    hbm = [pltpu.with_memory_space_constraint(a, pltpu.HBM) for a in list(arrs) + lands]
    out = pl.pallas_call(
        body, name="scatter_blocks_start",
        out_shape=[dma((3 * n,)), dma((3 * n,))] + [pltpu.HBM(a.shape, a.dtype) for a in hbm]
        + [jax.ShapeDtypeStruct((8, 128), F32)],
        in_specs=[_HBM] * (2 * n), out_specs=[_SEM, _SEM] + [_HBM] * (2 * n) + [pl.BlockSpec(memory_space=pltpu.VMEM)],
        input_output_aliases={i: 2 + i for i in range(2 * n)},
        compiler_params=pltpu.CompilerParams(has_side_effects=_EFFECT),
    )(*hbm)
    return out[0], out[1], out[2:2 + n], out[2 + n:2 + 2 * n], out[-1]


def _scatter_blocks_wait(s_sem, r_sem, srcs, lands, after):
    n = len(srcs)

    def body(*refs):
        src, land, s_sem, r_sem = refs[:n], refs[n:2 * n], refs[2 * n], refs[2 * n + 1]
        for cp in _scatter_copies(src, land, s_sem, r_sem):
            cp.wait_send()
            cp.wait_recv()

    out = pl.pallas_call(
        body, name="scatter_blocks_wait",
        out_shape=[pltpu.HBM(a.shape, a.dtype) for a in list(srcs) + list(lands)],
        in_specs=[_HBM] * (2 * n) + [_SEM, _SEM] + [ANY] * len(after), out_specs=[_HBM] * (2 * n),
        input_output_aliases={i: i for i in range(2 * n)},
        compiler_params=pltpu.CompilerParams(has_side_effects=_EFFECT),
    )(*srcs, *lands, s_sem, r_sem, *after)
    return out[:n], out[n:]


def _share_halves(arrs):
    n = len(arrs)

    def body(*refs):
        buf, s_sem, r_sem = refs[n:2 * n], refs[2 * n], refs[2 * n + 1]
        x, y, c = _place()
        cps = []
        for i in range(n):
            mine = buf[i].at[_half(buf[i].shape[0], c)]
            cps.append(_rcopy(mine, mine, s_sem.at[i], r_sem.at[i], (x, y, 1 - c)))
        for cp in cps:
            cp.start()
        for i in range(n):
            theirs = buf[i].at[_half(buf[i].shape[0], 1 - c)]
            _rcopy(theirs, theirs, s_sem.at[i], r_sem.at[i], (x, y, 1 - c)).wait_recv()
        for cp in cps:
            cp.wait_send()

    dma = pltpu.SemaphoreType.DMA
    return pl.pallas_call(
        body, name="share_halves", in_specs=[ANY] * n, out_specs=[ANY] * n,
        out_shape=[jax.ShapeDtypeStruct(a.shape, a.dtype) for a in arrs],
        input_output_aliases={i: i for i in range(n)}, scratch_shapes=[dma((n,)), dma((n,))],
    )(*arrs)


def _small_gather_copies(src, land, s_sem, r_sem):
    x, y, c = _place()
    cps = []
    for d in range(1, N_DEV):
        peer = ((1 - x) if d & 4 else x), ((1 - y) if d & 2 else y), ((1 - c) if d & 1 else c)
        cps.append(_rcopy(src, land.at[4 * x + 2 * y + c], s_sem.at[d - 1], r_sem.at[d - 1], peer))
    return cps


def _small_gather_start(packed):
    def body(src, land, s_sem, r_sem, _, __, token):
        for cp in _small_gather_copies(src, land, s_sem, r_sem):
            cp.start()
        token[...] = jnp.zeros_like(token)

    dma = pltpu.SemaphoreType.DMA
    hbm = [pltpu.with_memory_space_constraint(a, pltpu.HBM) for a in (packed, lax.empty((N_DEV,) + packed.shape, F32))]
    return pl.pallas_call(
        body, name="small_gather_start",
        out_shape=[dma((N_DEV - 1,)), dma((N_DEV - 1,))] + [pltpu.HBM(a.shape, a.dtype) for a in hbm]
        + [jax.ShapeDtypeStruct((8, 128), F32)],
        in_specs=[_HBM] * 2, out_specs=[_SEM, _SEM, _HBM, _HBM, pl.BlockSpec(memory_space=pltpu.VMEM)],
        input_output_aliases={0: 2, 1: 3}, compiler_params=pltpu.CompilerParams(has_side_effects=_EFFECT),
    )(*hbm)


def _small_gather_wait(s_sem, r_sem, src, land, after):
    def body(src, land, s_sem, r_sem, *_):
        for cp in _small_gather_copies(src, land, s_sem, r_sem):
            cp.wait_send()
            cp.wait_recv()

    return pl.pallas_call(
        body, name="small_gather_wait", out_shape=[pltpu.HBM(src.shape, src.dtype), pltpu.HBM(land.shape, land.dtype)],
        in_specs=[_HBM, _HBM, _SEM, _SEM, ANY], out_specs=[_HBM, _HBM], input_output_aliases={0: 0, 1: 1},
        compiler_params=pltpu.CompilerParams(has_side_effects=_EFFECT),
    )(src, land, s_sem, r_sem, after)


def _small_sum(own, land, dev_arr):
    def body(me_ref, own_ref, land_ref, o_ref):
        acc = jnp.zeros(o_ref.shape, F32)
        for d in range(N_DEV):
            acc = acc + jnp.where(me_ref[0] == d, own_ref[...], land_ref[d])
        o_ref[...] = acc

    return pl.pallas_call(
        body, name="small_sum", out_shape=jax.ShapeDtypeStruct(own.shape, F32),
        grid_spec=pltpu.PrefetchScalarGridSpec(
            num_scalar_prefetch=1, grid=(1,),
            in_specs=[pl.BlockSpec(own.shape, lambda i, m: (0, 0)), pl.BlockSpec(land.shape, lambda i, m: (0, 0, 0))],
            out_specs=pl.BlockSpec(own.shape, lambda i, m: (0, 0))),
    )(dev_arr, own, land)


def _row_tile(rows, cols):
    tr = max(8, min(rows, (1 << 20) // (4 * cols) // 8 * 8))
    while rows % tr:
        tr -= 8
    return tr


def _chip_sum(name, g5, recv, c_arr):
    nb, _, hs, cols = g5.shape
    tr = _row_tile(hs, cols)

    def body(_, a_ref, b_ref, o_ref):
        o_ref[...] = (a_ref[...] + b_ref[...].astype(F32)).astype(o_ref.dtype)

    blk = pl.BlockSpec((None, tr, cols), lambda b, i, c: (b, i, 0))
    return pl.pallas_call(
        body, name=name,
        grid_spec=pltpu.PrefetchScalarGridSpec(
            num_scalar_prefetch=1, grid=(nb, hs // tr),
            in_specs=[pl.BlockSpec((None, None, tr, cols), lambda b, i, c: (b, c[0], i, 0)), blk], out_specs=blk),
        out_shape=jax.ShapeDtypeStruct((nb, hs, cols), WIRE_DTYPE),
    )(c_arr, g5, recv)


def _final_sum(name, own, recv, place_arr):
    _, hs, cols = own.shape
    tr = _row_tile(hs, cols)
    nt = hs // tr

    def body(_, a_ref, r_ref, o_ref):
        o_ref[...] = ((a_ref[...].astype(F32) + r_ref[0].astype(F32)) + r_ref[1].astype(F32)) + r_ref[2].astype(F32)

    return pl.pallas_call(
        body, name=name,
        grid_spec=pltpu.PrefetchScalarGridSpec(
            num_scalar_prefetch=1, grid=(nt,),
            in_specs=[pl.BlockSpec((None, tr, cols), lambda i, m: (m[0], i, 0)),
                      pl.BlockSpec((3, tr, cols), lambda i, m: (0, i, 0))],
            out_specs=pl.BlockSpec((tr, cols), lambda i, m: (m[1] * nt + i, 0))),
        out_shape=jax.ShapeDtypeStruct((2 * hs, cols), F32),
    )(place_arr, own, recv)


def _adamw(w, g, m, v):
    m = ADAM_B1 * m + (1.0 - ADAM_B1) * g
    v = ADAM_B2 * v + (1.0 - ADAM_B2) * (g * g)
    m_hat = m / (1.0 - ADAM_B1 ** ADAM_STEP)
    v_hat = v / (1.0 - ADAM_B2 ** ADAM_STEP)
    return -ADAM_LR * (m_hat / (jnp.sqrt(v_hat) + ADAM_EPS) + ADAM_WD * w), m, v


def _adamw_call(name, w, g, m, v):
    rows, cols = w.shape
    tr = _row_tile(rows, cols)
    if 4 * tr * cols >= (1 << 18):
        blk, steps = pl.BlockSpec((tr, cols), lambda i: (i, 0)), rows // tr
    else:
        blk, steps = pl.BlockSpec((rows, 128), lambda i: (0, i)), cols // 128

    def body(w_ref, g_ref, m_ref, v_ref, d_ref, nm_ref, nv_ref, go_ref):
        g = g_ref[...]
        d_ref[...], nm_ref[...], nv_ref[...] = _adamw(w_ref[...], g, m_ref[...], v_ref[...])
        go_ref[...] = g

    return pl.pallas_call(
        body, name=name, grid=(steps,), in_specs=[blk] * 4, out_specs=[blk] * 4,
        out_shape=[jax.ShapeDtypeStruct(w.shape, F32)] * 4,
        compiler_params=pltpu.CompilerParams(dimension_semantics=("parallel",)),
    )(w, g, m, v)


def _adamw_small(ws, gs, ms, vs):
    n = len(ws)

    def body(*refs):
        for i in range(n):
            w_ref, g_ref, m_ref, v_ref = (refs[k * n + i] for k in range(4))
            d, nm, nv = _adamw(w_ref[...], g_ref[...], m_ref[...], v_ref[...])
            refs[4 * n + i][...] = d
            refs[5 * n + i][...] = nm
            refs[6 * n + i][...] = nv

    out = pl.pallas_call(
        body, name="adamw_small", out_shape=[jax.ShapeDtypeStruct(a.shape, F32) for a in ws] * 3,
    )(*ws, *gs, *ms, *vs)
    return out[:n], out[n:2 * n], out[2 * n:]


_BIG = ("w_in", "w_oa", "w_ob", "w_out", "w_pg", "w_ple")
_SMALL = ("norm_g", "ln_a_g", "ln_a_b", "w_s", "b_s", "conv_w", "conv_b", "dt_bias", "a_log", "d_skip", "ssm_norm_g",
          "ple_norm_g", "final_g")
_WEIGHTS = ("norm_g", "w_in", "ln_a_g", "ln_a_b", "w_s", "b_s", "conv_w", "conv_b", "dt_bias", "a_log", "d_skip",
            "ssm_norm_g", "w_oa", "w_ob", "w_out", "ple_norm_g", "w_pg", "w_ple", "final_g")
_COL_SHARDED = ("w_in", "w_ple")
_PACK = 1024


def _blocks_to_full(col_sharded, blocks):
    if col_sharded:
        return jnp.concatenate([blocks[k] for k in range(N_CHIPS)], axis=1)
    return blocks.reshape(N_CHIPS * blocks.shape[1], blocks.shape[2])


def _full_to_blocks(col_sharded, full):
    if col_sharded:
        w = full.shape[1] // N_CHIPS
        return jnp.stack([full[:, k * w:(k + 1) * w] for k in range(N_CHIPS)])
    return full.reshape(N_CHIPS, full.shape[0] // N_CHIPS, full.shape[1])


def _two_d(n, a):
    if n == "w_s":
        return a.reshape(G_A * CHUNK, CHUNK)
    if n in ("b_s", "conv_w"):
        return a.reshape(a.shape[-2], a.shape[-1])
    return a.reshape(1, a.shape[-1])


def kernel(x, p, norm_g, w_in, ln_a_g, ln_a_b, w_s, b_s, conv_w, conv_b, dt_bias, a_log, d_skip, ssm_norm_g, w_oa, w_ob, w_out, ple_norm_g, w_pg, w_ple, final_g, loss_target, m_norm_g, m_w_in, m_ln_a_g, m_ln_a_b, m_w_s, m_b_s, m_conv_w, m_conv_b, m_dt_bias, m_a_log, m_d_skip, m_ssm_norm_g, m_w_oa, m_w_ob, m_w_out, m_ple_norm_g, m_w_pg, m_w_ple, m_final_g, v_norm_g, v_w_in, v_ln_a_g, v_ln_a_b, v_w_s, v_b_s, v_conv_w, v_conv_b, v_dt_bias, v_a_log, v_d_skip, v_ssm_norm_g, v_w_oa, v_w_ob, v_w_out, v_ple_norm_g, v_w_pg, v_w_ple, v_final_g):
    wt = dict(norm_g=norm_g, w_in=w_in, ln_a_g=ln_a_g, ln_a_b=ln_a_b, w_s=w_s, b_s=b_s, conv_w=conv_w, conv_b=conv_b,
              dt_bias=dt_bias, a_log=a_log, d_skip=d_skip, ssm_norm_g=ssm_norm_g, w_oa=w_oa, w_ob=w_ob, w_out=w_out,
              ple_norm_g=ple_norm_g, w_pg=w_pg, w_ple=w_ple, final_g=final_g)
    mom = dict(norm_g=m_norm_g, w_in=m_w_in, ln_a_g=m_ln_a_g, ln_a_b=m_ln_a_b, w_s=m_w_s, b_s=m_b_s, conv_w=m_conv_w,
               conv_b=m_conv_b, dt_bias=m_dt_bias, a_log=m_a_log, d_skip=m_d_skip, ssm_norm_g=m_ssm_norm_g, w_oa=m_w_oa,
               w_ob=m_w_ob, w_out=m_w_out, ple_norm_g=m_ple_norm_g, w_pg=m_w_pg, w_ple=m_w_ple, final_g=m_final_g)
    vel = dict(norm_g=v_norm_g, w_in=v_w_in, ln_a_g=v_ln_a_g, ln_a_b=v_ln_a_b, w_s=v_w_s, b_s=v_b_s, conv_w=v_conv_w,
               conv_b=v_conv_b, dt_bias=v_dt_bias, a_log=v_a_log, d_skip=v_d_skip, ssm_norm_g=v_ssm_norm_g, w_oa=v_w_oa,
               w_ob=v_w_ob, w_out=v_w_out, ple_norm_g=v_ple_norm_g, w_pg=v_w_pg, w_ple=v_w_ple, final_g=v_final_g)
    xi, yi, ci = _place()
    me = 2 * xi + yi
    c_arr = jnp.reshape(ci, (1,)).astype(jnp.int32)
    place_arr = jnp.stack([me, ci]).astype(jnp.int32)

    shard = {n: wt[n][0] for n in _BIG}
    wire = {n: shard[n].astype(WIRE_DTYPE) for n in _BIG}
    w_in_blocks, conv_blocks = _gather_weights([wire["w_in"]], conv_w[0])
    g_ssem, g_rsem, g_sent, g_lands, g_token = _late_gather_start([wire[n] for n in _BIG[1:]])
    full = {"conv_w": _blocks_to_full(True, conv_blocks)}
    for n in _SMALL:
        if n != "conv_w":
            full[n] = wt[n][0] if wt[n].ndim > 2 else wt[n].reshape(1, wt[n].shape[-1])

    def late_weights(after):
        blocks = _late_gather_wait(g_ssem, g_rsem, g_sent, g_lands, after)
        return {n: _blocks_to_full(n in _COL_SHARDED, b) for n, b in zip(_BIG[1:], blocks)}

    w = _layout_weights(full, w_in_blocks=w_in_blocks)
    loss_row, g, ctx = _forward_backward(x[0], p[0, 0], loss_target[0], w, late_weights, after=(g_token,))
    loss = lax.psum(loss_row[0, 0], ("x", "y", "c"))

    parts = {n: _full_to_blocks(n in _COL_SHARDED, g[n]) for n in _BIG[1:]}
    parts["w_main"] = g["w_main"][None]
    parts["w_dt"] = jnp.pad(_lanes_to_heads(g["w_dt"]), ((0, 0), (0, 128 - N_HEADS)))[None]
    names = ("w_main", "w_dt") + _BIG[1:]
    g5 = {n: parts[n].reshape(parts[n].shape[0], 2, parts[n].shape[1] // 2, parts[n].shape[2]) for n in names}
    to_sibling = [lax.dynamic_index_in_dim(g5[n], 1 - ci, axis=1, keepdims=False).astype(WIRE_DTYPE) for n in names]
    from_sibling = _swap_with_sibling(to_sibling)
    chip = {n: _chip_sum("chip_sum_" + n, g5[n], r, c_arr) for n, r in zip(names, from_sibling)}
    chip["w_in"] = _w_in_grad_blocks(chip["w_main"][0], chip["w_dt"][0])
    chip_wire = [chip[n] for n in _BIG]
    s_sem, r_sem, sent, lands, token = _scatter_blocks_start(chip_wire)
    grad_x, g["norm_g"] = _input_grad(x[0], w, ctx, after=(token,))
    g = _natural_grads(g)

    pieces = [_two_d(n, g[n]).reshape(-1) for n in _SMALL]
    sizes = [v.shape[0] for v in pieces]
    padded = [-(-s // _PACK) * _PACK for s in sizes]
    packed = jnp.concatenate([jnp.pad(v, (0, ps - s)) for v, s, ps in zip(pieces, sizes, padded)]).reshape(-1, 128)
    a_ssem, a_rsem, a_src, a_land, a_token = _small_gather_start(packed)

    sent, from_chips = _scatter_blocks_wait(s_sem, r_sem, sent, lands, (grad_x, a_token))
    halves = [_final_sum("final_sum_" + n, a, r, place_arr) for n, a, r in zip(_BIG, sent, from_chips)]
    grads = dict(zip(_BIG, _share_halves(halves)))
    delta, new_m, new_v = {}, {}, {}
    for n in _BIG:
        t = jnp.transpose if n == "w_in" else (lambda a: a)
        res = _adamw_call("adamw_" + n, t(shard[n]), t(grads[n]), t(mom[n][0]), t(vel[n][0]))
        delta[n], new_m[n], new_v[n], grads[n] = (t(r) for r in res)

    a_src, a_land = _small_gather_wait(a_ssem, a_rsem, a_src, a_land, delta["w_in"])
    summed = _small_sum(a_src, a_land, jnp.reshape(4 * xi + 2 * yi + ci, (1,)).astype(jnp.int32)).reshape(-1)
    off = 0
    for n, s, ps in zip(_SMALL, sizes, padded):
        grads[n] = summed[off:off + s].reshape(_two_d(n, g[n]).shape)
        off += ps
    grads["conv_w"] = lax.dynamic_slice_in_dim(grads["conv_w"], me * (CONV_DIM // N_CHIPS), CONV_DIM // N_CHIPS, axis=1)
    small = _adamw_small([_two_d(n, wt[n]) for n in _SMALL], [grads[n] for n in _SMALL],
                         [_two_d(n, mom[n]) for n in _SMALL], [_two_d(n, vel[n]) for n in _SMALL])
    for i, n in enumerate(_SMALL):
        delta[n], new_m[n], new_v[n] = small[0][i], small[1][i], small[2][i]

    def shaped(d):
        return [d[n].reshape(wt[n].shape) for n in _WEIGHTS]

    return (loss, grad_x[None], *shaped(grads), *shaped(delta), *shaped(new_m), *shaped(new_v))
```

```python
import functools

import jax
import jax.numpy as jnp
from jax import lax
from jax.experimental import pallas as pl
from jax.experimental.pallas import tpu as pltpu

F32 = jnp.float32
MXU_DTYPE = jnp.bfloat16
ACT_DTYPE = jnp.bfloat16
WIRE_DTYPE = jnp.bfloat16

D_MODEL = 1024
PLE_DIM = 256
CHUNK = 128
EPS = 1e-6
E_A = D_MODEL
G_A = 4
D_INNER = 2 * D_MODEL
HEAD_DIM = 64
N_HEADS = D_INNER // HEAD_DIM
N_STATE = 128
N_GROUPS = 4
HEADS_PER_GROUP = N_HEADS // N_GROUPS
PAIRS_PER_GROUP = HEADS_PER_GROUP // 2
CONV_K = 4
CONV_DIM = D_INNER + 2 * N_GROUPS * N_STATE
N_IN = 3 * E_A + D_INNER + CONV_DIM + N_HEADS + 2 * D_MODEL
N_CHIPS = 4
N_DEV = 8
W_IN_BLOCK = N_IN // N_CHIPS

UVZ_W, XBC_W, ZB_W, G_W = 3 * E_A, CONV_DIM, D_INNER, 2 * D_MODEL
MAIN_W = UVZ_W + XBC_W + ZB_W + G_W
UVZ_CB, XBC_CB, ZB_CB, G_CB = 0, 1, 3, 4
DT_W = N_GROUPS * 128

ADAM_LR, ADAM_B1, ADAM_B2, ADAM_EPS, ADAM_WD, ADAM_STEP = 0.001, 0.9, 0.999, 1e-08, 0.01, 10

MESH = pl.DeviceIdType.MESH
ANY = pl.BlockSpec(memory_space=pl.ANY)


def _mxu(v):
    return v.astype(MXU_DTYPE)


def _dot(a, b, dims=(((1,), (0,)), ((), ()))):
    return lax.dot_general(_mxu(a), _mxu(b), dims, preferred_element_type=F32)


V7X_MXU_WIDTH = 256
V7X_SCOPED_VMEM_BYTES = 60000 * 1024
MM_TILE = 4 * V7X_MXU_WIDTH
MM_VMEM_BUDGET = 4 * V7X_SCOPED_VMEM_BYTES // 5


def _mm_tk(m, n, k, tm, tn, a_bytes, b_bytes, out_bytes, extra_bytes):
    one_tile = m == tm and n == tn
    for parts in range(2 if one_tile else 1, k // 128 + 1):
        if k % parts or (k // parts) % 128 and parts > 1:
            continue
        tk = k // parts
        a_bufs = 1 if (parts == 1 and m == tm) else 2
        b_bufs = 1 if (parts == 1 and n == tn) else 2
        need = (tk * (a_bufs * tm * a_bytes + b_bufs * tn * b_bytes) + 2 * tm * tn * (out_bytes + extra_bytes)
                + (tm * tn * 4 if parts > 1 else 0))
        if need <= MM_VMEM_BUDGET:
            return tk
    return 128


def _matmul(a, b, *, mode, name, out_dtype, m, n, k, tm=MM_TILE, tn=MM_TILE, tk=None, a_off=0, b_off=0,
            extras=(), epilogue=None, after=()):
    tm, tn = min(tm, m), min(tn, n)
    if tk is None:
        tk = _mm_tk(m, n, k, tm, tn, a.dtype.itemsize, b.dtype.itemsize, jnp.dtype(out_dtype).itemsize,
                    sum(e.dtype.itemsize for e in extras))
    tk = min(tk, k)
    assert m % tm == 0 and n % tn == 0 and k % tk == 0, (name, m, n, k, tm, tn, tk)
    nk = k // tk
    a_mode = pl.Buffered(1) if (nk == 1 and m == tm) else None
    b_mode = pl.Buffered(1) if (nk == 1 and n == tn) else None
    if mode == "nn":
        assert a_off % tk == 0 and b_off % tn == 0
        a_spec = pl.BlockSpec((tm, tk), lambda i, j, kk: (i, kk + a_off // tk), pipeline_mode=a_mode)
        b_spec = pl.BlockSpec((tk, tn), lambda i, j, kk: (kk, j + b_off // tn), pipeline_mode=b_mode)
        dims = (((1,), (0,)), ((), ()))
    elif mode == "nt":
        a_spec = pl.BlockSpec((tm, tk), lambda i, j, kk: (i, kk), pipeline_mode=a_mode)
        b_spec = pl.BlockSpec((tn, tk), lambda i, j, kk: (j, kk), pipeline_mode=b_mode)
        dims = (((1,), (1,)), ((), ()))
    else:
        assert a_off % tm == 0 and b_off % tn == 0
        a_spec = pl.BlockSpec((tk, tm), lambda i, j, kk: (kk, i + a_off // tm), pipeline_mode=a_mode)
        b_spec = pl.BlockSpec((tk, tn), lambda i, j, kk: (kk, j + b_off // tn), pipeline_mode=b_mode)
        dims = (((0,), (0,)), ((), ()))
    ne = len(extras)

    def finish(acc, extra_refs, o_ref):
        res = acc if epilogue is None else epilogue(acc, *[e[...] for e in extra_refs])
        o_ref[...] = res.astype(o_ref.dtype)

    def body(a_ref, b_ref, *rest):
        extra_refs, o_ref = rest[:ne], rest[ne + len(after)]
        part = _dot(a_ref[...], b_ref[...], dims)
        if nk == 1:
            finish(part, extra_refs, o_ref)
            return
        acc_ref = rest[ne + len(after) + 1]
        kk = pl.program_id(2)

        @pl.when(kk == 0)
        def _():
            acc_ref[...] = part

        @pl.when(kk > 0)
        def _():
            acc_ref[...] += part

        @pl.when(kk == nk - 1)
        def _():
            finish(acc_ref[...], extra_refs, o_ref)

    o_spec = pl.BlockSpec((tm, tn), lambda i, j, kk: (i, j))
    return pl.pallas_call(
        body, name=name, grid=(m // tm, n // tn, nk),
        in_specs=[a_spec, b_spec] + [o_spec] * ne + [ANY] * len(after), out_specs=o_spec,
        out_shape=jax.ShapeDtypeStruct((m, n), out_dtype),
        scratch_shapes=[pltpu.VMEM((tm, tn), F32)] if nk > 1 else [],
        compiler_params=pltpu.CompilerParams(dimension_semantics=("parallel", "parallel", "arbitrary")),
    )(a, b, *extras, *after)


def _rows_matmul(name, f, rows, pars, b, *, out_dtype, n, k, tm, nrows, tn=MM_TILE, extras=(), epilogue=None, after=(),
                 side=None):
    tm, tn = min(tm, nrows), min(tn, n)
    assert nrows % tm == 0 and n % tn == 0, (name, nrows, n, tm, tn)
    nr, npar, ne, nj = len(rows), len(pars), len(extras), n // tn
    ns = 0 if side is None else 1
    n_in = nr + npar + 1 + ne + ns + len(after)

    def body(*refs):
        row_refs, par_refs, b_ref = refs[:nr], refs[nr:nr + npar], refs[nr + npar]
        extra_refs = refs[nr + npar + 1:nr + npar + 1 + ne]
        a_ref, o_ref = refs[n_in], refs[n_in + 1]

        def make_a():
            a = f(*[r[...].astype(F32) for r in row_refs], *[p[...] for p in par_refs])[0]
            a_ref[...] = a.astype(a_ref.dtype)
            if ns:
                refs[n_in + 2][...] = _dot(a_ref[...], refs[nr + npar + 1 + ne][...]).astype(refs[n_in + 2].dtype)

        if nj == 1:
            make_a()
        else:
            pl.when(pl.program_id(1) == 0)(make_a)
        res = _dot(a_ref[...], b_ref[...])
        if epilogue is not None:
            res = epilogue(res, *[e[...] for e in extra_refs])
        o_ref[...] = res.astype(o_ref.dtype)

    o_spec = pl.BlockSpec((tm, tn), lambda i, j: (i, j))
    side_in = [] if side is None else [pl.BlockSpec(tuple(side[0].shape), lambda i, j: (0, 0))]
    side_out = [] if side is None else [pl.BlockSpec((tm, side[0].shape[1]), lambda i, j: (i, 0))]
    side_shape = [] if side is None else [jax.ShapeDtypeStruct((nrows, side[0].shape[1]), side[1])]
    return pl.pallas_call(
        body, name=name, grid=(nrows // tm, nj),
        in_specs=[pl.BlockSpec((tm, w), lambda i, j, cb=cb: (i, cb)) for _, w, cb in rows]
        + [pl.BlockSpec(tuple(p.shape), lambda i, j, nd=p.ndim: (0,) * nd) for p in pars]
        + [pl.BlockSpec((k, tn), lambda i, j: (0, j))] + [o_spec] * ne + side_in + [ANY] * len(after),
        out_specs=[pl.BlockSpec((tm, k), lambda i, j: (i, 0)), o_spec] + side_out,
        out_shape=[jax.ShapeDtypeStruct((nrows, k), ACT_DTYPE), jax.ShapeDtypeStruct((nrows, n), out_dtype)] + side_shape,
        compiler_params=pltpu.CompilerParams(dimension_semantics=("parallel", "arbitrary")),
    )(*[r[0] for r in rows], *pars, b, *extras, *([] if side is None else [side[0]]), *after)


def _row_spec(tm, width, cb):
    return pl.BlockSpec((tm, width), lambda i: (i, cb))


def _whole_spec(shape):
    nd = len(shape)
    return pl.BlockSpec(tuple(shape), lambda i: (0,) * nd)


def _rows_vjp_call(name, f, rows, pars, cots, drows, *, tm, nrows, cot_mm=None):
    tm = min(tm, nrows)
    nr, npar, nc = len(rows), len(pars), len(cots)
    mm_args, mm_specs = [], []
    if cot_mm is not None:
        mm_a, mm_b, mm_add = cot_mm
        mm_args = [mm_a, mm_b] + ([] if mm_add is None else [mm_add])
        mm_specs = [_row_spec(tm, mm_a.shape[1], 0), _whole_spec(mm_b.shape)]
        mm_specs += [] if mm_add is None else [_row_spec(tm, mm_b.shape[0], 0)]
    alias_bufs, aliases = [], {}
    out_shape, out_specs = [], []
    for (arr, w, cb), d in zip(rows, drows):
        if d is None:
            continue
        dt, into = d
        if into is None:
            out_shape.append(jax.ShapeDtypeStruct((nrows, w), dt))
            out_specs.append(_row_spec(tm, w, 0))
        else:
            buf, total, ocb = into
            if buf is not None:
                aliases[nr + npar + nc + len(alias_bufs)] = len(out_shape)
                alias_bufs.append(buf)
            out_shape.append(jax.ShapeDtypeStruct((nrows, total), dt))
            out_specs.append(_row_spec(tm, w, ocb))
    n_drow = len(out_shape)
    for p in pars:
        out_shape.append(jax.ShapeDtypeStruct(p.shape, F32))
        out_specs.append(_whole_spec(p.shape))
    na = len(alias_bufs)

    def body(*refs):
        rv = [r[...].astype(F32) for r in refs[:nr]]
        pv = [p[...] for p in refs[nr:nr + npar]]
        cv = tuple(c[...].astype(F32) for c in refs[nr + npar:nr + npar + nc])
        o_refs = refs[nr + npar + nc + na + len(mm_args):]
        if mm_args:
            mm_refs = refs[nr + npar + nc + na:nr + npar + nc + na + len(mm_args)]
            c0 = _dot(mm_refs[0][...], mm_refs[1][...], (((1,), (1,)), ((), ())))
            if len(mm_refs) == 3:
                c0 = c0 + mm_refs[2][...].astype(F32)
            cv = (c0,) + cv
        _, vjp = jax.vjp(f, *rv, *pv)
        g = vjp(cv)
        oi = 0
        for ri, d in enumerate(drows):
            if d is not None:
                o_refs[oi][...] = g[ri].astype(o_refs[oi].dtype)
                oi += 1
        first = pl.program_id(0) == 0
        for pi in range(npar):
            acc = o_refs[n_drow + pi]

            @pl.when(first)
            def _(acc=acc):
                acc[...] = jnp.zeros_like(acc)

            acc[...] += g[nr + pi]

    return pl.pallas_call(
        body, name=name, grid=(nrows // tm,),
        in_specs=[_row_spec(tm, w, cb) for _, w, cb in rows] + [_whole_spec(p.shape) for p in pars]
        + [_row_spec(tm, w, cb) for _, w, cb in cots] + [ANY] * na + mm_specs,
        out_specs=out_specs, out_shape=out_shape, input_output_aliases=aliases,
        compiler_params=pltpu.CompilerParams(dimension_semantics=("arbitrary",)),
    )(*[r[0] for r in rows], *pars, *[c[0] for c in cots], *alias_bufs, *mm_args)


def _rms(x, g):
    return x * lax.rsqrt(jnp.mean(x * x, axis=-1, keepdims=True) + EPS) * g


def _f_rms(x, g):
    return (_rms(x, g),)


def _tril_mask():
    return lax.broadcasted_iota(jnp.int32, (CHUNK, CHUNK), 0) >= lax.broadcasted_iota(jnp.int32, (CHUNK, CHUNK), 1)


def _f_branch_a(uvz, ln_g, ln_b, w_s, b_s):
    u = jax.nn.gelu(uvz[:, :E_A])
    v = jax.nn.gelu(uvz[:, E_A:2 * E_A])
    z = uvz[:, 2 * E_A:]
    xc = v - jnp.mean(v, axis=-1, keepdims=True)
    vn = xc * lax.rsqrt(jnp.mean(xc * xc, axis=-1, keepdims=True) + EPS) * ln_g + ln_b
    mask = _tril_mask()
    ws = [jnp.where(mask, w_s[g], 0.0) for g in range(G_A)]
    gw = E_A // G_A
    rows = []
    for c in range(uvz.shape[0] // CHUNK):
        vc = vn[c * CHUNK:(c + 1) * CHUNK]
        rows.append(jnp.concatenate([_dot(ws[g], vc[:, g * gw:(g + 1) * gw]) + b_s[g] for g in range(G_A)], axis=1))
    sv = rows[0] if len(rows) == 1 else jnp.concatenate(rows, axis=0)
    return (u * sv * jax.nn.silu(z),)


def _f_gnorm(y, zb, g):
    yz = y * jax.nn.silu(zb)
    gw = D_INNER // N_GROUPS
    parts = []
    for i in range(N_GROUPS):
        s = yz[:, i * gw:(i + 1) * gw]
        parts.append(s * lax.rsqrt(jnp.mean(s * s, axis=-1, keepdims=True) + EPS))
    return (jnp.concatenate(parts, axis=1) * g,)


def _f_merge(g2, oa, ob):
    return (jax.nn.sigmoid(g2[:, :D_MODEL]) * oa + jax.nn.sigmoid(g2[:, D_MODEL:]) * ob,)


def _f_loss(x1, gp, pe, tgt, fg):
    x2 = x1 + jax.nn.sigmoid(gp) * pe
    err = _rms(x2, fg) - tgt
    return 0.5 * jnp.sum(jnp.mean(err * err, axis=-1))


def _head(x1, p, tgt, ple_g, w_pg, w_ple, fg, *, tm, nrows):
    tm = min(tm, nrows)

    def body(x1_ref, p_ref, t_ref, pg_ref, wpg_ref, wple_ref, fg_ref, hp_ref, dx_ref, dgp_ref, dpe_ref, dfg_ref, loss_ref):
        x1 = x1_ref[...]
        hp_ref[...] = _rms(x1, pg_ref[...]).astype(hp_ref.dtype)
        gp = _dot(hp_ref[...], wpg_ref[...])
        pe = _dot(p_ref[...], wple_ref[...])
        loss, vjp = jax.vjp(_f_loss, x1, gp, pe, t_ref[...], fg_ref[...])
        dx, dgp, dpe, _, dfg = vjp(jnp.ones((), F32))
        dx_ref[...] = dx
        dgp_ref[...] = dgp.astype(dgp_ref.dtype)
        dpe_ref[...] = dpe.astype(dpe_ref.dtype)

        @pl.when(pl.program_id(0) == 0)
        def _():
            dfg_ref[...] = jnp.zeros_like(dfg_ref)
            loss_ref[...] = jnp.zeros_like(loss_ref)

        dfg_ref[...] += dfg
        loss_ref[...] += jnp.full(loss_ref.shape, loss, F32)

    row = _row_spec(tm, D_MODEL, 0)
    act = jax.ShapeDtypeStruct((nrows, D_MODEL), ACT_DTYPE)
    return pl.pallas_call(
        body, name="head", grid=(nrows // tm,),
        in_specs=[row, _row_spec(tm, PLE_DIM, 0), row, _whole_spec((1, D_MODEL)), _whole_spec(w_pg.shape),
                  _whole_spec(w_ple.shape), _whole_spec((1, D_MODEL))],
        out_specs=[row, row, row, row, _whole_spec((1, D_MODEL)), _whole_spec((1, 128))],
        out_shape=[act, jax.ShapeDtypeStruct((nrows, D_MODEL), F32), act, act, jax.ShapeDtypeStruct((1, D_MODEL), F32),
                   jax.ShapeDtypeStruct((1, 128), F32)],
        compiler_params=pltpu.CompilerParams(dimension_semantics=("arbitrary",)),
    )(x1, p, tgt, ple_g, w_pg, w_ple, fg)


def _shift_rows(cur, edge, j, up):
    tm = cur.shape[0]
    row = lax.broadcasted_iota(jnp.int32, cur.shape, 0)
    if up:
        sh = pltpu.roll(cur, tm - j, 0)
        e = jnp.tile(pltpu.roll(edge, 8 - j, 0), (tm // 8, 1))
        return jnp.where(row >= tm - j, e, sh)
    sh = pltpu.roll(cur, j, 0)
    e = jnp.tile(pltpu.roll(edge, j, 0), (tm // 8, 1))
    return jnp.where(row < j, e, sh)


def _conv_pre(cur, prev, w, b):
    acc = cur * w[CONV_K - 1:CONV_K] + b
    for j in range(1, CONV_K):
        acc = acc + _shift_rows(cur, prev, j, up=False) * w[CONV_K - 1 - j:CONV_K - j]
    return acc


def _halo_specs(tm, nrows, cb, before):
    nb = tm // 8
    last = nrows // 8 - 1
    if before:
        return pl.BlockSpec((8, XBC_W), lambda i: (jnp.maximum(i * nb - 1, 0), cb))
    return pl.BlockSpec((8, XBC_W), lambda i: (jnp.minimum((i + 1) * nb, last), cb))


def _conv_fwd(proj, conv_w, conv_b, *, tm, nrows):
    tm = min(tm, nrows)

    def body(cur_ref, prev_ref, w_ref, b_ref, o_ref, pre_ref):
        prev = jnp.where(pl.program_id(0) == 0, 0.0, prev_ref[...].astype(F32))
        pre = _conv_pre(cur_ref[...].astype(F32), prev, w_ref[...], b_ref[...])
        o_ref[...] = jax.nn.silu(pre).astype(o_ref.dtype)
        pre_ref[...] = pre.astype(pre_ref.dtype)

    out = jax.ShapeDtypeStruct((nrows, XBC_W), ACT_DTYPE)
    return pl.pallas_call(
        body, name="conv_fwd", grid=(nrows // tm,),
        in_specs=[_row_spec(tm, XBC_W, XBC_CB), _halo_specs(tm, nrows, XBC_CB, True),
                  _whole_spec((CONV_K, XBC_W)), _whole_spec((1, XBC_W))],
        out_specs=[_row_spec(tm, XBC_W, 0)] * 2, out_shape=[out, out],
        compiler_params=pltpu.CompilerParams(dimension_semantics=("parallel",)),
    )(proj, proj, conv_w, conv_b)


def _conv_bwd_act(pre, dact, *, tm, nrows):
    tm = min(tm, nrows)
    nb = N_GROUPS * N_STATE

    def body(pre_ref, dxs_ref, dbm_ref, dcm_ref, dpre_ref, db_ref):
        pre = pre_ref[...].astype(F32)
        sg = jax.nn.sigmoid(pre)
        dy = jnp.concatenate([dxs_ref[...], dbm_ref[...], dcm_ref[...]], axis=1).astype(F32)
        dpre = dy * sg * (1.0 + pre * (1.0 - sg))
        dpre_ref[...] = dpre.astype(dpre_ref.dtype)

        @pl.when(pl.program_id(0) == 0)
        def _():
            db_ref[...] = jnp.zeros_like(db_ref)

        db_ref[...] += jnp.sum(dpre, axis=0, keepdims=True)

    return pl.pallas_call(
        body, name="conv_bwd_act", grid=(nrows // tm,),
        in_specs=[_row_spec(tm, XBC_W, 0), _row_spec(tm, D_INNER, 0), _row_spec(tm, nb, 0), _row_spec(tm, nb, 0)],
        out_specs=[_row_spec(tm, XBC_W, 0), _whole_spec((1, XBC_W))],
        out_shape=[jax.ShapeDtypeStruct((nrows, XBC_W), ACT_DTYPE), jax.ShapeDtypeStruct((1, XBC_W), F32)],
        compiler_params=pltpu.CompilerParams(dimension_semantics=("arbitrary",)),
    )(pre, *dact)


def _conv_bwd_x(dpre, proj, conv_w, dproj, *, tm, nrows):
    tm = min(tm, nrows)
    ntiles = nrows // tm

    def body(cur_ref, nxt_ref, x_ref, w_ref, _, o_ref, dw_ref):
        cur = cur_ref[...].astype(F32)
        nxt = jnp.where(pl.program_id(0) == ntiles - 1, 0.0, nxt_ref[...].astype(F32))
        x = x_ref[...].astype(F32)
        w = w_ref[...]

        @pl.when(pl.program_id(0) == 0)
        def _():
            dw_ref[...] = jnp.zeros_like(dw_ref)

        acc = cur * w[CONV_K - 1:CONV_K]
        dw_ref[CONV_K - 1:CONV_K, :] += jnp.sum(cur * x, axis=0, keepdims=True)
        for j in range(1, CONV_K):
            u = _shift_rows(cur, nxt, j, up=True)
            acc = acc + u * w[CONV_K - 1 - j:CONV_K - j]
            dw_ref[CONV_K - 1 - j:CONV_K - j, :] += jnp.sum(u * x, axis=0, keepdims=True)
        o_ref[...] = acc.astype(o_ref.dtype)

    return pl.pallas_call(
        body, name="conv_bwd_x", grid=(ntiles,),
        in_specs=[_row_spec(tm, XBC_W, 0), _halo_specs(tm, nrows, 0, False), _row_spec(tm, XBC_W, XBC_CB),
                  _whole_spec((CONV_K, XBC_W)), ANY],
        out_specs=[_row_spec(tm, XBC_W, XBC_CB), _whole_spec((CONV_K, XBC_W))],
        out_shape=[jax.ShapeDtypeStruct(dproj.shape, dproj.dtype), jax.ShapeDtypeStruct((CONV_K, XBC_W), F32)],
        input_output_aliases={4: 0},
        compiler_params=pltpu.CompilerParams(dimension_semantics=("arbitrary",)),
    )(dpre, dpre, proj, conv_w, dproj)


SSD_SPAN = 4
SSD_FWD_SPANS = 2
_XS_GW = D_INNER // N_GROUPS
_NT = (((1,), (1,)), ((), ()))
_TN = (((0,), (0,)), ((), ()))


def _bf16_terms(x, terms):
    parts, rest = [], x
    for _ in range(terms):
        part = rest.astype(jnp.bfloat16)
        parts.append(part)
        rest = rest - part.astype(F32)
    return parts


def _head_lane_matrix():
    return (lax.broadcasted_iota(jnp.int32, (128, _XS_GW), 0)
            == lax.broadcasted_iota(jnp.int32, (128, _XS_GW), 1) // HEAD_DIM).astype(jnp.bfloat16)


@functools.partial(jax.custom_vjp, nondiff_argnums=(1,))
def _head_lanes(cols, terms):
    e = _head_lane_matrix()
    return sum(jnp.dot(t, e, preferred_element_type=F32) for t in _bf16_terms(cols, terms))


def _head_lanes_fwd(cols, terms):
    return _head_lanes(cols, terms), None


def _head_lanes_bwd(terms, _, g):
    e = _head_lane_matrix()
    return (sum(lax.dot_general(t, e, _NT, preferred_element_type=F32) for t in _bf16_terms(g, 2)),)


_head_lanes.defvjp(_head_lanes_fwd, _head_lanes_bwd)


def _ssd_chunk(k, xs, bm, cm, dtr, hprev, dtb, alog, dsk):
    causal, tri, lo = k
    dt = jax.nn.softplus(dtr + dtb)
    da = dt * (-jnp.exp(alog))
    cs = jnp.dot(tri, da, precision=lax.Precision.HIGHEST, preferred_element_type=F32)
    cst = cs.T
    cs_l = _head_lanes(cs, 3)
    xdt = xs * _head_lanes(dt, 2)
    cb = _dot(cm, bm, _NT)
    yd = []
    for q in range(PAIRS_PER_GROUP):
        xq = xdt[:, 128 * q:128 * (q + 1)]
        y2 = [_dot(cb * jnp.exp(jnp.where(causal, cs[:, h:h + 1] - cst[h:h + 1, :], -jnp.inf)), xq)
              for h in (2 * q, 2 * q + 1)]
        yd.append(jnp.where(lo, y2[0], y2[1]))
    y_off = jnp.exp(cs_l) * _dot(cm, hprev, _NT)
    st = _dot(xdt * jnp.exp(cs_l[CHUNK - 1:CHUNK, :] - cs_l), bm, _TN)
    cdec = jnp.exp(cs[CHUNK - 1:CHUNK, :])
    cd_rows = jnp.concatenate(
        [jnp.broadcast_to(cdec[:, h:h + 1], (HEAD_DIM, N_STATE)) for h in range(HEADS_PER_GROUP)], axis=0)
    dsk_l = _head_lanes(jnp.broadcast_to(dsk, (8, 128)), 2)[:1]
    y = jnp.concatenate(yd, axis=1) + y_off + xs * dsk_l
    return y, cd_rows * hprev + st


def _ssd_span(xs, bm, cm, dtr, h0, dtb, alog, dsk):
    li = lax.broadcasted_iota(jnp.int32, (CHUNK, CHUNK), 0)
    si = lax.broadcasted_iota(jnp.int32, (CHUNK, CHUNK), 1)
    causal = li >= si
    k = (causal, causal.astype(F32), si < HEAD_DIM)
    h, ys = h0, []
    for t in range(xs.shape[0] // CHUNK):
        r = slice(t * CHUNK, (t + 1) * CHUNK)
        y, h = _ssd_chunk(k, xs[r], bm[r], cm[r], dtr[r], h, dtb, alog, dsk)
        ys.append(y)
    return (ys[0] if len(ys) == 1 else jnp.concatenate(ys, axis=0)), h


def _ssd_specs(rev, nsteps, rows):
    def s_of(s):
        return nsteps - 1 - s if rev else s

    xs = pl.BlockSpec((rows, _XS_GW), lambda g, s: (s_of(s), g))
    bm = pl.BlockSpec((rows, N_STATE), lambda g, s: (s_of(s), D_INNER // N_STATE + g))
    cm = pl.BlockSpec((rows, N_STATE), lambda g, s: (s_of(s), D_INNER // N_STATE + N_GROUPS + g))
    dt = pl.BlockSpec((rows, 128), lambda g, s: (s_of(s), g))
    par = pl.BlockSpec((1, 128), lambda g, s: (0, g))
    st = pl.BlockSpec((None, None, _XS_GW, N_STATE), lambda g, s: (g, s_of(s), 0, 0))
    return xs, bm, cm, dt, par, st


def _ssd_fwd(act, dtr, dtb, alog, dsk, *, nrows):
    span = CHUNK * min(SSD_SPAN, nrows // CHUNK)
    per_step = SSD_FWD_SPANS if nrows % (SSD_FWD_SPANS * span) == 0 else 1
    rows = per_step * span
    nsteps = nrows // rows
    xs, bm, cm, dt, par, _ = _ssd_specs(False, nsteps, rows)
    st = pl.BlockSpec((None, per_step, _XS_GW, N_STATE), lambda g, s: (g, s, 0, 0))

    def body(xs_ref, b_ref, c_ref, dt_ref, dtb_ref, al_ref, dk_ref, y_ref, st_ref, h_ref):
        @pl.when(pl.program_id(1) == 0)
        def _():
            h_ref[...] = jnp.zeros_like(h_ref)

        h = h_ref[...]
        for i in range(per_step):
            r = slice(i * span, (i + 1) * span)
            st_ref[i] = h
            y, h = _ssd_span(xs_ref[r, :].astype(F32), b_ref[r, :].astype(F32), c_ref[r, :].astype(F32), dt_ref[r, :],
                             h, dtb_ref[...], al_ref[...], dk_ref[...])
            y_ref[r, :] = y.astype(y_ref.dtype)
        h_ref[...] = h

    return pl.pallas_call(
        body, name="ssd_fwd", grid=(N_GROUPS, nsteps),
        in_specs=[xs, bm, cm, dt, par, par, par], out_specs=[xs, st],
        out_shape=[jax.ShapeDtypeStruct((nrows, D_INNER), ACT_DTYPE),
                   jax.ShapeDtypeStruct((N_GROUPS, nrows // span, _XS_GW, N_STATE), F32)],
        scratch_shapes=[pltpu.VMEM((_XS_GW, N_STATE), F32)],
        compiler_params=pltpu.CompilerParams(dimension_semantics=("arbitrary", "arbitrary")),
    )(act, act, act, dtr, dtb, alog, dsk)


def _ssd_bwd(act, dtr, dtb, alog, dsk, states, dy, *, nrows):
    rows = CHUNK * min(SSD_SPAN, nrows // CHUNK)
    nsteps = nrows // rows
    xs, bm, cm, dt, par, st = _ssd_specs(True, nsteps, rows)

    def body(xs_ref, b_ref, c_ref, dt_ref, dtb_ref, al_ref, dk_ref, st_ref, dy_ref,
             dxs_ref, db_ref, dc_ref, ddt_ref, ddtb_ref, dal_ref, ddk_ref, dh_ref):
        @pl.when(pl.program_id(1) == 0)
        def _():
            dh_ref[...] = jnp.zeros_like(dh_ref)
            ddtb_ref[...] = jnp.zeros_like(ddtb_ref)
            dal_ref[...] = jnp.zeros_like(dal_ref)
            ddk_ref[...] = jnp.zeros_like(ddk_ref)

        _, vjp = jax.vjp(_ssd_span, xs_ref[...].astype(F32), b_ref[...].astype(F32), c_ref[...].astype(F32),
                         dt_ref[...], st_ref[...], dtb_ref[...], al_ref[...], dk_ref[...])
        dxs, db, dc, ddt, dh, ddtb, dal, ddk = vjp((dy_ref[...].astype(F32), dh_ref[...]))
        dxs_ref[...] = dxs.astype(dxs_ref.dtype)
        db_ref[...] = db.astype(db_ref.dtype)
        dc_ref[...] = dc.astype(dc_ref.dtype)
        ddt_ref[...] = ddt
        dh_ref[...] = dh
        ddtb_ref[...] += ddtb
        dal_ref[...] += dal
        ddk_ref[...] += ddk

    nb = N_GROUPS * N_STATE
    bspec = pl.BlockSpec((rows, N_STATE), lambda g, s: (nsteps - 1 - s, g))
    return pl.pallas_call(
        body, name="ssd_bwd", grid=(N_GROUPS, nsteps),
        in_specs=[xs, bm, cm, dt, par, par, par, st, xs],
        out_specs=[xs, bspec, bspec, dt, par, par, par],
        out_shape=[jax.ShapeDtypeStruct((nrows, D_INNER), ACT_DTYPE), jax.ShapeDtypeStruct((nrows, nb), ACT_DTYPE),
                   jax.ShapeDtypeStruct((nrows, nb), ACT_DTYPE), jax.ShapeDtypeStruct((nrows, DT_W), F32),
                   jax.ShapeDtypeStruct((1, DT_W), F32), jax.ShapeDtypeStruct((1, DT_W), F32),
                   jax.ShapeDtypeStruct((1, DT_W), F32)],
        scratch_shapes=[pltpu.VMEM((_XS_GW, N_STATE), F32)],
        compiler_params=pltpu.CompilerParams(dimension_semantics=("arbitrary", "arbitrary")),
    )(act, act, act, dtr, dtb, alog, dsk, states, dy)


def _add_epilogue(acc, r):
    return r + acc


def _rms_and_skip(x, g):
    return _rms(x, g), x


def _forward_backward(x, p, tgt, w, late_weights=None, after=()):
    s = x.shape[0]
    act_t, f32 = ACT_DTYPE, F32
    mm = functools.partial(_matmul)
    h, proj, dtr = _rows_matmul("proj", _f_rms, [(x, D_MODEL, 0)], [w["norm_g"]], w["w_main"], out_dtype=act_t,
                                n=MAIN_W, k=D_MODEL, tm=1024, tn=2 * MM_TILE, nrows=s, after=after, side=(w["w_dt"], f32))
    act, conv_pre = _conv_fwd(proj, w["conv_w"], w["conv_b"], tm=512, nrows=s)
    y, states = _ssd_fwd(act, dtr, w["dt_bias"], w["a_log"], w["d_skip"], nrows=s)
    if late_weights is not None:
        w = {**w, **late_weights(states)}
    a_pars = [w["ln_a_g"], w["ln_a_b"], w["w_s"], w["b_s"]]
    y_a, o_a = _rows_matmul("out_a", _f_branch_a, [(proj, UVZ_W, UVZ_CB)], a_pars, w["w_oa"], out_dtype=act_t,
                            n=D_MODEL, k=E_A, tm=512, nrows=s)
    gn_rows = [(y, D_INNER, 0), (proj, ZB_W, ZB_CB)]
    y_b, o_b = _rows_matmul("out_b", _f_gnorm, gn_rows, [w["ssm_norm_g"]], w["w_ob"], out_dtype=act_t,
                            n=D_MODEL, k=D_INNER, tm=1024, nrows=s)
    mg_rows = [(proj, G_W, G_CB), (o_a, D_MODEL, 0), (o_b, D_MODEL, 0)]
    merged, x1 = _rows_matmul("out_proj", _f_merge, mg_rows, [], w["w_out"], out_dtype=f32, n=D_MODEL, k=D_MODEL,
                              tm=1024, nrows=s, extras=(x,), epilogue=_add_epilogue)
    g = {}
    hp, dx2, dgp, dpe, g["final_g"], loss = _head(x1, p, tgt, w["ple_norm_g"], w["w_pg"], w["w_ple"], w["final_g"],
                                                   tm=512, nrows=s)
    g["w_pg"] = mm(hp, dgp, mode="tn", name="d_w_pg", out_dtype=f32, m=D_MODEL, n=D_MODEL, k=s)
    g["w_ple"] = mm(p, dpe, mode="tn", name="d_w_ple", out_dtype=f32, m=PLE_DIM, n=D_MODEL, k=s)
    dx1, g["ple_norm_g"] = _rows_vjp_call(
        "ple_norm_bwd", _rms_and_skip, [(x1, D_MODEL, 0)], [w["ple_norm_g"]], [(dx2, D_MODEL, 0)],
        [(f32, None)], tm=1024, nrows=s, cot_mm=(dgp, w["w_pg"], None))
    g["w_out"] = mm(merged, dx1, mode="tn", name="d_w_out", out_dtype=f32, m=D_MODEL, n=D_MODEL, k=s)
    dproj, do_a, do_b = _rows_vjp_call(
        "merge_bwd", _f_merge, mg_rows, [], [],
        [(act_t, (None, MAIN_W, G_CB)), (act_t, None), (act_t, None)], tm=1024, nrows=s, cot_mm=(dx1, w["w_out"], None))
    g["w_oa"] = mm(y_a, do_a, mode="tn", name="d_w_oa", out_dtype=f32, m=E_A, n=D_MODEL, k=s)
    g["w_ob"] = mm(y_b, do_b, mode="tn", name="d_w_ob", out_dtype=f32, m=D_INNER, n=D_MODEL, k=s)
    dy, dproj, g["ssm_norm_g"] = _rows_vjp_call(
        "gnorm_bwd", _f_gnorm, gn_rows, [w["ssm_norm_g"]], [],
        [(act_t, None), (act_t, (dproj, MAIN_W, ZB_CB))], tm=512, nrows=s, cot_mm=(do_b, w["w_ob"], None))
    dxs, dbm, dcm, ddtr, g["dt_bias"], g["a_log"], g["d_skip"] = _ssd_bwd(
        act, dtr, w["dt_bias"], w["a_log"], w["d_skip"], states, dy, nrows=s)
    dpre, g["conv_b"] = _conv_bwd_act(conv_pre, (dxs, dbm, dcm), tm=1024, nrows=s)
    dproj, g["conv_w"] = _conv_bwd_x(dpre, proj, w["conv_w"], dproj, tm=512, nrows=s)
    dproj, g["ln_a_g"], g["ln_a_b"], g["w_s"], g["b_s"] = _rows_vjp_call(
        "branch_a_bwd", _f_branch_a, [(proj, UVZ_W, UVZ_CB)], a_pars, [],
        [(act_t, (dproj, MAIN_W, UVZ_CB))], tm=512, nrows=s, cot_mm=(do_a, w["w_oa"], None))
    g["w_main"] = mm(h, dproj, mode="tn", name="d_w_main", out_dtype=f32, m=D_MODEL, n=MAIN_W, k=s, tn=MM_TILE // 2)
    g["w_dt"] = mm(h, ddtr, mode="tn", name="d_w_dt", out_dtype=f32, m=D_MODEL, n=DT_W, k=s)
    return loss, g, (dproj, ddtr, dx1)


def _input_grad(x, w, ctx, after=()):
    dproj, ddtr, dx1 = ctx
    s = x.shape[0]
    dh = _matmul(dproj, w["w_main"], mode="nt", name="d_h_main", out_dtype=F32, m=s, n=D_MODEL, k=MAIN_W,
                 tm=MM_TILE // 2, after=after)
    return _rows_vjp_call(
        "pre_norm_bwd", _rms_and_skip, [(x, D_MODEL, 0)], [w["norm_g"]], [(dx1, D_MODEL, 0)],
        [(F32, None)], tm=1024, nrows=s, cot_mm=(ddtr, w["w_dt"], dh))


def _local_step(x, p, tgt, w):
    loss, g, ctx = _forward_backward(x, p, tgt, w)
    grad_x, g["norm_g"] = _input_grad(x, w, ctx)
    return loss, grad_x, g


_O_ZB = 3 * E_A
_O_XBC = _O_ZB + D_INNER
_O_DT = _O_XBC + CONV_DIM
_O_G = _O_DT + N_HEADS


def _heads_to_lanes(v):
    r = v.shape[0]
    v = v.reshape(r, N_GROUPS, HEADS_PER_GROUP)
    return jnp.pad(v, ((0, 0), (0, 0), (0, 128 - HEADS_PER_GROUP))).reshape(r, DT_W)


def _lanes_to_heads(v):
    r = v.shape[0]
    return v.reshape(r, N_GROUPS, 128)[:, :, :HEADS_PER_GROUP].reshape(r, N_HEADS)


def _block_cols(blocks, a, b):
    parts = []
    for k in range(N_CHIPS):
        lo, hi = max(a, k * W_IN_BLOCK), min(b, (k + 1) * W_IN_BLOCK)
        if lo < hi:
            parts.append(blocks[k][:, lo - k * W_IN_BLOCK:hi - k * W_IN_BLOCK])
    return parts


_W_IN_SEGMENTS = ((0, _O_ZB, "m", 0), (_O_ZB, _O_XBC, "m", UVZ_W + XBC_W), (_O_XBC, _O_DT, "m", UVZ_W),
                  (_O_DT, _O_G, "d", 0), (_O_G, N_IN, "m", MAIN_W - G_W))


def _w_in_grad_blocks(gm, gdt):
    blocks = []
    for k in range(N_CHIPS):
        a, b = k * W_IN_BLOCK, (k + 1) * W_IN_BLOCK
        parts = []
        for s, e, src, off in _W_IN_SEGMENTS:
            lo, hi = max(a, s), min(b, e)
            if lo < hi:
                parts.append((gm if src == "m" else gdt)[:, off + lo - s:off + hi - s])
        blocks.append(jnp.concatenate(parts, axis=1))
    return jnp.stack(blocks)


def _layout_weights(f, w_in_blocks=None):
    w = dict(f)
    if w_in_blocks is None:
        w_in = w.pop("w_in")
        w_in_blocks = jnp.stack([w_in[:, k * W_IN_BLOCK:(k + 1) * W_IN_BLOCK] for k in range(N_CHIPS)])
    cols = functools.partial(_block_cols, w_in_blocks)
    w["w_main"] = jnp.concatenate(cols(0, _O_ZB) + cols(_O_XBC, _O_DT) + cols(_O_ZB, _O_XBC) + cols(_O_G, N_IN), axis=1)
    w["w_dt"] = _heads_to_lanes(jnp.concatenate(cols(_O_DT, _O_G), axis=1))
    w["b_s"] = f["b_s"].reshape(G_A, CHUNK, 1)
    for n in ("dt_bias", "a_log", "d_skip"):
        w[n] = _heads_to_lanes(f[n])
    return w


def _natural_grads(g):
    out = dict(g)
    gm = out.pop("w_main")
    gdt = _lanes_to_heads(out.pop("w_dt"))
    out["w_in"] = jnp.concatenate(
        [gm[:, :UVZ_W], gm[:, UVZ_W + XBC_W:UVZ_W + XBC_W + ZB_W], gm[:, UVZ_W:UVZ_W + XBC_W], gdt, gm[:, MAIN_W - G_W:]],
        axis=1)
    out["b_s"] = g["b_s"].reshape(G_A, CHUNK)
    for n in ("dt_bias", "a_log", "d_skip"):
        out[n] = _lanes_to_heads(g[n])
    return out


def _place():
    return lax.axis_index("x"), lax.axis_index("y"), lax.axis_index("c")


def _other_chips(x, y):
    return [(1 - x, y), (x, 1 - y), (1 - x, 1 - y)]


def _rcopy(src, dst, ssem, rsem, dev):
    return pltpu.make_async_remote_copy(src_ref=src, dst_ref=dst, send_sem=ssem, recv_sem=rsem,
                                        device_id=dev, device_id_type=MESH)


def _half(ref_rows, half):
    hs = ref_rows // 2
    return pl.ds(pl.multiple_of(half * hs, 16), hs)


GATHER_CHUNKS = 4


def _gather_weights(shards, conv_shard):
    nw, nq = len(shards), GATHER_CHUNKS

    def body(*refs):
        sh, cv = refs[:nw], refs[nw]
        out, cvo = refs[nw + 1:2 * nw + 1], refs[2 * nw + 1]
        ici_s, ici_r, fw_s, fw_r, own_s, own_r, cv_s, cv_r = refs[2 * nw + 2:]
        x, y, c = _place()
        me, sib, chips = 2 * x + y, (x, y, 1 - c), _other_chips(x, y)
        own = [_rcopy(sh[w], out[w].at[me], own_s.at[w], own_r.at[w], sib) for w in range(nw)]
        own.append(_rcopy(cv, cvo.at[me], own_s.at[nw], own_r.at[nw], sib))
        for cp in own:
            cp.start()
        pieces = [(w, q, j) for w in range(nw) for q in range(nq) for j in range(3)]

        def rows(w, half, q):
            hs = sh[w].shape[0] // 2
            return pl.ds(pl.multiple_of(half * hs + q * (hs // nq), 16), hs // nq)

        def sem(w, q, j):
            return (3 * w + j) * nq + q

        def landed(w, q, j, half):
            return out[w].at[2 * chips[j][0] + chips[j][1], rows(w, half, q)]

        sends = [_rcopy(sh[w].at[rows(w, c, q)], out[w].at[me, rows(w, c, q)], ici_s.at[sem(w, q, j)],
                        ici_r.at[sem(w, q, j)], (*chips[j], c)) for w, q, j in pieces if j < 2]
        for j, chip in enumerate(chips):
            sends.append(_rcopy(cv, cvo.at[me], cv_s.at[j], cv_r.at[j], (*chip, c)))
        for cp in sends:
            cp.start()

        def arrived(w, q, j):
            slab = landed(w, q, j, c)
            _rcopy(slab, slab, ici_s.at[sem(w, q, j)], ici_r.at[sem(w, q, j)], (*chips[j], c)).wait_recv()
            fwd = _rcopy(slab, slab, fw_s.at[sem(w, q, j)], fw_r.at[sem(w, q, j)], sib)
            fwd.start()
            sends.append(fwd)
            if j < 2 and (q % 2 == 0) == (j == 0):
                hop = _rcopy(slab, slab, ici_s.at[sem(w, q, 2)], ici_r.at[sem(w, q, 2)], (*chips[1 - j], c))
                hop.start()
                sends.append(hop)

        for w, q, j in pieces:
            if j < 2:
                arrived(w, q, j)
        for w, q, j in pieces:
            if j == 2:
                arrived(w, q, j)
        for j, chip in enumerate(chips):
            blk = cvo.at[2 * chip[0] + chip[1]]
            _rcopy(blk, blk, cv_s.at[j], cv_r.at[j], (*chip, c)).wait_recv()
        for w, q, j in pieces:
            slab = landed(w, q, j, 1 - c)
            _rcopy(slab, slab, fw_s.at[sem(w, q, j)], fw_r.at[sem(w, q, j)], sib).wait_recv()
        for cp in sends:
            cp.wait_send()
        for cp in own:
            cp.wait()

    dma = pltpu.SemaphoreType.DMA
    n_ici = 3 * nw * nq
    return pl.pallas_call(
        body, name="gather_weights",
        in_specs=[ANY] * (nw + 1), out_specs=[ANY] * (nw + 1),
        out_shape=[jax.ShapeDtypeStruct((N_CHIPS,) + s.shape, s.dtype) for s in shards]
        + [jax.ShapeDtypeStruct((N_CHIPS,) + conv_shard.shape, conv_shard.dtype)],
        scratch_shapes=[dma((n_ici,)), dma((n_ici,)), dma((n_ici,)), dma((n_ici,)), dma((nw + 1,)), dma((nw + 1,)),
                        dma((3,)), dma((3,))],
    )(*shards, conv_shard)


_HBM = pl.BlockSpec(memory_space=pltpu.HBM)
_SEM = pl.BlockSpec(memory_space=pltpu.SEMAPHORE)
_EFFECT = pltpu.SideEffectType.DATAFLOW_SIDE_EFFECTING


def _late_gather_copies(sh, out, s_sem, r_sem):
    x, y, c = _place()
    to = [(*chip, c) for chip in _other_chips(x, y)] + [(x, y, 1 - c)]
    return [_rcopy(sh[w], out[w].at[2 * x + y], s_sem.at[4 * w + j], r_sem.at[4 * w + j], dev)
            for w in range(len(sh)) for j, dev in enumerate(to)]


def _late_gather_start(shards):
    n = len(shards)
    lands = [lax.empty((N_CHIPS,) + a.shape, a.dtype) for a in shards]

    def body(*refs):
        for cp in _late_gather_copies(refs[:n], refs[n:2 * n], refs[2 * n], refs[2 * n + 1]):
            cp.start()
        refs[-1][...] = jnp.zeros_like(refs[-1])

    dma = pltpu.SemaphoreType.DMA
    hbm = [pltpu.with_memory_space_constraint(a, pltpu.HBM) for a in list(shards) + lands]
    out = pl.pallas_call(
        body, name="late_gather_start",
        out_shape=[dma((4 * n,)), dma((4 * n,))] + [pltpu.HBM(a.shape, a.dtype) for a in hbm]
        + [jax.ShapeDtypeStruct((8, 128), F32)],
        in_specs=[_HBM] * (2 * n), out_specs=[_SEM, _SEM] + [_HBM] * (2 * n) + [pl.BlockSpec(memory_space=pltpu.VMEM)],
        input_output_aliases={i: 2 + i for i in range(2 * n)},
        compiler_params=pltpu.CompilerParams(has_side_effects=_EFFECT),
    )(*hbm)
    return out[0], out[1], out[2:2 + n], out[2 + n:2 + 2 * n], out[-1]


def _late_gather_wait(s_sem, r_sem, srcs, lands, after):
    n = len(srcs)

    def body(*refs):
        for cp in _late_gather_copies(refs[:n], refs[n:2 * n], refs[2 * n], refs[2 * n + 1]):
            cp.wait_send()
            cp.wait_recv()

    out = pl.pallas_call(
        body, name="late_gather_wait",
        out_shape=[pltpu.HBM(a.shape, a.dtype) for a in list(srcs) + list(lands)],
        in_specs=[_HBM] * (2 * n) + [_SEM, _SEM, ANY], out_specs=[_HBM] * (2 * n),
        input_output_aliases={i: i for i in range(2 * n)},
        compiler_params=pltpu.CompilerParams(has_side_effects=_EFFECT),
    )(*srcs, *lands, s_sem, r_sem, after)
    return out[n:]


def _swap_with_sibling(arrs):
    n = len(arrs)

    def body(*refs):
        src, dst, s_sem, r_sem = refs[:n], refs[n:2 * n], refs[2 * n], refs[2 * n + 1]
        x, y, c = _place()
        cps = [_rcopy(src[i], dst[i], s_sem.at[i], r_sem.at[i], (x, y, 1 - c)) for i in range(n)]
        for cp in cps:
            cp.start()
        for cp in cps:
            cp.wait()

    dma = pltpu.SemaphoreType.DMA
    return pl.pallas_call(
        body, name="swap_with_sibling", in_specs=[ANY] * n, out_specs=[ANY] * n,
        out_shape=[jax.ShapeDtypeStruct(a.shape, a.dtype) for a in arrs], scratch_shapes=[dma((n,)), dma((n,))],
    )(*arrs)


def _scatter_copies(src, land, s_sem, r_sem):
    x, y, c = _place()
    return [_rcopy(src[i].at[2 * chip[0] + chip[1]], land[i].at[j], s_sem.at[3 * i + j], r_sem.at[3 * i + j], (*chip, c))
            for i in range(len(src)) for j, chip in enumerate(_other_chips(x, y))]


def _scatter_blocks_start(arrs):
    n = len(arrs)
    lands = [lax.empty((3,) + a.shape[1:], a.dtype) for a in arrs]

    def body(*refs):
        src, land, s_sem, r_sem, token = refs[:n], refs[n:2 * n], refs[2 * n], refs[2 * n + 1], refs[-1]
        for cp in _scatter_copies(src, land, s_sem, r_sem):
            cp.start()
        token[...] = jnp.zeros_like(token)

    dma = pltpu.SemaphoreType.DMA
    hbm = [pltpu.with_memory_space_constraint(a, pltpu.HBM) for a in list(arrs) + lands]
    out = pl.pallas_call(
        body, name="scatter_blocks_start",
        out_shape=[dma((3 * n,)), dma((3 * n,))] + [pltpu.HBM(a.shape, a.dtype) for a in hbm]
        + [jax.ShapeDtypeStruct((8, 128), F32)],
        in_specs=[_HBM] * (2 * n), out_specs=[_SEM, _SEM] + [_HBM] * (2 * n) + [pl.BlockSpec(memory_space=pltpu.VMEM)],
        input_output_aliases={i: 2 + i for i in range(2 * n)},
        compiler_params=pltpu.CompilerParams(has_side_effects=_EFFECT),
    )(*hbm)
    return out[0], out[1], out[2:2 + n], out[2 + n:2 + 2 * n], out[-1]


def _scatter_blocks_wait(s_sem, r_sem, srcs, lands, after):
    n = len(srcs)

    def body(*refs):
        src, land, s_sem, r_sem = refs[:n], refs[n:2 * n], refs[2 * n], refs[2 * n + 1]
        for cp in _scatter_copies(src, land, s_sem, r_sem):
            cp.wait_send()
            cp.wait_recv()

    out = pl.pallas_call(
        body, name="scatter_blocks_wait",
        out_shape=[pltpu.HBM(a.shape, a.dtype) for a in list(srcs) + list(lands)],
        in_specs=[_HBM] * (2 * n) + [_SEM, _SEM] + [ANY] * len(after), out_specs=[_HBM] * (2 * n),
        input_output_aliases={i: i for i in range(2 * n)},
        compiler_params=pltpu.CompilerParams(has_side_effects=_EFFECT),
    )(*srcs, *lands, s_sem, r_sem, *after)
    return out[:n], out[n:]


def _share_halves(arrs):
    n = len(arrs)

    def body(*refs):
        buf, s_sem, r_sem = refs[n:2 * n], refs[2 * n], refs[2 * n + 1]
        x, y, c = _place()
        cps = []
        for i in range(n):
            mine = buf[i].at[_half(buf[i].shape[0], c)]
            cps.append(_rcopy(mine, mine, s_sem.at[i], r_sem.at[i], (x, y, 1 - c)))
        for cp in cps:
            cp.start()
        for i in range(n):
            theirs = buf[i].at[_half(buf[i].shape[0], 1 - c)]
            _rcopy(theirs, theirs, s_sem.at[i], r_sem.at[i], (x, y, 1 - c)).wait_recv()
        for cp in cps:
            cp.wait_send()

    dma = pltpu.SemaphoreType.DMA
    return pl.pallas_call(
        body, name="share_halves", in_specs=[ANY] * n, out_specs=[ANY] * n,
        out_shape=[jax.ShapeDtypeStruct(a.shape, a.dtype) for a in arrs],
        input_output_aliases={i: i for i in range(n)}, scratch_shapes=[dma((n,)), dma((n,))],
    )(*arrs)


def _small_gather_copies(src, land, s_sem, r_sem):
    x, y, c = _place()
    cps = []
    for d in range(1, N_DEV):
        peer = ((1 - x) if d & 4 else x), ((1 - y) if d & 2 else y), ((1 - c) if d & 1 else c)
        cps.append(_rcopy(src, land.at[4 * x + 2 * y + c], s_sem.at[d - 1], r_sem.at[d - 1], peer))
    return cps


def _small_gather_start(packed):
    def body(src, land, s_sem, r_sem, _, __, token):
        for cp in _small_gather_copies(src, land, s_sem, r_sem):
            cp.start()
        token[...] = jnp.zeros_like(token)

    dma = pltpu.SemaphoreType.DMA
    hbm = [pltpu.with_memory_space_constraint(a, pltpu.HBM) for a in (packed, lax.empty((N_DEV,) + packed.shape, F32))]
    return pl.pallas_call(
        body, name="small_gather_start",
        out_shape=[dma((N_DEV - 1,)), dma((N_DEV - 1,))] + [pltpu.HBM(a.shape, a.dtype) for a in hbm]
        + [jax.ShapeDtypeStruct((8, 128), F32)],
        in_specs=[_HBM] * 2, out_specs=[_SEM, _SEM, _HBM, _HBM, pl.BlockSpec(memory_space=pltpu.VMEM)],
        input_output_aliases={0: 2, 1: 3}, compiler_params=pltpu.CompilerParams(has_side_effects=_EFFECT),
    )(*hbm)


def _small_gather_wait(s_sem, r_sem, src, land, after):
    def body(src, land, s_sem, r_sem, *_):
        for cp in _small_gather_copies(src, land, s_sem, r_sem):
            cp.wait_send()
            cp.wait_recv()

    return pl.pallas_call(
        body, name="small_gather_wait", out_shape=[pltpu.HBM(src.shape, src.dtype), pltpu.HBM(land.shape, land.dtype)],
        in_specs=[_HBM, _HBM, _SEM, _SEM, ANY], out_specs=[_HBM, _HBM], input_output_aliases={0: 0, 1: 1},
        compiler_params=pltpu.CompilerParams(has_side_effects=_EFFECT),
    )(src, land, s_sem, r_sem, after)


def _small_sum(own, land, dev_arr):
    def body(me_ref, own_ref, land_ref, o_ref):
        acc = jnp.zeros(o_ref.shape, F32)
        for d in range(N_DEV):
            acc = acc + jnp.where(me_ref[0] == d, own_ref[...], land_ref[d])
        o_ref[...] = acc

    return pl.pallas_call(
        body, name="small_sum", out_shape=jax.ShapeDtypeStruct(own.shape, F32),
        grid_spec=pltpu.PrefetchScalarGridSpec(
            num_scalar_prefetch=1, grid=(1,),
            in_specs=[pl.BlockSpec(own.shape, lambda i, m: (0, 0)), pl.BlockSpec(land.shape, lambda i, m: (0, 0, 0))],
            out_specs=pl.BlockSpec(own.shape, lambda i, m: (0, 0))),
    )(dev_arr, own, land)


def _row_tile(rows, cols):
    tr = max(8, min(rows, (1 << 20) // (4 * cols) // 8 * 8))
    while rows % tr:
        tr -= 8
    return tr


def _chip_sum(name, g5, recv, c_arr):
    nb, _, hs, cols = g5.shape
    tr = _row_tile(hs, cols)

    def body(_, a_ref, b_ref, o_ref):
        o_ref[...] = (a_ref[...] + b_ref[...].astype(F32)).astype(o_ref.dtype)

    blk = pl.BlockSpec((None, tr, cols), lambda b, i, c: (b, i, 0))
    return pl.pallas_call(
        body, name=name,
        grid_spec=pltpu.PrefetchScalarGridSpec(
            num_scalar_prefetch=1, grid=(nb, hs // tr),
            in_specs=[pl.BlockSpec((None, None, tr, cols), lambda b, i, c: (b, c[0], i, 0)), blk], out_specs=blk),
        out_shape=jax.ShapeDtypeStruct((nb, hs, cols), WIRE_DTYPE),
    )(c_arr, g5, recv)


def _final_sum(name, own, recv, place_arr):
    _, hs, cols = own.shape
    tr = _row_tile(hs, cols)
    nt = hs // tr

    def body(_, a_ref, r_ref, o_ref):
        o_ref[...] = ((a_ref[...].astype(F32) + r_ref[0].astype(F32)) + r_ref[1].astype(F32)) + r_ref[2].astype(F32)

    return pl.pallas_call(
        body, name=name,
        grid_spec=pltpu.PrefetchScalarGridSpec(
            num_scalar_prefetch=1, grid=(nt,),
            in_specs=[pl.BlockSpec((None, tr, cols), lambda i, m: (m[0], i, 0)),
                      pl.BlockSpec((3, tr, cols), lambda i, m: (0, i, 0))],
            out_specs=pl.BlockSpec((tr, cols), lambda i, m: (m[1] * nt + i, 0))),
        out_shape=jax.ShapeDtypeStruct((2 * hs, cols), F32),
    )(place_arr, own, recv)


def _adamw(w, g, m, v):
    m = ADAM_B1 * m + (1.0 - ADAM_B1) * g
    v = ADAM_B2 * v + (1.0 - ADAM_B2) * (g * g)
    m_hat = m / (1.0 - ADAM_B1 ** ADAM_STEP)
    v_hat = v / (1.0 - ADAM_B2 ** ADAM_STEP)
    return -ADAM_LR * (m_hat / (jnp.sqrt(v_hat) + ADAM_EPS) + ADAM_WD * w), m, v


def _adamw_call(name, w, g, m, v):
    rows, cols = w.shape
    tr = _row_tile(rows, cols)
    if 4 * tr * cols >= (1 << 18):
        blk, steps = pl.BlockSpec((tr, cols), lambda i: (i, 0)), rows // tr
    else:
        blk, steps = pl.BlockSpec((rows, 128), lambda i: (0, i)), cols // 128

    def body(w_ref, g_ref, m_ref, v_ref, d_ref, nm_ref, nv_ref, go_ref):
        g = g_ref[...]
        d_ref[...], nm_ref[...], nv_ref[...] = _adamw(w_ref[...], g, m_ref[...], v_ref[...])
        go_ref[...] = g

    return pl.pallas_call(
        body, name=name, grid=(steps,), in_specs=[blk] * 4, out_specs=[blk] * 4,
        out_shape=[jax.ShapeDtypeStruct(w.shape, F32)] * 4,
        compiler_params=pltpu.CompilerParams(dimension_semantics=("parallel",)),
    )(w, g, m, v)


def _adamw_small(ws, gs, ms, vs):
    n = len(ws)

    def body(*refs):
        for i in range(n):
            w_ref, g_ref, m_ref, v_ref = (refs[k * n + i] for k in range(4))
            d, nm, nv = _adamw(w_ref[...], g_ref[...], m_ref[...], v_ref[...])
            refs[4 * n + i][...] = d
            refs[5 * n + i][...] = nm
            refs[6 * n + i][...] = nv

    out = pl.pallas_call(
        body, name="adamw_small", out_shape=[jax.ShapeDtypeStruct(a.shape, F32) for a in ws] * 3,
    )(*ws, *gs, *ms, *vs)
    return out[:n], out[n:2 * n], out[2 * n:]


_BIG = ("w_in", "w_oa", "w_ob", "w_out", "w_pg", "w_ple")
_SMALL = ("norm_g", "ln_a_g", "ln_a_b", "w_s", "b_s", "conv_w", "conv_b", "dt_bias", "a_log", "d_skip", "ssm_norm_g",
          "ple_norm_g", "final_g")
_WEIGHTS = ("norm_g", "w_in", "ln_a_g", "ln_a_b", "w_s", "b_s", "conv_w", "conv_b", "dt_bias", "a_log", "d_skip",
            "ssm_norm_g", "w_oa", "w_ob", "w_out", "ple_norm_g", "w_pg", "w_ple", "final_g")
_COL_SHARDED = ("w_in", "w_ple")
_PACK = 1024


def _blocks_to_full(col_sharded, blocks):
    if col_sharded:
        return jnp.concatenate([blocks[k] for k in range(N_CHIPS)], axis=1)
    return blocks.reshape(N_CHIPS * blocks.shape[1], blocks.shape[2])


def _full_to_blocks(col_sharded, full):
    if col_sharded:
        w = full.shape[1] // N_CHIPS
        return jnp.stack([full[:, k * w:(k + 1) * w] for k in range(N_CHIPS)])
    return full.reshape(N_CHIPS, full.shape[0] // N_CHIPS, full.shape[1])


def _two_d(n, a):
    if n == "w_s":
        return a.reshape(G_A * CHUNK, CHUNK)
    if n in ("b_s", "conv_w"):
        return a.reshape(a.shape[-2], a.shape[-1])
    return a.reshape(1, a.shape[-1])


def kernel(x, p, norm_g, w_in, ln_a_g, ln_a_b, w_s, b_s, conv_w, conv_b, dt_bias, a_log, d_skip, ssm_norm_g, w_oa, w_ob, w_out, ple_norm_g, w_pg, w_ple, final_g, loss_target, m_norm_g, m_w_in, m_ln_a_g, m_ln_a_b, m_w_s, m_b_s, m_conv_w, m_conv_b, m_dt_bias, m_a_log, m_d_skip, m_ssm_norm_g, m_w_oa, m_w_ob, m_w_out, m_ple_norm_g, m_w_pg, m_w_ple, m_final_g, v_norm_g, v_w_in, v_ln_a_g, v_ln_a_b, v_w_s, v_b_s, v_conv_w, v_conv_b, v_dt_bias, v_a_log, v_d_skip, v_ssm_norm_g, v_w_oa, v_w_ob, v_w_out, v_ple_norm_g, v_w_pg, v_w_ple, v_final_g):
    wt = dict(norm_g=norm_g, w_in=w_in, ln_a_g=ln_a_g, ln_a_b=ln_a_b, w_s=w_s, b_s=b_s, conv_w=conv_w, conv_b=conv_b,
              dt_bias=dt_bias, a_log=a_log, d_skip=d_skip, ssm_norm_g=ssm_norm_g, w_oa=w_oa, w_ob=w_ob, w_out=w_out,
              ple_norm_g=ple_norm_g, w_pg=w_pg, w_ple=w_ple, final_g=final_g)
    mom = dict(norm_g=m_norm_g, w_in=m_w_in, ln_a_g=m_ln_a_g, ln_a_b=m_ln_a_b, w_s=m_w_s, b_s=m_b_s, conv_w=m_conv_w,
               conv_b=m_conv_b, dt_bias=m_dt_bias, a_log=m_a_log, d_skip=m_d_skip, ssm_norm_g=m_ssm_norm_g, w_oa=m_w_oa,
               w_ob=m_w_ob, w_out=m_w_out, ple_norm_g=m_ple_norm_g, w_pg=m_w_pg, w_ple=m_w_ple, final_g=m_final_g)
    vel = dict(norm_g=v_norm_g, w_in=v_w_in, ln_a_g=v_ln_a_g, ln_a_b=v_ln_a_b, w_s=v_w_s, b_s=v_b_s, conv_w=v_conv_w,
               conv_b=v_conv_b, dt_bias=v_dt_bias, a_log=v_a_log, d_skip=v_d_skip, ssm_norm_g=v_ssm_norm_g, w_oa=v_w_oa,
               w_ob=v_w_ob, w_out=v_w_out, ple_norm_g=v_ple_norm_g, w_pg=v_w_pg, w_ple=v_w_ple, final_g=v_final_g)
    xi, yi, ci = _place()
    me = 2 * xi + yi
    c_arr = jnp.reshape(ci, (1,)).astype(jnp.int32)
    place_arr = jnp.stack([me, ci]).astype(jnp.int32)

    shard = {n: wt[n][0] for n in _BIG}
    wire = {n: shard[n].astype(WIRE_DTYPE) for n in _BIG}
    w_in_blocks, conv_blocks = _gather_weights([wire["w_in"]], conv_w[0])
    g_ssem, g_rsem, g_sent, g_lands, g_token = _late_gather_start([wire[n] for n in _BIG[1:]])
    full = {"conv_w": _blocks_to_full(True, conv_blocks)}
    for n in _SMALL:
        if n != "conv_w":
            full[n] = wt[n][0] if wt[n].ndim > 2 else wt[n].reshape(1, wt[n].shape[-1])

    def late_weights(after):
        blocks = _late_gather_wait(g_ssem, g_rsem, g_sent, g_lands, after)
        return {n: _blocks_to_full(n in _COL_SHARDED, b) for n, b in zip(_BIG[1:], blocks)}

    w = _layout_weights(full, w_in_blocks=w_in_blocks)
    loss_row, g, ctx = _forward_backward(x[0], p[0, 0], loss_target[0], w, late_weights, after=(g_token,))
    loss = lax.psum(loss_row[0, 0], ("x", "y", "c"))

    parts = {n: _full_to_blocks(n in _COL_SHARDED, g[n]) for n in _BIG[1:]}
    parts["w_main"] = g["w_main"][None]
    parts["w_dt"] = jnp.pad(_lanes_to_heads(g["w_dt"]), ((0, 0), (0, 128 - N_HEADS)))[None]
    names = ("w_main", "w_dt") + _BIG[1:]
    g5 = {n: parts[n].reshape(parts[n].shape[0], 2, parts[n].shape[1] // 2, parts[n].shape[2]) for n in names}
    to_sibling = [lax.dynamic_index_in_dim(g5[n], 1 - ci, axis=1, keepdims=False).astype(WIRE_DTYPE) for n in names]
    from_sibling = _swap_with_sibling(to_sibling)
    chip = {n: _chip_sum("chip_sum_" + n, g5[n], r, c_arr) for n, r in zip(names, from_sibling)}
    chip["w_in"] = _w_in_grad_blocks(chip["w_main"][0], chip["w_dt"][0])
    chip_wire = [chip[n] for n in _BIG]
    s_sem, r_sem, sent, lands, token = _scatter_blocks_start(chip_wire)
    grad_x, g["norm_g"] = _input_grad(x[0], w, ctx, after=(token,))
    g = _natural_grads(g)

    pieces = [_two_d(n, g[n]).reshape(-1) for n in _SMALL]
    sizes = [v.shape[0] for v in pieces]
    padded = [-(-s // _PACK) * _PACK for s in sizes]
    packed = jnp.concatenate([jnp.pad(v, (0, ps - s)) for v, s, ps in zip(pieces, sizes, padded)]).reshape(-1, 128)
    a_ssem, a_rsem, a_src, a_land, a_token = _small_gather_start(packed)

    sent, from_chips = _scatter_blocks_wait(s_sem, r_sem, sent, lands, (grad_x, a_token))
    halves = [_final_sum("final_sum_" + n, a, r, place_arr) for n, a, r in zip(_BIG, sent, from_chips)]
    grads = dict(zip(_BIG, _share_halves(halves)))
    delta, new_m, new_v = {}, {}, {}
    for n in _BIG:
        t = jnp.transpose if n == "w_in" else (lambda a: a)
        res = _adamw_call("adamw_" + n, t(shard[n]), t(grads[n]), t(mom[n][0]), t(vel[n][0]))
        delta[n], new_m[n], new_v[n], grads[n] = (t(r) for r in res)

    a_src, a_land = _small_gather_wait(a_ssem, a_rsem, a_src, a_land, delta["w_in"])
    summed = _small_sum(a_src, a_land, jnp.reshape(4 * xi + 2 * yi + ci, (1,)).astype(jnp.int32)).reshape(-1)
    off = 0
    for n, s, ps in zip(_SMALL, sizes, padded):
        grads[n] = summed[off:off + s].reshape(_two_d(n, g[n]).shape)
        off += ps
    grads["conv_w"] = lax.dynamic_slice_in_dim(grads["conv_w"], me * (CONV_DIM // N_CHIPS), CONV_DIM // N_CHIPS, axis=1)
    small = _adamw_small([_two_d(n, wt[n]) for n in _SMALL], [grads[n] for n in _SMALL],
                         [_two_d(n, mom[n]) for n in _SMALL], [_two_d(n, vel[n]) for n in _SMALL])
    for i, n in enumerate(_SMALL):
        delta[n], new_m[n], new_v[n] = small[0][i], small[1][i], small[2][i]

    def shaped(d):
        return [d[n].reshape(wt[n].shape) for n in _WEIGHTS]

    return (loss, grad_x[None], *shaped(grads), *shaped(delta), *shaped(new_m), *shaped(new_v))
```

```python
import functools

import jax
import jax.numpy as jnp
from jax import lax
from jax.experimental import pallas as pl
from jax.experimental.pallas import tpu as pltpu

F32 = jnp.float32
MXU_DTYPE = jnp.bfloat16
ACT_DTYPE = jnp.bfloat16
WIRE_DTYPE = jnp.bfloat16

D_MODEL = 1024
PLE_DIM = 256
CHUNK = 128
EPS = 1e-6
E_A = D_MODEL
G_A = 4
D_INNER = 2 * D_MODEL
HEAD_DIM = 64
N_HEADS = D_INNER // HEAD_DIM
N_STATE = 128
N_GROUPS = 4
HEADS_PER_GROUP = N_HEADS // N_GROUPS
PAIRS_PER_GROUP = HEADS_PER_GROUP // 2
CONV_K = 4
CONV_DIM = D_INNER + 2 * N_GROUPS * N_STATE
N_IN = 3 * E_A + D_INNER + CONV_DIM + N_HEADS + 2 * D_MODEL
N_CHIPS = 4
N_DEV = 8
W_IN_BLOCK = N_IN // N_CHIPS

UVZ_W, XBC_W, ZB_W, G_W = 3 * E_A, CONV_DIM, D_INNER, 2 * D_MODEL
MAIN_W = UVZ_W + XBC_W + ZB_W + G_W
UVZ_CB, XBC_CB, ZB_CB, G_CB = 0, 1, 3, 4
DT_W = N_GROUPS * 128

ADAM_LR, ADAM_B1, ADAM_B2, ADAM_EPS, ADAM_WD, ADAM_STEP = 0.001, 0.9, 0.999, 1e-08, 0.01, 10

MESH = pl.DeviceIdType.MESH
ANY = pl.BlockSpec(memory_space=pl.ANY)


def _mxu(v):
    return v.astype(MXU_DTYPE)


def _dot(a, b, dims=(((1,), (0,)), ((), ()))):
    return lax.dot_general(_mxu(a), _mxu(b), dims, preferred_element_type=F32)


V7X_MXU_WIDTH = 256
V7X_SCOPED_VMEM_BYTES = 60000 * 1024
MM_TILE = 4 * V7X_MXU_WIDTH
MM_VMEM_BUDGET = 4 * V7X_SCOPED_VMEM_BYTES // 5


def _mm_tk(m, n, k, tm, tn, a_bytes, b_bytes, out_bytes, extra_bytes):
    one_tile = m == tm and n == tn
    for parts in range(2 if one_tile else 1, k // 128 + 1):
        if k % parts or (k // parts) % 128 and parts > 1:
            continue
        tk = k // parts
        a_bufs = 1 if (parts == 1 and m == tm) else 2
        b_bufs = 1 if (parts == 1 and n == tn) else 2
        need = (tk * (a_bufs * tm * a_bytes + b_bufs * tn * b_bytes) + 2 * tm * tn * (out_bytes + extra_bytes)
                + (tm * tn * 4 if parts > 1 else 0))
        if need <= MM_VMEM_BUDGET:
            return tk
    return 128


def _matmul(a, b, *, mode, name, out_dtype, m, n, k, tm=MM_TILE, tn=MM_TILE, tk=None, a_off=0, b_off=0,
            extras=(), epilogue=None, after=()):
    tm, tn = min(tm, m), min(tn, n)
    if tk is None:
        tk = _mm_tk(m, n, k, tm, tn, a.dtype.itemsize, b.dtype.itemsize, jnp.dtype(out_dtype).itemsize,
                    sum(e.dtype.itemsize for e in extras))
    tk = min(tk, k)
    assert m % tm == 0 and n % tn == 0 and k % tk == 0, (name, m, n, k, tm, tn, tk)
    nk = k // tk
    a_mode = pl.Buffered(1) if (nk == 1 and m == tm) else None
    b_mode = pl.Buffered(1) if (nk == 1 and n == tn) else None
    if mode == "nn":
        assert a_off % tk == 0 and b_off % tn == 0
        a_spec = pl.BlockSpec((tm, tk), lambda i, j, kk: (i, kk + a_off // tk), pipeline_mode=a_mode)
        b_spec = pl.BlockSpec((tk, tn), lambda i, j, kk: (kk, j + b_off // tn), pipeline_mode=b_mode)
        dims = (((1,), (0,)), ((), ()))
    elif mode == "nt":
        a_spec = pl.BlockSpec((tm, tk), lambda i, j, kk: (i, kk), pipeline_mode=a_mode)
        b_spec = pl.BlockSpec((tn, tk), lambda i, j, kk: (j, kk), pipeline_mode=b_mode)
        dims = (((1,), (1,)), ((), ()))
    else:
        assert a_off % tm == 0 and b_off % tn == 0
        a_spec = pl.BlockSpec((tk, tm), lambda i, j, kk: (kk, i + a_off // tm), pipeline_mode=a_mode)
        b_spec = pl.BlockSpec((tk, tn), lambda i, j, kk: (kk, j + b_off // tn), pipeline_mode=b_mode)
        dims = (((0,), (0,)), ((), ()))
    ne = len(extras)

    def finish(acc, extra_refs, o_ref):
        res = acc if epilogue is None else epilogue(acc, *[e[...] for e in extra_refs])
        o_ref[...] = res.astype(o_ref.dtype)

    def body(a_ref, b_ref, *rest):
        extra_refs, o_ref = rest[:ne], rest[ne + len(after)]
        part = _dot(a_ref[...], b_ref[...], dims)
        if nk == 1:
            finish(part, extra_refs, o_ref)
            return
        acc_ref = rest[ne + len(after) + 1]
        kk = pl.program_id(2)

        @pl.when(kk == 0)
        def _():
            acc_ref[...] = part

        @pl.when(kk > 0)
        def _():
            acc_ref[...] += part

        @pl.when(kk == nk - 1)
        def _():
            finish(acc_ref[...], extra_refs, o_ref)

    o_spec = pl.BlockSpec((tm, tn), lambda i, j, kk: (i, j))
    return pl.pallas_call(
        body, name=name, grid=(m // tm, n // tn, nk),
        in_specs=[a_spec, b_spec] + [o_spec] * ne + [ANY] * len(after), out_specs=o_spec,
        out_shape=jax.ShapeDtypeStruct((m, n), out_dtype),
        scratch_shapes=[pltpu.VMEM((tm, tn), F32)] if nk > 1 else [],
        compiler_params=pltpu.CompilerParams(dimension_semantics=("parallel", "parallel", "arbitrary")),
    )(a, b, *extras, *after)


def _rows_matmul(name, f, rows, pars, b, *, out_dtype, n, k, tm, nrows, tn=MM_TILE, extras=(), epilogue=None, after=(),
                 side=None):
    tm, tn = min(tm, nrows), min(tn, n)
    assert nrows % tm == 0 and n % tn == 0, (name, nrows, n, tm, tn)
    nr, npar, ne, nj = len(rows), len(pars), len(extras), n // tn
    ns = 0 if side is None else 1
    n_in = nr + npar + 1 + ne + ns + len(after)

    def body(*refs):
        row_refs, par_refs, b_ref = refs[:nr], refs[nr:nr + npar], refs[nr + npar]
        extra_refs = refs[nr + npar + 1:nr + npar + 1 + ne]
        a_ref, o_ref = refs[n_in], refs[n_in + 1]

        def make_a():
            a = f(*[r[...].astype(F32) for r in row_refs], *[p[...] for p in par_refs])[0]
            a_ref[...] = a.astype(a_ref.dtype)
            if ns:
                refs[n_in + 2][...] = _dot(a_ref[...], refs[nr + npar + 1 + ne][...]).astype(refs[n_in + 2].dtype)

        if nj == 1:
            make_a()
        else:
            pl.when(pl.program_id(1) == 0)(make_a)
        res = _dot(a_ref[...], b_ref[...])
        if epilogue is not None:
            res = epilogue(res, *[e[...] for e in extra_refs])
        o_ref[...] = res.astype(o_ref.dtype)

    o_spec = pl.BlockSpec((tm, tn), lambda i, j: (i, j))
    side_in = [] if side is None else [pl.BlockSpec(tuple(side[0].shape), lambda i, j: (0, 0))]
    side_out = [] if side is None else [pl.BlockSpec((tm, side[0].shape[1]), lambda i, j: (i, 0))]
    side_shape = [] if side is None else [jax.ShapeDtypeStruct((nrows, side[0].shape[1]), side[1])]
    return pl.pallas_call(
        body, name=name, grid=(nrows // tm, nj),
        in_specs=[pl.BlockSpec((tm, w), lambda i, j, cb=cb: (i, cb)) for _, w, cb in rows]
        + [pl.BlockSpec(tuple(p.shape), lambda i, j, nd=p.ndim: (0,) * nd) for p in pars]
        + [pl.BlockSpec((k, tn), lambda i, j: (0, j))] + [o_spec] * ne + side_in + [ANY] * len(after),
        out_specs=[pl.BlockSpec((tm, k), lambda i, j: (i, 0)), o_spec] + side_out,
        out_shape=[jax.ShapeDtypeStruct((nrows, k), ACT_DTYPE), jax.ShapeDtypeStruct((nrows, n), out_dtype)] + side_shape,
        compiler_params=pltpu.CompilerParams(dimension_semantics=("parallel", "arbitrary")),
    )(*[r[0] for r in rows], *pars, b, *extras, *([] if side is None else [side[0]]), *after)


def _row_spec(tm, width, cb):
    return pl.BlockSpec((tm, width), lambda i: (i, cb))


def _whole_spec(shape):
    nd = len(shape)
    return pl.BlockSpec(tuple(shape), lambda i: (0,) * nd)


def _rows_vjp_call(name, f, rows, pars, cots, drows, *, tm, nrows, cot_mm=None):
    tm = min(tm, nrows)
    nr, npar, nc = len(rows), len(pars), len(cots)
    mm_args, mm_specs = [], []
    if cot_mm is not None:
        mm_a, mm_b, mm_add = cot_mm
        mm_args = [mm_a, mm_b] + ([] if mm_add is None else [mm_add])
        mm_specs = [_row_spec(tm, mm_a.shape[1], 0), _whole_spec(mm_b.shape)]
        mm_specs += [] if mm_add is None else [_row_spec(tm, mm_b.shape[0], 0)]
    alias_bufs, aliases = [], {}
    out_shape, out_specs = [], []
    for (arr, w, cb), d in zip(rows, drows):
        if d is None:
            continue
        dt, into = d
        if into is None:
            out_shape.append(jax.ShapeDtypeStruct((nrows, w), dt))
            out_specs.append(_row_spec(tm, w, 0))
        else:
            buf, total, ocb = into
            if buf is not None:
                aliases[nr + npar + nc + len(alias_bufs)] = len(out_shape)
                alias_bufs.append(buf)
            out_shape.append(jax.ShapeDtypeStruct((nrows, total), dt))
            out_specs.append(_row_spec(tm, w, ocb))
    n_drow = len(out_shape)
    for p in pars:
        out_shape.append(jax.ShapeDtypeStruct(p.shape, F32))
        out_specs.append(_whole_spec(p.shape))
    na = len(alias_bufs)

    def body(*refs):
        rv = [r[...].astype(F32) for r in refs[:nr]]
        pv = [p[...] for p in refs[nr:nr + npar]]
        cv = tuple(c[...].astype(F32) for c in refs[nr + npar:nr + npar + nc])
        o_refs = refs[nr + npar + nc + na + len(mm_args):]
        if mm_args:
            mm_refs = refs[nr + npar + nc + na:nr + npar + nc + na + len(mm_args)]
            c0 = _dot(mm_refs[0][...], mm_refs[1][...], (((1,), (1,)), ((), ())))
            if len(mm_refs) == 3:
                c0 = c0 + mm_refs[2][...].astype(F32)
            cv = (c0,) + cv
        _, vjp = jax.vjp(f, *rv, *pv)
        g = vjp(cv)
        oi = 0
        for ri, d in enumerate(drows):
            if d is not None:
                o_refs[oi][...] = g[ri].astype(o_refs[oi].dtype)
                oi += 1
        first = pl.program_id(0) == 0
        for pi in range(npar):
            acc = o_refs[n_drow + pi]

            @pl.when(first)
            def _(acc=acc):
                acc[...] = jnp.zeros_like(acc)

            acc[...] += g[nr + pi]

    return pl.pallas_call(
        body, name=name, grid=(nrows // tm,),
        in_specs=[_row_spec(tm, w, cb) for _, w, cb in rows] + [_whole_spec(p.shape) for p in pars]
        + [_row_spec(tm, w, cb) for _, w, cb in cots] + [ANY] * na + mm_specs,
        out_specs=out_specs, out_shape=out_shape, input_output_aliases=aliases,
        compiler_params=pltpu.CompilerParams(dimension_semantics=("arbitrary",)),
    )(*[r[0] for r in rows], *pars, *[c[0] for c in cots], *alias_bufs, *mm_args)


def _rms(x, g):
    return x * lax.rsqrt(jnp.mean(x * x, axis=-1, keepdims=True) + EPS) * g


def _f_rms(x, g):
    return (_rms(x, g),)


def _tril_mask():
    return lax.broadcasted_iota(jnp.int32, (CHUNK, CHUNK), 0) >= lax.broadcasted_iota(jnp.int32, (CHUNK, CHUNK), 1)


def _f_branch_a(uvz, ln_g, ln_b, w_s, b_s):
    u = jax.nn.gelu(uvz[:, :E_A])
    v = jax.nn.gelu(uvz[:, E_A:2 * E_A])
    z = uvz[:, 2 * E_A:]
    xc = v - jnp.mean(v, axis=-1, keepdims=True)
    vn = xc * lax.rsqrt(jnp.mean(xc * xc, axis=-1, keepdims=True) + EPS) * ln_g + ln_b
    mask = _tril_mask()
    ws = [jnp.where(mask, w_s[g], 0.0) for g in range(G_A)]
    gw = E_A // G_A
    rows = []
    for c in range(uvz.shape[0] // CHUNK):
        vc = vn[c * CHUNK:(c + 1) * CHUNK]
        rows.append(jnp.concatenate([_dot(ws[g], vc[:, g * gw:(g + 1) * gw]) + b_s[g] for g in range(G_A)], axis=1))
    sv = rows[0] if len(rows) == 1 else jnp.concatenate(rows, axis=0)
    return (u * sv * jax.nn.silu(z),)


def _f_gnorm(y, zb, g):
    yz = y * jax.nn.silu(zb)
    gw = D_INNER // N_GROUPS
    parts = []
    for i in range(N_GROUPS):
        s = yz[:, i * gw:(i + 1) * gw]
        parts.append(s * lax.rsqrt(jnp.mean(s * s, axis=-1, keepdims=True) + EPS))
    return (jnp.concatenate(parts, axis=1) * g,)


def _f_merge(g2, oa, ob):
    return (jax.nn.sigmoid(g2[:, :D_MODEL]) * oa + jax.nn.sigmoid(g2[:, D_MODEL:]) * ob,)


def _f_loss(x1, gp, pe, tgt, fg):
    x2 = x1 + jax.nn.sigmoid(gp) * pe
    err = _rms(x2, fg) - tgt
    return 0.5 * jnp.sum(jnp.mean(err * err, axis=-1))


def _head(x1, p, tgt, ple_g, w_pg, w_ple, fg, *, tm, nrows):
    tm = min(tm, nrows)

    def body(x1_ref, p_ref, t_ref, pg_ref, wpg_ref, wple_ref, fg_ref, hp_ref, dx_ref, dgp_ref, dpe_ref, dfg_ref, loss_ref):
        x1 = x1_ref[...]
        hp_ref[...] = _rms(x1, pg_ref[...]).astype(hp_ref.dtype)
        gp = _dot(hp_ref[...], wpg_ref[...])
        pe = _dot(p_ref[...], wple_ref[...])
        loss, vjp = jax.vjp(_f_loss, x1, gp, pe, t_ref[...], fg_ref[...])
        dx, dgp, dpe, _, dfg = vjp(jnp.ones((), F32))
        dx_ref[...] = dx
        dgp_ref[...] = dgp.astype(dgp_ref.dtype)
        dpe_ref[...] = dpe.astype(dpe_ref.dtype)

        @pl.when(pl.program_id(0) == 0)
        def _():
            dfg_ref[...] = jnp.zeros_like(dfg_ref)
            loss_ref[...] = jnp.zeros_like(loss_ref)

        dfg_ref[...] += dfg
        loss_ref[...] += jnp.full(loss_ref.shape, loss, F32)

    row = _row_spec(tm, D_MODEL, 0)
    act = jax.ShapeDtypeStruct((nrows, D_MODEL), ACT_DTYPE)
    return pl.pallas_call(
        body, name="head", grid=(nrows // tm,),
        in_specs=[row, _row_spec(tm, PLE_DIM, 0), row, _whole_spec((1, D_MODEL)), _whole_spec(w_pg.shape),
                  _whole_spec(w_ple.shape), _whole_spec((1, D_MODEL))],
        out_specs=[row, row, row, row, _whole_spec((1, D_MODEL)), _whole_spec((1, 128))],
        out_shape=[act, jax.ShapeDtypeStruct((nrows, D_MODEL), F32), act, act, jax.ShapeDtypeStruct((1, D_MODEL), F32),
                   jax.ShapeDtypeStruct((1, 128), F32)],
        compiler_params=pltpu.CompilerParams(dimension_semantics=("arbitrary",)),
    )(x1, p, tgt, ple_g, w_pg, w_ple, fg)


def _shift_rows(cur, edge, j, up):
    tm = cur.shape[0]
    row = lax.broadcasted_iota(jnp.int32, cur.shape, 0)
    if up:
        sh = pltpu.roll(cur, tm - j, 0)
        e = jnp.tile(pltpu.roll(edge, 8 - j, 0), (tm // 8, 1))
        return jnp.where(row >= tm - j, e, sh)
    sh = pltpu.roll(cur, j, 0)
    e = jnp.tile(pltpu.roll(edge, j, 0), (tm // 8, 1))
    return jnp.where(row < j, e, sh)


def _conv_pre(cur, prev, w, b):
    acc = cur * w[CONV_K - 1:CONV_K] + b
    for j in range(1, CONV_K):
        acc = acc + _shift_rows(cur, prev, j, up=False) * w[CONV_K - 1 - j:CONV_K - j]
    return acc


def _halo_specs(tm, nrows, cb, before):
    nb = tm // 8
    last = nrows // 8 - 1
    if before:
        return pl.BlockSpec((8, XBC_W), lambda i: (jnp.maximum(i * nb - 1, 0), cb))
    return pl.BlockSpec((8, XBC_W), lambda i: (jnp.minimum((i + 1) * nb, last), cb))


def _conv_fwd(proj, conv_w, conv_b, *, tm, nrows):
    tm = min(tm, nrows)

    def body(cur_ref, prev_ref, w_ref, b_ref, o_ref, pre_ref):
        prev = jnp.where(pl.program_id(0) == 0, 0.0, prev_ref[...].astype(F32))
        pre = _conv_pre(cur_ref[...].astype(F32), prev, w_ref[...], b_ref[...])
        o_ref[...] = jax.nn.silu(pre).astype(o_ref.dtype)
        pre_ref[...] = pre.astype(pre_ref.dtype)

    out = jax.ShapeDtypeStruct((nrows, XBC_W), ACT_DTYPE)
    return pl.pallas_call(
        body, name="conv_fwd", grid=(nrows // tm,),
        in_specs=[_row_spec(tm, XBC_W, XBC_CB), _halo_specs(tm, nrows, XBC_CB, True),
                  _whole_spec((CONV_K, XBC_W)), _whole_spec((1, XBC_W))],
        out_specs=[_row_spec(tm, XBC_W, 0)] * 2, out_shape=[out, out],
        compiler_params=pltpu.CompilerParams(dimension_semantics=("parallel",)),
    )(proj, proj, conv_w, conv_b)


def _conv_bwd_act(pre, dact, *, tm, nrows):
    tm = min(tm, nrows)
    nb = N_GROUPS * N_STATE

    def body(pre_ref, dxs_ref, dbm_ref, dcm_ref, dpre_ref, db_ref):
        pre = pre_ref[...].astype(F32)
        sg = jax.nn.sigmoid(pre)
        dy = jnp.concatenate([dxs_ref[...], dbm_ref[...], dcm_ref[...]], axis=1).astype(F32)
        dpre = dy * sg * (1.0 + pre * (1.0 - sg))
        dpre_ref[...] = dpre.astype(dpre_ref.dtype)

        @pl.when(pl.program_id(0) == 0)
        def _():
            db_ref[...] = jnp.zeros_like(db_ref)

        db_ref[...] += jnp.sum(dpre, axis=0, keepdims=True)

    return pl.pallas_call(
        body, name="conv_bwd_act", grid=(nrows // tm,),
        in_specs=[_row_spec(tm, XBC_W, 0), _row_spec(tm, D_INNER, 0), _row_spec(tm, nb, 0), _row_spec(tm, nb, 0)],
        out_specs=[_row_spec(tm, XBC_W, 0), _whole_spec((1, XBC_W))],
        out_shape=[jax.ShapeDtypeStruct((nrows, XBC_W), ACT_DTYPE), jax.ShapeDtypeStruct((1, XBC_W), F32)],
        compiler_params=pltpu.CompilerParams(dimension_semantics=("arbitrary",)),
    )(pre, *dact)


def _conv_bwd_x(dpre, proj, conv_w, dproj, *, tm, nrows):
    tm = min(tm, nrows)
    ntiles = nrows // tm

    def body(cur_ref, nxt_ref, x_ref, w_ref, _, o_ref, dw_ref):
        cur = cur_ref[...].astype(F32)
        nxt = jnp.where(pl.program_id(0) == ntiles - 1, 0.0, nxt_ref[...].astype(F32))
        x = x_ref[...].astype(F32)
        w = w_ref[...]

        @pl.when(pl.program_id(0) == 0)
        def _():
            dw_ref[...] = jnp.zeros_like(dw_ref)

        acc = cur * w[CONV_K - 1:CONV_K]
        dw_ref[CONV_K - 1:CONV_K, :] += jnp.sum(cur * x, axis=0, keepdims=True)
        for j in range(1, CONV_K):
            u = _shift_rows(cur, nxt, j, up=True)
            acc = acc + u * w[CONV_K - 1 - j:CONV_K - j]
            dw_ref[CONV_K - 1 - j:CONV_K - j, :] += jnp.sum(u * x, axis=0, keepdims=True)
        o_ref[...] = acc.astype(o_ref.dtype)

    return pl.pallas_call(
        body, name="conv_bwd_x", grid=(ntiles,),
        in_specs=[_row_spec(tm, XBC_W, 0), _halo_specs(tm, nrows, 0, False), _row_spec(tm, XBC_W, XBC_CB),
                  _whole_spec((CONV_K, XBC_W)), ANY],
        out_specs=[_row_spec(tm, XBC_W, XBC_CB), _whole_spec((CONV_K, XBC_W))],
        out_shape=[jax.ShapeDtypeStruct(dproj.shape, dproj.dtype), jax.ShapeDtypeStruct((CONV_K, XBC_W), F32)],
        input_output_aliases={4: 0},
        compiler_params=pltpu.CompilerParams(dimension_semantics=("arbitrary",)),
    )(dpre, dpre, proj, conv_w, dproj)


SSD_SPAN = 4
SSD_FWD_SPANS = 2
_XS_GW = D_INNER // N_GROUPS
_NT = (((1,), (1,)), ((), ()))
_TN = (((0,), (0,)), ((), ()))


def _bf16_terms(x, terms):
    parts, rest = [], x
    for _ in range(terms):
        part = rest.astype(jnp.bfloat16)
        parts.append(part)
        rest = rest - part.astype(F32)
    return parts


def _head_lane_matrix():
    return (lax.broadcasted_iota(jnp.int32, (128, _XS_GW), 0)
            == lax.broadcasted_iota(jnp.int32, (128, _XS_GW), 1) // HEAD_DIM).astype(jnp.bfloat16)


@functools.partial(jax.custom_vjp, nondiff_argnums=(1,))
def _head_lanes(cols, terms):
    e = _head_lane_matrix()
    return sum(jnp.dot(t, e, preferred_element_type=F32) for t in _bf16_terms(cols, terms))


def _head_lanes_fwd(cols, terms):
    return _head_lanes(cols, terms), None


def _head_lanes_bwd(terms, _, g):
    e = _head_lane_matrix()
    return (sum(lax.dot_general(t, e, _NT, preferred_element_type=F32) for t in _bf16_terms(g, 2)),)


_head_lanes.defvjp(_head_lanes_fwd, _head_lanes_bwd)


def _ssd_chunk(k, xs, bm, cm, dtr, hprev, dtb, alog, dsk):
    causal, tri, lo = k
    dt = jax.nn.softplus(dtr + dtb)
    da = dt * (-jnp.exp(alog))
    cs = jnp.dot(tri, da, precision=lax.Precision.HIGHEST, preferred_element_type=F32)
    cst = cs.T
    cs_l = _head_lanes(cs, 3)
    xdt = xs * _head_lanes(dt, 2)
    cb = _dot(cm, bm, _NT)
    yd = []
    for q in range(PAIRS_PER_GROUP):
        xq = xdt[:, 128 * q:128 * (q + 1)]
        y2 = [_dot(cb * jnp.exp(jnp.where(causal, cs[:, h:h + 1] - cst[h:h + 1, :], -jnp.inf)), xq)
              for h in (2 * q, 2 * q + 1)]
        yd.append(jnp.where(lo, y2[0], y2[1]))
    y_off = jnp.exp(cs_l) * _dot(cm, hprev, _NT)
    st = _dot(xdt * jnp.exp(cs_l[CHUNK - 1:CHUNK, :] - cs_l), bm, _TN)
    cdec = jnp.exp(cs[CHUNK - 1:CHUNK, :])
    cd_rows = jnp.concatenate(
        [jnp.broadcast_to(cdec[:, h:h + 1], (HEAD_DIM, N_STATE)) for h in range(HEADS_PER_GROUP)], axis=0)
    dsk_l = _head_lanes(jnp.broadcast_to(dsk, (8, 128)), 2)[:1]
    y = jnp.concatenate(yd, axis=1) + y_off + xs * dsk_l
    return y, cd_rows * hprev + st


def _ssd_span(xs, bm, cm, dtr, h0, dtb, alog, dsk):
    li = lax.broadcasted_iota(jnp.int32, (CHUNK, CHUNK), 0)
    si = lax.broadcasted_iota(jnp.int32, (CHUNK, CHUNK), 1)
    causal = li >= si
    k = (causal, causal.astype(F32), si < HEAD_DIM)
    h, ys = h0, []
    for t in range(xs.shape[0] // CHUNK):
        r = slice(t * CHUNK, (t + 1) * CHUNK)
        y, h = _ssd_chunk(k, xs[r], bm[r], cm[r], dtr[r], h, dtb, alog, dsk)
        ys.append(y)
    return (ys[0] if len(ys) == 1 else jnp.concatenate(ys, axis=0)), h


def _ssd_specs(rev, nsteps, rows):
    def s_of(s):
        return nsteps - 1 - s if rev else s

    xs = pl.BlockSpec((rows, _XS_GW), lambda g, s: (s_of(s), g))
    bm = pl.BlockSpec((rows, N_STATE), lambda g, s: (s_of(s), D_INNER // N_STATE + g))
    cm = pl.BlockSpec((rows, N_STATE), lambda g, s: (s_of(s), D_INNER // N_STATE + N_GROUPS + g))
    dt = pl.BlockSpec((rows, 128), lambda g, s: (s_of(s), g))
    par = pl.BlockSpec((1, 128), lambda g, s: (0, g))
    st = pl.BlockSpec((None, None, _XS_GW, N_STATE), lambda g, s: (g, s_of(s), 0, 0))
    return xs, bm, cm, dt, par, st


def _ssd_fwd(act, dtr, dtb, alog, dsk, *, nrows):
    span = CHUNK * min(SSD_SPAN, nrows // CHUNK)
    per_step = SSD_FWD_SPANS if nrows % (SSD_FWD_SPANS * span) == 0 else 1
    rows = per_step * span
    nsteps = nrows // rows
    xs, bm, cm, dt, par, _ = _ssd_specs(False, nsteps, rows)
    st = pl.BlockSpec((None, per_step, _XS_GW, N_STATE), lambda g, s: (g, s, 0, 0))

    def body(xs_ref, b_ref, c_ref, dt_ref, dtb_ref, al_ref, dk_ref, y_ref, st_ref, h_ref):
        @pl.when(pl.program_id(1) == 0)
        def _():
            h_ref[...] = jnp.zeros_like(h_ref)

        h = h_ref[...]
        for i in range(per_step):
            r = slice(i * span, (i + 1) * span)
            st_ref[i] = h
            y, h = _ssd_span(xs_ref[r, :].astype(F32), b_ref[r, :].astype(F32), c_ref[r, :].astype(F32), dt_ref[r, :],
                             h, dtb_ref[...], al_ref[...], dk_ref[...])
            y_ref[r, :] = y.astype(y_ref.dtype)
        h_ref[...] = h

    return pl.pallas_call(
        body, name="ssd_fwd", grid=(N_GROUPS, nsteps),
        in_specs=[xs, bm, cm, dt, par, par, par], out_specs=[xs, st],
        out_shape=[jax.ShapeDtypeStruct((nrows, D_INNER), ACT_DTYPE),
                   jax.ShapeDtypeStruct((N_GROUPS, nrows // span, _XS_GW, N_STATE), F32)],
        scratch_shapes=[pltpu.VMEM((_XS_GW, N_STATE), F32)],
        compiler_params=pltpu.CompilerParams(dimension_semantics=("arbitrary", "arbitrary")),
    )(act, act, act, dtr, dtb, alog, dsk)


def _ssd_bwd(act, dtr, dtb, alog, dsk, states, dy, *, nrows):
    rows = CHUNK * min(SSD_SPAN, nrows // CHUNK)
    nsteps = nrows // rows
    xs, bm, cm, dt, par, st = _ssd_specs(True, nsteps, rows)

    def body(xs_ref, b_ref, c_ref, dt_ref, dtb_ref, al_ref, dk_ref, st_ref, dy_ref,
             dxs_ref, db_ref, dc_ref, ddt_ref, ddtb_ref, dal_ref, ddk_ref, dh_ref):
        @pl.when(pl.program_id(1) == 0)
        def _():
            dh_ref[...] = jnp.zeros_like(dh_ref)
            ddtb_ref[...] = jnp.zeros_like(ddtb_ref)
            dal_ref[...] = jnp.zeros_like(dal_ref)
            ddk_ref[...] = jnp.zeros_like(ddk_ref)

        _, vjp = jax.vjp(_ssd_span, xs_ref[...].astype(F32), b_ref[...].astype(F32), c_ref[...].astype(F32),
                         dt_ref[...], st_ref[...], dtb_ref[...], al_ref[...], dk_ref[...])
        dxs, db, dc, ddt, dh, ddtb, dal, ddk = vjp((dy_ref[...].astype(F32), dh_ref[...]))
        dxs_ref[...] = dxs.astype(dxs_ref.dtype)
        db_ref[...] = db.astype(db_ref.dtype)
        dc_ref[...] = dc.astype(dc_ref.dtype)
        ddt_ref[...] = ddt
        dh_ref[...] = dh
        ddtb_ref[...] += ddtb
        dal_ref[...] += dal
        ddk_ref[...] += ddk

    nb = N_GROUPS * N_STATE
    bspec = pl.BlockSpec((rows, N_STATE), lambda g, s: (nsteps - 1 - s, g))
    return pl.pallas_call(
        body, name="ssd_bwd", grid=(N_GROUPS, nsteps),
        in_specs=[xs, bm, cm, dt, par, par, par, st, xs],
        out_specs=[xs, bspec, bspec, dt, par, par, par],
        out_shape=[jax.ShapeDtypeStruct((nrows, D_INNER), ACT_DTYPE), jax.ShapeDtypeStruct((nrows, nb), ACT_DTYPE),
                   jax.ShapeDtypeStruct((nrows, nb), ACT_DTYPE), jax.ShapeDtypeStruct((nrows, DT_W), F32),
                   jax.ShapeDtypeStruct((1, DT_W), F32), jax.ShapeDtypeStruct((1, DT_W), F32),
                   jax.ShapeDtypeStruct((1, DT_W), F32)],
        scratch_shapes=[pltpu.VMEM((_XS_GW, N_STATE), F32)],
        compiler_params=pltpu.CompilerParams(dimension_semantics=("arbitrary", "arbitrary")),
    )(act, act, act, dtr, dtb, alog, dsk, states, dy)


def _add_epilogue(acc, r):
    return r + acc


def _rms_and_skip(x, g):
    return _rms(x, g), x


def _forward_backward(x, p, tgt, w, late_weights=None, after=()):
    s = x.shape[0]
    act_t, f32 = ACT_DTYPE, F32
    mm = functools.partial(_matmul)
    h, proj, dtr = _rows_matmul("proj", _f_rms, [(x, D_MODEL, 0)], [w["norm_g"]], w["w_main"], out_dtype=act_t,
                                n=MAIN_W, k=D_MODEL, tm=1024, tn=2 * MM_TILE, nrows=s, after=after, side=(w["w_dt"], f32))
    act, conv_pre = _conv_fwd(proj, w["conv_w"], w["conv_b"], tm=512, nrows=s)
    y, states = _ssd_fwd(act, dtr, w["dt_bias"], w["a_log"], w["d_skip"], nrows=s)
    if late_weights is not None:
        w = {**w, **late_weights(states)}
    a_pars = [w["ln_a_g"], w["ln_a_b"], w["w_s"], w["b_s"]]
    y_a, o_a = _rows_matmul("out_a", _f_branch_a, [(proj, UVZ_W, UVZ_CB)], a_pars, w["w_oa"], out_dtype=act_t,
                            n=D_MODEL, k=E_A, tm=512, nrows=s)
    gn_rows = [(y, D_INNER, 0), (proj, ZB_W, ZB_CB)]
    y_b, o_b = _rows_matmul("out_b", _f_gnorm, gn_rows, [w["ssm_norm_g"]], w["w_ob"], out_dtype=act_t,
                            n=D_MODEL, k=D_INNER, tm=1024, nrows=s)
    mg_rows = [(proj, G_W, G_CB), (o_a, D_MODEL, 0), (o_b, D_MODEL, 0)]
    merged, x1 = _rows_matmul("out_proj", _f_merge, mg_rows, [], w["w_out"], out_dtype=f32, n=D_MODEL, k=D_MODEL,
                              tm=1024, nrows=s, extras=(x,), epilogue=_add_epilogue)
    g = {}
    hp, dx2, dgp, dpe, g["final_g"], loss = _head(x1, p, tgt, w["ple_norm_g"], w["w_pg"], w["w_ple"], w["final_g"],
                                                   tm=512, nrows=s)
    g["w_pg"] = mm(hp, dgp, mode="tn", name="d_w_pg", out_dtype=f32, m=D_MODEL, n=D_MODEL, k=s, tn=MM_TILE // 2)
    g["w_ple"] = mm(p, dpe, mode="tn", name="d_w_ple", out_dtype=f32, m=PLE_DIM, n=D_MODEL, k=s)
    dx1, g["ple_norm_g"] = _rows_vjp_call(
        "ple_norm_bwd", _rms_and_skip, [(x1, D_MODEL, 0)], [w["ple_norm_g"]], [(dx2, D_MODEL, 0)],
        [(f32, None)], tm=1024, nrows=s, cot_mm=(dgp, w["w_pg"], None))
    g["w_out"] = mm(merged, dx1, mode="tn", name="d_w_out", out_dtype=f32, m=D_MODEL, n=D_MODEL, k=s)
    dproj, do_a, do_b = _rows_vjp_call(
        "merge_bwd", _f_merge, mg_rows, [], [],
        [(act_t, (None, MAIN_W, G_CB)), (act_t, None), (act_t, None)], tm=1024, nrows=s, cot_mm=(dx1, w["w_out"], None))
    g["w_oa"] = mm(y_a, do_a, mode="tn", name="d_w_oa", out_dtype=f32, m=E_A, n=D_MODEL, k=s, tn=MM_TILE // 2)
    g["w_ob"] = mm(y_b, do_b, mode="tn", name="d_w_ob", out_dtype=f32, m=D_INNER, n=D_MODEL, k=s)
    dy, dproj, g["ssm_norm_g"] = _rows_vjp_call(
        "gnorm_bwd", _f_gnorm, gn_rows, [w["ssm_norm_g"]], [],
        [(act_t, None), (act_t, (dproj, MAIN_W, ZB_CB))], tm=512, nrows=s, cot_mm=(do_b, w["w_ob"], None))
    dxs, dbm, dcm, ddtr, g["dt_bias"], g["a_log"], g["d_skip"] = _ssd_bwd(
        act, dtr, w["dt_bias"], w["a_log"], w["d_skip"], states, dy, nrows=s)
    dpre, g["conv_b"] = _conv_bwd_act(conv_pre, (dxs, dbm, dcm), tm=1024, nrows=s)
    dproj, g["conv_w"] = _conv_bwd_x(dpre, proj, w["conv_w"], dproj, tm=512, nrows=s)
    dproj, g["ln_a_g"], g["ln_a_b"], g["w_s"], g["b_s"] = _rows_vjp_call(
        "branch_a_bwd", _f_branch_a, [(proj, UVZ_W, UVZ_CB)], a_pars, [],
        [(act_t, (dproj, MAIN_W, UVZ_CB))], tm=512, nrows=s, cot_mm=(do_a, w["w_oa"], None))
    g["w_main"] = mm(h, dproj, mode="tn", name="d_w_main", out_dtype=f32, m=D_MODEL, n=MAIN_W, k=s, tn=MM_TILE // 2)
    g["w_dt"] = mm(h, ddtr, mode="tn", name="d_w_dt", out_dtype=f32, m=D_MODEL, n=DT_W, k=s)
    return loss, g, (dproj, ddtr, dx1)


def _input_grad(x, w, ctx, after=()):
    dproj, ddtr, dx1 = ctx
    s = x.shape[0]
    dh = _matmul(dproj, w["w_main"], mode="nt", name="d_h_main", out_dtype=F32, m=s, n=D_MODEL, k=MAIN_W,
                 tm=MM_TILE // 2, after=after)
    return _rows_vjp_call(
        "pre_norm_bwd", _rms_and_skip, [(x, D_MODEL, 0)], [w["norm_g"]], [(dx1, D_MODEL, 0)],
        [(F32, None)], tm=1024, nrows=s, cot_mm=(ddtr, w["w_dt"], dh))


def _local_step(x, p, tgt, w):
    loss, g, ctx = _forward_backward(x, p, tgt, w)
    grad_x, g["norm_g"] = _input_grad(x, w, ctx)
    return loss, grad_x, g


_O_ZB = 3 * E_A
_O_XBC = _O_ZB + D_INNER
_O_DT = _O_XBC + CONV_DIM
_O_G = _O_DT + N_HEADS


def _heads_to_lanes(v):
    r = v.shape[0]
    v = v.reshape(r, N_GROUPS, HEADS_PER_GROUP)
    return jnp.pad(v, ((0, 0), (0, 0), (0, 128 - HEADS_PER_GROUP))).reshape(r, DT_W)


def _lanes_to_heads(v):
    r = v.shape[0]
    return v.reshape(r, N_GROUPS, 128)[:, :, :HEADS_PER_GROUP].reshape(r, N_HEADS)


def _block_cols(blocks, a, b):
    parts = []
    for k in range(N_CHIPS):
        lo, hi = max(a, k * W_IN_BLOCK), min(b, (k + 1) * W_IN_BLOCK)
        if lo < hi:
            parts.append(blocks[k][:, lo - k * W_IN_BLOCK:hi - k * W_IN_BLOCK])
    return parts


_W_IN_SEGMENTS = ((0, _O_ZB, "m", 0), (_O_ZB, _O_XBC, "m", UVZ_W + XBC_W), (_O_XBC, _O_DT, "m", UVZ_W),
                  (_O_DT, _O_G, "d", 0), (_O_G, N_IN, "m", MAIN_W - G_W))


def _w_in_grad_blocks(gm, gdt):
    blocks = []
    for k in range(N_CHIPS):
        a, b = k * W_IN_BLOCK, (k + 1) * W_IN_BLOCK
        parts = []
        for s, e, src, off in _W_IN_SEGMENTS:
            lo, hi = max(a, s), min(b, e)
            if lo < hi:
                parts.append((gm if src == "m" else gdt)[:, off + lo - s:off + hi - s])
        blocks.append(jnp.concatenate(parts, axis=1))
    return jnp.stack(blocks)


def _layout_weights(f, w_in_blocks=None):
    w = dict(f)
    if w_in_blocks is None:
        w_in = w.pop("w_in")
        w_in_blocks = jnp.stack([w_in[:, k * W_IN_BLOCK:(k + 1) * W_IN_BLOCK] for k in range(N_CHIPS)])
    cols = functools.partial(_block_cols, w_in_blocks)
    w["w_main"] = jnp.concatenate(cols(0, _O_ZB) + cols(_O_XBC, _O_DT) + cols(_O_ZB, _O_XBC) + cols(_O_G, N_IN), axis=1)
    w["w_dt"] = _heads_to_lanes(jnp.concatenate(cols(_O_DT, _O_G), axis=1))
    w["b_s"] = f["b_s"].reshape(G_A, CHUNK, 1)
    for n in ("dt_bias", "a_log", "d_skip"):
        w[n] = _heads_to_lanes(f[n])
    return w


def _natural_grads(g):
    out = dict(g)
    gm = out.pop("w_main")
    gdt = _lanes_to_heads(out.pop("w_dt"))
    out["w_in"] = jnp.concatenate(
        [gm[:, :UVZ_W], gm[:, UVZ_W + XBC_W:UVZ_W + XBC_W + ZB_W], gm[:, UVZ_W:UVZ_W + XBC_W], gdt, gm[:, MAIN_W - G_W:]],
        axis=1)
    out["b_s"] = g["b_s"].reshape(G_A, CHUNK)
    for n in ("dt_bias", "a_log", "d_skip"):
        out[n] = _lanes_to_heads(g[n])
    return out


def _place():
    return lax.axis_index("x"), lax.axis_index("y"), lax.axis_index("c")


def _other_chips(x, y):
    return [(1 - x, y), (x, 1 - y), (1 - x, 1 - y)]


def _rcopy(src, dst, ssem, rsem, dev):
    return pltpu.make_async_remote_copy(src_ref=src, dst_ref=dst, send_sem=ssem, recv_sem=rsem,
                                        device_id=dev, device_id_type=MESH)


def _half(ref_rows, half):
    hs = ref_rows // 2
    return pl.ds(pl.multiple_of(half * hs, 16), hs)


GATHER_CHUNKS = 8


def _gather_weights(shards, conv_shard):
    nw, nq = len(shards), GATHER_CHUNKS

    def body(*refs):
        sh, cv = refs[:nw], refs[nw]
        out, cvo = refs[nw + 1:2 * nw + 1], refs[2 * nw + 1]
        ici_s, ici_r, fw_s, fw_r, own_s, own_r, cv_s, cv_r = refs[2 * nw + 2:]
        x, y, c = _place()
        me, sib, chips = 2 * x + y, (x, y, 1 - c), _other_chips(x, y)
        own = [_rcopy(sh[w], out[w].at[me], own_s.at[w], own_r.at[w], sib) for w in range(nw)]
        own.append(_rcopy(cv, cvo.at[me], own_s.at[nw], own_r.at[nw], sib))
        for cp in own:
            cp.start()
        pieces = [(w, q, j) for w in range(nw) for q in range(nq) for j in range(3)]

        def rows(w, half, q):
            hs = sh[w].shape[0] // 2
            return pl.ds(pl.multiple_of(half * hs + q * (hs // nq), 16), hs // nq)

        def sem(w, q, j):
            return (3 * w + j) * nq + q

        def landed(w, q, j, half):
            return out[w].at[2 * chips[j][0] + chips[j][1], rows(w, half, q)]

        sends = [_rcopy(sh[w].at[rows(w, c, q)], out[w].at[me, rows(w, c, q)], ici_s.at[sem(w, q, j)],
                        ici_r.at[sem(w, q, j)], (*chips[j], c)) for w, q, j in pieces if j < 2]
        for j, chip in enumerate(chips):
            sends.append(_rcopy(cv, cvo.at[me], cv_s.at[j], cv_r.at[j], (*chip, c)))
        for cp in sends:
            cp.start()

        def arrived(w, q, j):
            slab = landed(w, q, j, c)
            _rcopy(slab, slab, ici_s.at[sem(w, q, j)], ici_r.at[sem(w, q, j)], (*chips[j], c)).wait_recv()
            fwd = _rcopy(slab, slab, fw_s.at[sem(w, q, j)], fw_r.at[sem(w, q, j)], sib)
            fwd.start()
            sends.append(fwd)
            if j < 2 and (q % 2 == 0) == (j == 0):
                hop = _rcopy(slab, slab, ici_s.at[sem(w, q, 2)], ici_r.at[sem(w, q, 2)], (*chips[1 - j], c))
                hop.start()
                sends.append(hop)

        for w, q, j in pieces:
            if j < 2:
                arrived(w, q, j)
        for w, q, j in pieces:
            if j == 2:
                arrived(w, q, j)
        for j, chip in enumerate(chips):
            blk = cvo.at[2 * chip[0] + chip[1]]
            _rcopy(blk, blk, cv_s.at[j], cv_r.at[j], (*chip, c)).wait_recv()
        for w, q, j in pieces:
            slab = landed(w, q, j, 1 - c)
            _rcopy(slab, slab, fw_s.at[sem(w, q, j)], fw_r.at[sem(w, q, j)], sib).wait_recv()
        for cp in sends:
            cp.wait_send()
        for cp in own:
            cp.wait()

    dma = pltpu.SemaphoreType.DMA
    n_ici = 3 * nw * nq
    return pl.pallas_call(
        body, name="gather_weights",
        in_specs=[ANY] * (nw + 1), out_specs=[ANY] * (nw + 1),
        out_shape=[jax.ShapeDtypeStruct((N_CHIPS,) + s.shape, s.dtype) for s in shards]
        + [jax.ShapeDtypeStruct((N_CHIPS,) + conv_shard.shape, conv_shard.dtype)],
        scratch_shapes=[dma((n_ici,)), dma((n_ici,)), dma((n_ici,)), dma((n_ici,)), dma((nw + 1,)), dma((nw + 1,)),
                        dma((3,)), dma((3,))],
    )(*shards, conv_shard)


_HBM = pl.BlockSpec(memory_space=pltpu.HBM)
_SEM = pl.BlockSpec(memory_space=pltpu.SEMAPHORE)
_EFFECT = pltpu.SideEffectType.DATAFLOW_SIDE_EFFECTING


def _late_gather_copies(sh, out, s_sem, r_sem):
    x, y, c = _place()
    to = [(*chip, c) for chip in _other_chips(x, y)] + [(x, y, 1 - c)]
    return [_rcopy(sh[w], out[w].at[2 * x + y], s_sem.at[4 * w + j], r_sem.at[4 * w + j], dev)
            for w in range(len(sh)) for j, dev in enumerate(to)]


def _late_gather_start(shards):
    n = len(shards)
    lands = [lax.empty((N_CHIPS,) + a.shape, a.dtype) for a in shards]

    def body(*refs):
        for cp in _late_gather_copies(refs[:n], refs[n:2 * n], refs[2 * n], refs[2 * n + 1]):
            cp.start()
        refs[-1][...] = jnp.zeros_like(refs[-1])

    dma = pltpu.SemaphoreType.DMA
    hbm = [pltpu.with_memory_space_constraint(a, pltpu.HBM) for a in list(shards) + lands]
    out = pl.pallas_call(
        body, name="late_gather_start",
        out_shape=[dma((4 * n,)), dma((4 * n,))] + [pltpu.HBM(a.shape, a.dtype) for a in hbm]
        + [jax.ShapeDtypeStruct((8, 128), F32)],
        in_specs=[_HBM] * (2 * n), out_specs=[_SEM, _SEM] + [_HBM] * (2 * n) + [pl.BlockSpec(memory_space=pltpu.VMEM)],
        input_output_aliases={i: 2 + i for i in range(2 * n)},
        compiler_params=pltpu.CompilerParams(has_side_effects=_EFFECT),
    )(*hbm)
    return out[0], out[1], out[2:2 + n], out[2 + n:2 + 2 * n], out[-1]


def _late_gather_wait(s_sem, r_sem, srcs, lands, after):
    n = len(srcs)

    def body(*refs):
        for cp in _late_gather_copies(refs[:n], refs[n:2 * n], refs[2 * n], refs[2 * n + 1]):
            cp.wait_send()
            cp.wait_recv()

    out = pl.pallas_call(
        body, name="late_gather_wait",
        out_shape=[pltpu.HBM(a.shape, a.dtype) for a in list(srcs) + list(lands)],
        in_specs=[_HBM] * (2 * n) + [_SEM, _SEM, ANY], out_specs=[_HBM] * (2 * n),
        input_output_aliases={i: i for i in range(2 * n)},
        compiler_params=pltpu.CompilerParams(has_side_effects=_EFFECT),
    )(*srcs, *lands, s_sem, r_sem, after)
    return out[n:]


def _swap_with_sibling(arrs):
    n = len(arrs)

    def body(*refs):
        src, dst, s_sem, r_sem = refs[:n], refs[n:2 * n], refs[2 * n], refs[2 * n + 1]
        x, y, c = _place()
        cps = [_rcopy(src[i], dst[i], s_sem.at[i], r_sem.at[i], (x, y, 1 - c)) for i in range(n)]
        for cp in cps:
            cp.start()
        for cp in cps:
            cp.wait()

    dma = pltpu.SemaphoreType.DMA
    return pl.pallas_call(
        body, name="swap_with_sibling", in_specs=[ANY] * n, out_specs=[ANY] * n,
        out_shape=[jax.ShapeDtypeStruct(a.shape, a.dtype) for a in arrs], scratch_shapes=[dma((n,)), dma((n,))],
    )(*arrs)


def _scatter_copies(src, land, s_sem, r_sem):
    x, y, c = _place()
    return [_rcopy(src[i].at[2 * chip[0] + chip[1]], land[i].at[j], s_sem.at[3 * i + j], r_sem.at[3 * i + j], (*chip, c))
            for i in range(len(src)) for j, chip in enumerate(_other_chips(x, y))]


def _scatter_blocks_start(arrs):
    n = len(arrs)
    lands = [lax.empty((3,) + a.shape[1:], a.dtype) for a in arrs]

    def body(*refs):
        src, land, s_sem, r_sem, token = refs[:n], refs[n:2 * n], refs[2 * n], refs[2 * n + 1], refs[-1]
        for cp in _scatter_copies(src, land, s_sem, r_sem):
            cp.start()
        token[...] = jnp.zeros_like(token)

    dma = pltpu.SemaphoreType.DMA
    hbm = [pltpu.with_memory_space_constraint(a, pltpu.HBM) for a in list(arrs) + lands]
    out = pl.pallas_call(
        body, name="scatter_blocks_start",
        out_shape=[dma((3 * n,)), dma((3 * n,))] + [pltpu.HBM(a.shape, a.dtype) for a in hbm]
        + [jax.ShapeDtypeStruct((8, 128), F32)],
        in_specs=[_HBM] * (2 * n), out_specs=[_SEM, _SEM] + [_HBM] * (2 * n) + [pl.BlockSpec(memory_space=pltpu.VMEM)],
        input_output_aliases={i: 2 + i for i in range(2 * n)},
        compiler_params=pltpu.CompilerParams(has_side_effects=_EFFECT),
    )(*hbm)
    return out[0], out[1], out[2:2 + n], out[2 + n:2 + 2 * n], out[-1]


def _scatter_blocks_wait(s_sem, r_sem, srcs, lands, after):
    n = len(srcs)

    def body(*refs):
        src, land, s_sem, r_sem = refs[:n], refs[n:2 * n], refs[2 * n], refs[2 * n + 1]
        for cp in _scatter_copies(src, land, s_sem, r_sem):
            cp.wait_send()
            cp.wait_recv()

    out = pl.pallas_call(
        body, name="scatter_blocks_wait",
        out_shape=[pltpu.HBM(a.shape, a.dtype) for a in list(srcs) + list(lands)],
        in_specs=[_HBM] * (2 * n) + [_SEM, _SEM] + [ANY] * len(after), out_specs=[_HBM] * (2 * n),
        input_output_aliases={i: i for i in range(2 * n)},
        compiler_params=pltpu.CompilerParams(has_side_effects=_EFFECT),
    )(*srcs, *lands, s_sem, r_sem, *after)
    return out[:n], out[n:]


def _share_halves(arrs):
    n = len(arrs)

    def body(*refs):
        buf, s_sem, r_sem = refs[n:2 * n], refs[2 * n], refs[2 * n + 1]
        x, y, c = _place()
        cps = []
        for i in range(n):
            mine = buf[i].at[_half(buf[i].shape[0], c)]
            cps.append(_rcopy(mine, mine, s_sem.at[i], r_sem.at[i], (x, y, 1 - c)))
        for cp in cps:
            cp.start()
        for i in range(n):
            theirs = buf[i].at[_half(buf[i].shape[0], 1 - c)]
            _rcopy(theirs, theirs, s_sem.at[i], r_sem.at[i], (x, y, 1 - c)).wait_recv()
        for cp in cps:
            cp.wait_send()

    dma = pltpu.SemaphoreType.DMA
    return pl.pallas_call(
        body, name="share_halves", in_specs=[ANY] * n, out_specs=[ANY] * n,
        out_shape=[jax.ShapeDtypeStruct(a.shape, a.dtype) for a in arrs],
        input_output_aliases={i: i for i in range(n)}, scratch_shapes=[dma((n,)), dma((n,))],
    )(*arrs)


def _small_gather_copies(src, land, s_sem, r_sem):
    x, y, c = _place()
    cps = []
    for d in range(1, N_DEV):
        peer = ((1 - x) if d & 4 else x), ((1 - y) if d & 2 else y), ((1 - c) if d & 1 else c)
        cps.append(_rcopy(src, land.at[4 * x + 2 * y + c], s_sem.at[d - 1], r_sem.at[d - 1], peer))
    return cps


def _small_gather_start(packed):
    def body(src, land, s_sem, r_sem, _, __, token):
        for cp in _small_gather_copies(src, land, s_sem, r_sem):
            cp.start()
        token[...] = jnp.zeros_like(token)

    dma = pltpu.SemaphoreType.DMA
    hbm = [pltpu.with_memory_space_constraint(a, pltpu.HBM) for a in (packed, lax.empty((N_DEV,) + packed.shape, F32))]
    return pl.pallas_call(
        body, name="small_gather_start",
        out_shape=[dma((N_DEV - 1,)), dma((N_DEV - 1,))] + [pltpu.HBM(a.shape, a.dtype) for a in hbm]
        + [jax.ShapeDtypeStruct((8, 128), F32)],
        in_specs=[_HBM] * 2, out_specs=[_SEM, _SEM, _HBM, _HBM, pl.BlockSpec(memory_space=pltpu.VMEM)],
        input_output_aliases={0: 2, 1: 3}, compiler_params=pltpu.CompilerParams(has_side_effects=_EFFECT),
    )(*hbm)


def _small_gather_wait(s_sem, r_sem, src, land, after):
    def body(src, land, s_sem, r_sem, *_):
        for cp in _small_gather_copies(src, land, s_sem, r_sem):
            cp.wait_send()
            cp.wait_recv()

    return pl.pallas_call(
        body, name="small_gather_wait", out_shape=[pltpu.HBM(src.shape, src.dtype), pltpu.HBM(land.shape, land.dtype)],
        in_specs=[_HBM, _HBM, _SEM, _SEM, ANY], out_specs=[_HBM, _HBM], input_output_aliases={0: 0, 1: 1},
        compiler_params=pltpu.CompilerParams(has_side_effects=_EFFECT),
    )(src, land, s_sem, r_sem, after)


def _small_sum(own, land, dev_arr):
    def body(me_ref, own_ref, land_ref, o_ref):
        acc = jnp.zeros(o_ref.shape, F32)
        for d in range(N_DEV):
            acc = acc + jnp.where(me_ref[0] == d, own_ref[...], land_ref[d])
        o_ref[...] = acc

    return pl.pallas_call(
        body, name="small_sum", out_shape=jax.ShapeDtypeStruct(own.shape, F32),
        grid_spec=pltpu.PrefetchScalarGridSpec(
            num_scalar_prefetch=1, grid=(1,),
            in_specs=[pl.BlockSpec(own.shape, lambda i, m: (0, 0)), pl.BlockSpec(land.shape, lambda i, m: (0, 0, 0))],
            out_specs=pl.BlockSpec(own.shape, lambda i, m: (0, 0))),
    )(dev_arr, own, land)


def _row_tile(rows, cols):
    tr = max(8, min(rows, (1 << 20) // (4 * cols) // 8 * 8))
    while rows % tr:
        tr -= 8
    return tr


def _chip_sum(name, g5, recv, c_arr):
    nb, _, hs, cols = g5.shape
    tr = _row_tile(hs, cols)

    def body(_, a_ref, b_ref, o_ref):
        o_ref[...] = (a_ref[...] + b_ref[...].astype(F32)).astype(o_ref.dtype)

    blk = pl.BlockSpec((None, tr, cols), lambda b, i, c: (b, i, 0))
    return pl.pallas_call(
        body, name=name,
        grid_spec=pltpu.PrefetchScalarGridSpec(
            num_scalar_prefetch=1, grid=(nb, hs // tr),
            in_specs=[pl.BlockSpec((None, None, tr, cols), lambda b, i, c: (b, c[0], i, 0)), blk], out_specs=blk),
        out_shape=jax.ShapeDtypeStruct((nb, hs, cols), WIRE_DTYPE),
    )(c_arr, g5, recv)


def _final_sum(name, own, recv, place_arr):
    _, hs, cols = own.shape
    tr = _row_tile(hs, cols)
    nt = hs // tr

    def body(_, a_ref, r_ref, o_ref):
        o_ref[...] = ((a_ref[...].astype(F32) + r_ref[0].astype(F32)) + r_ref[1].astype(F32)) + r_ref[2].astype(F32)

    return pl.pallas_call(
        body, name=name,
        grid_spec=pltpu.PrefetchScalarGridSpec(
            num_scalar_prefetch=1, grid=(nt,),
            in_specs=[pl.BlockSpec((None, tr, cols), lambda i, m: (m[0], i, 0)),
                      pl.BlockSpec((3, tr, cols), lambda i, m: (0, i, 0))],
            out_specs=pl.BlockSpec((tr, cols), lambda i, m: (m[1] * nt + i, 0))),
        out_shape=jax.ShapeDtypeStruct((2 * hs, cols), F32),
    )(place_arr, own, recv)


def _adamw(w, g, m, v):
    m = ADAM_B1 * m + (1.0 - ADAM_B1) * g
    v = ADAM_B2 * v + (1.0 - ADAM_B2) * (g * g)
    m_hat = m / (1.0 - ADAM_B1 ** ADAM_STEP)
    v_hat = v / (1.0 - ADAM_B2 ** ADAM_STEP)
    return -ADAM_LR * (m_hat / (jnp.sqrt(v_hat) + ADAM_EPS) + ADAM_WD * w), m, v


def _adamw_call(name, w, g, m, v):
    rows, cols = w.shape
    tr = _row_tile(rows, cols)
    if 4 * tr * cols >= (1 << 18):
        blk, steps = pl.BlockSpec((tr, cols), lambda i: (i, 0)), rows // tr
    else:
        blk, steps = pl.BlockSpec((rows, 128), lambda i: (0, i)), cols // 128

    def body(w_ref, g_ref, m_ref, v_ref, d_ref, nm_ref, nv_ref, go_ref):
        g = g_ref[...]
        d_ref[...], nm_ref[...], nv_ref[...] = _adamw(w_ref[...], g, m_ref[...], v_ref[...])
        go_ref[...] = g

    return pl.pallas_call(
        body, name=name, grid=(steps,), in_specs=[blk] * 4, out_specs=[blk] * 4,
        out_shape=[jax.ShapeDtypeStruct(w.shape, F32)] * 4,
        compiler_params=pltpu.CompilerParams(dimension_semantics=("parallel",)),
    )(w, g, m, v)


def _adamw_small(ws, gs, ms, vs):
    n = len(ws)

    def body(*refs):
        for i in range(n):
            w_ref, g_ref, m_ref, v_ref = (refs[k * n + i] for k in range(4))
            d, nm, nv = _adamw(w_ref[...], g_ref[...], m_ref[...], v_ref[...])
            refs[4 * n + i][...] = d
            refs[5 * n + i][...] = nm
            refs[6 * n + i][...] = nv

    out = pl.pallas_call(
        body, name="adamw_small", out_shape=[jax.ShapeDtypeStruct(a.shape, F32) for a in ws] * 3,
    )(*ws, *gs, *ms, *vs)
    return out[:n], out[n:2 * n], out[2 * n:]


_BIG = ("w_in", "w_oa", "w_ob", "w_out", "w_pg", "w_ple")
_SMALL = ("norm_g", "ln_a_g", "ln_a_b", "w_s", "b_s", "conv_w", "conv_b", "dt_bias", "a_log", "d_skip", "ssm_norm_g",
          "ple_norm_g", "final_g")
_WEIGHTS = ("norm_g", "w_in", "ln_a_g", "ln_a_b", "w_s", "b_s", "conv_w", "conv_b", "dt_bias", "a_log", "d_skip",
            "ssm_norm_g", "w_oa", "w_ob", "w_out", "ple_norm_g", "w_pg", "w_ple", "final_g")
_COL_SHARDED = ("w_in", "w_ple")
_PACK = 1024


def _blocks_to_full(col_sharded, blocks):
    if col_sharded:
        return jnp.concatenate([blocks[k] for k in range(N_CHIPS)], axis=1)
    return blocks.reshape(N_CHIPS * blocks.shape[1], blocks.shape[2])


def _full_to_blocks(col_sharded, full):
    if col_sharded:
        w = full.shape[1] // N_CHIPS
        return jnp.stack([full[:, k * w:(k + 1) * w] for k in range(N_CHIPS)])
    return full.reshape(N_CHIPS, full.shape[0] // N_CHIPS, full.shape[1])


def _two_d(n, a):
    if n == "w_s":
        return a.reshape(G_A * CHUNK, CHUNK)
    if n in ("b_s", "conv_w"):
        return a.reshape(a.shape[-2], a.shape[-1])
    return a.reshape(1, a.shape[-1])


def kernel(x, p, norm_g, w_in, ln_a_g, ln_a_b, w_s, b_s, conv_w, conv_b, dt_bias, a_log, d_skip, ssm_norm_g, w_oa, w_ob, w_out, ple_norm_g, w_pg, w_ple, final_g, loss_target, m_norm_g, m_w_in, m_ln_a_g, m_ln_a_b, m_w_s, m_b_s, m_conv_w, m_conv_b, m_dt_bias, m_a_log, m_d_skip, m_ssm_norm_g, m_w_oa, m_w_ob, m_w_out, m_ple_norm_g, m_w_pg, m_w_ple, m_final_g, v_norm_g, v_w_in, v_ln_a_g, v_ln_a_b, v_w_s, v_b_s, v_conv_w, v_conv_b, v_dt_bias, v_a_log, v_d_skip, v_ssm_norm_g, v_w_oa, v_w_ob, v_w_out, v_ple_norm_g, v_w_pg, v_w_ple, v_final_g):
    wt = dict(norm_g=norm_g, w_in=w_in, ln_a_g=ln_a_g, ln_a_b=ln_a_b, w_s=w_s, b_s=b_s, conv_w=conv_w, conv_b=conv_b,
              dt_bias=dt_bias, a_log=a_log, d_skip=d_skip, ssm_norm_g=ssm_norm_g, w_oa=w_oa, w_ob=w_ob, w_out=w_out,
              ple_norm_g=ple_norm_g, w_pg=w_pg, w_ple=w_ple, final_g=final_g)
    mom = dict(norm_g=m_norm_g, w_in=m_w_in, ln_a_g=m_ln_a_g, ln_a_b=m_ln_a_b, w_s=m_w_s, b_s=m_b_s, conv_w=m_conv_w,
               conv_b=m_conv_b, dt_bias=m_dt_bias, a_log=m_a_log, d_skip=m_d_skip, ssm_norm_g=m_ssm_norm_g, w_oa=m_w_oa,
               w_ob=m_w_ob, w_out=m_w_out, ple_norm_g=m_ple_norm_g, w_pg=m_w_pg, w_ple=m_w_ple, final_g=m_final_g)
    vel = dict(norm_g=v_norm_g, w_in=v_w_in, ln_a_g=v_ln_a_g, ln_a_b=v_ln_a_b, w_s=v_w_s, b_s=v_b_s, conv_w=v_conv_w,
               conv_b=v_conv_b, dt_bias=v_dt_bias, a_log=v_a_log, d_skip=v_d_skip, ssm_norm_g=v_ssm_norm_g, w_oa=v_w_oa,
               w_ob=v_w_ob, w_out=v_w_out, ple_norm_g=v_ple_norm_g, w_pg=v_w_pg, w_ple=v_w_ple, final_g=v_final_g)
    xi, yi, ci = _place()
    me = 2 * xi + yi
    c_arr = jnp.reshape(ci, (1,)).astype(jnp.int32)
    place_arr = jnp.stack([me, ci]).astype(jnp.int32)

    shard = {n: wt[n][0] for n in _BIG}
    wire = {n: shard[n].astype(WIRE_DTYPE) for n in _BIG}
    w_in_blocks, conv_blocks = _gather_weights([wire["w_in"]], conv_w[0])
    g_ssem, g_rsem, g_sent, g_lands, g_token = _late_gather_start([wire[n] for n in _BIG[1:]])
    full = {"conv_w": _blocks_to_full(True, conv_blocks)}
    for n in _SMALL:
        if n != "conv_w":
            full[n] = wt[n][0] if wt[n].ndim > 2 else wt[n].reshape(1, wt[n].shape[-1])

    def late_weights(after):
        blocks = _late_gather_wait(g_ssem, g_rsem, g_sent, g_lands, after)
        return {n: _blocks_to_full(n in _COL_SHARDED, b) for n, b in zip(_BIG[1:], blocks)}

    w = _layout_weights(full, w_in_blocks=w_in_blocks)
    loss_row, g, ctx = _forward_backward(x[0], p[0, 0], loss_target[0], w, late_weights, after=(g_token,))
    loss = lax.psum(loss_row[0, 0], ("x", "y", "c"))

    parts = {n: _full_to_blocks(n in _COL_SHARDED, g[n]) for n in _BIG[1:]}
    parts["w_main"] = g["w_main"][None]
    parts["w_dt"] = jnp.pad(_lanes_to_heads(g["w_dt"]), ((0, 0), (0, 128 - N_HEADS)))[None]
    names = ("w_main", "w_dt") + _BIG[1:]
    g5 = {n: parts[n].reshape(parts[n].shape[0], 2, parts[n].shape[1] // 2, parts[n].shape[2]) for n in names}
    to_sibling = [lax.dynamic_index_in_dim(g5[n], 1 - ci, axis=1, keepdims=False).astype(WIRE_DTYPE) for n in names]
    from_sibling = _swap_with_sibling(to_sibling)
    chip = {n: _chip_sum("chip_sum_" + n, g5[n], r, c_arr) for n, r in zip(names, from_sibling)}
    chip["w_in"] = _w_in_grad_blocks(chip["w_main"][0], chip["w_dt"][0])
    chip_wire = [chip[n] for n in _BIG]
    s_sem, r_sem, sent, lands, token = _scatter_blocks_start(chip_wire)
    grad_x, g["norm_g"] = _input_grad(x[0], w, ctx, after=(token,))
    g = _natural_grads(g)

    pieces = [_two_d(n, g[n]).reshape(-1) for n in _SMALL]
    sizes = [v.shape[0] for v in pieces]
    padded = [-(-s // _PACK) * _PACK for s in sizes]
    packed = jnp.concatenate([jnp.pad(v, (0, ps - s)) for v, s, ps in zip(pieces, sizes, padded)]).reshape(-1, 128)
    a_ssem, a_rsem, a_src, a_land, a_token = _small_gather_start(packed)

    sent, from_chips = _scatter_blocks_wait(s_sem, r_sem, sent, lands, (grad_x, a_token))
    halves = [_final_sum("final_sum_" + n, a, r, place_arr) for n, a, r in zip(_BIG, sent, from_chips)]
    grads = dict(zip(_BIG, _share_halves(halves)))
    delta, new_m, new_v = {}, {}, {}
    for n in _BIG:
        t = jnp.transpose if n == "w_in" else (lambda a: a)
        res = _adamw_call("adamw_" + n, t(shard[n]), t(grads[n]), t(mom[n][0]), t(vel[n][0]))
        delta[n], new_m[n], new_v[n], grads[n] = (t(r) for r in res)

    a_src, a_land = _small_gather_wait(a_ssem, a_rsem, a_src, a_land, delta["w_in"])
    summed = _small_sum(a_src, a_land, jnp.reshape(4 * xi + 2 * yi + ci, (1,)).astype(jnp.int32)).reshape(-1)
    off = 0
    for n, s, ps in zip(_SMALL, sizes, padded):
        grads[n] = summed[off:off + s].reshape(_two_d(n, g[n]).shape)
        off += ps
    grads["conv_w"] = lax.dynamic_slice_in_dim(grads["conv_w"], me * (CONV_DIM // N_CHIPS), CONV_DIM // N_CHIPS, axis=1)
    small = _adamw_small([_two_d(n, wt[n]) for n in _SMALL], [grads[n] for n in _SMALL],
                         [_two_d(n, mom[n]) for n in _SMALL], [_two_d(n, vel[n]) for n in _SMALL])
    for i, n in enumerate(_SMALL):
        delta[n], new_m[n], new_v[n] = small[0][i], small[1][i], small[2][i]

    def shaped(d):
        return [d[n].reshape(wt[n].shape) for n in _WEIGHTS]

    return (loss, grad_x[None], *shaped(grads), *shaped(delta), *shaped(new_m), *shaped(new_v))
```

```python
import functools

import jax
import jax.numpy as jnp
from jax import lax
from jax.experimental import pallas as pl
from jax.experimental.pallas import tpu as pltpu

F32 = jnp.float32
MXU_DTYPE = jnp.bfloat16
ACT_DTYPE = jnp.bfloat16
WIRE_DTYPE = jnp.bfloat16

D_MODEL = 1024
PLE_DIM = 256
CHUNK = 128
EPS = 1e-6
E_A = D_MODEL
G_A = 4
D_INNER = 2 * D_MODEL
HEAD_DIM = 64
N_HEADS = D_INNER // HEAD_DIM
N_STATE = 128
N_GROUPS = 4
HEADS_PER_GROUP = N_HEADS // N_GROUPS
PAIRS_PER_GROUP = HEADS_PER_GROUP // 2
CONV_K = 4
CONV_DIM = D_INNER + 2 * N_GROUPS * N_STATE
N_IN = 3 * E_A + D_INNER + CONV_DIM + N_HEADS + 2 * D_MODEL
N_CHIPS = 4
N_DEV = 8
W_IN_BLOCK = N_IN // N_CHIPS

UVZ_W, XBC_W, ZB_W, G_W = 3 * E_A, CONV_DIM, D_INNER, 2 * D_MODEL
MAIN_W = UVZ_W + XBC_W + ZB_W + G_W
UVZ_CB, XBC_CB, ZB_CB, G_CB = 0, 1, 3, 4
DT_W = N_GROUPS * 128

ADAM_LR, ADAM_B1, ADAM_B2, ADAM_EPS, ADAM_WD, ADAM_STEP = 0.001, 0.9, 0.999, 1e-08, 0.01, 10

MESH = pl.DeviceIdType.MESH
ANY = pl.BlockSpec(memory_space=pl.ANY)


def _mxu(v):
    return v.astype(MXU_DTYPE)


def _dot(a, b, dims=(((1,), (0,)), ((), ()))):
    return lax.dot_general(_mxu(a), _mxu(b), dims, preferred_element_type=F32)


V7X_MXU_WIDTH = 256
V7X_SCOPED_VMEM_BYTES = 60000 * 1024
MM_TILE = 4 * V7X_MXU_WIDTH
MM_VMEM_BUDGET = 4 * V7X_SCOPED_VMEM_BYTES // 5


def _mm_tk(m, n, k, tm, tn, a_bytes, b_bytes, out_bytes, extra_bytes):
    one_tile = m == tm and n == tn
    for parts in range(2 if one_tile else 1, k // 128 + 1):
        if k % parts or (k // parts) % 128 and parts > 1:
            continue
        tk = k // parts
        a_bufs = 1 if (parts == 1 and m == tm) else 2
        b_bufs = 1 if (parts == 1 and n == tn) else 2
        need = (tk * (a_bufs * tm * a_bytes + b_bufs * tn * b_bytes) + 2 * tm * tn * (out_bytes + extra_bytes)
                + (tm * tn * 4 if parts > 1 else 0))
        if need <= MM_VMEM_BUDGET:
            return tk
    return 128


def _matmul(a, b, *, mode, name, out_dtype, m, n, k, tm=MM_TILE, tn=MM_TILE, tk=None, a_off=0, b_off=0,
            extras=(), epilogue=None, after=()):
    tm, tn = min(tm, m), min(tn, n)
    if tk is None:
        tk = _mm_tk(m, n, k, tm, tn, a.dtype.itemsize, b.dtype.itemsize, jnp.dtype(out_dtype).itemsize,
                    sum(e.dtype.itemsize for e in extras))
    tk = min(tk, k)
    assert m % tm == 0 and n % tn == 0 and k % tk == 0, (name, m, n, k, tm, tn, tk)
    nk = k // tk
    a_mode = pl.Buffered(1) if (nk == 1 and m == tm) else None
    b_mode = pl.Buffered(1) if (nk == 1 and n == tn) else None
    if mode == "nn":
        assert a_off % tk == 0 and b_off % tn == 0
        a_spec = pl.BlockSpec((tm, tk), lambda i, j, kk: (i, kk + a_off // tk), pipeline_mode=a_mode)
        b_spec = pl.BlockSpec((tk, tn), lambda i, j, kk: (kk, j + b_off // tn), pipeline_mode=b_mode)
        dims = (((1,), (0,)), ((), ()))
    elif mode == "nt":
        a_spec = pl.BlockSpec((tm, tk), lambda i, j, kk: (i, kk), pipeline_mode=a_mode)
        b_spec = pl.BlockSpec((tn, tk), lambda i, j, kk: (j, kk), pipeline_mode=b_mode)
        dims = (((1,), (1,)), ((), ()))
    else:
        assert a_off % tm == 0 and b_off % tn == 0
        a_spec = pl.BlockSpec((tk, tm), lambda i, j, kk: (kk, i + a_off // tm), pipeline_mode=a_mode)
        b_spec = pl.BlockSpec((tk, tn), lambda i, j, kk: (kk, j + b_off // tn), pipeline_mode=b_mode)
        dims = (((0,), (0,)), ((), ()))
    ne = len(extras)

    def finish(acc, extra_refs, o_ref):
        res = acc if epilogue is None else epilogue(acc, *[e[...] for e in extra_refs])
        o_ref[...] = res.astype(o_ref.dtype)

    def body(a_ref, b_ref, *rest):
        extra_refs, o_ref = rest[:ne], rest[ne + len(after)]
        part = _dot(a_ref[...], b_ref[...], dims)
        if nk == 1:
            finish(part, extra_refs, o_ref)
            return
        acc_ref = rest[ne + len(after) + 1]
        kk = pl.program_id(2)

        @pl.when(kk == 0)
        def _():
            acc_ref[...] = part

        @pl.when(kk > 0)
        def _():
            acc_ref[...] += part

        @pl.when(kk == nk - 1)
        def _():
            finish(acc_ref[...], extra_refs, o_ref)

    o_spec = pl.BlockSpec((tm, tn), lambda i, j, kk: (i, j))
    return pl.pallas_call(
        body, name=name, grid=(m // tm, n // tn, nk),
        in_specs=[a_spec, b_spec] + [o_spec] * ne + [ANY] * len(after), out_specs=o_spec,
        out_shape=jax.ShapeDtypeStruct((m, n), out_dtype),
        scratch_shapes=[pltpu.VMEM((tm, tn), F32)] if nk > 1 else [],
        compiler_params=pltpu.CompilerParams(dimension_semantics=("parallel", "parallel", "arbitrary")),
    )(a, b, *extras, *after)


def _rows_matmul(name, f, rows, pars, b, *, out_dtype, n, k, tm, nrows, tn=MM_TILE, extras=(), epilogue=None, after=(),
                 side=None, ahead=False):
    tm, tn = min(tm, nrows), min(tn, n)
    assert nrows % tm == 0 and n % tn == 0, (name, nrows, n, tm, tn)
    nr, npar, ne, nj, ni = len(rows), len(pars), len(extras), n // tn, nrows // tm
    ns = 0 if side is None else 1
    nx = nr if ahead else 0
    n_in = nr + npar + 1 + ne + ns + len(after) + nx

    def body(*refs):
        row_refs, par_refs, b_ref = refs[:nr], refs[nr:nr + npar], refs[nr + npar]
        extra_refs = refs[nr + npar + 1:nr + npar + 1 + ne]
        next_refs = refs[n_in - nx:n_in]
        a_ref, o_ref = refs[n_in], refs[n_in + 1]
        pars_v = [p[...] for p in par_refs]

        def make_a(from_refs):
            return f(*[r[...].astype(F32) for r in from_refs], *pars_v)[0].astype(a_ref.dtype)

        def side_product(a):
            if ns:
                refs[n_in + 2][...] = _dot(a, refs[nr + npar + 1 + ne][...]).astype(refs[n_in + 2].dtype)

        if ahead:
            a2 = refs[-1]
            i, j = pl.program_id(0), pl.program_id(1)
            slot = lax.rem(i, 2)

            @pl.when((i == 0) & (j == 0))
            def _():
                a2[0] = make_a(row_refs)

            @pl.when(j == 0)
            def _():
                side_product(a2[slot])

            a = a2[slot]
            a2[1 - slot] = make_a(next_refs)
            a_ref[...] = a
        else:
            def first():
                a_ref[...] = make_a(row_refs)
                side_product(a_ref[...])

            if nj == 1:
                first()
            else:
                pl.when(pl.program_id(1) == 0)(first)
            a = a_ref[...]
        res = _dot(a, b_ref[...])
        if epilogue is not None:
            res = epilogue(res, *[e[...] for e in extra_refs])
        o_ref[...] = res.astype(o_ref.dtype)

    o_spec = pl.BlockSpec((tm, tn), lambda i, j: (i, j))
    side_in = [] if side is None else [pl.BlockSpec(tuple(side[0].shape), lambda i, j: (0, 0))]
    side_out = [] if side is None else [pl.BlockSpec((tm, side[0].shape[1]), lambda i, j: (i, 0))]
    side_shape = [] if side is None else [jax.ShapeDtypeStruct((nrows, side[0].shape[1]), side[1])]
    next_in = [pl.BlockSpec((tm, w), lambda i, j, cb=cb: (jnp.minimum(i + 1, ni - 1), cb)) for _, w, cb in rows] if ahead else []
    return pl.pallas_call(
        body, name=name, grid=(ni, nj),
        in_specs=[pl.BlockSpec((tm, w), lambda i, j, cb=cb: (i, cb)) for _, w, cb in rows]
        + [pl.BlockSpec(tuple(p.shape), lambda i, j, nd=p.ndim: (0,) * nd) for p in pars]
        + [pl.BlockSpec((k, tn), lambda i, j: (0, j))] + [o_spec] * ne + side_in + [ANY] * len(after) + next_in,
        out_specs=[pl.BlockSpec((tm, k), lambda i, j: (i, 0)), o_spec] + side_out,
        out_shape=[jax.ShapeDtypeStruct((nrows, k), ACT_DTYPE), jax.ShapeDtypeStruct((nrows, n), out_dtype)] + side_shape,
        scratch_shapes=[pltpu.VMEM((2, tm, k), ACT_DTYPE)] if ahead else [],
        compiler_params=pltpu.CompilerParams(dimension_semantics=("arbitrary" if ahead else "parallel", "arbitrary")),
    )(*[r[0] for r in rows], *pars, b, *extras, *([] if side is None else [side[0]]), *after,
      *([r[0] for r in rows] if ahead else []))


def _row_spec(tm, width, cb):
    return pl.BlockSpec((tm, width), lambda i: (i, cb))


def _whole_spec(shape):
    nd = len(shape)
    return pl.BlockSpec(tuple(shape), lambda i: (0,) * nd)


def _rows_vjp_call(name, f, rows, pars, cots, drows, *, tm, nrows, cot_mm=None):
    tm = min(tm, nrows)
    nr, npar, nc = len(rows), len(pars), len(cots)
    mm_args, mm_specs = [], []
    if cot_mm is not None:
        mm_a, mm_b, mm_add = cot_mm
        mm_args = [mm_a, mm_b] + ([] if mm_add is None else [mm_add])
        mm_specs = [_row_spec(tm, mm_a.shape[1], 0), _whole_spec(mm_b.shape)]
        mm_specs += [] if mm_add is None else [_row_spec(tm, mm_b.shape[0], 0)]
    alias_bufs, aliases = [], {}
    out_shape, out_specs = [], []
    for (arr, w, cb), d in zip(rows, drows):
        if d is None:
            continue
        dt, into = d
        if into is None:
            out_shape.append(jax.ShapeDtypeStruct((nrows, w), dt))
            out_specs.append(_row_spec(tm, w, 0))
        else:
            buf, total, ocb = into
            if buf is not None:
                aliases[nr + npar + nc + len(alias_bufs)] = len(out_shape)
                alias_bufs.append(buf)
            out_shape.append(jax.ShapeDtypeStruct((nrows, total), dt))
            out_specs.append(_row_spec(tm, w, ocb))
    n_drow = len(out_shape)
    for p in pars:
        out_shape.append(jax.ShapeDtypeStruct(p.shape, F32))
        out_specs.append(_whole_spec(p.shape))
    na = len(alias_bufs)

    def body(*refs):
        rv = [r[...].astype(F32) for r in refs[:nr]]
        pv = [p[...] for p in refs[nr:nr + npar]]
        cv = tuple(c[...].astype(F32) for c in refs[nr + npar:nr + npar + nc])
        o_refs = refs[nr + npar + nc + na + len(mm_args):]
        if mm_args:
            mm_refs = refs[nr + npar + nc + na:nr + npar + nc + na + len(mm_args)]
            c0 = _dot(mm_refs[0][...], mm_refs[1][...], (((1,), (1,)), ((), ())))
            if len(mm_refs) == 3:
                c0 = c0 + mm_refs[2][...].astype(F32)
            cv = (c0,) + cv
        _, vjp = jax.vjp(f, *rv, *pv)
        g = vjp(cv)
        oi = 0
        for ri, d in enumerate(drows):
            if d is not None:
                o_refs[oi][...] = g[ri].astype(o_refs[oi].dtype)
                oi += 1
        first = pl.program_id(0) == 0
        for pi in range(npar):
            acc = o_refs[n_drow + pi]

            @pl.when(first)
            def _(acc=acc):
                acc[...] = jnp.zeros_like(acc)

            acc[...] += g[nr + pi]

    return pl.pallas_call(
        body, name=name, grid=(nrows // tm,),
        in_specs=[_row_spec(tm, w, cb) for _, w, cb in rows] + [_whole_spec(p.shape) for p in pars]
        + [_row_spec(tm, w, cb) for _, w, cb in cots] + [ANY] * na + mm_specs,
        out_specs=out_specs, out_shape=out_shape, input_output_aliases=aliases,
        compiler_params=pltpu.CompilerParams(dimension_semantics=("arbitrary",)),
    )(*[r[0] for r in rows], *pars, *[c[0] for c in cots], *alias_bufs, *mm_args)


def _rms(x, g):
    return x * lax.rsqrt(jnp.mean(x * x, axis=-1, keepdims=True) + EPS) * g


def _f_rms(x, g):
    return (_rms(x, g),)


def _tril_mask():
    return lax.broadcasted_iota(jnp.int32, (CHUNK, CHUNK), 0) >= lax.broadcasted_iota(jnp.int32, (CHUNK, CHUNK), 1)


def _f_branch_a(uvz, ln_g, ln_b, w_s, b_s):
    u = jax.nn.gelu(uvz[:, :E_A])
    v = jax.nn.gelu(uvz[:, E_A:2 * E_A])
    z = uvz[:, 2 * E_A:]
    xc = v - jnp.mean(v, axis=-1, keepdims=True)
    vn = xc * lax.rsqrt(jnp.mean(xc * xc, axis=-1, keepdims=True) + EPS) * ln_g + ln_b
    mask = _tril_mask()
    ws = [jnp.where(mask, w_s[g], 0.0) for g in range(G_A)]
    gw = E_A // G_A
    rows = []
    for c in range(uvz.shape[0] // CHUNK):
        vc = vn[c * CHUNK:(c + 1) * CHUNK]
        rows.append(jnp.concatenate([_dot(ws[g], vc[:, g * gw:(g + 1) * gw]) + b_s[g] for g in range(G_A)], axis=1))
    sv = rows[0] if len(rows) == 1 else jnp.concatenate(rows, axis=0)
    return (u * sv * jax.nn.silu(z),)


def _f_gnorm(y, zb, g):
    yz = y * jax.nn.silu(zb)
    gw = D_INNER // N_GROUPS
    parts = []
    for i in range(N_GROUPS):
        s = yz[:, i * gw:(i + 1) * gw]
        parts.append(s * lax.rsqrt(jnp.mean(s * s, axis=-1, keepdims=True) + EPS))
    return (jnp.concatenate(parts, axis=1) * g,)


def _f_merge(g2, oa, ob):
    return (jax.nn.sigmoid(g2[:, :D_MODEL]) * oa + jax.nn.sigmoid(g2[:, D_MODEL:]) * ob,)


def _f_loss(x1, gp, pe, tgt, fg):
    x2 = x1 + jax.nn.sigmoid(gp) * pe
    err = _rms(x2, fg) - tgt
    return 0.5 * jnp.sum(jnp.mean(err * err, axis=-1))


def _head(x1, p, tgt, ple_g, w_pg, w_ple, fg, *, tm, nrows):
    tm = min(tm, nrows)

    def body(x1_ref, p_ref, t_ref, pg_ref, wpg_ref, wple_ref, fg_ref, hp_ref, dx_ref, dgp_ref, dpe_ref, dfg_ref, loss_ref):
        x1 = x1_ref[...]
        hp_ref[...] = _rms(x1, pg_ref[...]).astype(hp_ref.dtype)
        gp = _dot(hp_ref[...], wpg_ref[...])
        pe = _dot(p_ref[...], wple_ref[...])
        loss, vjp = jax.vjp(_f_loss, x1, gp, pe, t_ref[...], fg_ref[...])
        dx, dgp, dpe, _, dfg = vjp(jnp.ones((), F32))
        dx_ref[...] = dx
        dgp_ref[...] = dgp.astype(dgp_ref.dtype)
        dpe_ref[...] = dpe.astype(dpe_ref.dtype)

        @pl.when(pl.program_id(0) == 0)
        def _():
            dfg_ref[...] = jnp.zeros_like(dfg_ref)
            loss_ref[...] = jnp.zeros_like(loss_ref)

        dfg_ref[...] += dfg
        loss_ref[...] += jnp.full(loss_ref.shape, loss, F32)

    row = _row_spec(tm, D_MODEL, 0)
    act = jax.ShapeDtypeStruct((nrows, D_MODEL), ACT_DTYPE)
    return pl.pallas_call(
        body, name="head", grid=(nrows // tm,),
        in_specs=[row, _row_spec(tm, PLE_DIM, 0), row, _whole_spec((1, D_MODEL)), _whole_spec(w_pg.shape),
                  _whole_spec(w_ple.shape), _whole_spec((1, D_MODEL))],
        out_specs=[row, row, row, row, _whole_spec((1, D_MODEL)), _whole_spec((1, 128))],
        out_shape=[act, jax.ShapeDtypeStruct((nrows, D_MODEL), F32), act, act, jax.ShapeDtypeStruct((1, D_MODEL), F32),
                   jax.ShapeDtypeStruct((1, 128), F32)],
        compiler_params=pltpu.CompilerParams(dimension_semantics=("arbitrary",)),
    )(x1, p, tgt, ple_g, w_pg, w_ple, fg)


def _shift_rows(cur, edge, j, up):
    tm = cur.shape[0]
    row = lax.broadcasted_iota(jnp.int32, cur.shape, 0)
    if up:
        sh = pltpu.roll(cur, tm - j, 0)
        e = jnp.tile(pltpu.roll(edge, 8 - j, 0), (tm // 8, 1))
        return jnp.where(row >= tm - j, e, sh)
    sh = pltpu.roll(cur, j, 0)
    e = jnp.tile(pltpu.roll(edge, j, 0), (tm // 8, 1))
    return jnp.where(row < j, e, sh)


def _conv_pre(cur, prev, w, b):
    acc = cur * w[CONV_K - 1:CONV_K] + b
    for j in range(1, CONV_K):
        acc = acc + _shift_rows(cur, prev, j, up=False) * w[CONV_K - 1 - j:CONV_K - j]
    return acc


def _halo_specs(tm, nrows, cb, before):
    nb = tm // 8
    last = nrows // 8 - 1
    if before:
        return pl.BlockSpec((8, XBC_W), lambda i: (jnp.maximum(i * nb - 1, 0), cb))
    return pl.BlockSpec((8, XBC_W), lambda i: (jnp.minimum((i + 1) * nb, last), cb))


def _conv_fwd(proj, conv_w, conv_b, *, tm, nrows):
    tm = min(tm, nrows)

    def body(cur_ref, prev_ref, w_ref, b_ref, o_ref, pre_ref):
        prev = jnp.where(pl.program_id(0) == 0, 0.0, prev_ref[...].astype(F32))
        pre = _conv_pre(cur_ref[...].astype(F32), prev, w_ref[...], b_ref[...])
        o_ref[...] = jax.nn.silu(pre).astype(o_ref.dtype)
        pre_ref[...] = pre.astype(pre_ref.dtype)

    out = jax.ShapeDtypeStruct((nrows, XBC_W), ACT_DTYPE)
    return pl.pallas_call(
        body, name="conv_fwd", grid=(nrows // tm,),
        in_specs=[_row_spec(tm, XBC_W, XBC_CB), _halo_specs(tm, nrows, XBC_CB, True),
                  _whole_spec((CONV_K, XBC_W)), _whole_spec((1, XBC_W))],
        out_specs=[_row_spec(tm, XBC_W, 0)] * 2, out_shape=[out, out],
        compiler_params=pltpu.CompilerParams(dimension_semantics=("parallel",)),
    )(proj, proj, conv_w, conv_b)


def _conv_bwd_act(pre, dact, *, tm, nrows):
    tm = min(tm, nrows)
    nb = N_GROUPS * N_STATE

    def body(pre_ref, dxs_ref, dbm_ref, dcm_ref, dpre_ref, db_ref):
        pre = pre_ref[...].astype(F32)
        sg = jax.nn.sigmoid(pre)
        dy = jnp.concatenate([dxs_ref[...], dbm_ref[...], dcm_ref[...]], axis=1).astype(F32)
        dpre = dy * sg * (1.0 + pre * (1.0 - sg))
        dpre_ref[...] = dpre.astype(dpre_ref.dtype)

        @pl.when(pl.program_id(0) == 0)
        def _():
            db_ref[...] = jnp.zeros_like(db_ref)

        db_ref[...] += jnp.sum(dpre, axis=0, keepdims=True)

    return pl.pallas_call(
        body, name="conv_bwd_act", grid=(nrows // tm,),
        in_specs=[_row_spec(tm, XBC_W, 0), _row_spec(tm, D_INNER, 0), _row_spec(tm, nb, 0), _row_spec(tm, nb, 0)],
        out_specs=[_row_spec(tm, XBC_W, 0), _whole_spec((1, XBC_W))],
        out_shape=[jax.ShapeDtypeStruct((nrows, XBC_W), ACT_DTYPE), jax.ShapeDtypeStruct((1, XBC_W), F32)],
        compiler_params=pltpu.CompilerParams(dimension_semantics=("arbitrary",)),
    )(pre, *dact)


def _conv_bwd_x(dpre, proj, conv_w, dproj, *, tm, nrows):
    tm = min(tm, nrows)
    ntiles = nrows // tm

    def body(cur_ref, nxt_ref, x_ref, w_ref, _, o_ref, dw_ref):
        cur = cur_ref[...].astype(F32)
        nxt = jnp.where(pl.program_id(0) == ntiles - 1, 0.0, nxt_ref[...].astype(F32))
        x = x_ref[...].astype(F32)
        w = w_ref[...]

        @pl.when(pl.program_id(0) == 0)
        def _():
            dw_ref[...] = jnp.zeros_like(dw_ref)

        acc = cur * w[CONV_K - 1:CONV_K]
        dw_ref[CONV_K - 1:CONV_K, :] += jnp.sum(cur * x, axis=0, keepdims=True)
        for j in range(1, CONV_K):
            u = _shift_rows(cur, nxt, j, up=True)
            acc = acc + u * w[CONV_K - 1 - j:CONV_K - j]
            dw_ref[CONV_K - 1 - j:CONV_K - j, :] += jnp.sum(u * x, axis=0, keepdims=True)
        o_ref[...] = acc.astype(o_ref.dtype)

    return pl.pallas_call(
        body, name="conv_bwd_x", grid=(ntiles,),
        in_specs=[_row_spec(tm, XBC_W, 0), _halo_specs(tm, nrows, 0, False), _row_spec(tm, XBC_W, XBC_CB),
                  _whole_spec((CONV_K, XBC_W)), ANY],
        out_specs=[_row_spec(tm, XBC_W, XBC_CB), _whole_spec((CONV_K, XBC_W))],
        out_shape=[jax.ShapeDtypeStruct(dproj.shape, dproj.dtype), jax.ShapeDtypeStruct((CONV_K, XBC_W), F32)],
        input_output_aliases={4: 0},
        compiler_params=pltpu.CompilerParams(dimension_semantics=("arbitrary",)),
    )(dpre, dpre, proj, conv_w, dproj)


SSD_SPAN = 4
SSD_FWD_SPANS = 4
_XS_GW = D_INNER // N_GROUPS
_NT = (((1,), (1,)), ((), ()))
_TN = (((0,), (0,)), ((), ()))


def _bf16_terms(x, terms):
    parts, rest = [], x
    for _ in range(terms):
        part = rest.astype(jnp.bfloat16)
        parts.append(part)
        rest = rest - part.astype(F32)
    return parts


def _head_lane_matrix():
    return (lax.broadcasted_iota(jnp.int32, (128, _XS_GW), 0)
            == lax.broadcasted_iota(jnp.int32, (128, _XS_GW), 1) // HEAD_DIM).astype(jnp.bfloat16)


@functools.partial(jax.custom_vjp, nondiff_argnums=(1,))
def _head_lanes(cols, terms):
    e = _head_lane_matrix()
    return sum(jnp.dot(t, e, preferred_element_type=F32) for t in _bf16_terms(cols, terms))


def _head_lanes_fwd(cols, terms):
    return _head_lanes(cols, terms), None


def _head_lanes_bwd(terms, _, g):
    e = _head_lane_matrix()
    return (sum(lax.dot_general(t, e, _NT, preferred_element_type=F32) for t in _bf16_terms(g, 2)),)


_head_lanes.defvjp(_head_lanes_fwd, _head_lanes_bwd)


def _ssd_chunk(k, xs, bm, cm, dtr, hprev, dtb, alog, dsk):
    causal, tri, lo = k
    dt = jax.nn.softplus(dtr + dtb)
    da = dt * (-jnp.exp(alog))
    cs = jnp.dot(tri, da, precision=lax.Precision.HIGHEST, preferred_element_type=F32)
    cst = cs.T
    cs_l = _head_lanes(cs, 3)
    xdt = xs * _head_lanes(dt, 2)
    cb = _dot(cm, bm, _NT)
    yd = []
    for q in range(PAIRS_PER_GROUP):
        xq = xdt[:, 128 * q:128 * (q + 1)]
        y2 = [_dot(cb * jnp.exp(jnp.where(causal, cs[:, h:h + 1] - cst[h:h + 1, :], -jnp.inf)), xq)
              for h in (2 * q, 2 * q + 1)]
        yd.append(jnp.where(lo, y2[0], y2[1]))
    y_off = jnp.exp(cs_l) * _dot(cm, hprev, _NT)
    st = _dot(xdt * jnp.exp(cs_l[CHUNK - 1:CHUNK, :] - cs_l), bm, _TN)
    cdec = jnp.exp(cs[CHUNK - 1:CHUNK, :])
    cd_rows = jnp.concatenate(
        [jnp.broadcast_to(cdec[:, h:h + 1], (HEAD_DIM, N_STATE)) for h in range(HEADS_PER_GROUP)], axis=0)
    dsk_l = _head_lanes(jnp.broadcast_to(dsk, (8, 128)), 2)[:1]
    y = jnp.concatenate(yd, axis=1) + y_off + xs * dsk_l
    return y, cd_rows * hprev + st


def _ssd_span(xs, bm, cm, dtr, h0, dtb, alog, dsk):
    li = lax.broadcasted_iota(jnp.int32, (CHUNK, CHUNK), 0)
    si = lax.broadcasted_iota(jnp.int32, (CHUNK, CHUNK), 1)
    causal = li >= si
    k = (causal, causal.astype(F32), si < HEAD_DIM)
    h, ys = h0, []
    for t in range(xs.shape[0] // CHUNK):
        r = slice(t * CHUNK, (t + 1) * CHUNK)
        y, h = _ssd_chunk(k, xs[r], bm[r], cm[r], dtr[r], h, dtb, alog, dsk)
        ys.append(y)
    return (ys[0] if len(ys) == 1 else jnp.concatenate(ys, axis=0)), h


def _ssd_specs(rev, nsteps, rows):
    def s_of(s):
        return nsteps - 1 - s if rev else s

    xs = pl.BlockSpec((rows, _XS_GW), lambda g, s: (s_of(s), g))
    bm = pl.BlockSpec((rows, N_STATE), lambda g, s: (s_of(s), D_INNER // N_STATE + g))
    cm = pl.BlockSpec((rows, N_STATE), lambda g, s: (s_of(s), D_INNER // N_STATE + N_GROUPS + g))
    dt = pl.BlockSpec((rows, 128), lambda g, s: (s_of(s), g))
    par = pl.BlockSpec((1, 128), lambda g, s: (0, g))
    st = pl.BlockSpec((None, None, _XS_GW, N_STATE), lambda g, s: (g, s_of(s), 0, 0))
    return xs, bm, cm, dt, par, st


def _ssd_fwd(act, dtr, dtb, alog, dsk, *, nrows):
    span = CHUNK * min(SSD_SPAN, nrows // CHUNK)
    per_step = SSD_FWD_SPANS if nrows % (SSD_FWD_SPANS * span) == 0 else 1
    rows = per_step * span
    nsteps = nrows // rows
    xs, bm, cm, dt, par, _ = _ssd_specs(False, nsteps, rows)
    st = pl.BlockSpec((None, per_step, _XS_GW, N_STATE), lambda g, s: (g, s, 0, 0))

    def body(xs_ref, b_ref, c_ref, dt_ref, dtb_ref, al_ref, dk_ref, y_ref, st_ref, h_ref):
        @pl.when(pl.program_id(1) == 0)
        def _():
            h_ref[...] = jnp.zeros_like(h_ref)

        h = h_ref[...]
        for i in range(per_step):
            r = slice(i * span, (i + 1) * span)
            st_ref[i] = h
            y, h = _ssd_span(xs_ref[r, :].astype(F32), b_ref[r, :].astype(F32), c_ref[r, :].astype(F32), dt_ref[r, :],
                             h, dtb_ref[...], al_ref[...], dk_ref[...])
            y_ref[r, :] = y.astype(y_ref.dtype)
        h_ref[...] = h

    return pl.pallas_call(
        body, name="ssd_fwd", grid=(N_GROUPS, nsteps),
        in_specs=[xs, bm, cm, dt, par, par, par], out_specs=[xs, st],
        out_shape=[jax.ShapeDtypeStruct((nrows, D_INNER), ACT_DTYPE),
                   jax.ShapeDtypeStruct((N_GROUPS, nrows // span, _XS_GW, N_STATE), F32)],
        scratch_shapes=[pltpu.VMEM((_XS_GW, N_STATE), F32)],
        compiler_params=pltpu.CompilerParams(dimension_semantics=("arbitrary", "arbitrary")),
    )(act, act, act, dtr, dtb, alog, dsk)


def _ssd_bwd(act, dtr, dtb, alog, dsk, states, dy, *, nrows):
    rows = CHUNK * min(SSD_SPAN, nrows // CHUNK)
    nsteps = nrows // rows
    xs, bm, cm, dt, par, st = _ssd_specs(True, nsteps, rows)

    def body(xs_ref, b_ref, c_ref, dt_ref, dtb_ref, al_ref, dk_ref, st_ref, dy_ref,
             dxs_ref, db_ref, dc_ref, ddt_ref, ddtb_ref, dal_ref, ddk_ref, dh_ref):
        @pl.when(pl.program_id(1) == 0)
        def _():
            dh_ref[...] = jnp.zeros_like(dh_ref)
            ddtb_ref[...] = jnp.zeros_like(ddtb_ref)
            dal_ref[...] = jnp.zeros_like(dal_ref)
            ddk_ref[...] = jnp.zeros_like(ddk_ref)

        _, vjp = jax.vjp(_ssd_span, xs_ref[...].astype(F32), b_ref[...].astype(F32), c_ref[...].astype(F32),
                         dt_ref[...], st_ref[...], dtb_ref[...], al_ref[...], dk_ref[...])
        dxs, db, dc, ddt, dh, ddtb, dal, ddk = vjp((dy_ref[...].astype(F32), dh_ref[...]))
        dxs_ref[...] = dxs.astype(dxs_ref.dtype)
        db_ref[...] = db.astype(db_ref.dtype)
        dc_ref[...] = dc.astype(dc_ref.dtype)
        ddt_ref[...] = ddt
        dh_ref[...] = dh
        ddtb_ref[...] += ddtb
        dal_ref[...] += dal
        ddk_ref[...] += ddk

    nb = N_GROUPS * N_STATE
    bspec = pl.BlockSpec((rows, N_STATE), lambda g, s: (nsteps - 1 - s, g))
    return pl.pallas_call(
        body, name="ssd_bwd", grid=(N_GROUPS, nsteps),
        in_specs=[xs, bm, cm, dt, par, par, par, st, xs],
        out_specs=[xs, bspec, bspec, dt, par, par, par],
        out_shape=[jax.ShapeDtypeStruct((nrows, D_INNER), ACT_DTYPE), jax.ShapeDtypeStruct((nrows, nb), ACT_DTYPE),
                   jax.ShapeDtypeStruct((nrows, nb), ACT_DTYPE), jax.ShapeDtypeStruct((nrows, DT_W), F32),
                   jax.ShapeDtypeStruct((1, DT_W), F32), jax.ShapeDtypeStruct((1, DT_W), F32),
                   jax.ShapeDtypeStruct((1, DT_W), F32)],
        scratch_shapes=[pltpu.VMEM((_XS_GW, N_STATE), F32)],
        compiler_params=pltpu.CompilerParams(dimension_semantics=("arbitrary", "arbitrary")),
    )(act, act, act, dtr, dtb, alog, dsk, states, dy)


def _add_epilogue(acc, r):
    return r + acc


def _rms_and_skip(x, g):
    return _rms(x, g), x


def _forward_backward(x, p, tgt, w, late_weights=None, after=()):
    s = x.shape[0]
    act_t, f32 = ACT_DTYPE, F32
    mm = functools.partial(_matmul)
    h, proj, dtr = _rows_matmul("proj", _f_rms, [(x, D_MODEL, 0)], [w["norm_g"]], w["w_main"], out_dtype=act_t,
                                n=MAIN_W, k=D_MODEL, tm=1024, tn=2 * MM_TILE, nrows=s, after=after, side=(w["w_dt"], f32),
                                ahead=True)
    act, conv_pre = _conv_fwd(proj, w["conv_w"], w["conv_b"], tm=512, nrows=s)
    y, states = _ssd_fwd(act, dtr, w["dt_bias"], w["a_log"], w["d_skip"], nrows=s)
    if late_weights is not None:
        w = {**w, **late_weights(states)}
    a_pars = [w["ln_a_g"], w["ln_a_b"], w["w_s"], w["b_s"]]
    y_a, o_a = _rows_matmul("out_a", _f_branch_a, [(proj, UVZ_W, UVZ_CB)], a_pars, w["w_oa"], out_dtype=act_t,
                            n=D_MODEL, k=E_A, tm=512, nrows=s)
    gn_rows = [(y, D_INNER, 0), (proj, ZB_W, ZB_CB)]
    y_b, o_b = _rows_matmul("out_b", _f_gnorm, gn_rows, [w["ssm_norm_g"]], w["w_ob"], out_dtype=act_t,
                            n=D_MODEL, k=D_INNER, tm=1024, nrows=s)
    mg_rows = [(proj, G_W, G_CB), (o_a, D_MODEL, 0), (o_b, D_MODEL, 0)]
    merged, x1 = _rows_matmul("out_proj", _f_merge, mg_rows, [], w["w_out"], out_dtype=f32, n=D_MODEL, k=D_MODEL,
                              tm=1024, nrows=s, extras=(x,), epilogue=_add_epilogue)
    g = {}
    hp, dx2, dgp, dpe, g["final_g"], loss = _head(x1, p, tgt, w["ple_norm_g"], w["w_pg"], w["w_ple"], w["final_g"],
                                                   tm=512, nrows=s)
    g["w_pg"] = mm(hp, dgp, mode="tn", name="d_w_pg", out_dtype=f32, m=D_MODEL, n=D_MODEL, k=s, tn=MM_TILE // 2)
    g["w_ple"] = mm(p, dpe, mode="tn", name="d_w_ple", out_dtype=f32, m=PLE_DIM, n=D_MODEL, k=s)
    dx1, g["ple_norm_g"] = _rows_vjp_call(
        "ple_norm_bwd", _rms_and_skip, [(x1, D_MODEL, 0)], [w["ple_norm_g"]], [(dx2, D_MODEL, 0)],
        [(f32, None)], tm=1024, nrows=s, cot_mm=(dgp, w["w_pg"], None))
    g["w_out"] = mm(merged, dx1, mode="tn", name="d_w_out", out_dtype=f32, m=D_MODEL, n=D_MODEL, k=s)
    dproj, do_a, do_b = _rows_vjp_call(
        "merge_bwd", _f_merge, mg_rows, [], [],
        [(act_t, (None, MAIN_W, G_CB)), (act_t, None), (act_t, None)], tm=1024, nrows=s, cot_mm=(dx1, w["w_out"], None))
    g["w_oa"] = mm(y_a, do_a, mode="tn", name="d_w_oa", out_dtype=f32, m=E_A, n=D_MODEL, k=s, tn=MM_TILE // 2)
    g["w_ob"] = mm(y_b, do_b, mode="tn", name="d_w_ob", out_dtype=f32, m=D_INNER, n=D_MODEL, k=s)
    dy, dproj, g["ssm_norm_g"] = _rows_vjp_call(
        "gnorm_bwd", _f_gnorm, gn_rows, [w["ssm_norm_g"]], [],
        [(act_t, None), (act_t, (dproj, MAIN_W, ZB_CB))], tm=512, nrows=s, cot_mm=(do_b, w["w_ob"], None))
    dxs, dbm, dcm, ddtr, g["dt_bias"], g["a_log"], g["d_skip"] = _ssd_bwd(
        act, dtr, w["dt_bias"], w["a_log"], w["d_skip"], states, dy, nrows=s)
    dpre, g["conv_b"] = _conv_bwd_act(conv_pre, (dxs, dbm, dcm), tm=1024, nrows=s)
    dproj, g["conv_w"] = _conv_bwd_x(dpre, proj, w["conv_w"], dproj, tm=512, nrows=s)
    dproj, g["ln_a_g"], g["ln_a_b"], g["w_s"], g["b_s"] = _rows_vjp_call(
        "branch_a_bwd", _f_branch_a, [(proj, UVZ_W, UVZ_CB)], a_pars, [],
        [(act_t, (dproj, MAIN_W, UVZ_CB))], tm=512, nrows=s, cot_mm=(do_a, w["w_oa"], None))
    g["w_main"] = mm(h, dproj, mode="tn", name="d_w_main", out_dtype=f32, m=D_MODEL, n=MAIN_W, k=s, tn=MM_TILE // 2)
    g["w_dt"] = mm(h, ddtr, mode="tn", name="d_w_dt", out_dtype=f32, m=D_MODEL, n=DT_W, k=s)
    return loss, g, (dproj, ddtr, dx1)


def _input_grad(x, w, ctx, after=()):
    dproj, ddtr, dx1 = ctx
    s = x.shape[0]
    dh = _matmul(dproj, w["w_main"], mode="nt", name="d_h_main", out_dtype=F32, m=s, n=D_MODEL, k=MAIN_W,
                 tm=MM_TILE // 2, after=after)
    return _rows_vjp_call(
        "pre_norm_bwd", _rms_and_skip, [(x, D_MODEL, 0)], [w["norm_g"]], [(dx1, D_MODEL, 0)],
        [(F32, None)], tm=1024, nrows=s, cot_mm=(ddtr, w["w_dt"], dh))


def _local_step(x, p, tgt, w):
    loss, g, ctx = _forward_backward(x, p, tgt, w)
    grad_x, g["norm_g"] = _input_grad(x, w, ctx)
    return loss, grad_x, g


_O_ZB = 3 * E_A
_O_XBC = _O_ZB + D_INNER
_O_DT = _O_XBC + CONV_DIM
_O_G = _O_DT + N_HEADS


def _heads_to_lanes(v):
    r = v.shape[0]
    v = v.reshape(r, N_GROUPS, HEADS_PER_GROUP)
    return jnp.pad(v, ((0, 0), (0, 0), (0, 128 - HEADS_PER_GROUP))).reshape(r, DT_W)


def _lanes_to_heads(v):
    r = v.shape[0]
    return v.reshape(r, N_GROUPS, 128)[:, :, :HEADS_PER_GROUP].reshape(r, N_HEADS)


def _block_cols(blocks, a, b):
    parts = []
    for k in range(N_CHIPS):
        lo, hi = max(a, k * W_IN_BLOCK), min(b, (k + 1) * W_IN_BLOCK)
        if lo < hi:
            parts.append(blocks[k][:, lo - k * W_IN_BLOCK:hi - k * W_IN_BLOCK])
    return parts


_W_IN_SEGMENTS = ((0, _O_ZB, "m", 0), (_O_ZB, _O_XBC, "m", UVZ_W + XBC_W), (_O_XBC, _O_DT, "m", UVZ_W),
                  (_O_DT, _O_G, "d", 0), (_O_G, N_IN, "m", MAIN_W - G_W))


def _w_in_grad_blocks(gm, gdt):
    blocks = []
    for k in range(N_CHIPS):
        a, b = k * W_IN_BLOCK, (k + 1) * W_IN_BLOCK
        parts = []
        for s, e, src, off in _W_IN_SEGMENTS:
            lo, hi = max(a, s), min(b, e)
            if lo < hi:
                parts.append((gm if src == "m" else gdt)[:, off + lo - s:off + hi - s])
        blocks.append(jnp.concatenate(parts, axis=1))
    return jnp.stack(blocks)


def _layout_weights(f, w_in_blocks=None):
    w = dict(f)
    if w_in_blocks is None:
        w_in = w.pop("w_in")
        w_in_blocks = jnp.stack([w_in[:, k * W_IN_BLOCK:(k + 1) * W_IN_BLOCK] for k in range(N_CHIPS)])
    cols = functools.partial(_block_cols, w_in_blocks)
    w["w_main"] = jnp.concatenate(cols(0, _O_ZB) + cols(_O_XBC, _O_DT) + cols(_O_ZB, _O_XBC) + cols(_O_G, N_IN), axis=1)
    w["w_dt"] = _heads_to_lanes(jnp.concatenate(cols(_O_DT, _O_G), axis=1))
    w["b_s"] = f["b_s"].reshape(G_A, CHUNK, 1)
    for n in ("dt_bias", "a_log", "d_skip"):
        w[n] = _heads_to_lanes(f[n])
    return w


def _natural_grads(g):
    out = dict(g)
    gm = out.pop("w_main")
    gdt = _lanes_to_heads(out.pop("w_dt"))
    out["w_in"] = jnp.concatenate(
        [gm[:, :UVZ_W], gm[:, UVZ_W + XBC_W:UVZ_W + XBC_W + ZB_W], gm[:, UVZ_W:UVZ_W + XBC_W], gdt, gm[:, MAIN_W - G_W:]],
        axis=1)
    out["b_s"] = g["b_s"].reshape(G_A, CHUNK)
    for n in ("dt_bias", "a_log", "d_skip"):
        out[n] = _lanes_to_heads(g[n])
    return out


def _place():
    return lax.axis_index("x"), lax.axis_index("y"), lax.axis_index("c")


def _other_chips(x, y):
    return [(1 - x, y), (x, 1 - y), (1 - x, 1 - y)]


def _rcopy(src, dst, ssem, rsem, dev):
    return pltpu.make_async_remote_copy(src_ref=src, dst_ref=dst, send_sem=ssem, recv_sem=rsem,
                                        device_id=dev, device_id_type=MESH)


def _half(ref_rows, half):
    hs = ref_rows // 2
    return pl.ds(pl.multiple_of(half * hs, 16), hs)


GATHER_CHUNKS = 8


def _gather_weights(shards, conv_shard):
    nw, nq = len(shards), GATHER_CHUNKS

    def body(*refs):
        sh, cv = refs[:nw], refs[nw]
        out, cvo = refs[nw + 1:2 * nw + 1], refs[2 * nw + 1]
        ici_s, ici_r, fw_s, fw_r, own_s, own_r, cv_s, cv_r = refs[2 * nw + 2:]
        x, y, c = _place()
        me, sib, chips = 2 * x + y, (x, y, 1 - c), _other_chips(x, y)
        own = [_rcopy(sh[w], out[w].at[me], own_s.at[w], own_r.at[w], sib) for w in range(nw)]
        own.append(_rcopy(cv, cvo.at[me], own_s.at[nw], own_r.at[nw], sib))
        for cp in own:
            cp.start()
        pieces = [(w, q, j) for w in range(nw) for q in range(nq) for j in range(3)]

        def rows(w, half, q):
            hs = sh[w].shape[0] // 2
            return pl.ds(pl.multiple_of(half * hs + q * (hs // nq), 16), hs // nq)

        def sem(w, q, j):
            return (3 * w + j) * nq + q

        def landed(w, q, j, half):
            return out[w].at[2 * chips[j][0] + chips[j][1], rows(w, half, q)]

        sends = [_rcopy(sh[w].at[rows(w, c, q)], out[w].at[me, rows(w, c, q)], ici_s.at[sem(w, q, j)],
                        ici_r.at[sem(w, q, j)], (*chips[j], c)) for w, q, j in pieces if j < 2]
        for j, chip in enumerate(chips):
            sends.append(_rcopy(cv, cvo.at[me], cv_s.at[j], cv_r.at[j], (*chip, c)))
        for cp in sends:
            cp.start()

        def arrived(w, q, j):
            slab = landed(w, q, j, c)
            _rcopy(slab, slab, ici_s.at[sem(w, q, j)], ici_r.at[sem(w, q, j)], (*chips[j], c)).wait_recv()
            fwd = _rcopy(slab, slab, fw_s.at[sem(w, q, j)], fw_r.at[sem(w, q, j)], sib)
            fwd.start()
            sends.append(fwd)
            if j < 2 and (q % 2 == 0) == (j == 0):
                hop = _rcopy(slab, slab, ici_s.at[sem(w, q, 2)], ici_r.at[sem(w, q, 2)], (*chips[1 - j], c))
                hop.start()
                sends.append(hop)

        for w, q, j in pieces:
            if j < 2:
                arrived(w, q, j)
        for w, q, j in pieces:
            if j == 2:
                arrived(w, q, j)
        for j, chip in enumerate(chips):
            blk = cvo.at[2 * chip[0] + chip[1]]
            _rcopy(blk, blk, cv_s.at[j], cv_r.at[j], (*chip, c)).wait_recv()
        for w, q, j in pieces:
            slab = landed(w, q, j, 1 - c)
            _rcopy(slab, slab, fw_s.at[sem(w, q, j)], fw_r.at[sem(w, q, j)], sib).wait_recv()
        for cp in sends:
            cp.wait_send()
        for cp in own:
            cp.wait()

    dma = pltpu.SemaphoreType.DMA
    n_ici = 3 * nw * nq
    return pl.pallas_call(
        body, name="gather_weights",
        in_specs=[ANY] * (nw + 1), out_specs=[ANY] * (nw + 1),
        out_shape=[jax.ShapeDtypeStruct((N_CHIPS,) + s.shape, s.dtype) for s in shards]
        + [jax.ShapeDtypeStruct((N_CHIPS,) + conv_shard.shape, conv_shard.dtype)],
        scratch_shapes=[dma((n_ici,)), dma((n_ici,)), dma((n_ici,)), dma((n_ici,)), dma((nw + 1,)), dma((nw + 1,)),
                        dma((3,)), dma((3,))],
    )(*shards, conv_shard)


_HBM = pl.BlockSpec(memory_space=pltpu.HBM)
_SEM = pl.BlockSpec(memory_space=pltpu.SEMAPHORE)
_EFFECT = pltpu.SideEffectType.DATAFLOW_SIDE_EFFECTING


def _late_gather_copies(sh, out, s_sem, r_sem):
    x, y, c = _place()
    to = [(*chip, c) for chip in _other_chips(x, y)] + [(x, y, 1 - c)]
    return [_rcopy(sh[w], out[w].at[2 * x + y], s_sem.at[4 * w + j], r_sem.at[4 * w + j], dev)
            for w in range(len(sh)) for j, dev in enumerate(to)]


def _late_gather_start(shards):
    n = len(shards)
    lands = [lax.empty((N_CHIPS,) + a.shape, a.dtype) for a in shards]

    def body(*refs):
        for cp in _late_gather_copies(refs[:n], refs[n:2 * n], refs[2 * n], refs[2 * n + 1]):
            cp.start()
        refs[-1][...] = jnp.zeros_like(refs[-1])

    dma = pltpu.SemaphoreType.DMA
    hbm = [pltpu.with_memory_space_constraint(a, pltpu.HBM) for a in list(shards) + lands]
    out = pl.pallas_call(
        body, name="late_gather_start",
        out_shape=[dma((4 * n,)), dma((4 * n,))] + [pltpu.HBM(a.shape, a.dtype) for a in hbm]
        + [jax.ShapeDtypeStruct((8, 128), F32)],
        in_specs=[_HBM] * (2 * n), out_specs=[_SEM, _SEM] + [_HBM] * (2 * n) + [pl.BlockSpec(memory_space=pltpu.VMEM)],
        input_output_aliases={i: 2 + i for i in range(2 * n)},
        compiler_params=pltpu.CompilerParams(has_side_effects=_EFFECT),
    )(*hbm)
    return out[0], out[1], out[2:2 + n], out[2 + n:2 + 2 * n], out[-1]


def _late_gather_wait(s_sem, r_sem, srcs, lands, after):
    n = len(srcs)

    def body(*refs):
        for cp in _late_gather_copies(refs[:n], refs[n:2 * n], refs[2 * n], refs[2 * n + 1]):
            cp.wait_send()
            cp.wait_recv()

    out = pl.pallas_call(
        body, name="late_gather_wait",
        out_shape=[pltpu.HBM(a.shape, a.dtype) for a in list(srcs) + list(lands)],
        in_specs=[_HBM] * (2 * n) + [_SEM, _SEM, ANY], out_specs=[_HBM] * (2 * n),
        input_output_aliases={i: i for i in range(2 * n)},
        compiler_params=pltpu.CompilerParams(has_side_effects=_EFFECT),
    )(*srcs, *lands, s_sem, r_sem, after)
    return out[n:]


def _swap_with_sibling(arrs):
    n = len(arrs)

    def body(*refs):
        src, dst, s_sem, r_sem = refs[:n], refs[n:2 * n], refs[2 * n], refs[2 * n + 1]
        x, y, c = _place()
        cps = [_rcopy(src[i], dst[i], s_sem.at[i], r_sem.at[i], (x, y, 1 - c)) for i in range(n)]
        for cp in cps:
            cp.start()
        for cp in cps:
            cp.wait()

    dma = pltpu.SemaphoreType.DMA
    return pl.pallas_call(
        body, name="swap_with_sibling", in_specs=[ANY] * n, out_specs=[ANY] * n,
        out_shape=[jax.ShapeDtypeStruct(a.shape, a.dtype) for a in arrs], scratch_shapes=[dma((n,)), dma((n,))],
    )(*arrs)


def _scatter_copies(src, land, s_sem, r_sem):
    x, y, c = _place()
    return [_rcopy(src[i].at[2 * chip[0] + chip[1]], land[i].at[j], s_sem.at[3 * i + j], r_sem.at[3 * i + j], (*chip, c))
            for i in range(len(src)) for j, chip in enumerate(_other_chips(x, y))]


def _scatter_blocks_start(arrs):
    n = len(arrs)
    lands = [lax.empty((3,) + a.shape[1:], a.dtype) for a in arrs]

    def body(*refs):
        src, land, s_sem, r_sem, token = refs[:n], refs[n:2 * n], refs[2 * n], refs[2 * n + 1], refs[-1]
        for cp in _scatter_copies(src, land, s_sem, r_sem):
            cp.start()
        token[...] = jnp.zeros_like(token)

    dma = pltpu.SemaphoreType.DMA
    hbm = [pltpu.with_memory_space_constraint(a, pltpu.HBM) for a in list(arrs) + lands]
    out = pl.pallas_call(
        body, name="scatter_blocks_start",
        out_shape=[dma((3 * n,)), dma((3 * n,))] + [pltpu.HBM(a.shape, a.dtype) for a in hbm]
        + [jax.ShapeDtypeStruct((8, 128), F32)],
        in_specs=[_HBM] * (2 * n), out_specs=[_SEM, _SEM] + [_HBM] * (2 * n) + [pl.BlockSpec(memory_space=pltpu.VMEM)],
        input_output_aliases={i: 2 + i for i in range(2 * n)},
        compiler_params=pltpu.CompilerParams(has_side_effects=_EFFECT),
    )(*hbm)
    return out[0], out[1], out[2:2 + n], out[2 + n:2 + 2 * n], out[-1]


def _scatter_blocks_wait(s_sem, r_sem, srcs, lands, after):
    n = len(srcs)

    def body(*refs):
        src, land, s_sem, r_sem = refs[:n], refs[n:2 * n], refs[2 * n], refs[2 * n + 1]
        for cp in _scatter_copies(src, land, s_sem, r_sem):
            cp.wait_send()
            cp.wait_recv()

    out = pl.pallas_call(
        body, name="scatter_blocks_wait",
        out_shape=[pltpu.HBM(a.shape, a.dtype) for a in list(srcs) + list(lands)],
        in_specs=[_HBM] * (2 * n) + [_SEM, _SEM] + [ANY] * len(after), out_specs=[_HBM] * (2 * n),
        input_output_aliases={i: i for i in range(2 * n)},
        compiler_params=pltpu.CompilerParams(has_side_effects=_EFFECT),
    )(*srcs, *lands, s_sem, r_sem, *after)
    return out[:n], out[n:]


def _share_halves(arrs):
    n = len(arrs)

    def body(*refs):
        buf, s_sem, r_sem = refs[n:2 * n], refs[2 * n], refs[2 * n + 1]
        x, y, c = _place()
        cps = []
        for i in range(n):
            mine = buf[i].at[_half(buf[i].shape[0], c)]
            cps.append(_rcopy(mine, mine, s_sem.at[i], r_sem.at[i], (x, y, 1 - c)))
        for cp in cps:
            cp.start()
        for i in range(n):
            theirs = buf[i].at[_half(buf[i].shape[0], 1 - c)]
            _rcopy(theirs, theirs, s_sem.at[i], r_sem.at[i], (x, y, 1 - c)).wait_recv()
        for cp in cps:
            cp.wait_send()

    dma = pltpu.SemaphoreType.DMA
    return pl.pallas_call(
        body, name="share_halves", in_specs=[ANY] * n, out_specs=[ANY] * n,
        out_shape=[jax.ShapeDtypeStruct(a.shape, a.dtype) for a in arrs],
        input_output_aliases={i: i for i in range(n)}, scratch_shapes=[dma((n,)), dma((n,))],
    )(*arrs)


def _small_gather_copies(src, land, s_sem, r_sem):
    x, y, c = _place()
    cps = []
    for d in range(1, N_DEV):
        peer = ((1 - x) if d & 4 else x), ((1 - y) if d & 2 else y), ((1 - c) if d & 1 else c)
        cps.append(_rcopy(src, land.at[4 * x + 2 * y + c], s_sem.at[d - 1], r_sem.at[d - 1], peer))
    return cps


def _small_gather_start(packed):
    def body(src, land, s_sem, r_sem, _, __, token):
        for cp in _small_gather_copies(src, land, s_sem, r_sem):
            cp.start()
        token[...] = jnp.zeros_like(token)

    dma = pltpu.SemaphoreType.DMA
    hbm = [pltpu.with_memory_space_constraint(a, pltpu.HBM) for a in (packed, lax.empty((N_DEV,) + packed.shape, F32))]
    return pl.pallas_call(
        body, name="small_gather_start",
        out_shape=[dma((N_DEV - 1,)), dma((N_DEV - 1,))] + [pltpu.HBM(a.shape, a.dtype) for a in hbm]
        + [jax.ShapeDtypeStruct((8, 128), F32)],
        in_specs=[_HBM] * 2, out_specs=[_SEM, _SEM, _HBM, _HBM, pl.BlockSpec(memory_space=pltpu.VMEM)],
        input_output_aliases={0: 2, 1: 3}, compiler_params=pltpu.CompilerParams(has_side_effects=_EFFECT),
    )(*hbm)


def _small_gather_wait(s_sem, r_sem, src, land, after):
    def body(src, land, s_sem, r_sem, *_):
        for cp in _small_gather_copies(src, land, s_sem, r_sem):
            cp.wait_send()
            cp.wait_recv()

    return pl.pallas_call(
        body, name="small_gather_wait", out_shape=[pltpu.HBM(src.shape, src.dtype), pltpu.HBM(land.shape, land.dtype)],
        in_specs=[_HBM, _HBM, _SEM, _SEM, ANY], out_specs=[_HBM, _HBM], input_output_aliases={0: 0, 1: 1},
        compiler_params=pltpu.CompilerParams(has_side_effects=_EFFECT),
    )(src, land, s_sem, r_sem, after)


def _small_sum(own, land, dev_arr):
    def body(me_ref, own_ref, land_ref, o_ref):
        acc = jnp.zeros(o_ref.shape, F32)
        for d in range(N_DEV):
            acc = acc + jnp.where(me_ref[0] == d, own_ref[...], land_ref[d])
        o_ref[...] = acc

    return pl.pallas_call(
        body, name="small_sum", out_shape=jax.ShapeDtypeStruct(own.shape, F32),
        grid_spec=pltpu.PrefetchScalarGridSpec(
            num_scalar_prefetch=1, grid=(1,),
            in_specs=[pl.BlockSpec(own.shape, lambda i, m: (0, 0)), pl.BlockSpec(land.shape, lambda i, m: (0, 0, 0))],
            out_specs=pl.BlockSpec(own.shape, lambda i, m: (0, 0))),
    )(dev_arr, own, land)


def _row_tile(rows, cols):
    tr = max(8, min(rows, (1 << 20) // (4 * cols) // 8 * 8))
    while rows % tr:
        tr -= 8
    return tr


def _chip_sum(name, g5, recv, c_arr):
    nb, _, hs, cols = g5.shape
    tr = _row_tile(hs, cols)

    def body(_, a_ref, b_ref, o_ref):
        o_ref[...] = (a_ref[...] + b_ref[...].astype(F32)).astype(o_ref.dtype)

    blk = pl.BlockSpec((None, tr, cols), lambda b, i, c: (b, i, 0))
    return pl.pallas_call(
        body, name=name,
        grid_spec=pltpu.PrefetchScalarGridSpec(
            num_scalar_prefetch=1, grid=(nb, hs // tr),
            in_specs=[pl.BlockSpec((None, None, tr, cols), lambda b, i, c: (b, c[0], i, 0)), blk], out_specs=blk),
        out_shape=jax.ShapeDtypeStruct((nb, hs, cols), WIRE_DTYPE),
    )(c_arr, g5, recv)


def _final_sum(name, own, recv, place_arr):
    _, hs, cols = own.shape
    tr = _row_tile(hs, cols)
    nt = hs // tr

    def body(_, a_ref, r_ref, o_ref):
        o_ref[...] = ((a_ref[...].astype(F32) + r_ref[0].astype(F32)) + r_ref[1].astype(F32)) + r_ref[2].astype(F32)

    return pl.pallas_call(
        body, name=name,
        grid_spec=pltpu.PrefetchScalarGridSpec(
            num_scalar_prefetch=1, grid=(nt,),
            in_specs=[pl.BlockSpec((None, tr, cols), lambda i, m: (m[0], i, 0)),
                      pl.BlockSpec((3, tr, cols), lambda i, m: (0, i, 0))],
            out_specs=pl.BlockSpec((tr, cols), lambda i, m: (m[1] * nt + i, 0))),
        out_shape=jax.ShapeDtypeStruct((2 * hs, cols), F32),
    )(place_arr, own, recv)


def _adamw(w, g, m, v):
    m = ADAM_B1 * m + (1.0 - ADAM_B1) * g
    v = ADAM_B2 * v + (1.0 - ADAM_B2) * (g * g)
    m_hat = m / (1.0 - ADAM_B1 ** ADAM_STEP)
    v_hat = v / (1.0 - ADAM_B2 ** ADAM_STEP)
    return -ADAM_LR * (m_hat / (jnp.sqrt(v_hat) + ADAM_EPS) + ADAM_WD * w), m, v


def _adamw_call(name, w, g, m, v):
    rows, cols = w.shape
    tr = _row_tile(rows, cols)
    if 4 * tr * cols >= (1 << 18):
        blk, steps = pl.BlockSpec((tr, cols), lambda i: (i, 0)), rows // tr
    else:
        blk, steps = pl.BlockSpec((rows, 128), lambda i: (0, i)), cols // 128

    def body(w_ref, g_ref, m_ref, v_ref, d_ref, nm_ref, nv_ref, go_ref):
        g = g_ref[...]
        d_ref[...], nm_ref[...], nv_ref[...] = _adamw(w_ref[...], g, m_ref[...], v_ref[...])
        go_ref[...] = g

    return pl.pallas_call(
        body, name=name, grid=(steps,), in_specs=[blk] * 4, out_specs=[blk] * 4,
        out_shape=[jax.ShapeDtypeStruct(w.shape, F32)] * 4,
        compiler_params=pltpu.CompilerParams(dimension_semantics=("parallel",)),
    )(w, g, m, v)


def _adamw_small(ws, gs, ms, vs):
    n = len(ws)

    def body(*refs):
        for i in range(n):
            w_ref, g_ref, m_ref, v_ref = (refs[k * n + i] for k in range(4))
            d, nm, nv = _adamw(w_ref[...], g_ref[...], m_ref[...], v_ref[...])
            refs[4 * n + i][...] = d
            refs[5 * n + i][...] = nm
            refs[6 * n + i][...] = nv

    out = pl.pallas_call(
        body, name="adamw_small", out_shape=[jax.ShapeDtypeStruct(a.shape, F32) for a in ws] * 3,
    )(*ws, *gs, *ms, *vs)
    return out[:n], out[n:2 * n], out[2 * n:]


_BIG = ("w_in", "w_oa", "w_ob", "w_out", "w_pg", "w_ple")
_SMALL = ("norm_g", "ln_a_g", "ln_a_b", "w_s", "b_s", "conv_w", "conv_b", "dt_bias", "a_log", "d_skip", "ssm_norm_g",
          "ple_norm_g", "final_g")
_WEIGHTS = ("norm_g", "w_in", "ln_a_g", "ln_a_b", "w_s", "b_s", "conv_w", "conv_b", "dt_bias", "a_log", "d_skip",
            "ssm_norm_g", "w_oa", "w_ob", "w_out", "ple_norm_g", "w_pg", "w_ple", "final_g")
_COL_SHARDED = ("w_in", "w_ple")
_PACK = 1024


def _blocks_to_full(col_sharded, blocks):
    if col_sharded:
        return jnp.concatenate([blocks[k] for k in range(N_CHIPS)], axis=1)
    return blocks.reshape(N_CHIPS * blocks.shape[1], blocks.shape[2])


def _full_to_blocks(col_sharded, full):
    if col_sharded:
        w = full.shape[1] // N_CHIPS
        return jnp.stack([full[:, k * w:(k + 1) * w] for k in range(N_CHIPS)])
    return full.reshape(N_CHIPS, full.shape[0] // N_CHIPS, full.shape[1])


def _two_d(n, a):
    if n == "w_s":
        return a.reshape(G_A * CHUNK, CHUNK)
    if n in ("b_s", "conv_w"):
        return a.reshape(a.shape[-2], a.shape[-1])
    return a.reshape(1, a.shape[-1])


def kernel(x, p, norm_g, w_in, ln_a_g, ln_a_b, w_s, b_s, conv_w, conv_b, dt_bias, a_log, d_skip, ssm_norm_g, w_oa, w_ob, w_out, ple_norm_g, w_pg, w_ple, final_g, loss_target, m_norm_g, m_w_in, m_ln_a_g, m_ln_a_b, m_w_s, m_b_s, m_conv_w, m_conv_b, m_dt_bias, m_a_log, m_d_skip, m_ssm_norm_g, m_w_oa, m_w_ob, m_w_out, m_ple_norm_g, m_w_pg, m_w_ple, m_final_g, v_norm_g, v_w_in, v_ln_a_g, v_ln_a_b, v_w_s, v_b_s, v_conv_w, v_conv_b, v_dt_bias, v_a_log, v_d_skip, v_ssm_norm_g, v_w_oa, v_w_ob, v_w_out, v_ple_norm_g, v_w_pg, v_w_ple, v_final_g):
    wt = dict(norm_g=norm_g, w_in=w_in, ln_a_g=ln_a_g, ln_a_b=ln_a_b, w_s=w_s, b_s=b_s, conv_w=conv_w, conv_b=conv_b,
              dt_bias=dt_bias, a_log=a_log, d_skip=d_skip, ssm_norm_g=ssm_norm_g, w_oa=w_oa, w_ob=w_ob, w_out=w_out,
              ple_norm_g=ple_norm_g, w_pg=w_pg, w_ple=w_ple, final_g=final_g)
    mom = dict(norm_g=m_norm_g, w_in=m_w_in, ln_a_g=m_ln_a_g, ln_a_b=m_ln_a_b, w_s=m_w_s, b_s=m_b_s, conv_w=m_conv_w,
               conv_b=m_conv_b, dt_bias=m_dt_bias, a_log=m_a_log, d_skip=m_d_skip, ssm_norm_g=m_ssm_norm_g, w_oa=m_w_oa,
               w_ob=m_w_ob, w_out=m_w_out, ple_norm_g=m_ple_norm_g, w_pg=m_w_pg, w_ple=m_w_ple, final_g=m_final_g)
    vel = dict(norm_g=v_norm_g, w_in=v_w_in, ln_a_g=v_ln_a_g, ln_a_b=v_ln_a_b, w_s=v_w_s, b_s=v_b_s, conv_w=v_conv_w,
               conv_b=v_conv_b, dt_bias=v_dt_bias, a_log=v_a_log, d_skip=v_d_skip, ssm_norm_g=v_ssm_norm_g, w_oa=v_w_oa,
               w_ob=v_w_ob, w_out=v_w_out, ple_norm_g=v_ple_norm_g, w_pg=v_w_pg, w_ple=v_w_ple, final_g=v_final_g)
    xi, yi, ci = _place()
    me = 2 * xi + yi
    c_arr = jnp.reshape(ci, (1,)).astype(jnp.int32)
    place_arr = jnp.stack([me, ci]).astype(jnp.int32)

    shard = {n: wt[n][0] for n in _BIG}
    wire = {n: shard[n].astype(WIRE_DTYPE) for n in _BIG}
    w_in_blocks, conv_blocks = _gather_weights([wire["w_in"]], conv_w[0])
    g_ssem, g_rsem, g_sent, g_lands, g_token = _late_gather_start([wire[n] for n in _BIG[1:]])
    full = {"conv_w": _blocks_to_full(True, conv_blocks)}
    for n in _SMALL:
        if n != "conv_w":
            full[n] = wt[n][0] if wt[n].ndim > 2 else wt[n].reshape(1, wt[n].shape[-1])

    def late_weights(after):
        blocks = _late_gather_wait(g_ssem, g_rsem, g_sent, g_lands, after)
        return {n: _blocks_to_full(n in _COL_SHARDED, b) for n, b in zip(_BIG[1:], blocks)}

    w = _layout_weights(full, w_in_blocks=w_in_blocks)
    loss_row, g, ctx = _forward_backward(x[0], p[0, 0], loss_target[0], w, late_weights, after=(g_token,))
    loss = lax.psum(loss_row[0, 0], ("x", "y", "c"))

    parts = {n: _full_to_blocks(n in _COL_SHARDED, g[n]) for n in _BIG[1:]}
    parts["w_main"] = g["w_main"][None]
    parts["w_dt"] = jnp.pad(_lanes_to_heads(g["w_dt"]), ((0, 0), (0, 128 - N_HEADS)))[None]
    names = ("w_main", "w_dt") + _BIG[1:]
    g5 = {n: parts[n].reshape(parts[n].shape[0], 2, parts[n].shape[1] // 2, parts[n].shape[2]) for n in names}
    to_sibling = [lax.dynamic_index_in_dim(g5[n], 1 - ci, axis=1, keepdims=False).astype(WIRE_DTYPE) for n in names]
    from_sibling = _swap_with_sibling(to_sibling)
    chip = {n: _chip_sum("chip_sum_" + n, g5[n], r, c_arr) for n, r in zip(names, from_sibling)}
    chip["w_in"] = _w_in_grad_blocks(chip["w_main"][0], chip["w_dt"][0])
    chip_wire = [chip[n] for n in _BIG]
    s_sem, r_sem, sent, lands, token = _scatter_blocks_start(chip_wire)
    grad_x, g["norm_g"] = _input_grad(x[0], w, ctx, after=(token,))
    g = _natural_grads(g)

    pieces = [_two_d(n, g[n]).reshape(-1) for n in _SMALL]
    sizes = [v.shape[0] for v in pieces]
    padded = [-(-s // _PACK) * _PACK for s in sizes]
    packed = jnp.concatenate([jnp.pad(v, (0, ps - s)) for v, s, ps in zip(pieces, sizes, padded)]).reshape(-1, 128)
    a_ssem, a_rsem, a_src, a_land, a_token = _small_gather_start(packed)

    sent, from_chips = _scatter_blocks_wait(s_sem, r_sem, sent, lands, (grad_x, a_token))
    halves = [_final_sum("final_sum_" + n, a, r, place_arr) for n, a, r in zip(_BIG, sent, from_chips)]
    grads = dict(zip(_BIG, _share_halves(halves)))
    delta, new_m, new_v = {}, {}, {}
    for n in _BIG:
        t = jnp.transpose if n == "w_in" else (lambda a: a)
        res = _adamw_call("adamw_" + n, t(shard[n]), t(grads[n]), t(mom[n][0]), t(vel[n][0]))
        delta[n], new_m[n], new_v[n], grads[n] = (t(r) for r in res)

    a_src, a_land = _small_gather_wait(a_ssem, a_rsem, a_src, a_land, delta["w_in"])
    summed = _small_sum(a_src, a_land, jnp.reshape(4 * xi + 2 * yi + ci, (1,)).astype(jnp.int32)).reshape(-1)
    off = 0
    for n, s, ps in zip(_SMALL, sizes, padded):
        grads[n] = summed[off:off + s].reshape(_two_d(n, g[n]).shape)
        off += ps
    grads["conv_w"] = lax.dynamic_slice_in_dim(grads["conv_w"], me * (CONV_DIM // N_CHIPS), CONV_DIM // N_CHIPS, axis=1)
    small = _adamw_small([_two_d(n, wt[n]) for n in _SMALL], [grads[n] for n in _SMALL],
                         [_two_d(n, mom[n]) for n in _SMALL], [_two_d(n, vel[n]) for n in _SMALL])
    for i, n in enumerate(_SMALL):
        delta[n], new_m[n], new_v[n] = small[0][i], small[1][i], small[2][i]

    def shaped(d):
        return [d[n].reshape(wt[n].shape) for n in _WEIGHTS]

    return (loss, grad_x[None], *shaped(grads), *shaped(delta), *shaped(new_m), *shaped(new_v))
```

```python
import functools

import jax
import jax.numpy as jnp
from jax import lax
from jax.experimental import pallas as pl
from jax.experimental.pallas import tpu as pltpu

F32 = jnp.float32
MXU_DTYPE = jnp.bfloat16
ACT_DTYPE = jnp.bfloat16
WIRE_DTYPE = jnp.bfloat16

D_MODEL = 1024
PLE_DIM = 256
CHUNK = 128
EPS = 1e-6
E_A = D_MODEL
G_A = 4
D_INNER = 2 * D_MODEL
HEAD_DIM = 64
N_HEADS = D_INNER // HEAD_DIM
N_STATE = 128
N_GROUPS = 4
HEADS_PER_GROUP = N_HEADS // N_GROUPS
PAIRS_PER_GROUP = HEADS_PER_GROUP // 2
CONV_K = 4
CONV_DIM = D_INNER + 2 * N_GROUPS * N_STATE
N_IN = 3 * E_A + D_INNER + CONV_DIM + N_HEADS + 2 * D_MODEL
N_CHIPS = 4
N_DEV = 8
W_IN_BLOCK = N_IN // N_CHIPS

UVZ_W, XBC_W, ZB_W, G_W = 3 * E_A, CONV_DIM, D_INNER, 2 * D_MODEL
MAIN_W = UVZ_W + XBC_W + ZB_W + G_W
UVZ_CB, XBC_CB, ZB_CB, G_CB = 0, 1, 3, 4
DT_W = N_GROUPS * 128

ADAM_LR, ADAM_B1, ADAM_B2, ADAM_EPS, ADAM_WD, ADAM_STEP = 0.001, 0.9, 0.999, 1e-08, 0.01, 10

MESH = pl.DeviceIdType.MESH
ANY = pl.BlockSpec(memory_space=pl.ANY)


def _mxu(v):
    return v.astype(MXU_DTYPE)


def _dot(a, b, dims=(((1,), (0,)), ((), ()))):
    return lax.dot_general(_mxu(a), _mxu(b), dims, preferred_element_type=F32)


V7X_MXU_WIDTH = 256
V7X_SCOPED_VMEM_BYTES = 60000 * 1024
MM_TILE = 4 * V7X_MXU_WIDTH
MM_VMEM_BUDGET = 4 * V7X_SCOPED_VMEM_BYTES // 5


def _mm_tk(m, n, k, tm, tn, a_bytes, b_bytes, out_bytes, extra_bytes):
    one_tile = m == tm and n == tn
    for parts in range(2 if one_tile else 1, k // 128 + 1):
        if k % parts or (k // parts) % 128 and parts > 1:
            continue
        tk = k // parts
        a_bufs = 1 if (parts == 1 and m == tm) else 2
        b_bufs = 1 if (parts == 1 and n == tn) else 2
        need = (tk * (a_bufs * tm * a_bytes + b_bufs * tn * b_bytes) + 2 * tm * tn * (out_bytes + extra_bytes)
                + (tm * tn * 4 if parts > 1 else 0))
        if need <= MM_VMEM_BUDGET:
            return tk
    return 128


def _matmul(a, b, *, mode, name, out_dtype, m, n, k, tm=MM_TILE, tn=MM_TILE, tk=None, a_off=0, b_off=0,
            extras=(), epilogue=None, after=()):
    tm, tn = min(tm, m), min(tn, n)
    if tk is None:
        tk = _mm_tk(m, n, k, tm, tn, a.dtype.itemsize, b.dtype.itemsize, jnp.dtype(out_dtype).itemsize,
                    sum(e.dtype.itemsize for e in extras))
    tk = min(tk, k)
    assert m % tm == 0 and n % tn == 0 and k % tk == 0, (name, m, n, k, tm, tn, tk)
    nk = k // tk
    a_mode = pl.Buffered(1) if (nk == 1 and m == tm) else None
    b_mode = pl.Buffered(1) if (nk == 1 and n == tn) else None
    if mode == "nn":
        assert a_off % tk == 0 and b_off % tn == 0
        a_spec = pl.BlockSpec((tm, tk), lambda i, j, kk: (i, kk + a_off // tk), pipeline_mode=a_mode)
        b_spec = pl.BlockSpec((tk, tn), lambda i, j, kk: (kk, j + b_off // tn), pipeline_mode=b_mode)
        dims = (((1,), (0,)), ((), ()))
    elif mode == "nt":
        a_spec = pl.BlockSpec((tm, tk), lambda i, j, kk: (i, kk), pipeline_mode=a_mode)
        b_spec = pl.BlockSpec((tn, tk), lambda i, j, kk: (j, kk), pipeline_mode=b_mode)
        dims = (((1,), (1,)), ((), ()))
    else:
        assert a_off % tm == 0 and b_off % tn == 0
        a_spec = pl.BlockSpec((tk, tm), lambda i, j, kk: (kk, i + a_off // tm), pipeline_mode=a_mode)
        b_spec = pl.BlockSpec((tk, tn), lambda i, j, kk: (kk, j + b_off // tn), pipeline_mode=b_mode)
        dims = (((0,), (0,)), ((), ()))
    ne = len(extras)

    def finish(acc, extra_refs, o_ref):
        res = acc if epilogue is None else epilogue(acc, *[e[...] for e in extra_refs])
        o_ref[...] = res.astype(o_ref.dtype)

    def body(a_ref, b_ref, *rest):
        extra_refs, o_ref = rest[:ne], rest[ne + len(after)]
        part = _dot(a_ref[...], b_ref[...], dims)
        if nk == 1:
            finish(part, extra_refs, o_ref)
            return
        acc_ref = rest[ne + len(after) + 1]
        kk = pl.program_id(2)

        @pl.when(kk == 0)
        def _():
            acc_ref[...] = part

        @pl.when(kk > 0)
        def _():
            acc_ref[...] += part

        @pl.when(kk == nk - 1)
        def _():
            finish(acc_ref[...], extra_refs, o_ref)

    o_spec = pl.BlockSpec((tm, tn), lambda i, j, kk: (i, j))
    return pl.pallas_call(
        body, name=name, grid=(m // tm, n // tn, nk),
        in_specs=[a_spec, b_spec] + [o_spec] * ne + [ANY] * len(after), out_specs=o_spec,
        out_shape=jax.ShapeDtypeStruct((m, n), out_dtype),
        scratch_shapes=[pltpu.VMEM((tm, tn), F32)] if nk > 1 else [],
        compiler_params=pltpu.CompilerParams(dimension_semantics=("parallel", "parallel", "arbitrary")),
    )(a, b, *extras, *after)


def _rows_matmul(name, f, rows, pars, b, *, out_dtype, n, k, tm, nrows, tn=MM_TILE, extras=(), epilogue=None, after=(),
                 side=None):
    tm, tn = min(tm, nrows), min(tn, n)
    assert nrows % tm == 0 and n % tn == 0, (name, nrows, n, tm, tn)
    nr, npar, ne, nj = len(rows), len(pars), len(extras), n // tn
    ns = 0 if side is None else 1
    n_in = nr + npar + 1 + ne + ns + len(after)

    def body(*refs):
        row_refs, par_refs, b_ref = refs[:nr], refs[nr:nr + npar], refs[nr + npar]
        extra_refs = refs[nr + npar + 1:nr + npar + 1 + ne]
        a_ref, o_ref = refs[n_in], refs[n_in + 1]

        def make_a():
            a = f(*[r[...].astype(F32) for r in row_refs], *[p[...] for p in par_refs])[0]
            a_ref[...] = a.astype(a_ref.dtype)
            if ns:
                refs[n_in + 2][...] = _dot(a_ref[...], refs[nr + npar + 1 + ne][...]).astype(refs[n_in + 2].dtype)

        if nj == 1:
            make_a()
        else:
            pl.when(pl.program_id(1) == 0)(make_a)
        res = _dot(a_ref[...], b_ref[...])
        if epilogue is not None:
            res = epilogue(res, *[e[...] for e in extra_refs])
        o_ref[...] = res.astype(o_ref.dtype)

    o_spec = pl.BlockSpec((tm, tn), lambda i, j: (i, j))
    side_in = [] if side is None else [pl.BlockSpec(tuple(side[0].shape), lambda i, j: (0, 0))]
    side_out = [] if side is None else [pl.BlockSpec((tm, side[0].shape[1]), lambda i, j: (i, 0))]
    side_shape = [] if side is None else [jax.ShapeDtypeStruct((nrows, side[0].shape[1]), side[1])]
    return pl.pallas_call(
        body, name=name, grid=(nrows // tm, nj),
        in_specs=[pl.BlockSpec((tm, w), lambda i, j, cb=cb: (i, cb)) for _, w, cb in rows]
        + [pl.BlockSpec(tuple(p.shape), lambda i, j, nd=p.ndim: (0,) * nd) for p in pars]
        + [pl.BlockSpec((k, tn), lambda i, j: (0, j))] + [o_spec] * ne + side_in + [ANY] * len(after),
        out_specs=[pl.BlockSpec((tm, k), lambda i, j: (i, 0)), o_spec] + side_out,
        out_shape=[jax.ShapeDtypeStruct((nrows, k), ACT_DTYPE), jax.ShapeDtypeStruct((nrows, n), out_dtype)] + side_shape,
        compiler_params=pltpu.CompilerParams(dimension_semantics=("parallel", "arbitrary")),
    )(*[r[0] for r in rows], *pars, b, *extras, *([] if side is None else [side[0]]), *after)


def _row_spec(tm, width, cb):
    return pl.BlockSpec((tm, width), lambda i: (i, cb))


def _whole_spec(shape):
    nd = len(shape)
    return pl.BlockSpec(tuple(shape), lambda i: (0,) * nd)


def _rows_vjp_call(name, f, rows, pars, cots, drows, *, tm, nrows, cot_mm=None):
    tm = min(tm, nrows)
    nr, npar, nc = len(rows), len(pars), len(cots)
    mm_args, mm_specs = [], []
    if cot_mm is not None:
        mm_a, mm_b, mm_add = cot_mm
        mm_args = [mm_a, mm_b] + ([] if mm_add is None else [mm_add])
        mm_specs = [_row_spec(tm, mm_a.shape[1], 0), _whole_spec(mm_b.shape)]
        mm_specs += [] if mm_add is None else [_row_spec(tm, mm_b.shape[0], 0)]
    alias_bufs, aliases = [], {}
    out_shape, out_specs = [], []
    for (arr, w, cb), d in zip(rows, drows):
        if d is None:
            continue
        dt, into = d
        if into is None:
            out_shape.append(jax.ShapeDtypeStruct((nrows, w), dt))
            out_specs.append(_row_spec(tm, w, 0))
        else:
            buf, total, ocb = into
            if buf is not None:
                aliases[nr + npar + nc + len(alias_bufs)] = len(out_shape)
                alias_bufs.append(buf)
            out_shape.append(jax.ShapeDtypeStruct((nrows, total), dt))
            out_specs.append(_row_spec(tm, w, ocb))
    n_drow = len(out_shape)
    for p in pars:
        out_shape.append(jax.ShapeDtypeStruct(p.shape, F32))
        out_specs.append(_whole_spec(p.shape))
    na = len(alias_bufs)

    def body(*refs):
        rv = [r[...].astype(F32) for r in refs[:nr]]
        pv = [p[...] for p in refs[nr:nr + npar]]
        cv = tuple(c[...].astype(F32) for c in refs[nr + npar:nr + npar + nc])
        o_refs = refs[nr + npar + nc + na + len(mm_args):]
        if mm_args:
            mm_refs = refs[nr + npar + nc + na:nr + npar + nc + na + len(mm_args)]
            c0 = _dot(mm_refs[0][...], mm_refs[1][...], (((1,), (1,)), ((), ())))
            if len(mm_refs) == 3:
                c0 = c0 + mm_refs[2][...].astype(F32)
            cv = (c0,) + cv
        _, vjp = jax.vjp(f, *rv, *pv)
        g = vjp(cv)
        oi = 0
        for ri, d in enumerate(drows):
            if d is not None:
                o_refs[oi][...] = g[ri].astype(o_refs[oi].dtype)
                oi += 1
        first = pl.program_id(0) == 0
        for pi in range(npar):
            acc = o_refs[n_drow + pi]

            @pl.when(first)
            def _(acc=acc):
                acc[...] = jnp.zeros_like(acc)

            acc[...] += g[nr + pi]

    return pl.pallas_call(
        body, name=name, grid=(nrows // tm,),
        in_specs=[_row_spec(tm, w, cb) for _, w, cb in rows] + [_whole_spec(p.shape) for p in pars]
        + [_row_spec(tm, w, cb) for _, w, cb in cots] + [ANY] * na + mm_specs,
        out_specs=out_specs, out_shape=out_shape, input_output_aliases=aliases,
        compiler_params=pltpu.CompilerParams(dimension_semantics=("arbitrary",)),
    )(*[r[0] for r in rows], *pars, *[c[0] for c in cots], *alias_bufs, *mm_args)


def _rms(x, g):
    return x * lax.rsqrt(jnp.mean(x * x, axis=-1, keepdims=True) + EPS) * g


def _f_rms(x, g):
    return (_rms(x, g),)


def _tril_mask():
    return lax.broadcasted_iota(jnp.int32, (CHUNK, CHUNK), 0) >= lax.broadcasted_iota(jnp.int32, (CHUNK, CHUNK), 1)


def _f_branch_a(uvz, ln_g, ln_b, w_s, b_s):
    u = jax.nn.gelu(uvz[:, :E_A])
    v = jax.nn.gelu(uvz[:, E_A:2 * E_A])
    z = uvz[:, 2 * E_A:]
    xc = v - jnp.mean(v, axis=-1, keepdims=True)
    vn = xc * lax.rsqrt(jnp.mean(xc * xc, axis=-1, keepdims=True) + EPS) * ln_g + ln_b
    mask = _tril_mask()
    ws = [jnp.where(mask, w_s[g], 0.0) for g in range(G_A)]
    gw = E_A // G_A
    rows = []
    for c in range(uvz.shape[0] // CHUNK):
        vc = vn[c * CHUNK:(c + 1) * CHUNK]
        rows.append(jnp.concatenate([_dot(ws[g], vc[:, g * gw:(g + 1) * gw]) + b_s[g] for g in range(G_A)], axis=1))
    sv = rows[0] if len(rows) == 1 else jnp.concatenate(rows, axis=0)
    return (u * sv * jax.nn.silu(z),)


def _f_gnorm(y, zb, g):
    yz = y * jax.nn.silu(zb)
    gw = D_INNER // N_GROUPS
    parts = []
    for i in range(N_GROUPS):
        s = yz[:, i * gw:(i + 1) * gw]
        parts.append(s * lax.rsqrt(jnp.mean(s * s, axis=-1, keepdims=True) + EPS))
    return (jnp.concatenate(parts, axis=1) * g,)


def _f_merge(g2, oa, ob):
    return (jax.nn.sigmoid(g2[:, :D_MODEL]) * oa + jax.nn.sigmoid(g2[:, D_MODEL:]) * ob,)


def _f_loss(x1, gp, pe, tgt, fg):
    x2 = x1 + jax.nn.sigmoid(gp) * pe
    err = _rms(x2, fg) - tgt
    return 0.5 * jnp.sum(jnp.mean(err * err, axis=-1))


def _head(x1, p, tgt, ple_g, w_pg, w_ple, fg, *, tm, nrows):
    tm = min(tm, nrows)

    def body(x1_ref, p_ref, t_ref, pg_ref, wpg_ref, wple_ref, fg_ref, hp_ref, dx_ref, dgp_ref, dpe_ref, dfg_ref, loss_ref):
        x1 = x1_ref[...]
        hp_ref[...] = _rms(x1, pg_ref[...]).astype(hp_ref.dtype)
        gp = _dot(hp_ref[...], wpg_ref[...])
        pe = _dot(p_ref[...], wple_ref[...])
        loss, vjp = jax.vjp(_f_loss, x1, gp, pe, t_ref[...], fg_ref[...])
        dx, dgp, dpe, _, dfg = vjp(jnp.ones((), F32))
        dx_ref[...] = dx
        dgp_ref[...] = dgp.astype(dgp_ref.dtype)
        dpe_ref[...] = dpe.astype(dpe_ref.dtype)

        @pl.when(pl.program_id(0) == 0)
        def _():
            dfg_ref[...] = jnp.zeros_like(dfg_ref)
            loss_ref[...] = jnp.zeros_like(loss_ref)

        dfg_ref[...] += dfg
        loss_ref[...] += jnp.full(loss_ref.shape, loss, F32)

    row = _row_spec(tm, D_MODEL, 0)
    act = jax.ShapeDtypeStruct((nrows, D_MODEL), ACT_DTYPE)
    return pl.pallas_call(
        body, name="head", grid=(nrows // tm,),
        in_specs=[row, _row_spec(tm, PLE_DIM, 0), row, _whole_spec((1, D_MODEL)), _whole_spec(w_pg.shape),
                  _whole_spec(w_ple.shape), _whole_spec((1, D_MODEL))],
        out_specs=[row, row, row, row, _whole_spec((1, D_MODEL)), _whole_spec((1, 128))],
        out_shape=[act, jax.ShapeDtypeStruct((nrows, D_MODEL), F32), act, act, jax.ShapeDtypeStruct((1, D_MODEL), F32),
                   jax.ShapeDtypeStruct((1, 128), F32)],
        compiler_params=pltpu.CompilerParams(dimension_semantics=("arbitrary",)),
    )(x1, p, tgt, ple_g, w_pg, w_ple, fg)


def _shift_rows(cur, edge, j, up):
    tm = cur.shape[0]
    row = lax.broadcasted_iota(jnp.int32, cur.shape, 0)
    if up:
        sh = pltpu.roll(cur, tm - j, 0)
        e = jnp.tile(pltpu.roll(edge, 8 - j, 0), (tm // 8, 1))
        return jnp.where(row >= tm - j, e, sh)
    sh = pltpu.roll(cur, j, 0)
    e = jnp.tile(pltpu.roll(edge, j, 0), (tm // 8, 1))
    return jnp.where(row < j, e, sh)


def _conv_pre(cur, prev, w, b):
    acc = cur * w[CONV_K - 1:CONV_K] + b
    for j in range(1, CONV_K):
        acc = acc + _shift_rows(cur, prev, j, up=False) * w[CONV_K - 1 - j:CONV_K - j]
    return acc


def _halo_specs(tm, nrows, cb, before):
    nb = tm // 8
    last = nrows // 8 - 1
    if before:
        return pl.BlockSpec((8, XBC_W), lambda i: (jnp.maximum(i * nb - 1, 0), cb))
    return pl.BlockSpec((8, XBC_W), lambda i: (jnp.minimum((i + 1) * nb, last), cb))


def _conv_fwd(proj, conv_w, conv_b, *, tm, nrows):
    tm = min(tm, nrows)

    def body(cur_ref, prev_ref, w_ref, b_ref, o_ref, pre_ref):
        prev = jnp.where(pl.program_id(0) == 0, 0.0, prev_ref[...].astype(F32))
        pre = _conv_pre(cur_ref[...].astype(F32), prev, w_ref[...], b_ref[...])
        o_ref[...] = jax.nn.silu(pre).astype(o_ref.dtype)
        pre_ref[...] = pre.astype(pre_ref.dtype)

    out = jax.ShapeDtypeStruct((nrows, XBC_W), ACT_DTYPE)
    return pl.pallas_call(
        body, name="conv_fwd", grid=(nrows // tm,),
        in_specs=[_row_spec(tm, XBC_W, XBC_CB), _halo_specs(tm, nrows, XBC_CB, True),
                  _whole_spec((CONV_K, XBC_W)), _whole_spec((1, XBC_W))],
        out_specs=[_row_spec(tm, XBC_W, 0)] * 2, out_shape=[out, out],
        compiler_params=pltpu.CompilerParams(dimension_semantics=("parallel",)),
    )(proj, proj, conv_w, conv_b)


def _conv_bwd_act(pre, dact, *, tm, nrows):
    tm = min(tm, nrows)
    nb = N_GROUPS * N_STATE

    def body(pre_ref, dxs_ref, dbm_ref, dcm_ref, dpre_ref, db_ref):
        pre = pre_ref[...].astype(F32)
        sg = jax.nn.sigmoid(pre)
        dy = jnp.concatenate([dxs_ref[...], dbm_ref[...], dcm_ref[...]], axis=1).astype(F32)
        dpre = dy * sg * (1.0 + pre * (1.0 - sg))
        dpre_ref[...] = dpre.astype(dpre_ref.dtype)

        @pl.when(pl.program_id(0) == 0)
        def _():
            db_ref[...] = jnp.zeros_like(db_ref)

        db_ref[...] += jnp.sum(dpre, axis=0, keepdims=True)

    return pl.pallas_call(
        body, name="conv_bwd_act", grid=(nrows // tm,),
        in_specs=[_row_spec(tm, XBC_W, 0), _row_spec(tm, D_INNER, 0), _row_spec(tm, nb, 0), _row_spec(tm, nb, 0)],
        out_specs=[_row_spec(tm, XBC_W, 0), _whole_spec((1, XBC_W))],
        out_shape=[jax.ShapeDtypeStruct((nrows, XBC_W), ACT_DTYPE), jax.ShapeDtypeStruct((1, XBC_W), F32)],
        compiler_params=pltpu.CompilerParams(dimension_semantics=("arbitrary",)),
    )(pre, *dact)


def _conv_bwd_x(dpre, proj, conv_w, dproj, *, tm, nrows):
    tm = min(tm, nrows)
    ntiles = nrows // tm

    def body(cur_ref, nxt_ref, x_ref, w_ref, _, o_ref, dw_ref):
        cur = cur_ref[...].astype(F32)
        nxt = jnp.where(pl.program_id(0) == ntiles - 1, 0.0, nxt_ref[...].astype(F32))
        x = x_ref[...].astype(F32)
        w = w_ref[...]

        @pl.when(pl.program_id(0) == 0)
        def _():
            dw_ref[...] = jnp.zeros_like(dw_ref)

        acc = cur * w[CONV_K - 1:CONV_K]
        dw_ref[CONV_K - 1:CONV_K, :] += jnp.sum(cur * x, axis=0, keepdims=True)
        for j in range(1, CONV_K):
            u = _shift_rows(cur, nxt, j, up=True)
            acc = acc + u * w[CONV_K - 1 - j:CONV_K - j]
            dw_ref[CONV_K - 1 - j:CONV_K - j, :] += jnp.sum(u * x, axis=0, keepdims=True)
        o_ref[...] = acc.astype(o_ref.dtype)

    return pl.pallas_call(
        body, name="conv_bwd_x", grid=(ntiles,),
        in_specs=[_row_spec(tm, XBC_W, 0), _halo_specs(tm, nrows, 0, False), _row_spec(tm, XBC_W, XBC_CB),
                  _whole_spec((CONV_K, XBC_W)), ANY],
        out_specs=[_row_spec(tm, XBC_W, XBC_CB), _whole_spec((CONV_K, XBC_W))],
        out_shape=[jax.ShapeDtypeStruct(dproj.shape, dproj.dtype), jax.ShapeDtypeStruct((CONV_K, XBC_W), F32)],
        input_output_aliases={4: 0},
        compiler_params=pltpu.CompilerParams(dimension_semantics=("arbitrary",)),
    )(dpre, dpre, proj, conv_w, dproj)


SSD_SPAN = 4
SSD_FWD_SPANS = 4
_XS_GW = D_INNER // N_GROUPS
_NT = (((1,), (1,)), ((), ()))
_TN = (((0,), (0,)), ((), ()))


def _bf16_terms(x, terms):
    parts, rest = [], x
    for _ in range(terms):
        part = rest.astype(jnp.bfloat16)
        parts.append(part)
        rest = rest - part.astype(F32)
    return parts


def _head_lane_matrix():
    return (lax.broadcasted_iota(jnp.int32, (128, _XS_GW), 0)
            == lax.broadcasted_iota(jnp.int32, (128, _XS_GW), 1) // HEAD_DIM).astype(jnp.bfloat16)


@functools.partial(jax.custom_vjp, nondiff_argnums=(1,))
def _head_lanes(cols, terms):
    e = _head_lane_matrix()
    return sum(jnp.dot(t, e, preferred_element_type=F32) for t in _bf16_terms(cols, terms))


def _head_lanes_fwd(cols, terms):
    return _head_lanes(cols, terms), None


def _head_lanes_bwd(terms, _, g):
    e = _head_lane_matrix()
    return (sum(lax.dot_general(t, e, _NT, preferred_element_type=F32) for t in _bf16_terms(g, 2)),)


_head_lanes.defvjp(_head_lanes_fwd, _head_lanes_bwd)


def _ssd_chunk(k, xs, bm, cm, dtr, hprev, dtb, alog, dsk):
    causal, tri, lo = k
    dt = jax.nn.softplus(dtr + dtb)
    da = dt * (-jnp.exp(alog))
    cs = jnp.dot(tri, da, precision=lax.Precision.HIGHEST, preferred_element_type=F32)
    cst = cs.T
    cs_l = _head_lanes(cs, 3)
    xdt = xs * _head_lanes(dt, 2)
    cb = _dot(cm, bm, _NT)
    yd = []
    for q in range(PAIRS_PER_GROUP):
        xq = xdt[:, 128 * q:128 * (q + 1)]
        y2 = [_dot(cb * jnp.exp(jnp.where(causal, cs[:, h:h + 1] - cst[h:h + 1, :], -jnp.inf)), xq)
              for h in (2 * q, 2 * q + 1)]
        yd.append(jnp.where(lo, y2[0], y2[1]))
    y_off = jnp.exp(cs_l) * _dot(cm, hprev, _NT)
    st = _dot(xdt * jnp.exp(cs_l[CHUNK - 1:CHUNK, :] - cs_l), bm, _TN)
    cdec = jnp.exp(cs[CHUNK - 1:CHUNK, :])
    cd_rows = jnp.concatenate(
        [jnp.broadcast_to(cdec[:, h:h + 1], (HEAD_DIM, N_STATE)) for h in range(HEADS_PER_GROUP)], axis=0)
    dsk_l = _head_lanes(jnp.broadcast_to(dsk, (8, 128)), 2)[:1]
    y = jnp.concatenate(yd, axis=1) + y_off + xs * dsk_l
    return y, cd_rows * hprev + st


def _ssd_span(xs, bm, cm, dtr, h0, dtb, alog, dsk):
    li = lax.broadcasted_iota(jnp.int32, (CHUNK, CHUNK), 0)
    si = lax.broadcasted_iota(jnp.int32, (CHUNK, CHUNK), 1)
    causal = li >= si
    k = (causal, causal.astype(F32), si < HEAD_DIM)
    h, ys = h0, []
    for t in range(xs.shape[0] // CHUNK):
        r = slice(t * CHUNK, (t + 1) * CHUNK)
        y, h = _ssd_chunk(k, xs[r], bm[r], cm[r], dtr[r], h, dtb, alog, dsk)
        ys.append(y)
    return (ys[0] if len(ys) == 1 else jnp.concatenate(ys, axis=0)), h


def _ssd_specs(rev, nsteps, rows):
    def s_of(s):
        return nsteps - 1 - s if rev else s

    xs = pl.BlockSpec((rows, _XS_GW), lambda g, s: (s_of(s), g))
    bm = pl.BlockSpec((rows, N_STATE), lambda g, s: (s_of(s), D_INNER // N_STATE + g))
    cm = pl.BlockSpec((rows, N_STATE), lambda g, s: (s_of(s), D_INNER // N_STATE + N_GROUPS + g))
    dt = pl.BlockSpec((rows, 128), lambda g, s: (s_of(s), g))
    par = pl.BlockSpec((1, 128), lambda g, s: (0, g))
    st = pl.BlockSpec((None, None, _XS_GW, N_STATE), lambda g, s: (g, s_of(s), 0, 0))
    return xs, bm, cm, dt, par, st


def _ssd_fwd(act, dtr, dtb, alog, dsk, *, nrows):
    span = CHUNK * min(SSD_SPAN, nrows // CHUNK)
    per_step = SSD_FWD_SPANS if nrows % (SSD_FWD_SPANS * span) == 0 else 1
    rows = per_step * span
    nsteps = nrows // rows
    xs, bm, cm, dt, par, _ = _ssd_specs(False, nsteps, rows)
    st = pl.BlockSpec((None, per_step, _XS_GW, N_STATE), lambda g, s: (g, s, 0, 0))

    def body(xs_ref, b_ref, c_ref, dt_ref, dtb_ref, al_ref, dk_ref, y_ref, st_ref, h_ref):
        @pl.when(pl.program_id(1) == 0)
        def _():
            h_ref[...] = jnp.zeros_like(h_ref)

        h = h_ref[...]
        for i in range(per_step):
            r = slice(i * span, (i + 1) * span)
            st_ref[i] = h
            y, h = _ssd_span(xs_ref[r, :].astype(F32), b_ref[r, :].astype(F32), c_ref[r, :].astype(F32), dt_ref[r, :],
                             h, dtb_ref[...], al_ref[...], dk_ref[...])
            y_ref[r, :] = y.astype(y_ref.dtype)
        h_ref[...] = h

    return pl.pallas_call(
        body, name="ssd_fwd", grid=(N_GROUPS, nsteps),
        in_specs=[xs, bm, cm, dt, par, par, par], out_specs=[xs, st],
        out_shape=[jax.ShapeDtypeStruct((nrows, D_INNER), ACT_DTYPE),
                   jax.ShapeDtypeStruct((N_GROUPS, nrows // span, _XS_GW, N_STATE), F32)],
        scratch_shapes=[pltpu.VMEM((_XS_GW, N_STATE), F32)],
        compiler_params=pltpu.CompilerParams(dimension_semantics=("arbitrary", "arbitrary")),
    )(act, act, act, dtr, dtb, alog, dsk)


def _ssd_bwd(act, dtr, dtb, alog, dsk, states, dy, *, nrows):
    rows = CHUNK * min(SSD_SPAN, nrows // CHUNK)
    nsteps = nrows // rows
    xs, bm, cm, dt, par, st = _ssd_specs(True, nsteps, rows)

    def body(xs_ref, b_ref, c_ref, dt_ref, dtb_ref, al_ref, dk_ref, st_ref, dy_ref,
             dxs_ref, db_ref, dc_ref, ddt_ref, ddtb_ref, dal_ref, ddk_ref, dh_ref):
        @pl.when(pl.program_id(1) == 0)
        def _():
            dh_ref[...] = jnp.zeros_like(dh_ref)
            ddtb_ref[...] = jnp.zeros_like(ddtb_ref)
            dal_ref[...] = jnp.zeros_like(dal_ref)
            ddk_ref[...] = jnp.zeros_like(ddk_ref)

        _, vjp = jax.vjp(_ssd_span, xs_ref[...].astype(F32), b_ref[...].astype(F32), c_ref[...].astype(F32),
                         dt_ref[...], st_ref[...], dtb_ref[...], al_ref[...], dk_ref[...])
        dxs, db, dc, ddt, dh, ddtb, dal, ddk = vjp((dy_ref[...].astype(F32), dh_ref[...]))
        dxs_ref[...] = dxs.astype(dxs_ref.dtype)
        db_ref[...] = db.astype(db_ref.dtype)
        dc_ref[...] = dc.astype(dc_ref.dtype)
        ddt_ref[...] = ddt
        dh_ref[...] = dh
        ddtb_ref[...] += ddtb
        dal_ref[...] += dal
        ddk_ref[...] += ddk

    nb = N_GROUPS * N_STATE
    bspec = pl.BlockSpec((rows, N_STATE), lambda g, s: (nsteps - 1 - s, g))
    return pl.pallas_call(
        body, name="ssd_bwd", grid=(N_GROUPS, nsteps),
        in_specs=[xs, bm, cm, dt, par, par, par, st, xs],
        out_specs=[xs, bspec, bspec, dt, par, par, par],
        out_shape=[jax.ShapeDtypeStruct((nrows, D_INNER), ACT_DTYPE), jax.ShapeDtypeStruct((nrows, nb), ACT_DTYPE),
                   jax.ShapeDtypeStruct((nrows, nb), ACT_DTYPE), jax.ShapeDtypeStruct((nrows, DT_W), F32),
                   jax.ShapeDtypeStruct((1, DT_W), F32), jax.ShapeDtypeStruct((1, DT_W), F32),
                   jax.ShapeDtypeStruct((1, DT_W), F32)],
        scratch_shapes=[pltpu.VMEM((_XS_GW, N_STATE), F32)],
        compiler_params=pltpu.CompilerParams(dimension_semantics=("arbitrary", "arbitrary")),
    )(act, act, act, dtr, dtb, alog, dsk, states, dy)


def _add_epilogue(acc, r):
    return r + acc


def _rms_and_skip(x, g):
    return _rms(x, g), x


def _forward_backward(x, p, tgt, w, late_weights=None, after=()):
    s = x.shape[0]
    act_t, f32 = ACT_DTYPE, F32
    mm = functools.partial(_matmul)
    h, proj, dtr = _rows_matmul("proj", _f_rms, [(x, D_MODEL, 0)], [w["norm_g"]], w["w_main"], out_dtype=act_t,
                                n=MAIN_W, k=D_MODEL, tm=1024, tn=2 * MM_TILE, nrows=s, after=after, side=(w["w_dt"], f32))
    act, conv_pre = _conv_fwd(proj, w["conv_w"], w["conv_b"], tm=512, nrows=s)
    y, states = _ssd_fwd(act, dtr, w["dt_bias"], w["a_log"], w["d_skip"], nrows=s)
    if late_weights is not None:
        w = {**w, **late_weights(states)}
    a_pars = [w["ln_a_g"], w["ln_a_b"], w["w_s"], w["b_s"]]
    y_a, o_a = _rows_matmul("out_a", _f_branch_a, [(proj, UVZ_W, UVZ_CB)], a_pars, w["w_oa"], out_dtype=act_t,
                            n=D_MODEL, k=E_A, tm=512, nrows=s)
    gn_rows = [(y, D_INNER, 0), (proj, ZB_W, ZB_CB)]
    y_b, o_b = _rows_matmul("out_b", _f_gnorm, gn_rows, [w["ssm_norm_g"]], w["w_ob"], out_dtype=act_t,
                            n=D_MODEL, k=D_INNER, tm=1024, nrows=s)
    mg_rows = [(proj, G_W, G_CB), (o_a, D_MODEL, 0), (o_b, D_MODEL, 0)]
    merged, x1 = _rows_matmul("out_proj", _f_merge, mg_rows, [], w["w_out"], out_dtype=f32, n=D_MODEL, k=D_MODEL,
                              tm=1024, nrows=s, extras=(x,), epilogue=_add_epilogue)
    g = {}
    hp, dx2, dgp, dpe, g["final_g"], loss = _head(x1, p, tgt, w["ple_norm_g"], w["w_pg"], w["w_ple"], w["final_g"],
                                                   tm=512, nrows=s)
    g["w_pg"] = mm(hp, dgp, mode="tn", name="d_w_pg", out_dtype=f32, m=D_MODEL, n=D_MODEL, k=s, tn=MM_TILE // 2)
    g["w_ple"] = mm(p, dpe, mode="tn", name="d_w_ple", out_dtype=f32, m=PLE_DIM, n=D_MODEL, k=s)
    dx1, g["ple_norm_g"] = _rows_vjp_call(
        "ple_norm_bwd", _rms_and_skip, [(x1, D_MODEL, 0)], [w["ple_norm_g"]], [(dx2, D_MODEL, 0)],
        [(f32, None)], tm=1024, nrows=s, cot_mm=(dgp, w["w_pg"], None))
    g["w_out"] = mm(merged, dx1, mode="tn", name="d_w_out", out_dtype=f32, m=D_MODEL, n=D_MODEL, k=s)
    dproj, do_a, do_b = _rows_vjp_call(
        "merge_bwd", _f_merge, mg_rows, [], [],
        [(act_t, (None, MAIN_W, G_CB)), (act_t, None), (act_t, None)], tm=1024, nrows=s, cot_mm=(dx1, w["w_out"], None))
    g["w_oa"] = mm(y_a, do_a, mode="tn", name="d_w_oa", out_dtype=f32, m=E_A, n=D_MODEL, k=s, tn=MM_TILE // 2)
    g["w_ob"] = mm(y_b, do_b, mode="tn", name="d_w_ob", out_dtype=f32, m=D_INNER, n=D_MODEL, k=s)
    dy, dproj, g["ssm_norm_g"] = _rows_vjp_call(
        "gnorm_bwd", _f_gnorm, gn_rows, [w["ssm_norm_g"]], [],
        [(act_t, None), (act_t, (dproj, MAIN_W, ZB_CB))], tm=512, nrows=s, cot_mm=(do_b, w["w_ob"], None))
    dxs, dbm, dcm, ddtr, g["dt_bias"], g["a_log"], g["d_skip"] = _ssd_bwd(
        act, dtr, w["dt_bias"], w["a_log"], w["d_skip"], states, dy, nrows=s)
    dpre, g["conv_b"] = _conv_bwd_act(conv_pre, (dxs, dbm, dcm), tm=1024, nrows=s)
    dproj, g["conv_w"] = _conv_bwd_x(dpre, proj, w["conv_w"], dproj, tm=512, nrows=s)
    dproj, g["ln_a_g"], g["ln_a_b"], g["w_s"], g["b_s"] = _rows_vjp_call(
        "branch_a_bwd", _f_branch_a, [(proj, UVZ_W, UVZ_CB)], a_pars, [],
        [(act_t, (dproj, MAIN_W, UVZ_CB))], tm=512, nrows=s, cot_mm=(do_a, w["w_oa"], None))
    g["w_main"] = mm(h, dproj, mode="tn", name="d_w_main", out_dtype=f32, m=D_MODEL, n=MAIN_W, k=s, tn=MM_TILE // 2)
    g["w_dt"] = mm(h, ddtr, mode="tn", name="d_w_dt", out_dtype=f32, m=D_MODEL, n=DT_W, k=s)
    return loss, g, (dproj, ddtr, dx1)


def _input_grad(x, w, ctx, after=()):
    dproj, ddtr, dx1 = ctx
    s = x.shape[0]
    dh = _matmul(dproj, w["w_main"], mode="nt", name="d_h_main", out_dtype=F32, m=s, n=D_MODEL, k=MAIN_W,
                 tm=MM_TILE // 2, after=after)
    return _rows_vjp_call(
        "pre_norm_bwd", _rms_and_skip, [(x, D_MODEL, 0)], [w["norm_g"]], [(dx1, D_MODEL, 0)],
        [(F32, None)], tm=1024, nrows=s, cot_mm=(ddtr, w["w_dt"], dh))


def _local_step(x, p, tgt, w):
    loss, g, ctx = _forward_backward(x, p, tgt, w)
    grad_x, g["norm_g"] = _input_grad(x, w, ctx)
    return loss, grad_x, g


_O_ZB = 3 * E_A
_O_XBC = _O_ZB + D_INNER
_O_DT = _O_XBC + CONV_DIM
_O_G = _O_DT + N_HEADS


def _heads_to_lanes(v):
    r = v.shape[0]
    v = v.reshape(r, N_GROUPS, HEADS_PER_GROUP)
    return jnp.pad(v, ((0, 0), (0, 0), (0, 128 - HEADS_PER_GROUP))).reshape(r, DT_W)


def _lanes_to_heads(v):
    r = v.shape[0]
    return v.reshape(r, N_GROUPS, 128)[:, :, :HEADS_PER_GROUP].reshape(r, N_HEADS)


def _block_cols(blocks, a, b):
    parts = []
    for k in range(N_CHIPS):
        lo, hi = max(a, k * W_IN_BLOCK), min(b, (k + 1) * W_IN_BLOCK)
        if lo < hi:
            parts.append(blocks[k][:, lo - k * W_IN_BLOCK:hi - k * W_IN_BLOCK])
    return parts


_W_IN_SEGMENTS = ((0, _O_ZB, "m", 0), (_O_ZB, _O_XBC, "m", UVZ_W + XBC_W), (_O_XBC, _O_DT, "m", UVZ_W),
                  (_O_DT, _O_G, "d", 0), (_O_G, N_IN, "m", MAIN_W - G_W))


def _w_in_grad_blocks(gm, gdt):
    blocks = []
    for k in range(N_CHIPS):
        a, b = k * W_IN_BLOCK, (k + 1) * W_IN_BLOCK
        parts = []
        for s, e, src, off in _W_IN_SEGMENTS:
            lo, hi = max(a, s), min(b, e)
            if lo < hi:
                parts.append((gm if src == "m" else gdt)[:, off + lo - s:off + hi - s])
        blocks.append(jnp.concatenate(parts, axis=1))
    return jnp.stack(blocks)


def _layout_weights(f, w_in_blocks=None):
    w = dict(f)
    if w_in_blocks is None:
        w_in = w.pop("w_in")
        w_in_blocks = jnp.stack([w_in[:, k * W_IN_BLOCK:(k + 1) * W_IN_BLOCK] for k in range(N_CHIPS)])
    cols = functools.partial(_block_cols, w_in_blocks)
    w["w_main"] = jnp.concatenate(cols(0, _O_ZB) + cols(_O_XBC, _O_DT) + cols(_O_ZB, _O_XBC) + cols(_O_G, N_IN), axis=1)
    w["w_dt"] = _heads_to_lanes(jnp.concatenate(cols(_O_DT, _O_G), axis=1))
    w["b_s"] = f["b_s"].reshape(G_A, CHUNK, 1)
    for n in ("dt_bias", "a_log", "d_skip"):
        w[n] = _heads_to_lanes(f[n])
    return w


def _natural_grads(g):
    out = dict(g)
    gm = out.pop("w_main")
    gdt = _lanes_to_heads(out.pop("w_dt"))
    out["w_in"] = jnp.concatenate(
        [gm[:, :UVZ_W], gm[:, UVZ_W + XBC_W:UVZ_W + XBC_W + ZB_W], gm[:, UVZ_W:UVZ_W + XBC_W], gdt, gm[:, MAIN_W - G_W:]],
        axis=1)
    out["b_s"] = g["b_s"].reshape(G_A, CHUNK)
    for n in ("dt_bias", "a_log", "d_skip"):
        out[n] = _lanes_to_heads(g[n])
    return out


def _place():
    return lax.axis_index("x"), lax.axis_index("y"), lax.axis_index("c")


def _other_chips(x, y):
    return [(1 - x, y), (x, 1 - y), (1 - x, 1 - y)]


def _rcopy(src, dst, ssem, rsem, dev):
    return pltpu.make_async_remote_copy(src_ref=src, dst_ref=dst, send_sem=ssem, recv_sem=rsem,
                                        device_id=dev, device_id_type=MESH)


def _half(ref_rows, half):
    hs = ref_rows // 2
    return pl.ds(pl.multiple_of(half * hs, 16), hs)


GATHER_CHUNKS = 8


def _gather_weights(shards, conv_shard):
    nw, nq = len(shards), GATHER_CHUNKS

    def body(*refs):
        sh, cv = refs[:nw], refs[nw]
        out, cvo = refs[nw + 1:2 * nw + 1], refs[2 * nw + 1]
        ici_s, ici_r, fw_s, fw_r, own_s, own_r, cv_s, cv_r = refs[2 * nw + 2:]
        x, y, c = _place()
        me, sib, chips = 2 * x + y, (x, y, 1 - c), _other_chips(x, y)
        own = [_rcopy(sh[w], out[w].at[me], own_s.at[w], own_r.at[w], sib) for w in range(nw)]
        own.append(_rcopy(cv, cvo.at[me], own_s.at[nw], own_r.at[nw], sib))
        for cp in own:
            cp.start()
        pieces = [(w, q, j) for w in range(nw) for q in range(nq) for j in range(3)]

        def rows(w, half, q):
            hs = sh[w].shape[0] // 2
            return pl.ds(pl.multiple_of(half * hs + q * (hs // nq), 16), hs // nq)

        def sem(w, q, j):
            return (3 * w + j) * nq + q

        def landed(w, q, j, half):
            return out[w].at[2 * chips[j][0] + chips[j][1], rows(w, half, q)]

        sends = [_rcopy(sh[w].at[rows(w, c, q)], out[w].at[me, rows(w, c, q)], ici_s.at[sem(w, q, j)],
                        ici_r.at[sem(w, q, j)], (*chips[j], c)) for w, q, j in pieces if j < 2]
        for j, chip in enumerate(chips):
            sends.append(_rcopy(cv, cvo.at[me], cv_s.at[j], cv_r.at[j], (*chip, c)))
        for cp in sends:
            cp.start()

        def arrived(w, q, j):
            slab = landed(w, q, j, c)
            _rcopy(slab, slab, ici_s.at[sem(w, q, j)], ici_r.at[sem(w, q, j)], (*chips[j], c)).wait_recv()
            fwd = _rcopy(slab, slab, fw_s.at[sem(w, q, j)], fw_r.at[sem(w, q, j)], sib)
            fwd.start()
            sends.append(fwd)
            if j < 2 and (q % 2 == 0) == (j == 0):
                hop = _rcopy(slab, slab, ici_s.at[sem(w, q, 2)], ici_r.at[sem(w, q, 2)], (*chips[1 - j], c))
                hop.start()
                sends.append(hop)

        for w, q, j in pieces:
            if j < 2:
                arrived(w, q, j)
        for w, q, j in pieces:
            if j == 2:
                arrived(w, q, j)
        for j, chip in enumerate(chips):
            blk = cvo.at[2 * chip[0] + chip[1]]
            _rcopy(blk, blk, cv_s.at[j], cv_r.at[j], (*chip, c)).wait_recv()
        for w, q, j in pieces:
            slab = landed(w, q, j, 1 - c)
            _rcopy(slab, slab, fw_s.at[sem(w, q, j)], fw_r.at[sem(w, q, j)], sib).wait_recv()
        for cp in sends:
            cp.wait_send()
        for cp in own:
            cp.wait()

    dma = pltpu.SemaphoreType.DMA
    n_ici = 3 * nw * nq
    return pl.pallas_call(
        body, name="gather_weights",
        in_specs=[ANY] * (nw + 1), out_specs=[ANY] * (nw + 1),
        out_shape=[jax.ShapeDtypeStruct((N_CHIPS,) + s.shape, s.dtype) for s in shards]
        + [jax.ShapeDtypeStruct((N_CHIPS,) + conv_shard.shape, conv_shard.dtype)],
        scratch_shapes=[dma((n_ici,)), dma((n_ici,)), dma((n_ici,)), dma((n_ici,)), dma((nw + 1,)), dma((nw + 1,)),
                        dma((3,)), dma((3,))],
    )(*shards, conv_shard)


_HBM = pl.BlockSpec(memory_space=pltpu.HBM)
_SEM = pl.BlockSpec(memory_space=pltpu.SEMAPHORE)
_EFFECT = pltpu.SideEffectType.DATAFLOW_SIDE_EFFECTING


def _late_gather_copies(sh, out, s_sem, r_sem):
    x, y, c = _place()
    to = [(*chip, c) for chip in _other_chips(x, y)] + [(x, y, 1 - c)]
    return [_rcopy(sh[w], out[w].at[2 * x + y], s_sem.at[4 * w + j], r_sem.at[4 * w + j], dev)
            for w in range(len(sh)) for j, dev in enumerate(to)]


def _late_gather_start(shards):
    n = len(shards)
    lands = [lax.empty((N_CHIPS,) + a.shape, a.dtype) for a in shards]

    def body(*refs):
        for cp in _late_gather_copies(refs[:n], refs[n:2 * n], refs[2 * n], refs[2 * n + 1]):
            cp.start()
        refs[-1][...] = jnp.zeros_like(refs[-1])

    dma = pltpu.SemaphoreType.DMA
    hbm = [pltpu.with_memory_space_constraint(a, pltpu.HBM) for a in list(shards) + lands]
    out = pl.pallas_call(
        body, name="late_gather_start",
        out_shape=[dma((4 * n,)), dma((4 * n,))] + [pltpu.HBM(a.shape, a.dtype) for a in hbm]
        + [jax.ShapeDtypeStruct((8, 128), F32)],
        in_specs=[_HBM] * (2 * n), out_specs=[_SEM, _SEM] + [_HBM] * (2 * n) + [pl.BlockSpec(memory_space=pltpu.VMEM)],
        input_output_aliases={i: 2 + i for i in range(2 * n)},
        compiler_params=pltpu.CompilerParams(has_side_effects=_EFFECT),
    )(*hbm)
    return out[0], out[1], out[2:2 + n], out[2 + n:2 + 2 * n], out[-1]


def _late_gather_wait(s_sem, r_sem, srcs, lands, after):
    n = len(srcs)

    def body(*refs):
        for cp in _late_gather_copies(refs[:n], refs[n:2 * n], refs[2 * n], refs[2 * n + 1]):
            cp.wait_send()
            cp.wait_recv()

    out = pl.pallas_call(
        body, name="late_gather_wait",
        out_shape=[pltpu.HBM(a.shape, a.dtype) for a in list(srcs) + list(lands)],
        in_specs=[_HBM] * (2 * n) + [_SEM, _SEM, ANY], out_specs=[_HBM] * (2 * n),
        input_output_aliases={i: i for i in range(2 * n)},
        compiler_params=pltpu.CompilerParams(has_side_effects=_EFFECT),
    )(*srcs, *lands, s_sem, r_sem, after)
    return out[n:]


def _swap_with_sibling(arrs):
    n = len(arrs)

    def body(*refs):
        src, dst, s_sem, r_sem = refs[:n], refs[n:2 * n], refs[2 * n], refs[2 * n + 1]
        x, y, c = _place()
        cps = [_rcopy(src[i], dst[i], s_sem.at[i], r_sem.at[i], (x, y, 1 - c)) for i in range(n)]
        for cp in cps:
            cp.start()
        for cp in cps:
            cp.wait()

    dma = pltpu.SemaphoreType.DMA
    return pl.pallas_call(
        body, name="swap_with_sibling", in_specs=[ANY] * n, out_specs=[ANY] * n,
        out_shape=[jax.ShapeDtypeStruct(a.shape, a.dtype) for a in arrs], scratch_shapes=[dma((n,)), dma((n,))],
    )(*arrs)


def _scatter_copies(src, land, s_sem, r_sem):
    x, y, c = _place()
    return [_rcopy(src[i].at[2 * chip[0] + chip[1]], land[i].at[j], s_sem.at[3 * i + j], r_sem.at[3 * i + j], (*chip, c))
            for i in range(len(src)) for j, chip in enumerate(_other_chips(x, y))]


def _scatter_blocks_start(arrs):
    n = len(arrs)
    lands = [lax.empty((3,) + a.shape[1:], a.dtype) for a in arrs]

    def body(*refs):
        src, land, s_sem, r_sem, token = refs[:n], refs[n:2 * n], refs[2 * n], refs[2 * n + 1], refs[-1]
        for cp in _scatter_copies(src, land, s_sem, r_sem):
            cp.start()
        token[...] = jnp.zeros_like(token)

    dma = pltpu.SemaphoreType.DMA
    hbm = [pltpu.with_memory_space_constraint(a, pltpu.HBM) for a in list(arrs) + lands]
    out = pl.pallas_call(
        body, name="scatter_blocks_start",
        out_shape=[dma((3 * n,)), dma((3 * n,))] + [pltpu.HBM(a.shape, a.dtype) for a in hbm]
        + [jax.ShapeDtypeStruct((8, 128), F32)],
        in_specs=[_HBM] * (2 * n), out_specs=[_SEM, _SEM] + [_HBM] * (2 * n) + [pl.BlockSpec(memory_space=pltpu.VMEM)],
        input_output_aliases={i: 2 + i for i in range(2 * n)},
        compiler_params=pltpu.CompilerParams(has_side_effects=_EFFECT),
    )(*hbm)
    return out[0], out[1], out[2:2 + n], out[2 + n:2 + 2 * n], out[-1]


def _scatter_blocks_wait(s_sem, r_sem, srcs, lands, after):
    n = len(srcs)

    def body(*refs):
        src, land, s_sem, r_sem = refs[:n], refs[n:2 * n], refs[2 * n], refs[2 * n + 1]
        for cp in _scatter_copies(src, land, s_sem, r_sem):
            cp.wait_send()
            cp.wait_recv()

    out = pl.pallas_call(
        body, name="scatter_blocks_wait",
        out_shape=[pltpu.HBM(a.shape, a.dtype) for a in list(srcs) + list(lands)],
        in_specs=[_HBM] * (2 * n) + [_SEM, _SEM] + [ANY] * len(after), out_specs=[_HBM] * (2 * n),
        input_output_aliases={i: i for i in range(2 * n)},
        compiler_params=pltpu.CompilerParams(has_side_effects=_EFFECT),
    )(*srcs, *lands, s_sem, r_sem, *after)
    return out[:n], out[n:]


def _share_halves(arrs):
    n = len(arrs)

    def body(*refs):
        buf, s_sem, r_sem = refs[n:2 * n], refs[2 * n], refs[2 * n + 1]
        x, y, c = _place()
        cps = []
        for i in range(n):
            mine = buf[i].at[_half(buf[i].shape[0], c)]
            cps.append(_rcopy(mine, mine, s_sem.at[i], r_sem.at[i], (x, y, 1 - c)))
        for cp in cps:
            cp.start()
        for i in range(n):
            theirs = buf[i].at[_half(buf[i].shape[0], 1 - c)]
            _rcopy(theirs, theirs, s_sem.at[i], r_sem.at[i], (x, y, 1 - c)).wait_recv()
        for cp in cps:
            cp.wait_send()

    dma = pltpu.SemaphoreType.DMA
    return pl.pallas_call(
        body, name="share_halves", in_specs=[ANY] * n, out_specs=[ANY] * n,
        out_shape=[jax.ShapeDtypeStruct(a.shape, a.dtype) for a in arrs],
        input_output_aliases={i: i for i in range(n)}, scratch_shapes=[dma((n,)), dma((n,))],
    )(*arrs)


def _small_gather_copies(src, land, s_sem, r_sem):
    x, y, c = _place()
    cps = []
    for d in range(1, N_DEV):
        peer = ((1 - x) if d & 4 else x), ((1 - y) if d & 2 else y), ((1 - c) if d & 1 else c)
        cps.append(_rcopy(src, land.at[4 * x + 2 * y + c], s_sem.at[d - 1], r_sem.at[d - 1], peer))
    return cps


def _small_gather_start(packed):
    def body(src, land, s_sem, r_sem, _, __, token):
        for cp in _small_gather_copies(src, land, s_sem, r_sem):
            cp.start()
        token[...] = jnp.zeros_like(token)

    dma = pltpu.SemaphoreType.DMA
    hbm = [pltpu.with_memory_space_constraint(a, pltpu.HBM) for a in (packed, lax.empty((N_DEV,) + packed.shape, F32))]
    return pl.pallas_call(
        body, name="small_gather_start",
        out_shape=[dma((N_DEV - 1,)), dma((N_DEV - 1,))] + [pltpu.HBM(a.shape, a.dtype) for a in hbm]
        + [jax.ShapeDtypeStruct((8, 128), F32)],
        in_specs=[_HBM] * 2, out_specs=[_SEM, _SEM, _HBM, _HBM, pl.BlockSpec(memory_space=pltpu.VMEM)],
        input_output_aliases={0: 2, 1: 3}, compiler_params=pltpu.CompilerParams(has_side_effects=_EFFECT),
    )(*hbm)


def _small_gather_wait(s_sem, r_sem, src, land, after):
    def body(src, land, s_sem, r_sem, *_):
        for cp in _small_gather_copies(src, land, s_sem, r_sem):
            cp.wait_send()
            cp.wait_recv()

    return pl.pallas_call(
        body, name="small_gather_wait", out_shape=[pltpu.HBM(src.shape, src.dtype), pltpu.HBM(land.shape, land.dtype)],
        in_specs=[_HBM, _HBM, _SEM, _SEM, ANY], out_specs=[_HBM, _HBM], input_output_aliases={0: 0, 1: 1},
        compiler_params=pltpu.CompilerParams(has_side_effects=_EFFECT),
    )(src, land, s_sem, r_sem, after)


def _small_sum(own, land, dev_arr):
    def body(me_ref, own_ref, land_ref, o_ref):
        acc = jnp.zeros(o_ref.shape, F32)
        for d in range(N_DEV):
            acc = acc + jnp.where(me_ref[0] == d, own_ref[...], land_ref[d])
        o_ref[...] = acc

    return pl.pallas_call(
        body, name="small_sum", out_shape=jax.ShapeDtypeStruct(own.shape, F32),
        grid_spec=pltpu.PrefetchScalarGridSpec(
            num_scalar_prefetch=1, grid=(1,),
            in_specs=[pl.BlockSpec(own.shape, lambda i, m: (0, 0)), pl.BlockSpec(land.shape, lambda i, m: (0, 0, 0))],
            out_specs=pl.BlockSpec(own.shape, lambda i, m: (0, 0))),
    )(dev_arr, own, land)


def _row_tile(rows, cols):
    tr = max(8, min(rows, (1 << 20) // (4 * cols) // 8 * 8))
    while rows % tr:
        tr -= 8
    return tr


def _chip_sum(name, g5, recv, c_arr):
    nb, _, hs, cols = g5.shape
    tr = _row_tile(hs, cols)

    def body(_, a_ref, b_ref, o_ref):
        o_ref[...] = (a_ref[...] + b_ref[...].astype(F32)).astype(o_ref.dtype)

    blk = pl.BlockSpec((None, tr, cols), lambda b, i, c: (b, i, 0))
    return pl.pallas_call(
        body, name=name,
        grid_spec=pltpu.PrefetchScalarGridSpec(
            num_scalar_prefetch=1, grid=(nb, hs // tr),
            in_specs=[pl.BlockSpec((None, None, tr, cols), lambda b, i, c: (b, c[0], i, 0)), blk], out_specs=blk),
        out_shape=jax.ShapeDtypeStruct((nb, hs, cols), WIRE_DTYPE),
    )(c_arr, g5, recv)


def _final_sum(name, own, recv, place_arr):
    _, hs, cols = own.shape
    tr = _row_tile(hs, cols)
    nt = hs // tr

    def body(_, a_ref, r_ref, o_ref):
        o_ref[...] = ((a_ref[...].astype(F32) + r_ref[0].astype(F32)) + r_ref[1].astype(F32)) + r_ref[2].astype(F32)

    return pl.pallas_call(
        body, name=name,
        grid_spec=pltpu.PrefetchScalarGridSpec(
            num_scalar_prefetch=1, grid=(nt,),
            in_specs=[pl.BlockSpec((None, tr, cols), lambda i, m: (m[0], i, 0)),
                      pl.BlockSpec((3, tr, cols), lambda i, m: (0, i, 0))],
            out_specs=pl.BlockSpec((tr, cols), lambda i, m: (m[1] * nt + i, 0))),
        out_shape=jax.ShapeDtypeStruct((2 * hs, cols), F32),
    )(place_arr, own, recv)


def _adamw(w, g, m, v):
    m = ADAM_B1 * m + (1.0 - ADAM_B1) * g
    v = ADAM_B2 * v + (1.0 - ADAM_B2) * (g * g)
    m_hat = m / (1.0 - ADAM_B1 ** ADAM_STEP)
    v_hat = v / (1.0 - ADAM_B2 ** ADAM_STEP)
    return -ADAM_LR * (m_hat / (jnp.sqrt(v_hat) + ADAM_EPS) + ADAM_WD * w), m, v


def _adamw_call(name, w, g, m, v):
    rows, cols = w.shape
    tr = _row_tile(rows, cols)
    if 4 * tr * cols >= (1 << 18):
        blk, steps = pl.BlockSpec((tr, cols), lambda i: (i, 0)), rows // tr
    else:
        blk, steps = pl.BlockSpec((rows, 128), lambda i: (0, i)), cols // 128

    def body(w_ref, g_ref, m_ref, v_ref, d_ref, nm_ref, nv_ref, go_ref):
        g = g_ref[...]
        d_ref[...], nm_ref[...], nv_ref[...] = _adamw(w_ref[...], g, m_ref[...], v_ref[...])
        go_ref[...] = g

    return pl.pallas_call(
        body, name=name, grid=(steps,), in_specs=[blk] * 4, out_specs=[blk] * 4,
        out_shape=[jax.ShapeDtypeStruct(w.shape, F32)] * 4,
        compiler_params=pltpu.CompilerParams(dimension_semantics=("parallel",)),
    )(w, g, m, v)


def _adamw_small(ws, gs, ms, vs):
    n = len(ws)

    def body(*refs):
        for i in range(n):
            w_ref, g_ref, m_ref, v_ref = (refs[k * n + i] for k in range(4))
            d, nm, nv = _adamw(w_ref[...], g_ref[...], m_ref[...], v_ref[...])
            refs[4 * n + i][...] = d
            refs[5 * n + i][...] = nm
            refs[6 * n + i][...] = nv

    out = pl.pallas_call(
        body, name="adamw_small", out_shape=[jax.ShapeDtypeStruct(a.shape, F32) for a in ws] * 3,
    )(*ws, *gs, *ms, *vs)
    return out[:n], out[n:2 * n], out[2 * n:]


_BIG = ("w_in", "w_oa", "w_ob", "w_out", "w_pg", "w_ple")
_SMALL = ("norm_g", "ln_a_g", "ln_a_b", "w_s", "b_s", "conv_w", "conv_b", "dt_bias", "a_log", "d_skip", "ssm_norm_g",
          "ple_norm_g", "final_g")
_WEIGHTS = ("norm_g", "w_in", "ln_a_g", "ln_a_b", "w_s", "b_s", "conv_w", "conv_b", "dt_bias", "a_log", "d_skip",
            "ssm_norm_g", "w_oa", "w_ob", "w_out", "ple_norm_g", "w_pg", "w_ple", "final_g")
_COL_SHARDED = ("w_in", "w_ple")
_PACK = 1024


def _blocks_to_full(col_sharded, blocks):
    if col_sharded:
        return jnp.concatenate([blocks[k] for k in range(N_CHIPS)], axis=1)
    return blocks.reshape(N_CHIPS * blocks.shape[1], blocks.shape[2])


def _full_to_blocks(col_sharded, full):
    if col_sharded:
        w = full.shape[1] // N_CHIPS
        return jnp.stack([full[:, k * w:(k + 1) * w] for k in range(N_CHIPS)])
    return full.reshape(N_CHIPS, full.shape[0] // N_CHIPS, full.shape[1])


def _two_d(n, a):
    if n == "w_s":
        return a.reshape(G_A * CHUNK, CHUNK)
    if n in ("b_s", "conv_w"):
        return a.reshape(a.shape[-2], a.shape[-1])
    return a.reshape(1, a.shape[-1])


def kernel(x, p, norm_g, w_in, ln_a_g, ln_a_b, w_s, b_s, conv_w, conv_b, dt_bias, a_log, d_skip, ssm_norm_g, w_oa, w_ob, w_out, ple_norm_g, w_pg, w_ple, final_g, loss_target, m_norm_g, m_w_in, m_ln_a_g, m_ln_a_b, m_w_s, m_b_s, m_conv_w, m_conv_b, m_dt_bias, m_a_log, m_d_skip, m_ssm_norm_g, m_w_oa, m_w_ob, m_w_out, m_ple_norm_g, m_w_pg, m_w_ple, m_final_g, v_norm_g, v_w_in, v_ln_a_g, v_ln_a_b, v_w_s, v_b_s, v_conv_w, v_conv_b, v_dt_bias, v_a_log, v_d_skip, v_ssm_norm_g, v_w_oa, v_w_ob, v_w_out, v_ple_norm_g, v_w_pg, v_w_ple, v_final_g):
    wt = dict(norm_g=norm_g, w_in=w_in, ln_a_g=ln_a_g, ln_a_b=ln_a_b, w_s=w_s, b_s=b_s, conv_w=conv_w, conv_b=conv_b,
              dt_bias=dt_bias, a_log=a_log, d_skip=d_skip, ssm_norm_g=ssm_norm_g, w_oa=w_oa, w_ob=w_ob, w_out=w_out,
              ple_norm_g=ple_norm_g, w_pg=w_pg, w_ple=w_ple, final_g=final_g)
    mom = dict(norm_g=m_norm_g, w_in=m_w_in, ln_a_g=m_ln_a_g, ln_a_b=m_ln_a_b, w_s=m_w_s, b_s=m_b_s, conv_w=m_conv_w,
               conv_b=m_conv_b, dt_bias=m_dt_bias, a_log=m_a_log, d_skip=m_d_skip, ssm_norm_g=m_ssm_norm_g, w_oa=m_w_oa,
               w_ob=m_w_ob, w_out=m_w_out, ple_norm_g=m_ple_norm_g, w_pg=m_w_pg, w_ple=m_w_ple, final_g=m_final_g)
    vel = dict(norm_g=v_norm_g, w_in=v_w_in, ln_a_g=v_ln_a_g, ln_a_b=v_ln_a_b, w_s=v_w_s, b_s=v_b_s, conv_w=v_conv_w,
               conv_b=v_conv_b, dt_bias=v_dt_bias, a_log=v_a_log, d_skip=v_d_skip, ssm_norm_g=v_ssm_norm_g, w_oa=v_w_oa,
               w_ob=v_w_ob, w_out=v_w_out, ple_norm_g=v_ple_norm_g, w_pg=v_w_pg, w_ple=v_w_ple, final_g=v_final_g)
    xi, yi, ci = _place()
    me = 2 * xi + yi
    c_arr = jnp.reshape(ci, (1,)).astype(jnp.int32)
    place_arr = jnp.stack([me, ci]).astype(jnp.int32)

    shard = {n: wt[n][0] for n in _BIG}
    wire = {n: shard[n].astype(WIRE_DTYPE) for n in _BIG}
    w_in_blocks, conv_blocks = _gather_weights([wire["w_in"]], conv_w[0])
    g_ssem, g_rsem, g_sent, g_lands, g_token = _late_gather_start([wire[n] for n in _BIG[1:]])
    full = {"conv_w": _blocks_to_full(True, conv_blocks)}
    for n in _SMALL:
        if n != "conv_w":
            full[n] = wt[n][0] if wt[n].ndim > 2 else wt[n].reshape(1, wt[n].shape[-1])

    def late_weights(after):
        blocks = _late_gather_wait(g_ssem, g_rsem, g_sent, g_lands, after)
        return {n: _blocks_to_full(n in _COL_SHARDED, b) for n, b in zip(_BIG[1:], blocks)}

    w = _layout_weights(full, w_in_blocks=w_in_blocks)
    loss_row, g, ctx = _forward_backward(x[0], p[0, 0], loss_target[0], w, late_weights, after=(g_token,))
    loss = lax.psum(loss_row[0, 0], ("x", "y", "c"))

    parts = {n: _full_to_blocks(n in _COL_SHARDED, g[n]) for n in _BIG[1:]}
    parts["w_main"] = g["w_main"][None]
    parts["w_dt"] = jnp.pad(_lanes_to_heads(g["w_dt"]), ((0, 0), (0, 128 - N_HEADS)))[None]
    names = ("w_main", "w_dt") + _BIG[1:]
    g5 = {n: parts[n].reshape(parts[n].shape[0], 2, parts[n].shape[1] // 2, parts[n].shape[2]) for n in names}
    to_sibling = [lax.dynamic_index_in_dim(g5[n], 1 - ci, axis=1, keepdims=False).astype(WIRE_DTYPE) for n in names]
    from_sibling = _swap_with_sibling(to_sibling)
    chip = {n: _chip_sum("chip_sum_" + n, g5[n], r, c_arr) for n, r in zip(names, from_sibling)}
    chip["w_in"] = _w_in_grad_blocks(chip["w_main"][0], chip["w_dt"][0])
    chip_wire = [chip[n] for n in _BIG]
    s_sem, r_sem, sent, lands, token = _scatter_blocks_start(chip_wire)
    grad_x, g["norm_g"] = _input_grad(x[0], w, ctx, after=(token,))
    g = _natural_grads(g)

    pieces = [_two_d(n, g[n]).reshape(-1) for n in _SMALL]
    sizes = [v.shape[0] for v in pieces]
    padded = [-(-s // _PACK) * _PACK for s in sizes]
    packed = jnp.concatenate([jnp.pad(v, (0, ps - s)) for v, s, ps in zip(pieces, sizes, padded)]).reshape(-1, 128)
    a_ssem, a_rsem, a_src, a_land, a_token = _small_gather_start(packed)

    sent, from_chips = _scatter_blocks_wait(s_sem, r_sem, sent, lands, (grad_x, a_token))
    halves = [_final_sum("final_sum_" + n, a, r, place_arr) for n, a, r in zip(_BIG, sent, from_chips)]
    grads = dict(zip(_BIG, _share_halves(halves)))
    delta, new_m, new_v = {}, {}, {}
    for n in _BIG:
        t = jnp.transpose if n == "w_in" else (lambda a: a)
        res = _adamw_call("adamw_" + n, t(shard[n]), t(grads[n]), t(mom[n][0]), t(vel[n][0]))
        delta[n], new_m[n], new_v[n], grads[n] = (t(r) for r in res)

    a_src, a_land = _small_gather_wait(a_ssem, a_rsem, a_src, a_land, delta["w_in"])
    summed = _small_sum(a_src, a_land, jnp.reshape(4 * xi + 2 * yi + ci, (1,)).astype(jnp.int32)).reshape(-1)
    off = 0
    for n, s, ps in zip(_SMALL, sizes, padded):
        grads[n] = summed[off:off + s].reshape(_two_d(n, g[n]).shape)
        off += ps
    grads["conv_w"] = lax.dynamic_slice_in_dim(grads["conv_w"], me * (CONV_DIM // N_CHIPS), CONV_DIM // N_CHIPS, axis=1)
    small = _adamw_small([_two_d(n, wt[n]) for n in _SMALL], [grads[n] for n in _SMALL],
                         [_two_d(n, mom[n]) for n in _SMALL], [_two_d(n, vel[n]) for n in _SMALL])
    for i, n in enumerate(_SMALL):
        delta[n], new_m[n], new_v[n] = small[0][i], small[1][i], small[2][i]

    def shaped(d):
        return [d[n].reshape(wt[n].shape) for n in _WEIGHTS]

    return (loss, grad_x[None], *shaped(grads), *shaped(delta), *shaped(new_m), *shaped(new_v))
```

```python
import functools

import jax
import jax.numpy as jnp
from jax import lax
from jax.experimental import pallas as pl
from jax.experimental.pallas import tpu as pltpu

F32 = jnp.float32
MXU_DTYPE = jnp.bfloat16
ACT_DTYPE = jnp.bfloat16
WIRE_DTYPE = jnp.bfloat16

D_MODEL = 1024
PLE_DIM = 256
CHUNK = 128
EPS = 1e-6
E_A = D_MODEL
G_A = 4
D_INNER = 2 * D_MODEL
HEAD_DIM = 64
N_HEADS = D_INNER // HEAD_DIM
N_STATE = 128
N_GROUPS = 4
HEADS_PER_GROUP = N_HEADS // N_GROUPS
PAIRS_PER_GROUP = HEADS_PER_GROUP // 2
CONV_K = 4
CONV_DIM = D_INNER + 2 * N_GROUPS * N_STATE
N_IN = 3 * E_A + D_INNER + CONV_DIM + N_HEADS + 2 * D_MODEL
N_CHIPS = 4
N_DEV = 8
W_IN_BLOCK = N_IN // N_CHIPS

UVZ_W, XBC_W, ZB_W, G_W = 3 * E_A, CONV_DIM, D_INNER, 2 * D_MODEL
MAIN_W = UVZ_W + XBC_W + ZB_W + G_W
UVZ_CB, XBC_CB, ZB_CB, G_CB = 0, 1, 3, 4
DT_W = N_GROUPS * 128

ADAM_LR, ADAM_B1, ADAM_B2, ADAM_EPS, ADAM_WD, ADAM_STEP = 0.001, 0.9, 0.999, 1e-08, 0.01, 10

MESH = pl.DeviceIdType.MESH
ANY = pl.BlockSpec(memory_space=pl.ANY)


def _mxu(v):
    return v.astype(MXU_DTYPE)


def _dot(a, b, dims=(((1,), (0,)), ((), ()))):
    return lax.dot_general(_mxu(a), _mxu(b), dims, preferred_element_type=F32)


V7X_MXU_WIDTH = 256
V7X_SCOPED_VMEM_BYTES = 60000 * 1024
MM_TILE = 4 * V7X_MXU_WIDTH
MM_VMEM_BUDGET = 4 * V7X_SCOPED_VMEM_BYTES // 5


def _mm_tk(m, n, k, tm, tn, a_bytes, b_bytes, out_bytes, extra_bytes):
    one_tile = m == tm and n == tn
    for parts in range(2 if one_tile else 1, k // 128 + 1):
        if k % parts or (k // parts) % 128 and parts > 1:
            continue
        tk = k // parts
        a_bufs = 1 if (parts == 1 and m == tm) else 2
        b_bufs = 1 if (parts == 1 and n == tn) else 2
        need = (tk * (a_bufs * tm * a_bytes + b_bufs * tn * b_bytes) + 2 * tm * tn * (out_bytes + extra_bytes)
                + (tm * tn * 4 if parts > 1 else 0))
        if need <= MM_VMEM_BUDGET:
            return tk
    return 128


def _matmul(a, b, *, mode, name, out_dtype, m, n, k, tm=MM_TILE, tn=MM_TILE, tk=None, a_off=0, b_off=0,
            extras=(), epilogue=None, after=()):
    tm, tn = min(tm, m), min(tn, n)
    if tk is None:
        tk = _mm_tk(m, n, k, tm, tn, a.dtype.itemsize, b.dtype.itemsize, jnp.dtype(out_dtype).itemsize,
                    sum(e.dtype.itemsize for e in extras))
    tk = min(tk, k)
    assert m % tm == 0 and n % tn == 0 and k % tk == 0, (name, m, n, k, tm, tn, tk)
    nk = k // tk
    a_mode = pl.Buffered(1) if (nk == 1 and m == tm) else None
    b_mode = pl.Buffered(1) if (nk == 1 and n == tn) else None
    if mode == "nn":
        assert a_off % tk == 0 and b_off % tn == 0
        a_spec = pl.BlockSpec((tm, tk), lambda i, j, kk: (i, kk + a_off // tk), pipeline_mode=a_mode)
        b_spec = pl.BlockSpec((tk, tn), lambda i, j, kk: (kk, j + b_off // tn), pipeline_mode=b_mode)
        dims = (((1,), (0,)), ((), ()))
    elif mode == "nt":
        a_spec = pl.BlockSpec((tm, tk), lambda i, j, kk: (i, kk), pipeline_mode=a_mode)
        b_spec = pl.BlockSpec((tn, tk), lambda i, j, kk: (j, kk), pipeline_mode=b_mode)
        dims = (((1,), (1,)), ((), ()))
    else:
        assert a_off % tm == 0 and b_off % tn == 0
        a_spec = pl.BlockSpec((tk, tm), lambda i, j, kk: (kk, i + a_off // tm), pipeline_mode=a_mode)
        b_spec = pl.BlockSpec((tk, tn), lambda i, j, kk: (kk, j + b_off // tn), pipeline_mode=b_mode)
        dims = (((0,), (0,)), ((), ()))
    ne = len(extras)

    def finish(acc, extra_refs, o_ref):
        res = acc if epilogue is None else epilogue(acc, *[e[...] for e in extra_refs])
        o_ref[...] = res.astype(o_ref.dtype)

    def body(a_ref, b_ref, *rest):
        extra_refs, o_ref = rest[:ne], rest[ne + len(after)]
        part = _dot(a_ref[...], b_ref[...], dims)
        if nk == 1:
            finish(part, extra_refs, o_ref)
            return
        acc_ref = rest[ne + len(after) + 1]
        kk = pl.program_id(2)

        @pl.when(kk == 0)
        def _():
            acc_ref[...] = part

        @pl.when(kk > 0)
        def _():
            acc_ref[...] += part

        @pl.when(kk == nk - 1)
        def _():
            finish(acc_ref[...], extra_refs, o_ref)

    o_spec = pl.BlockSpec((tm, tn), lambda i, j, kk: (i, j))
    return pl.pallas_call(
        body, name=name, grid=(m // tm, n // tn, nk),
        in_specs=[a_spec, b_spec] + [o_spec] * ne + [ANY] * len(after), out_specs=o_spec,
        out_shape=jax.ShapeDtypeStruct((m, n), out_dtype),
        scratch_shapes=[pltpu.VMEM((tm, tn), F32)] if nk > 1 else [],
        compiler_params=pltpu.CompilerParams(dimension_semantics=("parallel", "parallel", "arbitrary")),
    )(a, b, *extras, *after)


def _rows_matmul(name, f, rows, pars, b, *, out_dtype, n, k, tm, nrows, tn=MM_TILE, extras=(), epilogue=None, after=(),
                 side=None):
    tm, tn = min(tm, nrows), min(tn, n)
    assert nrows % tm == 0 and n % tn == 0, (name, nrows, n, tm, tn)
    nr, npar, ne, nj = len(rows), len(pars), len(extras), n // tn
    ns = 0 if side is None else 1
    n_in = nr + npar + 1 + ne + ns + len(after)

    def body(*refs):
        row_refs, par_refs, b_ref = refs[:nr], refs[nr:nr + npar], refs[nr + npar]
        extra_refs = refs[nr + npar + 1:nr + npar + 1 + ne]
        a_ref, o_ref = refs[n_in], refs[n_in + 1]

        def make_a():
            a = f(*[r[...].astype(F32) for r in row_refs], *[p[...] for p in par_refs])[0]
            a_ref[...] = a.astype(a_ref.dtype)
            if ns:
                refs[n_in + 2][...] = _dot(a_ref[...], refs[nr + npar + 1 + ne][...]).astype(refs[n_in + 2].dtype)

        if nj == 1:
            make_a()
        else:
            pl.when(pl.program_id(1) == 0)(make_a)
        res = _dot(a_ref[...], b_ref[...])
        if epilogue is not None:
            res = epilogue(res, *[e[...] for e in extra_refs])
        o_ref[...] = res.astype(o_ref.dtype)

    o_spec = pl.BlockSpec((tm, tn), lambda i, j: (i, j))
    side_in = [] if side is None else [pl.BlockSpec(tuple(side[0].shape), lambda i, j: (0, 0))]
    side_out = [] if side is None else [pl.BlockSpec((tm, side[0].shape[1]), lambda i, j: (i, 0))]
    side_shape = [] if side is None else [jax.ShapeDtypeStruct((nrows, side[0].shape[1]), side[1])]
    return pl.pallas_call(
        body, name=name, grid=(nrows // tm, nj),
        in_specs=[pl.BlockSpec((tm, w), lambda i, j, cb=cb: (i, cb)) for _, w, cb in rows]
        + [pl.BlockSpec(tuple(p.shape), lambda i, j, nd=p.ndim: (0,) * nd) for p in pars]
        + [pl.BlockSpec((k, tn), lambda i, j: (0, j))] + [o_spec] * ne + side_in + [ANY] * len(after),
        out_specs=[pl.BlockSpec((tm, k), lambda i, j: (i, 0)), o_spec] + side_out,
        out_shape=[jax.ShapeDtypeStruct((nrows, k), ACT_DTYPE), jax.ShapeDtypeStruct((nrows, n), out_dtype)] + side_shape,
        compiler_params=pltpu.CompilerParams(dimension_semantics=("parallel", "arbitrary")),
    )(*[r[0] for r in rows], *pars, b, *extras, *([] if side is None else [side[0]]), *after)


def _row_spec(tm, width, cb):
    return pl.BlockSpec((tm, width), lambda i: (i, cb))


def _whole_spec(shape):
    nd = len(shape)
    return pl.BlockSpec(tuple(shape), lambda i: (0,) * nd)


def _rows_vjp_call(name, f, rows, pars, cots, drows, *, tm, nrows, cot_mm=None):
    tm = min(tm, nrows)
    nr, npar, nc = len(rows), len(pars), len(cots)
    mm_args, mm_specs = [], []
    if cot_mm is not None:
        mm_a, mm_b, mm_add = cot_mm
        mm_args = [mm_a, mm_b] + ([] if mm_add is None else [mm_add])
        mm_specs = [_row_spec(tm, mm_a.shape[1], 0), _whole_spec(mm_b.shape)]
        mm_specs += [] if mm_add is None else [_row_spec(tm, mm_b.shape[0], 0)]
    alias_bufs, aliases = [], {}
    out_shape, out_specs = [], []
    for (arr, w, cb), d in zip(rows, drows):
        if d is None:
            continue
        dt, into = d
        if into is None:
            out_shape.append(jax.ShapeDtypeStruct((nrows, w), dt))
            out_specs.append(_row_spec(tm, w, 0))
        else:
            buf, total, ocb = into
            if buf is not None:
                aliases[nr + npar + nc + len(alias_bufs)] = len(out_shape)
                alias_bufs.append(buf)
            out_shape.append(jax.ShapeDtypeStruct((nrows, total), dt))
            out_specs.append(_row_spec(tm, w, ocb))
    n_drow = len(out_shape)
    for p in pars:
        out_shape.append(jax.ShapeDtypeStruct(p.shape, F32))
        out_specs.append(_whole_spec(p.shape))
    na = len(alias_bufs)

    def body(*refs):
        rv = [r[...].astype(F32) for r in refs[:nr]]
        pv = [p[...] for p in refs[nr:nr + npar]]
        cv = tuple(c[...].astype(F32) for c in refs[nr + npar:nr + npar + nc])
        o_refs = refs[nr + npar + nc + na + len(mm_args):]
        if mm_args:
            mm_refs = refs[nr + npar + nc + na:nr + npar + nc + na + len(mm_args)]
            c0 = _dot(mm_refs[0][...], mm_refs[1][...], (((1,), (1,)), ((), ())))
            if len(mm_refs) == 3:
                c0 = c0 + mm_refs[2][...].astype(F32)
            cv = (c0,) + cv
        _, vjp = jax.vjp(f, *rv, *pv)
        g = vjp(cv)
        oi = 0
        for ri, d in enumerate(drows):
            if d is not None:
                o_refs[oi][...] = g[ri].astype(o_refs[oi].dtype)
                oi += 1
        first = pl.program_id(0) == 0
        for pi in range(npar):
            acc = o_refs[n_drow + pi]

            @pl.when(first)
            def _(acc=acc):
                acc[...] = jnp.zeros_like(acc)

            acc[...] += g[nr + pi]

    return pl.pallas_call(
        body, name=name, grid=(nrows // tm,),
        in_specs=[_row_spec(tm, w, cb) for _, w, cb in rows] + [_whole_spec(p.shape) for p in pars]
        + [_row_spec(tm, w, cb) for _, w, cb in cots] + [ANY] * na + mm_specs,
        out_specs=out_specs, out_shape=out_shape, input_output_aliases=aliases,
        compiler_params=pltpu.CompilerParams(dimension_semantics=("arbitrary",)),
    )(*[r[0] for r in rows], *pars, *[c[0] for c in cots], *alias_bufs, *mm_args)


def _rms(x, g):
    return x * lax.rsqrt(jnp.mean(x * x, axis=-1, keepdims=True) + EPS) * g


def _f_rms(x, g):
    return (_rms(x, g),)


def _tril_mask():
    return lax.broadcasted_iota(jnp.int32, (CHUNK, CHUNK), 0) >= lax.broadcasted_iota(jnp.int32, (CHUNK, CHUNK), 1)


def _f_branch_a(uvz, ln_g, ln_b, w_s, b_s):
    u = jax.nn.gelu(uvz[:, :E_A])
    v = jax.nn.gelu(uvz[:, E_A:2 * E_A])
    z = uvz[:, 2 * E_A:]
    xc = v - jnp.mean(v, axis=-1, keepdims=True)
    vn = xc * lax.rsqrt(jnp.mean(xc * xc, axis=-1, keepdims=True) + EPS) * ln_g + ln_b
    mask = _tril_mask()
    ws = [jnp.where(mask, w_s[g], 0.0) for g in range(G_A)]
    gw = E_A // G_A
    rows = []
    for c in range(uvz.shape[0] // CHUNK):
        vc = vn[c * CHUNK:(c + 1) * CHUNK]
        rows.append(jnp.concatenate([_dot(ws[g], vc[:, g * gw:(g + 1) * gw]) + b_s[g] for g in range(G_A)], axis=1))
    sv = rows[0] if len(rows) == 1 else jnp.concatenate(rows, axis=0)
    return (u * sv * jax.nn.silu(z),)


def _f_gnorm(y, zb, g):
    yz = y * jax.nn.silu(zb)
    gw = D_INNER // N_GROUPS
    parts = []
    for i in range(N_GROUPS):
        s = yz[:, i * gw:(i + 1) * gw]
        parts.append(s * lax.rsqrt(jnp.mean(s * s, axis=-1, keepdims=True) + EPS))
    return (jnp.concatenate(parts, axis=1) * g,)


def _f_merge(g2, oa, ob):
    return (jax.nn.sigmoid(g2[:, :D_MODEL]) * oa + jax.nn.sigmoid(g2[:, D_MODEL:]) * ob,)


def _f_loss(x1, gp, pe, tgt, fg):
    x2 = x1 + jax.nn.sigmoid(gp) * pe
    err = _rms(x2, fg) - tgt
    return 0.5 * jnp.sum(jnp.mean(err * err, axis=-1))


def _head(x1, p, tgt, ple_g, w_pg, w_ple, fg, *, tm, nrows):
    tm = min(tm, nrows)

    def body(x1_ref, p_ref, t_ref, pg_ref, wpg_ref, wple_ref, fg_ref, hp_ref, dx_ref, dgp_ref, dpe_ref, dfg_ref, loss_ref):
        x1 = x1_ref[...]
        hp_ref[...] = _rms(x1, pg_ref[...]).astype(hp_ref.dtype)
        gp = _dot(hp_ref[...], wpg_ref[...])
        pe = _dot(p_ref[...], wple_ref[...])
        loss, vjp = jax.vjp(_f_loss, x1, gp, pe, t_ref[...], fg_ref[...])
        dx, dgp, dpe, _, dfg = vjp(jnp.ones((), F32))
        dx_ref[...] = dx
        dgp_ref[...] = dgp.astype(dgp_ref.dtype)
        dpe_ref[...] = dpe.astype(dpe_ref.dtype)

        @pl.when(pl.program_id(0) == 0)
        def _():
            dfg_ref[...] = jnp.zeros_like(dfg_ref)
            loss_ref[...] = jnp.zeros_like(loss_ref)

        dfg_ref[...] += dfg
        loss_ref[...] += jnp.full(loss_ref.shape, loss, F32)

    row = _row_spec(tm, D_MODEL, 0)
    act = jax.ShapeDtypeStruct((nrows, D_MODEL), ACT_DTYPE)
    return pl.pallas_call(
        body, name="head", grid=(nrows // tm,),
        in_specs=[row, _row_spec(tm, PLE_DIM, 0), row, _whole_spec((1, D_MODEL)), _whole_spec(w_pg.shape),
                  _whole_spec(w_ple.shape), _whole_spec((1, D_MODEL))],
        out_specs=[row, row, row, row, _whole_spec((1, D_MODEL)), _whole_spec((1, 128))],
        out_shape=[act, jax.ShapeDtypeStruct((nrows, D_MODEL), F32), act, act, jax.ShapeDtypeStruct((1, D_MODEL), F32),
                   jax.ShapeDtypeStruct((1, 128), F32)],
        compiler_params=pltpu.CompilerParams(dimension_semantics=("arbitrary",)),
    )(x1, p, tgt, ple_g, w_pg, w_ple, fg)


def _shift_rows(cur, edge, j, up):
    tm = cur.shape[0]
    row = lax.broadcasted_iota(jnp.int32, cur.shape, 0)
    if up:
        sh = pltpu.roll(cur, tm - j, 0)
        e = jnp.tile(pltpu.roll(edge, 8 - j, 0), (tm // 8, 1))
        return jnp.where(row >= tm - j, e, sh)
    sh = pltpu.roll(cur, j, 0)
    e = jnp.tile(pltpu.roll(edge, j, 0), (tm // 8, 1))
    return jnp.where(row < j, e, sh)


def _conv_pre(cur, prev, w, b):
    acc = cur * w[CONV_K - 1:CONV_K] + b
    for j in range(1, CONV_K):
        acc = acc + _shift_rows(cur, prev, j, up=False) * w[CONV_K - 1 - j:CONV_K - j]
    return acc


def _halo_specs(tm, nrows, cb, before):
    nb = tm // 8
    last = nrows // 8 - 1
    if before:
        return pl.BlockSpec((8, XBC_W), lambda i: (jnp.maximum(i * nb - 1, 0), cb))
    return pl.BlockSpec((8, XBC_W), lambda i: (jnp.minimum((i + 1) * nb, last), cb))


def _conv_fwd(proj, conv_w, conv_b, *, tm, nrows):
    tm = min(tm, nrows)

    def body(cur_ref, prev_ref, w_ref, b_ref, o_ref, pre_ref):
        prev = jnp.where(pl.program_id(0) == 0, 0.0, prev_ref[...].astype(F32))
        pre = _conv_pre(cur_ref[...].astype(F32), prev, w_ref[...], b_ref[...])
        o_ref[...] = jax.nn.silu(pre).astype(o_ref.dtype)
        pre_ref[...] = pre.astype(pre_ref.dtype)

    out = jax.ShapeDtypeStruct((nrows, XBC_W), ACT_DTYPE)
    return pl.pallas_call(
        body, name="conv_fwd", grid=(nrows // tm,),
        in_specs=[_row_spec(tm, XBC_W, XBC_CB), _halo_specs(tm, nrows, XBC_CB, True),
                  _whole_spec((CONV_K, XBC_W)), _whole_spec((1, XBC_W))],
        out_specs=[_row_spec(tm, XBC_W, 0)] * 2, out_shape=[out, out],
        compiler_params=pltpu.CompilerParams(dimension_semantics=("parallel",)),
    )(proj, proj, conv_w, conv_b)


def _conv_bwd_act(pre, dact, *, tm, nrows):
    tm = min(tm, nrows)
    nb = N_GROUPS * N_STATE

    def body(pre_ref, dxs_ref, dbm_ref, dcm_ref, dpre_ref, db_ref):
        pre = pre_ref[...].astype(F32)
        sg = jax.nn.sigmoid(pre)
        dy = jnp.concatenate([dxs_ref[...], dbm_ref[...], dcm_ref[...]], axis=1).astype(F32)
        dpre = dy * sg * (1.0 + pre * (1.0 - sg))
        dpre_ref[...] = dpre.astype(dpre_ref.dtype)

        @pl.when(pl.program_id(0) == 0)
        def _():
            db_ref[...] = jnp.zeros_like(db_ref)

        db_ref[...] += jnp.sum(dpre, axis=0, keepdims=True)

    return pl.pallas_call(
        body, name="conv_bwd_act", grid=(nrows // tm,),
        in_specs=[_row_spec(tm, XBC_W, 0), _row_spec(tm, D_INNER, 0), _row_spec(tm, nb, 0), _row_spec(tm, nb, 0)],
        out_specs=[_row_spec(tm, XBC_W, 0), _whole_spec((1, XBC_W))],
        out_shape=[jax.ShapeDtypeStruct((nrows, XBC_W), ACT_DTYPE), jax.ShapeDtypeStruct((1, XBC_W), F32)],
        compiler_params=pltpu.CompilerParams(dimension_semantics=("arbitrary",)),
    )(pre, *dact)


def _conv_bwd_x(dpre, proj, conv_w, dproj, *, tm, nrows):
    tm = min(tm, nrows)
    ntiles = nrows // tm

    def body(cur_ref, nxt_ref, x_ref, w_ref, _, o_ref, dw_ref):
        cur = cur_ref[...].astype(F32)
        nxt = jnp.where(pl.program_id(0) == ntiles - 1, 0.0, nxt_ref[...].astype(F32))
        x = x_ref[...].astype(F32)
        w = w_ref[...]

        @pl.when(pl.program_id(0) == 0)
        def _():
            dw_ref[...] = jnp.zeros_like(dw_ref)

        acc = cur * w[CONV_K - 1:CONV_K]
        dw_ref[CONV_K - 1:CONV_K, :] += jnp.sum(cur * x, axis=0, keepdims=True)
        for j in range(1, CONV_K):
            u = _shift_rows(cur, nxt, j, up=True)
            acc = acc + u * w[CONV_K - 1 - j:CONV_K - j]
            dw_ref[CONV_K - 1 - j:CONV_K - j, :] += jnp.sum(u * x, axis=0, keepdims=True)
        o_ref[...] = acc.astype(o_ref.dtype)

    return pl.pallas_call(
        body, name="conv_bwd_x", grid=(ntiles,),
        in_specs=[_row_spec(tm, XBC_W, 0), _halo_specs(tm, nrows, 0, False), _row_spec(tm, XBC_W, XBC_CB),
                  _whole_spec((CONV_K, XBC_W)), ANY],
        out_specs=[_row_spec(tm, XBC_W, XBC_CB), _whole_spec((CONV_K, XBC_W))],
        out_shape=[jax.ShapeDtypeStruct(dproj.shape, dproj.dtype), jax.ShapeDtypeStruct((CONV_K, XBC_W), F32)],
        input_output_aliases={4: 0},
        compiler_params=pltpu.CompilerParams(dimension_semantics=("arbitrary",)),
    )(dpre, dpre, proj, conv_w, dproj)


SSD_SPAN = 4
SSD_FWD_SPANS = 8
_XS_GW = D_INNER // N_GROUPS
_NT = (((1,), (1,)), ((), ()))
_TN = (((0,), (0,)), ((), ()))


def _bf16_terms(x, terms):
    parts, rest = [], x
    for _ in range(terms):
        part = rest.astype(jnp.bfloat16)
        parts.append(part)
        rest = rest - part.astype(F32)
    return parts


def _head_lane_matrix():
    return (lax.broadcasted_iota(jnp.int32, (128, _XS_GW), 0)
            == lax.broadcasted_iota(jnp.int32, (128, _XS_GW), 1) // HEAD_DIM).astype(jnp.bfloat16)


@functools.partial(jax.custom_vjp, nondiff_argnums=(1,))
def _head_lanes(cols, terms):
    e = _head_lane_matrix()
    return sum(jnp.dot(t, e, preferred_element_type=F32) for t in _bf16_terms(cols, terms))


def _head_lanes_fwd(cols, terms):
    return _head_lanes(cols, terms), None


def _head_lanes_bwd(terms, _, g):
    e = _head_lane_matrix()
    return (sum(lax.dot_general(t, e, _NT, preferred_element_type=F32) for t in _bf16_terms(g, 2)),)


_head_lanes.defvjp(_head_lanes_fwd, _head_lanes_bwd)


def _ssd_chunk(k, xs, bm, cm, dtr, hprev, dtb, alog, dsk):
    causal, tri, lo = k
    dt = jax.nn.softplus(dtr + dtb)
    da = dt * (-jnp.exp(alog))
    cs = jnp.dot(tri, da, precision=lax.Precision.HIGHEST, preferred_element_type=F32)
    cst = cs.T
    cs_l = _head_lanes(cs, 3)
    xdt = xs * _head_lanes(dt, 2)
    cb = _dot(cm, bm, _NT)
    yd = []
    for q in range(PAIRS_PER_GROUP):
        xq = xdt[:, 128 * q:128 * (q + 1)]
        y2 = [_dot(cb * jnp.exp(jnp.where(causal, cs[:, h:h + 1] - cst[h:h + 1, :], -jnp.inf)), xq)
              for h in (2 * q, 2 * q + 1)]
        yd.append(jnp.where(lo, y2[0], y2[1]))
    y_off = jnp.exp(cs_l) * _dot(cm, hprev, _NT)
    st = _dot(xdt * jnp.exp(cs_l[CHUNK - 1:CHUNK, :] - cs_l), bm, _TN)
    cdec = jnp.exp(cs[CHUNK - 1:CHUNK, :])
    cd_rows = jnp.concatenate(
        [jnp.broadcast_to(cdec[:, h:h + 1], (HEAD_DIM, N_STATE)) for h in range(HEADS_PER_GROUP)], axis=0)
    dsk_l = _head_lanes(jnp.broadcast_to(dsk, (8, 128)), 2)[:1]
    y = jnp.concatenate(yd, axis=1) + y_off + xs * dsk_l
    return y, cd_rows * hprev + st


def _ssd_span(xs, bm, cm, dtr, h0, dtb, alog, dsk):
    li = lax.broadcasted_iota(jnp.int32, (CHUNK, CHUNK), 0)
    si = lax.broadcasted_iota(jnp.int32, (CHUNK, CHUNK), 1)
    causal = li >= si
    k = (causal, causal.astype(F32), si < HEAD_DIM)
    h, ys = h0, []
    for t in range(xs.shape[0] // CHUNK):
        r = slice(t * CHUNK, (t + 1) * CHUNK)
        y, h = _ssd_chunk(k, xs[r], bm[r], cm[r], dtr[r], h, dtb, alog, dsk)
        ys.append(y)
    return (ys[0] if len(ys) == 1 else jnp.concatenate(ys, axis=0)), h


def _ssd_specs(rev, nsteps, rows):
    def s_of(s):
        return nsteps - 1 - s if rev else s

    xs = pl.BlockSpec((rows, _XS_GW), lambda g, s: (s_of(s), g))
    bm = pl.BlockSpec((rows, N_STATE), lambda g, s: (s_of(s), D_INNER // N_STATE + g))
    cm = pl.BlockSpec((rows, N_STATE), lambda g, s: (s_of(s), D_INNER // N_STATE + N_GROUPS + g))
    dt = pl.BlockSpec((rows, 128), lambda g, s: (s_of(s), g))
    par = pl.BlockSpec((1, 128), lambda g, s: (0, g))
    st = pl.BlockSpec((None, None, _XS_GW, N_STATE), lambda g, s: (g, s_of(s), 0, 0))
    return xs, bm, cm, dt, par, st


def _ssd_fwd(act, dtr, dtb, alog, dsk, *, nrows):
    span = CHUNK * min(SSD_SPAN, nrows // CHUNK)
    per_step = SSD_FWD_SPANS if nrows % (SSD_FWD_SPANS * span) == 0 else 1
    rows = per_step * span
    nsteps = nrows // rows
    xs, bm, cm, dt, par, _ = _ssd_specs(False, nsteps, rows)
    st = pl.BlockSpec((None, per_step, _XS_GW, N_STATE), lambda g, s: (g, s, 0, 0))

    def body(xs_ref, b_ref, c_ref, dt_ref, dtb_ref, al_ref, dk_ref, y_ref, st_ref, h_ref):
        @pl.when(pl.program_id(1) == 0)
        def _():
            h_ref[...] = jnp.zeros_like(h_ref)

        h = h_ref[...]
        for i in range(per_step):
            r = slice(i * span, (i + 1) * span)
            st_ref[i] = h
            y, h = _ssd_span(xs_ref[r, :].astype(F32), b_ref[r, :].astype(F32), c_ref[r, :].astype(F32), dt_ref[r, :],
                             h, dtb_ref[...], al_ref[...], dk_ref[...])
            y_ref[r, :] = y.astype(y_ref.dtype)
        h_ref[...] = h

    return pl.pallas_call(
        body, name="ssd_fwd", grid=(N_GROUPS, nsteps),
        in_specs=[xs, bm, cm, dt, par, par, par], out_specs=[xs, st],
        out_shape=[jax.ShapeDtypeStruct((nrows, D_INNER), ACT_DTYPE),
                   jax.ShapeDtypeStruct((N_GROUPS, nrows // span, _XS_GW, N_STATE), F32)],
        scratch_shapes=[pltpu.VMEM((_XS_GW, N_STATE), F32)],
        compiler_params=pltpu.CompilerParams(dimension_semantics=("arbitrary", "arbitrary")),
    )(act, act, act, dtr, dtb, alog, dsk)


def _ssd_bwd(act, dtr, dtb, alog, dsk, states, dy, *, nrows):
    rows = CHUNK * min(SSD_SPAN, nrows // CHUNK)
    nsteps = nrows // rows
    xs, bm, cm, dt, par, st = _ssd_specs(True, nsteps, rows)

    def body(xs_ref, b_ref, c_ref, dt_ref, dtb_ref, al_ref, dk_ref, st_ref, dy_ref,
             dxs_ref, db_ref, dc_ref, ddt_ref, ddtb_ref, dal_ref, ddk_ref, dh_ref):
        @pl.when(pl.program_id(1) == 0)
        def _():
            dh_ref[...] = jnp.zeros_like(dh_ref)
            ddtb_ref[...] = jnp.zeros_like(ddtb_ref)
            dal_ref[...] = jnp.zeros_like(dal_ref)
            ddk_ref[...] = jnp.zeros_like(ddk_ref)

        _, vjp = jax.vjp(_ssd_span, xs_ref[...].astype(F32), b_ref[...].astype(F32), c_ref[...].astype(F32),
                         dt_ref[...], st_ref[...], dtb_ref[...], al_ref[...], dk_ref[...])
        dxs, db, dc, ddt, dh, ddtb, dal, ddk = vjp((dy_ref[...].astype(F32), dh_ref[...]))
        dxs_ref[...] = dxs.astype(dxs_ref.dtype)
        db_ref[...] = db.astype(db_ref.dtype)
        dc_ref[...] = dc.astype(dc_ref.dtype)
        ddt_ref[...] = ddt
        dh_ref[...] = dh
        ddtb_ref[...] += ddtb
        dal_ref[...] += dal
        ddk_ref[...] += ddk

    nb = N_GROUPS * N_STATE
    bspec = pl.BlockSpec((rows, N_STATE), lambda g, s: (nsteps - 1 - s, g))
    return pl.pallas_call(
        body, name="ssd_bwd", grid=(N_GROUPS, nsteps),
        in_specs=[xs, bm, cm, dt, par, par, par, st, xs],
        out_specs=[xs, bspec, bspec, dt, par, par, par],
        out_shape=[jax.ShapeDtypeStruct((nrows, D_INNER), ACT_DTYPE), jax.ShapeDtypeStruct((nrows, nb), ACT_DTYPE),
                   jax.ShapeDtypeStruct((nrows, nb), ACT_DTYPE), jax.ShapeDtypeStruct((nrows, DT_W), F32),
                   jax.ShapeDtypeStruct((1, DT_W), F32), jax.ShapeDtypeStruct((1, DT_W), F32),
                   jax.ShapeDtypeStruct((1, DT_W), F32)],
        scratch_shapes=[pltpu.VMEM((_XS_GW, N_STATE), F32)],
        compiler_params=pltpu.CompilerParams(dimension_semantics=("arbitrary", "arbitrary")),
    )(act, act, act, dtr, dtb, alog, dsk, states, dy)


def _add_epilogue(acc, r):
    return r + acc


def _rms_and_skip(x, g):
    return _rms(x, g), x


def _forward_backward(x, p, tgt, w, late_weights=None, after=()):
    s = x.shape[0]
    act_t, f32 = ACT_DTYPE, F32
    mm = functools.partial(_matmul)
    h, proj, dtr = _rows_matmul("proj", _f_rms, [(x, D_MODEL, 0)], [w["norm_g"]], w["w_main"], out_dtype=act_t,
                                n=MAIN_W, k=D_MODEL, tm=1024, tn=2 * MM_TILE, nrows=s, after=after, side=(w["w_dt"], f32))
    act, conv_pre = _conv_fwd(proj, w["conv_w"], w["conv_b"], tm=512, nrows=s)
    y, states = _ssd_fwd(act, dtr, w["dt_bias"], w["a_log"], w["d_skip"], nrows=s)
    if late_weights is not None:
        w = {**w, **late_weights(states)}
    a_pars = [w["ln_a_g"], w["ln_a_b"], w["w_s"], w["b_s"]]
    y_a, o_a = _rows_matmul("out_a", _f_branch_a, [(proj, UVZ_W, UVZ_CB)], a_pars, w["w_oa"], out_dtype=act_t,
                            n=D_MODEL, k=E_A, tm=512, nrows=s)
    gn_rows = [(y, D_INNER, 0), (proj, ZB_W, ZB_CB)]
    y_b, o_b = _rows_matmul("out_b", _f_gnorm, gn_rows, [w["ssm_norm_g"]], w["w_ob"], out_dtype=act_t,
                            n=D_MODEL, k=D_INNER, tm=1024, nrows=s)
    mg_rows = [(proj, G_W, G_CB), (o_a, D_MODEL, 0), (o_b, D_MODEL, 0)]
    merged, x1 = _rows_matmul("out_proj", _f_merge, mg_rows, [], w["w_out"], out_dtype=f32, n=D_MODEL, k=D_MODEL,
                              tm=1024, nrows=s, extras=(x,), epilogue=_add_epilogue)
    g = {}
    hp, dx2, dgp, dpe, g["final_g"], loss = _head(x1, p, tgt, w["ple_norm_g"], w["w_pg"], w["w_ple"], w["final_g"],
                                                   tm=512, nrows=s)
    g["w_pg"] = mm(hp, dgp, mode="tn", name="d_w_pg", out_dtype=f32, m=D_MODEL, n=D_MODEL, k=s, tn=MM_TILE // 2)
    g["w_ple"] = mm(p, dpe, mode="tn", name="d_w_ple", out_dtype=f32, m=PLE_DIM, n=D_MODEL, k=s)
    dx1, g["ple_norm_g"] = _rows_vjp_call(
        "ple_norm_bwd", _rms_and_skip, [(x1, D_MODEL, 0)], [w["ple_norm_g"]], [(dx2, D_MODEL, 0)],
        [(f32, None)], tm=1024, nrows=s, cot_mm=(dgp, w["w_pg"], None))
    g["w_out"] = mm(merged, dx1, mode="tn", name="d_w_out", out_dtype=f32, m=D_MODEL, n=D_MODEL, k=s)
    dproj, do_a, do_b = _rows_vjp_call(
        "merge_bwd", _f_merge, mg_rows, [], [],
        [(act_t, (None, MAIN_W, G_CB)), (act_t, None), (act_t, None)], tm=1024, nrows=s, cot_mm=(dx1, w["w_out"], None))
    g["w_oa"] = mm(y_a, do_a, mode="tn", name="d_w_oa", out_dtype=f32, m=E_A, n=D_MODEL, k=s, tn=MM_TILE // 2)
    g["w_ob"] = mm(y_b, do_b, mode="tn", name="d_w_ob", out_dtype=f32, m=D_INNER, n=D_MODEL, k=s)
    dy, dproj, g["ssm_norm_g"] = _rows_vjp_call(
        "gnorm_bwd", _f_gnorm, gn_rows, [w["ssm_norm_g"]], [],
        [(act_t, None), (act_t, (dproj, MAIN_W, ZB_CB))], tm=512, nrows=s, cot_mm=(do_b, w["w_ob"], None))
    dxs, dbm, dcm, ddtr, g["dt_bias"], g["a_log"], g["d_skip"] = _ssd_bwd(
        act, dtr, w["dt_bias"], w["a_log"], w["d_skip"], states, dy, nrows=s)
    dpre, g["conv_b"] = _conv_bwd_act(conv_pre, (dxs, dbm, dcm), tm=1024, nrows=s)
    dproj, g["conv_w"] = _conv_bwd_x(dpre, proj, w["conv_w"], dproj, tm=512, nrows=s)
    dproj, g["ln_a_g"], g["ln_a_b"], g["w_s"], g["b_s"] = _rows_vjp_call(
        "branch_a_bwd", _f_branch_a, [(proj, UVZ_W, UVZ_CB)], a_pars, [],
        [(act_t, (dproj, MAIN_W, UVZ_CB))], tm=512, nrows=s, cot_mm=(do_a, w["w_oa"], None))
    g["w_main"] = mm(h, dproj, mode="tn", name="d_w_main", out_dtype=f32, m=D_MODEL, n=MAIN_W, k=s, tn=MM_TILE // 2)
    g["w_dt"] = mm(h, ddtr, mode="tn", name="d_w_dt", out_dtype=f32, m=D_MODEL, n=DT_W, k=s)
    return loss, g, (dproj, ddtr, dx1)


def _input_grad(x, w, ctx, after=()):
    dproj, ddtr, dx1 = ctx
    s = x.shape[0]
    dh = _matmul(dproj, w["w_main"], mode="nt", name="d_h_main", out_dtype=F32, m=s, n=D_MODEL, k=MAIN_W,
                 tm=MM_TILE // 2, after=after)
    return _rows_vjp_call(
        "pre_norm_bwd", _rms_and_skip, [(x, D_MODEL, 0)], [w["norm_g"]], [(dx1, D_MODEL, 0)],
        [(F32, None)], tm=1024, nrows=s, cot_mm=(ddtr, w["w_dt"], dh))


def _local_step(x, p, tgt, w):
    loss, g, ctx = _forward_backward(x, p, tgt, w)
    grad_x, g["norm_g"] = _input_grad(x, w, ctx)
    return loss, grad_x, g


_O_ZB = 3 * E_A
_O_XBC = _O_ZB + D_INNER
_O_DT = _O_XBC + CONV_DIM
_O_G = _O_DT + N_HEADS


def _heads_to_lanes(v):
    r = v.shape[0]
    v = v.reshape(r, N_GROUPS, HEADS_PER_GROUP)
    return jnp.pad(v, ((0, 0), (0, 0), (0, 128 - HEADS_PER_GROUP))).reshape(r, DT_W)


def _lanes_to_heads(v):
    r = v.shape[0]
    return v.reshape(r, N_GROUPS, 128)[:, :, :HEADS_PER_GROUP].reshape(r, N_HEADS)


def _block_cols(blocks, a, b):
    parts = []
    for k in range(N_CHIPS):
        lo, hi = max(a, k * W_IN_BLOCK), min(b, (k + 1) * W_IN_BLOCK)
        if lo < hi:
            parts.append(blocks[k][:, lo - k * W_IN_BLOCK:hi - k * W_IN_BLOCK])
    return parts


_W_IN_SEGMENTS = ((0, _O_ZB, "m", 0), (_O_ZB, _O_XBC, "m", UVZ_W + XBC_W), (_O_XBC, _O_DT, "m", UVZ_W),
                  (_O_DT, _O_G, "d", 0), (_O_G, N_IN, "m", MAIN_W - G_W))


def _w_in_grad_blocks(gm, gdt):
    blocks = []
    for k in range(N_CHIPS):
        a, b = k * W_IN_BLOCK, (k + 1) * W_IN_BLOCK
        parts = []
        for s, e, src, off in _W_IN_SEGMENTS:
            lo, hi = max(a, s), min(b, e)
            if lo < hi:
                parts.append((gm if src == "m" else gdt)[:, off + lo - s:off + hi - s])
        blocks.append(jnp.concatenate(parts, axis=1))
    return jnp.stack(blocks)


def _layout_weights(f, w_in_blocks=None):
    w = dict(f)
    if w_in_blocks is None:
        w_in = w.pop("w_in")
        w_in_blocks = jnp.stack([w_in[:, k * W_IN_BLOCK:(k + 1) * W_IN_BLOCK] for k in range(N_CHIPS)])
    cols = functools.partial(_block_cols, w_in_blocks)
    w["w_main"] = jnp.concatenate(cols(0, _O_ZB) + cols(_O_XBC, _O_DT) + cols(_O_ZB, _O_XBC) + cols(_O_G, N_IN), axis=1)
    w["w_dt"] = _heads_to_lanes(jnp.concatenate(cols(_O_DT, _O_G), axis=1))
    w["b_s"] = f["b_s"].reshape(G_A, CHUNK, 1)
    for n in ("dt_bias", "a_log", "d_skip"):
        w[n] = _heads_to_lanes(f[n])
    return w


def _natural_grads(g):
    out = dict(g)
    gm = out.pop("w_main")
    gdt = _lanes_to_heads(out.pop("w_dt"))
    out["w_in"] = jnp.concatenate(
        [gm[:, :UVZ_W], gm[:, UVZ_W + XBC_W:UVZ_W + XBC_W + ZB_W], gm[:, UVZ_W:UVZ_W + XBC_W], gdt, gm[:, MAIN_W - G_W:]],
        axis=1)
    out["b_s"] = g["b_s"].reshape(G_A, CHUNK)
    for n in ("dt_bias", "a_log", "d_skip"):
        out[n] = _lanes_to_heads(g[n])
    return out


def _place():
    return lax.axis_index("x"), lax.axis_index("y"), lax.axis_index("c")


def _other_chips(x, y):
    return [(1 - x, y), (x, 1 - y), (1 - x, 1 - y)]


def _rcopy(src, dst, ssem, rsem, dev):
    return pltpu.make_async_remote_copy(src_ref=src, dst_ref=dst, send_sem=ssem, recv_sem=rsem,
                                        device_id=dev, device_id_type=MESH)


def _half(ref_rows, half):
    hs = ref_rows // 2
    return pl.ds(pl.multiple_of(half * hs, 16), hs)


GATHER_CHUNKS = 8


def _gather_weights(shards, conv_shard):
    nw, nq = len(shards), GATHER_CHUNKS

    def body(*refs):
        sh, cv = refs[:nw], refs[nw]
        out, cvo = refs[nw + 1:2 * nw + 1], refs[2 * nw + 1]
        ici_s, ici_r, fw_s, fw_r, own_s, own_r, cv_s, cv_r = refs[2 * nw + 2:]
        x, y, c = _place()
        me, sib, chips = 2 * x + y, (x, y, 1 - c), _other_chips(x, y)
        own = [_rcopy(sh[w], out[w].at[me], own_s.at[w], own_r.at[w], sib) for w in range(nw)]
        own.append(_rcopy(cv, cvo.at[me], own_s.at[nw], own_r.at[nw], sib))
        for cp in own:
            cp.start()
        pieces = [(w, q, j) for w in range(nw) for q in range(nq) for j in range(3)]

        def rows(w, half, q):
            hs = sh[w].shape[0] // 2
            return pl.ds(pl.multiple_of(half * hs + q * (hs // nq), 16), hs // nq)

        def sem(w, q, j):
            return (3 * w + j) * nq + q

        def landed(w, q, j, half):
            return out[w].at[2 * chips[j][0] + chips[j][1], rows(w, half, q)]

        sends = [_rcopy(sh[w].at[rows(w, c, q)], out[w].at[me, rows(w, c, q)], ici_s.at[sem(w, q, j)],
                        ici_r.at[sem(w, q, j)], (*chips[j], c)) for w, q, j in pieces if j < 2]
        for j, chip in enumerate(chips):
            sends.append(_rcopy(cv, cvo.at[me], cv_s.at[j], cv_r.at[j], (*chip, c)))
        for cp in sends:
            cp.start()

        def arrived(w, q, j):
            slab = landed(w, q, j, c)
            _rcopy(slab, slab, ici_s.at[sem(w, q, j)], ici_r.at[sem(w, q, j)], (*chips[j], c)).wait_recv()
            fwd = _rcopy(slab, slab, fw_s.at[sem(w, q, j)], fw_r.at[sem(w, q, j)], sib)
            fwd.start()
            sends.append(fwd)
            if j < 2 and (q % 2 == 0) == (j == 0):
                hop = _rcopy(slab, slab, ici_s.at[sem(w, q, 2)], ici_r.at[sem(w, q, 2)], (*chips[1 - j], c))
                hop.start()
                sends.append(hop)

        for w, q, j in pieces:
            if j < 2:
                arrived(w, q, j)
        for w, q, j in pieces:
            if j == 2:
                arrived(w, q, j)
        for j, chip in enumerate(chips):
            blk = cvo.at[2 * chip[0] + chip[1]]
            _rcopy(blk, blk, cv_s.at[j], cv_r.at[j], (*chip, c)).wait_recv()
        for w, q, j in pieces:
            slab = landed(w, q, j, 1 - c)
            _rcopy(slab, slab, fw_s.at[sem(w, q, j)], fw_r.at[sem(w, q, j)], sib).wait_recv()
        for cp in sends:
            cp.wait_send()
        for cp in own:
            cp.wait()

    dma = pltpu.SemaphoreType.DMA
    n_ici = 3 * nw * nq
    return pl.pallas_call(
        body, name="gather_weights",
        in_specs=[ANY] * (nw + 1), out_specs=[ANY] * (nw + 1),
        out_shape=[jax.ShapeDtypeStruct((N_CHIPS,) + s.shape, s.dtype) for s in shards]
        + [jax.ShapeDtypeStruct((N_CHIPS,) + conv_shard.shape, conv_shard.dtype)],
        scratch_shapes=[dma((n_ici,)), dma((n_ici,)), dma((n_ici,)), dma((n_ici,)), dma((nw + 1,)), dma((nw + 1,)),
                        dma((3,)), dma((3,))],
    )(*shards, conv_shard)


_HBM = pl.BlockSpec(memory_space=pltpu.HBM)
_SEM = pl.BlockSpec(memory_space=pltpu.SEMAPHORE)
_EFFECT = pltpu.SideEffectType.DATAFLOW_SIDE_EFFECTING


def _late_gather_copies(sh, out, s_sem, r_sem):
    x, y, c = _place()
    to = [(*chip, c) for chip in _other_chips(x, y)] + [(x, y, 1 - c)]
    return [_rcopy(sh[w], out[w].at[2 * x + y], s_sem.at[4 * w + j], r_sem.at[4 * w + j], dev)
            for w in range(len(sh)) for j, dev in enumerate(to)]


def _late_gather_start(shards):
    n = len(shards)
    lands = [lax.empty((N_CHIPS,) + a.shape, a.dtype) for a in shards]

    def body(*refs):
        for cp in _late_gather_copies(refs[:n], refs[n:2 * n], refs[2 * n], refs[2 * n + 1]):
            cp.start()
        refs[-1][...] = jnp.zeros_like(refs[-1])

    dma = pltpu.SemaphoreType.DMA
    hbm = [pltpu.with_memory_space_constraint(a, pltpu.HBM) for a in list(shards) + lands]
    out = pl.pallas_call(
        body, name="late_gather_start",
        out_shape=[dma((4 * n,)), dma((4 * n,))] + [pltpu.HBM(a.shape, a.dtype) for a in hbm]
        + [jax.ShapeDtypeStruct((8, 128), F32)],
        in_specs=[_HBM] * (2 * n), out_specs=[_SEM, _SEM] + [_HBM] * (2 * n) + [pl.BlockSpec(memory_space=pltpu.VMEM)],
        input_output_aliases={i: 2 + i for i in range(2 * n)},
        compiler_params=pltpu.CompilerParams(has_side_effects=_EFFECT),
    )(*hbm)
    return out[0], out[1], out[2:2 + n], out[2 + n:2 + 2 * n], out[-1]


def _late_gather_wait(s_sem, r_sem, srcs, lands, after):
    n = len(srcs)

    def body(*refs):
        for cp in _late_gather_copies(refs[:n], refs[n:2 * n], refs[2 * n], refs[2 * n + 1]):
            cp.wait_send()
            cp.wait_recv()

    out = pl.pallas_call(
        body, name="late_gather_wait",
        out_shape=[pltpu.HBM(a.shape, a.dtype) for a in list(srcs) + list(lands)],
        in_specs=[_HBM] * (2 * n) + [_SEM, _SEM, ANY], out_specs=[_HBM] * (2 * n),
        input_output_aliases={i: i for i in range(2 * n)},
        compiler_params=pltpu.CompilerParams(has_side_effects=_EFFECT),
    )(*srcs, *lands, s_sem, r_sem, after)
    return out[n:]


def _swap_with_sibling(arrs):
    n = len(arrs)

    def body(*refs):
        src, dst, s_sem, r_sem = refs[:n], refs[n:2 * n], refs[2 * n], refs[2 * n + 1]
        x, y, c = _place()
        cps = [_rcopy(src[i], dst[i], s_sem.at[i], r_sem.at[i], (x, y, 1 - c)) for i in range(n)]
        for cp in cps:
            cp.start()
        for cp in cps:
            cp.wait()

    dma = pltpu.SemaphoreType.DMA
    return pl.pallas_call(
        body, name="swap_with_sibling", in_specs=[ANY] * n, out_specs=[ANY] * n,
        out_shape=[jax.ShapeDtypeStruct(a.shape, a.dtype) for a in arrs], scratch_shapes=[dma((n,)), dma((n,))],
    )(*arrs)


def _scatter_copies(src, land, s_sem, r_sem):
    x, y, c = _place()
    return [_rcopy(src[i].at[2 * chip[0] + chip[1]], land[i].at[j], s_sem.at[3 * i + j], r_sem.at[3 * i + j], (*chip, c))
            for i in range(len(src)) for j, chip in enumerate(_other_chips(x, y))]


def _scatter_blocks_start(arrs):
    n = len(arrs)
    lands = [lax.empty((3,) + a.shape[1:], a.dtype) for a in arrs]

    def body(*refs):
        src, land, s_sem, r_sem, token = refs[:n], refs[n:2 * n], refs[2 * n], refs[2 * n + 1], refs[-1]
        for cp in _scatter_copies(src, land, s_sem, r_sem):
            cp.start()
        token[...] = jnp.zeros_like(token)

    dma = pltpu.SemaphoreType.DMA
    hbm = [pltpu.with_memory_space_constraint(a, pltpu.HBM) for a in list(arrs) + lands]
    out = pl.pallas_call(
        body, name="scatter_blocks_start",
        out_shape=[dma((3 * n,)), dma((3 * n,))] + [pltpu.HBM(a.shape, a.dtype) for a in hbm]
        + [jax.ShapeDtypeStruct((8, 128), F32)],
        in_specs=[_HBM] * (2 * n), out_specs=[_SEM, _SEM] + [_HBM] * (2 * n) + [pl.BlockSpec(memory_space=pltpu.VMEM)],
        input_output_aliases={i: 2 + i for i in range(2 * n)},
        compiler_params=pltpu.CompilerParams(has_side_effects=_EFFECT),
    )(*hbm)
    return out[0], out[1], out[2:2 + n], out[2 + n:2 + 2 * n], out[-1]


def _scatter_blocks_wait(s_sem, r_sem, srcs, lands, after):
    n = len(srcs)

    def body(*refs):
        src, land, s_sem, r_sem = refs[:n], refs[n:2 * n], refs[2 * n], refs[2 * n + 1]
        for cp in _scatter_copies(src, land, s_sem, r_sem):
            cp.wait_send()
            cp.wait_recv()

    out = pl.pallas_call(
        body, name="scatter_blocks_wait",
        out_shape=[pltpu.HBM(a.shape, a.dtype) for a in list(srcs) + list(lands)],
        in_specs=[_HBM] * (2 * n) + [_SEM, _SEM] + [ANY] * len(after), out_specs=[_HBM] * (2 * n),
        input_output_aliases={i: i for i in range(2 * n)},
        compiler_params=pltpu.CompilerParams(has_side_effects=_EFFECT),
    )(*srcs, *lands, s_sem, r_sem, *after)
    return out[:n], out[n:]


def _share_halves(arrs):
    n = len(arrs)

    def body(*refs):
        buf, s_sem, r_sem = refs[n:2 * n], refs[2 * n], refs[2 * n + 1]
        x, y, c = _place()
        cps = []
        for i in range(n):
            mine = buf[i].at[_half(buf[i].shape[0], c)]
            cps.append(_rcopy(mine, mine, s_sem.at[i], r_sem.at[i], (x, y, 1 - c)))
        for cp in cps:
            cp.start()
        for i in range(n):
            theirs = buf[i].at[_half(buf[i].shape[0], 1 - c)]
            _rcopy(theirs, theirs, s_sem.at[i], r_sem.at[i], (x, y, 1 - c)).wait_recv()
        for cp in cps:
            cp.wait_send()

    dma = pltpu.SemaphoreType.DMA
    return pl.pallas_call(
        body, name="share_halves", in_specs=[ANY] * n, out_specs=[ANY] * n,
        out_shape=[jax.ShapeDtypeStruct(a.shape, a.dtype) for a in arrs],
        input_output_aliases={i: i for i in range(n)}, scratch_shapes=[dma((n,)), dma((n,))],
    )(*arrs)


def _small_gather_copies(src, land, s_sem, r_sem):
    x, y, c = _place()
    cps = []
    for d in range(1, N_DEV):
        peer = ((1 - x) if d & 4 else x), ((1 - y) if d & 2 else y), ((1 - c) if d & 1 else c)
        cps.append(_rcopy(src, land.at[4 * x + 2 * y + c], s_sem.at[d - 1], r_sem.at[d - 1], peer))
    return cps


def _small_gather_start(packed):
    def body(src, land, s_sem, r_sem, _, __, token):
        for cp in _small_gather_copies(src, land, s_sem, r_sem):
            cp.start()
        token[...] = jnp.zeros_like(token)

    dma = pltpu.SemaphoreType.DMA
    hbm = [pltpu.with_memory_space_constraint(a, pltpu.HBM) for a in (packed, lax.empty((N_DEV,) + packed.shape, F32))]
    return pl.pallas_call(
        body, name="small_gather_start",
        out_shape=[dma((N_DEV - 1,)), dma((N_DEV - 1,))] + [pltpu.HBM(a.shape, a.dtype) for a in hbm]
        + [jax.ShapeDtypeStruct((8, 128), F32)],
        in_specs=[_HBM] * 2, out_specs=[_SEM, _SEM, _HBM, _HBM, pl.BlockSpec(memory_space=pltpu.VMEM)],
        input_output_aliases={0: 2, 1: 3}, compiler_params=pltpu.CompilerParams(has_side_effects=_EFFECT),
    )(*hbm)


def _small_gather_wait(s_sem, r_sem, src, land, after):
    def body(src, land, s_sem, r_sem, *_):
        for cp in _small_gather_copies(src, land, s_sem, r_sem):
            cp.wait_send()
            cp.wait_recv()

    return pl.pallas_call(
        body, name="small_gather_wait", out_shape=[pltpu.HBM(src.shape, src.dtype), pltpu.HBM(land.shape, land.dtype)],
        in_specs=[_HBM, _HBM, _SEM, _SEM, ANY], out_specs=[_HBM, _HBM], input_output_aliases={0: 0, 1: 1},
        compiler_params=pltpu.CompilerParams(has_side_effects=_EFFECT),
    )(src, land, s_sem, r_sem, after)


def _small_sum(own, land, dev_arr):
    def body(me_ref, own_ref, land_ref, o_ref):
        acc = jnp.zeros(o_ref.shape, F32)
        for d in range(N_DEV):
            acc = acc + jnp.where(me_ref[0] == d, own_ref[...], land_ref[d])
        o_ref[...] = acc

    return pl.pallas_call(
        body, name="small_sum", out_shape=jax.ShapeDtypeStruct(own.shape, F32),
        grid_spec=pltpu.PrefetchScalarGridSpec(
            num_scalar_prefetch=1, grid=(1,),
            in_specs=[pl.BlockSpec(own.shape, lambda i, m: (0, 0)), pl.BlockSpec(land.shape, lambda i, m: (0, 0, 0))],
            out_specs=pl.BlockSpec(own.shape, lambda i, m: (0, 0))),
    )(dev_arr, own, land)


def _row_tile(rows, cols):
    tr = max(8, min(rows, (1 << 20) // (4 * cols) // 8 * 8))
    while rows % tr:
        tr -= 8
    return tr


def _chip_sum(name, g5, recv, c_arr):
    nb, _, hs, cols = g5.shape
    tr = _row_tile(hs, cols)

    def body(_, a_ref, b_ref, o_ref):
        o_ref[...] = (a_ref[...] + b_ref[...].astype(F32)).astype(o_ref.dtype)

    blk = pl.BlockSpec((None, tr, cols), lambda b, i, c: (b, i, 0))
    return pl.pallas_call(
        body, name=name,
        grid_spec=pltpu.PrefetchScalarGridSpec(
            num_scalar_prefetch=1, grid=(nb, hs // tr),
            in_specs=[pl.BlockSpec((None, None, tr, cols), lambda b, i, c: (b, c[0], i, 0)), blk], out_specs=blk),
        out_shape=jax.ShapeDtypeStruct((nb, hs, cols), WIRE_DTYPE),
    )(c_arr, g5, recv)


def _final_sum(name, own, recv, place_arr):
    _, hs, cols = own.shape
    tr = _row_tile(hs, cols)
    nt = hs // tr

    def body(_, a_ref, r_ref, o_ref):
        o_ref[...] = ((a_ref[...].astype(F32) + r_ref[0].astype(F32)) + r_ref[1].astype(F32)) + r_ref[2].astype(F32)

    return pl.pallas_call(
        body, name=name,
        grid_spec=pltpu.PrefetchScalarGridSpec(
            num_scalar_prefetch=1, grid=(nt,),
            in_specs=[pl.BlockSpec((None, tr, cols), lambda i, m: (m[0], i, 0)),
                      pl.BlockSpec((3, tr, cols), lambda i, m: (0, i, 0))],
            out_specs=pl.BlockSpec((tr, cols), lambda i, m: (m[1] * nt + i, 0))),
        out_shape=jax.ShapeDtypeStruct((2 * hs, cols), F32),
    )(place_arr, own, recv)


def _adamw(w, g, m, v):
    m = ADAM_B1 * m + (1.0 - ADAM_B1) * g
    v = ADAM_B2 * v + (1.0 - ADAM_B2) * (g * g)
    m_hat = m / (1.0 - ADAM_B1 ** ADAM_STEP)
    v_hat = v / (1.0 - ADAM_B2 ** ADAM_STEP)
    return -ADAM_LR * (m_hat / (jnp.sqrt(v_hat) + ADAM_EPS) + ADAM_WD * w), m, v


def _adamw_call(name, w, g, m, v):
    rows, cols = w.shape
    tr = _row_tile(rows, cols)
    if 4 * tr * cols >= (1 << 18):
        blk, steps = pl.BlockSpec((tr, cols), lambda i: (i, 0)), rows // tr
    else:
        blk, steps = pl.BlockSpec((rows, 128), lambda i: (0, i)), cols // 128

    def body(w_ref, g_ref, m_ref, v_ref, d_ref, nm_ref, nv_ref, go_ref):
        g = g_ref[...]
        d_ref[...], nm_ref[...], nv_ref[...] = _adamw(w_ref[...], g, m_ref[...], v_ref[...])
        go_ref[...] = g

    return pl.pallas_call(
        body, name=name, grid=(steps,), in_specs=[blk] * 4, out_specs=[blk] * 4,
        out_shape=[jax.ShapeDtypeStruct(w.shape, F32)] * 4,
        compiler_params=pltpu.CompilerParams(dimension_semantics=("parallel",)),
    )(w, g, m, v)


def _adamw_small(ws, gs, ms, vs):
    n = len(ws)

    def body(*refs):
        for i in range(n):
            w_ref, g_ref, m_ref, v_ref = (refs[k * n + i] for k in range(4))
            d, nm, nv = _adamw(w_ref[...], g_ref[...], m_ref[...], v_ref[...])
            refs[4 * n + i][...] = d
            refs[5 * n + i][...] = nm
            refs[6 * n + i][...] = nv

    out = pl.pallas_call(
        body, name="adamw_small", out_shape=[jax.ShapeDtypeStruct(a.shape, F32) for a in ws] * 3,
    )(*ws, *gs, *ms, *vs)
    return out[:n], out[n:2 * n], out[2 * n:]


_BIG = ("w_in", "w_oa", "w_ob", "w_out", "w_pg", "w_ple")
_SMALL = ("norm_g", "ln_a_g", "ln_a_b", "w_s", "b_s", "conv_w", "conv_b", "dt_bias", "a_log", "d_skip", "ssm_norm_g",
          "ple_norm_g", "final_g")
_WEIGHTS = ("norm_g", "w_in", "ln_a_g", "ln_a_b", "w_s", "b_s", "conv_w", "conv_b", "dt_bias", "a_log", "d_skip",
            "ssm_norm_g", "w_oa", "w_ob", "w_out", "ple_norm_g", "w_pg", "w_ple", "final_g")
_COL_SHARDED = ("w_in", "w_ple")
_PACK = 1024


def _blocks_to_full(col_sharded, blocks):
    if col_sharded:
        return jnp.concatenate([blocks[k] for k in range(N_CHIPS)], axis=1)
    return blocks.reshape(N_CHIPS * blocks.shape[1], blocks.shape[2])


def _full_to_blocks(col_sharded, full):
    if col_sharded:
        w = full.shape[1] // N_CHIPS
        return jnp.stack([full[:, k * w:(k + 1) * w] for k in range(N_CHIPS)])
    return full.reshape(N_CHIPS, full.shape[0] // N_CHIPS, full.shape[1])


def _two_d(n, a):
    if n == "w_s":
        return a.reshape(G_A * CHUNK, CHUNK)
    if n in ("b_s", "conv_w"):
        return a.reshape(a.shape[-2], a.shape[-1])
    return a.reshape(1, a.shape[-1])


def kernel(x, p, norm_g, w_in, ln_a_g, ln_a_b, w_s, b_s, conv_w, conv_b, dt_bias, a_log, d_skip, ssm_norm_g, w_oa, w_ob, w_out, ple_norm_g, w_pg, w_ple, final_g, loss_target, m_norm_g, m_w_in, m_ln_a_g, m_ln_a_b, m_w_s, m_b_s, m_conv_w, m_conv_b, m_dt_bias, m_a_log, m_d_skip, m_ssm_norm_g, m_w_oa, m_w_ob, m_w_out, m_ple_norm_g, m_w_pg, m_w_ple, m_final_g, v_norm_g, v_w_in, v_ln_a_g, v_ln_a_b, v_w_s, v_b_s, v_conv_w, v_conv_b, v_dt_bias, v_a_log, v_d_skip, v_ssm_norm_g, v_w_oa, v_w_ob, v_w_out, v_ple_norm_g, v_w_pg, v_w_ple, v_final_g):
    wt = dict(norm_g=norm_g, w_in=w_in, ln_a_g=ln_a_g, ln_a_b=ln_a_b, w_s=w_s, b_s=b_s, conv_w=conv_w, conv_b=conv_b,
              dt_bias=dt_bias, a_log=a_log, d_skip=d_skip, ssm_norm_g=ssm_norm_g, w_oa=w_oa, w_ob=w_ob, w_out=w_out,
              ple_norm_g=ple_norm_g, w_pg=w_pg, w_ple=w_ple, final_g=final_g)
    mom = dict(norm_g=m_norm_g, w_in=m_w_in, ln_a_g=m_ln_a_g, ln_a_b=m_ln_a_b, w_s=m_w_s, b_s=m_b_s, conv_w=m_conv_w,
               conv_b=m_conv_b, dt_bias=m_dt_bias, a_log=m_a_log, d_skip=m_d_skip, ssm_norm_g=m_ssm_norm_g, w_oa=m_w_oa,
               w_ob=m_w_ob, w_out=m_w_out, ple_norm_g=m_ple_norm_g, w_pg=m_w_pg, w_ple=m_w_ple, final_g=m_final_g)
    vel = dict(norm_g=v_norm_g, w_in=v_w_in, ln_a_g=v_ln_a_g, ln_a_b=v_ln_a_b, w_s=v_w_s, b_s=v_b_s, conv_w=v_conv_w,
               conv_b=v_conv_b, dt_bias=v_dt_bias, a_log=v_a_log, d_skip=v_d_skip, ssm_norm_g=v_ssm_norm_g, w_oa=v_w_oa,
               w_ob=v_w_ob, w_out=v_w_out, ple_norm_g=v_ple_norm_g, w_pg=v_w_pg, w_ple=v_w_ple, final_g=v_final_g)
    xi, yi, ci = _place()
    me = 2 * xi + yi
    c_arr = jnp.reshape(ci, (1,)).astype(jnp.int32)
    place_arr = jnp.stack([me, ci]).astype(jnp.int32)

    shard = {n: wt[n][0] for n in _BIG}
    wire = {n: shard[n].astype(WIRE_DTYPE) for n in _BIG}
    w_in_blocks, conv_blocks = _gather_weights([wire["w_in"]], conv_w[0])
    g_ssem, g_rsem, g_sent, g_lands, g_token = _late_gather_start([wire[n] for n in _BIG[1:]])
    full = {"conv_w": _blocks_to_full(True, conv_blocks)}
    for n in _SMALL:
        if n != "conv_w":
            full[n] = wt[n][0] if wt[n].ndim > 2 else wt[n].reshape(1, wt[n].shape[-1])

    def late_weights(after):
        blocks = _late_gather_wait(g_ssem, g_rsem, g_sent, g_lands, after)
        return {n: _blocks_to_full(n in _COL_SHARDED, b) for n, b in zip(_BIG[1:], blocks)}

    w = _layout_weights(full, w_in_blocks=w_in_blocks)
    loss_row, g, ctx = _forward_backward(x[0], p[0, 0], loss_target[0], w, late_weights, after=(g_token,))
    loss = lax.psum(loss_row[0, 0], ("x", "y", "c"))

    parts = {n: _full_to_blocks(n in _COL_SHARDED, g[n]) for n in _BIG[1:]}
    parts["w_main"] = g["w_main"][None]
    parts["w_dt"] = jnp.pad(_lanes_to_heads(g["w_dt"]), ((0, 0), (0, 128 - N_HEADS)))[None]
    names = ("w_main", "w_dt") + _BIG[1:]
    g5 = {n: parts[n].reshape(parts[n].shape[0], 2, parts[n].shape[1] // 2, parts[n].shape[2]) for n in names}
    to_sibling = [lax.dynamic_index_in_dim(g5[n], 1 - ci, axis=1, keepdims=False).astype(WIRE_DTYPE) for n in names]
    from_sibling = _swap_with_sibling(to_sibling)
    chip = {n: _chip_sum("chip_sum_" + n, g5[n], r, c_arr) for n, r in zip(names, from_sibling)}
    chip["w_in"] = _w_in_grad_blocks(chip["w_main"][0], chip["w_dt"][0])
    chip_wire = [chip[n] for n in _BIG]
    s_sem, r_sem, sent, lands, token = _scatter_blocks_start(chip_wire)
    grad_x, g["norm_g"] = _input_grad(x[0], w, ctx, after=(token,))
    g = _natural_grads(g)

    pieces = [_two_d(n, g[n]).reshape(-1) for n in _SMALL]
    sizes = [v.shape[0] for v in pieces]
    padded = [-(-s // _PACK) * _PACK for s in sizes]
    packed = jnp.concatenate([jnp.pad(v, (0, ps - s)) for v, s, ps in zip(pieces, sizes, padded)]).reshape(-1, 128)
    a_ssem, a_rsem, a_src, a_land, a_token = _small_gather_start(packed)

    sent, from_chips = _scatter_blocks_wait(s_sem, r_sem, sent, lands, (grad_x, a_token))
    halves = [_final_sum("final_sum_" + n, a, r, place_arr) for n, a, r in zip(_BIG, sent, from_chips)]
    grads = dict(zip(_BIG, _share_halves(halves)))
    delta, new_m, new_v = {}, {}, {}
    for n in _BIG:
        t = jnp.transpose if n == "w_in" else (lambda a: a)
        res = _adamw_call("adamw_" + n, t(shard[n]), t(grads[n]), t(mom[n][0]), t(vel[n][0]))
        delta[n], new_m[n], new_v[n], grads[n] = (t(r) for r in res)

    a_src, a_land = _small_gather_wait(a_ssem, a_rsem, a_src, a_land, delta["w_in"])
    summed = _small_sum(a_src, a_land, jnp.reshape(4 * xi + 2 * yi + ci, (1,)).astype(jnp.int32)).reshape(-1)
    off = 0
    for n, s, ps in zip(_SMALL, sizes, padded):
        grads[n] = summed[off:off + s].reshape(_two_d(n, g[n]).shape)
        off += ps
    grads["conv_w"] = lax.dynamic_slice_in_dim(grads["conv_w"], me * (CONV_DIM // N_CHIPS), CONV_DIM // N_CHIPS, axis=1)
    small = _adamw_small([_two_d(n, wt[n]) for n in _SMALL], [grads[n] for n in _SMALL],
                         [_two_d(n, mom[n]) for n in _SMALL], [_two_d(n, vel[n]) for n in _SMALL])
    for i, n in enumerate(_SMALL):
        delta[n], new_m[n], new_v[n] = small[0][i], small[1][i], small[2][i]

    def shaped(d):
        return [d[n].reshape(wt[n].shape) for n in _WEIGHTS]

    return (loss, grad_x[None], *shaped(grads), *shaped(delta), *shaped(new_m), *shaped(new_v))
```

```python
import functools

import jax
import jax.numpy as jnp
from jax import lax
from jax.experimental import pallas as pl
from jax.experimental.pallas import tpu as pltpu

F32 = jnp.float32
MXU_DTYPE = jnp.bfloat16
ACT_DTYPE = jnp.bfloat16
WIRE_DTYPE = jnp.bfloat16

D_MODEL = 1024
PLE_DIM = 256
CHUNK = 128
EPS = 1e-6
E_A = D_MODEL
G_A = 4
D_INNER = 2 * D_MODEL
HEAD_DIM = 64
N_HEADS = D_INNER // HEAD_DIM
N_STATE = 128
N_GROUPS = 4
HEADS_PER_GROUP = N_HEADS // N_GROUPS
PAIRS_PER_GROUP = HEADS_PER_GROUP // 2
CONV_K = 4
CONV_DIM = D_INNER + 2 * N_GROUPS * N_STATE
N_IN = 3 * E_A + D_INNER + CONV_DIM + N_HEADS + 2 * D_MODEL
N_CHIPS = 4
N_DEV = 8
W_IN_BLOCK = N_IN // N_CHIPS

UVZ_W, XBC_W, ZB_W, G_W = 3 * E_A, CONV_DIM, D_INNER, 2 * D_MODEL
MAIN_W = UVZ_W + XBC_W + ZB_W + G_W
UVZ_CB, XBC_CB, ZB_CB, G_CB = 0, 1, 3, 4
DT_W = N_GROUPS * 128

ADAM_LR, ADAM_B1, ADAM_B2, ADAM_EPS, ADAM_WD, ADAM_STEP = 0.001, 0.9, 0.999, 1e-08, 0.01, 10

MESH = pl.DeviceIdType.MESH
ANY = pl.BlockSpec(memory_space=pl.ANY)


def _mxu(v):
    return v.astype(MXU_DTYPE)


def _dot(a, b, dims=(((1,), (0,)), ((), ()))):
    return lax.dot_general(_mxu(a), _mxu(b), dims, preferred_element_type=F32)


V7X_MXU_WIDTH = 256
V7X_SCOPED_VMEM_BYTES = 60000 * 1024
MM_TILE = 4 * V7X_MXU_WIDTH
MM_VMEM_BUDGET = 4 * V7X_SCOPED_VMEM_BYTES // 5


def _mm_tk(m, n, k, tm, tn, a_bytes, b_bytes, out_bytes, extra_bytes):
    one_tile = m == tm and n == tn
    for parts in range(2 if one_tile else 1, k // 128 + 1):
        if k % parts or (k // parts) % 128 and parts > 1:
            continue
        tk = k // parts
        a_bufs = 1 if (parts == 1 and m == tm) else 2
        b_bufs = 1 if (parts == 1 and n == tn) else 2
        need = (tk * (a_bufs * tm * a_bytes + b_bufs * tn * b_bytes) + 2 * tm * tn * (out_bytes + extra_bytes)
                + (tm * tn * 4 if parts > 1 else 0))
        if need <= MM_VMEM_BUDGET:
            return tk
    return 128


def _matmul(a, b, *, mode, name, out_dtype, m, n, k, tm=MM_TILE, tn=MM_TILE, tk=None, a_off=0, b_off=0,
            extras=(), epilogue=None, after=()):
    tm, tn = min(tm, m), min(tn, n)
    if tk is None:
        tk = _mm_tk(m, n, k, tm, tn, a.dtype.itemsize, b.dtype.itemsize, jnp.dtype(out_dtype).itemsize,
                    sum(e.dtype.itemsize for e in extras))
    tk = min(tk, k)
    assert m % tm == 0 and n % tn == 0 and k % tk == 0, (name, m, n, k, tm, tn, tk)
    nk = k // tk
    a_mode = pl.Buffered(1) if (nk == 1 and m == tm) else None
    b_mode = pl.Buffered(1) if (nk == 1 and n == tn) else None
    if mode == "nn":
        assert a_off % tk == 0 and b_off % tn == 0
        a_spec = pl.BlockSpec((tm, tk), lambda i, j, kk: (i, kk + a_off // tk), pipeline_mode=a_mode)
        b_spec = pl.BlockSpec((tk, tn), lambda i, j, kk: (kk, j + b_off // tn), pipeline_mode=b_mode)
        dims = (((1,), (0,)), ((), ()))
    elif mode == "nt":
        a_spec = pl.BlockSpec((tm, tk), lambda i, j, kk: (i, kk), pipeline_mode=a_mode)
        b_spec = pl.BlockSpec((tn, tk), lambda i, j, kk: (j, kk), pipeline_mode=b_mode)
        dims = (((1,), (1,)), ((), ()))
    else:
        assert a_off % tm == 0 and b_off % tn == 0
        a_spec = pl.BlockSpec((tk, tm), lambda i, j, kk: (kk, i + a_off // tm), pipeline_mode=a_mode)
        b_spec = pl.BlockSpec((tk, tn), lambda i, j, kk: (kk, j + b_off // tn), pipeline_mode=b_mode)
        dims = (((0,), (0,)), ((), ()))
    ne = len(extras)

    def finish(acc, extra_refs, o_ref):
        res = acc if epilogue is None else epilogue(acc, *[e[...] for e in extra_refs])
        o_ref[...] = res.astype(o_ref.dtype)

    def body(a_ref, b_ref, *rest):
        extra_refs, o_ref = rest[:ne], rest[ne + len(after)]
        part = _dot(a_ref[...], b_ref[...], dims)
        if nk == 1:
            finish(part, extra_refs, o_ref)
            return
        acc_ref = rest[ne + len(after) + 1]
        kk = pl.program_id(2)

        @pl.when(kk == 0)
        def _():
            acc_ref[...] = part

        @pl.when(kk > 0)
        def _():
            acc_ref[...] += part

        @pl.when(kk == nk - 1)
        def _():
            finish(acc_ref[...], extra_refs, o_ref)

    o_spec = pl.BlockSpec((tm, tn), lambda i, j, kk: (i, j))
    return pl.pallas_call(
        body, name=name, grid=(m // tm, n // tn, nk),
        in_specs=[a_spec, b_spec] + [o_spec] * ne + [ANY] * len(after), out_specs=o_spec,
        out_shape=jax.ShapeDtypeStruct((m, n), out_dtype),
        scratch_shapes=[pltpu.VMEM((tm, tn), F32)] if nk > 1 else [],
        compiler_params=pltpu.CompilerParams(dimension_semantics=("parallel", "parallel", "arbitrary")),
    )(a, b, *extras, *after)


def _rows_matmul(name, f, rows, pars, b, *, out_dtype, n, k, tm, nrows, tn=MM_TILE, extras=(), epilogue=None, after=(),
                 side=None):
    tm, tn = min(tm, nrows), min(tn, n)
    assert nrows % tm == 0 and n % tn == 0, (name, nrows, n, tm, tn)
    nr, npar, ne, nj = len(rows), len(pars), len(extras), n // tn
    ns = 0 if side is None else 1
    n_in = nr + npar + 1 + ne + ns + len(after)

    def body(*refs):
        row_refs, par_refs, b_ref = refs[:nr], refs[nr:nr + npar], refs[nr + npar]
        extra_refs = refs[nr + npar + 1:nr + npar + 1 + ne]
        a_ref, o_ref = refs[n_in], refs[n_in + 1]

        def make_a():
            a = f(*[r[...].astype(F32) for r in row_refs], *[p[...] for p in par_refs])[0]
            a_ref[...] = a.astype(a_ref.dtype)
            if ns:
                refs[n_in + 2][...] = _dot(a_ref[...], refs[nr + npar + 1 + ne][...]).astype(refs[n_in + 2].dtype)

        if nj == 1:
            make_a()
        else:
            pl.when(pl.program_id(1) == 0)(make_a)
        res = _dot(a_ref[...], b_ref[...])
        if epilogue is not None:
            res = epilogue(res, *[e[...] for e in extra_refs])
        o_ref[...] = res.astype(o_ref.dtype)

    o_spec = pl.BlockSpec((tm, tn), lambda i, j: (i, j))
    side_in = [] if side is None else [pl.BlockSpec(tuple(side[0].shape), lambda i, j: (0, 0))]
    side_out = [] if side is None else [pl.BlockSpec((tm, side[0].shape[1]), lambda i, j: (i, 0))]
    side_shape = [] if side is None else [jax.ShapeDtypeStruct((nrows, side[0].shape[1]), side[1])]
    return pl.pallas_call(
        body, name=name, grid=(nrows // tm, nj),
        in_specs=[pl.BlockSpec((tm, w), lambda i, j, cb=cb: (i, cb)) for _, w, cb in rows]
        + [pl.BlockSpec(tuple(p.shape), lambda i, j, nd=p.ndim: (0,) * nd) for p in pars]
        + [pl.BlockSpec((k, tn), lambda i, j: (0, j))] + [o_spec] * ne + side_in + [ANY] * len(after),
        out_specs=[pl.BlockSpec((tm, k), lambda i, j: (i, 0)), o_spec] + side_out,
        out_shape=[jax.ShapeDtypeStruct((nrows, k), ACT_DTYPE), jax.ShapeDtypeStruct((nrows, n), out_dtype)] + side_shape,
        compiler_params=pltpu.CompilerParams(dimension_semantics=("parallel", "arbitrary")),
    )(*[r[0] for r in rows], *pars, b, *extras, *([] if side is None else [side[0]]), *after)


def _row_spec(tm, width, cb):
    return pl.BlockSpec((tm, width), lambda i: (i, cb))


def _whole_spec(shape):
    nd = len(shape)
    return pl.BlockSpec(tuple(shape), lambda i: (0,) * nd)


def _rows_vjp_call(name, f, rows, pars, cots, drows, *, tm, nrows, cot_mm=None):
    tm = min(tm, nrows)
    nr, npar, nc = len(rows), len(pars), len(cots)
    mm_args, mm_specs = [], []
    if cot_mm is not None:
        mm_a, mm_b, mm_add = cot_mm
        mm_args = [mm_a, mm_b] + ([] if mm_add is None else [mm_add])
        mm_specs = [_row_spec(tm, mm_a.shape[1], 0), _whole_spec(mm_b.shape)]
        mm_specs += [] if mm_add is None else [_row_spec(tm, mm_b.shape[0], 0)]
    alias_bufs, aliases = [], {}
    out_shape, out_specs = [], []
    for (arr, w, cb), d in zip(rows, drows):
        if d is None:
            continue
        dt, into = d
        if into is None:
            out_shape.append(jax.ShapeDtypeStruct((nrows, w), dt))
            out_specs.append(_row_spec(tm, w, 0))
        else:
            buf, total, ocb = into
            if buf is not None:
                aliases[nr + npar + nc + len(alias_bufs)] = len(out_shape)
                alias_bufs.append(buf)
            out_shape.append(jax.ShapeDtypeStruct((nrows, total), dt))
            out_specs.append(_row_spec(tm, w, ocb))
    n_drow = len(out_shape)
    for p in pars:
        out_shape.append(jax.ShapeDtypeStruct(p.shape, F32))
        out_specs.append(_whole_spec(p.shape))
    na = len(alias_bufs)

    def body(*refs):
        rv = [r[...].astype(F32) for r in refs[:nr]]
        pv = [p[...] for p in refs[nr:nr + npar]]
        cv = tuple(c[...].astype(F32) for c in refs[nr + npar:nr + npar + nc])
        o_refs = refs[nr + npar + nc + na + len(mm_args):]
        if mm_args:
            mm_refs = refs[nr + npar + nc + na:nr + npar + nc + na + len(mm_args)]
            c0 = _dot(mm_refs[0][...], mm_refs[1][...], (((1,), (1,)), ((), ())))
            if len(mm_refs) == 3:
                c0 = c0 + mm_refs[2][...].astype(F32)
            cv = (c0,) + cv
        _, vjp = jax.vjp(f, *rv, *pv)
        g = vjp(cv)
        oi = 0
        for ri, d in enumerate(drows):
            if d is not None:
                o_refs[oi][...] = g[ri].astype(o_refs[oi].dtype)
                oi += 1
        first = pl.program_id(0) == 0
        for pi in range(npar):
            acc = o_refs[n_drow + pi]

            @pl.when(first)
            def _(acc=acc):
                acc[...] = jnp.zeros_like(acc)

            acc[...] += g[nr + pi]

    return pl.pallas_call(
        body, name=name, grid=(nrows // tm,),
        in_specs=[_row_spec(tm, w, cb) for _, w, cb in rows] + [_whole_spec(p.shape) for p in pars]
        + [_row_spec(tm, w, cb) for _, w, cb in cots] + [ANY] * na + mm_specs,
        out_specs=out_specs, out_shape=out_shape, input_output_aliases=aliases,
        compiler_params=pltpu.CompilerParams(dimension_semantics=("arbitrary",)),
    )(*[r[0] for r in rows], *pars, *[c[0] for c in cots], *alias_bufs, *mm_args)


def _rms(x, g):
    return x * lax.rsqrt(jnp.mean(x * x, axis=-1, keepdims=True) + EPS) * g


def _f_rms(x, g):
    return (_rms(x, g),)


def _tril_mask():
    return lax.broadcasted_iota(jnp.int32, (CHUNK, CHUNK), 0) >= lax.broadcasted_iota(jnp.int32, (CHUNK, CHUNK), 1)


def _f_branch_a(uvz, ln_g, ln_b, w_s, b_s):
    u = jax.nn.gelu(uvz[:, :E_A])
    v = jax.nn.gelu(uvz[:, E_A:2 * E_A])
    z = uvz[:, 2 * E_A:]
    xc = v - jnp.mean(v, axis=-1, keepdims=True)
    vn = xc * lax.rsqrt(jnp.mean(xc * xc, axis=-1, keepdims=True) + EPS) * ln_g + ln_b
    mask = _tril_mask()
    ws = [jnp.where(mask, w_s[g], 0.0) for g in range(G_A)]
    gw = E_A // G_A
    rows = []
    for c in range(uvz.shape[0] // CHUNK):
        vc = vn[c * CHUNK:(c + 1) * CHUNK]
        rows.append(jnp.concatenate([_dot(ws[g], vc[:, g * gw:(g + 1) * gw]) + b_s[g] for g in range(G_A)], axis=1))
    sv = rows[0] if len(rows) == 1 else jnp.concatenate(rows, axis=0)
    return (u * sv * jax.nn.silu(z),)


def _f_gnorm(y, zb, g):
    yz = y * jax.nn.silu(zb)
    gw = D_INNER // N_GROUPS
    parts = []
    for i in range(N_GROUPS):
        s = yz[:, i * gw:(i + 1) * gw]
        parts.append(s * lax.rsqrt(jnp.mean(s * s, axis=-1, keepdims=True) + EPS))
    return (jnp.concatenate(parts, axis=1) * g,)


def _f_merge(g2, oa, ob):
    return (jax.nn.sigmoid(g2[:, :D_MODEL]) * oa + jax.nn.sigmoid(g2[:, D_MODEL:]) * ob,)


def _f_loss(x1, gp, pe, tgt, fg):
    x2 = x1 + jax.nn.sigmoid(gp) * pe
    err = _rms(x2, fg) - tgt
    return 0.5 * jnp.sum(jnp.mean(err * err, axis=-1))


def _head(x1, p, tgt, ple_g, w_pg, w_ple, fg, *, tm, nrows):
    tm = min(tm, nrows)

    def body(x1_ref, p_ref, t_ref, pg_ref, wpg_ref, wple_ref, fg_ref, hp_ref, dx_ref, dgp_ref, dpe_ref, dfg_ref, loss_ref):
        x1 = x1_ref[...]
        hp_ref[...] = _rms(x1, pg_ref[...]).astype(hp_ref.dtype)
        gp = _dot(hp_ref[...], wpg_ref[...])
        pe = _dot(p_ref[...], wple_ref[...])
        loss, vjp = jax.vjp(_f_loss, x1, gp, pe, t_ref[...], fg_ref[...])
        dx, dgp, dpe, _, dfg = vjp(jnp.ones((), F32))
        dx_ref[...] = dx
        dgp_ref[...] = dgp.astype(dgp_ref.dtype)
        dpe_ref[...] = dpe.astype(dpe_ref.dtype)

        @pl.when(pl.program_id(0) == 0)
        def _():
            dfg_ref[...] = jnp.zeros_like(dfg_ref)
            loss_ref[...] = jnp.zeros_like(loss_ref)

        dfg_ref[...] += dfg
        loss_ref[...] += jnp.full(loss_ref.shape, loss, F32)

    row = _row_spec(tm, D_MODEL, 0)
    act = jax.ShapeDtypeStruct((nrows, D_MODEL), ACT_DTYPE)
    return pl.pallas_call(
        body, name="head", grid=(nrows // tm,),
        in_specs=[row, _row_spec(tm, PLE_DIM, 0), row, _whole_spec((1, D_MODEL)), _whole_spec(w_pg.shape),
                  _whole_spec(w_ple.shape), _whole_spec((1, D_MODEL))],
        out_specs=[row, row, row, row, _whole_spec((1, D_MODEL)), _whole_spec((1, 128))],
        out_shape=[act, jax.ShapeDtypeStruct((nrows, D_MODEL), F32), act, act, jax.ShapeDtypeStruct((1, D_MODEL), F32),
                   jax.ShapeDtypeStruct((1, 128), F32)],
        compiler_params=pltpu.CompilerParams(dimension_semantics=("arbitrary",)),
    )(x1, p, tgt, ple_g, w_pg, w_ple, fg)


def _shift_rows(cur, edge, j, up):
    tm = cur.shape[0]
    row = lax.broadcasted_iota(jnp.int32, cur.shape, 0)
    if up:
        sh = pltpu.roll(cur, tm - j, 0)
        e = jnp.tile(pltpu.roll(edge, 8 - j, 0), (tm // 8, 1))
        return jnp.where(row >= tm - j, e, sh)
    sh = pltpu.roll(cur, j, 0)
    e = jnp.tile(pltpu.roll(edge, j, 0), (tm // 8, 1))
    return jnp.where(row < j, e, sh)


def _conv_pre(cur, prev, w, b):
    acc = cur * w[CONV_K - 1:CONV_K] + b
    for j in range(1, CONV_K):
        acc = acc + _shift_rows(cur, prev, j, up=False) * w[CONV_K - 1 - j:CONV_K - j]
    return acc


def _halo_specs(tm, nrows, cb, before):
    nb = tm // 8
    last = nrows // 8 - 1
    if before:
        return pl.BlockSpec((8, XBC_W), lambda i: (jnp.maximum(i * nb - 1, 0), cb))
    return pl.BlockSpec((8, XBC_W), lambda i: (jnp.minimum((i + 1) * nb, last), cb))


def _conv_fwd(proj, conv_w, conv_b, *, tm, nrows):
    tm = min(tm, nrows)

    def body(cur_ref, prev_ref, w_ref, b_ref, o_ref, pre_ref):
        prev = jnp.where(pl.program_id(0) == 0, 0.0, prev_ref[...].astype(F32))
        pre = _conv_pre(cur_ref[...].astype(F32), prev, w_ref[...], b_ref[...])
        o_ref[...] = jax.nn.silu(pre).astype(o_ref.dtype)
        pre_ref[...] = pre.astype(pre_ref.dtype)

    out = jax.ShapeDtypeStruct((nrows, XBC_W), ACT_DTYPE)
    return pl.pallas_call(
        body, name="conv_fwd", grid=(nrows // tm,),
        in_specs=[_row_spec(tm, XBC_W, XBC_CB), _halo_specs(tm, nrows, XBC_CB, True),
                  _whole_spec((CONV_K, XBC_W)), _whole_spec((1, XBC_W))],
        out_specs=[_row_spec(tm, XBC_W, 0)] * 2, out_shape=[out, out],
        compiler_params=pltpu.CompilerParams(dimension_semantics=("parallel",)),
    )(proj, proj, conv_w, conv_b)


def _conv_bwd_act(pre, dact, *, tm, nrows):
    tm = min(tm, nrows)
    nb = N_GROUPS * N_STATE

    def body(pre_ref, dxs_ref, dbm_ref, dcm_ref, dpre_ref, db_ref):
        pre = pre_ref[...].astype(F32)
        sg = jax.nn.sigmoid(pre)
        dy = jnp.concatenate([dxs_ref[...], dbm_ref[...], dcm_ref[...]], axis=1).astype(F32)
        dpre = dy * sg * (1.0 + pre * (1.0 - sg))
        dpre_ref[...] = dpre.astype(dpre_ref.dtype)

        @pl.when(pl.program_id(0) == 0)
        def _():
            db_ref[...] = jnp.zeros_like(db_ref)

        db_ref[...] += jnp.sum(dpre, axis=0, keepdims=True)

    return pl.pallas_call(
        body, name="conv_bwd_act", grid=(nrows // tm,),
        in_specs=[_row_spec(tm, XBC_W, 0), _row_spec(tm, D_INNER, 0), _row_spec(tm, nb, 0), _row_spec(tm, nb, 0)],
        out_specs=[_row_spec(tm, XBC_W, 0), _whole_spec((1, XBC_W))],
        out_shape=[jax.ShapeDtypeStruct((nrows, XBC_W), ACT_DTYPE), jax.ShapeDtypeStruct((1, XBC_W), F32)],
        compiler_params=pltpu.CompilerParams(dimension_semantics=("arbitrary",)),
    )(pre, *dact)


def _conv_bwd_x(dpre, proj, conv_w, dproj, *, tm, nrows):
    tm = min(tm, nrows)
    ntiles = nrows // tm

    def body(cur_ref, nxt_ref, x_ref, w_ref, _, o_ref, dw_ref):
        cur = cur_ref[...].astype(F32)
        nxt = jnp.where(pl.program_id(0) == ntiles - 1, 0.0, nxt_ref[...].astype(F32))
        x = x_ref[...].astype(F32)
        w = w_ref[...]

        @pl.when(pl.program_id(0) == 0)
        def _():
            dw_ref[...] = jnp.zeros_like(dw_ref)

        acc = cur * w[CONV_K - 1:CONV_K]
        dw_ref[CONV_K - 1:CONV_K, :] += jnp.sum(cur * x, axis=0, keepdims=True)
        for j in range(1, CONV_K):
            u = _shift_rows(cur, nxt, j, up=True)
            acc = acc + u * w[CONV_K - 1 - j:CONV_K - j]
            dw_ref[CONV_K - 1 - j:CONV_K - j, :] += jnp.sum(u * x, axis=0, keepdims=True)
        o_ref[...] = acc.astype(o_ref.dtype)

    return pl.pallas_call(
        body, name="conv_bwd_x", grid=(ntiles,),
        in_specs=[_row_spec(tm, XBC_W, 0), _halo_specs(tm, nrows, 0, False), _row_spec(tm, XBC_W, XBC_CB),
                  _whole_spec((CONV_K, XBC_W)), ANY],
        out_specs=[_row_spec(tm, XBC_W, XBC_CB), _whole_spec((CONV_K, XBC_W))],
        out_shape=[jax.ShapeDtypeStruct(dproj.shape, dproj.dtype), jax.ShapeDtypeStruct((CONV_K, XBC_W), F32)],
        input_output_aliases={4: 0},
        compiler_params=pltpu.CompilerParams(dimension_semantics=("arbitrary",)),
    )(dpre, dpre, proj, conv_w, dproj)


SSD_SPAN = 4
SSD_FWD_SPANS = 4
_XS_GW = D_INNER // N_GROUPS
_NT = (((1,), (1,)), ((), ()))
_TN = (((0,), (0,)), ((), ()))


def _bf16_terms(x, terms):
    parts, rest = [], x
    for _ in range(terms):
        part = rest.astype(jnp.bfloat16)
        parts.append(part)
        rest = rest - part.astype(F32)
    return parts


def _head_lane_matrix():
    return (lax.broadcasted_iota(jnp.int32, (128, _XS_GW), 0)
            == lax.broadcasted_iota(jnp.int32, (128, _XS_GW), 1) // HEAD_DIM).astype(jnp.bfloat16)


@functools.partial(jax.custom_vjp, nondiff_argnums=(1,))
def _head_lanes(cols, terms):
    e = _head_lane_matrix()
    return sum(jnp.dot(t, e, preferred_element_type=F32) for t in _bf16_terms(cols, terms))


def _head_lanes_fwd(cols, terms):
    return _head_lanes(cols, terms), None


def _head_lanes_bwd(terms, _, g):
    e = _head_lane_matrix()
    return (sum(lax.dot_general(t, e, _NT, preferred_element_type=F32) for t in _bf16_terms(g, 2)),)


_head_lanes.defvjp(_head_lanes_fwd, _head_lanes_bwd)


def _ssd_chunk(k, xs, bm, cm, dtr, hprev, dtb, alog, dsk):
    causal, tri, lo = k
    dt = jax.nn.softplus(dtr + dtb)
    da = dt * (-jnp.exp(alog))
    cs = jnp.dot(tri, da, precision=lax.Precision.HIGHEST, preferred_element_type=F32)
    cst = cs.T
    cs_l = _head_lanes(cs, 3)
    xdt = xs * _head_lanes(dt, 2)
    cb = _dot(cm, bm, _NT)
    yd = []
    for q in range(PAIRS_PER_GROUP):
        xq = xdt[:, 128 * q:128 * (q + 1)]
        y2 = [_dot(cb * jnp.exp(jnp.where(causal, cs[:, h:h + 1] - cst[h:h + 1, :], -jnp.inf)), xq)
              for h in (2 * q, 2 * q + 1)]
        yd.append(jnp.where(lo, y2[0], y2[1]))
    y_off = jnp.exp(cs_l) * _dot(cm, hprev, _NT)
    st = _dot(xdt * jnp.exp(cs_l[CHUNK - 1:CHUNK, :] - cs_l), bm, _TN)
    cdec = jnp.exp(cs[CHUNK - 1:CHUNK, :])
    cd_rows = jnp.concatenate(
        [jnp.broadcast_to(cdec[:, h:h + 1], (HEAD_DIM, N_STATE)) for h in range(HEADS_PER_GROUP)], axis=0)
    dsk_l = _head_lanes(jnp.broadcast_to(dsk, (8, 128)), 2)[:1]
    y = jnp.concatenate(yd, axis=1) + y_off + xs * dsk_l
    return y, cd_rows * hprev + st


def _ssd_span(xs, bm, cm, dtr, h0, dtb, alog, dsk):
    li = lax.broadcasted_iota(jnp.int32, (CHUNK, CHUNK), 0)
    si = lax.broadcasted_iota(jnp.int32, (CHUNK, CHUNK), 1)
    causal = li >= si
    k = (causal, causal.astype(F32), si < HEAD_DIM)
    h, ys = h0, []
    for t in range(xs.shape[0] // CHUNK):
        r = slice(t * CHUNK, (t + 1) * CHUNK)
        y, h = _ssd_chunk(k, xs[r], bm[r], cm[r], dtr[r], h, dtb, alog, dsk)
        ys.append(y)
    return (ys[0] if len(ys) == 1 else jnp.concatenate(ys, axis=0)), h


def _ssd_specs(rev, nsteps, rows):
    def s_of(s):
        return nsteps - 1 - s if rev else s

    xs = pl.BlockSpec((rows, _XS_GW), lambda g, s: (s_of(s), g))
    bm = pl.BlockSpec((rows, N_STATE), lambda g, s: (s_of(s), D_INNER // N_STATE + g))
    cm = pl.BlockSpec((rows, N_STATE), lambda g, s: (s_of(s), D_INNER // N_STATE + N_GROUPS + g))
    dt = pl.BlockSpec((rows, 128), lambda g, s: (s_of(s), g))
    par = pl.BlockSpec((1, 128), lambda g, s: (0, g))
    st = pl.BlockSpec((None, None, _XS_GW, N_STATE), lambda g, s: (g, s_of(s), 0, 0))
    return xs, bm, cm, dt, par, st


def _ssd_fwd(act, dtr, dtb, alog, dsk, *, nrows):
    span = CHUNK * min(SSD_SPAN, nrows // CHUNK)
    per_step = SSD_FWD_SPANS if nrows % (SSD_FWD_SPANS * span) == 0 else 1
    rows = per_step * span
    nsteps = nrows // rows
    xs, bm, cm, dt, par, _ = _ssd_specs(False, nsteps, rows)
    st = pl.BlockSpec((None, per_step, _XS_GW, N_STATE), lambda g, s: (g, s, 0, 0))

    def body(xs_ref, b_ref, c_ref, dt_ref, dtb_ref, al_ref, dk_ref, y_ref, st_ref, h_ref):
        @pl.when(pl.program_id(1) == 0)
        def _():
            h_ref[...] = jnp.zeros_like(h_ref)

        h = h_ref[...]
        for i in range(per_step):
            r = slice(i * span, (i + 1) * span)
            st_ref[i] = h
            y, h = _ssd_span(xs_ref[r, :].astype(F32), b_ref[r, :].astype(F32), c_ref[r, :].astype(F32), dt_ref[r, :],
                             h, dtb_ref[...], al_ref[...], dk_ref[...])
            y_ref[r, :] = y.astype(y_ref.dtype)
        h_ref[...] = h

    return pl.pallas_call(
        body, name="ssd_fwd", grid=(N_GROUPS, nsteps),
        in_specs=[xs, bm, cm, dt, par, par, par], out_specs=[xs, st],
        out_shape=[jax.ShapeDtypeStruct((nrows, D_INNER), ACT_DTYPE),
                   jax.ShapeDtypeStruct((N_GROUPS, nrows // span, _XS_GW, N_STATE), F32)],
        scratch_shapes=[pltpu.VMEM((_XS_GW, N_STATE), F32)],
        compiler_params=pltpu.CompilerParams(dimension_semantics=("arbitrary", "arbitrary")),
    )(act, act, act, dtr, dtb, alog, dsk)


def _ssd_bwd(act, dtr, dtb, alog, dsk, states, dy, *, nrows):
    rows = CHUNK * min(SSD_SPAN, nrows // CHUNK)
    nsteps = nrows // rows
    xs, bm, cm, dt, par, st = _ssd_specs(True, nsteps, rows)

    def body(xs_ref, b_ref, c_ref, dt_ref, dtb_ref, al_ref, dk_ref, st_ref, dy_ref,
             dxs_ref, db_ref, dc_ref, ddt_ref, ddtb_ref, dal_ref, ddk_ref, dh_ref):
        @pl.when(pl.program_id(1) == 0)
        def _():
            dh_ref[...] = jnp.zeros_like(dh_ref)
            ddtb_ref[...] = jnp.zeros_like(ddtb_ref)
            dal_ref[...] = jnp.zeros_like(dal_ref)
            ddk_ref[...] = jnp.zeros_like(ddk_ref)

        _, vjp = jax.vjp(_ssd_span, xs_ref[...].astype(F32), b_ref[...].astype(F32), c_ref[...].astype(F32),
                         dt_ref[...], st_ref[...], dtb_ref[...], al_ref[...], dk_ref[...])
        dxs, db, dc, ddt, dh, ddtb, dal, ddk = vjp((dy_ref[...].astype(F32), dh_ref[...]))
        dxs_ref[...] = dxs.astype(dxs_ref.dtype)
        db_ref[...] = db.astype(db_ref.dtype)
        dc_ref[...] = dc.astype(dc_ref.dtype)
        ddt_ref[...] = ddt
        dh_ref[...] = dh
        ddtb_ref[...] += ddtb
        dal_ref[...] += dal
        ddk_ref[...] += ddk

    nb = N_GROUPS * N_STATE
    bspec = pl.BlockSpec((rows, N_STATE), lambda g, s: (nsteps - 1 - s, g))
    return pl.pallas_call(
        body, name="ssd_bwd", grid=(N_GROUPS, nsteps),
        in_specs=[xs, bm, cm, dt, par, par, par, st, xs],
        out_specs=[xs, bspec, bspec, dt, par, par, par],
        out_shape=[jax.ShapeDtypeStruct((nrows, D_INNER), ACT_DTYPE), jax.ShapeDtypeStruct((nrows, nb), ACT_DTYPE),
                   jax.ShapeDtypeStruct((nrows, nb), ACT_DTYPE), jax.ShapeDtypeStruct((nrows, DT_W), F32),
                   jax.ShapeDtypeStruct((1, DT_W), F32), jax.ShapeDtypeStruct((1, DT_W), F32),
                   jax.ShapeDtypeStruct((1, DT_W), F32)],
        scratch_shapes=[pltpu.VMEM((_XS_GW, N_STATE), F32)],
        compiler_params=pltpu.CompilerParams(dimension_semantics=("arbitrary", "arbitrary")),
    )(act, act, act, dtr, dtb, alog, dsk, states, dy)


def _add_epilogue(acc, r):
    return r + acc


def _rms_and_skip(x, g):
    return _rms(x, g), x


def _forward_backward(x, p, tgt, w, late_weights=None, after=()):
    s = x.shape[0]
    act_t, f32 = ACT_DTYPE, F32
    mm = functools.partial(_matmul)
    h, proj, dtr = _rows_matmul("proj", _f_rms, [(x, D_MODEL, 0)], [w["norm_g"]], w["w_main"], out_dtype=act_t,
                                n=MAIN_W, k=D_MODEL, tm=1024, tn=2 * MM_TILE, nrows=s, after=after, side=(w["w_dt"], f32))
    act, conv_pre = _conv_fwd(proj, w["conv_w"], w["conv_b"], tm=512, nrows=s)
    y, states = _ssd_fwd(act, dtr, w["dt_bias"], w["a_log"], w["d_skip"], nrows=s)
    if late_weights is not None:
        w = {**w, **late_weights(states)}
    a_pars = [w["ln_a_g"], w["ln_a_b"], w["w_s"], w["b_s"]]
    y_a, o_a = _rows_matmul("out_a", _f_branch_a, [(proj, UVZ_W, UVZ_CB)], a_pars, w["w_oa"], out_dtype=act_t,
                            n=D_MODEL, k=E_A, tm=512, nrows=s)
    gn_rows = [(y, D_INNER, 0), (proj, ZB_W, ZB_CB)]
    y_b, o_b = _rows_matmul("out_b", _f_gnorm, gn_rows, [w["ssm_norm_g"]], w["w_ob"], out_dtype=act_t,
                            n=D_MODEL, k=D_INNER, tm=1024, nrows=s)
    mg_rows = [(proj, G_W, G_CB), (o_a, D_MODEL, 0), (o_b, D_MODEL, 0)]
    merged, x1 = _rows_matmul("out_proj", _f_merge, mg_rows, [], w["w_out"], out_dtype=f32, n=D_MODEL, k=D_MODEL,
                              tm=1024, nrows=s, extras=(x,), epilogue=_add_epilogue)
    g = {}
    hp, dx2, dgp, dpe, g["final_g"], loss = _head(x1, p, tgt, w["ple_norm_g"], w["w_pg"], w["w_ple"], w["final_g"],
                                                   tm=512, nrows=s)
    g["w_pg"] = mm(hp, dgp, mode="tn", name="d_w_pg", out_dtype=f32, m=D_MODEL, n=D_MODEL, k=s, tn=MM_TILE // 2)
    g["w_ple"] = mm(p, dpe, mode="tn", name="d_w_ple", out_dtype=f32, m=PLE_DIM, n=D_MODEL, k=s)
    dx1, g["ple_norm_g"] = _rows_vjp_call(
        "ple_norm_bwd", _rms_and_skip, [(x1, D_MODEL, 0)], [w["ple_norm_g"]], [(dx2, D_MODEL, 0)],
        [(f32, None)], tm=1024, nrows=s, cot_mm=(dgp, w["w_pg"], None))
    g["w_out"] = mm(merged, dx1, mode="tn", name="d_w_out", out_dtype=f32, m=D_MODEL, n=D_MODEL, k=s)
    dproj, do_a, do_b = _rows_vjp_call(
        "merge_bwd", _f_merge, mg_rows, [], [],
        [(act_t, (None, MAIN_W, G_CB)), (act_t, None), (act_t, None)], tm=1024, nrows=s, cot_mm=(dx1, w["w_out"], None))
    g["w_oa"] = mm(y_a, do_a, mode="tn", name="d_w_oa", out_dtype=f32, m=E_A, n=D_MODEL, k=s, tn=MM_TILE // 2)
    g["w_ob"] = mm(y_b, do_b, mode="tn", name="d_w_ob", out_dtype=f32, m=D_INNER, n=D_MODEL, k=s)
    dy, dproj, g["ssm_norm_g"] = _rows_vjp_call(
        "gnorm_bwd", _f_gnorm, gn_rows, [w["ssm_norm_g"]], [],
        [(act_t, None), (act_t, (dproj, MAIN_W, ZB_CB))], tm=512, nrows=s, cot_mm=(do_b, w["w_ob"], None))
    dxs, dbm, dcm, ddtr, g["dt_bias"], g["a_log"], g["d_skip"] = _ssd_bwd(
        act, dtr, w["dt_bias"], w["a_log"], w["d_skip"], states, dy, nrows=s)
    dpre, g["conv_b"] = _conv_bwd_act(conv_pre, (dxs, dbm, dcm), tm=1024, nrows=s)
    dproj, g["conv_w"] = _conv_bwd_x(dpre, proj, w["conv_w"], dproj, tm=512, nrows=s)
    dproj, g["ln_a_g"], g["ln_a_b"], g["w_s"], g["b_s"] = _rows_vjp_call(
        "branch_a_bwd", _f_branch_a, [(proj, UVZ_W, UVZ_CB)], a_pars, [],
        [(act_t, (dproj, MAIN_W, UVZ_CB))], tm=512, nrows=s, cot_mm=(do_a, w["w_oa"], None))
    g["w_main"] = mm(h, dproj, mode="tn", name="d_w_main", out_dtype=f32, m=D_MODEL, n=MAIN_W, k=s, tn=MM_TILE // 2)
    g["w_dt"] = mm(h, ddtr, mode="tn", name="d_w_dt", out_dtype=f32, m=D_MODEL, n=DT_W, k=s)
    return loss, g, (dproj, ddtr, dx1)


def _input_grad(x, w, ctx, after=()):
    dproj, ddtr, dx1 = ctx
    s = x.shape[0]
    dh = _matmul(dproj, w["w_main"], mode="nt", name="d_h_main", out_dtype=F32, m=s, n=D_MODEL, k=MAIN_W,
                 tm=MM_TILE // 2, after=after)
    return _rows_vjp_call(
        "pre_norm_bwd", _rms_and_skip, [(x, D_MODEL, 0)], [w["norm_g"]], [(dx1, D_MODEL, 0)],
        [(F32, None)], tm=1024, nrows=s, cot_mm=(ddtr, w["w_dt"], dh))


def _local_step(x, p, tgt, w):
    loss, g, ctx = _forward_backward(x, p, tgt, w)
    grad_x, g["norm_g"] = _input_grad(x, w, ctx)
    return loss, grad_x, g


_O_ZB = 3 * E_A
_O_XBC = _O_ZB + D_INNER
_O_DT = _O_XBC + CONV_DIM
_O_G = _O_DT + N_HEADS


def _heads_to_lanes(v):
    r = v.shape[0]
    v = v.reshape(r, N_GROUPS, HEADS_PER_GROUP)
    return jnp.pad(v, ((0, 0), (0, 0), (0, 128 - HEADS_PER_GROUP))).reshape(r, DT_W)


def _lanes_to_heads(v):
    r = v.shape[0]
    return v.reshape(r, N_GROUPS, 128)[:, :, :HEADS_PER_GROUP].reshape(r, N_HEADS)


def _block_cols(blocks, a, b):
    parts = []
    for k in range(N_CHIPS):
        lo, hi = max(a, k * W_IN_BLOCK), min(b, (k + 1) * W_IN_BLOCK)
        if lo < hi:
            parts.append(blocks[k][:, lo - k * W_IN_BLOCK:hi - k * W_IN_BLOCK])
    return parts


_W_IN_SEGMENTS = ((0, _O_ZB, "m", 0), (_O_ZB, _O_XBC, "m", UVZ_W + XBC_W), (_O_XBC, _O_DT, "m", UVZ_W),
                  (_O_DT, _O_G, "d", 0), (_O_G, N_IN, "m", MAIN_W - G_W))


def _w_in_grad_blocks(gm, gdt):
    blocks = []
    for k in range(N_CHIPS):
        a, b = k * W_IN_BLOCK, (k + 1) * W_IN_BLOCK
        parts = []
        for s, e, src, off in _W_IN_SEGMENTS:
            lo, hi = max(a, s), min(b, e)
            if lo < hi:
                parts.append((gm if src == "m" else gdt)[:, off + lo - s:off + hi - s])
        blocks.append(jnp.concatenate(parts, axis=1))
    return jnp.stack(blocks)


def _layout_weights(f, w_in_blocks=None):
    w = dict(f)
    if w_in_blocks is None:
        w_in = w.pop("w_in")
        w_in_blocks = jnp.stack([w_in[:, k * W_IN_BLOCK:(k + 1) * W_IN_BLOCK] for k in range(N_CHIPS)])
    cols = functools.partial(_block_cols, w_in_blocks)
    w["w_main"] = jnp.concatenate(cols(0, _O_ZB) + cols(_O_XBC, _O_DT) + cols(_O_ZB, _O_XBC) + cols(_O_G, N_IN), axis=1)
    w["w_dt"] = _heads_to_lanes(jnp.concatenate(cols(_O_DT, _O_G), axis=1))
    w["b_s"] = f["b_s"].reshape(G_A, CHUNK, 1)
    for n in ("dt_bias", "a_log", "d_skip"):
        w[n] = _heads_to_lanes(f[n])
    return w


def _natural_grads(g):
    out = dict(g)
    gm = out.pop("w_main")
    gdt = _lanes_to_heads(out.pop("w_dt"))
    out["w_in"] = jnp.concatenate(
        [gm[:, :UVZ_W], gm[:, UVZ_W + XBC_W:UVZ_W + XBC_W + ZB_W], gm[:, UVZ_W:UVZ_W + XBC_W], gdt, gm[:, MAIN_W - G_W:]],
        axis=1)
    out["b_s"] = g["b_s"].reshape(G_A, CHUNK)
    for n in ("dt_bias", "a_log", "d_skip"):
        out[n] = _lanes_to_heads(g[n])
    return out


def _place():
    return lax.axis_index("x"), lax.axis_index("y"), lax.axis_index("c")


def _other_chips(x, y):
    return [(1 - x, y), (x, 1 - y), (1 - x, 1 - y)]


def _rcopy(src, dst, ssem, rsem, dev):
    return pltpu.make_async_remote_copy(src_ref=src, dst_ref=dst, send_sem=ssem, recv_sem=rsem,
                                        device_id=dev, device_id_type=MESH)


def _half(ref_rows, half):
    hs = ref_rows // 2
    return pl.ds(pl.multiple_of(half * hs, 16), hs)


GATHER_CHUNKS = 8


def _gather_weights(shards, conv_shard):
    nw, nq = len(shards), GATHER_CHUNKS

    def body(*refs):
        sh, cv = refs[:nw], refs[nw]
        out, cvo = refs[nw + 1:2 * nw + 1], refs[2 * nw + 1]
        ici_s, ici_r, fw_s, fw_r, own_s, own_r, cv_s, cv_r = refs[2 * nw + 2:]
        x, y, c = _place()
        me, sib, chips = 2 * x + y, (x, y, 1 - c), _other_chips(x, y)
        own = [_rcopy(sh[w], out[w].at[me], own_s.at[w], own_r.at[w], sib) for w in range(nw)]
        own.append(_rcopy(cv, cvo.at[me], own_s.at[nw], own_r.at[nw], sib))
        for cp in own:
            cp.start()
        pieces = [(w, q, j) for w in range(nw) for q in range(nq) for j in range(3)]

        def rows(w, half, q):
            hs = sh[w].shape[0] // 2
            return pl.ds(pl.multiple_of(half * hs + q * (hs // nq), 16), hs // nq)

        def sem(w, q, j):
            return (3 * w + j) * nq + q

        def landed(w, q, j, half):
            return out[w].at[2 * chips[j][0] + chips[j][1], rows(w, half, q)]

        sends = [_rcopy(sh[w].at[rows(w, c, q)], out[w].at[me, rows(w, c, q)], ici_s.at[sem(w, q, j)],
                        ici_r.at[sem(w, q, j)], (*chips[j], c)) for w, q, j in pieces if j < 2]
        for j, chip in enumerate(chips):
            sends.append(_rcopy(cv, cvo.at[me], cv_s.at[j], cv_r.at[j], (*chip, c)))
        for cp in sends:
            cp.start()

        def arrived(w, q, j):
            slab = landed(w, q, j, c)
            _rcopy(slab, slab, ici_s.at[sem(w, q, j)], ici_r.at[sem(w, q, j)], (*chips[j], c)).wait_recv()
            fwd = _rcopy(slab, slab, fw_s.at[sem(w, q, j)], fw_r.at[sem(w, q, j)], sib)
            fwd.start()
            sends.append(fwd)
            if j < 2 and (q % 2 == 0) == (j == 0):
                hop = _rcopy(slab, slab, ici_s.at[sem(w, q, 2)], ici_r.at[sem(w, q, 2)], (*chips[1 - j], c))
                hop.start()
                sends.append(hop)

        for w, q, j in pieces:
            if j < 2:
                arrived(w, q, j)
        for w, q, j in pieces:
            if j == 2:
                arrived(w, q, j)
        for j, chip in enumerate(chips):
            blk = cvo.at[2 * chip[0] + chip[1]]
            _rcopy(blk, blk, cv_s.at[j], cv_r.at[j], (*chip, c)).wait_recv()
        for w, q, j in pieces:
            slab = landed(w, q, j, 1 - c)
            _rcopy(slab, slab, fw_s.at[sem(w, q, j)], fw_r.at[sem(w, q, j)], sib).wait_recv()
        for cp in sends:
            cp.wait_send()
        for cp in own:
            cp.wait()

    dma = pltpu.SemaphoreType.DMA
    n_ici = 3 * nw * nq
    return pl.pallas_call(
        body, name="gather_weights",
        in_specs=[ANY] * (nw + 1), out_specs=[ANY] * (nw + 1),
        out_shape=[jax.ShapeDtypeStruct((N_CHIPS,) + s.shape, s.dtype) for s in shards]
        + [jax.ShapeDtypeStruct((N_CHIPS,) + conv_shard.shape, conv_shard.dtype)],
        scratch_shapes=[dma((n_ici,)), dma((n_ici,)), dma((n_ici,)), dma((n_ici,)), dma((nw + 1,)), dma((nw + 1,)),
                        dma((3,)), dma((3,))],
    )(*shards, conv_shard)


_HBM = pl.BlockSpec(memory_space=pltpu.HBM)
_SEM = pl.BlockSpec(memory_space=pltpu.SEMAPHORE)
_EFFECT = pltpu.SideEffectType.DATAFLOW_SIDE_EFFECTING


def _late_gather_copies(sh, out, s_sem, r_sem):
    x, y, c = _place()
    to = [(*chip, c) for chip in _other_chips(x, y)] + [(x, y, 1 - c)]
    return [_rcopy(sh[w], out[w].at[2 * x + y], s_sem.at[4 * w + j], r_sem.at[4 * w + j], dev)
            for w in range(len(sh)) for j, dev in enumerate(to)]


def _late_gather_start(shards):
    n = len(shards)
    lands = [lax.empty((N_CHIPS,) + a.shape, a.dtype) for a in shards]

    def body(*refs):
        for cp in _late_gather_copies(refs[:n], refs[n:2 * n], refs[2 * n], refs[2 * n + 1]):
            cp.start()
        refs[-1][...] = jnp.zeros_like(refs[-1])

    dma = pltpu.SemaphoreType.DMA
    hbm = [pltpu.with_memory_space_constraint(a, pltpu.HBM) for a in list(shards) + lands]
    out = pl.pallas_call(
        body, name="late_gather_start",
        out_shape=[dma((4 * n,)), dma((4 * n,))] + [pltpu.HBM(a.shape, a.dtype) for a in hbm]
        + [jax.ShapeDtypeStruct((8, 128), F32)],
        in_specs=[_HBM] * (2 * n), out_specs=[_SEM, _SEM] + [_HBM] * (2 * n) + [pl.BlockSpec(memory_space=pltpu.VMEM)],
        input_output_aliases={i: 2 + i for i in range(2 * n)},
        compiler_params=pltpu.CompilerParams(has_side_effects=_EFFECT),
    )(*hbm)
    return out[0], out[1], out[2:2 + n], out[2 + n:2 + 2 * n], out[-1]


def _late_gather_wait(s_sem, r_sem, srcs, lands, after):
    n = len(srcs)

    def body(*refs):
        for cp in _late_gather_copies(refs[:n], refs[n:2 * n], refs[2 * n], refs[2 * n + 1]):
            cp.wait_send()
            cp.wait_recv()

    out = pl.pallas_call(
        body, name="late_gather_wait",
        out_shape=[pltpu.HBM(a.shape, a.dtype) for a in list(srcs) + list(lands)],
        in_specs=[_HBM] * (2 * n) + [_SEM, _SEM, ANY], out_specs=[_HBM] * (2 * n),
        input_output_aliases={i: i for i in range(2 * n)},
        compiler_params=pltpu.CompilerParams(has_side_effects=_EFFECT),
    )(*srcs, *lands, s_sem, r_sem, after)
    return out[n:]


def _swap_with_sibling(arrs):
    n = len(arrs)

    def body(*refs):
        src, dst, s_sem, r_sem = refs[:n], refs[n:2 * n], refs[2 * n], refs[2 * n + 1]
        x, y, c = _place()
        cps = [_rcopy(src[i], dst[i], s_sem.at[i], r_sem.at[i], (x, y, 1 - c)) for i in range(n)]
        for cp in cps:
            cp.start()
        for cp in cps:
            cp.wait()

    dma = pltpu.SemaphoreType.DMA
    return pl.pallas_call(
        body, name="swap_with_sibling", in_specs=[ANY] * n, out_specs=[ANY] * n,
        out_shape=[jax.ShapeDtypeStruct(a.shape, a.dtype) for a in arrs], scratch_shapes=[dma((n,)), dma((n,))],
    )(*arrs)


def _scatter_copies(src, land, s_sem, r_sem):
    x, y, c = _place()
    return [_rcopy(src[i].at[2 * chip[0] + chip[1]], land[i].at[j], s_sem.at[3 * i + j], r_sem.at[3 * i + j], (*chip, c))
            for i in range(len(src)) for j, chip in enumerate(_other_chips(x, y))]


def _scatter_blocks_start(arrs):
    n = len(arrs)
    lands = [lax.empty((3,) + a.shape[1:], a.dtype) for a in arrs]

    def body(*refs):
        src, land, s_sem, r_sem, token = refs[:n], refs[n:2 * n], refs[2 * n], refs[2 * n + 1], refs[-1]
        for cp in _scatter_copies(src, land, s_sem, r_sem):
            cp.start()
        token[...] = jnp.zeros_like(token)

    dma = pltpu.SemaphoreType.DMA
    hbm = [pltpu.with_memory_space_constraint(a, pltpu.HBM) for a in list(arrs) + lands]
    out = pl.pallas_call(
        body, name="scatter_blocks_start",
        out_shape=[dma((3 * n,)), dma((3 * n,))] + [pltpu.HBM(a.shape, a.dtype) for a in hbm]
        + [jax.ShapeDtypeStruct((8, 128), F32)],
        in_specs=[_HBM] * (2 * n), out_specs=[_SEM, _SEM] + [_HBM] * (2 * n) + [pl.BlockSpec(memory_space=pltpu.VMEM)],
        input_output_aliases={i: 2 + i for i in range(2 * n)},
        compiler_params=pltpu.CompilerParams(has_side_effects=_EFFECT),
    )(*hbm)
    return out[0], out[1], out[2:2 + n], out[2 + n:2 + 2 * n], out[-1]


def _scatter_blocks_wait(s_sem, r_sem, srcs, lands, after):
    n = len(srcs)

    def body(*refs):
        src, land, s_sem, r_sem = refs[:n], refs[n:2 * n], refs[2 * n], refs[2 * n + 1]
        for cp in _scatter_copies(src, land, s_sem, r_sem):
            cp.wait_send()
            cp.wait_recv()

    out = pl.pallas_call(
        body, name="scatter_blocks_wait",
        out_shape=[pltpu.HBM(a.shape, a.dtype) for a in list(srcs) + list(lands)],
        in_specs=[_HBM] * (2 * n) + [_SEM, _SEM] + [ANY] * len(after), out_specs=[_HBM] * (2 * n),
        input_output_aliases={i: i for i in range(2 * n)},
        compiler_params=pltpu.CompilerParams(has_side_effects=_EFFECT),
    )(*srcs, *lands, s_sem, r_sem, *after)
    return out[:n], out[n:]


def _share_halves(arrs):
    n = len(arrs)

    def body(*refs):
        buf, s_sem, r_sem = refs[n:2 * n], refs[2 * n], refs[2 * n + 1]
        x, y, c = _place()
        cps = []
        for i in range(n):
            mine = buf[i].at[_half(buf[i].shape[0], c)]
            cps.append(_rcopy(mine, mine, s_sem.at[i], r_sem.at[i], (x, y, 1 - c)))
        for cp in cps:
            cp.start()
        for i in range(n):
            theirs = buf[i].at[_half(buf[i].shape[0], 1 - c)]
            _rcopy(theirs, theirs, s_sem.at[i], r_sem.at[i], (x, y, 1 - c)).wait_recv()
        for cp in cps:
            cp.wait_send()

    dma = pltpu.SemaphoreType.DMA
    return pl.pallas_call(
        body, name="share_halves", in_specs=[ANY] * n, out_specs=[ANY] * n,
        out_shape=[jax.ShapeDtypeStruct(a.shape, a.dtype) for a in arrs],
        input_output_aliases={i: i for i in range(n)}, scratch_shapes=[dma((n,)), dma((n,))],
    )(*arrs)


def _small_gather_copies(src, land, s_sem, r_sem):
    x, y, c = _place()
    cps = []
    for d in range(1, N_DEV):
        peer = ((1 - x) if d & 4 else x), ((1 - y) if d & 2 else y), ((1 - c) if d & 1 else c)
        cps.append(_rcopy(src, land.at[4 * x + 2 * y + c], s_sem.at[d - 1], r_sem.at[d - 1], peer))
    return cps


def _small_gather_start(packed):
    def body(src, land, s_sem, r_sem, _, __, token):
        for cp in _small_gather_copies(src, land, s_sem, r_sem):
            cp.start()
        token[...] = jnp.zeros_like(token)

    dma = pltpu.SemaphoreType.DMA
    hbm = [pltpu.with_memory_space_constraint(a, pltpu.HBM) for a in (packed, lax.empty((N_DEV,) + packed.shape, F32))]
    return pl.pallas_call(
        body, name="small_gather_start",
        out_shape=[dma((N_DEV - 1,)), dma((N_DEV - 1,))] + [pltpu.HBM(a.shape, a.dtype) for a in hbm]
        + [jax.ShapeDtypeStruct((8, 128), F32)],
        in_specs=[_HBM] * 2, out_specs=[_SEM, _SEM, _HBM, _HBM, pl.BlockSpec(memory_space=pltpu.VMEM)],
        input_output_aliases={0: 2, 1: 3}, compiler_params=pltpu.CompilerParams(has_side_effects=_EFFECT),
    )(*hbm)


def _small_gather_wait(s_sem, r_sem, src, land, after):
    def body(src, land, s_sem, r_sem, *_):
        for cp in _small_gather_copies(src, land, s_sem, r_sem):
            cp.wait_send()
            cp.wait_recv()

    return pl.pallas_call(
        body, name="small_gather_wait", out_shape=[pltpu.HBM(src.shape, src.dtype), pltpu.HBM(land.shape, land.dtype)],
        in_specs=[_HBM, _HBM, _SEM, _SEM, ANY], out_specs=[_HBM, _HBM], input_output_aliases={0: 0, 1: 1},
        compiler_params=pltpu.CompilerParams(has_side_effects=_EFFECT),
    )(src, land, s_sem, r_sem, after)


def _small_sum(own, land, dev_arr):
    def body(me_ref, own_ref, land_ref, o_ref):
        acc = jnp.zeros(o_ref.shape, F32)
        for d in range(N_DEV):
            acc = acc + jnp.where(me_ref[0] == d, own_ref[...], land_ref[d])
        o_ref[...] = acc

    return pl.pallas_call(
        body, name="small_sum", out_shape=jax.ShapeDtypeStruct(own.shape, F32),
        grid_spec=pltpu.PrefetchScalarGridSpec(
            num_scalar_prefetch=1, grid=(1,),
            in_specs=[pl.BlockSpec(own.shape, lambda i, m: (0, 0)), pl.BlockSpec(land.shape, lambda i, m: (0, 0, 0))],
            out_specs=pl.BlockSpec(own.shape, lambda i, m: (0, 0))),
    )(dev_arr, own, land)


def _row_tile(rows, cols):
    tr = max(8, min(rows, (1 << 20) // (4 * cols) // 8 * 8))
    while rows % tr:
        tr -= 8
    return tr


def _chip_sum(name, g5, recv, c_arr):
    nb, _, hs, cols = g5.shape
    tr = _row_tile(hs, cols)

    def body(_, a_ref, b_ref, o_ref):
        o_ref[...] = (a_ref[...] + b_ref[...].astype(F32)).astype(o_ref.dtype)

    blk = pl.BlockSpec((None, tr, cols), lambda b, i, c: (b, i, 0))
    return pl.pallas_call(
        body, name=name,
        grid_spec=pltpu.PrefetchScalarGridSpec(
            num_scalar_prefetch=1, grid=(nb, hs // tr),
            in_specs=[pl.BlockSpec((None, None, tr, cols), lambda b, i, c: (b, c[0], i, 0)), blk], out_specs=blk),
        out_shape=jax.ShapeDtypeStruct((nb, hs, cols), WIRE_DTYPE),
    )(c_arr, g5, recv)


def _final_sum(name, own, recv, place_arr):
    _, hs, cols = own.shape
    tr = _row_tile(hs, cols)
    nt = hs // tr

    def body(_, a_ref, r_ref, o_ref):
        o_ref[...] = ((a_ref[...].astype(F32) + r_ref[0].astype(F32)) + r_ref[1].astype(F32)) + r_ref[2].astype(F32)

    return pl.pallas_call(
        body, name=name,
        grid_spec=pltpu.PrefetchScalarGridSpec(
            num_scalar_prefetch=1, grid=(nt,),
            in_specs=[pl.BlockSpec((None, tr, cols), lambda i, m: (m[0], i, 0)),
                      pl.BlockSpec((3, tr, cols), lambda i, m: (0, i, 0))],
            out_specs=pl.BlockSpec((tr, cols), lambda i, m: (m[1] * nt + i, 0))),
        out_shape=jax.ShapeDtypeStruct((2 * hs, cols), F32),
    )(place_arr, own, recv)


def _adamw(w, g, m, v):
    m = ADAM_B1 * m + (1.0 - ADAM_B1) * g
    v = ADAM_B2 * v + (1.0 - ADAM_B2) * (g * g)
    m_hat = m / (1.0 - ADAM_B1 ** ADAM_STEP)
    v_hat = v / (1.0 - ADAM_B2 ** ADAM_STEP)
    return -ADAM_LR * (m_hat / (jnp.sqrt(v_hat) + ADAM_EPS) + ADAM_WD * w), m, v


def _adamw_call(name, w, g, m, v):
    rows, cols = w.shape
    tr = _row_tile(rows, cols)
    if 4 * tr * cols >= (1 << 18):
        blk, steps = pl.BlockSpec((tr, cols), lambda i: (i, 0)), rows // tr
    else:
        blk, steps = pl.BlockSpec((rows, 128), lambda i: (0, i)), cols // 128

    def body(w_ref, g_ref, m_ref, v_ref, d_ref, nm_ref, nv_ref, go_ref):
        g = g_ref[...]
        d_ref[...], nm_ref[...], nv_ref[...] = _adamw(w_ref[...], g, m_ref[...], v_ref[...])
        go_ref[...] = g

    return pl.pallas_call(
        body, name=name, grid=(steps,), in_specs=[blk] * 4, out_specs=[blk] * 4,
        out_shape=[jax.ShapeDtypeStruct(w.shape, F32)] * 4,
        compiler_params=pltpu.CompilerParams(dimension_semantics=("parallel",)),
    )(w, g, m, v)


def _adamw_small(ws, gs, ms, vs):
    n = len(ws)

    def body(*refs):
        for i in range(n):
            w_ref, g_ref, m_ref, v_ref = (refs[k * n + i] for k in range(4))
            d, nm, nv = _adamw(w_ref[...], g_ref[...], m_ref[...], v_ref[...])
            refs[4 * n + i][...] = d
            refs[5 * n + i][...] = nm
            refs[6 * n + i][...] = nv

    out = pl.pallas_call(
        body, name="adamw_small", out_shape=[jax.ShapeDtypeStruct(a.shape, F32) for a in ws] * 3,
    )(*ws, *gs, *ms, *vs)
    return out[:n], out[n:2 * n], out[2 * n:]


_BIG = ("w_in", "w_oa", "w_ob", "w_out", "w_pg", "w_ple")
_SMALL = ("norm_g", "ln_a_g", "ln_a_b", "w_s", "b_s", "conv_w", "conv_b", "dt_bias", "a_log", "d_skip", "ssm_norm_g",
          "ple_norm_g", "final_g")
_WEIGHTS = ("norm_g", "w_in", "ln_a_g", "ln_a_b", "w_s", "b_s", "conv_w", "conv_b", "dt_bias", "a_log", "d_skip",
            "ssm_norm_g", "w_oa", "w_ob", "w_out", "ple_norm_g", "w_pg", "w_ple", "final_g")
_COL_SHARDED = ("w_in", "w_ple")
_PACK = 1024


def _blocks_to_full(col_sharded, blocks):
    if col_sharded:
        return jnp.concatenate([blocks[k] for k in range(N_CHIPS)], axis=1)
    return blocks.reshape(N_CHIPS * blocks.shape[1], blocks.shape[2])


def _full_to_blocks(col_sharded, full):
    if col_sharded:
        w = full.shape[1] // N_CHIPS
        return jnp.stack([full[:, k * w:(k + 1) * w] for k in range(N_CHIPS)])
    return full.reshape(N_CHIPS, full.shape[0] // N_CHIPS, full.shape[1])


def _two_d(n, a):
    if n == "w_s":
        return a.reshape(G_A * CHUNK, CHUNK)
    if n in ("b_s", "conv_w"):
        return a.reshape(a.shape[-2], a.shape[-1])
    return a.reshape(1, a.shape[-1])


def kernel(x, p, norm_g, w_in, ln_a_g, ln_a_b, w_s, b_s, conv_w, conv_b, dt_bias, a_log, d_skip, ssm_norm_g, w_oa, w_ob, w_out, ple_norm_g, w_pg, w_ple, final_g, loss_target, m_norm_g, m_w_in, m_ln_a_g, m_ln_a_b, m_w_s, m_b_s, m_conv_w, m_conv_b, m_dt_bias, m_a_log, m_d_skip, m_ssm_norm_g, m_w_oa, m_w_ob, m_w_out, m_ple_norm_g, m_w_pg, m_w_ple, m_final_g, v_norm_g, v_w_in, v_ln_a_g, v_ln_a_b, v_w_s, v_b_s, v_conv_w, v_conv_b, v_dt_bias, v_a_log, v_d_skip, v_ssm_norm_g, v_w_oa, v_w_ob, v_w_out, v_ple_norm_g, v_w_pg, v_w_ple, v_final_g):
    wt = dict(norm_g=norm_g, w_in=w_in, ln_a_g=ln_a_g, ln_a_b=ln_a_b, w_s=w_s, b_s=b_s, conv_w=conv_w, conv_b=conv_b,
              dt_bias=dt_bias, a_log=a_log, d_skip=d_skip, ssm_norm_g=ssm_norm_g, w_oa=w_oa, w_ob=w_ob, w_out=w_out,
              ple_norm_g=ple_norm_g, w_pg=w_pg, w_ple=w_ple, final_g=final_g)
    mom = dict(norm_g=m_norm_g, w_in=m_w_in, ln_a_g=m_ln_a_g, ln_a_b=m_ln_a_b, w_s=m_w_s, b_s=m_b_s, conv_w=m_conv_w,
               conv_b=m_conv_b, dt_bias=m_dt_bias, a_log=m_a_log, d_skip=m_d_skip, ssm_norm_g=m_ssm_norm_g, w_oa=m_w_oa,
               w_ob=m_w_ob, w_out=m_w_out, ple_norm_g=m_ple_norm_g, w_pg=m_w_pg, w_ple=m_w_ple, final_g=m_final_g)
    vel = dict(norm_g=v_norm_g, w_in=v_w_in, ln_a_g=v_ln_a_g, ln_a_b=v_ln_a_b, w_s=v_w_s, b_s=v_b_s, conv_w=v_conv_w,
               conv_b=v_conv_b, dt_bias=v_dt_bias, a_log=v_a_log, d_skip=v_d_skip, ssm_norm_g=v_ssm_norm_g, w_oa=v_w_oa,
               w_ob=v_w_ob, w_out=v_w_out, ple_norm_g=v_ple_norm_g, w_pg=v_w_pg, w_ple=v_w_ple, final_g=v_final_g)
    xi, yi, ci = _place()
    me = 2 * xi + yi
    c_arr = jnp.reshape(ci, (1,)).astype(jnp.int32)
    place_arr = jnp.stack([me, ci]).astype(jnp.int32)

    shard = {n: wt[n][0] for n in _BIG}
    wire = {n: shard[n].astype(WIRE_DTYPE) for n in _BIG}
    w_in_blocks, conv_blocks = _gather_weights([wire["w_in"]], conv_w[0])
    g_ssem, g_rsem, g_sent, g_lands, g_token = _late_gather_start([wire[n] for n in _BIG[1:]])
    full = {"conv_w": _blocks_to_full(True, conv_blocks)}
    for n in _SMALL:
        if n != "conv_w":
            full[n] = wt[n][0] if wt[n].ndim > 2 else wt[n].reshape(1, wt[n].shape[-1])

    def late_weights(after):
        blocks = _late_gather_wait(g_ssem, g_rsem, g_sent, g_lands, after)
        return {n: _blocks_to_full(n in _COL_SHARDED, b) for n, b in zip(_BIG[1:], blocks)}

    w = _layout_weights(full, w_in_blocks=w_in_blocks)
    loss_row, g, ctx = _forward_backward(x[0], p[0, 0], loss_target[0], w, late_weights, after=(g_token,))

    parts = {n: _full_to_blocks(n in _COL_SHARDED, g[n]) for n in _BIG[1:]}
    parts["w_main"] = g["w_main"][None]
    parts["w_dt"] = jnp.pad(_lanes_to_heads(g["w_dt"]), ((0, 0), (0, 128 - N_HEADS)))[None]
    names = ("w_main", "w_dt") + _BIG[1:]
    g5 = {n: parts[n].reshape(parts[n].shape[0], 2, parts[n].shape[1] // 2, parts[n].shape[2]) for n in names}
    to_sibling = [lax.dynamic_index_in_dim(g5[n], 1 - ci, axis=1, keepdims=False).astype(WIRE_DTYPE) for n in names]
    from_sibling = _swap_with_sibling(to_sibling)
    chip = {n: _chip_sum("chip_sum_" + n, g5[n], r, c_arr) for n, r in zip(names, from_sibling)}
    chip["w_in"] = _w_in_grad_blocks(chip["w_main"][0], chip["w_dt"][0])
    chip_wire = [chip[n] for n in _BIG]
    s_sem, r_sem, sent, lands, token = _scatter_blocks_start(chip_wire)
    grad_x, g["norm_g"] = _input_grad(x[0], w, ctx, after=(token,))
    g = _natural_grads(g)

    pieces = [_two_d(n, g[n]).reshape(-1) for n in _SMALL] + [loss_row[0, :1]]
    sizes = [v.shape[0] for v in pieces]
    padded = [-(-s // _PACK) * _PACK for s in sizes]
    packed = jnp.concatenate([jnp.pad(v, (0, ps - s)) for v, s, ps in zip(pieces, sizes, padded)]).reshape(-1, 128)
    a_ssem, a_rsem, a_src, a_land, a_token = _small_gather_start(packed)

    sent, from_chips = _scatter_blocks_wait(s_sem, r_sem, sent, lands, (grad_x, a_token))
    halves = [_final_sum("final_sum_" + n, a, r, place_arr) for n, a, r in zip(_BIG, sent, from_chips)]
    grads = dict(zip(_BIG, _share_halves(halves)))
    delta, new_m, new_v = {}, {}, {}
    for n in _BIG:
        t = jnp.transpose if n == "w_in" else (lambda a: a)
        res = _adamw_call("adamw_" + n, t(shard[n]), t(grads[n]), t(mom[n][0]), t(vel[n][0]))
        delta[n], new_m[n], new_v[n], grads[n] = (t(r) for r in res)

    a_src, a_land = _small_gather_wait(a_ssem, a_rsem, a_src, a_land, delta["w_in"])
    summed = _small_sum(a_src, a_land, jnp.reshape(4 * xi + 2 * yi + ci, (1,)).astype(jnp.int32)).reshape(-1)
    off = 0
    for n, s, ps in zip(_SMALL, sizes, padded):
        grads[n] = summed[off:off + s].reshape(_two_d(n, g[n]).shape)
        off += ps
    loss = summed[off]
    grads["conv_w"] = lax.dynamic_slice_in_dim(grads["conv_w"], me * (CONV_DIM // N_CHIPS), CONV_DIM // N_CHIPS, axis=1)
    small = _adamw_small([_two_d(n, wt[n]) for n in _SMALL], [grads[n] for n in _SMALL],
                         [_two_d(n, mom[n]) for n in _SMALL], [_two_d(n, vel[n]) for n in _SMALL])
    for i, n in enumerate(_SMALL):
        delta[n], new_m[n], new_v[n] = small[0][i], small[1][i], small[2][i]

    def shaped(d):
        return [d[n].reshape(wt[n].shape) for n in _WEIGHTS]

    return (loss, grad_x[None], *shaped(grads), *shaped(delta), *shaped(new_m), *shaped(new_v))
```

```python
import functools

import jax
import jax.numpy as jnp
from jax import lax
from jax.experimental import pallas as pl
from jax.experimental.pallas import tpu as pltpu

F32 = jnp.float32
MXU_DTYPE = jnp.bfloat16
ACT_DTYPE = jnp.bfloat16
WIRE_DTYPE = jnp.bfloat16

D_MODEL = 1024
PLE_DIM = 256
CHUNK = 128
EPS = 1e-6
E_A = D_MODEL
G_A = 4
D_INNER = 2 * D_MODEL
HEAD_DIM = 64
N_HEADS = D_INNER // HEAD_DIM
N_STATE = 128
N_GROUPS = 4
HEADS_PER_GROUP = N_HEADS // N_GROUPS
PAIRS_PER_GROUP = HEADS_PER_GROUP // 2
CONV_K = 4
CONV_DIM = D_INNER + 2 * N_GROUPS * N_STATE
N_IN = 3 * E_A + D_INNER + CONV_DIM + N_HEADS + 2 * D_MODEL
N_CHIPS = 4
N_DEV = 8
W_IN_BLOCK = N_IN // N_CHIPS

UVZ_W, XBC_W, ZB_W, G_W = 3 * E_A, CONV_DIM, D_INNER, 2 * D_MODEL
MAIN_W = UVZ_W + XBC_W + ZB_W + G_W
UVZ_CB, XBC_CB, ZB_CB, G_CB = 0, 1, 3, 4
DT_W = N_GROUPS * 128

ADAM_LR, ADAM_B1, ADAM_B2, ADAM_EPS, ADAM_WD, ADAM_STEP = 0.001, 0.9, 0.999, 1e-08, 0.01, 10

MESH = pl.DeviceIdType.MESH
ANY = pl.BlockSpec(memory_space=pl.ANY)


def _mxu(v):
    return v.astype(MXU_DTYPE)


def _dot(a, b, dims=(((1,), (0,)), ((), ()))):
    return lax.dot_general(_mxu(a), _mxu(b), dims, preferred_element_type=F32)


V7X_MXU_WIDTH = 256
V7X_SCOPED_VMEM_BYTES = 60000 * 1024
MM_TILE = 4 * V7X_MXU_WIDTH
MM_VMEM_BUDGET = 4 * V7X_SCOPED_VMEM_BYTES // 5


def _mm_tk(m, n, k, tm, tn, a_bytes, b_bytes, out_bytes, extra_bytes):
    one_tile = m == tm and n == tn
    for parts in range(2 if one_tile else 1, k // 128 + 1):
        if k % parts or (k // parts) % 128 and parts > 1:
            continue
        tk = k // parts
        a_bufs = 1 if (parts == 1 and m == tm) else 2
        b_bufs = 1 if (parts == 1 and n == tn) else 2
        need = (tk * (a_bufs * tm * a_bytes + b_bufs * tn * b_bytes) + 2 * tm * tn * (out_bytes + extra_bytes)
                + (tm * tn * 4 if parts > 1 else 0))
        if need <= MM_VMEM_BUDGET:
            return tk
    return 128


def _matmul(a, b, *, mode, name, out_dtype, m, n, k, tm=MM_TILE, tn=MM_TILE, tk=None, a_off=0, b_off=0,
            extras=(), epilogue=None, after=()):
    tm, tn = min(tm, m), min(tn, n)
    if tk is None:
        tk = _mm_tk(m, n, k, tm, tn, a.dtype.itemsize, b.dtype.itemsize, jnp.dtype(out_dtype).itemsize,
                    sum(e.dtype.itemsize for e in extras))
    tk = min(tk, k)
    assert m % tm == 0 and n % tn == 0 and k % tk == 0, (name, m, n, k, tm, tn, tk)
    nk = k // tk
    a_mode = pl.Buffered(1) if (nk == 1 and m == tm) else None
    b_mode = pl.Buffered(1) if (nk == 1 and n == tn) else None
    if mode == "nn":
        assert a_off % tk == 0 and b_off % tn == 0
        a_spec = pl.BlockSpec((tm, tk), lambda i, j, kk: (i, kk + a_off // tk), pipeline_mode=a_mode)
        b_spec = pl.BlockSpec((tk, tn), lambda i, j, kk: (kk, j + b_off // tn), pipeline_mode=b_mode)
        dims = (((1,), (0,)), ((), ()))
    elif mode == "nt":
        a_spec = pl.BlockSpec((tm, tk), lambda i, j, kk: (i, kk), pipeline_mode=a_mode)
        b_spec = pl.BlockSpec((tn, tk), lambda i, j, kk: (j, kk), pipeline_mode=b_mode)
        dims = (((1,), (1,)), ((), ()))
    else:
        assert a_off % tm == 0 and b_off % tn == 0
        a_spec = pl.BlockSpec((tk, tm), lambda i, j, kk: (kk, i + a_off // tm), pipeline_mode=a_mode)
        b_spec = pl.BlockSpec((tk, tn), lambda i, j, kk: (kk, j + b_off // tn), pipeline_mode=b_mode)
        dims = (((0,), (0,)), ((), ()))
    ne = len(extras)

    def finish(acc, extra_refs, o_ref):
        res = acc if epilogue is None else epilogue(acc, *[e[...] for e in extra_refs])
        o_ref[...] = res.astype(o_ref.dtype)

    def body(a_ref, b_ref, *rest):
        extra_refs, o_ref = rest[:ne], rest[ne + len(after)]
        part = _dot(a_ref[...], b_ref[...], dims)
        if nk == 1:
            finish(part, extra_refs, o_ref)
            return
        acc_ref = rest[ne + len(after) + 1]
        kk = pl.program_id(2)

        @pl.when(kk == 0)
        def _():
            acc_ref[...] = part

        @pl.when(kk > 0)
        def _():
            acc_ref[...] += part

        @pl.when(kk == nk - 1)
        def _():
            finish(acc_ref[...], extra_refs, o_ref)

    o_spec = pl.BlockSpec((tm, tn), lambda i, j, kk: (i, j))
    return pl.pallas_call(
        body, name=name, grid=(m // tm, n // tn, nk),
        in_specs=[a_spec, b_spec] + [o_spec] * ne + [ANY] * len(after), out_specs=o_spec,
        out_shape=jax.ShapeDtypeStruct((m, n), out_dtype),
        scratch_shapes=[pltpu.VMEM((tm, tn), F32)] if nk > 1 else [],
        compiler_params=pltpu.CompilerParams(dimension_semantics=("parallel", "parallel", "arbitrary")),
    )(a, b, *extras, *after)


def _rows_matmul(name, f, rows, pars, b, *, out_dtype, n, k, tm, nrows, tn=MM_TILE, extras=(), epilogue=None, after=(),
                 side=None):
    tm, tn = min(tm, nrows), min(tn, n)
    assert nrows % tm == 0 and n % tn == 0, (name, nrows, n, tm, tn)
    nr, npar, ne, nj = len(rows), len(pars), len(extras), n // tn
    ns = 0 if side is None else 1
    n_in = nr + npar + 1 + ne + ns + len(after)

    def body(*refs):
        row_refs, par_refs, b_ref = refs[:nr], refs[nr:nr + npar], refs[nr + npar]
        extra_refs = refs[nr + npar + 1:nr + npar + 1 + ne]
        a_ref, o_ref = refs[n_in], refs[n_in + 1]

        def make_a():
            a = f(*[r[...].astype(F32) for r in row_refs], *[p[...] for p in par_refs])[0]
            a_ref[...] = a.astype(a_ref.dtype)
            if ns:
                refs[n_in + 2][...] = _dot(a_ref[...], refs[nr + npar + 1 + ne][...]).astype(refs[n_in + 2].dtype)

        if nj == 1:
            make_a()
        else:
            pl.when(pl.program_id(1) == 0)(make_a)
        res = _dot(a_ref[...], b_ref[...])
        if epilogue is not None:
            res = epilogue(res, *[e[...] for e in extra_refs])
        o_ref[...] = res.astype(o_ref.dtype)

    o_spec = pl.BlockSpec((tm, tn), lambda i, j: (i, j))
    side_in = [] if side is None else [pl.BlockSpec(tuple(side[0].shape), lambda i, j: (0, 0))]
    side_out = [] if side is None else [pl.BlockSpec((tm, side[0].shape[1]), lambda i, j: (i, 0))]
    side_shape = [] if side is None else [jax.ShapeDtypeStruct((nrows, side[0].shape[1]), side[1])]
    return pl.pallas_call(
        body, name=name, grid=(nrows // tm, nj),
        in_specs=[pl.BlockSpec((tm, w), lambda i, j, cb=cb: (i, cb)) for _, w, cb in rows]
        + [pl.BlockSpec(tuple(p.shape), lambda i, j, nd=p.ndim: (0,) * nd) for p in pars]
        + [pl.BlockSpec((k, tn), lambda i, j: (0, j))] + [o_spec] * ne + side_in + [ANY] * len(after),
        out_specs=[pl.BlockSpec((tm, k), lambda i, j: (i, 0)), o_spec] + side_out,
        out_shape=[jax.ShapeDtypeStruct((nrows, k), ACT_DTYPE), jax.ShapeDtypeStruct((nrows, n), out_dtype)] + side_shape,
        compiler_params=pltpu.CompilerParams(dimension_semantics=("parallel", "arbitrary")),
    )(*[r[0] for r in rows], *pars, b, *extras, *([] if side is None else [side[0]]), *after)


def _row_spec(tm, width, cb):
    return pl.BlockSpec((tm, width), lambda i: (i, cb))


def _whole_spec(shape):
    nd = len(shape)
    return pl.BlockSpec(tuple(shape), lambda i: (0,) * nd)


def _rows_vjp_call(name, f, rows, pars, cots, drows, *, tm, nrows, cot_mm=None):
    tm = min(tm, nrows)
    nr, npar, nc = len(rows), len(pars), len(cots)
    mm_args, mm_specs = [], []
    if cot_mm is not None:
        mm_a, mm_b, mm_add = cot_mm
        mm_args = [mm_a, mm_b] + ([] if mm_add is None else [mm_add])
        mm_specs = [_row_spec(tm, mm_a.shape[1], 0), _whole_spec(mm_b.shape)]
        mm_specs += [] if mm_add is None else [_row_spec(tm, mm_b.shape[0], 0)]
    alias_bufs, aliases = [], {}
    out_shape, out_specs = [], []
    for (arr, w, cb), d in zip(rows, drows):
        if d is None:
            continue
        dt, into = d
        if into is None:
            out_shape.append(jax.ShapeDtypeStruct((nrows, w), dt))
            out_specs.append(_row_spec(tm, w, 0))
        else:
            buf, total, ocb = into
            if buf is not None:
                aliases[nr + npar + nc + len(alias_bufs)] = len(out_shape)
                alias_bufs.append(buf)
            out_shape.append(jax.ShapeDtypeStruct((nrows, total), dt))
            out_specs.append(_row_spec(tm, w, ocb))
    n_drow = len(out_shape)
    for p in pars:
        out_shape.append(jax.ShapeDtypeStruct(p.shape, F32))
        out_specs.append(_whole_spec(p.shape))
    na = len(alias_bufs)

    def body(*refs):
        rv = [r[...].astype(F32) for r in refs[:nr]]
        pv = [p[...] for p in refs[nr:nr + npar]]
        cv = tuple(c[...].astype(F32) for c in refs[nr + npar:nr + npar + nc])
        o_refs = refs[nr + npar + nc + na + len(mm_args):]
        if mm_args:
            mm_refs = refs[nr + npar + nc + na:nr + npar + nc + na + len(mm_args)]
            c0 = _dot(mm_refs[0][...], mm_refs[1][...], (((1,), (1,)), ((), ())))
            if len(mm_refs) == 3:
                c0 = c0 + mm_refs[2][...].astype(F32)
            cv = (c0,) + cv
        _, vjp = jax.vjp(f, *rv, *pv)
        g = vjp(cv)
        oi = 0
        for ri, d in enumerate(drows):
            if d is not None:
                o_refs[oi][...] = g[ri].astype(o_refs[oi].dtype)
                oi += 1
        first = pl.program_id(0) == 0
        for pi in range(npar):
            acc = o_refs[n_drow + pi]

            @pl.when(first)
            def _(acc=acc):
                acc[...] = jnp.zeros_like(acc)

            acc[...] += g[nr + pi]

    return pl.pallas_call(
        body, name=name, grid=(nrows // tm,),
        in_specs=[_row_spec(tm, w, cb) for _, w, cb in rows] + [_whole_spec(p.shape) for p in pars]
        + [_row_spec(tm, w, cb) for _, w, cb in cots] + [ANY] * na + mm_specs,
        out_specs=out_specs, out_shape=out_shape, input_output_aliases=aliases,
        compiler_params=pltpu.CompilerParams(dimension_semantics=("arbitrary",)),
    )(*[r[0] for r in rows], *pars, *[c[0] for c in cots], *alias_bufs, *mm_args)


def _rms(x, g):
    return x * lax.rsqrt(jnp.mean(x * x, axis=-1, keepdims=True) + EPS) * g


def _f_rms(x, g):
    return (_rms(x, g),)


def _tril_mask():
    return lax.broadcasted_iota(jnp.int32, (CHUNK, CHUNK), 0) >= lax.broadcasted_iota(jnp.int32, (CHUNK, CHUNK), 1)


def _f_branch_a(uvz, ln_g, ln_b, w_s, b_s):
    u = jax.nn.gelu(uvz[:, :E_A])
    v = jax.nn.gelu(uvz[:, E_A:2 * E_A])
    z = uvz[:, 2 * E_A:]
    xc = v - jnp.mean(v, axis=-1, keepdims=True)
    vn = xc * lax.rsqrt(jnp.mean(xc * xc, axis=-1, keepdims=True) + EPS) * ln_g + ln_b
    mask = _tril_mask()
    ws = [jnp.where(mask, w_s[g], 0.0) for g in range(G_A)]
    gw = E_A // G_A
    rows = []
    for c in range(uvz.shape[0] // CHUNK):
        vc = vn[c * CHUNK:(c + 1) * CHUNK]
        rows.append(jnp.concatenate([_dot(ws[g], vc[:, g * gw:(g + 1) * gw]) + b_s[g] for g in range(G_A)], axis=1))
    sv = rows[0] if len(rows) == 1 else jnp.concatenate(rows, axis=0)
    return (u * sv * jax.nn.silu(z),)


def _f_gnorm(y, zb, g):
    yz = y * jax.nn.silu(zb)
    gw = D_INNER // N_GROUPS
    parts = []
    for i in range(N_GROUPS):
        s = yz[:, i * gw:(i + 1) * gw]
        parts.append(s * lax.rsqrt(jnp.mean(s * s, axis=-1, keepdims=True) + EPS))
    return (jnp.concatenate(parts, axis=1) * g,)


def _f_merge(g2, oa, ob):
    return (jax.nn.sigmoid(g2[:, :D_MODEL]) * oa + jax.nn.sigmoid(g2[:, D_MODEL:]) * ob,)


def _f_loss(x1, gp, pe, tgt, fg):
    x2 = x1 + jax.nn.sigmoid(gp) * pe
    err = _rms(x2, fg) - tgt
    return 0.5 * jnp.sum(jnp.mean(err * err, axis=-1))


def _head(x1, p, tgt, ple_g, w_pg, w_ple, fg, *, tm, nrows):
    tm = min(tm, nrows)

    def body(x1_ref, p_ref, t_ref, pg_ref, wpg_ref, wple_ref, fg_ref, hp_ref, dx_ref, dgp_ref, dpe_ref, dfg_ref, loss_ref):
        x1 = x1_ref[...]
        hp_ref[...] = _rms(x1, pg_ref[...]).astype(hp_ref.dtype)
        gp = _dot(hp_ref[...], wpg_ref[...])
        pe = _dot(p_ref[...], wple_ref[...])
        loss, vjp = jax.vjp(_f_loss, x1, gp, pe, t_ref[...], fg_ref[...])
        dx, dgp, dpe, _, dfg = vjp(jnp.ones((), F32))
        dx_ref[...] = dx
        dgp_ref[...] = dgp.astype(dgp_ref.dtype)
        dpe_ref[...] = dpe.astype(dpe_ref.dtype)

        @pl.when(pl.program_id(0) == 0)
        def _():
            dfg_ref[...] = jnp.zeros_like(dfg_ref)
            loss_ref[...] = jnp.zeros_like(loss_ref)

        dfg_ref[...] += dfg
        loss_ref[...] += jnp.full(loss_ref.shape, loss, F32)

    row = _row_spec(tm, D_MODEL, 0)
    act = jax.ShapeDtypeStruct((nrows, D_MODEL), ACT_DTYPE)
    return pl.pallas_call(
        body, name="head", grid=(nrows // tm,),
        in_specs=[row, _row_spec(tm, PLE_DIM, 0), row, _whole_spec((1, D_MODEL)), _whole_spec(w_pg.shape),
                  _whole_spec(w_ple.shape), _whole_spec((1, D_MODEL))],
        out_specs=[row, row, row, row, _whole_spec((1, D_MODEL)), _whole_spec((1, 128))],
        out_shape=[act, jax.ShapeDtypeStruct((nrows, D_MODEL), F32), act, act, jax.ShapeDtypeStruct((1, D_MODEL), F32),
                   jax.ShapeDtypeStruct((1, 128), F32)],
        compiler_params=pltpu.CompilerParams(dimension_semantics=("arbitrary",)),
    )(x1, p, tgt, ple_g, w_pg, w_ple, fg)


def _shift_rows(cur, edge, j, up):
    tm = cur.shape[0]
    row = lax.broadcasted_iota(jnp.int32, cur.shape, 0)
    if up:
        sh = pltpu.roll(cur, tm - j, 0)
        e = jnp.tile(pltpu.roll(edge, 8 - j, 0), (tm // 8, 1))
        return jnp.where(row >= tm - j, e, sh)
    sh = pltpu.roll(cur, j, 0)
    e = jnp.tile(pltpu.roll(edge, j, 0), (tm // 8, 1))
    return jnp.where(row < j, e, sh)


def _conv_pre(cur, prev, w, b):
    acc = cur * w[CONV_K - 1:CONV_K] + b
    for j in range(1, CONV_K):
        acc = acc + _shift_rows(cur, prev, j, up=False) * w[CONV_K - 1 - j:CONV_K - j]
    return acc


def _halo_specs(tm, nrows, cb, before):
    nb = tm // 8
    last = nrows // 8 - 1
    if before:
        return pl.BlockSpec((8, XBC_W), lambda i: (jnp.maximum(i * nb - 1, 0), cb))
    return pl.BlockSpec((8, XBC_W), lambda i: (jnp.minimum((i + 1) * nb, last), cb))


def _conv_fwd(proj, conv_w, conv_b, *, tm, nrows):
    tm = min(tm, nrows)

    def body(cur_ref, prev_ref, w_ref, b_ref, o_ref, pre_ref):
        prev = jnp.where(pl.program_id(0) == 0, 0.0, prev_ref[...].astype(F32))
        pre = _conv_pre(cur_ref[...].astype(F32), prev, w_ref[...], b_ref[...])
        o_ref[...] = jax.nn.silu(pre).astype(o_ref.dtype)
        pre_ref[...] = pre.astype(pre_ref.dtype)

    out = jax.ShapeDtypeStruct((nrows, XBC_W), ACT_DTYPE)
    return pl.pallas_call(
        body, name="conv_fwd", grid=(nrows // tm,),
        in_specs=[_row_spec(tm, XBC_W, XBC_CB), _halo_specs(tm, nrows, XBC_CB, True),
                  _whole_spec((CONV_K, XBC_W)), _whole_spec((1, XBC_W))],
        out_specs=[_row_spec(tm, XBC_W, 0)] * 2, out_shape=[out, out],
        compiler_params=pltpu.CompilerParams(dimension_semantics=("parallel",)),
    )(proj, proj, conv_w, conv_b)


def _conv_bwd_act(pre, dact, *, tm, nrows):
    tm = min(tm, nrows)
    nb = N_GROUPS * N_STATE

    def body(pre_ref, dxs_ref, dbm_ref, dcm_ref, dpre_ref, db_ref):
        pre = pre_ref[...].astype(F32)
        sg = jax.nn.sigmoid(pre)
        dy = jnp.concatenate([dxs_ref[...], dbm_ref[...], dcm_ref[...]], axis=1).astype(F32)
        dpre = dy * sg * (1.0 + pre * (1.0 - sg))
        dpre_ref[...] = dpre.astype(dpre_ref.dtype)

        @pl.when(pl.program_id(0) == 0)
        def _():
            db_ref[...] = jnp.zeros_like(db_ref)

        db_ref[...] += jnp.sum(dpre, axis=0, keepdims=True)

    return pl.pallas_call(
        body, name="conv_bwd_act", grid=(nrows // tm,),
        in_specs=[_row_spec(tm, XBC_W, 0), _row_spec(tm, D_INNER, 0), _row_spec(tm, nb, 0), _row_spec(tm, nb, 0)],
        out_specs=[_row_spec(tm, XBC_W, 0), _whole_spec((1, XBC_W))],
        out_shape=[jax.ShapeDtypeStruct((nrows, XBC_W), ACT_DTYPE), jax.ShapeDtypeStruct((1, XBC_W), F32)],
        compiler_params=pltpu.CompilerParams(dimension_semantics=("arbitrary",)),
    )(pre, *dact)


def _conv_bwd_x(dpre, proj, conv_w, dproj, *, tm, nrows):
    tm = min(tm, nrows)
    ntiles = nrows // tm

    def body(cur_ref, nxt_ref, x_ref, w_ref, _, o_ref, dw_ref):
        cur = cur_ref[...].astype(F32)
        nxt = jnp.where(pl.program_id(0) == ntiles - 1, 0.0, nxt_ref[...].astype(F32))
        x = x_ref[...].astype(F32)
        w = w_ref[...]

        @pl.when(pl.program_id(0) == 0)
        def _():
            dw_ref[...] = jnp.zeros_like(dw_ref)

        acc = cur * w[CONV_K - 1:CONV_K]
        dw_ref[CONV_K - 1:CONV_K, :] += jnp.sum(cur * x, axis=0, keepdims=True)
        for j in range(1, CONV_K):
            u = _shift_rows(cur, nxt, j, up=True)
            acc = acc + u * w[CONV_K - 1 - j:CONV_K - j]
            dw_ref[CONV_K - 1 - j:CONV_K - j, :] += jnp.sum(u * x, axis=0, keepdims=True)
        o_ref[...] = acc.astype(o_ref.dtype)

    return pl.pallas_call(
        body, name="conv_bwd_x", grid=(ntiles,),
        in_specs=[_row_spec(tm, XBC_W, 0), _halo_specs(tm, nrows, 0, False), _row_spec(tm, XBC_W, XBC_CB),
                  _whole_spec((CONV_K, XBC_W)), ANY],
        out_specs=[_row_spec(tm, XBC_W, XBC_CB), _whole_spec((CONV_K, XBC_W))],
        out_shape=[jax.ShapeDtypeStruct(dproj.shape, dproj.dtype), jax.ShapeDtypeStruct((CONV_K, XBC_W), F32)],
        input_output_aliases={4: 0},
        compiler_params=pltpu.CompilerParams(dimension_semantics=("arbitrary",)),
    )(dpre, dpre, proj, conv_w, dproj)


SSD_SPAN = 4
SSD_FWD_SPANS = 4
_XS_GW = D_INNER // N_GROUPS
_NT = (((1,), (1,)), ((), ()))
_TN = (((0,), (0,)), ((), ()))


def _bf16_terms(x, terms):
    parts, rest = [], x
    for _ in range(terms):
        part = rest.astype(jnp.bfloat16)
        parts.append(part)
        rest = rest - part.astype(F32)
    return parts


def _head_lane_matrix():
    return (lax.broadcasted_iota(jnp.int32, (128, _XS_GW), 0)
            == lax.broadcasted_iota(jnp.int32, (128, _XS_GW), 1) // HEAD_DIM).astype(jnp.bfloat16)


@functools.partial(jax.custom_vjp, nondiff_argnums=(1,))
def _head_lanes(cols, terms):
    e = _head_lane_matrix()
    return sum(jnp.dot(t, e, preferred_element_type=F32) for t in _bf16_terms(cols, terms))


def _head_lanes_fwd(cols, terms):
    return _head_lanes(cols, terms), None


def _head_lanes_bwd(terms, _, g):
    e = _head_lane_matrix()
    return (sum(lax.dot_general(t, e, _NT, preferred_element_type=F32) for t in _bf16_terms(g, 2)),)


_head_lanes.defvjp(_head_lanes_fwd, _head_lanes_bwd)


def _ssd_chunk(k, xs, bm, cm, dtr, hprev, dtb, alog, dsk):
    causal, tri, lo = k
    dt = jax.nn.softplus(dtr + dtb)
    da = dt * (-jnp.exp(alog))
    cs = jnp.dot(tri, da, precision=lax.Precision.HIGHEST, preferred_element_type=F32)
    cst = cs.T
    cs_l = _head_lanes(cs, 3)
    xdt = xs * _head_lanes(dt, 2)
    cb = _dot(cm, bm, _NT)
    yd = []
    for q in range(PAIRS_PER_GROUP):
        xq = xdt[:, 128 * q:128 * (q + 1)]
        y2 = [_dot(cb * jnp.exp(jnp.where(causal, cs[:, h:h + 1] - cst[h:h + 1, :], -jnp.inf)), xq)
              for h in (2 * q, 2 * q + 1)]
        yd.append(jnp.where(lo, y2[0], y2[1]))
    y_off = jnp.exp(cs_l) * _dot(cm, hprev, _NT)
    st = _dot(xdt * jnp.exp(cs_l[CHUNK - 1:CHUNK, :] - cs_l), bm, _TN)
    cdec = jnp.exp(cs[CHUNK - 1:CHUNK, :])
    cd_rows = jnp.concatenate(
        [jnp.broadcast_to(cdec[:, h:h + 1], (HEAD_DIM, N_STATE)) for h in range(HEADS_PER_GROUP)], axis=0)
    dsk_l = _head_lanes(jnp.broadcast_to(dsk, (8, 128)), 2)[:1]
    y = jnp.concatenate(yd, axis=1) + y_off + xs * dsk_l
    return y, cd_rows * hprev + st


def _ssd_span(xs, bm, cm, dtr, h0, dtb, alog, dsk):
    li = lax.broadcasted_iota(jnp.int32, (CHUNK, CHUNK), 0)
    si = lax.broadcasted_iota(jnp.int32, (CHUNK, CHUNK), 1)
    causal = li >= si
    k = (causal, causal.astype(F32), si < HEAD_DIM)
    h, ys = h0, []
    for t in range(xs.shape[0] // CHUNK):
        r = slice(t * CHUNK, (t + 1) * CHUNK)
        y, h = _ssd_chunk(k, xs[r], bm[r], cm[r], dtr[r], h, dtb, alog, dsk)
        ys.append(y)
    return (ys[0] if len(ys) == 1 else jnp.concatenate(ys, axis=0)), h


def _ssd_specs(rev, nsteps, rows):
    def s_of(s):
        return nsteps - 1 - s if rev else s

    xs = pl.BlockSpec((rows, _XS_GW), lambda g, s: (s_of(s), g))
    bm = pl.BlockSpec((rows, N_STATE), lambda g, s: (s_of(s), D_INNER // N_STATE + g))
    cm = pl.BlockSpec((rows, N_STATE), lambda g, s: (s_of(s), D_INNER // N_STATE + N_GROUPS + g))
    dt = pl.BlockSpec((rows, 128), lambda g, s: (s_of(s), g))
    par = pl.BlockSpec((1, 128), lambda g, s: (0, g))
    st = pl.BlockSpec((None, None, _XS_GW, N_STATE), lambda g, s: (g, s_of(s), 0, 0))
    return xs, bm, cm, dt, par, st


def _ssd_fwd(act, dtr, dtb, alog, dsk, *, nrows):
    span = CHUNK * min(SSD_SPAN, nrows // CHUNK)
    per_step = SSD_FWD_SPANS if nrows % (SSD_FWD_SPANS * span) == 0 else 1
    rows = per_step * span
    nsteps = nrows // rows
    xs, bm, cm, dt, par, _ = _ssd_specs(False, nsteps, rows)
    st = pl.BlockSpec((None, per_step, _XS_GW, N_STATE), lambda g, s: (g, s, 0, 0))

    def body(xs_ref, b_ref, c_ref, dt_ref, dtb_ref, al_ref, dk_ref, y_ref, st_ref, h_ref):
        @pl.when(pl.program_id(1) == 0)
        def _():
            h_ref[...] = jnp.zeros_like(h_ref)

        h = h_ref[...]
        for i in range(per_step):
            r = slice(i * span, (i + 1) * span)
            st_ref[i] = h
            y, h = _ssd_span(xs_ref[r, :].astype(F32), b_ref[r, :].astype(F32), c_ref[r, :].astype(F32), dt_ref[r, :],
                             h, dtb_ref[...], al_ref[...], dk_ref[...])
            y_ref[r, :] = y.astype(y_ref.dtype)
        h_ref[...] = h

    return pl.pallas_call(
        body, name="ssd_fwd", grid=(N_GROUPS, nsteps),
        in_specs=[xs, bm, cm, dt, par, par, par], out_specs=[xs, st],
        out_shape=[jax.ShapeDtypeStruct((nrows, D_INNER), ACT_DTYPE),
                   jax.ShapeDtypeStruct((N_GROUPS, nrows // span, _XS_GW, N_STATE), F32)],
        scratch_shapes=[pltpu.VMEM((_XS_GW, N_STATE), F32)],
        compiler_params=pltpu.CompilerParams(dimension_semantics=("arbitrary", "arbitrary")),
    )(act, act, act, dtr, dtb, alog, dsk)


def _ssd_bwd(act, dtr, dtb, alog, dsk, states, dy, *, nrows):
    rows = CHUNK * min(SSD_SPAN, nrows // CHUNK)
    nsteps = nrows // rows
    xs, bm, cm, dt, par, st = _ssd_specs(True, nsteps, rows)

    def body(xs_ref, b_ref, c_ref, dt_ref, dtb_ref, al_ref, dk_ref, st_ref, dy_ref,
             dxs_ref, db_ref, dc_ref, ddt_ref, ddtb_ref, dal_ref, ddk_ref, dh_ref):
        @pl.when(pl.program_id(1) == 0)
        def _():
            dh_ref[...] = jnp.zeros_like(dh_ref)
            ddtb_ref[...] = jnp.zeros_like(ddtb_ref)
            dal_ref[...] = jnp.zeros_like(dal_ref)
            ddk_ref[...] = jnp.zeros_like(ddk_ref)

        _, vjp = jax.vjp(_ssd_span, xs_ref[...].astype(F32), b_ref[...].astype(F32), c_ref[...].astype(F32),
                         dt_ref[...], st_ref[...], dtb_ref[...], al_ref[...], dk_ref[...])
        dxs, db, dc, ddt, dh, ddtb, dal, ddk = vjp((dy_ref[...].astype(F32), dh_ref[...]))
        dxs_ref[...] = dxs.astype(dxs_ref.dtype)
        db_ref[...] = db.astype(db_ref.dtype)
        dc_ref[...] = dc.astype(dc_ref.dtype)
        ddt_ref[...] = ddt.astype(ddt_ref.dtype)
        dh_ref[...] = dh
        ddtb_ref[...] += ddtb
        dal_ref[...] += dal
        ddk_ref[...] += ddk

    nb = N_GROUPS * N_STATE
    bspec = pl.BlockSpec((rows, N_STATE), lambda g, s: (nsteps - 1 - s, g))
    return pl.pallas_call(
        body, name="ssd_bwd", grid=(N_GROUPS, nsteps),
        in_specs=[xs, bm, cm, dt, par, par, par, st, xs],
        out_specs=[xs, bspec, bspec, dt, par, par, par],
        out_shape=[jax.ShapeDtypeStruct((nrows, D_INNER), ACT_DTYPE), jax.ShapeDtypeStruct((nrows, nb), ACT_DTYPE),
                   jax.ShapeDtypeStruct((nrows, nb), ACT_DTYPE), jax.ShapeDtypeStruct((nrows, DT_W), ACT_DTYPE),
                   jax.ShapeDtypeStruct((1, DT_W), F32), jax.ShapeDtypeStruct((1, DT_W), F32),
                   jax.ShapeDtypeStruct((1, DT_W), F32)],
        scratch_shapes=[pltpu.VMEM((_XS_GW, N_STATE), F32)],
        compiler_params=pltpu.CompilerParams(dimension_semantics=("arbitrary", "arbitrary")),
    )(act, act, act, dtr, dtb, alog, dsk, states, dy)


def _add_epilogue(acc, r):
    return r + acc


def _rms_and_skip(x, g):
    return _rms(x, g), x


def _forward_backward(x, p, tgt, w, late_weights=None, after=()):
    s = x.shape[0]
    act_t, f32 = ACT_DTYPE, F32
    mm = functools.partial(_matmul)
    h, proj, dtr = _rows_matmul("proj", _f_rms, [(x, D_MODEL, 0)], [w["norm_g"]], w["w_main"], out_dtype=act_t,
                                n=MAIN_W, k=D_MODEL, tm=1024, tn=2 * MM_TILE, nrows=s, after=after, side=(w["w_dt"], f32))
    act, conv_pre = _conv_fwd(proj, w["conv_w"], w["conv_b"], tm=512, nrows=s)
    y, states = _ssd_fwd(act, dtr, w["dt_bias"], w["a_log"], w["d_skip"], nrows=s)
    if late_weights is not None:
        w = {**w, **late_weights(states)}
    a_pars = [w["ln_a_g"], w["ln_a_b"], w["w_s"], w["b_s"]]
    y_a, o_a = _rows_matmul("out_a", _f_branch_a, [(proj, UVZ_W, UVZ_CB)], a_pars, w["w_oa"], out_dtype=act_t,
                            n=D_MODEL, k=E_A, tm=512, nrows=s)
    gn_rows = [(y, D_INNER, 0), (proj, ZB_W, ZB_CB)]
    y_b, o_b = _rows_matmul("out_b", _f_gnorm, gn_rows, [w["ssm_norm_g"]], w["w_ob"], out_dtype=act_t,
                            n=D_MODEL, k=D_INNER, tm=1024, nrows=s)
    mg_rows = [(proj, G_W, G_CB), (o_a, D_MODEL, 0), (o_b, D_MODEL, 0)]
    merged, x1 = _rows_matmul("out_proj", _f_merge, mg_rows, [], w["w_out"], out_dtype=f32, n=D_MODEL, k=D_MODEL,
                              tm=1024, nrows=s, extras=(x,), epilogue=_add_epilogue)
    g = {}
    hp, dx2, dgp, dpe, g["final_g"], loss = _head(x1, p, tgt, w["ple_norm_g"], w["w_pg"], w["w_ple"], w["final_g"],
                                                   tm=512, nrows=s)
    g["w_pg"] = mm(hp, dgp, mode="tn", name="d_w_pg", out_dtype=f32, m=D_MODEL, n=D_MODEL, k=s, tn=MM_TILE // 2)
    g["w_ple"] = mm(p, dpe, mode="tn", name="d_w_ple", out_dtype=f32, m=PLE_DIM, n=D_MODEL, k=s)
    dx1, g["ple_norm_g"] = _rows_vjp_call(
        "ple_norm_bwd", _rms_and_skip, [(x1, D_MODEL, 0)], [w["ple_norm_g"]], [(dx2, D_MODEL, 0)],
        [(f32, None)], tm=1024, nrows=s, cot_mm=(dgp, w["w_pg"], None))
    g["w_out"] = mm(merged, dx1, mode="tn", name="d_w_out", out_dtype=f32, m=D_MODEL, n=D_MODEL, k=s)
    dproj, do_a, do_b = _rows_vjp_call(
        "merge_bwd", _f_merge, mg_rows, [], [],
        [(act_t, (None, MAIN_W, G_CB)), (act_t, None), (act_t, None)], tm=1024, nrows=s, cot_mm=(dx1, w["w_out"], None))
    g["w_oa"] = mm(y_a, do_a, mode="tn", name="d_w_oa", out_dtype=f32, m=E_A, n=D_MODEL, k=s, tn=MM_TILE // 2)
    g["w_ob"] = mm(y_b, do_b, mode="tn", name="d_w_ob", out_dtype=f32, m=D_INNER, n=D_MODEL, k=s)
    dy, dproj, g["ssm_norm_g"] = _rows_vjp_call(
        "gnorm_bwd", _f_gnorm, gn_rows, [w["ssm_norm_g"]], [],
        [(act_t, None), (act_t, (dproj, MAIN_W, ZB_CB))], tm=512, nrows=s, cot_mm=(do_b, w["w_ob"], None))
    dxs, dbm, dcm, ddtr, g["dt_bias"], g["a_log"], g["d_skip"] = _ssd_bwd(
        act, dtr, w["dt_bias"], w["a_log"], w["d_skip"], states, dy, nrows=s)
    dpre, g["conv_b"] = _conv_bwd_act(conv_pre, (dxs, dbm, dcm), tm=1024, nrows=s)
    dproj, g["conv_w"] = _conv_bwd_x(dpre, proj, w["conv_w"], dproj, tm=512, nrows=s)
    dproj, g["ln_a_g"], g["ln_a_b"], g["w_s"], g["b_s"] = _rows_vjp_call(
        "branch_a_bwd", _f_branch_a, [(proj, UVZ_W, UVZ_CB)], a_pars, [],
        [(act_t, (dproj, MAIN_W, UVZ_CB))], tm=512, nrows=s, cot_mm=(do_a, w["w_oa"], None))
    g["w_main"] = mm(h, dproj, mode="tn", name="d_w_main", out_dtype=f32, m=D_MODEL, n=MAIN_W, k=s, tn=MM_TILE // 2)
    g["w_dt"] = mm(h, ddtr, mode="tn", name="d_w_dt", out_dtype=f32, m=D_MODEL, n=DT_W, k=s)
    return loss, g, (dproj, ddtr, dx1)


def _input_grad(x, w, ctx, after=()):
    dproj, ddtr, dx1 = ctx
    s = x.shape[0]
    dh = _matmul(dproj, w["w_main"], mode="nt", name="d_h_main", out_dtype=F32, m=s, n=D_MODEL, k=MAIN_W,
                 tm=MM_TILE // 2, after=after)
    return _rows_vjp_call(
        "pre_norm_bwd", _rms_and_skip, [(x, D_MODEL, 0)], [w["norm_g"]], [(dx1, D_MODEL, 0)],
        [(F32, None)], tm=1024, nrows=s, cot_mm=(ddtr, w["w_dt"], dh))


def _local_step(x, p, tgt, w):
    loss, g, ctx = _forward_backward(x, p, tgt, w)
    grad_x, g["norm_g"] = _input_grad(x, w, ctx)
    return loss, grad_x, g


_O_ZB = 3 * E_A
_O_XBC = _O_ZB + D_INNER
_O_DT = _O_XBC + CONV_DIM
_O_G = _O_DT + N_HEADS


def _heads_to_lanes(v):
    r = v.shape[0]
    v = v.reshape(r, N_GROUPS, HEADS_PER_GROUP)
    return jnp.pad(v, ((0, 0), (0, 0), (0, 128 - HEADS_PER_GROUP))).reshape(r, DT_W)


def _lanes_to_heads(v):
    r = v.shape[0]
    return v.reshape(r, N_GROUPS, 128)[:, :, :HEADS_PER_GROUP].reshape(r, N_HEADS)


def _block_cols(blocks, a, b):
    parts = []
    for k in range(N_CHIPS):
        lo, hi = max(a, k * W_IN_BLOCK), min(b, (k + 1) * W_IN_BLOCK)
        if lo < hi:
            parts.append(blocks[k][:, lo - k * W_IN_BLOCK:hi - k * W_IN_BLOCK])
    return parts


_W_IN_SEGMENTS = ((0, _O_ZB, "m", 0), (_O_ZB, _O_XBC, "m", UVZ_W + XBC_W), (_O_XBC, _O_DT, "m", UVZ_W),
                  (_O_DT, _O_G, "d", 0), (_O_G, N_IN, "m", MAIN_W - G_W))


def _w_in_grad_blocks(gm, gdt):
    blocks = []
    for k in range(N_CHIPS):
        a, b = k * W_IN_BLOCK, (k + 1) * W_IN_BLOCK
        parts = []
        for s, e, src, off in _W_IN_SEGMENTS:
            lo, hi = max(a, s), min(b, e)
            if lo < hi:
                parts.append((gm if src == "m" else gdt)[:, off + lo - s:off + hi - s])
        blocks.append(jnp.concatenate(parts, axis=1))
    return jnp.stack(blocks)


def _layout_weights(f, w_in_blocks=None):
    w = dict(f)
    if w_in_blocks is None:
        w_in = w.pop("w_in")
        w_in_blocks = jnp.stack([w_in[:, k * W_IN_BLOCK:(k + 1) * W_IN_BLOCK] for k in range(N_CHIPS)])
    cols = functools.partial(_block_cols, w_in_blocks)
    w["w_main"] = jnp.concatenate(cols(0, _O_ZB) + cols(_O_XBC, _O_DT) + cols(_O_ZB, _O_XBC) + cols(_O_G, N_IN), axis=1)
    w["w_dt"] = _heads_to_lanes(jnp.concatenate(cols(_O_DT, _O_G), axis=1))
    w["b_s"] = f["b_s"].reshape(G_A, CHUNK, 1)
    for n in ("dt_bias", "a_log", "d_skip"):
        w[n] = _heads_to_lanes(f[n])
    return w


def _natural_grads(g):
    out = dict(g)
    gm = out.pop("w_main")
    gdt = _lanes_to_heads(out.pop("w_dt"))
    out["w_in"] = jnp.concatenate(
        [gm[:, :UVZ_W], gm[:, UVZ_W + XBC_W:UVZ_W + XBC_W + ZB_W], gm[:, UVZ_W:UVZ_W + XBC_W], gdt, gm[:, MAIN_W - G_W:]],
        axis=1)
    out["b_s"] = g["b_s"].reshape(G_A, CHUNK)
    for n in ("dt_bias", "a_log", "d_skip"):
        out[n] = _lanes_to_heads(g[n])
    return out


def _place():
    return lax.axis_index("x"), lax.axis_index("y"), lax.axis_index("c")


def _other_chips(x, y):
    return [(1 - x, y), (x, 1 - y), (1 - x, 1 - y)]


def _rcopy(src, dst, ssem, rsem, dev):
    return pltpu.make_async_remote_copy(src_ref=src, dst_ref=dst, send_sem=ssem, recv_sem=rsem,
                                        device_id=dev, device_id_type=MESH)


def _half(ref_rows, half):
    hs = ref_rows // 2
    return pl.ds(pl.multiple_of(half * hs, 16), hs)


GATHER_CHUNKS = 8


def _gather_weights(shards, conv_shard):
    nw, nq = len(shards), GATHER_CHUNKS

    def body(*refs):
        sh, cv = refs[:nw], refs[nw]
        out, cvo = refs[nw + 1:2 * nw + 1], refs[2 * nw + 1]
        ici_s, ici_r, fw_s, fw_r, own_s, own_r, cv_s, cv_r = refs[2 * nw + 2:]
        x, y, c = _place()
        me, sib, chips = 2 * x + y, (x, y, 1 - c), _other_chips(x, y)
        own = [_rcopy(sh[w], out[w].at[me], own_s.at[w], own_r.at[w], sib) for w in range(nw)]
        own.append(_rcopy(cv, cvo.at[me], own_s.at[nw], own_r.at[nw], sib))
        for cp in own:
            cp.start()
        pieces = [(w, q, j) for w in range(nw) for q in range(nq) for j in range(3)]

        def rows(w, half, q):
            hs = sh[w].shape[0] // 2
            return pl.ds(pl.multiple_of(half * hs + q * (hs // nq), 16), hs // nq)

        def sem(w, q, j):
            return (3 * w + j) * nq + q

        def landed(w, q, j, half):
            return out[w].at[2 * chips[j][0] + chips[j][1], rows(w, half, q)]

        sends = [_rcopy(sh[w].at[rows(w, c, q)], out[w].at[me, rows(w, c, q)], ici_s.at[sem(w, q, j)],
                        ici_r.at[sem(w, q, j)], (*chips[j], c)) for w, q, j in pieces if j < 2]
        for j, chip in enumerate(chips):
            sends.append(_rcopy(cv, cvo.at[me], cv_s.at[j], cv_r.at[j], (*chip, c)))
        for cp in sends:
            cp.start()

        def arrived(w, q, j):
            slab = landed(w, q, j, c)
            _rcopy(slab, slab, ici_s.at[sem(w, q, j)], ici_r.at[sem(w, q, j)], (*chips[j], c)).wait_recv()
            fwd = _rcopy(slab, slab, fw_s.at[sem(w, q, j)], fw_r.at[sem(w, q, j)], sib)
            fwd.start()
            sends.append(fwd)
            if j < 2 and (q % 2 == 0) == (j == 0):
                hop = _rcopy(slab, slab, ici_s.at[sem(w, q, 2)], ici_r.at[sem(w, q, 2)], (*chips[1 - j], c))
                hop.start()
                sends.append(hop)

        for w, q, j in pieces:
            if j < 2:
                arrived(w, q, j)
        for w, q, j in pieces:
            if j == 2:
                arrived(w, q, j)
        for j, chip in enumerate(chips):
            blk = cvo.at[2 * chip[0] + chip[1]]
            _rcopy(blk, blk, cv_s.at[j], cv_r.at[j], (*chip, c)).wait_recv()
        for w, q, j in pieces:
            slab = landed(w, q, j, 1 - c)
            _rcopy(slab, slab, fw_s.at[sem(w, q, j)], fw_r.at[sem(w, q, j)], sib).wait_recv()
        for cp in sends:
            cp.wait_send()
        for cp in own:
            cp.wait()

    dma = pltpu.SemaphoreType.DMA
    n_ici = 3 * nw * nq
    return pl.pallas_call(
        body, name="gather_weights",
        in_specs=[ANY] * (nw + 1), out_specs=[ANY] * (nw + 1),
        out_shape=[jax.ShapeDtypeStruct((N_CHIPS,) + s.shape, s.dtype) for s in shards]
        + [jax.ShapeDtypeStruct((N_CHIPS,) + conv_shard.shape, conv_shard.dtype)],
        scratch_shapes=[dma((n_ici,)), dma((n_ici,)), dma((n_ici,)), dma((n_ici,)), dma((nw + 1,)), dma((nw + 1,)),
                        dma((3,)), dma((3,))],
    )(*shards, conv_shard)


_HBM = pl.BlockSpec(memory_space=pltpu.HBM)
_SEM = pl.BlockSpec(memory_space=pltpu.SEMAPHORE)
_EFFECT = pltpu.SideEffectType.DATAFLOW_SIDE_EFFECTING


def _late_gather_copies(sh, out, s_sem, r_sem):
    x, y, c = _place()
    to = [(*chip, c) for chip in _other_chips(x, y)] + [(x, y, 1 - c)]
    return [_rcopy(sh[w], out[w].at[2 * x + y], s_sem.at[4 * w + j], r_sem.at[4 * w + j], dev)
            for w in range(len(sh)) for j, dev in enumerate(to)]


def _late_gather_start(shards):
    n = len(shards)
    lands = [lax.empty((N_CHIPS,) + a.shape, a.dtype) for a in shards]

    def body(*refs):
        for cp in _late_gather_copies(refs[:n], refs[n:2 * n], refs[2 * n], refs[2 * n + 1]):
            cp.start()
        refs[-1][...] = jnp.zeros_like(refs[-1])

    dma = pltpu.SemaphoreType.DMA
    hbm = [pltpu.with_memory_space_constraint(a, pltpu.HBM) for a in list(shards) + lands]
    out = pl.pallas_call(
        body, name="late_gather_start",
        out_shape=[dma((4 * n,)), dma((4 * n,))] + [pltpu.HBM(a.shape, a.dtype) for a in hbm]
        + [jax.ShapeDtypeStruct((8, 128), F32)],
        in_specs=[_HBM] * (2 * n), out_specs=[_SEM, _SEM] + [_HBM] * (2 * n) + [pl.BlockSpec(memory_space=pltpu.VMEM)],
        input_output_aliases={i: 2 + i for i in range(2 * n)},
        compiler_params=pltpu.CompilerParams(has_side_effects=_EFFECT),
    )(*hbm)
    return out[0], out[1], out[2:2 + n], out[2 + n:2 + 2 * n], out[-1]


def _late_gather_wait(s_sem, r_sem, srcs, lands, after):
    n = len(srcs)

    def body(*refs):
        for cp in _late_gather_copies(refs[:n], refs[n:2 * n], refs[2 * n], refs[2 * n + 1]):
            cp.wait_send()
            cp.wait_recv()

    out = pl.pallas_call(
        body, name="late_gather_wait",
        out_shape=[pltpu.HBM(a.shape, a.dtype) for a in list(srcs) + list(lands)],
        in_specs=[_HBM] * (2 * n) + [_SEM, _SEM, ANY], out_specs=[_HBM] * (2 * n),
        input_output_aliases={i: i for i in range(2 * n)},
        compiler_params=pltpu.CompilerParams(has_side_effects=_EFFECT),
    )(*srcs, *lands, s_sem, r_sem, after)
    return out[n:]


def _swap_with_sibling(arrs):
    n = len(arrs)

    def body(*refs):
        src, dst, s_sem, r_sem = refs[:n], refs[n:2 * n], refs[2 * n], refs[2 * n + 1]
        x, y, c = _place()
        cps = [_rcopy(src[i], dst[i], s_sem.at[i], r_sem.at[i], (x, y, 1 - c)) for i in range(n)]
        for cp in cps:
            cp.start()
        for cp in cps:
            cp.wait()

    dma = pltpu.SemaphoreType.DMA
    return pl.pallas_call(
        body, name="swap_with_sibling", in_specs=[ANY] * n, out_specs=[ANY] * n,
        out_shape=[jax.ShapeDtypeStruct(a.shape, a.dtype) for a in arrs], scratch_shapes=[dma((n,)), dma((n,))],
    )(*arrs)


def _scatter_copies(src, land, s_sem, r_sem):
    x, y, c = _place()
    return [_rcopy(src[i].at[2 * chip[0] + chip[1]], land[i].at[j], s_sem.at[3 * i + j], r_sem.at[3 * i + j], (*chip, c))
            for i in range(len(src)) for j, chip in enumerate(_other_chips(x, y))]


def _scatter_blocks_start(arrs):
    n = len(arrs)
    lands = [lax.empty((3,) + a.shape[1:], a.dtype) for a in arrs]

    def body(*refs):
        src, land, s_sem, r_sem, token = refs[:n], refs[n:2 * n], refs[2 * n], refs[2 * n + 1], refs[-1]
        for cp in _scatter_copies(src, land, s_sem, r_sem):
            cp.start()
        token[...] = jnp.zeros_like(token)

    dma = pltpu.SemaphoreType.DMA
    hbm = [pltpu.with_memory_space_constraint(a, pltpu.HBM) for a in list(arrs) + lands]
    out = pl.pallas_call(
        body, name="scatter_blocks_start",
        out_shape=[dma((3 * n,)), dma((3 * n,))] + [pltpu.HBM(a.shape, a.dtype) for a in hbm]
        + [jax.ShapeDtypeStruct((8, 128), F32)],
        in_specs=[_HBM] * (2 * n), out_specs=[_SEM, _SEM] + [_HBM] * (2 * n) + [pl.BlockSpec(memory_space=pltpu.VMEM)],
        input_output_aliases={i: 2 + i for i in range(2 * n)},
        compiler_params=pltpu.CompilerParams(has_side_effects=_EFFECT),
    )(*hbm)
    return out[0], out[1], out[2:2 + n], out[2 + n:2 + 2 * n], out[-1]


def _scatter_blocks_wait(s_sem, r_sem, srcs, lands, after):
    n = len(srcs)

    def body(*refs):
        src, land, s_sem, r_sem = refs[:n], refs[n:2 * n], refs[2 * n], refs[2 * n + 1]
        for cp in _scatter_copies(src, land, s_sem, r_sem):
            cp.wait_send()
            cp.wait_recv()

    out = pl.pallas_call(
        body, name="scatter_blocks_wait",
        out_shape=[pltpu.HBM(a.shape, a.dtype) for a in list(srcs) + list(lands)],
        in_specs=[_HBM] * (2 * n) + [_SEM, _SEM] + [ANY] * len(after), out_specs=[_HBM] * (2 * n),
        input_output_aliases={i: i for i in range(2 * n)},
        compiler_params=pltpu.CompilerParams(has_side_effects=_EFFECT),
    )(*srcs, *lands, s_sem, r_sem, *after)
    return out[:n], out[n:]


def _share_halves(arrs):
    n = len(arrs)

    def body(*refs):
        buf, s_sem, r_sem = refs[n:2 * n], refs[2 * n], refs[2 * n + 1]
        x, y, c = _place()
        cps = []
        for i in range(n):
            mine = buf[i].at[_half(buf[i].shape[0], c)]
            cps.append(_rcopy(mine, mine, s_sem.at[i], r_sem.at[i], (x, y, 1 - c)))
        for cp in cps:
            cp.start()
        for i in range(n):
            theirs = buf[i].at[_half(buf[i].shape[0], 1 - c)]
            _rcopy(theirs, theirs, s_sem.at[i], r_sem.at[i], (x, y, 1 - c)).wait_recv()
        for cp in cps:
            cp.wait_send()

    dma = pltpu.SemaphoreType.DMA
    return pl.pallas_call(
        body, name="share_halves", in_specs=[ANY] * n, out_specs=[ANY] * n,
        out_shape=[jax.ShapeDtypeStruct(a.shape, a.dtype) for a in arrs],
        input_output_aliases={i: i for i in range(n)}, scratch_shapes=[dma((n,)), dma((n,))],
    )(*arrs)


def _small_gather_copies(src, land, s_sem, r_sem):
    x, y, c = _place()
    cps = []
    for d in range(1, N_DEV):
        peer = ((1 - x) if d & 4 else x), ((1 - y) if d & 2 else y), ((1 - c) if d & 1 else c)
        cps.append(_rcopy(src, land.at[4 * x + 2 * y + c], s_sem.at[d - 1], r_sem.at[d - 1], peer))
    return cps


def _small_gather_start(packed):
    def body(src, land, s_sem, r_sem, _, __, token):
        for cp in _small_gather_copies(src, land, s_sem, r_sem):
            cp.start()
        token[...] = jnp.zeros_like(token)

    dma = pltpu.SemaphoreType.DMA
    hbm = [pltpu.with_memory_space_constraint(a, pltpu.HBM) for a in (packed, lax.empty((N_DEV,) + packed.shape, F32))]
    return pl.pallas_call(
        body, name="small_gather_start",
        out_shape=[dma((N_DEV - 1,)), dma((N_DEV - 1,))] + [pltpu.HBM(a.shape, a.dtype) for a in hbm]
        + [jax.ShapeDtypeStruct((8, 128), F32)],
        in_specs=[_HBM] * 2, out_specs=[_SEM, _SEM, _HBM, _HBM, pl.BlockSpec(memory_space=pltpu.VMEM)],
        input_output_aliases={0: 2, 1: 3}, compiler_params=pltpu.CompilerParams(has_side_effects=_EFFECT),
    )(*hbm)


def _small_gather_wait(s_sem, r_sem, src, land, after):
    def body(src, land, s_sem, r_sem, *_):
        for cp in _small_gather_copies(src, land, s_sem, r_sem):
            cp.wait_send()
            cp.wait_recv()

    return pl.pallas_call(
        body, name="small_gather_wait", out_shape=[pltpu.HBM(src.shape, src.dtype), pltpu.HBM(land.shape, land.dtype)],
        in_specs=[_HBM, _HBM, _SEM, _SEM, ANY], out_specs=[_HBM, _HBM], input_output_aliases={0: 0, 1: 1},
        compiler_params=pltpu.CompilerParams(has_side_effects=_EFFECT),
    )(src, land, s_sem, r_sem, after)


def _small_sum(own, land, dev_arr):
    def body(me_ref, own_ref, land_ref, o_ref):
        acc = jnp.zeros(o_ref.shape, F32)
        for d in range(N_DEV):
            acc = acc + jnp.where(me_ref[0] == d, own_ref[...], land_ref[d])
        o_ref[...] = acc

    return pl.pallas_call(
        body, name="small_sum", out_shape=jax.ShapeDtypeStruct(own.shape, F32),
        grid_spec=pltpu.PrefetchScalarGridSpec(
            num_scalar_prefetch=1, grid=(1,),
            in_specs=[pl.BlockSpec(own.shape, lambda i, m: (0, 0)), pl.BlockSpec(land.shape, lambda i, m: (0, 0, 0))],
            out_specs=pl.BlockSpec(own.shape, lambda i, m: (0, 0))),
    )(dev_arr, own, land)


def _row_tile(rows, cols):
    tr = max(8, min(rows, (1 << 20) // (4 * cols) // 8 * 8))
    while rows % tr:
        tr -= 8
    return tr


def _chip_sum(name, g5, recv, c_arr):
    nb, _, hs, cols = g5.shape
    tr = _row_tile(hs, cols)

    def body(_, a_ref, b_ref, o_ref):
        o_ref[...] = (a_ref[...] + b_ref[...].astype(F32)).astype(o_ref.dtype)

    blk = pl.BlockSpec((None, tr, cols), lambda b, i, c: (b, i, 0))
    return pl.pallas_call(
        body, name=name,
        grid_spec=pltpu.PrefetchScalarGridSpec(
            num_scalar_prefetch=1, grid=(nb, hs // tr),
            in_specs=[pl.BlockSpec((None, None, tr, cols), lambda b, i, c: (b, c[0], i, 0)), blk], out_specs=blk),
        out_shape=jax.ShapeDtypeStruct((nb, hs, cols), WIRE_DTYPE),
    )(c_arr, g5, recv)


def _final_sum(name, own, recv, place_arr):
    _, hs, cols = own.shape
    tr = _row_tile(hs, cols)
    nt = hs // tr

    def body(_, a_ref, r_ref, o_ref):
        o_ref[...] = ((a_ref[...].astype(F32) + r_ref[0].astype(F32)) + r_ref[1].astype(F32)) + r_ref[2].astype(F32)

    return pl.pallas_call(
        body, name=name,
        grid_spec=pltpu.PrefetchScalarGridSpec(
            num_scalar_prefetch=1, grid=(nt,),
            in_specs=[pl.BlockSpec((None, tr, cols), lambda i, m: (m[0], i, 0)),
                      pl.BlockSpec((3, tr, cols), lambda i, m: (0, i, 0))],
            out_specs=pl.BlockSpec((tr, cols), lambda i, m: (m[1] * nt + i, 0))),
        out_shape=jax.ShapeDtypeStruct((2 * hs, cols), F32),
    )(place_arr, own, recv)


def _adamw(w, g, m, v):
    m = ADAM_B1 * m + (1.0 - ADAM_B1) * g
    v = ADAM_B2 * v + (1.0 - ADAM_B2) * (g * g)
    m_hat = m / (1.0 - ADAM_B1 ** ADAM_STEP)
    v_hat = v / (1.0 - ADAM_B2 ** ADAM_STEP)
    return -ADAM_LR * (m_hat / (jnp.sqrt(v_hat) + ADAM_EPS) + ADAM_WD * w), m, v


def _adamw_call(name, w, g, m, v):
    rows, cols = w.shape
    tr = _row_tile(rows, cols)
    if 4 * tr * cols >= (1 << 18):
        blk, steps = pl.BlockSpec((tr, cols), lambda i: (i, 0)), rows // tr
    else:
        blk, steps = pl.BlockSpec((rows, 128), lambda i: (0, i)), cols // 128

    def body(w_ref, g_ref, m_ref, v_ref, d_ref, nm_ref, nv_ref, go_ref):
        g = g_ref[...]
        d_ref[...], nm_ref[...], nv_ref[...] = _adamw(w_ref[...], g, m_ref[...], v_ref[...])
        go_ref[...] = g

    return pl.pallas_call(
        body, name=name, grid=(steps,), in_specs=[blk] * 4, out_specs=[blk] * 4,
        out_shape=[jax.ShapeDtypeStruct(w.shape, F32)] * 4,
        compiler_params=pltpu.CompilerParams(dimension_semantics=("parallel",)),
    )(w, g, m, v)


def _adamw_small(ws, gs, ms, vs):
    n = len(ws)

    def body(*refs):
        for i in range(n):
            w_ref, g_ref, m_ref, v_ref = (refs[k * n + i] for k in range(4))
            d, nm, nv = _adamw(w_ref[...], g_ref[...], m_ref[...], v_ref[...])
            refs[4 * n + i][...] = d
            refs[5 * n + i][...] = nm
            refs[6 * n + i][...] = nv

    out = pl.pallas_call(
        body, name="adamw_small", out_shape=[jax.ShapeDtypeStruct(a.shape, F32) for a in ws] * 3,
    )(*ws, *gs, *ms, *vs)
    return out[:n], out[n:2 * n], out[2 * n:]


_BIG = ("w_in", "w_oa", "w_ob", "w_out", "w_pg", "w_ple")
_SMALL = ("norm_g", "ln_a_g", "ln_a_b", "w_s", "b_s", "conv_w", "conv_b", "dt_bias", "a_log", "d_skip", "ssm_norm_g",
          "ple_norm_g", "final_g")
_WEIGHTS = ("norm_g", "w_in", "ln_a_g", "ln_a_b", "w_s", "b_s", "conv_w", "conv_b", "dt_bias", "a_log", "d_skip",
            "ssm_norm_g", "w_oa", "w_ob", "w_out", "ple_norm_g", "w_pg", "w_ple", "final_g")
_COL_SHARDED = ("w_in", "w_ple")
_PACK = 1024


def _blocks_to_full(col_sharded, blocks):
    if col_sharded:
        return jnp.concatenate([blocks[k] for k in range(N_CHIPS)], axis=1)
    return blocks.reshape(N_CHIPS * blocks.shape[1], blocks.shape[2])


def _full_to_blocks(col_sharded, full):
    if col_sharded:
        w = full.shape[1] // N_CHIPS
        return jnp.stack([full[:, k * w:(k + 1) * w] for k in range(N_CHIPS)])
    return full.reshape(N_CHIPS, full.shape[0] // N_CHIPS, full.shape[1])


def _two_d(n, a):
    if n == "w_s":
        return a.reshape(G_A * CHUNK, CHUNK)
    if n in ("b_s", "conv_w"):
        return a.reshape(a.shape[-2], a.shape[-1])
    return a.reshape(1, a.shape[-1])


def kernel(x, p, norm_g, w_in, ln_a_g, ln_a_b, w_s, b_s, conv_w, conv_b, dt_bias, a_log, d_skip, ssm_norm_g, w_oa, w_ob, w_out, ple_norm_g, w_pg, w_ple, final_g, loss_target, m_norm_g, m_w_in, m_ln_a_g, m_ln_a_b, m_w_s, m_b_s, m_conv_w, m_conv_b, m_dt_bias, m_a_log, m_d_skip, m_ssm_norm_g, m_w_oa, m_w_ob, m_w_out, m_ple_norm_g, m_w_pg, m_w_ple, m_final_g, v_norm_g, v_w_in, v_ln_a_g, v_ln_a_b, v_w_s, v_b_s, v_conv_w, v_conv_b, v_dt_bias, v_a_log, v_d_skip, v_ssm_norm_g, v_w_oa, v_w_ob, v_w_out, v_ple_norm_g, v_w_pg, v_w_ple, v_final_g):
    wt = dict(norm_g=norm_g, w_in=w_in, ln_a_g=ln_a_g, ln_a_b=ln_a_b, w_s=w_s, b_s=b_s, conv_w=conv_w, conv_b=conv_b,
              dt_bias=dt_bias, a_log=a_log, d_skip=d_skip, ssm_norm_g=ssm_norm_g, w_oa=w_oa, w_ob=w_ob, w_out=w_out,
              ple_norm_g=ple_norm_g, w_pg=w_pg, w_ple=w_ple, final_g=final_g)
    mom = dict(norm_g=m_norm_g, w_in=m_w_in, ln_a_g=m_ln_a_g, ln_a_b=m_ln_a_b, w_s=m_w_s, b_s=m_b_s, conv_w=m_conv_w,
               conv_b=m_conv_b, dt_bias=m_dt_bias, a_log=m_a_log, d_skip=m_d_skip, ssm_norm_g=m_ssm_norm_g, w_oa=m_w_oa,
               w_ob=m_w_ob, w_out=m_w_out, ple_norm_g=m_ple_norm_g, w_pg=m_w_pg, w_ple=m_w_ple, final_g=m_final_g)
    vel = dict(norm_g=v_norm_g, w_in=v_w_in, ln_a_g=v_ln_a_g, ln_a_b=v_ln_a_b, w_s=v_w_s, b_s=v_b_s, conv_w=v_conv_w,
               conv_b=v_conv_b, dt_bias=v_dt_bias, a_log=v_a_log, d_skip=v_d_skip, ssm_norm_g=v_ssm_norm_g, w_oa=v_w_oa,
               w_ob=v_w_ob, w_out=v_w_out, ple_norm_g=v_ple_norm_g, w_pg=v_w_pg, w_ple=v_w_ple, final_g=v_final_g)
    xi, yi, ci = _place()
    me = 2 * xi + yi
    c_arr = jnp.reshape(ci, (1,)).astype(jnp.int32)
    place_arr = jnp.stack([me, ci]).astype(jnp.int32)

    shard = {n: wt[n][0] for n in _BIG}
    wire = {n: shard[n].astype(WIRE_DTYPE) for n in _BIG}
    w_in_blocks, conv_blocks = _gather_weights([wire["w_in"]], conv_w[0])
    g_ssem, g_rsem, g_sent, g_lands, g_token = _late_gather_start([wire[n] for n in _BIG[1:]])
    full = {"conv_w": _blocks_to_full(True, conv_blocks)}
    for n in _SMALL:
        if n != "conv_w":
            full[n] = wt[n][0] if wt[n].ndim > 2 else wt[n].reshape(1, wt[n].shape[-1])

    def late_weights(after):
        blocks = _late_gather_wait(g_ssem, g_rsem, g_sent, g_lands, after)
        return {n: _blocks_to_full(n in _COL_SHARDED, b) for n, b in zip(_BIG[1:], blocks)}

    w = _layout_weights(full, w_in_blocks=w_in_blocks)
    loss_row, g, ctx = _forward_backward(x[0], p[0, 0], loss_target[0], w, late_weights, after=(g_token,))

    parts = {n: _full_to_blocks(n in _COL_SHARDED, g[n]) for n in _BIG[1:]}
    parts["w_main"] = g["w_main"][None]
    parts["w_dt"] = jnp.pad(_lanes_to_heads(g["w_dt"]), ((0, 0), (0, 128 - N_HEADS)))[None]
    names = ("w_main", "w_dt") + _BIG[1:]
    g5 = {n: parts[n].reshape(parts[n].shape[0], 2, parts[n].shape[1] // 2, parts[n].shape[2]) for n in names}
    to_sibling = [lax.dynamic_index_in_dim(g5[n], 1 - ci, axis=1, keepdims=False).astype(WIRE_DTYPE) for n in names]
    from_sibling = _swap_with_sibling(to_sibling)
    chip = {n: _chip_sum("chip_sum_" + n, g5[n], r, c_arr) for n, r in zip(names, from_sibling)}
    chip["w_in"] = _w_in_grad_blocks(chip["w_main"][0], chip["w_dt"][0])
    chip_wire = [chip[n] for n in _BIG]
    s_sem, r_sem, sent, lands, token = _scatter_blocks_start(chip_wire)
    grad_x, g["norm_g"] = _input_grad(x[0], w, ctx, after=(token,))
    g = _natural_grads(g)

    pieces = [_two_d(n, g[n]).reshape(-1) for n in _SMALL] + [loss_row[0, :1]]
    sizes = [v.shape[0] for v in pieces]
    padded = [-(-s // _PACK) * _PACK for s in sizes]
    packed = jnp.concatenate([jnp.pad(v, (0, ps - s)) for v, s, ps in zip(pieces, sizes, padded)]).reshape(-1, 128)
    a_ssem, a_rsem, a_src, a_land, a_token = _small_gather_start(packed)

    sent, from_chips = _scatter_blocks_wait(s_sem, r_sem, sent, lands, (grad_x, a_token))
    halves = [_final_sum("final_sum_" + n, a, r, place_arr) for n, a, r in zip(_BIG, sent, from_chips)]
    grads = dict(zip(_BIG, _share_halves(halves)))
    delta, new_m, new_v = {}, {}, {}
    for n in _BIG:
        t = jnp.transpose if n == "w_in" else (lambda a: a)
        res = _adamw_call("adamw_" + n, t(shard[n]), t(grads[n]), t(mom[n][0]), t(vel[n][0]))
        delta[n], new_m[n], new_v[n], grads[n] = (t(r) for r in res)

    a_src, a_land = _small_gather_wait(a_ssem, a_rsem, a_src, a_land, delta["w_in"])
    summed = _small_sum(a_src, a_land, jnp.reshape(4 * xi + 2 * yi + ci, (1,)).astype(jnp.int32)).reshape(-1)
    off = 0
    for n, s, ps in zip(_SMALL, sizes, padded):
        grads[n] = summed[off:off + s].reshape(_two_d(n, g[n]).shape)
        off += ps
    loss = summed[off]
    grads["conv_w"] = lax.dynamic_slice_in_dim(grads["conv_w"], me * (CONV_DIM // N_CHIPS), CONV_DIM // N_CHIPS, axis=1)
    small = _adamw_small([_two_d(n, wt[n]) for n in _SMALL], [grads[n] for n in _SMALL],
                         [_two_d(n, mom[n]) for n in _SMALL], [_two_d(n, vel[n]) for n in _SMALL])
    for i, n in enumerate(_SMALL):
        delta[n], new_m[n], new_v[n] = small[0][i], small[1][i], small[2][i]

    def shaped(d):
        return [d[n].reshape(wt[n].shape) for n in _WEIGHTS]

    return (loss, grad_x[None], *shaped(grads), *shaped(delta), *shaped(new_m), *shaped(new_v))
```
